```python
import math
import jax, jax.numpy as jnp
from jax import lax
import numpy as np

D_MODEL = 1024
BATCH = 32
SEQ = 2048
DEPTH = 1

CHUNK = 64
MIX_WIDTH = D_MODEL
SSM_WIDTH = MIX_WIDTH // 2
POOL_WIDTH = MIX_WIDTH - SSM_WIDTH
SSM_GROUP = 16
SSM_GROUPS = SSM_WIDTH // SSM_GROUP
SSM_STATE = 64
POOL_WINDOWS = (2, 4, 8, 16)
POOL_GROUPS = len(POOL_WINDOWS)
POOL_GROUP_WIDTH = POOL_WIDTH // POOL_GROUPS
D_FF = ((8 * D_MODEL // 3 + 255) // 256) * 256
CONV_WIDTH = 3
N_MOD = 6
EPS = 1e-6
DT_MIN = 1e-3
DT_MAX = 1e-1

kernel_name = "hymba_s5_pool_convffn_block"


def rms_norm(x, g):
    xf = x.astype(jnp.float32)
    y = xf * lax.rsqrt(jnp.mean(xf * xf, axis=-1, keepdims=True) + EPS)
    return (y * g.astype(jnp.float32)).astype(x.dtype)


def modulate(h, shift, scale):
    return h * (1 + scale[:, None, :]) + shift[:, None, :]


def s5_mixer(u, lam_re, lam_im, log_dt, b_re, b_im, c_re, c_im, d_skip):
    f32 = jnp.float32
    bsz, seq, _ = u.shape
    n_chunks = seq // CHUNK
    lam = lax.complex(lam_re.astype(f32), lam_im.astype(f32))
    dt = jnp.exp(log_dt.astype(f32))[:, None]
    lam_bar = jnp.exp(lam * dt)
    b_mat = lax.complex(b_re.astype(f32), b_im.astype(f32))
    b_bar = ((lam_bar - 1) / lam)[..., None] * b_mat
    c_mat = lax.complex(c_re.astype(f32), c_im.astype(f32))
    d = d_skip.astype(f32).reshape(SSM_GROUPS, SSM_GROUP)
    u_chunks = u.astype(f32).reshape(bsz, n_chunks, CHUNK, SSM_GROUPS, SSM_GROUP)
    u_chunks = u_chunks.transpose(1, 0, 2, 3, 4)

    def combine(e1, e2):
        a1, b1 = e1
        a2, b2 = e2
        return a1 * a2, a2 * b1 + b2

    def chunk_step(state, u_c):
        bu = jnp.einsum('bcgh,gph->bcgp', u_c.astype(jnp.complex64), b_bar)
        bu = bu.at[:, 0].add(lam_bar * state)
        a = jnp.broadcast_to(lam_bar, bu.shape)
        _, xs = lax.associative_scan(combine, (a, bu), axis=1)
        y = jnp.einsum('bcgp,ghp->bcgh', xs, c_mat).real + d * u_c
        return xs[:, -1], y

    state0 = jnp.zeros((bsz, SSM_GROUPS, SSM_STATE), jnp.complex64)
    _, ys = lax.scan(chunk_step, state0, u_chunks)
    return ys.transpose(1, 0, 2, 3, 4).reshape(bsz, seq, SSM_WIDTH)


def multiscale_pool(p, w_pool, b_pool, pool_scale):
    f32 = jnp.float32
    bsz, seq, _ = p.shape
    pf = p.astype(f32)
    cs = jnp.concatenate([jnp.zeros((bsz, 1, POOL_WIDTH), f32), jnp.cumsum(pf, axis=1)], axis=1)
    pos = jnp.arange(1, seq + 1, dtype=f32)[None, :, None]
    pooled = []
    for gi, w in enumerate(POOL_WINDOWS):
        sl = slice(gi * POOL_GROUP_WIDTH, (gi + 1) * POOL_GROUP_WIDTH)
        cs_g = cs[:, :, sl]
        lagged = jnp.concatenate(
            [jnp.zeros((bsz, w - 1, POOL_GROUP_WIDTH), f32), cs_g[:, :seq + 1 - w]], axis=1)
        mean = (cs_g[:, 1:] - lagged) / jnp.minimum(pos, float(w))
        pooled.append(mean - pf[:, :, sl])
    z = jnp.stack(pooled, axis=2)
    z = jnp.einsum('blgc,gcd->blgd', z, w_pool) + b_pool
    return z.reshape(bsz, seq, POOL_WIDTH) * pool_scale


def causal_dwconv(h, w, b):
    ch = h.shape[-1]
    y = lax.conv_general_dilated(
        h, w[:, None, :].astype(h.dtype), window_strides=(1,),
        padding=[(CONV_WIDTH - 1, 0)], dimension_numbers=('NWC', 'WIO', 'NWC'),
        feature_group_count=ch)
    return y + b


def hybrid_layer(x, c, w_ada, b_ada, g_norm_mix, w_in, ssm_lam_re, ssm_lam_im, ssm_log_dt,
                 ssm_b_re, ssm_b_im, ssm_c_re, ssm_c_im, ssm_d, w_glu, b_glu, w_pool, b_pool,
                 pool_scale, w_out, g_norm_ffn, w_up, w_conv, b_conv, w_down):
    mod = jax.nn.silu(c) @ w_ada + b_ada
    sh1, sc1, gt1, sh2, sc2, gt2 = jnp.split(mod, N_MOD, axis=-1)

    h = modulate(rms_norm(x, g_norm_mix), sh1, sc1)
    z_in = h @ w_in
    u, p = z_in[..., :SSM_WIDTH], z_in[..., SSM_WIDTH:]
    y = jax.nn.gelu(s5_mixer(u, ssm_lam_re, ssm_lam_im, ssm_log_dt, ssm_b_re, ssm_b_im,
                             ssm_c_re, ssm_c_im, ssm_d))
    val, gate = jnp.split(y @ w_glu + b_glu, 2, axis=-1)
    y_ssm = val * jax.nn.sigmoid(gate)
    y_pool = multiscale_pool(p, w_pool, b_pool, pool_scale)
    mixed = jnp.concatenate([y_ssm, y_pool], axis=-1) @ w_out
    x = x + gt1[:, None, :] * mixed

    h = modulate(rms_norm(x, g_norm_ffn), sh2, sc2)
    v, g = jnp.split(h @ w_up, 2, axis=-1)
    g = causal_dwconv(g, w_conv, b_conv)
    x = x + gt2[:, None, :] * ((jax.nn.silu(g) * v) @ w_down)
    return x


def _fwd_setup_inputs(seed: int = 0) -> dict:
    key = jax.random.key(seed)
    ks = jax.random.split(key, 32)

    def nrm(k, shape, scale):
        return jax.random.normal(k, shape, jnp.float32) * scale

    L = DEPTH
    G, P, H = SSM_GROUPS, SSM_STATE, SSM_GROUP
    n_idx = jnp.arange(P, dtype=jnp.float32)
    return {
        "x": nrm(ks[0], (BATCH, SEQ, D_MODEL), 1.0),
        "c": nrm(ks[1], (BATCH, D_MODEL), 1.0),
        "w_ada": nrm(ks[2], (L, D_MODEL, N_MOD * D_MODEL), D_MODEL ** -0.5),
        "b_ada": nrm(ks[3], (L, N_MOD * D_MODEL), 0.02),
        "g_norm_mix": 1.0 + nrm(ks[4], (L, D_MODEL), 0.02),
        "w_in": nrm(ks[5], (L, D_MODEL, MIX_WIDTH), D_MODEL ** -0.5),
        "ssm_lam_re": -0.5 + nrm(ks[6], (L, G, P), 0.01),
        "ssm_lam_im": math.pi * n_idx + nrm(ks[7], (L, G, P), 0.01),
        "ssm_log_dt": jax.random.uniform(ks[8], (L, G), jnp.float32,
                                         math.log(DT_MIN), math.log(DT_MAX)),
        "ssm_b_re": nrm(ks[9], (L, G, P, H), (2 * H) ** -0.5),
        "ssm_b_im": nrm(ks[10], (L, G, P, H), (2 * H) ** -0.5),
        "ssm_c_re": nrm(ks[11], (L, G, H, P), (2 * P) ** -0.5 * 4.0),
        "ssm_c_im": nrm(ks[12], (L, G, H, P), (2 * P) ** -0.5 * 4.0),
        "ssm_d": nrm(ks[13], (L, SSM_WIDTH), 1.0),
        "w_glu": nrm(ks[14], (L, SSM_WIDTH, 2 * SSM_WIDTH), SSM_WIDTH ** -0.5),
        "b_glu": nrm(ks[15], (L, 2 * SSM_WIDTH), 0.02),
        "w_pool": nrm(ks[16], (L, POOL_GROUPS, POOL_GROUP_WIDTH, POOL_GROUP_WIDTH),
                      POOL_GROUP_WIDTH ** -0.5),
        "b_pool": nrm(ks[17], (L, POOL_GROUPS, POOL_GROUP_WIDTH), 0.02),
        "pool_scale": 1.0 + nrm(ks[18], (L, POOL_WIDTH), 0.1),
        "w_out": nrm(ks[19], (L, MIX_WIDTH, D_MODEL), MIX_WIDTH ** -0.5),
        "g_norm_ffn": 1.0 + nrm(ks[20], (L, D_MODEL), 0.02),
        "w_up": nrm(ks[21], (L, D_MODEL, 2 * D_FF), D_MODEL ** -0.5),
        "w_conv": nrm(ks[22], (L, CONV_WIDTH, D_FF), CONV_WIDTH ** -0.5),
        "b_conv": nrm(ks[23], (L, D_FF), 0.02),
        "w_down": nrm(ks[24], (L, D_FF, D_MODEL), D_FF ** -0.5),
        "g_norm_final": 1.0 + nrm(ks[25], (D_MODEL,), 0.02),
    }


def _fwd_reference(x, c, w_ada, b_ada, g_norm_mix, w_in, ssm_lam_re, ssm_lam_im, ssm_log_dt,
              ssm_b_re, ssm_b_im, ssm_c_re, ssm_c_im, ssm_d, w_glu, b_glu, w_pool, b_pool,
              pool_scale, w_out, g_norm_ffn, w_up, w_conv, b_conv, w_down, g_norm_final):
    for l in range(DEPTH):
        x = hybrid_layer(x, c, w_ada[l], b_ada[l], g_norm_mix[l], w_in[l], ssm_lam_re[l],
                         ssm_lam_im[l], ssm_log_dt[l], ssm_b_re[l], ssm_b_im[l], ssm_c_re[l],
                         ssm_c_im[l], ssm_d[l], w_glu[l], b_glu[l], w_pool[l], b_pool[l],
                         pool_scale[l], w_out[l], g_norm_ffn[l], w_up[l], w_conv[l],
                         b_conv[l], w_down[l])
    return rms_norm(x, g_norm_final)


import jax as _jax
import jax.numpy as _jnp

TWIN_FORMAT = 'train_step'
FWD_PARAMS = ['x', 'c', 'w_ada', 'b_ada', 'g_norm_mix', 'w_in', 'ssm_lam_re', 'ssm_lam_im', 'ssm_log_dt', 'ssm_b_re', 'ssm_b_im', 'ssm_c_re', 'ssm_c_im', 'ssm_d', 'w_glu', 'b_glu', 'w_pool', 'b_pool', 'pool_scale', 'w_out', 'g_norm_ffn', 'w_up', 'w_conv', 'b_conv', 'w_down', 'g_norm_final']
TWIN_WEIGHTS = ['w_ada', 'b_ada', 'g_norm_mix', 'w_in', 'ssm_lam_re', 'ssm_lam_im', 'ssm_log_dt', 'ssm_b_re', 'ssm_b_im', 'ssm_c_re', 'ssm_c_im', 'ssm_d', 'w_glu', 'b_glu', 'w_pool', 'b_pool', 'pool_scale', 'w_out', 'g_norm_ffn', 'w_up', 'w_conv', 'b_conv', 'w_down', 'g_norm_final']
TWIN_DIFF_INPUT = 'x'
TWIN_INPUTS = ['x', 'c', 'w_ada', 'b_ada', 'g_norm_mix', 'w_in', 'ssm_lam_re', 'ssm_lam_im', 'ssm_log_dt', 'ssm_b_re', 'ssm_b_im', 'ssm_c_re', 'ssm_c_im', 'ssm_d', 'w_glu', 'b_glu', 'w_pool', 'b_pool', 'pool_scale', 'w_out', 'g_norm_ffn', 'w_up', 'w_conv', 'b_conv', 'w_down', 'g_norm_final', 'loss_target', 'm_w_ada', 'm_b_ada', 'm_g_norm_mix', 'm_w_in', 'm_ssm_lam_re', 'm_ssm_lam_im', 'm_ssm_log_dt', 'm_ssm_b_re', 'm_ssm_b_im', 'm_ssm_c_re', 'm_ssm_c_im', 'm_ssm_d', 'm_w_glu', 'm_b_glu', 'm_w_pool', 'm_b_pool', 'm_pool_scale', 'm_w_out', 'm_g_norm_ffn', 'm_w_up', 'm_w_conv', 'm_b_conv', 'm_w_down', 'm_g_norm_final', 'v_w_ada', 'v_b_ada', 'v_g_norm_mix', 'v_w_in', 'v_ssm_lam_re', 'v_ssm_lam_im', 'v_ssm_log_dt', 'v_ssm_b_re', 'v_ssm_b_im', 'v_ssm_c_re', 'v_ssm_c_im', 'v_ssm_d', 'v_w_glu', 'v_b_glu', 'v_w_pool', 'v_b_pool', 'v_pool_scale', 'v_w_out', 'v_g_norm_ffn', 'v_w_up', 'v_w_conv', 'v_b_conv', 'v_w_down', 'v_g_norm_final']
TWIN_OUTPUTS = ['loss', 'grad_x', 'grad_w_ada', 'grad_b_ada', 'grad_g_norm_mix', 'grad_w_in', 'grad_ssm_lam_re', 'grad_ssm_lam_im', 'grad_ssm_log_dt', 'grad_ssm_b_re', 'grad_ssm_b_im', 'grad_ssm_c_re', 'grad_ssm_c_im', 'grad_ssm_d', 'grad_w_glu', 'grad_b_glu', 'grad_w_pool', 'grad_b_pool', 'grad_pool_scale', 'grad_w_out', 'grad_g_norm_ffn', 'grad_w_up', 'grad_w_conv', 'grad_b_conv', 'grad_w_down', 'grad_g_norm_final', 'delta_w_ada', 'delta_b_ada', 'delta_g_norm_mix', 'delta_w_in', 'delta_ssm_lam_re', 'delta_ssm_lam_im', 'delta_ssm_log_dt', 'delta_ssm_b_re', 'delta_ssm_b_im', 'delta_ssm_c_re', 'delta_ssm_c_im', 'delta_ssm_d', 'delta_w_glu', 'delta_b_glu', 'delta_w_pool', 'delta_b_pool', 'delta_pool_scale', 'delta_w_out', 'delta_g_norm_ffn', 'delta_w_up', 'delta_w_conv', 'delta_b_conv', 'delta_w_down', 'delta_g_norm_final', 'new_m_w_ada', 'new_m_b_ada', 'new_m_g_norm_mix', 'new_m_w_in', 'new_m_ssm_lam_re', 'new_m_ssm_lam_im', 'new_m_ssm_log_dt', 'new_m_ssm_b_re', 'new_m_ssm_b_im', 'new_m_ssm_c_re', 'new_m_ssm_c_im', 'new_m_ssm_d', 'new_m_w_glu', 'new_m_b_glu', 'new_m_w_pool', 'new_m_b_pool', 'new_m_pool_scale', 'new_m_w_out', 'new_m_g_norm_ffn', 'new_m_w_up', 'new_m_w_conv', 'new_m_b_conv', 'new_m_w_down', 'new_m_g_norm_final', 'new_v_w_ada', 'new_v_b_ada', 'new_v_g_norm_mix', 'new_v_w_in', 'new_v_ssm_lam_re', 'new_v_ssm_lam_im', 'new_v_ssm_log_dt', 'new_v_ssm_b_re', 'new_v_ssm_b_im', 'new_v_ssm_c_re', 'new_v_ssm_c_im', 'new_v_ssm_d', 'new_v_w_glu', 'new_v_b_glu', 'new_v_w_pool', 'new_v_b_pool', 'new_v_pool_scale', 'new_v_w_out', 'new_v_g_norm_ffn', 'new_v_w_up', 'new_v_w_conv', 'new_v_b_conv', 'new_v_w_down', 'new_v_g_norm_final']
TWIN_LEAF_KINDS = {'loss': 'loss', 'grad_x': 'grad_x', 'grad_w_ada': 'grad_w', 'grad_b_ada': 'grad_w', 'grad_g_norm_mix': 'grad_w', 'grad_w_in': 'grad_w', 'grad_ssm_lam_re': 'grad_w', 'grad_ssm_lam_im': 'grad_w', 'grad_ssm_log_dt': 'grad_w', 'grad_ssm_b_re': 'grad_w', 'grad_ssm_b_im': 'grad_w', 'grad_ssm_c_re': 'grad_w', 'grad_ssm_c_im': 'grad_w', 'grad_ssm_d': 'grad_w', 'grad_w_glu': 'grad_w', 'grad_b_glu': 'grad_w', 'grad_w_pool': 'grad_w', 'grad_b_pool': 'grad_w', 'grad_pool_scale': 'grad_w', 'grad_w_out': 'grad_w', 'grad_g_norm_ffn': 'grad_w', 'grad_w_up': 'grad_w', 'grad_w_conv': 'grad_w', 'grad_b_conv': 'grad_w', 'grad_w_down': 'grad_w', 'grad_g_norm_final': 'grad_w', 'delta_w_ada': 'delta_w', 'delta_b_ada': 'delta_w', 'delta_g_norm_mix': 'delta_w', 'delta_w_in': 'delta_w', 'delta_ssm_lam_re': 'delta_w', 'delta_ssm_lam_im': 'delta_w', 'delta_ssm_log_dt': 'delta_w', 'delta_ssm_b_re': 'delta_w', 'delta_ssm_b_im': 'delta_w', 'delta_ssm_c_re': 'delta_w', 'delta_ssm_c_im': 'delta_w', 'delta_ssm_d': 'delta_w', 'delta_w_glu': 'delta_w', 'delta_b_glu': 'delta_w', 'delta_w_pool': 'delta_w', 'delta_b_pool': 'delta_w', 'delta_pool_scale': 'delta_w', 'delta_w_out': 'delta_w', 'delta_g_norm_ffn': 'delta_w', 'delta_w_up': 'delta_w', 'delta_w_conv': 'delta_w', 'delta_b_conv': 'delta_w', 'delta_w_down': 'delta_w', 'delta_g_norm_final': 'delta_w', 'new_m_w_ada': 'new_m', 'new_m_b_ada': 'new_m', 'new_m_g_norm_mix': 'new_m', 'new_m_w_in': 'new_m', 'new_m_ssm_lam_re': 'new_m', 'new_m_ssm_lam_im': 'new_m', 'new_m_ssm_log_dt': 'new_m', 'new_m_ssm_b_re': 'new_m', 'new_m_ssm_b_im': 'new_m', 'new_m_ssm_c_re': 'new_m', 'new_m_ssm_c_im': 'new_m', 'new_m_ssm_d': 'new_m', 'new_m_w_glu': 'new_m', 'new_m_b_glu': 'new_m', 'new_m_w_pool': 'new_m', 'new_m_b_pool': 'new_m', 'new_m_pool_scale': 'new_m', 'new_m_w_out': 'new_m', 'new_m_g_norm_ffn': 'new_m', 'new_m_w_up': 'new_m', 'new_m_w_conv': 'new_m', 'new_m_b_conv': 'new_m', 'new_m_w_down': 'new_m', 'new_m_g_norm_final': 'new_m', 'new_v_w_ada': 'new_v', 'new_v_b_ada': 'new_v', 'new_v_g_norm_mix': 'new_v', 'new_v_w_in': 'new_v', 'new_v_ssm_lam_re': 'new_v', 'new_v_ssm_lam_im': 'new_v', 'new_v_ssm_log_dt': 'new_v', 'new_v_ssm_b_re': 'new_v', 'new_v_ssm_b_im': 'new_v', 'new_v_ssm_c_re': 'new_v', 'new_v_ssm_c_im': 'new_v', 'new_v_ssm_d': 'new_v', 'new_v_w_glu': 'new_v', 'new_v_b_glu': 'new_v', 'new_v_w_pool': 'new_v', 'new_v_b_pool': 'new_v', 'new_v_pool_scale': 'new_v', 'new_v_w_out': 'new_v', 'new_v_g_norm_ffn': 'new_v', 'new_v_w_up': 'new_v', 'new_v_w_conv': 'new_v', 'new_v_b_conv': 'new_v', 'new_v_w_down': 'new_v', 'new_v_g_norm_final': 'new_v'}


def _forward(args):
    return _fwd_reference(*[args[k] for k in FWD_PARAMS])


def _output_shape():
    out = _jax.eval_shape(lambda: _forward(_fwd_setup_inputs(0)))
    return out.shape, out.dtype

N_MICROBATCH = 1
ADAM_LR = 0.001
ADAM_B1 = 0.9
ADAM_B2 = 0.999
ADAM_EPS = 1e-08
ADAM_WD = 0.01
ADAM_STEP = 10
PER_EXAMPLE_BATCH_AXIS = {'x': 0, 'c': 0, 'loss_target': 0}
SHARED_INPUTS = []
_WEIGHT_DTYPES = {'w_ada': _jnp.float32, 'b_ada': _jnp.float32, 'g_norm_mix': _jnp.float32, 'w_in': _jnp.float32, 'ssm_lam_re': _jnp.float32, 'ssm_lam_im': _jnp.float32, 'ssm_log_dt': _jnp.float32, 'ssm_b_re': _jnp.float32, 'ssm_b_im': _jnp.float32, 'ssm_c_re': _jnp.float32, 'ssm_c_im': _jnp.float32, 'ssm_d': _jnp.float32, 'w_glu': _jnp.float32, 'b_glu': _jnp.float32, 'w_pool': _jnp.float32, 'b_pool': _jnp.float32, 'pool_scale': _jnp.float32, 'w_out': _jnp.float32, 'g_norm_ffn': _jnp.float32, 'w_up': _jnp.float32, 'w_conv': _jnp.float32, 'b_conv': _jnp.float32, 'w_down': _jnp.float32, 'g_norm_final': _jnp.float32}
MOMENT_SCALE = {'w_ada': 1.278976e-01, 'b_ada': 2.319012e-01, 'g_norm_mix': 1.262373e-01, 'w_in': 1.257714e-01, 'ssm_lam_re': 8.419343e-02, 'ssm_lam_im': 3.612361e-02, 'ssm_log_dt': 2.187612e+01, 'ssm_b_re': 2.617860e-02, 'ssm_b_im': 2.169996e-02, 'ssm_c_re': 1.159958e-02, 'ssm_c_im': 1.364977e-02, 'ssm_d': 9.117533e-02, 'w_glu': 6.247032e-02, 'b_glu': 6.266451e-02, 'w_pool': 1.559125e-01, 'b_pool': 1.556505e-01, 'pool_scale': 1.558916e-01, 'w_out': 1.246198e-01, 'g_norm_ffn': 1.716721e-01, 'w_up': 8.176037e-02, 'w_conv': 8.615414e-02, 'b_conv': 6.424466e-02, 'w_down': 1.327035e-01, 'g_norm_final': 6.584275e+01}


def _to_microbatches(a, axis):
    t = _jnp.moveaxis(a, axis, 0)
    t = t.reshape((N_MICROBATCH, t.shape[0] // N_MICROBATCH) + t.shape[1:])
    return _jnp.moveaxis(t, 1, axis + 1)


def setup_inputs(seed: int = 0) -> dict:
    inp = _fwd_setup_inputs(seed)
    key = _jax.random.fold_in(_jax.random.key(seed), 7919)
    shape, _ = _output_shape()
    out = dict(inp)
    out["loss_target"] = _jax.random.normal(_jax.random.fold_in(key, 0), shape, _jnp.float32)
    for i, name in enumerate(TWIN_WEIGHTS):
        w = inp[name].astype(_jnp.float32)
        if MOMENT_SCALE is None:
            s = _jnp.sqrt(_jnp.mean(_jnp.square(w)) + 1e-30)
        else:
            s = MOMENT_SCALE[name]
        km, kv = _jax.random.split(_jax.random.fold_in(key, i + 1))
        out[name] = w
        out["m_" + name] = s * _jax.random.normal(km, w.shape, _jnp.float32)
        out["v_" + name] = (s * s) * _jax.random.uniform(kv, w.shape, _jnp.float32, 0.5, 1.5)
    if N_MICROBATCH > 1:
        for name, axis in PER_EXAMPLE_BATCH_AXIS.items():
            out[name] = _to_microbatches(out[name], axis)
    return {'x': out['x'], 'c': out['c'], 'w_ada': out['w_ada'], 'b_ada': out['b_ada'], 'g_norm_mix': out['g_norm_mix'], 'w_in': out['w_in'], 'ssm_lam_re': out['ssm_lam_re'], 'ssm_lam_im': out['ssm_lam_im'], 'ssm_log_dt': out['ssm_log_dt'], 'ssm_b_re': out['ssm_b_re'], 'ssm_b_im': out['ssm_b_im'], 'ssm_c_re': out['ssm_c_re'], 'ssm_c_im': out['ssm_c_im'], 'ssm_d': out['ssm_d'], 'w_glu': out['w_glu'], 'b_glu': out['b_glu'], 'w_pool': out['w_pool'], 'b_pool': out['b_pool'], 'pool_scale': out['pool_scale'], 'w_out': out['w_out'], 'g_norm_ffn': out['g_norm_ffn'], 'w_up': out['w_up'], 'w_conv': out['w_conv'], 'b_conv': out['b_conv'], 'w_down': out['w_down'], 'g_norm_final': out['g_norm_final'], 'loss_target': out['loss_target'], 'm_w_ada': out['m_w_ada'], 'm_b_ada': out['m_b_ada'], 'm_g_norm_mix': out['m_g_norm_mix'], 'm_w_in': out['m_w_in'], 'm_ssm_lam_re': out['m_ssm_lam_re'], 'm_ssm_lam_im': out['m_ssm_lam_im'], 'm_ssm_log_dt': out['m_ssm_log_dt'], 'm_ssm_b_re': out['m_ssm_b_re'], 'm_ssm_b_im': out['m_ssm_b_im'], 'm_ssm_c_re': out['m_ssm_c_re'], 'm_ssm_c_im': out['m_ssm_c_im'], 'm_ssm_d': out['m_ssm_d'], 'm_w_glu': out['m_w_glu'], 'm_b_glu': out['m_b_glu'], 'm_w_pool': out['m_w_pool'], 'm_b_pool': out['m_b_pool'], 'm_pool_scale': out['m_pool_scale'], 'm_w_out': out['m_w_out'], 'm_g_norm_ffn': out['m_g_norm_ffn'], 'm_w_up': out['m_w_up'], 'm_w_conv': out['m_w_conv'], 'm_b_conv': out['m_b_conv'], 'm_w_down': out['m_w_down'], 'm_g_norm_final': out['m_g_norm_final'], 'v_w_ada': out['v_w_ada'], 'v_b_ada': out['v_b_ada'], 'v_g_norm_mix': out['v_g_norm_mix'], 'v_w_in': out['v_w_in'], 'v_ssm_lam_re': out['v_ssm_lam_re'], 'v_ssm_lam_im': out['v_ssm_lam_im'], 'v_ssm_log_dt': out['v_ssm_log_dt'], 'v_ssm_b_re': out['v_ssm_b_re'], 'v_ssm_b_im': out['v_ssm_b_im'], 'v_ssm_c_re': out['v_ssm_c_re'], 'v_ssm_c_im': out['v_ssm_c_im'], 'v_ssm_d': out['v_ssm_d'], 'v_w_glu': out['v_w_glu'], 'v_b_glu': out['v_b_glu'], 'v_w_pool': out['v_w_pool'], 'v_b_pool': out['v_b_pool'], 'v_pool_scale': out['v_pool_scale'], 'v_w_out': out['v_w_out'], 'v_g_norm_ffn': out['v_g_norm_ffn'], 'v_w_up': out['v_w_up'], 'v_w_conv': out['v_w_conv'], 'v_b_conv': out['v_b_conv'], 'v_w_down': out['v_w_down'], 'v_g_norm_final': out['v_g_norm_final']}


def _loss(weights, diff, rest, loss_target):
    with _jax.named_scope("forward"):
        args = {**rest, TWIN_DIFF_INPUT: diff, **{k: w.astype(_WEIGHT_DTYPES[k]) for k, w in weights.items()}}
        y = _forward(args)
    with _jax.named_scope("loss_head"):
        err = _jnp.square(y.astype(_jnp.float32) - loss_target)
        return 0.5 * _jnp.sum(_jnp.mean(err, axis=-1)) if err.ndim else 0.5 * err


def _adamw(w, g, m, v):
    m = ADAM_B1 * m + (1.0 - ADAM_B1) * g
    v = ADAM_B2 * v + (1.0 - ADAM_B2) * _jnp.square(g)
    m_hat = m / (1.0 - ADAM_B1 ** ADAM_STEP)
    v_hat = v / (1.0 - ADAM_B2 ** ADAM_STEP)
    delta = -ADAM_LR * (m_hat / (_jnp.sqrt(v_hat) + ADAM_EPS) + ADAM_WD * w)
    return delta, m, v


def reference(x, c, w_ada, b_ada, g_norm_mix, w_in, ssm_lam_re, ssm_lam_im, ssm_log_dt, ssm_b_re, ssm_b_im, ssm_c_re, ssm_c_im, ssm_d, w_glu, b_glu, w_pool, b_pool, pool_scale, w_out, g_norm_ffn, w_up, w_conv, b_conv, w_down, g_norm_final, loss_target, m_w_ada, m_b_ada, m_g_norm_mix, m_w_in, m_ssm_lam_re, m_ssm_lam_im, m_ssm_log_dt, m_ssm_b_re, m_ssm_b_im, m_ssm_c_re, m_ssm_c_im, m_ssm_d, m_w_glu, m_b_glu, m_w_pool, m_b_pool, m_pool_scale, m_w_out, m_g_norm_ffn, m_w_up, m_w_conv, m_b_conv, m_w_down, m_g_norm_final, v_w_ada, v_b_ada, v_g_norm_mix, v_w_in, v_ssm_lam_re, v_ssm_lam_im, v_ssm_log_dt, v_ssm_b_re, v_ssm_b_im, v_ssm_c_re, v_ssm_c_im, v_ssm_d, v_w_glu, v_b_glu, v_w_pool, v_b_pool, v_pool_scale, v_w_out, v_g_norm_ffn, v_w_up, v_w_conv, v_b_conv, v_w_down, v_g_norm_final):
    given = dict(x=x, c=c, w_ada=w_ada, b_ada=b_ada, g_norm_mix=g_norm_mix, w_in=w_in, ssm_lam_re=ssm_lam_re, ssm_lam_im=ssm_lam_im, ssm_log_dt=ssm_log_dt, ssm_b_re=ssm_b_re, ssm_b_im=ssm_b_im, ssm_c_re=ssm_c_re, ssm_c_im=ssm_c_im, ssm_d=ssm_d, w_glu=w_glu, b_glu=b_glu, w_pool=w_pool, b_pool=b_pool, pool_scale=pool_scale, w_out=w_out, g_norm_ffn=g_norm_ffn, w_up=w_up, w_conv=w_conv, b_conv=b_conv, w_down=w_down, g_norm_final=g_norm_final, loss_target=loss_target, m_w_ada=m_w_ada, m_b_ada=m_b_ada, m_g_norm_mix=m_g_norm_mix, m_w_in=m_w_in, m_ssm_lam_re=m_ssm_lam_re, m_ssm_lam_im=m_ssm_lam_im, m_ssm_log_dt=m_ssm_log_dt, m_ssm_b_re=m_ssm_b_re, m_ssm_b_im=m_ssm_b_im, m_ssm_c_re=m_ssm_c_re, m_ssm_c_im=m_ssm_c_im, m_ssm_d=m_ssm_d, m_w_glu=m_w_glu, m_b_glu=m_b_glu, m_w_pool=m_w_pool, m_b_pool=m_b_pool, m_pool_scale=m_pool_scale, m_w_out=m_w_out, m_g_norm_ffn=m_g_norm_ffn, m_w_up=m_w_up, m_w_conv=m_w_conv, m_b_conv=m_b_conv, m_w_down=m_w_down, m_g_norm_final=m_g_norm_final, v_w_ada=v_w_ada, v_b_ada=v_b_ada, v_g_norm_mix=v_g_norm_mix, v_w_in=v_w_in, v_ssm_lam_re=v_ssm_lam_re, v_ssm_lam_im=v_ssm_lam_im, v_ssm_log_dt=v_ssm_log_dt, v_ssm_b_re=v_ssm_b_re, v_ssm_b_im=v_ssm_b_im, v_ssm_c_re=v_ssm_c_re, v_ssm_c_im=v_ssm_c_im, v_ssm_d=v_ssm_d, v_w_glu=v_w_glu, v_b_glu=v_b_glu, v_w_pool=v_w_pool, v_b_pool=v_b_pool, v_pool_scale=v_pool_scale, v_w_out=v_w_out, v_g_norm_ffn=v_g_norm_ffn, v_w_up=v_w_up, v_w_conv=v_w_conv, v_b_conv=v_b_conv, v_w_down=v_w_down, v_g_norm_final=v_g_norm_final)
    weights = {n: given[n] for n in TWIN_WEIGHTS}
    shared = {n: given[n] for n in SHARED_INPUTS}
    per_example = {n: given[n] for n in ['x', 'c']}
    grad_fn = _jax.value_and_grad(_loss, argnums=(0, 1))

    def one_microbatch(ex, loss_target):
        ex = dict(ex)
        diff = ex.pop(TWIN_DIFF_INPUT)
        return grad_fn(weights, diff, {**shared, **ex}, loss_target)

    if N_MICROBATCH == 1:
        loss, (grad_w, grad_x) = one_microbatch(per_example, given["loss_target"])
    else:
        def body(carry, xs):
            loss_sum, grad_sum = carry
            l_k, (gw_k, gx_k) = one_microbatch(xs[0], xs[1])
            with _jax.named_scope("update"):
                return (loss_sum + l_k, _jax.tree.map(_jnp.add, grad_sum, gw_k)), gx_k

        init = (_jnp.zeros((), _jnp.float32), _jax.tree.map(_jnp.zeros_like, weights))
        (loss, grad_w), grad_x = _jax.lax.scan(body, init, (per_example, given["loss_target"]))
    with _jax.named_scope("update"):
        delta_w, new_m, new_v = {}, {}, {}
        for n in TWIN_WEIGHTS:
            delta_w[n], new_m[n], new_v[n] = _adamw(weights[n], grad_w[n], given["m_" + n], given["v_" + n])
    return (loss, grad_x, *[grad_w[n] for n in TWIN_WEIGHTS], *[delta_w[n] for n in TWIN_WEIGHTS],
            *[new_m[n] for n in TWIN_WEIGHTS], *[new_v[n] for n in TWIN_WEIGHTS])
```

```python
import functools
import math

import jax
import jax.numpy as jnp
from jax import lax
from jax.experimental import pallas as pl
from jax.experimental.pallas import tpu as pltpu

F32 = jnp.float32
BF16 = jnp.bfloat16

D = 1024
SSM_W = 512
POOL_W = 512
GRP = 32
GCH = 16
NST = 64
HALF_ST = GRP * NST // 2
HALF_CH = SSM_W // 2
DFF = 2816
FF_CH = 1408
N_MOD = 6
N_DEV = 8
EPS = 1e-6
POOL_WINDOWS = (2, 4, 8, 16)
POOL_HALO = 16
CONV_HALO = 8
GELU_C = math.sqrt(2.0 / math.pi)
GELU_A = 0.044715

ADAM_LR = 0.001
ADAM_B1 = 0.9
ADAM_B2 = 0.999
ADAM_EPS = 1e-08
ADAM_WD = 0.01
ADAM_STEP = 10

VMEM_LIMIT = 56 * 1024 * 1024
TT_MIX = 512
TT_FFN = 256
T_SSM = 128
MESH = pl.DeviceIdType.MESH
NT = (((1,), (1,)), ((), ()))
TN = (((0,), (0,)), ((), ()))
ANY = pl.BlockSpec(memory_space=pl.ANY)
VMEM = pl.BlockSpec(memory_space=pltpu.VMEM)


def _params(n_grid, vmem=VMEM_LIMIT):
    return pltpu.CompilerParams(dimension_semantics=("arbitrary",) * n_grid, vmem_limit_bytes=vmem)


def _dot(a, b):
    return jnp.dot(a, b, preferred_element_type=F32)


def _dot_nt(a, b):
    return lax.dot_general(a, b, NT, preferred_element_type=F32)


def _dot_tn(a, b):
    return lax.dot_general(a, b, TN, preferred_element_type=F32)


def _colsum(a):
    return jnp.sum(a, axis=0, keepdims=True)


def _rms(x):
    rstd = lax.rsqrt(jnp.mean(x * x, axis=-1, keepdims=True) + EPS)
    return x * rstd, rstd


def _rms_bwd(dxhat, xhat, rstd):
    return rstd * (dxhat - xhat * jnp.mean(dxhat * xhat, axis=-1, keepdims=True))


def _gelu(x):
    return 0.5 * x * (1.0 + jnp.tanh(GELU_C * (x + GELU_A * x * x * x)))


def _gelu_grad(x):
    x2 = x * x
    th = jnp.tanh(GELU_C * (x + GELU_A * x * x2))
    return 0.5 * (1.0 + th) + 0.5 * x * (1.0 - th * th) * GELU_C * (1.0 + 3.0 * GELU_A * x2)


def _adamw(w, g, m, v):
    m = ADAM_B1 * m + (1.0 - ADAM_B1) * g
    v = ADAM_B2 * v + (1.0 - ADAM_B2) * (g * g)
    m_hat = m / (1.0 - ADAM_B1 ** ADAM_STEP)
    v_hat = v / (1.0 - ADAM_B2 ** ADAM_STEP)
    delta = -ADAM_LR * (m_hat / (jnp.sqrt(v_hat) + ADAM_EPS) + ADAM_WD * w)
    return delta, m, v


def _my_place():
    return lax.axis_index("x"), lax.axis_index("y"), lax.axis_index("c")


def _allgather8(xb, name):
    def body(x_ref, out_ref, send_sems, recv_sems, local_sem):
        x, y, c = _my_place()
        me, sibling = (x, y, c), (x, y, 1 - c)
        chips = [(1 - x, y), (x, 1 - y), (1 - x, 1 - y)]

        def slot(px, py, pc):
            return out_ref.at[4 * px + 2 * py + pc]

        def copy(k, block, to, src=None):
            return pltpu.make_async_remote_copy(
                src_ref=slot(*block) if src is None else src, dst_ref=slot(*block),
                send_sem=send_sems.at[k], recv_sem=recv_sems.at[k], device_id=to, device_id_type=MESH)

        mine = pltpu.make_async_copy(x_ref, slot(*me), local_sem)
        mine.start()
        first = [copy(0, me, sibling, src=x_ref)]
        first += [copy(1 + j, me, (*chip, c), src=x_ref) for j, chip in enumerate(chips)]
        for cp in first:
            cp.start()
        passed = [copy(4 + j, (*chip, c), sibling) for j, chip in enumerate(chips)]
        for j, chip in enumerate(chips):
            copy(1 + j, (*chip, c), me).wait_recv()
            passed[j].start()
        copy(0, sibling, me).wait_recv()
        for j, chip in enumerate(chips):
            copy(4 + j, (*chip, 1 - c), me).wait_recv()
        for cp in first + passed:
            cp.wait_send()
        mine.wait()

    return pl.pallas_call(
        body, name=name, out_shape=jax.ShapeDtypeStruct((N_DEV,) + xb.shape, xb.dtype),
        in_specs=[ANY], out_specs=ANY,
        scratch_shapes=[pltpu.SemaphoreType.DMA((7,)), pltpu.SemaphoreType.DMA((7,)), pltpu.SemaphoreType.DMA(())],
    )(xb)


def _pair_exchange(g42, name):
    _, _, r, cdim = g42.shape

    def body(g_ref, out_ref, send_sem, recv_sem):
        x, y, c = _my_place()
        cp = pltpu.make_async_remote_copy(
            src_ref=g_ref.at[:, 1 - c], dst_ref=out_ref, send_sem=send_sem, recv_sem=recv_sem,
            device_id=(x, y, 1 - c), device_id_type=MESH)
        cp.start()
        cp.wait()

    return pl.pallas_call(
        body, name=name, out_shape=jax.ShapeDtypeStruct((4, r, cdim), g42.dtype),
        in_specs=[ANY], out_specs=ANY,
        scratch_shapes=[pltpu.SemaphoreType.DMA(()), pltpu.SemaphoreType.DMA(())],
    )(g42)


def _chip_exchange(s4, name):
    _, r, cdim = s4.shape

    def body(s_ref, out_ref, send_sems, recv_sems):
        x, y, c = _my_place()
        copies = []
        for d in (1, 2, 3):
            px, py = x ^ (d >> 1), y ^ (d & 1)
            copies.append(pltpu.make_async_remote_copy(
                src_ref=s_ref.at[2 * px + py], dst_ref=out_ref.at[d - 1],
                send_sem=send_sems.at[d - 1], recv_sem=recv_sems.at[d - 1],
                device_id=(px, py, c), device_id_type=MESH))
        for cp in copies:
            cp.start()
        for cp in copies:
            cp.wait()

    return pl.pallas_call(
        body, name=name, out_shape=jax.ShapeDtypeStruct((3, r, cdim), s4.dtype),
        in_specs=[ANY], out_specs=ANY,
        scratch_shapes=[pltpu.SemaphoreType.DMA((3,)), pltpu.SemaphoreType.DMA((3,))],
    )(s4)


def _row_tile(r):
    for t in (128, 64, 32, 16, 8):
        if r % t == 0:
            return t
    return r


def _pair_sum(g42, recv, cbit):
    _, _, r, cdim = g42.shape
    tr = _row_tile(r)

    def body(cb_ref, g_ref, r_ref, o_ref):
        o_ref[...] = g_ref[:, 0] + r_ref[...]

    return pl.pallas_call(
        body, name="grad_pair_sum",
        grid_spec=pltpu.PrefetchScalarGridSpec(
            num_scalar_prefetch=1, grid=(r // tr,),
            in_specs=[pl.BlockSpec((4, 1, tr, cdim), lambda i, cb: (0, cb[0], i, 0)),
                      pl.BlockSpec((4, tr, cdim), lambda i, cb: (0, i, 0))],
            out_specs=pl.BlockSpec((4, tr, cdim), lambda i, cb: (0, i, 0))),
        out_shape=jax.ShapeDtypeStruct((4, r, cdim), F32),
        compiler_params=_params(1),
    )(cbit, g42, recv)


def _final_sum_adamw(s4, recv3, qidx, w, m, v):
    r, cdim = w.shape
    tr = _row_tile(r)

    def body(q_ref, s_ref, r_ref, w_ref, m_ref, v_ref, g_out, d_out, m_out, v_out):
        g = s_ref[0] + r_ref[0] + r_ref[1] + r_ref[2]
        d, mn, vn = _adamw(w_ref[...], g, m_ref[...], v_ref[...])
        g_out[...] = g
        d_out[...] = d
        m_out[...] = mn
        v_out[...] = vn

    blk = pl.BlockSpec((tr, cdim), lambda i, q: (i, 0))
    shp = jax.ShapeDtypeStruct((r, cdim), F32)
    return pl.pallas_call(
        body, name="grad_final_sum_adamw",
        grid_spec=pltpu.PrefetchScalarGridSpec(
            num_scalar_prefetch=1, grid=(r // tr,),
            in_specs=[pl.BlockSpec((1, tr, cdim), lambda i, q: (q[0], i, 0)),
                      pl.BlockSpec((3, tr, cdim), lambda i, q: (0, i, 0)), blk, blk, blk],
            out_specs=[blk, blk, blk, blk]),
        out_shape=[shp, shp, shp, shp],
        compiler_params=_params(1),
    )(qidx, s4, recv3, w, m, v)


def _sum8_adamw(parts, w, m, v):
    r, cdim = w.shape
    tr = _row_tile(r)

    def body(p_ref, w_ref, m_ref, v_ref, g_out, d_out, m_out, v_out):
        g = p_ref[0]
        for k in range(1, N_DEV):
            g = g + p_ref[k]
        d, mn, vn = _adamw(w_ref[...], g, m_ref[...], v_ref[...])
        g_out[...] = g
        d_out[...] = d
        m_out[...] = mn
        v_out[...] = vn

    blk = pl.BlockSpec((tr, cdim), lambda i: (i, 0))
    shp = jax.ShapeDtypeStruct((r, cdim), F32)
    return pl.pallas_call(
        body, name="small_sum_adamw", grid=(r // tr,),
        in_specs=[pl.BlockSpec((N_DEV, tr, cdim), lambda i: (0, i, 0)), blk, blk, blk],
        out_specs=[blk, blk, blk, blk], out_shape=[shp, shp, shp, shp],
        compiler_params=_params(1),
    )(parts, w, m, v)


def _adamw_plain(g, w, m, v):
    r, cdim = w.shape
    tr = _row_tile(r)

    def body(g_ref, w_ref, m_ref, v_ref, d_out, m_out, v_out):
        d, mn, vn = _adamw(w_ref[...], g_ref[...], m_ref[...], v_ref[...])
        d_out[...] = d
        m_out[...] = mn
        v_out[...] = vn

    blk = pl.BlockSpec((tr, cdim), lambda i: (i, 0))
    shp = jax.ShapeDtypeStruct((r, cdim), F32)
    return pl.pallas_call(
        body, name="adamw_plain", grid=(r // tr,), in_specs=[blk, blk, blk, blk],
        out_specs=[blk, blk, blk], out_shape=[shp, shp, shp], compiler_params=_params(1),
    )(g, w, m, v)


def _ada_part(c_all, w_ada):
    def body(c_ref, w_ref, o_ref):
        cv = c_ref[...]
        o_ref[...] = _dot(cv * jax.nn.sigmoid(cv), w_ref[...])

    return pl.pallas_call(
        body, name="ada_part", in_specs=[VMEM, VMEM], out_specs=VMEM,
        out_shape=jax.ShapeDtypeStruct((c_all.shape[0], w_ada.shape[1]), F32),
        compiler_params=_params(0),
    )(c_all, w_ada)


def _ada_bwd(c_all, dmod_cols, dmod_blocks, w_ada, m_w, v_w, b_blocks, m_b, v_b):
    def body(c_ref, dc_ref, db_ref, w_ref, mw_ref, vw_ref, b_ref, mb_ref, vb_ref,
             gw_o, dw_o, mw_o, vw_o, gb_o, dbb_o, mb_o, vb_o):
        cv = c_ref[...]
        gw = _dot_tn(cv * jax.nn.sigmoid(cv), dc_ref[...])
        d, mn, vn = _adamw(w_ref[...], gw, mw_ref[...], vw_ref[...])
        gw_o[...] = gw
        dw_o[...] = d
        mw_o[...] = mn
        vw_o[...] = vn
        for k in range(N_DEV):
            gb = _colsum(db_ref[k])
            d, mn, vn = _adamw(b_ref[k], gb, mb_ref[k], vb_ref[k])
            gb_o[k] = gb
            dbb_o[k] = d
            mb_o[k] = mn
            vb_o[k] = vn

    ws = jax.ShapeDtypeStruct(w_ada.shape, F32)
    bs = jax.ShapeDtypeStruct(b_blocks.shape, F32)
    return pl.pallas_call(
        body, name="ada_bwd", in_specs=[VMEM] * 9, out_specs=[VMEM] * 8,
        out_shape=[ws, ws, ws, ws, bs, bs, bs, bs], compiler_params=_params(0),
    )(c_all, dmod_cols, dmod_blocks, w_ada, m_w, v_w, b_blocks, m_b, v_b)


def _ssm_param_fn(lr, li, ldt, br, bi):
    dt = jnp.exp(ldt)
    mag = jnp.exp(lr * dt)
    ang = li * dt
    lbr = mag * jnp.cos(ang)
    lbi = mag * jnp.sin(ang)
    nr = lbr - 1.0
    den = lr * lr + li * li
    cr = (nr * lr + lbi * li) / den
    ci = (lbi * lr - nr * li) / den
    return lbr, lbi, cr * br - ci * bi, cr * bi + ci * br


def _ssm_prep(lr, li, ldt, br, bi):
    def body(lr_ref, li_ref, ldt_ref, br_ref, bi_ref, lbr_o, lbi_o, bbr_o, bbi_o):
        lbr, lbi, bbr, bbi = _ssm_param_fn(lr_ref[...], li_ref[...], ldt_ref[...], br_ref[...], bi_ref[...])
        lbr_o[...] = lbr
        lbi_o[...] = lbi
        bbr_o[...] = bbr
        bbi_o[...] = bbi

    row = jax.ShapeDtypeStruct(lr.shape, F32)
    mat = jax.ShapeDtypeStruct(br.shape, F32)
    return pl.pallas_call(
        body, name="ssm_prep", in_specs=[VMEM] * 5, out_specs=[VMEM] * 4,
        out_shape=[row, row, mat, mat], compiler_params=_params(0),
    )(lr, li, ldt, br, bi)


def _ssm_param_bwd(lr, li, ldt, br, bi, dlam8, dbbr, dbbi, dd8):
    nv = dlam8.shape[1]

    def body(lr_ref, li_ref, ldt_ref, br_ref, bi_ref, dl_ref, dbr_ref, dbi_ref, dd_ref,
             glr_o, gli_o, gldt_o, gbr_o, gbi_o, gd_o):
        halves_r, halves_i, halves_d = [], [], []
        for e in range(2):
            ar = dl_ref[0, e:e + 1, :]
            ai = dl_ref[1, e:e + 1, :]
            ad = dd_ref[e:e + 1, :]
            for b in range(1, nv // 2):
                ar = ar + dl_ref[0, 2 * b + e:2 * b + e + 1, :]
                ai = ai + dl_ref[1, 2 * b + e:2 * b + e + 1, :]
                ad = ad + dd_ref[2 * b + e:2 * b + e + 1, :]
            halves_r.append(ar)
            halves_i.append(ai)
            halves_d.append(ad)
        dlbr = jnp.concatenate(halves_r, axis=1)
        dlbi = jnp.concatenate(halves_i, axis=1)
        gd_o[...] = jnp.concatenate(halves_d, axis=1)
        _, vjp = jax.vjp(_ssm_param_fn, lr_ref[...], li_ref[...], ldt_ref[...], br_ref[...], bi_ref[...])
        glr, gli, gldt, gbr, gbi = vjp((dlbr, dlbi, dbr_ref[...], dbi_ref[...]))
        glr_o[...] = glr
        gli_o[...] = gli
        gldt_o[...] = gldt
        gbr_o[...] = gbr
        gbi_o[...] = gbi

    row = jax.ShapeDtypeStruct(lr.shape, F32)
    mat = jax.ShapeDtypeStruct(br.shape, F32)
    return pl.pallas_call(
        body, name="ssm_param_bwd", in_specs=[VMEM] * 9, out_specs=[VMEM] * 6,
        out_shape=[row, row, row, mat, mat, jax.ShapeDtypeStruct((1, SSM_W), F32)],
        compiler_params=_params(0),
    )(lr, li, ldt, br, bi, dlam8, dbbr, dbbi, dd8)


def _blockdiag(m):
    _, g, a, b = m.shape
    eye = jnp.eye(g, dtype=m.dtype)
    return jnp.einsum("egab,gk->egakb", m, eye).reshape(2, g * a, g * b)


def _blockdiag_take(t, a, b):
    return jnp.einsum("gagb->gab", t.reshape(GRP // 2, a, GRP // 2, b))


def _mixer_in_fwd(x, sh1, sc1, g_mix, w_in_b):
    bsz, seq, _ = x.shape
    tt = min(seq, TT_MIX)

    def body(x_ref, sh_ref, sc_ref, g_ref, w_ref, u_ref, p_ref):
        xhat, _ = _rms(x_ref[0])
        h = xhat * g_ref[...] * (1.0 + sc_ref[0]) + sh_ref[0]
        z = _dot(h.astype(BF16), w_ref[...])
        u_ref[...] = z[:, :SSM_W]
        p_ref[0] = z[:, SSM_W:]

    row = pl.BlockSpec((1, 1, D), lambda b, t: (b, 0, 0))
    return pl.pallas_call(
        body, name="mixer_in_fwd", grid=(bsz, seq // tt),
        in_specs=[pl.BlockSpec((1, tt, D), lambda b, t: (b, t, 0)), row, row,
                  pl.BlockSpec((1, D), lambda b, t: (0, 0)), VMEM],
        out_specs=[pl.BlockSpec((tt, SSM_W), lambda b, t: (t, b)),
                   pl.BlockSpec((1, tt, POOL_W), lambda b, t: (b, t, 0))],
        out_shape=[jax.ShapeDtypeStruct((seq, bsz * SSM_W), F32), jax.ShapeDtypeStruct((bsz, seq, POOL_W), F32)],
        compiler_params=_params(2),
    )(x, sh1, sc1, g_mix, w_in_b)


def _ssm_project_in(ub, par0, bb_ref, s_re, s_im, row0, tlen, nv):
    for part, sref in ((0, s_re), (1, s_im)):
        for k in range(HALF_ST // 512):
            c0 = part * HALF_ST + k * 512
            a0 = _dot(ub, bb_ref[:, c0:c0 + 512])
            a1 = _dot(ub, bb_ref[:, 2 * HALF_ST + c0:2 * HALF_ST + c0 + 512])
            sref[pl.ds(row0, tlen), :, k * 512:(k + 1) * 512] = jnp.where(par0, a0, a1).reshape(tlen, nv, 512)


def _ssm_fwd(u2r, bb, cc, lam8, d8, tlen):
    nv = lam8.shape[1]
    rows = nv * tlen
    n_chunks = u2r.shape[0] // rows

    def body(u_ref, bb_ref, cc_ref, lam_ref, d_ref, y_ref, xc_ref, s_re, s_im, st):
        @pl.when(pl.program_id(0) == 0)
        def _():
            st[...] = jnp.zeros_like(st)

        xc_ref[0] = st[...]
        u = u_ref[...]
        par0 = (lax.broadcasted_iota(jnp.int32, (rows, 1), 0) % 2) == 0
        _ssm_project_in(u.astype(BF16), par0, bb_ref, s_re, s_im, 0, tlen, nv)
        for hb in range(HALF_ST // 512):
            ls = slice(hb * 512, (hb + 1) * 512)
            lr = lam_ref[0, :, ls]
            li = lam_ref[1, :, ls]

            def step(t, carry, ls=ls, lr=lr, li=li):
                xr, xi = carry
                nr = lr * xr - li * xi + s_re[t, :, ls]
                ni = lr * xi + li * xr + s_im[t, :, ls]
                s_re[t, :, ls] = nr
                s_im[t, :, ls] = ni
                return nr, ni

            xr, xi = lax.fori_loop(0, tlen, step, (st[0, :, ls], st[1, :, ls]), unroll=8)
            st[0, :, ls] = xr
            st[1, :, ls] = xi
        xre = s_re[...].reshape(rows, HALF_ST).astype(BF16)
        xim = s_im[...].reshape(rows, HALF_ST).astype(BF16)
        y2 = _dot(xre, cc_ref[0:HALF_ST, :]) + _dot(xim, cc_ref[HALF_ST:, :])
        y = jnp.where(par0, y2[:, :HALF_CH], y2[:, HALF_CH:])
        skip = (u.reshape(tlen, nv, HALF_CH) * d_ref[...][None]).reshape(rows, HALF_CH)
        y_ref[...] = y + skip

    return pl.pallas_call(
        body, name="ssm_fwd", grid=(n_chunks,),
        in_specs=[pl.BlockSpec((rows, HALF_CH), lambda c: (c, 0)), VMEM, VMEM, VMEM, VMEM],
        out_specs=[pl.BlockSpec((rows, HALF_CH), lambda c: (c, 0)),
                   pl.BlockSpec((1, 2, nv, HALF_ST), lambda c: (c, 0, 0, 0))],
        out_shape=[jax.ShapeDtypeStruct(u2r.shape, F32), jax.ShapeDtypeStruct((n_chunks, 2, nv, HALF_ST), F32)],
        scratch_shapes=[pltpu.VMEM((tlen, nv, HALF_ST), F32), pltpu.VMEM((tlen, nv, HALF_ST), F32),
                        pltpu.VMEM((2, nv, HALF_ST), F32)],
        compiler_params=_params(1),
    )(u2r, bb, cc, lam8, d8)


def _pool_forward(ext, pv, pos, wp_ref, bp_ref):
    cur = ext
    zs, zls = [], []
    for gi, w in enumerate(POOL_WINDOWS):
        cur = cur + pltpu.roll(cur, w // 2, 0)
        sw = cur[POOL_HALO:, 0:128]
        z = sw / jnp.minimum(pos, float(w)) - pv[:, gi * 128:(gi + 1) * 128]
        zs.append(z)
        zls.append(_dot(z.astype(BF16), wp_ref[gi]) + bp_ref[:, gi * 128:(gi + 1) * 128])
        if gi + 1 < len(POOL_WINDOWS):
            cur = cur[:, 128:]
    return zs, zls


def _mixer_out_fwd(y2, p, x, gt1, w_glu_b, b_glu, w_pool_b, b_pool, pscale, w_out_b):
    bsz, seq, _ = x.shape
    tt = min(seq, TT_MIX)

    def body(y_ref, p_ref, x_ref, gt_ref, wg_ref, bg_ref, wp_ref, bp_ref, ps_ref, wo_ref, x1_ref, mix_ref, ext):
        ti = pl.program_id(1)

        @pl.when(ti == 0)
        def _():
            ext[0:POOL_HALO, :] = jnp.zeros((POOL_HALO, POOL_W), F32)

        pv = p_ref[0]
        ext[POOL_HALO:, :] = pv
        pos = (ti * tt + lax.broadcasted_iota(jnp.int32, (tt, 1), 0) + 1).astype(F32)
        _, zls = _pool_forward(ext[...], pv, pos, wp_ref, bp_ref)
        ext[0:POOL_HALO, :] = pv[tt - POOL_HALO:, :]
        a = _gelu(y_ref[...])
        gl = _dot(a.astype(BF16), wg_ref[...]) + bg_ref[...]
        y_ssm = gl[:, :SSM_W] * jax.nn.sigmoid(gl[:, SSM_W:])
        y_pool = [zl * ps_ref[:, gi * 128:(gi + 1) * 128] for gi, zl in enumerate(zls)]
        mixcat = jnp.concatenate([y_ssm] + y_pool, axis=1).astype(BF16)
        mix_ref[0] = mixcat
        x1_ref[0] = x_ref[0] + gt_ref[0] * _dot(mixcat, wo_ref[...])

    xt = pl.BlockSpec((1, tt, D), lambda b, t: (b, t, 0))
    return pl.pallas_call(
        body, name="mixer_out_fwd", grid=(bsz, seq // tt),
        in_specs=[pl.BlockSpec((tt, SSM_W), lambda b, t: (t, b)),
                  pl.BlockSpec((1, tt, POOL_W), lambda b, t: (b, t, 0)), xt,
                  pl.BlockSpec((1, 1, D), lambda b, t: (b, 0, 0)), VMEM, VMEM, VMEM, VMEM, VMEM, VMEM],
        out_specs=[xt, xt],
        out_shape=[jax.ShapeDtypeStruct(x.shape, F32), jax.ShapeDtypeStruct(x.shape, BF16)],
        scratch_shapes=[pltpu.VMEM((POOL_HALO + tt, POOL_W), F32)],
        compiler_params=_params(2),
    )(y2, p, x, gt1, w_glu_b, b_glu, w_pool_b, b_pool, pscale, w_out_b)


def _conv_gate(g, ge, wc, bc):
    g1 = pltpu.roll(ge, 1, 0)[CONV_HALO:]
    g2 = pltpu.roll(ge, 2, 0)[CONV_HALO:]
    return wc[2:3] * g + wc[1:2] * g1 + wc[0:1] * g2 + bc, g1, g2


def _ffn_fwd(x1, tgt, sh2, sc2, gt2, g_ffn, w_up_b, w_conv, b_conv, w_down_b, g_fin):
    bsz, seq, _ = x1.shape
    tt = min(seq, TT_FFN)
    n_t = seq // tt
    n_ck = DFF // FF_CH

    def body(x1_ref, tg_ref, sh_ref, sc_ref, gt_ref, gf_ref, wu_ref, wc_ref, bc_ref, wd_ref, gfin_ref,
             h2_ref, v_ref, g_ref, act_ref, ddn_ref, dx2_ref, loss_ref, dgfin_ref, dgt_ref, gext, lacc):
        b = pl.program_id(0)
        ti = pl.program_id(1)

        @pl.when((b == 0) & (ti == 0))
        def _():
            lacc[...] = jnp.zeros_like(lacc)
            dgfin_ref[...] = jnp.zeros_like(dgfin_ref)

        @pl.when(ti == 0)
        def _():
            dgt_ref[...] = jnp.zeros_like(dgt_ref)
            gext[:, 0:CONV_HALO, :] = jnp.zeros((n_ck, CONV_HALO, FF_CH), F32)

        x1v = x1_ref[0]
        xhat, _ = _rms(x1v)
        h2b = (xhat * gf_ref[...] * (1.0 + sc_ref[0]) + sh_ref[0]).astype(BF16)
        h2_ref[0] = h2b
        dn = jnp.zeros((tt, D), F32)
        for ck in range(n_ck):
            c0 = ck * FF_CH
            v = _dot(h2b, wu_ref[:, c0:c0 + FF_CH])
            g = _dot(h2b, wu_ref[:, DFF + c0:DFF + c0 + FF_CH])
            v_ref[0, :, c0:c0 + FF_CH] = v.astype(BF16)
            g_ref[0, :, c0:c0 + FF_CH] = g.astype(BF16)
            gext[ck, CONV_HALO:, :] = g
            gc, _, _ = _conv_gate(g, gext[ck], wc_ref[:, c0:c0 + FF_CH], bc_ref[:, c0:c0 + FF_CH])
            gext[ck, 0:CONV_HALO, :] = g[tt - CONV_HALO:, :]
            actb = (gc * jax.nn.sigmoid(gc) * v).astype(BF16)
            act_ref[0, :, c0:c0 + FF_CH] = actb
            dn = dn + _dot(actb, wd_ref[c0:c0 + FF_CH, :])
        gt = gt_ref[0]
        xh3, r3 = _rms(x1v + gt * dn)
        gfin = gfin_ref[...]
        diff = xh3 * gfin - tg_ref[0]
        lacc[...] += _colsum(diff * diff)
        dy = diff * (1.0 / D)
        dgfin_ref[...] += _colsum(dy * xh3)
        dx2 = _rms_bwd(dy * gfin, xh3, r3)
        dx2_ref[0] = dx2
        dgt_ref[0] += _colsum(dx2 * dn)
        ddn_ref[0] = (gt * dx2).astype(BF16)

        @pl.when((b == bsz - 1) & (ti == n_t - 1))
        def _():
            loss_ref[...] = jnp.full(loss_ref.shape, 0.5 / D * jnp.sum(lacc[...]), F32)

    xt = pl.BlockSpec((1, tt, D), lambda b, t: (b, t, 0))
    ft = pl.BlockSpec((1, tt, DFF), lambda b, t: (b, t, 0))
    row = pl.BlockSpec((1, 1, D), lambda b, t: (b, 0, 0))
    vec = pl.BlockSpec((1, D), lambda b, t: (0, 0))
    ff = jax.ShapeDtypeStruct((bsz, seq, DFF), BF16)
    xs = jax.ShapeDtypeStruct((bsz, seq, D), BF16)
    return pl.pallas_call(
        body, name="ffn_fwd", grid=(bsz, n_t),
        in_specs=[xt, xt, row, row, row, vec, VMEM, VMEM, VMEM, VMEM, vec],
        out_specs=[xt, ft, ft, ft, xt, xt, pl.BlockSpec((1, 128), lambda b, t: (0, 0)), vec, row],
        out_shape=[xs, ff, ff, ff, xs, jax.ShapeDtypeStruct((bsz, seq, D), F32),
                   jax.ShapeDtypeStruct((1, 128), F32), jax.ShapeDtypeStruct((1, D), F32),
                   jax.ShapeDtypeStruct((bsz, 1, D), F32)],
        scratch_shapes=[pltpu.VMEM((n_ck, CONV_HALO + tt, FF_CH), F32), pltpu.VMEM((1, D), F32)],
        compiler_params=_params(2),
    )(x1, tgt, sh2, sc2, gt2, g_ffn, w_up_b, w_conv, b_conv, w_down_b, g_fin)


def _ffn_bwd(ddn, gq, vq, x1, dx2, sh2, sc2, g_ffn, w_conv, b_conv, w_down_b, w_up_b):
    bsz, seq, _ = x1.shape
    tt = min(seq, TT_FFN)
    n_t = seq // tt
    n_ck = DFF // FF_CH
    hb = 16
    ext_rows = tt + CONV_HALO

    def body(ddn_ref, g_ref, gh_ref, v_ref, x1_ref, dx2_ref, sh_ref, sc_ref, gf_ref, wc_ref, bc_ref, wd_ref,
             wu_ref, dup_ref, dx1_ref, dsh_ref, dsc_ref, dgf_ref, dwc_ref, dbc_ref, gext, dext):
        b = pl.program_id(0)
        i = pl.program_id(1)
        tile = n_t - 1 - i

        @pl.when((b == 0) & (i == 0))
        def _():
            dgf_ref[...] = jnp.zeros_like(dgf_ref)
            dwc_ref[...] = jnp.zeros_like(dwc_ref)
            dbc_ref[...] = jnp.zeros_like(dbc_ref)

        @pl.when(i == 0)
        def _():
            dsh_ref[...] = jnp.zeros_like(dsh_ref)
            dsc_ref[...] = jnp.zeros_like(dsc_ref)
            dext[:, tt:, :] = jnp.zeros((n_ck, CONV_HALO, FF_CH), F32)

        ddnv = ddn_ref[0]
        has_halo = (tile > 0).astype(F32)
        dh2 = jnp.zeros((tt, D), F32)
        for ck in range(n_ck):
            c0 = ck * FF_CH
            dact = _dot_nt(ddnv, wd_ref[c0:c0 + FF_CH, :])
            g = g_ref[0, :, c0:c0 + FF_CH].astype(F32)
            v = v_ref[0, :, c0:c0 + FF_CH].astype(F32)
            gext[0:CONV_HALO, :] = gh_ref[0, :, c0:c0 + FF_CH].astype(F32)[hb - CONV_HALO:, :] * has_halo
            gext[CONV_HALO:, :] = g
            wc = wc_ref[:, c0:c0 + FF_CH]
            gc, g1, g2 = _conv_gate(g, gext[...], wc, bc_ref[:, c0:c0 + FF_CH])
            sg = jax.nn.sigmoid(gc)
            dv = dact * (gc * sg)
            dgc = dact * v * (sg * (1.0 + gc * (1.0 - sg)))
            dbc_ref[:, c0:c0 + FF_CH] += _colsum(dgc)
            dwc_ref[0:1, c0:c0 + FF_CH] += _colsum(dgc * g2)
            dwc_ref[1:2, c0:c0 + FF_CH] += _colsum(dgc * g1)
            dwc_ref[2:3, c0:c0 + FF_CH] += _colsum(dgc * g)
            dext[ck, 0:tt, :] = dgc
            de = dext[ck]
            dg = (wc[2:3] * dgc + wc[1:2] * pltpu.roll(de, ext_rows - 1, 0)[0:tt]
                  + wc[0:1] * pltpu.roll(de, ext_rows - 2, 0)[0:tt])
            dext[ck, tt:, :] = dgc[0:CONV_HALO, :]
            dvb = dv.astype(BF16)
            dgb = dg.astype(BF16)
            dup_ref[0, :, c0:c0 + FF_CH] = dvb
            dup_ref[0, :, DFF + c0:DFF + c0 + FF_CH] = dgb
            dh2 = dh2 + _dot_nt(dvb, wu_ref[:, c0:c0 + FF_CH]) + _dot_nt(dgb, wu_ref[:, DFF + c0:DFF + c0 + FF_CH])
        xhat, rstd = _rms(x1_ref[0])
        gf = gf_ref[...]
        dsh_ref[0] += _colsum(dh2)
        dsc_ref[0] += _colsum(dh2 * xhat * gf)
        t = dh2 * (1.0 + sc_ref[0])
        dgf_ref[...] += _colsum(t * xhat)
        dx1_ref[0] = dx2_ref[0] + _rms_bwd(t * gf, xhat, rstd)

    def rev(b, t):
        return (b, n_t - 1 - t, 0)

    def halo(b, t):
        return (b, jnp.maximum((n_t - 1 - t) * (tt // hb) - 1, 0), 0)

    xt = pl.BlockSpec((1, tt, D), rev)
    ft = pl.BlockSpec((1, tt, DFF), rev)
    row = pl.BlockSpec((1, 1, D), lambda b, t: (b, 0, 0))
    vec = pl.BlockSpec((1, D), lambda b, t: (0, 0))
    rows = jax.ShapeDtypeStruct((bsz, 1, D), F32)
    return pl.pallas_call(
        body, name="ffn_bwd", grid=(bsz, n_t),
        in_specs=[xt, ft, pl.BlockSpec((1, hb, DFF), halo), ft, xt, xt, row, row, vec, VMEM, VMEM, VMEM, VMEM],
        out_specs=[pl.BlockSpec((1, tt, 2 * DFF), rev), xt, row, row, vec,
                   pl.BlockSpec((3, DFF), lambda b, t: (0, 0)), pl.BlockSpec((1, DFF), lambda b, t: (0, 0))],
        out_shape=[jax.ShapeDtypeStruct((bsz, seq, 2 * DFF), BF16), jax.ShapeDtypeStruct((bsz, seq, D), F32),
                   rows, rows, jax.ShapeDtypeStruct((1, D), F32), jax.ShapeDtypeStruct((3, DFF), F32),
                   jax.ShapeDtypeStruct((1, DFF), F32)],
        scratch_shapes=[pltpu.VMEM((CONV_HALO + tt, FF_CH), F32), pltpu.VMEM((n_ck, ext_rows, FF_CH), F32)],
        compiler_params=_params(2),
    )(ddn, gq, gq, vq, x1, dx2, sh2, sc2, g_ffn, w_conv, b_conv, w_down_b, w_up_b)


def _wgrad(a, b, bk2, name):
    n, k1 = a.shape
    _, k2 = b.shape
    tt = min(n, TT_MIX)

    def body(a_ref, b_ref, o_ref):
        @pl.when(pl.program_id(1) == 0)
        def _():
            o_ref[...] = jnp.zeros_like(o_ref)

        o_ref[...] += _dot_tn(a_ref[...], b_ref[...])

    return pl.pallas_call(
        body, name=name, grid=(k2 // bk2, n // tt),
        in_specs=[pl.BlockSpec((tt, k1), lambda j, i: (i, 0)), pl.BlockSpec((tt, bk2), lambda j, i: (i, j))],
        out_specs=pl.BlockSpec((k1, bk2), lambda j, i: (0, j)),
        out_shape=jax.ShapeDtypeStruct((k1, k2), F32), compiler_params=_params(2),
    )(a, b)


def _mixer_out_bwd(dx1, mixcat, y2, p, gt1, w_glu_b, b_glu, w_pool_b, b_pool, pscale, w_out_b):
    bsz, seq, _ = dx1.shape
    tt = min(seq, TT_MIX)
    n_t = seq // tt
    ext_rows = tt + POOL_HALO

    def body(dx1_ref, mc_ref, y_ref, p_ref, ph_ref, gt_ref, wg_ref, bg_ref, wp_ref, bp_ref, ps_ref, wo_ref,
             dy_ref, dp_ref, dwo_ref, dwg_ref, dbg_ref, dwp_ref, dbp_ref, dps_ref, dgt_ref, ext, qext):
        b = pl.program_id(0)
        i = pl.program_id(1)
        tile = n_t - 1 - i

        @pl.when((b == 0) & (i == 0))
        def _():
            for r in (dwo_ref, dwg_ref, dbg_ref, dwp_ref, dbp_ref, dps_ref):
                r[...] = jnp.zeros_like(r)

        @pl.when(i == 0)
        def _():
            dgt_ref[...] = jnp.zeros_like(dgt_ref)
            qext[tt:, :] = jnp.zeros((POOL_HALO, POOL_W), F32)

        dx1v = dx1_ref[0]
        mc = mc_ref[0]
        dgt_ref[0] += _colsum(dx1v * _dot(mc, wo_ref[...]))
        dmixed = (gt_ref[0] * dx1v).astype(BF16)
        dwo_ref[...] += _dot_tn(mc, dmixed)
        dmc = _dot_nt(dmixed, wo_ref[...])
        pv = p_ref[0]
        ext[0:POOL_HALO, :] = ph_ref[0] * (tile > 0).astype(F32)
        ext[POOL_HALO:, :] = pv
        pos = (tile * tt + lax.broadcasted_iota(jnp.int32, (tt, 1), 0) + 1).astype(F32)
        zs, zls = _pool_forward(ext[...], pv, pos, wp_ref, bp_ref)
        dzs = []
        for gi, w in enumerate(POOL_WINDOWS):
            cs = slice(gi * 128, (gi + 1) * 128)
            dyp = dmc[:, SSM_W + gi * 128:SSM_W + (gi + 1) * 128]
            dps_ref[:, cs] += _colsum(dyp * zls[gi])
            dzl = dyp * ps_ref[:, cs]
            dbp_ref[:, cs] += _colsum(dzl)
            dzlb = dzl.astype(BF16)
            dwp_ref[gi] += _dot_tn(zs[gi].astype(BF16), dzlb)
            dz = _dot_nt(dzlb, wp_ref[gi])
            dzs.append(dz)
            qext[0:tt, cs] = dz / jnp.minimum(pos, float(w))
        cur = qext[...]
        dps = []
        for gi, w in enumerate(POOL_WINDOWS):
            cur = cur + pltpu.roll(cur, ext_rows - w // 2, 0)
            dps.append(cur[0:tt, 0:128] - dzs[gi])
            if gi + 1 < len(POOL_WINDOWS):
                cur = cur[:, 128:]
        qhead = qext[0:POOL_HALO, :]
        qext[tt:, :] = qhead
        dp_ref[0] = jnp.concatenate(dps, axis=1)
        yv = y_ref[...]
        ab = _gelu(yv).astype(BF16)
        gl = _dot(ab, wg_ref[...]) + bg_ref[...]
        val = gl[:, :SSM_W]
        sg = jax.nn.sigmoid(gl[:, SSM_W:])
        dys = dmc[:, :SSM_W]
        dgl = jnp.concatenate([dys * sg, dys * val * sg * (1.0 - sg)], axis=1)
        dbg_ref[...] += _colsum(dgl)
        dglb = dgl.astype(BF16)
        dwg_ref[...] += _dot_tn(ab, dglb)
        dy_ref[...] = _dot_nt(dglb, wg_ref[...]) * _gelu_grad(yv)

    def rev(b, t):
        return (b, n_t - 1 - t, 0)

    def halo(b, t):
        return (b, jnp.maximum((n_t - 1 - t) * (tt // POOL_HALO) - 1, 0), 0)

    xt = pl.BlockSpec((1, tt, D), rev)
    pt = pl.BlockSpec((1, tt, POOL_W), rev)
    yt = pl.BlockSpec((tt, SSM_W), lambda b, t: (n_t - 1 - t, b))

    def whole(shape):
        return pl.BlockSpec(shape, lambda b, t: (0,) * len(shape))

    return pl.pallas_call(
        body, name="mixer_out_bwd", grid=(bsz, n_t),
        in_specs=[xt, xt, yt, pt, pl.BlockSpec((1, POOL_HALO, POOL_W), halo),
                  pl.BlockSpec((1, 1, D), lambda b, t: (b, 0, 0)), VMEM, VMEM, VMEM, VMEM, VMEM, VMEM],
        out_specs=[yt, pt, whole((D, D)), whole((SSM_W, 2 * SSM_W)), whole((1, 2 * SSM_W)),
                   whole((4, 128, 128)), whole((1, POOL_W)), whole((1, POOL_W)),
                   pl.BlockSpec((1, 1, D), lambda b, t: (b, 0, 0))],
        out_shape=[jax.ShapeDtypeStruct(y2.shape, F32), jax.ShapeDtypeStruct(p.shape, F32),
                   jax.ShapeDtypeStruct((D, D), F32), jax.ShapeDtypeStruct((SSM_W, 2 * SSM_W), F32),
                   jax.ShapeDtypeStruct((1, 2 * SSM_W), F32), jax.ShapeDtypeStruct((4, 128, 128), F32),
                   jax.ShapeDtypeStruct((1, POOL_W), F32), jax.ShapeDtypeStruct((1, POOL_W), F32),
                   jax.ShapeDtypeStruct((bsz, 1, D), F32)],
        scratch_shapes=[pltpu.VMEM((POOL_HALO + tt, POOL_W), F32), pltpu.VMEM((ext_rows, POOL_W), F32)],
        compiler_params=_params(2),
    )(dx1, mixcat, y2, p, p, gt1, w_glu_b, b_glu, w_pool_b, b_pool, pscale, w_out_b)


def _ssm_bwd(dy2r, u2r, xc, bb, cc, lam8, d8, tlen):
    nv = lam8.shape[1]
    rows = nv * tlen
    n_chunks = u2r.shape[0] // rows

    def body(dy_ref, u_ref, xc_ref, bb_ref, cc_ref, lam_ref, d_ref, du_ref, dcc_ref, dbb_ref, dlam_ref, dd_ref,
             s_re, s_im, g_re, g_im, gst):
        i = pl.program_id(0)

        @pl.when(i == 0)
        def _():
            for r in (gst, dcc_ref, dbb_ref, dlam_ref, dd_ref):
                r[...] = jnp.zeros_like(r)

        u = u_ref[...]
        dy = dy_ref[...]
        par0 = (lax.broadcasted_iota(jnp.int32, (rows, 1), 0) % 2) == 0
        s_re[0] = xc_ref[0, 0]
        s_im[0] = xc_ref[0, 1]
        _ssm_project_in(u.astype(BF16), par0, bb_ref, s_re, s_im, 1, tlen, nv)
        for hb in range(HALF_ST // 512):
            ls = slice(hb * 512, (hb + 1) * 512)
            lr = lam_ref[0, :, ls]
            li = lam_ref[1, :, ls]

            def fstep(t, carry, ls=ls, lr=lr, li=li):
                xr, xi = carry
                nr = lr * xr - li * xi + s_re[t + 1, :, ls]
                ni = lr * xi + li * xr + s_im[t + 1, :, ls]
                s_re[t + 1, :, ls] = nr
                s_im[t + 1, :, ls] = ni
                return nr, ni

            lax.fori_loop(0, tlen, fstep, (s_re[0, :, ls], s_im[0, :, ls]), unroll=8)
        zero = jnp.zeros_like(dy)
        dy2 = jnp.concatenate([jnp.where(par0, dy, zero), jnp.where(par0, zero, dy)], axis=1).astype(BF16)
        u2 = jnp.concatenate([jnp.where(par0, u, zero), jnp.where(par0, zero, u)], axis=1).astype(BF16)
        xre = s_re[pl.ds(1, tlen)].reshape(rows, HALF_ST).astype(BF16)
        xim = s_im[pl.ds(1, tlen)].reshape(rows, HALF_ST).astype(BF16)
        dcc_ref[0:HALF_ST, :] += _dot_tn(xre, dy2)
        dcc_ref[HALF_ST:, :] += _dot_tn(xim, dy2)
        for part, gref in ((0, g_re), (1, g_im)):
            for k in range(HALF_ST // 512):
                r0 = part * HALF_ST + k * 512
                gref[:, :, k * 512:(k + 1) * 512] = _dot_nt(dy2, cc_ref[r0:r0 + 512, :]).reshape(tlen, nv, 512)
        for hb in range(HALF_ST // 512):
            ls = slice(hb * 512, (hb + 1) * 512)
            lr = lam_ref[0, :, ls]
            li = lam_ref[1, :, ls]

            def bstep(k, carry, ls=ls, lr=lr, li=li):
                t = tlen - 1 - k
                gr, gi, ar, ai = carry
                ngr = g_re[t, :, ls] + lr * gr + li * gi
                ngi = g_im[t, :, ls] + lr * gi - li * gr
                g_re[t, :, ls] = ngr
                g_im[t, :, ls] = ngi
                xpr = s_re[t, :, ls]
                xpi = s_im[t, :, ls]
                return ngr, ngi, ar + ngr * xpr + ngi * xpi, ai + ngi * xpr - ngr * xpi

            init = (gst[0, :, ls], gst[1, :, ls], dlam_ref[0, :, ls], dlam_ref[1, :, ls])
            gr, gi, ar, ai = lax.fori_loop(0, tlen, bstep, init, unroll=4)
            gst[0, :, ls] = gr
            gst[1, :, ls] = gi
            dlam_ref[0, :, ls] = ar
            dlam_ref[1, :, ls] = ai
        gre = g_re[...].reshape(rows, HALF_ST).astype(BF16)
        gim = g_im[...].reshape(rows, HALF_ST).astype(BF16)
        du0 = _dot_nt(gre, bb_ref[:, 0:HALF_ST]) + _dot_nt(gim, bb_ref[:, HALF_ST:2 * HALF_ST])
        du1 = _dot_nt(gre, bb_ref[:, 2 * HALF_ST:3 * HALF_ST]) + _dot_nt(gim, bb_ref[:, 3 * HALF_ST:])
        skip = (dy.reshape(tlen, nv, HALF_CH) * d_ref[...][None]).reshape(rows, HALF_CH)
        du_ref[...] = jnp.where(par0, du0, du1) + skip
        dbb_ref[:, 0:HALF_ST] += _dot_tn(u2, gre)
        dbb_ref[:, HALF_ST:] += _dot_tn(u2, gim)
        dd_ref[...] += jnp.sum((dy * u).reshape(tlen, nv, HALF_CH), axis=0)

        @pl.when(i == n_chunks - 1)
        def _():
            dcc_ref[HALF_ST:, :] = -dcc_ref[HALF_ST:, :]

    def rev(c):
        return (n_chunks - 1 - c, 0)

    def whole(shape):
        return pl.BlockSpec(shape, lambda c: (0,) * len(shape))

    blk = pl.BlockSpec((rows, HALF_CH), rev)
    return pl.pallas_call(
        body, name="ssm_bwd", grid=(n_chunks,),
        in_specs=[blk, blk, pl.BlockSpec((1, 2, nv, HALF_ST), lambda c: (n_chunks - 1 - c, 0, 0, 0)),
                  VMEM, VMEM, VMEM, VMEM],
        out_specs=[blk, whole((2 * HALF_ST, SSM_W)), whole((SSM_W, 2 * HALF_ST)), whole((2, nv, HALF_ST)),
                   whole((nv, HALF_CH))],
        out_shape=[jax.ShapeDtypeStruct(u2r.shape, F32), jax.ShapeDtypeStruct((2 * HALF_ST, SSM_W), F32),
                   jax.ShapeDtypeStruct((SSM_W, 2 * HALF_ST), F32), jax.ShapeDtypeStruct((2, nv, HALF_ST), F32),
                   jax.ShapeDtypeStruct((nv, HALF_CH), F32)],
        scratch_shapes=[pltpu.VMEM((tlen + 1, nv, HALF_ST), F32), pltpu.VMEM((tlen + 1, nv, HALF_ST), F32),
                        pltpu.VMEM((tlen, nv, HALF_ST), F32), pltpu.VMEM((tlen, nv, HALF_ST), F32),
                        pltpu.VMEM((2, nv, HALF_ST), F32)],
        compiler_params=_params(1),
    )(dy2r, u2r, xc, bb, cc, lam8, d8)


def _mixer_in_bwd(du2, dp, x, dx1, sh1, sc1, g_mix, w_in_b):
    bsz, seq, _ = x.shape
    tt = min(seq, TT_MIX)

    def body(du_ref, dp_ref, x_ref, dx1_ref, sh_ref, sc_ref, g_ref, w_ref,
             dx_ref, dw_ref, dsh_ref, dsc_ref, dg_ref):
        b = pl.program_id(0)
        ti = pl.program_id(1)

        @pl.when((b == 0) & (ti == 0))
        def _():
            dw_ref[...] = jnp.zeros_like(dw_ref)
            dg_ref[...] = jnp.zeros_like(dg_ref)

        @pl.when(ti == 0)
        def _():
            dsh_ref[...] = jnp.zeros_like(dsh_ref)
            dsc_ref[...] = jnp.zeros_like(dsc_ref)

        dz = jnp.concatenate([du_ref[...], dp_ref[0]], axis=1).astype(BF16)
        xhat, rstd = _rms(x_ref[0])
        g = g_ref[...]
        sc = sc_ref[0]
        a = xhat * g
        h = (a * (1.0 + sc) + sh_ref[0]).astype(BF16)
        dw_ref[...] += _dot_tn(h, dz)
        dh = _dot_nt(dz, w_ref[...])
        dsh_ref[0] += _colsum(dh)
        dsc_ref[0] += _colsum(dh * a)
        t = dh * (1.0 + sc)
        dg_ref[...] += _colsum(t * xhat)
        dx_ref[0] = dx1_ref[0] + _rms_bwd(t * g, xhat, rstd)

    xt = pl.BlockSpec((1, tt, D), lambda b, t: (b, t, 0))
    row = pl.BlockSpec((1, 1, D), lambda b, t: (b, 0, 0))
    vec = pl.BlockSpec((1, D), lambda b, t: (0, 0))
    rows = jax.ShapeDtypeStruct((bsz, 1, D), F32)
    return pl.pallas_call(
        body, name="mixer_in_bwd", grid=(bsz, seq // tt),
        in_specs=[pl.BlockSpec((tt, SSM_W), lambda b, t: (t, b)),
                  pl.BlockSpec((1, tt, POOL_W), lambda b, t: (b, t, 0)), xt, xt, row, row, vec, VMEM],
        out_specs=[xt, pl.BlockSpec((D, D), lambda b, t: (0, 0)), row, row, vec],
        out_shape=[jax.ShapeDtypeStruct(x.shape, F32), jax.ShapeDtypeStruct((D, D), F32), rows, rows,
                   jax.ShapeDtypeStruct((1, D), F32)],
        compiler_params=_params(2),
    )(du2, dp, x, dx1, sh1, sc1, g_mix, w_in_b)


def _rows128(a):
    flat = a.reshape(-1)
    pad = (-flat.shape[0]) % 128
    if pad:
        flat = jnp.concatenate([flat, jnp.zeros((pad,), flat.dtype)])
    return flat.reshape(-1, 128)


def _pack(parts):
    rows = [_rows128(a) for a in parts]
    total = sum(r.shape[0] for r in rows)
    pad = (-total) % 8
    if pad:
        rows.append(jnp.zeros((pad, 128), rows[0].dtype))
    return jnp.concatenate(rows, axis=0)


def _unpack(packed, shapes):
    out, r0 = [], 0
    for shp in shapes:
        n = math.prod(shp)
        nr = -(-n // 128)
        out.append(packed[r0:r0 + nr].reshape(-1)[:n].reshape(shp))
        r0 += nr
    return out


def _slab(w_in, w_glu, w_out, w_up, w_down):
    return jnp.concatenate([a.reshape(-1, D) for a in (w_in, w_glu, w_out, w_up, w_down)], axis=0)


SLAB_SHAPES = ((128, D), (SSM_W, 128), (128, D), (D, 704), (352, D))


def _unslab(s, lead=()):
    out, r0 = [], 0
    for shp in SLAB_SHAPES:
        nr = shp[0] * shp[1] // D
        out.append(s[..., r0:r0 + nr, :].reshape(lead + shp))
        r0 += nr
    return out


def kernel(x, c, w_ada, b_ada, g_norm_mix, w_in, ssm_lam_re, ssm_lam_im, ssm_log_dt, ssm_b_re, ssm_b_im, ssm_c_re, ssm_c_im, ssm_d, w_glu, b_glu, w_pool, b_pool, pool_scale, w_out, g_norm_ffn, w_up, w_conv, b_conv, w_down, g_norm_final, loss_target, m_w_ada, m_b_ada, m_g_norm_mix, m_w_in, m_ssm_lam_re, m_ssm_lam_im, m_ssm_log_dt, m_ssm_b_re, m_ssm_b_im, m_ssm_c_re, m_ssm_c_im, m_ssm_d, m_w_glu, m_b_glu, m_w_pool, m_b_pool, m_pool_scale, m_w_out, m_g_norm_ffn, m_w_up, m_w_conv, m_b_conv, m_w_down, m_g_norm_final, v_w_ada, v_b_ada, v_g_norm_mix, v_w_in, v_ssm_lam_re, v_ssm_lam_im, v_ssm_log_dt, v_ssm_b_re, v_ssm_b_im, v_ssm_c_re, v_ssm_c_im, v_ssm_d, v_w_glu, v_b_glu, v_w_pool, v_b_pool, v_pool_scale, v_w_out, v_g_norm_ffn, v_w_up, v_w_conv, v_b_conv, v_w_down, v_g_norm_final):
    bsz, seq, _ = x.shape
    assert 2 * bsz == 8 and seq % 128 == 0
    px, py, pc = _my_place()
    me = 4 * px + 2 * py + pc
    qidx = (2 * px + py).astype(jnp.int32).reshape(1)
    cbit = pc.astype(jnp.int32).reshape(1)
    ncol = N_MOD * D // N_DEV

    cpad = jnp.zeros((16, D), F32).at[0:bsz].set(c).at[8:11, 0:352].set(w_conv[0])
    cg = _allgather8(cpad, "gather_c")
    c_all = cg[:, 0:8].reshape(64, D)
    w_conv_f = cg[:, 8:11, 0:352].transpose(1, 0, 2).reshape(3, DFF)
    slab_w = _slab(w_in[0], w_glu[0], w_out[0], w_up[0], w_down[0])
    wg = _allgather8(slab_w.astype(BF16), "gather_weights")
    g_in, g_glu, g_out, g_up, g_down = _unslab(wg, (N_DEV,))
    w_in_b = g_in.reshape(D, D)
    w_glu_b = g_glu.transpose(1, 0, 2).reshape(SSM_W, 2 * SSM_W)
    w_out_b = g_out.reshape(D, D)
    w_up_b = g_up.transpose(1, 0, 2).reshape(D, 2 * DFF)
    w_down_b = g_down.reshape(DFF, D)

    part = _ada_part(c_all, w_ada[0])
    pg = _allgather8(part, "gather_mod")
    mine = lax.dynamic_slice_in_dim(pg, 8 * me, 8, axis=1)[:, 0:bsz]
    mod = mine.transpose(1, 0, 2).reshape(bsz, N_MOD * D) + b_ada
    sh1, sc1, gt1, sh2, sc2, gt2 = [mod[:, k * D:(k + 1) * D].reshape(bsz, 1, D) for k in range(N_MOD)]

    lam_r = ssm_lam_re[0].reshape(1, GRP * NST)
    lam_i = ssm_lam_im[0].reshape(1, GRP * NST)
    ldt = jnp.repeat(ssm_log_dt[0], NST).reshape(1, GRP * NST)
    b_r = ssm_b_re[0].transpose(2, 0, 1).reshape(GCH, GRP * NST)
    b_i = ssm_b_im[0].transpose(2, 0, 1).reshape(GCH, GRP * NST)
    lbr, lbi, bbr, bbi = _ssm_prep(lam_r, lam_i, ldt, b_r, b_i)
    lam8 = jnp.stack([jnp.tile(lbr.reshape(2, HALF_ST), (bsz, 1)), jnp.tile(lbi.reshape(2, HALF_ST), (bsz, 1))])
    bd_r = _blockdiag(bbr.reshape(GCH, 2, GRP // 2, NST).transpose(1, 2, 0, 3))
    bd_i = _blockdiag(bbi.reshape(GCH, 2, GRP // 2, NST).transpose(1, 2, 0, 3))
    bb = jnp.concatenate([bd_r[0], bd_i[0], bd_r[1], bd_i[1]], axis=1).astype(BF16)
    cd_r = _blockdiag(ssm_c_re[0].reshape(2, GRP // 2, GCH, NST).transpose(0, 1, 3, 2))
    cd_i = _blockdiag(ssm_c_im[0].reshape(2, GRP // 2, GCH, NST).transpose(0, 1, 3, 2))
    cc = jnp.concatenate([jnp.concatenate([cd_r[0], cd_r[1]], axis=1),
                          jnp.concatenate([-cd_i[0], -cd_i[1]], axis=1)], axis=0).astype(BF16)
    d8 = jnp.tile(ssm_d[0].reshape(2, HALF_CH), (bsz, 1))

    tlen = min(seq, T_SSM)
    u2, p = _mixer_in_fwd(x, sh1, sc1, g_norm_mix, w_in_b)
    u2r = u2.reshape(seq * 2 * bsz, HALF_CH)
    y2r, xc = _ssm_fwd(u2r, bb, cc, lam8, d8, tlen)
    y2 = y2r.reshape(seq, bsz * SSM_W)
    w_pool_b = w_pool[0].astype(BF16)
    bp = b_pool[0].reshape(1, POOL_W)
    x1, mixcat = _mixer_out_fwd(y2, p, x, gt1, w_glu_b, b_glu, w_pool_b, bp, pool_scale, w_out_b)
    h2, vq, gq, act, ddn, dx2, loss_l, dg_fin, dgt2 = _ffn_fwd(
        x1, loss_target, sh2, sc2, gt2, g_norm_ffn, w_up_b, w_conv_f, b_conv, w_down_b, g_norm_final.reshape(1, D))
    loss = lax.psum(loss_l[0, 0], ("x", "y", "c"))

    dup, dx1, dsh2, dsc2, dg_ffn, dw_conv, db_conv = _ffn_bwd(
        ddn, gq, vq, x1, dx2, sh2, sc2, g_norm_ffn, w_conv_f, b_conv, w_down_b, w_up_b)
    ntok = bsz * seq
    dw_up = _wgrad(h2.reshape(ntok, D), dup.reshape(ntok, 2 * DFF), FF_CH, "wgrad_up")
    dw_down = _wgrad(act.reshape(ntok, DFF), ddn.reshape(ntok, D), 512, "wgrad_down")
    dy2, dp, dw_out, dw_glu, db_glu, dw_pool, db_pool, dpscale, dgt1 = _mixer_out_bwd(
        dx1, mixcat, y2, p, gt1, w_glu_b, b_glu, w_pool_b, bp, pool_scale, w_out_b)
    du2r, dcc, dbb, dlam8, dd8 = _ssm_bwd(dy2.reshape(u2r.shape), u2r, xc, bb, cc, lam8, d8, tlen)
    grad_x, dw_in, dsh1, dsc1, dg_mix = _mixer_in_bwd(
        du2r.reshape(u2.shape), dp, x, dx1, sh1, sc1, g_norm_mix, w_in_b)

    def take_c(t):
        return _blockdiag_take(t, NST, GCH).transpose(0, 2, 1)

    dc_re = jnp.concatenate([take_c(dcc[0:HALF_ST, e * HALF_CH:(e + 1) * HALF_CH]) for e in range(2)], axis=0)
    dc_im = jnp.concatenate([take_c(dcc[HALF_ST:, e * HALF_CH:(e + 1) * HALF_CH]) for e in range(2)], axis=0)

    def take_b(t):
        return _blockdiag_take(t, GCH, NST).transpose(1, 0, 2)

    dbbr = jnp.concatenate([take_b(dbb[e * HALF_CH:(e + 1) * HALF_CH, 0:HALF_ST]) for e in range(2)], axis=1)
    dbbi = jnp.concatenate([take_b(dbb[e * HALF_CH:(e + 1) * HALF_CH, HALF_ST:]) for e in range(2)], axis=1)
    glr, gli, gldt, gbr, gbi, gd = _ssm_param_bwd(
        lam_r, lam_i, ldt, b_r, b_i, dlam8, dbbr.reshape(GCH, GRP * NST), dbbi.reshape(GCH, GRP * NST), dd8)
    g_log_dt = jnp.sum(gldt.reshape(GRP, NST), axis=1)
    g_b_re = gbr.reshape(GCH, GRP, NST).transpose(1, 2, 0)
    g_b_im = gbi.reshape(GCH, GRP, NST).transpose(1, 2, 0)

    dmod = jnp.concatenate([t.reshape(bsz, D) for t in (dsh1, dsc1, dgt1, dsh2, dsc2, dgt2)], axis=1)
    dmod_blk = jnp.zeros((N_DEV, 8, ncol), F32).at[:, 0:bsz].set(dmod.reshape(bsz, N_DEV, ncol).transpose(1, 0, 2))
    dg = _allgather8(dmod_blk.reshape(N_DEV * 8, ncol), "gather_dmod")
    dmod_blocks = dg.reshape(N_DEV, N_DEV, 8, ncol).transpose(1, 0, 2, 3).reshape(N_DEV, 64, ncol)
    dmod_cols = lax.dynamic_index_in_dim(dmod_blocks, me, axis=0, keepdims=False)
    ada = _ada_bwd(c_all, dmod_cols, dmod_blocks, w_ada[0], m_w_ada[0], v_w_ada[0],
                   b_ada.reshape(N_DEV, 1, ncol), m_b_ada.reshape(N_DEV, 1, ncol), v_b_ada.reshape(N_DEV, 1, ncol))
    g_w_ada, d_w_ada, nm_w_ada, nv_w_ada = [t[None] for t in ada[0:4]]
    g_b_ada, d_b_ada, nm_b_ada, nv_b_ada = [t.reshape(1, N_MOD * D) for t in ada[4:8]]

    gslab = jnp.concatenate([
        dw_in.reshape(N_DEV, 128, D),
        dw_glu.reshape(SSM_W, N_DEV, 128).transpose(1, 0, 2).reshape(N_DEV, -1, D),
        dw_out.reshape(N_DEV, 128, D),
        dw_up.reshape(D, N_DEV, 704).transpose(1, 0, 2).reshape(N_DEV, -1, D),
        dw_down.reshape(N_DEV, 352, D)], axis=1)
    srows = gslab.shape[1]
    g42 = gslab.reshape(4, 2, srows, D)
    recv_a = _pair_exchange(g42, "grad_pair_exchange")
    s4 = _pair_sum(g42, recv_a, cbit)
    recv_c = _chip_exchange(s4, "grad_chip_exchange")
    slab_m = _slab(m_w_in[0], m_w_glu[0], m_w_out[0], m_w_up[0], m_w_down[0])
    slab_v = _slab(v_w_in[0], v_w_glu[0], v_w_out[0], v_w_up[0], v_w_down[0])
    big = [[t[None] for t in _unslab(s)] for s in _final_sum_adamw(s4, recv_c, qidx, slab_w, slab_m, slab_v)]
    (g_w_in, g_w_glu, g_w_out, g_w_up, g_w_down), (d_w_in, d_w_glu, d_w_out, d_w_up, d_w_down), \
        (nm_w_in, nm_w_glu, nm_w_out, nm_w_up, nm_w_down), (nv_w_in, nv_w_glu, nv_w_out, nv_w_up, nv_w_down) = big

    small_g = [dg_mix, glr, gli, g_log_dt, g_b_re, g_b_im, dc_re, dc_im, gd, db_glu, dw_pool, db_pool, dpscale,
               dg_ffn, db_conv, dg_fin, dw_conv]
    small_w = [g_norm_mix, ssm_lam_re, ssm_lam_im, ssm_log_dt, ssm_b_re, ssm_b_im, ssm_c_re, ssm_c_im, ssm_d, b_glu,
               w_pool, b_pool, pool_scale, g_norm_ffn, b_conv, g_norm_final]
    small_m = [m_g_norm_mix, m_ssm_lam_re, m_ssm_lam_im, m_ssm_log_dt, m_ssm_b_re, m_ssm_b_im, m_ssm_c_re,
               m_ssm_c_im, m_ssm_d, m_b_glu, m_w_pool, m_b_pool, m_pool_scale, m_g_norm_ffn, m_b_conv, m_g_norm_final]
    small_v = [v_g_norm_mix, v_ssm_lam_re, v_ssm_lam_im, v_ssm_log_dt, v_ssm_b_re, v_ssm_b_im, v_ssm_c_re,
               v_ssm_c_im, v_ssm_d, v_b_glu, v_w_pool, v_b_pool, v_pool_scale, v_g_norm_ffn, v_b_conv, v_g_norm_final]
    shapes = [w.shape for w in small_w]
    parts = _allgather8(_pack(small_g), "gather_small_grads")
    zero_conv = jnp.zeros((3, DFF), F32)
    sg_, sd_, sm_, sv_ = _sum8_adamw(parts, _pack(small_w + [zero_conv]), _pack(small_m + [zero_conv]),
                                     _pack(small_v + [zero_conv]))
    sg_l = _unpack(sg_, shapes + [(3, DFF)])
    sd_l = _unpack(sd_, shapes)
    sm_l = _unpack(sm_, shapes)
    sv_l = _unpack(sv_, shapes)
    g_conv_full = sg_l[-1]
    g_w_conv = lax.dynamic_slice_in_dim(g_conv_full, 352 * me, 352, axis=1)
    cv = _adamw_plain(_rows128(g_w_conv), _rows128(w_conv[0]), _rows128(m_w_conv[0]), _rows128(v_w_conv[0]))
    d_w_conv, nm_w_conv, nv_w_conv = [t.reshape(-1)[:3 * 352].reshape(1, 3, 352) for t in cv]
    g_w_conv = g_w_conv[None]

    def order(ada_w, ada_b, small, w_in_, w_glu_, w_out_, w_up_, w_conv_, w_down_):
        (mix, lre, lim, ldt_, bre, bim, cre, cim, dd, bglu, wpool, bpool, pscale, gffn, bconv, gfin) = small
        return [ada_w, ada_b, mix, w_in_, lre, lim, ldt_, bre, bim, cre, cim, dd, w_glu_, bglu, wpool, bpool, pscale,
                w_out_, gffn, w_up_, w_conv_, bconv, w_down_, gfin]

    grads = order(g_w_ada, g_b_ada, sg_l[:-1], g_w_in, g_w_glu, g_w_out, g_w_up, g_w_conv, g_w_down)
    deltas = order(d_w_ada, d_b_ada, sd_l, d_w_in, d_w_glu, d_w_out, d_w_up, d_w_conv, d_w_down)
    new_m = order(nm_w_ada, nm_b_ada, sm_l, nm_w_in, nm_w_glu, nm_w_out, nm_w_up, nm_w_conv, nm_w_down)
    new_v = order(nv_w_ada, nv_b_ada, sv_l, nv_w_in, nv_w_glu, nv_w_out, nv_w_up, nv_w_conv, nv_w_down)
    return (loss, grad_x, *grads, *deltas, *new_m, *new_v)
```

```python
import functools
import math

import jax
import jax.numpy as jnp
from jax import lax
from jax.experimental import pallas as pl
from jax.experimental.pallas import tpu as pltpu

F32 = jnp.float32
BF16 = jnp.bfloat16

D = 1024
SSM_W = 512
POOL_W = 512
GRP = 32
GCH = 16
NST = 64
HALF_ST = GRP * NST // 2
HALF_CH = SSM_W // 2
DFF = 2816
FF_CH = 1408
N_MOD = 6
N_DEV = 8
EPS = 1e-6
POOL_WINDOWS = (2, 4, 8, 16)
POOL_HALO = 16
CONV_HALO = 8
GELU_C = math.sqrt(2.0 / math.pi)
GELU_A = 0.044715

ADAM_LR = 0.001
ADAM_B1 = 0.9
ADAM_B2 = 0.999
ADAM_EPS = 1e-08
ADAM_WD = 0.01
ADAM_STEP = 10

VMEM_LIMIT = 56 * 1024 * 1024
TT_MIX = 512
TT_FFN = 256
T_SSM = 128
MESH = pl.DeviceIdType.MESH
NT = (((1,), (1,)), ((), ()))
TN = (((0,), (0,)), ((), ()))
ANY = pl.BlockSpec(memory_space=pl.ANY)
VMEM = pl.BlockSpec(memory_space=pltpu.VMEM)


def _params(n_grid, vmem=VMEM_LIMIT):
    return pltpu.CompilerParams(dimension_semantics=("arbitrary",) * n_grid, vmem_limit_bytes=vmem)


def _dot(a, b):
    return jnp.dot(a, b, preferred_element_type=F32)


def _dot_nt(a, b):
    return lax.dot_general(a, b, NT, preferred_element_type=F32)


def _dot_tn(a, b):
    return lax.dot_general(a, b, TN, preferred_element_type=F32)


def _colsum(a):
    return jnp.sum(a, axis=0, keepdims=True)


def _rms(x):
    rstd = lax.rsqrt(jnp.mean(x * x, axis=-1, keepdims=True) + EPS)
    return x * rstd, rstd


def _rms_bwd(dxhat, xhat, rstd):
    return rstd * (dxhat - xhat * jnp.mean(dxhat * xhat, axis=-1, keepdims=True))


def _gelu(x):
    return 0.5 * x * (1.0 + jnp.tanh(GELU_C * (x + GELU_A * x * x * x)))


def _gelu_grad(x):
    x2 = x * x
    th = jnp.tanh(GELU_C * (x + GELU_A * x * x2))
    return 0.5 * (1.0 + th) + 0.5 * x * (1.0 - th * th) * GELU_C * (1.0 + 3.0 * GELU_A * x2)


def _adamw(w, g, m, v):
    m = ADAM_B1 * m + (1.0 - ADAM_B1) * g
    v = ADAM_B2 * v + (1.0 - ADAM_B2) * (g * g)
    m_hat = m / (1.0 - ADAM_B1 ** ADAM_STEP)
    v_hat = v / (1.0 - ADAM_B2 ** ADAM_STEP)
    delta = -ADAM_LR * (m_hat / (jnp.sqrt(v_hat) + ADAM_EPS) + ADAM_WD * w)
    return delta, m, v


def _my_place():
    return lax.axis_index("x"), lax.axis_index("y"), lax.axis_index("c")


def _gather_plan(shards):
    n = len(shards)
    out_shape = [jax.ShapeDtypeStruct((N_DEV,) + tuple(s.shape), s.dtype) for s in shards]
    scratch = [pltpu.SemaphoreType.DMA((n, 7)), pltpu.SemaphoreType.DMA((n, 7)), pltpu.SemaphoreType.DMA((n,))]

    def stages(x_refs, out_refs, sems):
        send_sems, recv_sems, local_sems = sems
        x, y, c = _my_place()
        me, sibling = (x, y, c), (x, y, 1 - c)
        chips = [(1 - x, y), (x, 1 - y), (1 - x, 1 - y)]

        def copy(i, k, block, to, own=False):
            px, py, pc = block
            dst = out_refs[i].at[4 * px + 2 * py + pc]
            return pltpu.make_async_remote_copy(
                src_ref=x_refs[i] if own else dst, dst_ref=dst, send_sem=send_sems.at[i, k],
                recv_sem=recv_sems.at[i, k], device_id=to, device_id_type=MESH)

        def mine(i):
            return pltpu.make_async_copy(x_refs[i], out_refs[i].at[4 * x + 2 * y + c], local_sems.at[i])

        def start():
            for i in range(n):
                mine(i).start()
                copy(i, 0, me, sibling, own=True).start()
                for j, chip in enumerate(chips):
                    copy(i, 1 + j, me, (*chip, c), own=True).start()

        def forward():
            for i in range(n):
                for j, chip in enumerate(chips):
                    copy(i, 1 + j, (*chip, c), me).wait_recv()
                    copy(i, 4 + j, (*chip, c), sibling).start()

        def finish():
            for i in range(n):
                copy(i, 0, sibling, me).wait_recv()
                copy(i, 0, me, sibling, own=True).wait_send()
                for j, chip in enumerate(chips):
                    copy(i, 4 + j, (*chip, 1 - c), me).wait_recv()
                    copy(i, 1 + j, me, (*chip, c), own=True).wait_send()
                    copy(i, 4 + j, (*chip, c), sibling).wait_send()
                mine(i).wait()

        return [start, forward, finish]

    return n, out_shape, scratch, stages


def _pair_plan(g42s):
    n = len(g42s)
    out_shape = [jax.ShapeDtypeStruct((4,) + tuple(g.shape[2:]), g.dtype) for g in g42s]
    scratch = [pltpu.SemaphoreType.DMA((n,)), pltpu.SemaphoreType.DMA((n,))]

    def stages(g_refs, out_refs, sems):
        send_sems, recv_sems = sems
        x, y, c = _my_place()

        def copy(i):
            return pltpu.make_async_remote_copy(
                src_ref=g_refs[i].at[:, 1 - c], dst_ref=out_refs[i], send_sem=send_sems.at[i],
                recv_sem=recv_sems.at[i], device_id=(x, y, 1 - c), device_id_type=MESH)

        def start():
            for i in range(n):
                copy(i).start()

        def finish():
            for i in range(n):
                copy(i).wait()

        return [start, finish]

    return n, out_shape, scratch, stages


def _chip_plan(s4s):
    n = len(s4s)
    out_shape = [jax.ShapeDtypeStruct((3,) + tuple(s.shape[1:]), s.dtype) for s in s4s]
    scratch = [pltpu.SemaphoreType.DMA((n, 3)), pltpu.SemaphoreType.DMA((n, 3))]

    def stages(s_refs, out_refs, sems):
        send_sems, recv_sems = sems
        x, y, c = _my_place()

        def copy(i, d):
            px, py = x ^ (d >> 1), y ^ (d & 1)
            return pltpu.make_async_remote_copy(
                src_ref=s_refs[i].at[2 * px + py], dst_ref=out_refs[i].at[d - 1], send_sem=send_sems.at[i, d - 1],
                recv_sem=recv_sems.at[i, d - 1], device_id=(px, py, c), device_id_type=MESH)

        def start():
            for i in range(n):
                for d in (1, 2, 3):
                    copy(i, d).start()

        def finish():
            for i in range(n):
                for d in (1, 2, 3):
                    copy(i, d).wait()

        return [start, finish]

    return n, out_shape, scratch, stages


def _comm_call(plan, arrays, name):
    n, out_shape, scratch, stages = plan

    def body(*refs):
        for stage in stages(refs[:n], refs[n:2 * n], refs[2 * n:]):
            stage()

    return pl.pallas_call(
        body, name=name, out_shape=out_shape, in_specs=[ANY] * n, out_specs=[ANY] * n, scratch_shapes=scratch,
    )(*arrays)


def _fused_call(body, *, name, grid, in_specs, out_specs, out_shape, scratch_shapes, args, comm=None):
    if comm is None:
        out = pl.pallas_call(body, name=name, grid=grid, in_specs=in_specs, out_specs=out_specs, out_shape=out_shape,
                             scratch_shapes=scratch_shapes, compiler_params=_params(len(grid)))(*args)
        return out, []
    (n, c_shape, c_scratch, stages), arrays, steps = comm
    n_in, n_out, n_scr = len(in_specs), len(out_specs), len(scratch_shapes)

    def fused(*refs):
        ins, refs = refs[:n_in], refs[n_in:]
        c_ins, refs = refs[:n], refs[n:]
        outs, refs = refs[:n_out], refs[n_out:]
        c_outs, refs = refs[:n], refs[n:]
        scr, c_scr = refs[:n_scr], refs[n_scr:]
        step = pl.program_id(0)
        for k in range(1, len(grid)):
            step = step * grid[k] + pl.program_id(k)
        todo = list(zip(stages(c_ins, c_outs, c_scr), steps))
        for stage, at in todo:
            if at == 0:
                pl.when(step == 0)(stage)
        body(*ins, *outs, *scr)
        for stage, at in todo:
            if at != 0:
                pl.when(step == at)(stage)

    out = pl.pallas_call(
        fused, name=name, grid=grid, in_specs=list(in_specs) + [ANY] * n, out_specs=list(out_specs) + [ANY] * n,
        out_shape=list(out_shape) + list(c_shape), scratch_shapes=list(scratch_shapes) + list(c_scratch),
        compiler_params=_params(len(grid)))(*args, *arrays)
    return out[:n_out], out[n_out:]


def _row_tile(r):
    for t in (128, 64, 32, 16, 8):
        if r % t == 0:
            return t
    return r


def _pair_sum(g42, recv, place, name):
    _, _, r, cdim = g42.shape
    tr = _row_tile(r)

    def body(pl_ref, g_ref, r_ref, own_ref, s_ref):
        s_ref[...] = (g_ref[:, 0] + r_ref[...]).astype(BF16)
        q = pl_ref[1]
        own_ref[...] = g_ref[q, 0] + r_ref[q]

    return pl.pallas_call(
        body, name=name,
        grid_spec=pltpu.PrefetchScalarGridSpec(
            num_scalar_prefetch=1, grid=(r // tr,),
            in_specs=[pl.BlockSpec((4, 1, tr, cdim), lambda i, p: (0, p[0], i, 0)),
                      pl.BlockSpec((4, tr, cdim), lambda i, p: (0, i, 0))],
            out_specs=[pl.BlockSpec((tr, cdim), lambda i, p: (i, 0)),
                       pl.BlockSpec((4, tr, cdim), lambda i, p: (0, i, 0))]),
        out_shape=[jax.ShapeDtypeStruct((r, cdim), F32), jax.ShapeDtypeStruct((4, r, cdim), BF16)],
        compiler_params=_params(1),
    )(place, g42, recv)


def _final_sum_adamw(own, recv3, w, m, v, name):
    r, cdim = w.shape
    tr = _row_tile(r)

    def body(s_ref, r_ref, w_ref, m_ref, v_ref, g_out, d_out, m_out, v_out):
        g = s_ref[...] + r_ref[0].astype(F32) + r_ref[1].astype(F32) + r_ref[2].astype(F32)
        d, mn, vn = _adamw(w_ref[...], g, m_ref[...], v_ref[...])
        g_out[...] = g
        d_out[...] = d
        m_out[...] = mn
        v_out[...] = vn

    blk = pl.BlockSpec((tr, cdim), lambda i: (i, 0))
    shp = jax.ShapeDtypeStruct((r, cdim), F32)
    return pl.pallas_call(
        body, name=name, grid=(r // tr,),
        in_specs=[blk, pl.BlockSpec((3, tr, cdim), lambda i: (0, i, 0)), blk, blk, blk],
        out_specs=[blk, blk, blk, blk], out_shape=[shp, shp, shp, shp], compiler_params=_params(1),
    )(own, recv3, w, m, v)


def _sum8_adamw(parts, w, m, v):
    r, cdim = w.shape
    tr = _row_tile(r)

    def body(p_ref, w_ref, m_ref, v_ref, g_out, d_out, m_out, v_out):
        g = p_ref[0]
        for k in range(1, N_DEV):
            g = g + p_ref[k]
        d, mn, vn = _adamw(w_ref[...], g, m_ref[...], v_ref[...])
        g_out[...] = g
        d_out[...] = d
        m_out[...] = mn
        v_out[...] = vn

    blk = pl.BlockSpec((tr, cdim), lambda i: (i, 0))
    shp = jax.ShapeDtypeStruct((r, cdim), F32)
    return pl.pallas_call(
        body, name="small_sum_adamw", grid=(r // tr,),
        in_specs=[pl.BlockSpec((N_DEV, tr, cdim), lambda i: (0, i, 0)), blk, blk, blk],
        out_specs=[blk, blk, blk, blk], out_shape=[shp, shp, shp, shp],
        compiler_params=_params(1),
    )(parts, w, m, v)


def _adamw_plain(g, w, m, v):
    r, cdim = w.shape
    tr = _row_tile(r)

    def body(g_ref, w_ref, m_ref, v_ref, d_out, m_out, v_out):
        d, mn, vn = _adamw(w_ref[...], g_ref[...], m_ref[...], v_ref[...])
        d_out[...] = d
        m_out[...] = mn
        v_out[...] = vn

    blk = pl.BlockSpec((tr, cdim), lambda i: (i, 0))
    shp = jax.ShapeDtypeStruct((r, cdim), F32)
    return pl.pallas_call(
        body, name="adamw_plain", grid=(r // tr,), in_specs=[blk, blk, blk, blk],
        out_specs=[blk, blk, blk], out_shape=[shp, shp, shp], compiler_params=_params(1),
    )(g, w, m, v)


ADA_COLS = N_MOD * D // N_DEV
ADA_ROWS = 8 * N_DEV


def _ada_fwd(cpad, w_ada, b_blocks, mixer_shards):
    n_w, w_shape, w_scr, w_stages = _gather_plan(mixer_shards)
    _, _, c_scr, c_stages = _gather_plan([cpad])
    _, _, p_scr, p_stages = _gather_plan([jax.ShapeDtypeStruct((ADA_ROWS, ADA_COLS), F32)])

    def body(c_ref, wa_ref, b_ref, *refs):
        w_refs, refs = refs[:n_w], refs[n_w:]
        cg_ref, call_ref, mod_ref = refs[:3]
        wg_refs, refs = refs[3:3 + n_w], refs[3 + n_w:]
        part_ref, pg_ref = refs[:2]
        c_sems, p_sems, w_sems = refs[2:5], refs[5:8], refs[8:11]
        w_start, w_forward, w_finish = w_stages(w_refs, wg_refs, w_sems)
        w_start()
        for stage in c_stages([c_ref], [cg_ref], c_sems):
            stage()
        cv = cg_ref[:, 0:8, :].reshape(ADA_ROWS, D)
        call_ref[...] = cv
        part_ref[...] = _dot(cv * jax.nn.sigmoid(cv), wa_ref[...])
        for stage in p_stages([part_ref], [pg_ref], p_sems):
            stage()
        x, y, c = _my_place()
        r0 = pl.multiple_of(8 * (4 * x + 2 * y + c), 8)
        for k in range(N_DEV):
            mod_ref[:, k * ADA_COLS:(k + 1) * ADA_COLS] = pg_ref[k, pl.ds(r0, 8), :] + b_ref[k]
        w_forward()
        w_finish()

    out = pl.pallas_call(
        body, name="ada_fwd", in_specs=[VMEM, VMEM, VMEM] + [ANY] * n_w,
        out_specs=[VMEM, VMEM, VMEM] + [ANY] * n_w,
        out_shape=[jax.ShapeDtypeStruct((N_DEV,) + cpad.shape, F32), jax.ShapeDtypeStruct((ADA_ROWS, D), F32),
                   jax.ShapeDtypeStruct((8, N_MOD * D), F32)] + list(w_shape),
        scratch_shapes=[pltpu.VMEM((ADA_ROWS, ADA_COLS), F32), pltpu.VMEM((N_DEV, ADA_ROWS, ADA_COLS), F32)]
        + list(c_scr) + list(p_scr) + list(w_scr),
        compiler_params=_params(0),
    )(cpad, w_ada, b_blocks, *mixer_shards)
    return out[0], out[1], out[2], out[3:]


def _ada_bwd(dmod_blk, c_all, w_ada, m_w, v_w, b_blocks, m_b, v_b):
    _, _, g_scr, g_stages = _gather_plan([dmod_blk])

    def body(dm_ref, c_ref, w_ref, mw_ref, vw_ref, b_ref, mb_ref, vb_ref,
             gw_o, dw_o, mw_o, vw_o, gb_o, dbb_o, mb_o, vb_o, dg_ref, *sems):
        for stage in g_stages([dm_ref], [dg_ref], sems):
            stage()
        x, y, c = _my_place()
        r0 = pl.multiple_of(8 * (4 * x + 2 * y + c), 8)
        cols = dg_ref[:, pl.ds(r0, 8), :].reshape(ADA_ROWS, ADA_COLS)
        cv = c_ref[...]
        gw = _dot_tn(cv * jax.nn.sigmoid(cv), cols)
        d, mn, vn = _adamw(w_ref[...], gw, mw_ref[...], vw_ref[...])
        gw_o[...] = gw
        dw_o[...] = d
        mw_o[...] = mn
        vw_o[...] = vn
        for k in range(N_DEV):
            gb = _colsum(dg_ref[:, 8 * k:8 * k + 8, :].reshape(ADA_ROWS, ADA_COLS))
            d, mn, vn = _adamw(b_ref[k], gb, mb_ref[k], vb_ref[k])
            gb_o[k] = gb
            dbb_o[k] = d
            mb_o[k] = mn
            vb_o[k] = vn

    ws = jax.ShapeDtypeStruct(w_ada.shape, F32)
    bs = jax.ShapeDtypeStruct(b_blocks.shape, F32)
    return pl.pallas_call(
        body, name="ada_bwd", in_specs=[VMEM] * 8, out_specs=[VMEM] * 8,
        out_shape=[ws, ws, ws, ws, bs, bs, bs, bs],
        scratch_shapes=[pltpu.VMEM((N_DEV, ADA_ROWS, ADA_COLS), F32)] + list(g_scr),
        compiler_params=_params(0),
    )(dmod_blk, c_all, w_ada, m_w, v_w, b_blocks, m_b, v_b)


def _ssm_param_fn(lr, li, ldt, br, bi):
    dt = jnp.exp(ldt)
    mag = jnp.exp(lr * dt)
    ang = li * dt
    lbr = mag * jnp.cos(ang)
    lbi = mag * jnp.sin(ang)
    nr = lbr - 1.0
    den = lr * lr + li * li
    cr = (nr * lr + lbi * li) / den
    ci = (lbi * lr - nr * li) / den
    return lbr, lbi, cr * br - ci * bi, cr * bi + ci * br


def _ssm_prep(lr, li, ldt, br, bi):
    def body(lr_ref, li_ref, ldt_ref, br_ref, bi_ref, lbr_o, lbi_o, bbr_o, bbi_o):
        lbr, lbi, bbr, bbi = _ssm_param_fn(lr_ref[...], li_ref[...], ldt_ref[...], br_ref[...], bi_ref[...])
        lbr_o[...] = lbr
        lbi_o[...] = lbi
        bbr_o[...] = bbr
        bbi_o[...] = bbi

    row = jax.ShapeDtypeStruct(lr.shape, F32)
    mat = jax.ShapeDtypeStruct(br.shape, F32)
    return pl.pallas_call(
        body, name="ssm_prep", in_specs=[VMEM] * 5, out_specs=[VMEM] * 4,
        out_shape=[row, row, mat, mat], compiler_params=_params(0),
    )(lr, li, ldt, br, bi)


def _ssm_param_bwd(lr, li, ldt, br, bi, dlam8, dbbr, dbbi, dd8):
    nv = dlam8.shape[1]

    def body(lr_ref, li_ref, ldt_ref, br_ref, bi_ref, dl_ref, dbr_ref, dbi_ref, dd_ref,
             glr_o, gli_o, gldt_o, gbr_o, gbi_o, gd_o):
        halves_r, halves_i, halves_d = [], [], []
        for e in range(2):
            ar = dl_ref[0, e:e + 1, :]
            ai = dl_ref[1, e:e + 1, :]
            ad = dd_ref[e:e + 1, :]
            for b in range(1, nv // 2):
                ar = ar + dl_ref[0, 2 * b + e:2 * b + e + 1, :]
                ai = ai + dl_ref[1, 2 * b + e:2 * b + e + 1, :]
                ad = ad + dd_ref[2 * b + e:2 * b + e + 1, :]
            halves_r.append(ar)
            halves_i.append(ai)
            halves_d.append(ad)
        dlbr = jnp.concatenate(halves_r, axis=1)
        dlbi = jnp.concatenate(halves_i, axis=1)
        gd_o[...] = jnp.concatenate(halves_d, axis=1)
        _, vjp = jax.vjp(_ssm_param_fn, lr_ref[...], li_ref[...], ldt_ref[...], br_ref[...], bi_ref[...])
        glr, gli, gldt, gbr, gbi = vjp((dlbr, dlbi, dbr_ref[...], dbi_ref[...]))
        glr_o[...] = glr
        gli_o[...] = gli
        gldt_o[...] = gldt
        gbr_o[...] = gbr
        gbi_o[...] = gbi

    row = jax.ShapeDtypeStruct(lr.shape, F32)
    mat = jax.ShapeDtypeStruct(br.shape, F32)
    return pl.pallas_call(
        body, name="ssm_param_bwd", in_specs=[VMEM] * 9, out_specs=[VMEM] * 6,
        out_shape=[row, row, row, mat, mat, jax.ShapeDtypeStruct((1, SSM_W), F32)],
        compiler_params=_params(0),
    )(lr, li, ldt, br, bi, dlam8, dbbr, dbbi, dd8)


def _blockdiag(m):
    _, g, a, b = m.shape
    eye = jnp.eye(g, dtype=m.dtype)
    return jnp.einsum("egab,gk->egakb", m, eye).reshape(2, g * a, g * b)


def _blockdiag_take(t, a, b):
    return jnp.einsum("gagb->gab", t.reshape(GRP // 2, a, GRP // 2, b))


def _mixer_in_fwd(x, sh1, sc1, g_mix, w_in_b):
    bsz, seq, _ = x.shape
    tt = min(seq, TT_MIX)

    def body(x_ref, sh_ref, sc_ref, g_ref, w_ref, u_ref, p_ref):
        xhat, _ = _rms(x_ref[0])
        h = xhat * g_ref[...] * (1.0 + sc_ref[0]) + sh_ref[0]
        z = _dot(h.astype(BF16), w_ref[...])
        u_ref[...] = z[:, :SSM_W]
        p_ref[0] = z[:, SSM_W:]

    row = pl.BlockSpec((1, 1, D), lambda b, t: (b, 0, 0))
    return pl.pallas_call(
        body, name="mixer_in_fwd", grid=(bsz, seq // tt),
        in_specs=[pl.BlockSpec((1, tt, D), lambda b, t: (b, t, 0)), row, row,
                  pl.BlockSpec((1, D), lambda b, t: (0, 0)), VMEM],
        out_specs=[pl.BlockSpec((tt, SSM_W), lambda b, t: (t, b)),
                   pl.BlockSpec((1, tt, POOL_W), lambda b, t: (b, t, 0))],
        out_shape=[jax.ShapeDtypeStruct((seq, bsz * SSM_W), F32), jax.ShapeDtypeStruct((bsz, seq, POOL_W), F32)],
        compiler_params=_params(2),
    )(x, sh1, sc1, g_mix, w_in_b)


def _ssm_project_in(ub, par0, bb_ref, s_re, s_im, row0, tlen, nv):
    for part, sref in ((0, s_re), (1, s_im)):
        for k in range(HALF_ST // 512):
            c0 = part * HALF_ST + k * 512
            a0 = _dot(ub, bb_ref[:, c0:c0 + 512])
            a1 = _dot(ub, bb_ref[:, 2 * HALF_ST + c0:2 * HALF_ST + c0 + 512])
            sref[pl.ds(row0, tlen), :, k * 512:(k + 1) * 512] = jnp.where(par0, a0, a1).reshape(tlen, nv, 512)


def _ssm_fwd(u2r, bb, cc, lam8, d8, tlen, gather=None):
    nv = lam8.shape[1]
    rows = nv * tlen
    n_chunks = u2r.shape[0] // rows

    def body(u_ref, bb_ref, cc_ref, lam_ref, d_ref, y_ref, xc_ref, s_re, s_im, st):
        @pl.when(pl.program_id(0) == 0)
        def _():
            st[...] = jnp.zeros_like(st)

        xc_ref[0] = st[...]
        u = u_ref[...]
        par0 = (lax.broadcasted_iota(jnp.int32, (rows, 1), 0) % 2) == 0
        _ssm_project_in(u.astype(BF16), par0, bb_ref, s_re, s_im, 0, tlen, nv)
        for hb in range(HALF_ST // 512):
            ls = slice(hb * 512, (hb + 1) * 512)
            lr = lam_ref[0, :, ls]
            li = lam_ref[1, :, ls]

            def step(t, carry, ls=ls, lr=lr, li=li):
                xr, xi = carry
                nr = lr * xr - li * xi + s_re[t, :, ls]
                ni = lr * xi + li * xr + s_im[t, :, ls]
                s_re[t, :, ls] = nr
                s_im[t, :, ls] = ni
                return nr, ni

            xr, xi = lax.fori_loop(0, tlen, step, (st[0, :, ls], st[1, :, ls]), unroll=8)
            st[0, :, ls] = xr
            st[1, :, ls] = xi
        xre = s_re[...].reshape(rows, HALF_ST).astype(BF16)
        xim = s_im[...].reshape(rows, HALF_ST).astype(BF16)
        y2 = _dot(xre, cc_ref[0:HALF_ST, :]) + _dot(xim, cc_ref[HALF_ST:, :])
        y = jnp.where(par0, y2[:, :HALF_CH], y2[:, HALF_CH:])
        skip = (u.reshape(tlen, nv, HALF_CH) * d_ref[...][None]).reshape(rows, HALF_CH)
        y_ref[...] = y + skip

    if gather is not None:
        gather = (_gather_plan(gather), gather, [0, (3 * n_chunks) // 4, n_chunks - 1])
    return _fused_call(
        body, name="ssm_fwd", grid=(n_chunks,),
        in_specs=[pl.BlockSpec((rows, HALF_CH), lambda c: (c, 0)), VMEM, VMEM, VMEM, VMEM],
        out_specs=[pl.BlockSpec((rows, HALF_CH), lambda c: (c, 0)),
                   pl.BlockSpec((1, 2, nv, HALF_ST), lambda c: (c, 0, 0, 0))],
        out_shape=[jax.ShapeDtypeStruct(u2r.shape, F32), jax.ShapeDtypeStruct((n_chunks, 2, nv, HALF_ST), F32)],
        scratch_shapes=[pltpu.VMEM((tlen, nv, HALF_ST), F32), pltpu.VMEM((tlen, nv, HALF_ST), F32),
                        pltpu.VMEM((2, nv, HALF_ST), F32)],
        args=(u2r, bb, cc, lam8, d8), comm=gather)


def _pool_forward(ext, pv, pos, wp_ref, bp_ref):
    cur = ext
    zs, zls = [], []
    for gi, w in enumerate(POOL_WINDOWS):
        cur = cur + pltpu.roll(cur, w // 2, 0)
        sw = cur[POOL_HALO:, 0:128]
        z = sw / jnp.minimum(pos, float(w)) - pv[:, gi * 128:(gi + 1) * 128]
        zs.append(z)
        zls.append(_dot(z.astype(BF16), wp_ref[gi]) + bp_ref[:, gi * 128:(gi + 1) * 128])
        if gi + 1 < len(POOL_WINDOWS):
            cur = cur[:, 128:]
    return zs, zls


def _mixer_out_fwd(y2, p, x, gt1, w_glu_b, b_glu, w_pool_b, b_pool, pscale, w_out_b):
    bsz, seq, _ = x.shape
    tt = min(seq, TT_MIX)

    def body(y_ref, p_ref, x_ref, gt_ref, wg_ref, bg_ref, wp_ref, bp_ref, ps_ref, wo_ref, x1_ref, mix_ref, ext):
        ti = pl.program_id(1)

        @pl.when(ti == 0)
        def _():
            ext[0:POOL_HALO, :] = jnp.zeros((POOL_HALO, POOL_W), F32)

        pv = p_ref[0]
        ext[POOL_HALO:, :] = pv
        pos = (ti * tt + lax.broadcasted_iota(jnp.int32, (tt, 1), 0) + 1).astype(F32)
        _, zls = _pool_forward(ext[...], pv, pos, wp_ref, bp_ref)
        ext[0:POOL_HALO, :] = pv[tt - POOL_HALO:, :]
        a = _gelu(y_ref[...])
        gl = _dot(a.astype(BF16), wg_ref[...]) + bg_ref[...]
        y_ssm = gl[:, :SSM_W] * jax.nn.sigmoid(gl[:, SSM_W:])
        y_pool = [zl * ps_ref[:, gi * 128:(gi + 1) * 128] for gi, zl in enumerate(zls)]
        mixcat = jnp.concatenate([y_ssm] + y_pool, axis=1).astype(BF16)
        mix_ref[0] = mixcat
        x1_ref[0] = x_ref[0] + gt_ref[0] * _dot(mixcat, wo_ref[...])

    xt = pl.BlockSpec((1, tt, D), lambda b, t: (b, t, 0))
    return pl.pallas_call(
        body, name="mixer_out_fwd", grid=(bsz, seq // tt),
        in_specs=[pl.BlockSpec((tt, SSM_W), lambda b, t: (t, b)),
                  pl.BlockSpec((1, tt, POOL_W), lambda b, t: (b, t, 0)), xt,
                  pl.BlockSpec((1, 1, D), lambda b, t: (b, 0, 0)), VMEM, VMEM, VMEM, VMEM, VMEM, VMEM],
        out_specs=[xt, xt],
        out_shape=[jax.ShapeDtypeStruct(x.shape, F32), jax.ShapeDtypeStruct(x.shape, BF16)],
        scratch_shapes=[pltpu.VMEM((POOL_HALO + tt, POOL_W), F32)],
        compiler_params=_params(2),
    )(y2, p, x, gt1, w_glu_b, b_glu, w_pool_b, b_pool, pscale, w_out_b)


def _conv_gate(g, ge, wc, bc):
    g1 = pltpu.roll(ge, 1, 0)[CONV_HALO:]
    g2 = pltpu.roll(ge, 2, 0)[CONV_HALO:]
    return wc[2:3] * g + wc[1:2] * g1 + wc[0:1] * g2 + bc, g1, g2


def _ffn_fwd(x1, tgt, sh2, sc2, gt2, g_ffn, w_up_b, w_conv, b_conv, w_down_b, g_fin):
    bsz, seq, _ = x1.shape
    tt = min(seq, TT_FFN)
    n_t = seq // tt
    n_ck = DFF // FF_CH

    def body(x1_ref, tg_ref, sh_ref, sc_ref, gt_ref, gf_ref, wu_ref, wc_ref, bc_ref, wd_ref, gfin_ref,
             h2_ref, v_ref, g_ref, act_ref, ddn_ref, dx2_ref, loss_ref, dgfin_ref, dgt_ref, gext, lacc):
        b = pl.program_id(0)
        ti = pl.program_id(1)

        @pl.when((b == 0) & (ti == 0))
        def _():
            lacc[...] = jnp.zeros_like(lacc)
            dgfin_ref[...] = jnp.zeros_like(dgfin_ref)

        @pl.when(ti == 0)
        def _():
            dgt_ref[...] = jnp.zeros_like(dgt_ref)
            gext[:, 0:CONV_HALO, :] = jnp.zeros((n_ck, CONV_HALO, FF_CH), F32)

        x1v = x1_ref[0]
        xhat, _ = _rms(x1v)
        h2b = (xhat * gf_ref[...] * (1.0 + sc_ref[0]) + sh_ref[0]).astype(BF16)
        h2_ref[0] = h2b
        dn = jnp.zeros((tt, D), F32)
        for ck in range(n_ck):
            c0 = ck * FF_CH
            v = _dot(h2b, wu_ref[:, c0:c0 + FF_CH])
            g = _dot(h2b, wu_ref[:, DFF + c0:DFF + c0 + FF_CH])
            v_ref[0, :, c0:c0 + FF_CH] = v.astype(BF16)
            g_ref[0, :, c0:c0 + FF_CH] = g.astype(BF16)
            gext[ck, CONV_HALO:, :] = g
            gc, _, _ = _conv_gate(g, gext[ck], wc_ref[:, c0:c0 + FF_CH], bc_ref[:, c0:c0 + FF_CH])
            gext[ck, 0:CONV_HALO, :] = g[tt - CONV_HALO:, :]
            actb = (gc * jax.nn.sigmoid(gc) * v).astype(BF16)
            act_ref[0, :, c0:c0 + FF_CH] = actb
            dn = dn + _dot(actb, wd_ref[c0:c0 + FF_CH, :])
        gt = gt_ref[0]
        xh3, r3 = _rms(x1v + gt * dn)
        gfin = gfin_ref[...]
        diff = xh3 * gfin - tg_ref[0]
        lacc[...] += _colsum(diff * diff)
        dy = diff * (1.0 / D)
        dgfin_ref[...] += _colsum(dy * xh3)
        dx2 = _rms_bwd(dy * gfin, xh3, r3)
        dx2_ref[0] = dx2
        dgt_ref[0] += _colsum(dx2 * dn)
        ddn_ref[0] = (gt * dx2).astype(BF16)

        @pl.when((b == bsz - 1) & (ti == n_t - 1))
        def _():
            loss_ref[...] = jnp.full(loss_ref.shape, 0.5 / D * jnp.sum(lacc[...]), F32)

    xt = pl.BlockSpec((1, tt, D), lambda b, t: (b, t, 0))
    ft = pl.BlockSpec((1, tt, DFF), lambda b, t: (b, t, 0))
    row = pl.BlockSpec((1, 1, D), lambda b, t: (b, 0, 0))
    vec = pl.BlockSpec((1, D), lambda b, t: (0, 0))
    ff = jax.ShapeDtypeStruct((bsz, seq, DFF), BF16)
    xs = jax.ShapeDtypeStruct((bsz, seq, D), BF16)
    return pl.pallas_call(
        body, name="ffn_fwd", grid=(bsz, n_t),
        in_specs=[xt, xt, row, row, row, vec, VMEM, VMEM, VMEM, VMEM, vec],
        out_specs=[xt, ft, ft, ft, xt, xt, pl.BlockSpec((1, 128), lambda b, t: (0, 0)), vec, row],
        out_shape=[xs, ff, ff, ff, xs, jax.ShapeDtypeStruct((bsz, seq, D), F32),
                   jax.ShapeDtypeStruct((1, 128), F32), jax.ShapeDtypeStruct((1, D), F32),
                   jax.ShapeDtypeStruct((bsz, 1, D), F32)],
        scratch_shapes=[pltpu.VMEM((n_ck, CONV_HALO + tt, FF_CH), F32), pltpu.VMEM((1, D), F32)],
        compiler_params=_params(2),
    )(x1, tgt, sh2, sc2, gt2, g_ffn, w_up_b, w_conv, b_conv, w_down_b, g_fin)


def _ffn_bwd(ddn, gq, vq, x1, dx2, sh2, sc2, g_ffn, w_conv, b_conv, w_down_b, w_up_b):
    bsz, seq, _ = x1.shape
    tt = min(seq, TT_FFN)
    n_t = seq // tt
    n_ck = DFF // FF_CH
    hb = 16
    ext_rows = tt + CONV_HALO

    def body(ddn_ref, g_ref, gh_ref, v_ref, x1_ref, dx2_ref, sh_ref, sc_ref, gf_ref, wc_ref, bc_ref, wd_ref,
             wu_ref, dup_ref, dx1_ref, dsh_ref, dsc_ref, dgf_ref, dwc_ref, dbc_ref, gext, dext):
        b = pl.program_id(0)
        i = pl.program_id(1)
        tile = n_t - 1 - i

        @pl.when((b == 0) & (i == 0))
        def _():
            dgf_ref[...] = jnp.zeros_like(dgf_ref)
            dwc_ref[...] = jnp.zeros_like(dwc_ref)
            dbc_ref[...] = jnp.zeros_like(dbc_ref)

        @pl.when(i == 0)
        def _():
            dsh_ref[...] = jnp.zeros_like(dsh_ref)
            dsc_ref[...] = jnp.zeros_like(dsc_ref)
            dext[:, tt:, :] = jnp.zeros((n_ck, CONV_HALO, FF_CH), F32)

        ddnv = ddn_ref[0]
        has_halo = (tile > 0).astype(F32)
        dh2 = jnp.zeros((tt, D), F32)
        for ck in range(n_ck):
            c0 = ck * FF_CH
            dact = _dot_nt(ddnv, wd_ref[c0:c0 + FF_CH, :])
            g = g_ref[0, :, c0:c0 + FF_CH].astype(F32)
            v = v_ref[0, :, c0:c0 + FF_CH].astype(F32)
            gext[0:CONV_HALO, :] = gh_ref[0, :, c0:c0 + FF_CH].astype(F32)[hb - CONV_HALO:, :] * has_halo
            gext[CONV_HALO:, :] = g
            wc = wc_ref[:, c0:c0 + FF_CH]
            gc, g1, g2 = _conv_gate(g, gext[...], wc, bc_ref[:, c0:c0 + FF_CH])
            sg = jax.nn.sigmoid(gc)
            dv = dact * (gc * sg)
            dgc = dact * v * (sg * (1.0 + gc * (1.0 - sg)))
            dbc_ref[:, c0:c0 + FF_CH] += _colsum(dgc)
            dwc_ref[0:1, c0:c0 + FF_CH] += _colsum(dgc * g2)
            dwc_ref[1:2, c0:c0 + FF_CH] += _colsum(dgc * g1)
            dwc_ref[2:3, c0:c0 + FF_CH] += _colsum(dgc * g)
            dext[ck, 0:tt, :] = dgc
            de = dext[ck]
            dg = (wc[2:3] * dgc + wc[1:2] * pltpu.roll(de, ext_rows - 1, 0)[0:tt]
                  + wc[0:1] * pltpu.roll(de, ext_rows - 2, 0)[0:tt])
            dext[ck, tt:, :] = dgc[0:CONV_HALO, :]
            dvb = dv.astype(BF16)
            dgb = dg.astype(BF16)
            dup_ref[0, :, c0:c0 + FF_CH] = dvb
            dup_ref[0, :, DFF + c0:DFF + c0 + FF_CH] = dgb
            dh2 = dh2 + _dot_nt(dvb, wu_ref[:, c0:c0 + FF_CH]) + _dot_nt(dgb, wu_ref[:, DFF + c0:DFF + c0 + FF_CH])
        xhat, rstd = _rms(x1_ref[0])
        gf = gf_ref[...]
        dsh_ref[0] += _colsum(dh2)
        dsc_ref[0] += _colsum(dh2 * xhat * gf)
        t = dh2 * (1.0 + sc_ref[0])
        dgf_ref[...] += _colsum(t * xhat)
        dx1_ref[0] = dx2_ref[0] + _rms_bwd(t * gf, xhat, rstd)

    def rev(b, t):
        return (b, n_t - 1 - t, 0)

    def halo(b, t):
        return (b, jnp.maximum((n_t - 1 - t) * (tt // hb) - 1, 0), 0)

    xt = pl.BlockSpec((1, tt, D), rev)
    ft = pl.BlockSpec((1, tt, DFF), rev)
    row = pl.BlockSpec((1, 1, D), lambda b, t: (b, 0, 0))
    vec = pl.BlockSpec((1, D), lambda b, t: (0, 0))
    rows = jax.ShapeDtypeStruct((bsz, 1, D), F32)
    return pl.pallas_call(
        body, name="ffn_bwd", grid=(bsz, n_t),
        in_specs=[xt, ft, pl.BlockSpec((1, hb, DFF), halo), ft, xt, xt, row, row, vec, VMEM, VMEM, VMEM, VMEM],
        out_specs=[pl.BlockSpec((1, tt, 2 * DFF), rev), xt, row, row, vec,
                   pl.BlockSpec((3, DFF), lambda b, t: (0, 0)), pl.BlockSpec((1, DFF), lambda b, t: (0, 0))],
        out_shape=[jax.ShapeDtypeStruct((bsz, seq, 2 * DFF), BF16), jax.ShapeDtypeStruct((bsz, seq, D), F32),
                   rows, rows, jax.ShapeDtypeStruct((1, D), F32), jax.ShapeDtypeStruct((3, DFF), F32),
                   jax.ShapeDtypeStruct((1, DFF), F32)],
        scratch_shapes=[pltpu.VMEM((CONV_HALO + tt, FF_CH), F32), pltpu.VMEM((n_ck, ext_rows, FF_CH), F32)],
        compiler_params=_params(2),
    )(ddn, gq, gq, vq, x1, dx2, sh2, sc2, g_ffn, w_conv, b_conv, w_down_b, w_up_b)


def _wgrad(a, b, bk2, name):
    n, k1 = a.shape
    _, k2 = b.shape
    tt = min(n, TT_MIX)

    def body(a_ref, b_ref, o_ref):
        @pl.when(pl.program_id(1) == 0)
        def _():
            o_ref[...] = jnp.zeros_like(o_ref)

        o_ref[...] += _dot_tn(a_ref[...], b_ref[...])

    return pl.pallas_call(
        body, name=name, grid=(k2 // bk2, n // tt),
        in_specs=[pl.BlockSpec((tt, k1), lambda j, i: (i, 0)), pl.BlockSpec((tt, bk2), lambda j, i: (i, j))],
        out_specs=pl.BlockSpec((k1, bk2), lambda j, i: (0, j)),
        out_shape=jax.ShapeDtypeStruct((k1, k2), F32), compiler_params=_params(2),
    )(a, b)


def _mixer_out_bwd(dx1, mixcat, y2, p, gt1, w_glu_b, b_glu, w_pool_b, b_pool, pscale, w_out_b, pair=None):
    bsz, seq, _ = dx1.shape
    tt = min(seq, TT_MIX)
    n_t = seq // tt
    ext_rows = tt + POOL_HALO

    def body(dx1_ref, mc_ref, y_ref, p_ref, ph_ref, gt_ref, wg_ref, bg_ref, wp_ref, bp_ref, ps_ref, wo_ref,
             dy_ref, dp_ref, dwo_ref, dwg_ref, dbg_ref, dwp_ref, dbp_ref, dps_ref, dgt_ref, ext, qext):
        b = pl.program_id(0)
        i = pl.program_id(1)
        tile = n_t - 1 - i

        @pl.when((b == 0) & (i == 0))
        def _():
            for r in (dwo_ref, dwg_ref, dbg_ref, dwp_ref, dbp_ref, dps_ref):
                r[...] = jnp.zeros_like(r)

        @pl.when(i == 0)
        def _():
            dgt_ref[...] = jnp.zeros_like(dgt_ref)
            qext[tt:, :] = jnp.zeros((POOL_HALO, POOL_W), F32)

        dx1v = dx1_ref[0]
        mc = mc_ref[0]
        dgt_ref[0] += _colsum(dx1v * _dot(mc, wo_ref[...]))
        dmixed = (gt_ref[0] * dx1v).astype(BF16)
        dwo_ref[...] += _dot_tn(mc, dmixed)
        dmc = _dot_nt(dmixed, wo_ref[...])
        pv = p_ref[0]
        ext[0:POOL_HALO, :] = ph_ref[0] * (tile > 0).astype(F32)
        ext[POOL_HALO:, :] = pv
        pos = (tile * tt + lax.broadcasted_iota(jnp.int32, (tt, 1), 0) + 1).astype(F32)
        zs, zls = _pool_forward(ext[...], pv, pos, wp_ref, bp_ref)
        dzs = []
        for gi, w in enumerate(POOL_WINDOWS):
            cs = slice(gi * 128, (gi + 1) * 128)
            dyp = dmc[:, SSM_W + gi * 128:SSM_W + (gi + 1) * 128]
            dps_ref[:, cs] += _colsum(dyp * zls[gi])
            dzl = dyp * ps_ref[:, cs]
            dbp_ref[:, cs] += _colsum(dzl)
            dzlb = dzl.astype(BF16)
            dwp_ref[gi] += _dot_tn(zs[gi].astype(BF16), dzlb)
            dz = _dot_nt(dzlb, wp_ref[gi])
            dzs.append(dz)
            qext[0:tt, cs] = dz / jnp.minimum(pos, float(w))
        cur = qext[...]
        dps = []
        for gi, w in enumerate(POOL_WINDOWS):
            cur = cur + pltpu.roll(cur, ext_rows - w // 2, 0)
            dps.append(cur[0:tt, 0:128] - dzs[gi])
            if gi + 1 < len(POOL_WINDOWS):
                cur = cur[:, 128:]
        qhead = qext[0:POOL_HALO, :]
        qext[tt:, :] = qhead
        dp_ref[0] = jnp.concatenate(dps, axis=1)
        yv = y_ref[...]
        ab = _gelu(yv).astype(BF16)
        gl = _dot(ab, wg_ref[...]) + bg_ref[...]
        val = gl[:, :SSM_W]
        sg = jax.nn.sigmoid(gl[:, SSM_W:])
        dys = dmc[:, :SSM_W]
        dgl = jnp.concatenate([dys * sg, dys * val * sg * (1.0 - sg)], axis=1)
        dbg_ref[...] += _colsum(dgl)
        dglb = dgl.astype(BF16)
        dwg_ref[...] += _dot_tn(ab, dglb)
        dy_ref[...] = _dot_nt(dglb, wg_ref[...]) * _gelu_grad(yv)

    def rev(b, t):
        return (b, n_t - 1 - t, 0)

    def halo(b, t):
        return (b, jnp.maximum((n_t - 1 - t) * (tt // POOL_HALO) - 1, 0), 0)

    xt = pl.BlockSpec((1, tt, D), rev)
    pt = pl.BlockSpec((1, tt, POOL_W), rev)
    yt = pl.BlockSpec((tt, SSM_W), lambda b, t: (n_t - 1 - t, b))

    def whole(shape):
        return pl.BlockSpec(shape, lambda b, t: (0,) * len(shape))

    if pair is not None:
        pair = (_pair_plan(pair), pair, [0, bsz * n_t - 1])
    return _fused_call(
        body, name="mixer_out_bwd", grid=(bsz, n_t),
        in_specs=[xt, xt, yt, pt, pl.BlockSpec((1, POOL_HALO, POOL_W), halo),
                  pl.BlockSpec((1, 1, D), lambda b, t: (b, 0, 0)), VMEM, VMEM, VMEM, VMEM, VMEM, VMEM],
        out_specs=[yt, pt, whole((D, D)), whole((SSM_W, 2 * SSM_W)), whole((1, 2 * SSM_W)),
                   whole((4, 128, 128)), whole((1, POOL_W)), whole((1, POOL_W)),
                   pl.BlockSpec((1, 1, D), lambda b, t: (b, 0, 0))],
        out_shape=[jax.ShapeDtypeStruct(y2.shape, F32), jax.ShapeDtypeStruct(p.shape, F32),
                   jax.ShapeDtypeStruct((D, D), F32), jax.ShapeDtypeStruct((SSM_W, 2 * SSM_W), F32),
                   jax.ShapeDtypeStruct((1, 2 * SSM_W), F32), jax.ShapeDtypeStruct((4, 128, 128), F32),
                   jax.ShapeDtypeStruct((1, POOL_W), F32), jax.ShapeDtypeStruct((1, POOL_W), F32),
                   jax.ShapeDtypeStruct((bsz, 1, D), F32)],
        scratch_shapes=[pltpu.VMEM((POOL_HALO + tt, POOL_W), F32), pltpu.VMEM((ext_rows, POOL_W), F32)],
        args=(dx1, mixcat, y2, p, p, gt1, w_glu_b, b_glu, w_pool_b, b_pool, pscale, w_out_b), comm=pair)


def _ssm_bwd(dy2r, u2r, xc, bb, cc, lam8, d8, tlen, chip=None):
    nv = lam8.shape[1]
    rows = nv * tlen
    n_chunks = u2r.shape[0] // rows

    def body(dy_ref, u_ref, xc_ref, bb_ref, cc_ref, lam_ref, d_ref, du_ref, dcc_ref, dbb_ref, dlam_ref, dd_ref,
             s_re, s_im, g_re, g_im, gst):
        i = pl.program_id(0)

        @pl.when(i == 0)
        def _():
            for r in (gst, dcc_ref, dbb_ref, dlam_ref, dd_ref):
                r[...] = jnp.zeros_like(r)

        u = u_ref[...]
        dy = dy_ref[...]
        par0 = (lax.broadcasted_iota(jnp.int32, (rows, 1), 0) % 2) == 0
        s_re[0] = xc_ref[0, 0]
        s_im[0] = xc_ref[0, 1]
        _ssm_project_in(u.astype(BF16), par0, bb_ref, s_re, s_im, 1, tlen, nv)
        for hb in range(HALF_ST // 512):
            ls = slice(hb * 512, (hb + 1) * 512)
            lr = lam_ref[0, :, ls]
            li = lam_ref[1, :, ls]

            def fstep(t, carry, ls=ls, lr=lr, li=li):
                xr, xi = carry
                nr = lr * xr - li * xi + s_re[t + 1, :, ls]
                ni = lr * xi + li * xr + s_im[t + 1, :, ls]
                s_re[t + 1, :, ls] = nr
                s_im[t + 1, :, ls] = ni
                return nr, ni

            lax.fori_loop(0, tlen, fstep, (s_re[0, :, ls], s_im[0, :, ls]), unroll=8)
        zero = jnp.zeros_like(dy)
        dy2 = jnp.concatenate([jnp.where(par0, dy, zero), jnp.where(par0, zero, dy)], axis=1).astype(BF16)
        u2 = jnp.concatenate([jnp.where(par0, u, zero), jnp.where(par0, zero, u)], axis=1).astype(BF16)
        xre = s_re[pl.ds(1, tlen)].reshape(rows, HALF_ST).astype(BF16)
        xim = s_im[pl.ds(1, tlen)].reshape(rows, HALF_ST).astype(BF16)
        dcc_ref[0:HALF_ST, :] += _dot_tn(xre, dy2)
        dcc_ref[HALF_ST:, :] += _dot_tn(xim, dy2)
        for part, gref in ((0, g_re), (1, g_im)):
            for k in range(HALF_ST // 512):
                r0 = part * HALF_ST + k * 512
                gref[:, :, k * 512:(k + 1) * 512] = _dot_nt(dy2, cc_ref[r0:r0 + 512, :]).reshape(tlen, nv, 512)
        for hb in range(HALF_ST // 512):
            ls = slice(hb * 512, (hb + 1) * 512)
            lr = lam_ref[0, :, ls]
            li = lam_ref[1, :, ls]

            def bstep(k, carry, ls=ls, lr=lr, li=li):
                t = tlen - 1 - k
                gr, gi, ar, ai = carry
                ngr = g_re[t, :, ls] + lr * gr + li * gi
                ngi = g_im[t, :, ls] + lr * gi - li * gr
                g_re[t, :, ls] = ngr
                g_im[t, :, ls] = ngi
                xpr = s_re[t, :, ls]
                xpi = s_im[t, :, ls]
                return ngr, ngi, ar + ngr * xpr + ngi * xpi, ai + ngi * xpr - ngr * xpi

            init = (gst[0, :, ls], gst[1, :, ls], dlam_ref[0, :, ls], dlam_ref[1, :, ls])
            gr, gi, ar, ai = lax.fori_loop(0, tlen, bstep, init, unroll=4)
            gst[0, :, ls] = gr
            gst[1, :, ls] = gi
            dlam_ref[0, :, ls] = ar
            dlam_ref[1, :, ls] = ai
        gre = g_re[...].reshape(rows, HALF_ST).astype(BF16)
        gim = g_im[...].reshape(rows, HALF_ST).astype(BF16)
        du0 = _dot_nt(gre, bb_ref[:, 0:HALF_ST]) + _dot_nt(gim, bb_ref[:, HALF_ST:2 * HALF_ST])
        du1 = _dot_nt(gre, bb_ref[:, 2 * HALF_ST:3 * HALF_ST]) + _dot_nt(gim, bb_ref[:, 3 * HALF_ST:])
        skip = (dy.reshape(tlen, nv, HALF_CH) * d_ref[...][None]).reshape(rows, HALF_CH)
        du_ref[...] = jnp.where(par0, du0, du1) + skip
        dbb_ref[:, 0:HALF_ST] += _dot_tn(u2, gre)
        dbb_ref[:, HALF_ST:] += _dot_tn(u2, gim)
        dd_ref[...] += jnp.sum((dy * u).reshape(tlen, nv, HALF_CH), axis=0)

        @pl.when(i == n_chunks - 1)
        def _():
            dcc_ref[HALF_ST:, :] = -dcc_ref[HALF_ST:, :]

    def rev(c):
        return (n_chunks - 1 - c, 0)

    def whole(shape):
        return pl.BlockSpec(shape, lambda c: (0,) * len(shape))

    blk = pl.BlockSpec((rows, HALF_CH), rev)
    if chip is not None:
        chip = (_chip_plan(chip), chip, [0, n_chunks - 1])
    return _fused_call(
        body, name="ssm_bwd", grid=(n_chunks,),
        in_specs=[blk, blk, pl.BlockSpec((1, 2, nv, HALF_ST), lambda c: (n_chunks - 1 - c, 0, 0, 0)),
                  VMEM, VMEM, VMEM, VMEM],
        out_specs=[blk, whole((2 * HALF_ST, SSM_W)), whole((SSM_W, 2 * HALF_ST)), whole((2, nv, HALF_ST)),
                   whole((nv, HALF_CH))],
        out_shape=[jax.ShapeDtypeStruct(u2r.shape, F32), jax.ShapeDtypeStruct((2 * HALF_ST, SSM_W), F32),
                   jax.ShapeDtypeStruct((SSM_W, 2 * HALF_ST), F32), jax.ShapeDtypeStruct((2, nv, HALF_ST), F32),
                   jax.ShapeDtypeStruct((nv, HALF_CH), F32)],
        scratch_shapes=[pltpu.VMEM((tlen + 1, nv, HALF_ST), F32), pltpu.VMEM((tlen + 1, nv, HALF_ST), F32),
                        pltpu.VMEM((tlen, nv, HALF_ST), F32), pltpu.VMEM((tlen, nv, HALF_ST), F32),
                        pltpu.VMEM((2, nv, HALF_ST), F32)],
        args=(dy2r, u2r, xc, bb, cc, lam8, d8), comm=chip)


def _mixer_in_bwd(du2, dp, x, dx1, sh1, sc1, g_mix, w_in_b):
    bsz, seq, _ = x.shape
    tt = min(seq, TT_MIX)

    def body(du_ref, dp_ref, x_ref, dx1_ref, sh_ref, sc_ref, g_ref, w_ref,
             dx_ref, dw_ref, dsh_ref, dsc_ref, dg_ref):
        b = pl.program_id(0)
        ti = pl.program_id(1)

        @pl.when((b == 0) & (ti == 0))
        def _():
            dw_ref[...] = jnp.zeros_like(dw_ref)
            dg_ref[...] = jnp.zeros_like(dg_ref)

        @pl.when(ti == 0)
        def _():
            dsh_ref[...] = jnp.zeros_like(dsh_ref)
            dsc_ref[...] = jnp.zeros_like(dsc_ref)

        dz = jnp.concatenate([du_ref[...], dp_ref[0]], axis=1).astype(BF16)
        xhat, rstd = _rms(x_ref[0])
        g = g_ref[...]
        sc = sc_ref[0]
        a = xhat * g
        h = (a * (1.0 + sc) + sh_ref[0]).astype(BF16)
        dw_ref[...] += _dot_tn(h, dz)
        dh = _dot_nt(dz, w_ref[...])
        dsh_ref[0] += _colsum(dh)
        dsc_ref[0] += _colsum(dh * a)
        t = dh * (1.0 + sc)
        dg_ref[...] += _colsum(t * xhat)
        dx_ref[0] = dx1_ref[0] + _rms_bwd(t * g, xhat, rstd)

    xt = pl.BlockSpec((1, tt, D), lambda b, t: (b, t, 0))
    row = pl.BlockSpec((1, 1, D), lambda b, t: (b, 0, 0))
    vec = pl.BlockSpec((1, D), lambda b, t: (0, 0))
    rows = jax.ShapeDtypeStruct((bsz, 1, D), F32)
    return pl.pallas_call(
        body, name="mixer_in_bwd", grid=(bsz, seq // tt),
        in_specs=[pl.BlockSpec((tt, SSM_W), lambda b, t: (t, b)),
                  pl.BlockSpec((1, tt, POOL_W), lambda b, t: (b, t, 0)), xt, xt, row, row, vec, VMEM],
        out_specs=[xt, pl.BlockSpec((D, D), lambda b, t: (0, 0)), row, row, vec],
        out_shape=[jax.ShapeDtypeStruct(x.shape, F32), jax.ShapeDtypeStruct((D, D), F32), rows, rows,
                   jax.ShapeDtypeStruct((1, D), F32)],
        compiler_params=_params(2),
    )(du2, dp, x, dx1, sh1, sc1, g_mix, w_in_b)


def _rows128(a):
    flat = a.reshape(-1)
    pad = (-flat.shape[0]) % 128
    if pad:
        flat = jnp.concatenate([flat, jnp.zeros((pad,), flat.dtype)])
    return flat.reshape(-1, 128)


def _pack(parts):
    rows = [_rows128(a) for a in parts]
    total = sum(r.shape[0] for r in rows)
    pad = (-total) % 128
    if pad:
        rows.append(jnp.zeros((pad, 128), rows[0].dtype))
    return jnp.concatenate(rows, axis=0)


def _unpack(packed, shapes):
    out, r0 = [], 0
    for shp in shapes:
        n = math.prod(shp)
        nr = -(-n // 128)
        out.append(packed[r0:r0 + nr].reshape(-1)[:n].reshape(shp))
        r0 += nr
    return out


def kernel(x, c, w_ada, b_ada, g_norm_mix, w_in, ssm_lam_re, ssm_lam_im, ssm_log_dt, ssm_b_re, ssm_b_im, ssm_c_re, ssm_c_im, ssm_d, w_glu, b_glu, w_pool, b_pool, pool_scale, w_out, g_norm_ffn, w_up, w_conv, b_conv, w_down, g_norm_final, loss_target, m_w_ada, m_b_ada, m_g_norm_mix, m_w_in, m_ssm_lam_re, m_ssm_lam_im, m_ssm_log_dt, m_ssm_b_re, m_ssm_b_im, m_ssm_c_re, m_ssm_c_im, m_ssm_d, m_w_glu, m_b_glu, m_w_pool, m_b_pool, m_pool_scale, m_w_out, m_g_norm_ffn, m_w_up, m_w_conv, m_b_conv, m_w_down, m_g_norm_final, v_w_ada, v_b_ada, v_g_norm_mix, v_w_in, v_ssm_lam_re, v_ssm_lam_im, v_ssm_log_dt, v_ssm_b_re, v_ssm_b_im, v_ssm_c_re, v_ssm_c_im, v_ssm_d, v_w_glu, v_b_glu, v_w_pool, v_b_pool, v_pool_scale, v_w_out, v_g_norm_ffn, v_w_up, v_w_conv, v_b_conv, v_w_down, v_g_norm_final):
    bsz, seq, _ = x.shape
    assert 2 * bsz == 8 and seq % 128 == 0
    px, py, pc = _my_place()
    me = 4 * px + 2 * py + pc
    place = jnp.stack([pc, 2 * px + py]).astype(jnp.int32)
    ncol = ADA_COLS

    cpad = jnp.zeros((16, D), F32).at[0:bsz].set(c).at[8:11, 0:352].set(w_conv[0])
    cg, c_all, mod8, (g_in, g_glu, g_out) = _ada_fwd(
        cpad, w_ada[0], b_ada.reshape(N_DEV, 1, ncol),
        [w_in[0].astype(BF16), w_glu[0].astype(BF16), w_out[0].astype(BF16)])
    w_conv_f = cg[:, 8:11, 0:352].transpose(1, 0, 2).reshape(3, DFF)
    w_in_b = g_in.reshape(D, D)
    w_glu_b = g_glu.transpose(1, 0, 2).reshape(SSM_W, 2 * SSM_W)
    w_out_b = g_out.reshape(D, D)
    sh1, sc1, gt1, sh2, sc2, gt2 = [mod8[0:bsz, k * D:(k + 1) * D].reshape(bsz, 1, D) for k in range(N_MOD)]

    lam_r = ssm_lam_re[0].reshape(1, GRP * NST)
    lam_i = ssm_lam_im[0].reshape(1, GRP * NST)
    ldt = jnp.repeat(ssm_log_dt[0], NST).reshape(1, GRP * NST)
    b_r = ssm_b_re[0].transpose(2, 0, 1).reshape(GCH, GRP * NST)
    b_i = ssm_b_im[0].transpose(2, 0, 1).reshape(GCH, GRP * NST)
    lbr, lbi, bbr, bbi = _ssm_prep(lam_r, lam_i, ldt, b_r, b_i)
    lam8 = jnp.stack([jnp.tile(lbr.reshape(2, HALF_ST), (bsz, 1)), jnp.tile(lbi.reshape(2, HALF_ST), (bsz, 1))])
    bd_r = _blockdiag(bbr.reshape(GCH, 2, GRP // 2, NST).transpose(1, 2, 0, 3))
    bd_i = _blockdiag(bbi.reshape(GCH, 2, GRP // 2, NST).transpose(1, 2, 0, 3))
    bb = jnp.concatenate([bd_r[0], bd_i[0], bd_r[1], bd_i[1]], axis=1).astype(BF16)
    cd_r = _blockdiag(ssm_c_re[0].reshape(2, GRP // 2, GCH, NST).transpose(0, 1, 3, 2))
    cd_i = _blockdiag(ssm_c_im[0].reshape(2, GRP // 2, GCH, NST).transpose(0, 1, 3, 2))
    cc = jnp.concatenate([jnp.concatenate([cd_r[0], cd_r[1]], axis=1),
                          jnp.concatenate([-cd_i[0], -cd_i[1]], axis=1)], axis=0).astype(BF16)
    d8 = jnp.tile(ssm_d[0].reshape(2, HALF_CH), (bsz, 1))

    tlen = min(seq, T_SSM)
    u2, p = _mixer_in_fwd(x, sh1, sc1, g_norm_mix, w_in_b)
    u2r = u2.reshape(seq * 2 * bsz, HALF_CH)
    (y2r, xc), (g_up, g_down) = _ssm_fwd(u2r, bb, cc, lam8, d8, tlen,
                                          gather=[w_up[0].astype(BF16), w_down[0].astype(BF16)])
    w_up_b = g_up.transpose(1, 0, 2).reshape(D, 2 * DFF)
    w_down_b = g_down.reshape(DFF, D)
    y2 = y2r.reshape(seq, bsz * SSM_W)
    w_pool_b = w_pool[0].astype(BF16)
    bp = b_pool[0].reshape(1, POOL_W)
    x1, mixcat = _mixer_out_fwd(y2, p, x, gt1, w_glu_b, b_glu, w_pool_b, bp, pool_scale, w_out_b)
    h2, vq, gq, act, ddn, dx2, loss_l, dg_fin, dgt2 = _ffn_fwd(
        x1, loss_target, sh2, sc2, gt2, g_norm_ffn, w_up_b, w_conv_f, b_conv, w_down_b, g_norm_final.reshape(1, D))
    loss = lax.psum(loss_l[0, 0], ("x", "y", "c"))

    dup, dx1, dsh2, dsc2, dg_ffn, dw_conv, db_conv = _ffn_bwd(
        ddn, gq, vq, x1, dx2, sh2, sc2, g_norm_ffn, w_conv_f, b_conv, w_down_b, w_up_b)
    ntok = bsz * seq
    dw_up = _wgrad(h2.reshape(ntok, D), dup.reshape(ntok, 2 * DFF), FF_CH, "wgrad_up")
    dw_down = _wgrad(act.reshape(ntok, DFF), ddn.reshape(ntok, D), 512, "wgrad_down")
    g42_up = dw_up.reshape(D, N_DEV, 704).transpose(1, 0, 2).reshape(4, 2, D, 704)
    g42_down = dw_down.reshape(4, 2, 352, D)
    (dy2, dp, dw_out, dw_glu, db_glu, dw_pool, db_pool, dpscale, dgt1), (ra_up, ra_down) = _mixer_out_bwd(
        dx1, mixcat, y2, p, gt1, w_glu_b, b_glu, w_pool_b, bp, pool_scale, w_out_b, pair=[g42_up, g42_down])
    own_up, s_up = _pair_sum(g42_up, ra_up, place, "pair_sum_up")
    own_down, s_down = _pair_sum(g42_down, ra_down, place, "pair_sum_down")
    (du2r, dcc, dbb, dlam8, dd8), (rc_up, rc_down) = _ssm_bwd(
        dy2.reshape(u2r.shape), u2r, xc, bb, cc, lam8, d8, tlen, chip=[s_up, s_down])
    big_up = _final_sum_adamw(own_up, rc_up, w_up[0], m_w_up[0], v_w_up[0], "final_adamw_up")
    big_down = _final_sum_adamw(own_down, rc_down, w_down[0], m_w_down[0], v_w_down[0], "final_adamw_down")
    grad_x, dw_in, dsh1, dsc1, dg_mix = _mixer_in_bwd(
        du2r.reshape(u2.shape), dp, x, dx1, sh1, sc1, g_norm_mix, w_in_b)

    def take_c(t):
        return _blockdiag_take(t, NST, GCH).transpose(0, 2, 1)

    dc_re = jnp.concatenate([take_c(dcc[0:HALF_ST, e * HALF_CH:(e + 1) * HALF_CH]) for e in range(2)], axis=0)
    dc_im = jnp.concatenate([take_c(dcc[HALF_ST:, e * HALF_CH:(e + 1) * HALF_CH]) for e in range(2)], axis=0)

    def take_b(t):
        return _blockdiag_take(t, GCH, NST).transpose(1, 0, 2)

    dbbr = jnp.concatenate([take_b(dbb[e * HALF_CH:(e + 1) * HALF_CH, 0:HALF_ST]) for e in range(2)], axis=1)
    dbbi = jnp.concatenate([take_b(dbb[e * HALF_CH:(e + 1) * HALF_CH, HALF_ST:]) for e in range(2)], axis=1)
    glr, gli, gldt, gbr, gbi, gd = _ssm_param_bwd(
        lam_r, lam_i, ldt, b_r, b_i, dlam8, dbbr.reshape(GCH, GRP * NST), dbbi.reshape(GCH, GRP * NST), dd8)
    g_log_dt = jnp.sum(gldt.reshape(GRP, NST), axis=1)
    g_b_re = gbr.reshape(GCH, GRP, NST).transpose(1, 2, 0)
    g_b_im = gbi.reshape(GCH, GRP, NST).transpose(1, 2, 0)

    dmod = jnp.concatenate([t.reshape(bsz, D) for t in (dsh1, dsc1, dgt1, dsh2, dsc2, dgt2)], axis=1)
    dmod_blk = jnp.zeros((N_DEV, 8, ncol), F32).at[:, 0:bsz].set(dmod.reshape(bsz, N_DEV, ncol).transpose(1, 0, 2))
    ada = _ada_bwd(dmod_blk.reshape(ADA_ROWS, ncol), c_all, w_ada[0], m_w_ada[0], v_w_ada[0],
                   b_ada.reshape(N_DEV, 1, ncol), m_b_ada.reshape(N_DEV, 1, ncol), v_b_ada.reshape(N_DEV, 1, ncol))
    g_w_ada, d_w_ada, nm_w_ada, nv_w_ada = [t[None] for t in ada[0:4]]
    g_b_ada, d_b_ada, nm_b_ada, nv_b_ada = [t.reshape(1, N_MOD * D) for t in ada[4:8]]

    g42s = [dw_in.reshape(4, 2, 128, D),
            dw_glu.reshape(SSM_W, N_DEV, 128).transpose(1, 0, 2).reshape(4, 2, SSM_W, 128),
            dw_out.reshape(4, 2, 128, D)]
    ras = _comm_call(_pair_plan(g42s), g42s, "mixer_grad_pair_exchange")
    sums = [_pair_sum(g, r, place, "pair_sum_" + nm) for g, r, nm in zip(g42s, ras, ("in", "glu", "out"))]
    s4s = [s for _, s in sums]
    rcs = _comm_call(_chip_plan(s4s), s4s, "mixer_grad_chip_exchange")
    big_in = _final_sum_adamw(sums[0][0], rcs[0], w_in[0], m_w_in[0], v_w_in[0], "final_adamw_in")
    big_glu = _final_sum_adamw(sums[1][0], rcs[1], w_glu[0], m_w_glu[0], v_w_glu[0], "final_adamw_glu")
    big_out = _final_sum_adamw(sums[2][0], rcs[2], w_out[0], m_w_out[0], v_w_out[0], "final_adamw_out")
    (g_w_in, d_w_in, nm_w_in, nv_w_in), (g_w_glu, d_w_glu, nm_w_glu, nv_w_glu), (g_w_out, d_w_out, nm_w_out, nv_w_out), \
        (g_w_up, d_w_up, nm_w_up, nv_w_up), (g_w_down, d_w_down, nm_w_down, nv_w_down) = [
            [t[None] for t in b] for b in (big_in, big_glu, big_out, big_up, big_down)]

    small_g = [dg_mix, glr, gli, g_log_dt, g_b_re, g_b_im, dc_re, dc_im, gd, db_glu, dw_pool, db_pool, dpscale,
               dg_ffn, db_conv, dg_fin, dw_conv]
    small_w = [g_norm_mix, ssm_lam_re, ssm_lam_im, ssm_log_dt, ssm_b_re, ssm_b_im, ssm_c_re, ssm_c_im, ssm_d, b_glu,
               w_pool, b_pool, pool_scale, g_norm_ffn, b_conv, g_norm_final]
    small_m = [m_g_norm_mix, m_ssm_lam_re, m_ssm_lam_im, m_ssm_log_dt, m_ssm_b_re, m_ssm_b_im, m_ssm_c_re,
               m_ssm_c_im, m_ssm_d, m_b_glu, m_w_pool, m_b_pool, m_pool_scale, m_g_norm_ffn, m_b_conv, m_g_norm_final]
    small_v = [v_g_norm_mix, v_ssm_lam_re, v_ssm_lam_im, v_ssm_log_dt, v_ssm_b_re, v_ssm_b_im, v_ssm_c_re,
               v_ssm_c_im, v_ssm_d, v_b_glu, v_w_pool, v_b_pool, v_pool_scale, v_g_norm_ffn, v_b_conv, v_g_norm_final]
    shapes = [w.shape for w in small_w]
    packed = _pack(small_g)
    parts = _comm_call(_gather_plan([packed]), [packed], "gather_small_grads")[0]
    zero_conv = jnp.zeros((3, DFF), F32)
    sg_, sd_, sm_, sv_ = _sum8_adamw(parts, _pack(small_w + [zero_conv]), _pack(small_m + [zero_conv]),
                                     _pack(small_v + [zero_conv]))
    sg_l = _unpack(sg_, shapes + [(3, DFF)])
    sd_l = _unpack(sd_, shapes)
    sm_l = _unpack(sm_, shapes)
    sv_l = _unpack(sv_, shapes)
    g_conv_full = sg_l[-1]
    g_w_conv = lax.dynamic_slice_in_dim(g_conv_full, 352 * me, 352, axis=1)
    cv = _adamw_plain(_rows128(g_w_conv), _rows128(w_conv[0]), _rows128(m_w_conv[0]), _rows128(v_w_conv[0]))
    d_w_conv, nm_w_conv, nv_w_conv = [t.reshape(-1)[:3 * 352].reshape(1, 3, 352) for t in cv]
    g_w_conv = g_w_conv[None]

    def order(ada_w, ada_b, small, w_in_, w_glu_, w_out_, w_up_, w_conv_, w_down_):
        (mix, lre, lim, ldt_, bre, bim, cre, cim, dd, bglu, wpool, bpool, pscale, gffn, bconv, gfin) = small
        return [ada_w, ada_b, mix, w_in_, lre, lim, ldt_, bre, bim, cre, cim, dd, w_glu_, bglu, wpool, bpool, pscale,
                w_out_, gffn, w_up_, w_conv_, bconv, w_down_, gfin]

    grads = order(g_w_ada, g_b_ada, sg_l[:-1], g_w_in, g_w_glu, g_w_out, g_w_up, g_w_conv, g_w_down)
    deltas = order(d_w_ada, d_b_ada, sd_l, d_w_in, d_w_glu, d_w_out, d_w_up, d_w_conv, d_w_down)
    new_m = order(nm_w_ada, nm_b_ada, sm_l, nm_w_in, nm_w_glu, nm_w_out, nm_w_up, nm_w_conv, nm_w_down)
    new_v = order(nv_w_ada, nv_b_ada, sv_l, nv_w_in, nv_w_glu, nv_w_out, nv_w_up, nv_w_conv, nv_w_down)
    return (loss, grad_x, *grads, *deltas, *new_m, *new_v)
```

```python
import functools
import math

import jax
import jax.numpy as jnp
from jax import lax
from jax.experimental import pallas as pl
from jax.experimental.pallas import tpu as pltpu

F32 = jnp.float32
BF16 = jnp.bfloat16

D = 1024
SSM_W = 512
POOL_W = 512
GRP = 32
GCH = 16
NST = 64
HALF_ST = GRP * NST // 2
HALF_CH = SSM_W // 2
DFF = 2816
FF_CH = 1408
N_MOD = 6
N_DEV = 8
EPS = 1e-6
POOL_WINDOWS = (2, 4, 8, 16)
POOL_HALO = 16
CONV_HALO = 8
GELU_C = math.sqrt(2.0 / math.pi)
GELU_A = 0.044715

ADAM_LR = 0.001
ADAM_B1 = 0.9
ADAM_B2 = 0.999
ADAM_EPS = 1e-08
ADAM_WD = 0.01
ADAM_STEP = 10

VMEM_LIMIT = 56 * 1024 * 1024
TT_MIX = 512
TT_FFN = 256
T_SSM = 128
MESH = pl.DeviceIdType.MESH
NT = (((1,), (1,)), ((), ()))
TN = (((0,), (0,)), ((), ()))
ANY = pl.BlockSpec(memory_space=pl.ANY)
VMEM = pl.BlockSpec(memory_space=pltpu.VMEM)


def _params(n_grid, vmem=VMEM_LIMIT):
    return pltpu.CompilerParams(dimension_semantics=("arbitrary",) * n_grid, vmem_limit_bytes=vmem)


def _dot(a, b):
    return jnp.dot(a, b, preferred_element_type=F32)


def _dot_nt(a, b):
    return lax.dot_general(a, b, NT, preferred_element_type=F32)


def _dot_tn(a, b):
    return lax.dot_general(a, b, TN, preferred_element_type=F32)


def _colsum(a):
    return jnp.sum(a, axis=0, keepdims=True)


def _rms(x):
    rstd = lax.rsqrt(jnp.mean(x * x, axis=-1, keepdims=True) + EPS)
    return x * rstd, rstd


def _rms_bwd(dxhat, xhat, rstd):
    return rstd * (dxhat - xhat * jnp.mean(dxhat * xhat, axis=-1, keepdims=True))


def _gelu(x):
    return 0.5 * x * (1.0 + jnp.tanh(GELU_C * (x + GELU_A * x * x * x)))


def _gelu_grad(x):
    x2 = x * x
    th = jnp.tanh(GELU_C * (x + GELU_A * x * x2))
    return 0.5 * (1.0 + th) + 0.5 * x * (1.0 - th * th) * GELU_C * (1.0 + 3.0 * GELU_A * x2)


def _adamw(w, g, m, v):
    m = ADAM_B1 * m + (1.0 - ADAM_B1) * g
    v = ADAM_B2 * v + (1.0 - ADAM_B2) * (g * g)
    m_hat = m / (1.0 - ADAM_B1 ** ADAM_STEP)
    v_hat = v / (1.0 - ADAM_B2 ** ADAM_STEP)
    delta = -ADAM_LR * (m_hat / (jnp.sqrt(v_hat) + ADAM_EPS) + ADAM_WD * w)
    return delta, m, v


def _my_place():
    return lax.axis_index("x"), lax.axis_index("y"), lax.axis_index("c")


def _gather_plan(shards):
    n = len(shards)
    out_shape = [jax.ShapeDtypeStruct((N_DEV,) + tuple(s.shape), s.dtype) for s in shards]
    scratch = [pltpu.SemaphoreType.DMA((n, 7)), pltpu.SemaphoreType.DMA((n, 7)), pltpu.SemaphoreType.DMA((n,))]

    def stages(x_refs, out_refs, sems):
        send_sems, recv_sems, local_sems = sems
        x, y, c = _my_place()
        me, sibling = (x, y, c), (x, y, 1 - c)
        chips = [(1 - x, y), (x, 1 - y), (1 - x, 1 - y)]

        def copy(i, k, block, to, own=False):
            px, py, pc = block
            dst = out_refs[i].at[4 * px + 2 * py + pc]
            return pltpu.make_async_remote_copy(
                src_ref=x_refs[i] if own else dst, dst_ref=dst, send_sem=send_sems.at[i, k],
                recv_sem=recv_sems.at[i, k], device_id=to, device_id_type=MESH)

        def mine(i):
            return pltpu.make_async_copy(x_refs[i], out_refs[i].at[4 * x + 2 * y + c], local_sems.at[i])

        def start():
            for i in range(n):
                mine(i).start()
                copy(i, 0, me, sibling, own=True).start()
                for j, chip in enumerate(chips):
                    copy(i, 1 + j, me, (*chip, c), own=True).start()

        def forward():
            for i in range(n):
                for j, chip in enumerate(chips):
                    copy(i, 1 + j, (*chip, c), me).wait_recv()
                    copy(i, 4 + j, (*chip, c), sibling).start()

        def finish():
            for i in range(n):
                copy(i, 0, sibling, me).wait_recv()
                copy(i, 0, me, sibling, own=True).wait_send()
                for j, chip in enumerate(chips):
                    copy(i, 4 + j, (*chip, 1 - c), me).wait_recv()
                    copy(i, 1 + j, me, (*chip, c), own=True).wait_send()
                    copy(i, 4 + j, (*chip, c), sibling).wait_send()
                mine(i).wait()

        return [start, forward, finish]

    return n, out_shape, scratch, stages


def _pair_plan(g42s):
    n = len(g42s)
    out_shape = [jax.ShapeDtypeStruct((4,) + tuple(g.shape[2:]), g.dtype) for g in g42s]
    scratch = [pltpu.SemaphoreType.DMA((n,)), pltpu.SemaphoreType.DMA((n,))]

    def stages(g_refs, out_refs, sems):
        send_sems, recv_sems = sems
        x, y, c = _my_place()

        def copy(i):
            return pltpu.make_async_remote_copy(
                src_ref=g_refs[i].at[:, 1 - c], dst_ref=out_refs[i], send_sem=send_sems.at[i],
                recv_sem=recv_sems.at[i], device_id=(x, y, 1 - c), device_id_type=MESH)

        def start():
            for i in range(n):
                copy(i).start()

        def finish():
            for i in range(n):
                copy(i).wait()

        return [start, finish]

    return n, out_shape, scratch, stages


def _chip_plan(s4s):
    n = len(s4s)
    out_shape = [jax.ShapeDtypeStruct((3,) + tuple(s.shape[1:]), s.dtype) for s in s4s]
    scratch = [pltpu.SemaphoreType.DMA((n, 3)), pltpu.SemaphoreType.DMA((n, 3))]

    def stages(s_refs, out_refs, sems):
        send_sems, recv_sems = sems
        x, y, c = _my_place()

        def copy(i, d):
            px, py = x ^ (d >> 1), y ^ (d & 1)
            return pltpu.make_async_remote_copy(
                src_ref=s_refs[i].at[2 * px + py], dst_ref=out_refs[i].at[d - 1], send_sem=send_sems.at[i, d - 1],
                recv_sem=recv_sems.at[i, d - 1], device_id=(px, py, c), device_id_type=MESH)

        def start():
            for i in range(n):
                for d in (1, 2, 3):
                    copy(i, d).start()

        def finish():
            for i in range(n):
                for d in (1, 2, 3):
                    copy(i, d).wait()

        return [start, finish]

    return n, out_shape, scratch, stages


def _comm_call(plan, arrays, name):
    n, out_shape, scratch, stages = plan

    def body(*refs):
        for stage in stages(refs[:n], refs[n:2 * n], refs[2 * n:]):
            stage()

    return pl.pallas_call(
        body, name=name, out_shape=out_shape, in_specs=[ANY] * n, out_specs=[ANY] * n, scratch_shapes=scratch,
    )(*arrays)


def _fused_call(body, *, name, grid, in_specs, out_specs, out_shape, scratch_shapes, args, comm=None):
    if not comm:
        out = pl.pallas_call(body, name=name, grid=grid, in_specs=in_specs, out_specs=out_specs, out_shape=out_shape,
                             scratch_shapes=scratch_shapes, compiler_params=_params(len(grid)))(*args)
        return out, []
    counts = [plan[0] for plan, _, _ in comm]
    n = sum(counts)
    n_in, n_out, n_scr = len(in_specs), len(out_specs), len(scratch_shapes)

    def fused(*refs):
        ins, refs = refs[:n_in], refs[n_in:]
        c_ins, refs = refs[:n], refs[n:]
        outs, refs = refs[:n_out], refs[n_out:]
        c_outs, refs = refs[:n], refs[n:]
        scr, c_scr = refs[:n_scr], refs[n_scr:]
        step = pl.program_id(0)
        for k in range(1, len(grid)):
            step = step * grid[k] + pl.program_id(k)
        todo, a0, s0 = [], 0, 0
        for (cnt, _, plan_scratch, stages), _, steps in comm:
            sems = c_scr[s0:s0 + len(plan_scratch)]
            todo += list(zip(stages(c_ins[a0:a0 + cnt], c_outs[a0:a0 + cnt], sems), steps))
            a0 += cnt
            s0 += len(plan_scratch)
        for stage, at in todo:
            if at == 0:
                pl.when(step == 0)(stage)
        body(*ins, *outs, *scr)
        for stage, at in todo:
            if at != 0:
                pl.when(step == at)(stage)

    c_shape = [s for plan, _, _ in comm for s in plan[1]]
    c_scratch = [s for plan, _, _ in comm for s in plan[2]]
    arrays = [a for _, arrs, _ in comm for a in arrs]
    out = pl.pallas_call(
        fused, name=name, grid=grid, in_specs=list(in_specs) + [ANY] * n, out_specs=list(out_specs) + [ANY] * n,
        out_shape=list(out_shape) + c_shape, scratch_shapes=list(scratch_shapes) + c_scratch,
        compiler_params=_params(len(grid)))(*args, *arrays)
    outs, c_outs, split, a0 = out[:n_out], out[n_out:], [], 0
    for cnt in counts:
        split.append(c_outs[a0:a0 + cnt])
        a0 += cnt
    return outs, split


def _schedule(comm, n_steps):
    out = []
    for make_plan, arrays in comm or []:
        steps = [0, (3 * n_steps) // 4, n_steps - 1] if make_plan is _gather_plan else [0, n_steps - 1]
        out.append((make_plan(arrays), arrays, steps))
    return out


def _row_tile(r):
    for t in (128, 64, 32, 16, 8):
        if r % t == 0:
            return t
    return r


def _pair_sum(g42, recv, place, name):
    _, _, r, cdim = g42.shape
    tr = _row_tile(r)

    def body(pl_ref, g_ref, r_ref, own_ref, s_ref):
        s_ref[...] = (g_ref[:, 0] + r_ref[...]).astype(BF16)
        q = pl_ref[1]
        own_ref[...] = g_ref[q, 0] + r_ref[q]

    return pl.pallas_call(
        body, name=name,
        grid_spec=pltpu.PrefetchScalarGridSpec(
            num_scalar_prefetch=1, grid=(r // tr,),
            in_specs=[pl.BlockSpec((4, 1, tr, cdim), lambda i, p: (0, p[0], i, 0)),
                      pl.BlockSpec((4, tr, cdim), lambda i, p: (0, i, 0))],
            out_specs=[pl.BlockSpec((tr, cdim), lambda i, p: (i, 0)),
                       pl.BlockSpec((4, tr, cdim), lambda i, p: (0, i, 0))]),
        out_shape=[jax.ShapeDtypeStruct((r, cdim), F32), jax.ShapeDtypeStruct((4, r, cdim), BF16)],
        compiler_params=_params(1),
    )(place, g42, recv)


def _final_sum_adamw(own, recv3, w, m, v, name):
    r, cdim = w.shape
    tr = _row_tile(r)

    def body(s_ref, r_ref, w_ref, m_ref, v_ref, g_out, d_out, m_out, v_out):
        g = s_ref[...] + r_ref[0].astype(F32) + r_ref[1].astype(F32) + r_ref[2].astype(F32)
        d, mn, vn = _adamw(w_ref[...], g, m_ref[...], v_ref[...])
        g_out[...] = g
        d_out[...] = d
        m_out[...] = mn
        v_out[...] = vn

    blk = pl.BlockSpec((tr, cdim), lambda i: (i, 0))
    shp = jax.ShapeDtypeStruct((r, cdim), F32)
    return pl.pallas_call(
        body, name=name, grid=(r // tr,),
        in_specs=[blk, pl.BlockSpec((3, tr, cdim), lambda i: (0, i, 0)), blk, blk, blk],
        out_specs=[blk, blk, blk, blk], out_shape=[shp, shp, shp, shp], compiler_params=_params(1),
    )(own, recv3, w, m, v)


def _sum8_adamw(parts, w, m, v):
    r, cdim = w.shape
    tr = _row_tile(r)

    def body(p_ref, w_ref, m_ref, v_ref, g_out, d_out, m_out, v_out):
        g = p_ref[0]
        for k in range(1, N_DEV):
            g = g + p_ref[k]
        d, mn, vn = _adamw(w_ref[...], g, m_ref[...], v_ref[...])
        g_out[...] = g
        d_out[...] = d
        m_out[...] = mn
        v_out[...] = vn

    blk = pl.BlockSpec((tr, cdim), lambda i: (i, 0))
    shp = jax.ShapeDtypeStruct((r, cdim), F32)
    return pl.pallas_call(
        body, name="small_sum_adamw", grid=(r // tr,),
        in_specs=[pl.BlockSpec((N_DEV, tr, cdim), lambda i: (0, i, 0)), blk, blk, blk],
        out_specs=[blk, blk, blk, blk], out_shape=[shp, shp, shp, shp],
        compiler_params=_params(1),
    )(parts, w, m, v)


def _small_sum_adamw(items, sums_only):
    n, ne = len(items), len(sums_only)

    def total(p_ref):
        g = p_ref[0]
        for k in range(1, N_DEV):
            g = g + p_ref[k]
        return g

    def body(*refs):
        ins, outs = refs[:4 * n + ne], refs[4 * n + ne:]
        for i in range(n):
            p_ref, w_ref, m_ref, v_ref = ins[4 * i:4 * i + 4]
            g = total(p_ref)
            d, mn, vn = _adamw(w_ref[...], g, m_ref[...], v_ref[...])
            for o_ref, val in zip(outs[4 * i:4 * i + 4], (g, d, mn, vn)):
                o_ref[...] = val
        for j in range(ne):
            outs[4 * n + j][...] = total(ins[4 * n + j])

    args = [a for item in items for a in item] + list(sums_only)
    shapes = [jax.ShapeDtypeStruct(w.shape, F32) for _, w, _, _ in items for _ in range(4)]
    shapes += [jax.ShapeDtypeStruct(p.shape[1:], F32) for p in sums_only]
    out = pl.pallas_call(
        body, name="small_sum_adamw", in_specs=[VMEM] * len(args), out_specs=[VMEM] * len(shapes), out_shape=shapes,
        compiler_params=_params(0),
    )(*args)
    return [out[4 * i:4 * i + 4] for i in range(n)], out[4 * n:]


def _adamw_plain(g, w, m, v):
    r, cdim = w.shape
    tr = _row_tile(r)

    def body(g_ref, w_ref, m_ref, v_ref, d_out, m_out, v_out):
        d, mn, vn = _adamw(w_ref[...], g_ref[...], m_ref[...], v_ref[...])
        d_out[...] = d
        m_out[...] = mn
        v_out[...] = vn

    blk = pl.BlockSpec((tr, cdim), lambda i: (i, 0))
    shp = jax.ShapeDtypeStruct((r, cdim), F32)
    return pl.pallas_call(
        body, name="adamw_plain", grid=(r // tr,), in_specs=[blk, blk, blk, blk],
        out_specs=[blk, blk, blk], out_shape=[shp, shp, shp], compiler_params=_params(1),
    )(g, w, m, v)


ADA_COLS = N_MOD * D // N_DEV
ADA_ROWS = 8 * N_DEV


def _ada_fwd(cpad, w_ada, b_blocks, mixer_shards):
    n_w, w_shape, w_scr, w_stages = _gather_plan(mixer_shards)
    _, _, c_scr, c_stages = _gather_plan([cpad])
    _, _, p_scr, p_stages = _gather_plan([jax.ShapeDtypeStruct((ADA_ROWS, ADA_COLS), F32)])

    def body(c_ref, wa_ref, b_ref, *refs):
        w_refs, refs = refs[:n_w], refs[n_w:]
        cg_ref, call_ref, mod_ref = refs[:3]
        wg_refs, refs = refs[3:3 + n_w], refs[3 + n_w:]
        part_ref, pg_ref = refs[:2]
        c_sems, p_sems, w_sems = refs[2:5], refs[5:8], refs[8:11]
        w_start, w_forward, w_finish = w_stages(w_refs, wg_refs, w_sems)
        w_start()
        for stage in c_stages([c_ref], [cg_ref], c_sems):
            stage()
        cv = cg_ref[:, 0:8, :].reshape(ADA_ROWS, D)
        call_ref[...] = cv
        part_ref[...] = _dot(cv * jax.nn.sigmoid(cv), wa_ref[...])
        for stage in p_stages([part_ref], [pg_ref], p_sems):
            stage()
        x, y, c = _my_place()
        r0 = pl.multiple_of(8 * (4 * x + 2 * y + c), 8)
        for k in range(N_DEV):
            mod_ref[:, k * ADA_COLS:(k + 1) * ADA_COLS] = pg_ref[k, pl.ds(r0, 8), :] + b_ref[k]
        w_forward()
        w_finish()

    out = pl.pallas_call(
        body, name="ada_fwd", in_specs=[VMEM, VMEM, VMEM] + [ANY] * n_w,
        out_specs=[VMEM, VMEM, VMEM] + [ANY] * n_w,
        out_shape=[jax.ShapeDtypeStruct((N_DEV,) + cpad.shape, F32), jax.ShapeDtypeStruct((ADA_ROWS, D), F32),
                   jax.ShapeDtypeStruct((8, N_MOD * D), F32)] + list(w_shape),
        scratch_shapes=[pltpu.VMEM((ADA_ROWS, ADA_COLS), F32), pltpu.VMEM((N_DEV, ADA_ROWS, ADA_COLS), F32)]
        + list(c_scr) + list(p_scr) + list(w_scr),
        compiler_params=_params(0),
    )(cpad, w_ada, b_blocks, *mixer_shards)
    return out[0], out[1], out[2], out[3:]


ADA_RIDER_ROW = 4


def _ada_bwd(dmod_blk, c_all, w_ada, m_w, v_w, b_blocks, m_b, v_b, g_w, g_m, g_v):
    _, _, g_scr, g_stages = _gather_plan([dmod_blk])
    rest = D - ADA_COLS

    def body(dm_ref, c_ref, w_ref, mw_ref, vw_ref, b_ref, mb_ref, vb_ref, gw_ref, gm_ref, gv_ref,
             gw_o, dw_o, mw_o, vw_o, gb_o, dbb_o, mb_o, vb_o, gg_o, dgg_o, mg_o, vg_o, dg_ref, *sems):
        for stage in g_stages([dm_ref], [dg_ref], sems):
            stage()
        x, y, c = _my_place()
        r0 = pl.multiple_of(8 * (4 * x + 2 * y + c), 8)
        cols = dg_ref[:, pl.ds(r0, 8), :].reshape(ADA_ROWS, ADA_COLS)
        cv = c_ref[...]
        gw = _dot_tn(cv * jax.nn.sigmoid(cv), cols)
        d, mn, vn = _adamw(w_ref[...], gw, mw_ref[...], vw_ref[...])
        gw_o[...] = gw
        dw_o[...] = d
        mw_o[...] = mn
        vw_o[...] = vn
        is_example = lax.broadcasted_iota(jnp.int32, (8, 1), 0) < ADA_RIDER_ROW
        blocks = []
        for k in range(N_DEV):
            s = dg_ref[0, 8 * k:8 * k + 8, :]
            for dev in range(1, N_DEV):
                s = s + dg_ref[dev, 8 * k:8 * k + 8, :]
            blocks.append(s)
            gb = _colsum(jnp.where(is_example, s, 0.0))
            d, mn, vn = _adamw(b_ref[k], gb, mb_ref[k], vb_ref[k])
            gb_o[k] = gb
            dbb_o[k] = d
            mb_o[k] = mn
            vb_o[k] = vn
        rider = jnp.concatenate([blocks[0][ADA_RIDER_ROW:ADA_RIDER_ROW + 1, :],
                                 blocks[1][ADA_RIDER_ROW:ADA_RIDER_ROW + 1, 0:rest]], axis=1)
        d, mn, vn = _adamw(gw_ref[...], rider, gm_ref[...], gv_ref[...])
        gg_o[...] = rider
        dgg_o[...] = d
        mg_o[...] = mn
        vg_o[...] = vn

    ws = jax.ShapeDtypeStruct(w_ada.shape, F32)
    bs = jax.ShapeDtypeStruct(b_blocks.shape, F32)
    gs = jax.ShapeDtypeStruct(g_w.shape, F32)
    return pl.pallas_call(
        body, name="ada_bwd", in_specs=[VMEM] * 11, out_specs=[VMEM] * 12,
        out_shape=[ws, ws, ws, ws, bs, bs, bs, bs, gs, gs, gs, gs],
        scratch_shapes=[pltpu.VMEM((N_DEV, ADA_ROWS, ADA_COLS), F32)] + list(g_scr),
        compiler_params=_params(0),
    )(dmod_blk, c_all, w_ada, m_w, v_w, b_blocks, m_b, v_b, g_w, g_m, g_v)


def _ssm_param_fn(lr, li, ldt, br, bi):
    dt = jnp.exp(ldt)
    mag = jnp.exp(lr * dt)
    ang = li * dt
    lbr = mag * jnp.cos(ang)
    lbi = mag * jnp.sin(ang)
    nr = lbr - 1.0
    den = lr * lr + li * li
    cr = (nr * lr + lbi * li) / den
    ci = (lbi * lr - nr * li) / den
    return lbr, lbi, cr * br - ci * bi, cr * bi + ci * br


def _ssm_prep(lr, li, ldt, br, bi):
    def body(lr_ref, li_ref, ldt_ref, br_ref, bi_ref, lbr_o, lbi_o, bbr_o, bbi_o):
        lbr, lbi, bbr, bbi = _ssm_param_fn(lr_ref[...], li_ref[...], ldt_ref[...], br_ref[...], bi_ref[...])
        lbr_o[...] = lbr
        lbi_o[...] = lbi
        bbr_o[...] = bbr
        bbi_o[...] = bbi

    row = jax.ShapeDtypeStruct(lr.shape, F32)
    mat = jax.ShapeDtypeStruct(br.shape, F32)
    return pl.pallas_call(
        body, name="ssm_prep", in_specs=[VMEM] * 5, out_specs=[VMEM] * 4,
        out_shape=[row, row, mat, mat], compiler_params=_params(0),
    )(lr, li, ldt, br, bi)


def _ssm_param_bwd(lr, li, ldt, br, bi, dlam8, dbbr, dbbi, dd8):
    nv = dlam8.shape[1]

    def body(lr_ref, li_ref, ldt_ref, br_ref, bi_ref, dl_ref, dbr_ref, dbi_ref, dd_ref,
             glr_o, gli_o, gldt_o, gbr_o, gbi_o, gd_o):
        halves_r, halves_i, halves_d = [], [], []
        for e in range(2):
            ar = dl_ref[0, e:e + 1, :]
            ai = dl_ref[1, e:e + 1, :]
            ad = dd_ref[e:e + 1, :]
            for b in range(1, nv // 2):
                ar = ar + dl_ref[0, 2 * b + e:2 * b + e + 1, :]
                ai = ai + dl_ref[1, 2 * b + e:2 * b + e + 1, :]
                ad = ad + dd_ref[2 * b + e:2 * b + e + 1, :]
            halves_r.append(ar)
            halves_i.append(ai)
            halves_d.append(ad)
        dlbr = jnp.concatenate(halves_r, axis=1)
        dlbi = jnp.concatenate(halves_i, axis=1)
        gd_o[...] = jnp.concatenate(halves_d, axis=1)
        _, vjp = jax.vjp(_ssm_param_fn, lr_ref[...], li_ref[...], ldt_ref[...], br_ref[...], bi_ref[...])
        glr, gli, gldt, gbr, gbi = vjp((dlbr, dlbi, dbr_ref[...], dbi_ref[...]))
        glr_o[...] = glr
        gli_o[...] = gli
        gldt_o[...] = gldt
        gbr_o[...] = gbr
        gbi_o[...] = gbi

    row = jax.ShapeDtypeStruct(lr.shape, F32)
    mat = jax.ShapeDtypeStruct(br.shape, F32)
    return pl.pallas_call(
        body, name="ssm_param_bwd", in_specs=[VMEM] * 9, out_specs=[VMEM] * 6,
        out_shape=[row, row, row, mat, mat, jax.ShapeDtypeStruct((1, SSM_W), F32)],
        compiler_params=_params(0),
    )(lr, li, ldt, br, bi, dlam8, dbbr, dbbi, dd8)


def _blockdiag(m):
    _, g, a, b = m.shape
    eye = jnp.eye(g, dtype=m.dtype)
    return jnp.einsum("egab,gk->egakb", m, eye).reshape(2, g * a, g * b)


def _blockdiag_take(t, a, b):
    return jnp.einsum("gagb->gab", t.reshape(GRP // 2, a, GRP // 2, b))


def _mixer_in_fwd(x, sh1, sc1, g_mix, w_in_b, comm=None):
    bsz, seq, _ = x.shape
    tt = min(seq, TT_MIX)

    def body(x_ref, sh_ref, sc_ref, g_ref, w_ref, u_ref, p_ref):
        xhat, _ = _rms(x_ref[0])
        h = xhat * g_ref[...] * (1.0 + sc_ref[0]) + sh_ref[0]
        z = _dot(h.astype(BF16), w_ref[...])
        u_ref[...] = z[:, :SSM_W]
        p_ref[0] = z[:, SSM_W:]

    row = pl.BlockSpec((1, 1, D), lambda b, t: (b, 0, 0))
    return _fused_call(
        body, name="mixer_in_fwd", grid=(bsz, seq // tt),
        in_specs=[pl.BlockSpec((1, tt, D), lambda b, t: (b, t, 0)), row, row,
                  pl.BlockSpec((1, D), lambda b, t: (0, 0)), VMEM],
        out_specs=[pl.BlockSpec((tt, SSM_W), lambda b, t: (t, b)),
                   pl.BlockSpec((1, tt, POOL_W), lambda b, t: (b, t, 0))],
        out_shape=[jax.ShapeDtypeStruct((seq, bsz * SSM_W), F32), jax.ShapeDtypeStruct((bsz, seq, POOL_W), F32)],
        scratch_shapes=[], args=(x, sh1, sc1, g_mix, w_in_b), comm=_schedule(comm, bsz * (seq // tt)))


def _ssm_project_in(ub, par0, bb_ref, s_re, s_im, row0, tlen, nv):
    for part, sref in ((0, s_re), (1, s_im)):
        for k in range(HALF_ST // 512):
            c0 = part * HALF_ST + k * 512
            a0 = _dot(ub, bb_ref[:, c0:c0 + 512])
            a1 = _dot(ub, bb_ref[:, 2 * HALF_ST + c0:2 * HALF_ST + c0 + 512])
            sref[pl.ds(row0, tlen), :, k * 512:(k + 1) * 512] = jnp.where(par0, a0, a1).reshape(tlen, nv, 512)


def _ssm_fwd(u2r, bb, cc, lam8, d8, tlen, comm=None):
    nv = lam8.shape[1]
    rows = nv * tlen
    n_chunks = u2r.shape[0] // rows

    def body(u_ref, bb_ref, cc_ref, lam_ref, d_ref, y_ref, xc_ref, s_re, s_im, st):
        @pl.when(pl.program_id(0) == 0)
        def _():
            st[...] = jnp.zeros_like(st)

        xc_ref[0] = st[...]
        u = u_ref[...]
        par0 = (lax.broadcasted_iota(jnp.int32, (rows, 1), 0) % 2) == 0
        _ssm_project_in(u.astype(BF16), par0, bb_ref, s_re, s_im, 0, tlen, nv)
        for hb in range(HALF_ST // 512):
            ls = slice(hb * 512, (hb + 1) * 512)
            lr = lam_ref[0, :, ls]
            li = lam_ref[1, :, ls]

            def step(t, carry, ls=ls, lr=lr, li=li):
                xr, xi = carry
                nr = lr * xr - li * xi + s_re[t, :, ls]
                ni = lr * xi + li * xr + s_im[t, :, ls]
                s_re[t, :, ls] = nr
                s_im[t, :, ls] = ni
                return nr, ni

            xr, xi = lax.fori_loop(0, tlen, step, (st[0, :, ls], st[1, :, ls]), unroll=8)
            st[0, :, ls] = xr
            st[1, :, ls] = xi
        xre = s_re[...].reshape(rows, HALF_ST).astype(BF16)
        xim = s_im[...].reshape(rows, HALF_ST).astype(BF16)
        y2 = _dot(xre, cc_ref[0:HALF_ST, :]) + _dot(xim, cc_ref[HALF_ST:, :])
        y = jnp.where(par0, y2[:, :HALF_CH], y2[:, HALF_CH:])
        skip = (u.reshape(tlen, nv, HALF_CH) * d_ref[...][None]).reshape(rows, HALF_CH)
        y_ref[...] = y + skip

    return _fused_call(
        body, name="ssm_fwd", grid=(n_chunks,),
        in_specs=[pl.BlockSpec((rows, HALF_CH), lambda c: (c, 0)), VMEM, VMEM, VMEM, VMEM],
        out_specs=[pl.BlockSpec((rows, HALF_CH), lambda c: (c, 0)),
                   pl.BlockSpec((1, 2, nv, HALF_ST), lambda c: (c, 0, 0, 0))],
        out_shape=[jax.ShapeDtypeStruct(u2r.shape, F32), jax.ShapeDtypeStruct((n_chunks, 2, nv, HALF_ST), F32)],
        scratch_shapes=[pltpu.VMEM((tlen, nv, HALF_ST), F32), pltpu.VMEM((tlen, nv, HALF_ST), F32),
                        pltpu.VMEM((2, nv, HALF_ST), F32)],
        args=(u2r, bb, cc, lam8, d8), comm=_schedule(comm, n_chunks))


def _pool_forward(ext, pv, pos, wp_ref, bp_ref):
    cur = ext
    zs, zls = [], []
    for gi, w in enumerate(POOL_WINDOWS):
        cur = cur + pltpu.roll(cur, w // 2, 0)
        sw = cur[POOL_HALO:, 0:128]
        z = sw / jnp.minimum(pos, float(w)) - pv[:, gi * 128:(gi + 1) * 128]
        zs.append(z)
        zls.append(_dot(z.astype(BF16), wp_ref[gi]) + bp_ref[:, gi * 128:(gi + 1) * 128])
        if gi + 1 < len(POOL_WINDOWS):
            cur = cur[:, 128:]
    return zs, zls


def _mixer_out_fwd(y2, p, x, gt1, w_glu_b, b_glu, w_pool_b, b_pool, pscale, w_out_b):
    bsz, seq, _ = x.shape
    tt = min(seq, TT_MIX)

    def body(y_ref, p_ref, x_ref, gt_ref, wg_ref, bg_ref, wp_ref, bp_ref, ps_ref, wo_ref, x1_ref, mix_ref, ext):
        ti = pl.program_id(1)

        @pl.when(ti == 0)
        def _():
            ext[0:POOL_HALO, :] = jnp.zeros((POOL_HALO, POOL_W), F32)

        pv = p_ref[0]
        ext[POOL_HALO:, :] = pv
        pos = (ti * tt + lax.broadcasted_iota(jnp.int32, (tt, 1), 0) + 1).astype(F32)
        _, zls = _pool_forward(ext[...], pv, pos, wp_ref, bp_ref)
        ext[0:POOL_HALO, :] = pv[tt - POOL_HALO:, :]
        a = _gelu(y_ref[...])
        gl = _dot(a.astype(BF16), wg_ref[...]) + bg_ref[...]
        y_ssm = gl[:, :SSM_W] * jax.nn.sigmoid(gl[:, SSM_W:])
        y_pool = [zl * ps_ref[:, gi * 128:(gi + 1) * 128] for gi, zl in enumerate(zls)]
        mixcat = jnp.concatenate([y_ssm] + y_pool, axis=1).astype(BF16)
        mix_ref[0] = mixcat
        x1_ref[0] = x_ref[0] + gt_ref[0] * _dot(mixcat, wo_ref[...])

    xt = pl.BlockSpec((1, tt, D), lambda b, t: (b, t, 0))
    return pl.pallas_call(
        body, name="mixer_out_fwd", grid=(bsz, seq // tt),
        in_specs=[pl.BlockSpec((tt, SSM_W), lambda b, t: (t, b)),
                  pl.BlockSpec((1, tt, POOL_W), lambda b, t: (b, t, 0)), xt,
                  pl.BlockSpec((1, 1, D), lambda b, t: (b, 0, 0)), VMEM, VMEM, VMEM, VMEM, VMEM, VMEM],
        out_specs=[xt, xt],
        out_shape=[jax.ShapeDtypeStruct(x.shape, F32), jax.ShapeDtypeStruct(x.shape, BF16)],
        scratch_shapes=[pltpu.VMEM((POOL_HALO + tt, POOL_W), F32)],
        compiler_params=_params(2),
    )(y2, p, x, gt1, w_glu_b, b_glu, w_pool_b, b_pool, pscale, w_out_b)


def _conv_gate(g, ge, wc, bc):
    g1 = pltpu.roll(ge, 1, 0)[CONV_HALO:]
    g2 = pltpu.roll(ge, 2, 0)[CONV_HALO:]
    return wc[2:3] * g + wc[1:2] * g1 + wc[0:1] * g2 + bc, g1, g2


def _ffn_fwd(x1, tgt, sh2, sc2, gt2, g_ffn, w_up_b, w_conv, b_conv, w_down_b, g_fin):
    bsz, seq, _ = x1.shape
    tt = min(seq, TT_FFN)
    n_t = seq // tt
    n_ck = DFF // FF_CH

    def body(x1_ref, tg_ref, sh_ref, sc_ref, gt_ref, gf_ref, wu_ref, wc_ref, bc_ref, wd_ref, gfin_ref,
             h2_ref, v_ref, g_ref, act_ref, ddn_ref, dx2_ref, loss_ref, dgfin_ref, dgt_ref, gext, lacc):
        b = pl.program_id(0)
        ti = pl.program_id(1)

        @pl.when((b == 0) & (ti == 0))
        def _():
            lacc[...] = jnp.zeros_like(lacc)
            dgfin_ref[...] = jnp.zeros_like(dgfin_ref)

        @pl.when(ti == 0)
        def _():
            dgt_ref[...] = jnp.zeros_like(dgt_ref)
            gext[:, 0:CONV_HALO, :] = jnp.zeros((n_ck, CONV_HALO, FF_CH), F32)

        x1v = x1_ref[0]
        xhat, _ = _rms(x1v)
        h2b = (xhat * gf_ref[...] * (1.0 + sc_ref[0]) + sh_ref[0]).astype(BF16)
        h2_ref[0] = h2b
        dn = jnp.zeros((tt, D), F32)
        for ck in range(n_ck):
            c0 = ck * FF_CH
            v = _dot(h2b, wu_ref[:, c0:c0 + FF_CH])
            g = _dot(h2b, wu_ref[:, DFF + c0:DFF + c0 + FF_CH])
            v_ref[0, :, c0:c0 + FF_CH] = v.astype(BF16)
            g_ref[0, :, c0:c0 + FF_CH] = g.astype(BF16)
            gext[ck, CONV_HALO:, :] = g
            gc, _, _ = _conv_gate(g, gext[ck], wc_ref[:, c0:c0 + FF_CH], bc_ref[:, c0:c0 + FF_CH])
            gext[ck, 0:CONV_HALO, :] = g[tt - CONV_HALO:, :]
            actb = (gc * jax.nn.sigmoid(gc) * v).astype(BF16)
            act_ref[0, :, c0:c0 + FF_CH] = actb
            dn = dn + _dot(actb, wd_ref[c0:c0 + FF_CH, :])
        gt = gt_ref[0]
        xh3, r3 = _rms(x1v + gt * dn)
        gfin = gfin_ref[...]
        diff = xh3 * gfin - tg_ref[0]
        lacc[...] += _colsum(diff * diff)
        dy = diff * (1.0 / D)
        dgfin_ref[...] += _colsum(dy * xh3)
        dx2 = _rms_bwd(dy * gfin, xh3, r3)
        dx2_ref[0] = dx2
        dgt_ref[0] += _colsum(dx2 * dn)
        ddn_ref[0] = (gt * dx2).astype(BF16)

        @pl.when((b == bsz - 1) & (ti == n_t - 1))
        def _():
            loss_ref[...] = jnp.full(loss_ref.shape, 0.5 / D * jnp.sum(lacc[...]), F32)

    xt = pl.BlockSpec((1, tt, D), lambda b, t: (b, t, 0))
    ft = pl.BlockSpec((1, tt, DFF), lambda b, t: (b, t, 0))
    row = pl.BlockSpec((1, 1, D), lambda b, t: (b, 0, 0))
    vec = pl.BlockSpec((1, D), lambda b, t: (0, 0))
    ff = jax.ShapeDtypeStruct((bsz, seq, DFF), BF16)
    xs = jax.ShapeDtypeStruct((bsz, seq, D), BF16)
    return pl.pallas_call(
        body, name="ffn_fwd", grid=(bsz, n_t),
        in_specs=[xt, xt, row, row, row, vec, VMEM, VMEM, VMEM, VMEM, vec],
        out_specs=[xt, ft, ft, ft, xt, xt, pl.BlockSpec((1, 128), lambda b, t: (0, 0)), vec, row],
        out_shape=[xs, ff, ff, ff, xs, jax.ShapeDtypeStruct((bsz, seq, D), F32),
                   jax.ShapeDtypeStruct((1, 128), F32), jax.ShapeDtypeStruct((1, D), F32),
                   jax.ShapeDtypeStruct((bsz, 1, D), F32)],
        scratch_shapes=[pltpu.VMEM((n_ck, CONV_HALO + tt, FF_CH), F32), pltpu.VMEM((1, D), F32)],
        compiler_params=_params(2),
    )(x1, tgt, sh2, sc2, gt2, g_ffn, w_up_b, w_conv, b_conv, w_down_b, g_fin)


def _ffn_bwd(ddn, gq, vq, x1, dx2, sh2, sc2, g_ffn, w_conv, b_conv, w_down_b, w_up_b):
    bsz, seq, _ = x1.shape
    tt = min(seq, TT_FFN)
    n_t = seq // tt
    n_ck = DFF // FF_CH
    hb = 16
    ext_rows = tt + CONV_HALO

    def body(ddn_ref, g_ref, gh_ref, v_ref, x1_ref, dx2_ref, sh_ref, sc_ref, gf_ref, wc_ref, bc_ref, wd_ref,
             wu_ref, dup_ref, dx1_ref, dsh_ref, dsc_ref, dgf_ref, dwc_ref, dbc_ref, gext, dext):
        b = pl.program_id(0)
        i = pl.program_id(1)
        tile = n_t - 1 - i

        @pl.when((b == 0) & (i == 0))
        def _():
            dgf_ref[...] = jnp.zeros_like(dgf_ref)
            dwc_ref[...] = jnp.zeros_like(dwc_ref)
            dbc_ref[...] = jnp.zeros_like(dbc_ref)

        @pl.when(i == 0)
        def _():
            dsh_ref[...] = jnp.zeros_like(dsh_ref)
            dsc_ref[...] = jnp.zeros_like(dsc_ref)
            dext[:, tt:, :] = jnp.zeros((n_ck, CONV_HALO, FF_CH), F32)

        ddnv = ddn_ref[0]
        has_halo = (tile > 0).astype(F32)
        dh2 = jnp.zeros((tt, D), F32)
        for ck in range(n_ck):
            c0 = ck * FF_CH
            dact = _dot_nt(ddnv, wd_ref[c0:c0 + FF_CH, :])
            g = g_ref[0, :, c0:c0 + FF_CH].astype(F32)
            v = v_ref[0, :, c0:c0 + FF_CH].astype(F32)
            gext[0:CONV_HALO, :] = gh_ref[0, :, c0:c0 + FF_CH].astype(F32)[hb - CONV_HALO:, :] * has_halo
            gext[CONV_HALO:, :] = g
            wc = wc_ref[:, c0:c0 + FF_CH]
            gc, g1, g2 = _conv_gate(g, gext[...], wc, bc_ref[:, c0:c0 + FF_CH])
            sg = jax.nn.sigmoid(gc)
            dv = dact * (gc * sg)
            dgc = dact * v * (sg * (1.0 + gc * (1.0 - sg)))
            dbc_ref[:, c0:c0 + FF_CH] += _colsum(dgc)
            dwc_ref[0:1, c0:c0 + FF_CH] += _colsum(dgc * g2)
            dwc_ref[1:2, c0:c0 + FF_CH] += _colsum(dgc * g1)
            dwc_ref[2:3, c0:c0 + FF_CH] += _colsum(dgc * g)
            dext[ck, 0:tt, :] = dgc
            de = dext[ck]
            dg = (wc[2:3] * dgc + wc[1:2] * pltpu.roll(de, ext_rows - 1, 0)[0:tt]
                  + wc[0:1] * pltpu.roll(de, ext_rows - 2, 0)[0:tt])
            dext[ck, tt:, :] = dgc[0:CONV_HALO, :]
            dvb = dv.astype(BF16)
            dgb = dg.astype(BF16)
            dup_ref[0, :, c0:c0 + FF_CH] = dvb
            dup_ref[0, :, DFF + c0:DFF + c0 + FF_CH] = dgb
            dh2 = dh2 + _dot_nt(dvb, wu_ref[:, c0:c0 + FF_CH]) + _dot_nt(dgb, wu_ref[:, DFF + c0:DFF + c0 + FF_CH])
        xhat, rstd = _rms(x1_ref[0])
        gf = gf_ref[...]
        dsh_ref[0] += _colsum(dh2)
        dsc_ref[0] += _colsum(dh2 * xhat * gf)
        t = dh2 * (1.0 + sc_ref[0])
        dgf_ref[...] += _colsum(t * xhat)
        dx1_ref[0] = dx2_ref[0] + _rms_bwd(t * gf, xhat, rstd)

    def rev(b, t):
        return (b, n_t - 1 - t, 0)

    def halo(b, t):
        return (b, jnp.maximum((n_t - 1 - t) * (tt // hb) - 1, 0), 0)

    xt = pl.BlockSpec((1, tt, D), rev)
    ft = pl.BlockSpec((1, tt, DFF), rev)
    row = pl.BlockSpec((1, 1, D), lambda b, t: (b, 0, 0))
    vec = pl.BlockSpec((1, D), lambda b, t: (0, 0))
    rows = jax.ShapeDtypeStruct((bsz, 1, D), F32)
    return pl.pallas_call(
        body, name="ffn_bwd", grid=(bsz, n_t),
        in_specs=[xt, ft, pl.BlockSpec((1, hb, DFF), halo), ft, xt, xt, row, row, vec, VMEM, VMEM, VMEM, VMEM],
        out_specs=[pl.BlockSpec((1, tt, 2 * DFF), rev), xt, row, row, vec,
                   pl.BlockSpec((3, DFF), lambda b, t: (0, 0)), pl.BlockSpec((1, DFF), lambda b, t: (0, 0))],
        out_shape=[jax.ShapeDtypeStruct((bsz, seq, 2 * DFF), BF16), jax.ShapeDtypeStruct((bsz, seq, D), F32),
                   rows, rows, jax.ShapeDtypeStruct((1, D), F32), jax.ShapeDtypeStruct((3, DFF), F32),
                   jax.ShapeDtypeStruct((1, DFF), F32)],
        scratch_shapes=[pltpu.VMEM((CONV_HALO + tt, FF_CH), F32), pltpu.VMEM((n_ck, ext_rows, FF_CH), F32)],
        compiler_params=_params(2),
    )(ddn, gq, gq, vq, x1, dx2, sh2, sc2, g_ffn, w_conv, b_conv, w_down_b, w_up_b)


def _wgrad(a, b, bk2, name):
    n, k1 = a.shape
    _, k2 = b.shape
    tt = min(n, TT_MIX)

    def body(a_ref, b_ref, o_ref):
        @pl.when(pl.program_id(1) == 0)
        def _():
            o_ref[...] = jnp.zeros_like(o_ref)

        o_ref[...] += _dot_tn(a_ref[...], b_ref[...])

    return pl.pallas_call(
        body, name=name, grid=(k2 // bk2, n // tt),
        in_specs=[pl.BlockSpec((tt, k1), lambda j, i: (i, 0)), pl.BlockSpec((tt, bk2), lambda j, i: (i, j))],
        out_specs=pl.BlockSpec((k1, bk2), lambda j, i: (0, j)),
        out_shape=jax.ShapeDtypeStruct((k1, k2), F32), compiler_params=_params(2),
    )(a, b)


def _mixer_out_bwd(dx1, mixcat, y2, p, gt1, w_glu_b, b_glu, w_pool_b, b_pool, pscale, w_out_b, comm=None):
    bsz, seq, _ = dx1.shape
    tt = min(seq, TT_MIX)
    n_t = seq // tt
    ext_rows = tt + POOL_HALO

    def body(dx1_ref, mc_ref, y_ref, p_ref, ph_ref, gt_ref, wg_ref, bg_ref, wp_ref, bp_ref, ps_ref, wo_ref,
             dy_ref, dp_ref, dwo_ref, dwg_ref, dbg_ref, dwp_ref, dbp_ref, dps_ref, dgt_ref, ext, qext):
        b = pl.program_id(0)
        i = pl.program_id(1)
        tile = n_t - 1 - i

        @pl.when((b == 0) & (i == 0))
        def _():
            for r in (dwo_ref, dwg_ref, dbg_ref, dwp_ref, dbp_ref, dps_ref):
                r[...] = jnp.zeros_like(r)

        @pl.when(i == 0)
        def _():
            dgt_ref[...] = jnp.zeros_like(dgt_ref)
            qext[tt:, :] = jnp.zeros((POOL_HALO, POOL_W), F32)

        dx1v = dx1_ref[0]
        mc = mc_ref[0]
        dgt_ref[0] += _colsum(dx1v * _dot(mc, wo_ref[...]))
        dmixed = (gt_ref[0] * dx1v).astype(BF16)
        dwo_ref[...] += _dot_tn(mc, dmixed)
        dmc = _dot_nt(dmixed, wo_ref[...])
        pv = p_ref[0]
        ext[0:POOL_HALO, :] = ph_ref[0] * (tile > 0).astype(F32)
        ext[POOL_HALO:, :] = pv
        pos = (tile * tt + lax.broadcasted_iota(jnp.int32, (tt, 1), 0) + 1).astype(F32)
        zs, zls = _pool_forward(ext[...], pv, pos, wp_ref, bp_ref)
        dzs = []
        for gi, w in enumerate(POOL_WINDOWS):
            cs = slice(gi * 128, (gi + 1) * 128)
            dyp = dmc[:, SSM_W + gi * 128:SSM_W + (gi + 1) * 128]
            dps_ref[:, cs] += _colsum(dyp * zls[gi])
            dzl = dyp * ps_ref[:, cs]
            dbp_ref[:, cs] += _colsum(dzl)
            dzlb = dzl.astype(BF16)
            dwp_ref[gi] += _dot_tn(zs[gi].astype(BF16), dzlb)
            dz = _dot_nt(dzlb, wp_ref[gi])
            dzs.append(dz)
            qext[0:tt, cs] = dz / jnp.minimum(pos, float(w))
        cur = qext[...]
        dps = []
        for gi, w in enumerate(POOL_WINDOWS):
            cur = cur + pltpu.roll(cur, ext_rows - w // 2, 0)
            dps.append(cur[0:tt, 0:128] - dzs[gi])
            if gi + 1 < len(POOL_WINDOWS):
                cur = cur[:, 128:]
        qhead = qext[0:POOL_HALO, :]
        qext[tt:, :] = qhead
        dp_ref[0] = jnp.concatenate(dps, axis=1)
        yv = y_ref[...]
        ab = _gelu(yv).astype(BF16)
        gl = _dot(ab, wg_ref[...]) + bg_ref[...]
        val = gl[:, :SSM_W]
        sg = jax.nn.sigmoid(gl[:, SSM_W:])
        dys = dmc[:, :SSM_W]
        dgl = jnp.concatenate([dys * sg, dys * val * sg * (1.0 - sg)], axis=1)
        dbg_ref[...] += _colsum(dgl)
        dglb = dgl.astype(BF16)
        dwg_ref[...] += _dot_tn(ab, dglb)
        dy_ref[...] = _dot_nt(dglb, wg_ref[...]) * _gelu_grad(yv)

    def rev(b, t):
        return (b, n_t - 1 - t, 0)

    def halo(b, t):
        return (b, jnp.maximum((n_t - 1 - t) * (tt // POOL_HALO) - 1, 0), 0)

    xt = pl.BlockSpec((1, tt, D), rev)
    pt = pl.BlockSpec((1, tt, POOL_W), rev)
    yt = pl.BlockSpec((tt, SSM_W), lambda b, t: (n_t - 1 - t, b))

    def whole(shape):
        return pl.BlockSpec(shape, lambda b, t: (0,) * len(shape))

    return _fused_call(
        body, name="mixer_out_bwd", grid=(bsz, n_t),
        in_specs=[xt, xt, yt, pt, pl.BlockSpec((1, POOL_HALO, POOL_W), halo),
                  pl.BlockSpec((1, 1, D), lambda b, t: (b, 0, 0)), VMEM, VMEM, VMEM, VMEM, VMEM, VMEM],
        out_specs=[yt, pt, whole((D, D)), whole((SSM_W, 2 * SSM_W)), whole((1, 2 * SSM_W)),
                   whole((4, 128, 128)), whole((1, POOL_W)), whole((1, POOL_W)),
                   pl.BlockSpec((1, 1, D), lambda b, t: (b, 0, 0))],
        out_shape=[jax.ShapeDtypeStruct(y2.shape, F32), jax.ShapeDtypeStruct(p.shape, F32),
                   jax.ShapeDtypeStruct((D, D), F32), jax.ShapeDtypeStruct((SSM_W, 2 * SSM_W), F32),
                   jax.ShapeDtypeStruct((1, 2 * SSM_W), F32), jax.ShapeDtypeStruct((4, 128, 128), F32),
                   jax.ShapeDtypeStruct((1, POOL_W), F32), jax.ShapeDtypeStruct((1, POOL_W), F32),
                   jax.ShapeDtypeStruct((bsz, 1, D), F32)],
        scratch_shapes=[pltpu.VMEM((POOL_HALO + tt, POOL_W), F32), pltpu.VMEM((ext_rows, POOL_W), F32)],
        args=(dx1, mixcat, y2, p, p, gt1, w_glu_b, b_glu, w_pool_b, b_pool, pscale, w_out_b),
        comm=_schedule(comm, bsz * n_t))


def _ssm_bwd(dy2r, u2r, xc, bb, cc, lam8, d8, tlen, comm=None):
    nv = lam8.shape[1]
    rows = nv * tlen
    n_chunks = u2r.shape[0] // rows

    def body(dy_ref, u_ref, xc_ref, bb_ref, cc_ref, lam_ref, d_ref, du_ref, dcc_ref, dbb_ref, dlam_ref, dd_ref,
             s_re, s_im, g_re, g_im, gst):
        i = pl.program_id(0)

        @pl.when(i == 0)
        def _():
            for r in (gst, dcc_ref, dbb_ref, dlam_ref, dd_ref):
                r[...] = jnp.zeros_like(r)

        u = u_ref[...]
        dy = dy_ref[...]
        par0 = (lax.broadcasted_iota(jnp.int32, (rows, 1), 0) % 2) == 0
        s_re[0] = xc_ref[0, 0]
        s_im[0] = xc_ref[0, 1]
        _ssm_project_in(u.astype(BF16), par0, bb_ref, s_re, s_im, 1, tlen, nv)
        for hb in range(HALF_ST // 512):
            ls = slice(hb * 512, (hb + 1) * 512)
            lr = lam_ref[0, :, ls]
            li = lam_ref[1, :, ls]

            def fstep(t, carry, ls=ls, lr=lr, li=li):
                xr, xi = carry
                nr = lr * xr - li * xi + s_re[t + 1, :, ls]
                ni = lr * xi + li * xr + s_im[t + 1, :, ls]
                s_re[t + 1, :, ls] = nr
                s_im[t + 1, :, ls] = ni
                return nr, ni

            lax.fori_loop(0, tlen, fstep, (s_re[0, :, ls], s_im[0, :, ls]), unroll=8)
        zero = jnp.zeros_like(dy)
        dy2 = jnp.concatenate([jnp.where(par0, dy, zero), jnp.where(par0, zero, dy)], axis=1).astype(BF16)
        u2 = jnp.concatenate([jnp.where(par0, u, zero), jnp.where(par0, zero, u)], axis=1).astype(BF16)
        xre = s_re[pl.ds(1, tlen)].reshape(rows, HALF_ST).astype(BF16)
        xim = s_im[pl.ds(1, tlen)].reshape(rows, HALF_ST).astype(BF16)
        dcc_ref[0:HALF_ST, :] += _dot_tn(xre, dy2)
        dcc_ref[HALF_ST:, :] += _dot_tn(xim, dy2)
        for part, gref in ((0, g_re), (1, g_im)):
            for k in range(HALF_ST // 512):
                r0 = part * HALF_ST + k * 512
                gref[:, :, k * 512:(k + 1) * 512] = _dot_nt(dy2, cc_ref[r0:r0 + 512, :]).reshape(tlen, nv, 512)
        for hb in range(HALF_ST // 512):
            ls = slice(hb * 512, (hb + 1) * 512)
            lr = lam_ref[0, :, ls]
            li = lam_ref[1, :, ls]

            def bstep(k, carry, ls=ls, lr=lr, li=li):
                t = tlen - 1 - k
                gr, gi, ar, ai = carry
                ngr = g_re[t, :, ls] + lr * gr + li * gi
                ngi = g_im[t, :, ls] + lr * gi - li * gr
                g_re[t, :, ls] = ngr
                g_im[t, :, ls] = ngi
                xpr = s_re[t, :, ls]
                xpi = s_im[t, :, ls]
                return ngr, ngi, ar + ngr * xpr + ngi * xpi, ai + ngi * xpr - ngr * xpi

            init = (gst[0, :, ls], gst[1, :, ls], dlam_ref[0, :, ls], dlam_ref[1, :, ls])
            gr, gi, ar, ai = lax.fori_loop(0, tlen, bstep, init, unroll=4)
            gst[0, :, ls] = gr
            gst[1, :, ls] = gi
            dlam_ref[0, :, ls] = ar
            dlam_ref[1, :, ls] = ai
        gre = g_re[...].reshape(rows, HALF_ST).astype(BF16)
        gim = g_im[...].reshape(rows, HALF_ST).astype(BF16)
        du0 = _dot_nt(gre, bb_ref[:, 0:HALF_ST]) + _dot_nt(gim, bb_ref[:, HALF_ST:2 * HALF_ST])
        du1 = _dot_nt(gre, bb_ref[:, 2 * HALF_ST:3 * HALF_ST]) + _dot_nt(gim, bb_ref[:, 3 * HALF_ST:])
        skip = (dy.reshape(tlen, nv, HALF_CH) * d_ref[...][None]).reshape(rows, HALF_CH)
        du_ref[...] = jnp.where(par0, du0, du1) + skip
        dbb_ref[:, 0:HALF_ST] += _dot_tn(u2, gre)
        dbb_ref[:, HALF_ST:] += _dot_tn(u2, gim)
        dd_ref[...] += jnp.sum((dy * u).reshape(tlen, nv, HALF_CH), axis=0)

        @pl.when(i == n_chunks - 1)
        def _():
            dcc_ref[HALF_ST:, :] = -dcc_ref[HALF_ST:, :]

    def rev(c):
        return (n_chunks - 1 - c, 0)

    def whole(shape):
        return pl.BlockSpec(shape, lambda c: (0,) * len(shape))

    blk = pl.BlockSpec((rows, HALF_CH), rev)
    return _fused_call(
        body, name="ssm_bwd", grid=(n_chunks,),
        in_specs=[blk, blk, pl.BlockSpec((1, 2, nv, HALF_ST), lambda c: (n_chunks - 1 - c, 0, 0, 0)),
                  VMEM, VMEM, VMEM, VMEM],
        out_specs=[blk, whole((2 * HALF_ST, SSM_W)), whole((SSM_W, 2 * HALF_ST)), whole((2, nv, HALF_ST)),
                   whole((nv, HALF_CH))],
        out_shape=[jax.ShapeDtypeStruct(u2r.shape, F32), jax.ShapeDtypeStruct((2 * HALF_ST, SSM_W), F32),
                   jax.ShapeDtypeStruct((SSM_W, 2 * HALF_ST), F32), jax.ShapeDtypeStruct((2, nv, HALF_ST), F32),
                   jax.ShapeDtypeStruct((nv, HALF_CH), F32)],
        scratch_shapes=[pltpu.VMEM((tlen + 1, nv, HALF_ST), F32), pltpu.VMEM((tlen + 1, nv, HALF_ST), F32),
                        pltpu.VMEM((tlen, nv, HALF_ST), F32), pltpu.VMEM((tlen, nv, HALF_ST), F32),
                        pltpu.VMEM((2, nv, HALF_ST), F32)],
        args=(dy2r, u2r, xc, bb, cc, lam8, d8), comm=_schedule(comm, n_chunks))


def _mixer_in_bwd(du2, dp, x, dx1, sh1, sc1, g_mix, w_in_b, comm=None):
    bsz, seq, _ = x.shape
    tt = min(seq, TT_MIX)

    def body(du_ref, dp_ref, x_ref, dx1_ref, sh_ref, sc_ref, g_ref, w_ref,
             dx_ref, dw_ref, dsh_ref, dsc_ref, dg_ref):
        b = pl.program_id(0)
        ti = pl.program_id(1)

        @pl.when((b == 0) & (ti == 0))
        def _():
            dw_ref[...] = jnp.zeros_like(dw_ref)
            dg_ref[...] = jnp.zeros_like(dg_ref)

        @pl.when(ti == 0)
        def _():
            dsh_ref[...] = jnp.zeros_like(dsh_ref)
            dsc_ref[...] = jnp.zeros_like(dsc_ref)

        dz = jnp.concatenate([du_ref[...], dp_ref[0]], axis=1).astype(BF16)
        xhat, rstd = _rms(x_ref[0])
        g = g_ref[...]
        sc = sc_ref[0]
        a = xhat * g
        h = (a * (1.0 + sc) + sh_ref[0]).astype(BF16)
        dw_ref[...] += _dot_tn(h, dz)
        dh = _dot_nt(dz, w_ref[...])
        dsh_ref[0] += _colsum(dh)
        dsc_ref[0] += _colsum(dh * a)
        t = dh * (1.0 + sc)
        dg_ref[...] += _colsum(t * xhat)
        dx_ref[0] = dx1_ref[0] + _rms_bwd(t * g, xhat, rstd)

    xt = pl.BlockSpec((1, tt, D), lambda b, t: (b, t, 0))
    row = pl.BlockSpec((1, 1, D), lambda b, t: (b, 0, 0))
    vec = pl.BlockSpec((1, D), lambda b, t: (0, 0))
    rows = jax.ShapeDtypeStruct((bsz, 1, D), F32)
    return _fused_call(
        body, name="mixer_in_bwd", grid=(bsz, seq // tt),
        in_specs=[pl.BlockSpec((tt, SSM_W), lambda b, t: (t, b)),
                  pl.BlockSpec((1, tt, POOL_W), lambda b, t: (b, t, 0)), xt, xt, row, row, vec, VMEM],
        out_specs=[xt, pl.BlockSpec((D, D), lambda b, t: (0, 0)), row, row, vec],
        out_shape=[jax.ShapeDtypeStruct(x.shape, F32), jax.ShapeDtypeStruct((D, D), F32), rows, rows,
                   jax.ShapeDtypeStruct((1, D), F32)],
        scratch_shapes=[], args=(du2, dp, x, dx1, sh1, sc1, g_mix, w_in_b),
        comm=_schedule(comm, bsz * (seq // tt)))


def kernel(x, c, w_ada, b_ada, g_norm_mix, w_in, ssm_lam_re, ssm_lam_im, ssm_log_dt, ssm_b_re, ssm_b_im, ssm_c_re, ssm_c_im, ssm_d, w_glu, b_glu, w_pool, b_pool, pool_scale, w_out, g_norm_ffn, w_up, w_conv, b_conv, w_down, g_norm_final, loss_target, m_w_ada, m_b_ada, m_g_norm_mix, m_w_in, m_ssm_lam_re, m_ssm_lam_im, m_ssm_log_dt, m_ssm_b_re, m_ssm_b_im, m_ssm_c_re, m_ssm_c_im, m_ssm_d, m_w_glu, m_b_glu, m_w_pool, m_b_pool, m_pool_scale, m_w_out, m_g_norm_ffn, m_w_up, m_w_conv, m_b_conv, m_w_down, m_g_norm_final, v_w_ada, v_b_ada, v_g_norm_mix, v_w_in, v_ssm_lam_re, v_ssm_lam_im, v_ssm_log_dt, v_ssm_b_re, v_ssm_b_im, v_ssm_c_re, v_ssm_c_im, v_ssm_d, v_w_glu, v_b_glu, v_w_pool, v_b_pool, v_pool_scale, v_w_out, v_g_norm_ffn, v_w_up, v_w_conv, v_b_conv, v_w_down, v_g_norm_final):
    bsz, seq, _ = x.shape
    assert 2 * bsz == 8 and seq % 128 == 0
    px, py, pc = _my_place()
    me = 4 * px + 2 * py + pc
    place = jnp.stack([pc, 2 * px + py]).astype(jnp.int32)
    ncol = ADA_COLS

    cpad = jnp.zeros((16, D), F32).at[0:bsz].set(c).at[8:11, 0:352].set(w_conv[0])
    cg, c_all, mod8, (g_in,) = _ada_fwd(cpad, w_ada[0], b_ada.reshape(N_DEV, 1, ncol), [w_in[0].astype(BF16)])
    w_conv_f = cg[:, 8:11, 0:352].transpose(1, 0, 2).reshape(3, DFF)
    w_in_b = g_in.reshape(D, D)
    sh1, sc1, gt1, sh2, sc2, gt2 = [mod8[0:bsz, k * D:(k + 1) * D].reshape(bsz, 1, D) for k in range(N_MOD)]

    lam_r = ssm_lam_re[0].reshape(1, GRP * NST)
    lam_i = ssm_lam_im[0].reshape(1, GRP * NST)
    ldt = jnp.repeat(ssm_log_dt[0], NST).reshape(1, GRP * NST)
    b_r = ssm_b_re[0].transpose(2, 0, 1).reshape(GCH, GRP * NST)
    b_i = ssm_b_im[0].transpose(2, 0, 1).reshape(GCH, GRP * NST)
    lbr, lbi, bbr, bbi = _ssm_prep(lam_r, lam_i, ldt, b_r, b_i)
    lam8 = jnp.stack([jnp.tile(lbr.reshape(2, HALF_ST), (bsz, 1)), jnp.tile(lbi.reshape(2, HALF_ST), (bsz, 1))])
    bd_r = _blockdiag(bbr.reshape(GCH, 2, GRP // 2, NST).transpose(1, 2, 0, 3))
    bd_i = _blockdiag(bbi.reshape(GCH, 2, GRP // 2, NST).transpose(1, 2, 0, 3))
    bb = jnp.concatenate([bd_r[0], bd_i[0], bd_r[1], bd_i[1]], axis=1).astype(BF16)
    cd_r = _blockdiag(ssm_c_re[0].reshape(2, GRP // 2, GCH, NST).transpose(0, 1, 3, 2))
    cd_i = _blockdiag(ssm_c_im[0].reshape(2, GRP // 2, GCH, NST).transpose(0, 1, 3, 2))
    cc = jnp.concatenate([jnp.concatenate([cd_r[0], cd_r[1]], axis=1),
                          jnp.concatenate([-cd_i[0], -cd_i[1]], axis=1)], axis=0).astype(BF16)
    d8 = jnp.tile(ssm_d[0].reshape(2, HALF_CH), (bsz, 1))

    tlen = min(seq, T_SSM)
    (u2, p), ((g_glu, g_out),) = _mixer_in_fwd(
        x, sh1, sc1, g_norm_mix, w_in_b, comm=[(_gather_plan, [w_glu[0].astype(BF16), w_out[0].astype(BF16)])])
    w_glu_b = g_glu.transpose(1, 0, 2).reshape(SSM_W, 2 * SSM_W)
    w_out_b = g_out.reshape(D, D)
    u2r = u2.reshape(seq * 2 * bsz, HALF_CH)
    (y2r, xc), ((g_up, g_down),) = _ssm_fwd(
        u2r, bb, cc, lam8, d8, tlen, comm=[(_gather_plan, [w_up[0].astype(BF16), w_down[0].astype(BF16)])])
    w_up_b = g_up.transpose(1, 0, 2).reshape(D, 2 * DFF)
    w_down_b = g_down.reshape(DFF, D)
    y2 = y2r.reshape(seq, bsz * SSM_W)
    w_pool_b = w_pool[0].astype(BF16)
    bp = b_pool[0].reshape(1, POOL_W)
    x1, mixcat = _mixer_out_fwd(y2, p, x, gt1, w_glu_b, b_glu, w_pool_b, bp, pool_scale, w_out_b)
    h2, vq, gq, act, ddn, dx2, loss_l, dg_fin, dgt2 = _ffn_fwd(
        x1, loss_target, sh2, sc2, gt2, g_norm_ffn, w_up_b, w_conv_f, b_conv, w_down_b, g_norm_final.reshape(1, D))
    loss = lax.psum(loss_l[0, 0], ("x", "y", "c"))

    dup, dx1, dsh2, dsc2, dg_ffn, dw_conv, db_conv = _ffn_bwd(
        ddn, gq, vq, x1, dx2, sh2, sc2, g_norm_ffn, w_conv_f, b_conv, w_down_b, w_up_b)
    ntok = bsz * seq
    dw_up = _wgrad(h2.reshape(ntok, D), dup.reshape(ntok, 2 * DFF), FF_CH, "wgrad_up")
    dw_down = _wgrad(act.reshape(ntok, DFF), ddn.reshape(ntok, D), 512, "wgrad_down")
    g42_up = dw_up.reshape(D, N_DEV, 704).transpose(1, 0, 2).reshape(4, 2, D, 704)
    g42_down = dw_down.reshape(4, 2, 352, D)
    (dy2, dp, dw_out, dw_glu, db_glu, dw_pool, db_pool, dpscale, dgt1), ((ra_up, ra_down),) = _mixer_out_bwd(
        dx1, mixcat, y2, p, gt1, w_glu_b, b_glu, w_pool_b, bp, pool_scale, w_out_b,
        comm=[(_pair_plan, [g42_up, g42_down])])
    own_up, s_up = _pair_sum(g42_up, ra_up, place, "pair_sum_up")
    own_down, s_down = _pair_sum(g42_down, ra_down, place, "pair_sum_down")
    g42_glu = dw_glu.reshape(SSM_W, N_DEV, 128).transpose(1, 0, 2).reshape(4, 2, SSM_W, 128)
    g42_out = dw_out.reshape(4, 2, 128, D)
    (du2r, dcc, dbb, dlam8, dd8), ((rc_up, rc_down), (ra_glu, ra_out)) = _ssm_bwd(
        dy2.reshape(u2r.shape), u2r, xc, bb, cc, lam8, d8, tlen,
        comm=[(_chip_plan, [s_up, s_down]), (_pair_plan, [g42_glu, g42_out])])
    big_up = _final_sum_adamw(own_up, rc_up, w_up[0], m_w_up[0], v_w_up[0], "final_adamw_up")
    big_down = _final_sum_adamw(own_down, rc_down, w_down[0], m_w_down[0], v_w_down[0], "final_adamw_down")
    own_glu, s_glu = _pair_sum(g42_glu, ra_glu, place, "pair_sum_glu")
    own_out, s_out = _pair_sum(g42_out, ra_out, place, "pair_sum_out")

    def take_c(t):
        return _blockdiag_take(t, NST, GCH).transpose(0, 2, 1)

    dc_re = jnp.concatenate([take_c(dcc[0:HALF_ST, e * HALF_CH:(e + 1) * HALF_CH]) for e in range(2)], axis=0)
    dc_im = jnp.concatenate([take_c(dcc[HALF_ST:, e * HALF_CH:(e + 1) * HALF_CH]) for e in range(2)], axis=0)

    def take_b(t):
        return _blockdiag_take(t, GCH, NST).transpose(1, 0, 2)

    dbbr = jnp.concatenate([take_b(dbb[e * HALF_CH:(e + 1) * HALF_CH, 0:HALF_ST]) for e in range(2)], axis=1)
    dbbi = jnp.concatenate([take_b(dbb[e * HALF_CH:(e + 1) * HALF_CH, HALF_ST:]) for e in range(2)], axis=1)
    glr, gli, gldt, gbr, gbi, gd = _ssm_param_bwd(
        lam_r, lam_i, ldt, b_r, b_i, dlam8, dbbr.reshape(GCH, GRP * NST), dbbi.reshape(GCH, GRP * NST), dd8)
    g_log_dt = jnp.sum(gldt.reshape(GRP, NST), axis=1)

    def view(a, shp):
        return a.reshape(shp)

    small = [
        ("ssm_lam_re", (GRP, NST), glr.reshape(GRP, NST)), ("ssm_lam_im", (GRP, NST), gli.reshape(GRP, NST)),
        ("ssm_log_dt", (1, GRP), g_log_dt.reshape(1, GRP)),
        ("ssm_c_re", (GRP * GCH, NST), dc_re.reshape(GRP * GCH, NST)),
        ("ssm_c_im", (GRP * GCH, NST), dc_im.reshape(GRP * GCH, NST)),
        ("ssm_d", (1, SSM_W), gd), ("b_glu", (1, 2 * SSM_W), db_glu),
        ("w_pool", (POOL_W, 128), dw_pool.reshape(POOL_W, 128)), ("b_pool", (4, 128), db_pool.reshape(4, 128)),
        ("pool_scale", (1, POOL_W), dpscale), ("g_norm_ffn", (1, D), dg_ffn), ("b_conv", (1, DFF), db_conv),
        ("g_norm_final", (1, D), dg_fin)]
    given = dict(
        ssm_lam_re=(ssm_lam_re, m_ssm_lam_re, v_ssm_lam_re), ssm_lam_im=(ssm_lam_im, m_ssm_lam_im, v_ssm_lam_im),
        ssm_log_dt=(ssm_log_dt, m_ssm_log_dt, v_ssm_log_dt), ssm_c_re=(ssm_c_re, m_ssm_c_re, v_ssm_c_re),
        ssm_c_im=(ssm_c_im, m_ssm_c_im, v_ssm_c_im), ssm_d=(ssm_d, m_ssm_d, v_ssm_d), b_glu=(b_glu, m_b_glu, v_b_glu),
        w_pool=(w_pool, m_w_pool, v_w_pool), b_pool=(b_pool, m_b_pool, v_b_pool),
        pool_scale=(pool_scale, m_pool_scale, v_pool_scale), g_norm_ffn=(g_norm_ffn, m_g_norm_ffn, v_g_norm_ffn),
        b_conv=(b_conv, m_b_conv, v_b_conv), g_norm_final=(g_norm_final, m_g_norm_final, v_g_norm_final),
        ssm_b_re=(ssm_b_re, m_ssm_b_re, v_ssm_b_re), ssm_b_im=(ssm_b_im, m_ssm_b_im, v_ssm_b_im))
    b_view = (GRP * NST, GCH)
    early = [g for _, _, g in small] + [gbr.T, gbi.T, dw_conv]
    (grad_x, dw_in, dsh1, dsc1, dg_mix), ((rc_glu, rc_out), parts) = _mixer_in_bwd(
        du2r.reshape(u2.shape), dp, x, dx1, sh1, sc1, g_norm_mix, w_in_b,
        comm=[(_chip_plan, [s_glu, s_out]), (_gather_plan, early)])
    big_glu = _final_sum_adamw(own_glu, rc_glu, w_glu[0], m_w_glu[0], v_w_glu[0], "final_adamw_glu")
    big_out = _final_sum_adamw(own_out, rc_out, w_out[0], m_w_out[0], v_w_out[0], "final_adamw_out")
    items = [(pt,) + tuple(view(a, shp) for a in given[nm]) for pt, (nm, shp, _) in zip(parts, small)]
    small_out, (g_conv_full,) = _small_sum_adamw(items, [parts[-1]])
    result = {nm: [t.reshape(given[nm][0].shape) for t in quad] for quad, (nm, _, _) in zip(small_out, small)}
    for nm, pt in (("ssm_b_re", parts[-3]), ("ssm_b_im", parts[-2])):
        quad = _sum8_adamw(pt, *[view(a, b_view) for a in given[nm]])
        result[nm] = [t.reshape(given[nm][0].shape) for t in quad]
    g_w_conv = lax.dynamic_slice_in_dim(g_conv_full, 352 * me, 352, axis=1)
    result["w_conv"] = [g_w_conv[None]] + [t[None] for t in _adamw_plain(g_w_conv, w_conv[0], m_w_conv[0], v_w_conv[0])]

    g42_in = dw_in.reshape(4, 2, 128, D)
    (ra_in,) = _comm_call(_pair_plan([g42_in]), [g42_in], "in_grad_pair_exchange")
    own_in, s_in = _pair_sum(g42_in, ra_in, place, "pair_sum_in")
    (rc_in,) = _comm_call(_chip_plan([s_in]), [s_in], "in_grad_chip_exchange")
    big_in = _final_sum_adamw(own_in, rc_in, w_in[0], m_w_in[0], v_w_in[0], "final_adamw_in")
    for nm, quad in (("w_in", big_in), ("w_glu", big_glu), ("w_out", big_out), ("w_up", big_up), ("w_down", big_down)):
        result[nm] = [t[None] for t in quad]

    dmod = jnp.concatenate([t.reshape(bsz, D) for t in (dsh1, dsc1, dgt1, dsh2, dsc2, dgt2)], axis=1)
    dmod_blk = jnp.zeros((N_DEV, 8, ncol), F32).at[:, 0:bsz].set(dmod.reshape(bsz, N_DEV, ncol).transpose(1, 0, 2))
    dmod_blk = dmod_blk.at[0, ADA_RIDER_ROW].set(dg_mix[0, 0:ncol]).at[1, ADA_RIDER_ROW, 0:D - ncol].set(dg_mix[0, ncol:])
    ada = _ada_bwd(dmod_blk.reshape(ADA_ROWS, ncol), c_all, w_ada[0], m_w_ada[0], v_w_ada[0],
                   b_ada.reshape(N_DEV, 1, ncol), m_b_ada.reshape(N_DEV, 1, ncol), v_b_ada.reshape(N_DEV, 1, ncol),
                   g_norm_mix, m_g_norm_mix, v_g_norm_mix)
    result["w_ada"] = [t[None] for t in ada[0:4]]
    result["b_ada"] = [t.reshape(1, N_MOD * D) for t in ada[4:8]]
    result["g_norm_mix"] = list(ada[8:12])

    names = ["w_ada", "b_ada", "g_norm_mix", "w_in", "ssm_lam_re", "ssm_lam_im", "ssm_log_dt", "ssm_b_re", "ssm_b_im",
             "ssm_c_re", "ssm_c_im", "ssm_d", "w_glu", "b_glu", "w_pool", "b_pool", "pool_scale", "w_out", "g_norm_ffn",
             "w_up", "w_conv", "b_conv", "w_down", "g_norm_final"]
    return (loss, grad_x, *[result[nm][k] for k in range(4) for nm in names])
```

```python
import functools
import math

import jax
import jax.numpy as jnp
from jax import lax
from jax.experimental import pallas as pl
from jax.experimental.pallas import tpu as pltpu

F32 = jnp.float32
BF16 = jnp.bfloat16

D = 1024
SSM_W = 512
POOL_W = 512
GRP = 32
GCH = 16
NST = 64
HALF_ST = GRP * NST // 2
HALF_CH = SSM_W // 2
DFF = 2816
FF_CH = 1408
N_MOD = 6
N_DEV = 8
EPS = 1e-6
POOL_WINDOWS = (2, 4, 8, 16)
POOL_HALO = 16
CONV_HALO = 8
GELU_C = math.sqrt(2.0 / math.pi)
GELU_A = 0.044715

ADAM_LR = 0.001
ADAM_B1 = 0.9
ADAM_B2 = 0.999
ADAM_EPS = 1e-08
ADAM_WD = 0.01
ADAM_STEP = 10

VMEM_LIMIT = 56 * 1024 * 1024
TT_MIX = 512
TT_FFN = 256
T_SSM = 128
TT_WGRAD = 2048
MESH = pl.DeviceIdType.MESH
NT = (((1,), (1,)), ((), ()))
TN = (((0,), (0,)), ((), ()))
ANY = pl.BlockSpec(memory_space=pl.ANY)
VMEM = pl.BlockSpec(memory_space=pltpu.VMEM)


def _params(n_grid, vmem=VMEM_LIMIT):
    return pltpu.CompilerParams(dimension_semantics=("arbitrary",) * n_grid, vmem_limit_bytes=vmem)


def _dot(a, b):
    return jnp.dot(a, b, preferred_element_type=F32)


def _dot_nt(a, b):
    return lax.dot_general(a, b, NT, preferred_element_type=F32)


def _dot_tn(a, b):
    return lax.dot_general(a, b, TN, preferred_element_type=F32)


def _colsum(a):
    return jnp.sum(a, axis=0, keepdims=True)


def _rms(x):
    rstd = lax.rsqrt(jnp.mean(x * x, axis=-1, keepdims=True) + EPS)
    return x * rstd, rstd


def _rms_bwd(dxhat, xhat, rstd):
    return rstd * (dxhat - xhat * jnp.mean(dxhat * xhat, axis=-1, keepdims=True))


def _gelu(x):
    return 0.5 * x * (1.0 + jnp.tanh(GELU_C * (x + GELU_A * x * x * x)))


def _gelu_grad(x):
    x2 = x * x
    th = jnp.tanh(GELU_C * (x + GELU_A * x * x2))
    return 0.5 * (1.0 + th) + 0.5 * x * (1.0 - th * th) * GELU_C * (1.0 + 3.0 * GELU_A * x2)


def _adamw(w, g, m, v):
    m = ADAM_B1 * m + (1.0 - ADAM_B1) * g
    v = ADAM_B2 * v + (1.0 - ADAM_B2) * (g * g)
    m_hat = m / (1.0 - ADAM_B1 ** ADAM_STEP)
    v_hat = v / (1.0 - ADAM_B2 ** ADAM_STEP)
    delta = -ADAM_LR * (m_hat / (jnp.sqrt(v_hat) + ADAM_EPS) + ADAM_WD * w)
    return delta, m, v


def _my_place():
    return lax.axis_index("x"), lax.axis_index("y"), lax.axis_index("c")


def _gather_plan(shards):
    n = len(shards)
    out_shape = [jax.ShapeDtypeStruct((N_DEV,) + tuple(s.shape), s.dtype) for s in shards]
    scratch = [pltpu.SemaphoreType.DMA((n, 7)), pltpu.SemaphoreType.DMA((n, 7)), pltpu.SemaphoreType.DMA((n,))]

    def stages(x_refs, out_refs, sems):
        send_sems, recv_sems, local_sems = sems
        x, y, c = _my_place()
        me, sibling = (x, y, c), (x, y, 1 - c)
        chips = [(1 - x, y), (x, 1 - y), (1 - x, 1 - y)]

        def copy(i, k, block, to, own=False):
            px, py, pc = block
            dst = out_refs[i].at[4 * px + 2 * py + pc]
            return pltpu.make_async_remote_copy(
                src_ref=x_refs[i] if own else dst, dst_ref=dst, send_sem=send_sems.at[i, k],
                recv_sem=recv_sems.at[i, k], device_id=to, device_id_type=MESH)

        def mine(i):
            return pltpu.make_async_copy(x_refs[i], out_refs[i].at[4 * x + 2 * y + c], local_sems.at[i])

        def start():
            for i in range(n):
                mine(i).start()
                copy(i, 0, me, sibling, own=True).start()
                for j, chip in enumerate(chips):
                    copy(i, 1 + j, me, (*chip, c), own=True).start()

        def forward():
            for i in range(n):
                for j, chip in enumerate(chips):
                    copy(i, 1 + j, (*chip, c), me).wait_recv()
                    copy(i, 4 + j, (*chip, c), sibling).start()

        def finish():
            for i in range(n):
                copy(i, 0, sibling, me).wait_recv()
                copy(i, 0, me, sibling, own=True).wait_send()
                for j, chip in enumerate(chips):
                    copy(i, 4 + j, (*chip, 1 - c), me).wait_recv()
                    copy(i, 1 + j, me, (*chip, c), own=True).wait_send()
                    copy(i, 4 + j, (*chip, c), sibling).wait_send()
                mine(i).wait()

        return [start, forward, finish]

    return n, out_shape, scratch, stages


def _pair_plan(g42s):
    n = len(g42s)
    out_shape = [jax.ShapeDtypeStruct((4,) + tuple(g.shape[2:]), g.dtype) for g in g42s]
    scratch = [pltpu.SemaphoreType.DMA((n,)), pltpu.SemaphoreType.DMA((n,))]

    def stages(g_refs, out_refs, sems):
        send_sems, recv_sems = sems
        x, y, c = _my_place()

        def copy(i):
            return pltpu.make_async_remote_copy(
                src_ref=g_refs[i].at[:, 1 - c], dst_ref=out_refs[i], send_sem=send_sems.at[i],
                recv_sem=recv_sems.at[i], device_id=(x, y, 1 - c), device_id_type=MESH)

        def start():
            for i in range(n):
                copy(i).start()

        def finish():
            for i in range(n):
                copy(i).wait()

        return [start, finish]

    return n, out_shape, scratch, stages


def _chip_plan(s4s):
    n = len(s4s)
    out_shape = [jax.ShapeDtypeStruct((3,) + tuple(s.shape[1:]), s.dtype) for s in s4s]
    scratch = [pltpu.SemaphoreType.DMA((n, 3)), pltpu.SemaphoreType.DMA((n, 3))]

    def stages(s_refs, out_refs, sems):
        send_sems, recv_sems = sems
        x, y, c = _my_place()

        def copy(i, d):
            px, py = x ^ (d >> 1), y ^ (d & 1)
            return pltpu.make_async_remote_copy(
                src_ref=s_refs[i].at[2 * px + py], dst_ref=out_refs[i].at[d - 1], send_sem=send_sems.at[i, d - 1],
                recv_sem=recv_sems.at[i, d - 1], device_id=(px, py, c), device_id_type=MESH)

        def start():
            for i in range(n):
                for d in (1, 2, 3):
                    copy(i, d).start()

        def finish():
            for i in range(n):
                for d in (1, 2, 3):
                    copy(i, d).wait()

        return [start, finish]

    return n, out_shape, scratch, stages


def _comm_call(plan, arrays, name):
    n, out_shape, scratch, stages = plan

    def body(*refs):
        for stage in stages(refs[:n], refs[n:2 * n], refs[2 * n:]):
            stage()

    return pl.pallas_call(
        body, name=name, out_shape=out_shape, in_specs=[ANY] * n, out_specs=[ANY] * n, scratch_shapes=scratch,
    )(*arrays)


def _fused_call(body, *, name, grid, in_specs, out_specs, out_shape, scratch_shapes, args, comm=None):
    if not comm:
        out = pl.pallas_call(body, name=name, grid=grid, in_specs=in_specs, out_specs=out_specs, out_shape=out_shape,
                             scratch_shapes=scratch_shapes, compiler_params=_params(len(grid)))(*args)
        return out, []
    counts = [plan[0] for plan, _, _ in comm]
    n = sum(counts)
    n_in, n_out, n_scr = len(in_specs), len(out_specs), len(scratch_shapes)

    def fused(*refs):
        ins, refs = refs[:n_in], refs[n_in:]
        c_ins, refs = refs[:n], refs[n:]
        outs, refs = refs[:n_out], refs[n_out:]
        c_outs, refs = refs[:n], refs[n:]
        scr, c_scr = refs[:n_scr], refs[n_scr:]
        step = pl.program_id(0)
        for k in range(1, len(grid)):
            step = step * grid[k] + pl.program_id(k)
        todo, a0, s0 = [], 0, 0
        for (cnt, _, plan_scratch, stages), _, steps in comm:
            sems = c_scr[s0:s0 + len(plan_scratch)]
            todo += list(zip(stages(c_ins[a0:a0 + cnt], c_outs[a0:a0 + cnt], sems), steps))
            a0 += cnt
            s0 += len(plan_scratch)
        for stage, at in todo:
            if at == 0:
                pl.when(step == 0)(stage)
        body(*ins, *outs, *scr)
        for stage, at in todo:
            if at != 0:
                pl.when(step == at)(stage)

    c_shape = [s for plan, _, _ in comm for s in plan[1]]
    c_scratch = [s for plan, _, _ in comm for s in plan[2]]
    arrays = [a for _, arrs, _ in comm for a in arrs]
    out = pl.pallas_call(
        fused, name=name, grid=grid, in_specs=list(in_specs) + [ANY] * n, out_specs=list(out_specs) + [ANY] * n,
        out_shape=list(out_shape) + c_shape, scratch_shapes=list(scratch_shapes) + c_scratch,
        compiler_params=_params(len(grid)))(*args, *arrays)
    outs, c_outs, split, a0 = out[:n_out], out[n_out:], [], 0
    for cnt in counts:
        split.append(c_outs[a0:a0 + cnt])
        a0 += cnt
    return outs, split


def _schedule(comm, n_steps):
    out = []
    for make_plan, arrays in comm or []:
        steps = [0, (3 * n_steps) // 4, n_steps - 1] if make_plan is _gather_plan else [0, n_steps - 1]
        out.append((make_plan(arrays), arrays, steps))
    return out


def _row_tile(r):
    for t in (128, 64, 32, 16, 8):
        if r % t == 0:
            return t
    return r


def _pair_sum(g42, recv, place, name):
    _, _, r, cdim = g42.shape
    tr = _row_tile(r)

    def body(pl_ref, g_ref, r_ref, own_ref, s_ref):
        s_ref[...] = (g_ref[:, 0] + r_ref[...]).astype(BF16)
        q = pl_ref[1]
        own_ref[...] = g_ref[q, 0] + r_ref[q]

    return pl.pallas_call(
        body, name=name,
        grid_spec=pltpu.PrefetchScalarGridSpec(
            num_scalar_prefetch=1, grid=(r // tr,),
            in_specs=[pl.BlockSpec((4, 1, tr, cdim), lambda i, p: (0, p[0], i, 0)),
                      pl.BlockSpec((4, tr, cdim), lambda i, p: (0, i, 0))],
            out_specs=[pl.BlockSpec((tr, cdim), lambda i, p: (i, 0)),
                       pl.BlockSpec((4, tr, cdim), lambda i, p: (0, i, 0))]),
        out_shape=[jax.ShapeDtypeStruct((r, cdim), F32), jax.ShapeDtypeStruct((4, r, cdim), BF16)],
        compiler_params=_params(1),
    )(place, g42, recv)


def _final_sum_adamw(own, recv3, w, m, v, name):
    r, cdim = w.shape
    tr = _row_tile(r)

    def body(s_ref, r_ref, w_ref, m_ref, v_ref, g_out, d_out, m_out, v_out):
        g = s_ref[...] + r_ref[0].astype(F32) + r_ref[1].astype(F32) + r_ref[2].astype(F32)
        d, mn, vn = _adamw(w_ref[...], g, m_ref[...], v_ref[...])
        g_out[...] = g
        d_out[...] = d
        m_out[...] = mn
        v_out[...] = vn

    blk = pl.BlockSpec((tr, cdim), lambda i: (i, 0))
    shp = jax.ShapeDtypeStruct((r, cdim), F32)
    return pl.pallas_call(
        body, name=name, grid=(r // tr,),
        in_specs=[blk, pl.BlockSpec((3, tr, cdim), lambda i: (0, i, 0)), blk, blk, blk],
        out_specs=[blk, blk, blk, blk], out_shape=[shp, shp, shp, shp], compiler_params=_params(1),
    )(own, recv3, w, m, v)


def _small_sum_adamw(items, sums_only):
    n, ne = len(items), len(sums_only)

    def total(p_ref):
        g = p_ref[0]
        for k in range(1, N_DEV):
            g = g + p_ref[k]
        return g

    def body(*refs):
        ins, outs = refs[:4 * n + ne], refs[4 * n + ne:]
        for i in range(n):
            p_ref, w_ref, m_ref, v_ref = ins[4 * i:4 * i + 4]
            g = total(p_ref)
            d, mn, vn = _adamw(w_ref[...], g, m_ref[...], v_ref[...])
            for o_ref, val in zip(outs[4 * i:4 * i + 4], (g, d, mn, vn)):
                o_ref[...] = val
        for j in range(ne):
            outs[4 * n + j][...] = total(ins[4 * n + j])

    args = [a for item in items for a in item] + list(sums_only)
    shapes = [jax.ShapeDtypeStruct(w.shape, F32) for _, w, _, _ in items for _ in range(4)]
    shapes += [jax.ShapeDtypeStruct(p.shape[1:], F32) for p in sums_only]
    out = pl.pallas_call(
        body, name="small_sum_adamw", in_specs=[VMEM] * len(args), out_specs=[VMEM] * len(shapes), out_shape=shapes,
        compiler_params=_params(0),
    )(*args)
    return [out[4 * i:4 * i + 4] for i in range(n)], out[4 * n:]


def _adamw_plain(g, w, m, v):
    r, cdim = w.shape
    tr = _row_tile(r)

    def body(g_ref, w_ref, m_ref, v_ref, d_out, m_out, v_out):
        d, mn, vn = _adamw(w_ref[...], g_ref[...], m_ref[...], v_ref[...])
        d_out[...] = d
        m_out[...] = mn
        v_out[...] = vn

    blk = pl.BlockSpec((tr, cdim), lambda i: (i, 0))
    shp = jax.ShapeDtypeStruct((r, cdim), F32)
    return pl.pallas_call(
        body, name="adamw_plain", grid=(r // tr,), in_specs=[blk, blk, blk, blk],
        out_specs=[blk, blk, blk], out_shape=[shp, shp, shp], compiler_params=_params(1),
    )(g, w, m, v)


ADA_COLS = N_MOD * D // N_DEV
ADA_ROWS = 8 * N_DEV


def _ada_fwd(cpad, w_ada, b_blocks, mixer_shards):
    n_w, w_shape, w_scr, w_stages = _gather_plan(mixer_shards)
    _, _, c_scr, c_stages = _gather_plan([cpad])
    _, _, p_scr, p_stages = _gather_plan([jax.ShapeDtypeStruct((ADA_ROWS, ADA_COLS), F32)])

    def body(c_ref, wa_ref, b_ref, *refs):
        w_refs, refs = refs[:n_w], refs[n_w:]
        cg_ref, call_ref, mod_ref = refs[:3]
        wg_refs, refs = refs[3:3 + n_w], refs[3 + n_w:]
        part_ref, pg_ref = refs[:2]
        c_sems, p_sems, w_sems = refs[2:5], refs[5:8], refs[8:11]
        w_start, w_forward, w_finish = w_stages(w_refs, wg_refs, w_sems)
        w_start()
        for stage in c_stages([c_ref], [cg_ref], c_sems):
            stage()
        cv = cg_ref[:, 0:8, :].reshape(ADA_ROWS, D)
        call_ref[...] = cv
        part_ref[...] = _dot(cv * jax.nn.sigmoid(cv), wa_ref[...])
        for stage in p_stages([part_ref], [pg_ref], p_sems):
            stage()
        x, y, c = _my_place()
        r0 = pl.multiple_of(8 * (4 * x + 2 * y + c), 8)
        for k in range(N_DEV):
            mod_ref[:, k * ADA_COLS:(k + 1) * ADA_COLS] = pg_ref[k, pl.ds(r0, 8), :] + b_ref[k]
        w_forward()
        w_finish()

    out = pl.pallas_call(
        body, name="ada_fwd", in_specs=[VMEM, VMEM, VMEM] + [ANY] * n_w,
        out_specs=[VMEM, VMEM, VMEM] + [ANY] * n_w,
        out_shape=[jax.ShapeDtypeStruct((N_DEV,) + cpad.shape, F32), jax.ShapeDtypeStruct((ADA_ROWS, D), F32),
                   jax.ShapeDtypeStruct((8, N_MOD * D), F32)] + list(w_shape),
        scratch_shapes=[pltpu.VMEM((ADA_ROWS, ADA_COLS), F32), pltpu.VMEM((N_DEV, ADA_ROWS, ADA_COLS), F32)]
        + list(c_scr) + list(p_scr) + list(w_scr),
        compiler_params=_params(0),
    )(cpad, w_ada, b_blocks, *mixer_shards)
    return out[0], out[1], out[2], out[3:]


ADA_RIDER_ROW = 4


def _ada_bwd(dmod_blk, c_all, w_ada, m_w, v_w, b_blocks, m_b, v_b, g_w, g_m, g_v):
    _, _, g_scr, g_stages = _gather_plan([dmod_blk])
    rest = D - ADA_COLS

    def body(dm_ref, c_ref, w_ref, mw_ref, vw_ref, b_ref, mb_ref, vb_ref, gw_ref, gm_ref, gv_ref,
             gw_o, dw_o, mw_o, vw_o, gb_o, dbb_o, mb_o, vb_o, gg_o, dgg_o, mg_o, vg_o, dg_ref, *sems):
        for stage in g_stages([dm_ref], [dg_ref], sems):
            stage()
        x, y, c = _my_place()
        r0 = pl.multiple_of(8 * (4 * x + 2 * y + c), 8)
        cols = dg_ref[:, pl.ds(r0, 8), :].reshape(ADA_ROWS, ADA_COLS)
        cv = c_ref[...]
        gw = _dot_tn(cv * jax.nn.sigmoid(cv), cols)
        d, mn, vn = _adamw(w_ref[...], gw, mw_ref[...], vw_ref[...])
        gw_o[...] = gw
        dw_o[...] = d
        mw_o[...] = mn
        vw_o[...] = vn
        is_example = lax.broadcasted_iota(jnp.int32, (8, 1), 0) < ADA_RIDER_ROW
        blocks = []
        for k in range(N_DEV):
            s = dg_ref[0, 8 * k:8 * k + 8, :]
            for dev in range(1, N_DEV):
                s = s + dg_ref[dev, 8 * k:8 * k + 8, :]
            blocks.append(s)
            gb = _colsum(jnp.where(is_example, s, 0.0))
            d, mn, vn = _adamw(b_ref[k], gb, mb_ref[k], vb_ref[k])
            gb_o[k] = gb
            dbb_o[k] = d
            mb_o[k] = mn
            vb_o[k] = vn
        rider = jnp.concatenate([blocks[0][ADA_RIDER_ROW:ADA_RIDER_ROW + 1, :],
                                 blocks[1][ADA_RIDER_ROW:ADA_RIDER_ROW + 1, 0:rest]], axis=1)
        d, mn, vn = _adamw(gw_ref[...], rider, gm_ref[...], gv_ref[...])
        gg_o[...] = rider
        dgg_o[...] = d
        mg_o[...] = mn
        vg_o[...] = vn

    ws = jax.ShapeDtypeStruct(w_ada.shape, F32)
    bs = jax.ShapeDtypeStruct(b_blocks.shape, F32)
    gs = jax.ShapeDtypeStruct(g_w.shape, F32)
    return pl.pallas_call(
        body, name="ada_bwd", in_specs=[VMEM] * 11, out_specs=[VMEM] * 12,
        out_shape=[ws, ws, ws, ws, bs, bs, bs, bs, gs, gs, gs, gs],
        scratch_shapes=[pltpu.VMEM((N_DEV, ADA_ROWS, ADA_COLS), F32)] + list(g_scr),
        compiler_params=_params(0),
    )(dmod_blk, c_all, w_ada, m_w, v_w, b_blocks, m_b, v_b, g_w, g_m, g_v)


def _ssm_param_fn(lr, li, ldt, br, bi):
    dt = jnp.exp(ldt)
    mag = jnp.exp(lr * dt)
    ang = li * dt
    lbr = mag * jnp.cos(ang)
    lbi = mag * jnp.sin(ang)
    nr = lbr - 1.0
    den = lr * lr + li * li
    cr = (nr * lr + lbi * li) / den
    ci = (lbi * lr - nr * li) / den
    return lbr, lbi, cr * br - ci * bi, cr * bi + ci * br


def _ssm_prep(lr, li, ldt, br, bi):
    def body(lr_ref, li_ref, ldt_ref, br_ref, bi_ref, lbr_o, lbi_o, bbr_o, bbi_o):
        lbr, lbi, bbr, bbi = _ssm_param_fn(lr_ref[...], li_ref[...], ldt_ref[...], br_ref[...], bi_ref[...])
        lbr_o[...] = lbr
        lbi_o[...] = lbi
        bbr_o[...] = bbr
        bbi_o[...] = bbi

    row = jax.ShapeDtypeStruct(lr.shape, F32)
    mat = jax.ShapeDtypeStruct(br.shape, F32)
    return pl.pallas_call(
        body, name="ssm_prep", in_specs=[VMEM] * 5, out_specs=[VMEM] * 4,
        out_shape=[row, row, mat, mat], compiler_params=_params(0),
    )(lr, li, ldt, br, bi)


def _ssm_param_bwd(lr, li, ldt, br, bi, dlam8, dbbr, dbbi, dd8):
    nv = dlam8.shape[1]

    def body(lr_ref, li_ref, ldt_ref, br_ref, bi_ref, dl_ref, dbr_ref, dbi_ref, dd_ref,
             glr_o, gli_o, gldt_o, gbr_o, gbi_o, gd_o):
        halves_r, halves_i, halves_d = [], [], []
        for e in range(2):
            ar = dl_ref[0, e:e + 1, :]
            ai = dl_ref[1, e:e + 1, :]
            ad = dd_ref[e:e + 1, :]
            for b in range(1, nv // 2):
                ar = ar + dl_ref[0, 2 * b + e:2 * b + e + 1, :]
                ai = ai + dl_ref[1, 2 * b + e:2 * b + e + 1, :]
                ad = ad + dd_ref[2 * b + e:2 * b + e + 1, :]
            halves_r.append(ar)
            halves_i.append(ai)
            halves_d.append(ad)
        dlbr = jnp.concatenate(halves_r, axis=1)
        dlbi = jnp.concatenate(halves_i, axis=1)
        gd_o[...] = jnp.concatenate(halves_d, axis=1)
        _, vjp = jax.vjp(_ssm_param_fn, lr_ref[...], li_ref[...], ldt_ref[...], br_ref[...], bi_ref[...])
        glr, gli, gldt, gbr, gbi = vjp((dlbr, dlbi, dbr_ref[...], dbi_ref[...]))
        glr_o[...] = glr
        gli_o[...] = gli
        gldt_o[...] = gldt
        gbr_o[...] = gbr
        gbi_o[...] = gbi

    row = jax.ShapeDtypeStruct(lr.shape, F32)
    mat = jax.ShapeDtypeStruct(br.shape, F32)
    return pl.pallas_call(
        body, name="ssm_param_bwd", in_specs=[VMEM] * 9, out_specs=[VMEM] * 6,
        out_shape=[row, row, row, mat, mat, jax.ShapeDtypeStruct((1, SSM_W), F32)],
        compiler_params=_params(0),
    )(lr, li, ldt, br, bi, dlam8, dbbr, dbbi, dd8)


def _blockdiag(m):
    _, g, a, b = m.shape
    eye = jnp.eye(g, dtype=m.dtype)
    return jnp.einsum("egab,gk->egakb", m, eye).reshape(2, g * a, g * b)


def _blockdiag_take(t, a, b):
    return jnp.einsum("gagb->gab", t.reshape(GRP // 2, a, GRP // 2, b))


def _mixer_in_fwd(x, sh1, sc1, g_mix, w_in_b, comm=None):
    bsz, seq, _ = x.shape
    tt = min(seq, TT_MIX)

    def body(x_ref, sh_ref, sc_ref, g_ref, w_ref, u_ref, p_ref):
        xhat, _ = _rms(x_ref[0])
        h = xhat * g_ref[...] * (1.0 + sc_ref[0]) + sh_ref[0]
        z = _dot(h.astype(BF16), w_ref[...])
        u_ref[...] = z[:, :SSM_W]
        p_ref[0] = z[:, SSM_W:]

    row = pl.BlockSpec((1, 1, D), lambda b, t: (b, 0, 0))
    return _fused_call(
        body, name="mixer_in_fwd", grid=(bsz, seq // tt),
        in_specs=[pl.BlockSpec((1, tt, D), lambda b, t: (b, t, 0)), row, row,
                  pl.BlockSpec((1, D), lambda b, t: (0, 0)), VMEM],
        out_specs=[pl.BlockSpec((tt, SSM_W), lambda b, t: (t, b)),
                   pl.BlockSpec((1, tt, POOL_W), lambda b, t: (b, t, 0))],
        out_shape=[jax.ShapeDtypeStruct((seq, bsz * SSM_W), F32), jax.ShapeDtypeStruct((bsz, seq, POOL_W), F32)],
        scratch_shapes=[], args=(x, sh1, sc1, g_mix, w_in_b), comm=_schedule(comm, bsz * (seq // tt)))


def _ssm_project_in(ub, par0, bb_ref, s_re, s_im, row0, tlen, nv):
    for part, sref in ((0, s_re), (1, s_im)):
        for k in range(HALF_ST // 512):
            c0 = part * HALF_ST + k * 512
            a0 = _dot(ub, bb_ref[:, c0:c0 + 512])
            a1 = _dot(ub, bb_ref[:, 2 * HALF_ST + c0:2 * HALF_ST + c0 + 512])
            sref[pl.ds(row0, tlen), :, k * 512:(k + 1) * 512] = jnp.where(par0, a0, a1).reshape(tlen, nv, 512)


def _ssm_fwd(u2r, bb, cc, lam8, d8, tlen, comm=None):
    nv = lam8.shape[1]
    rows = nv * tlen
    n_chunks = u2r.shape[0] // rows

    def body(u_ref, bb_ref, cc_ref, lam_ref, d_ref, y_ref, xc_ref, s_re, s_im, st):
        @pl.when(pl.program_id(0) == 0)
        def _():
            st[...] = jnp.zeros_like(st)

        xc_ref[0] = st[...]
        u = u_ref[...]
        par0 = (lax.broadcasted_iota(jnp.int32, (rows, 1), 0) % 2) == 0
        _ssm_project_in(u.astype(BF16), par0, bb_ref, s_re, s_im, 0, tlen, nv)
        for hb in range(HALF_ST // 512):
            ls = slice(hb * 512, (hb + 1) * 512)
            lr = lam_ref[0, :, ls]
            li = lam_ref[1, :, ls]

            def step(t, carry, ls=ls, lr=lr, li=li):
                xr, xi = carry
                nr = lr * xr - li * xi + s_re[t, :, ls]
                ni = lr * xi + li * xr + s_im[t, :, ls]
                s_re[t, :, ls] = nr
                s_im[t, :, ls] = ni
                return nr, ni

            xr, xi = lax.fori_loop(0, tlen, step, (st[0, :, ls], st[1, :, ls]), unroll=8)
            st[0, :, ls] = xr
            st[1, :, ls] = xi
        xre = s_re[...].reshape(rows, HALF_ST).astype(BF16)
        xim = s_im[...].reshape(rows, HALF_ST).astype(BF16)
        y2 = _dot(xre, cc_ref[0:HALF_ST, :]) + _dot(xim, cc_ref[HALF_ST:, :])
        y = jnp.where(par0, y2[:, :HALF_CH], y2[:, HALF_CH:])
        skip = (u.reshape(tlen, nv, HALF_CH) * d_ref[...][None]).reshape(rows, HALF_CH)
        y_ref[...] = y + skip

    return _fused_call(
        body, name="ssm_fwd", grid=(n_chunks,),
        in_specs=[pl.BlockSpec((rows, HALF_CH), lambda c: (c, 0)), VMEM, VMEM, VMEM, VMEM],
        out_specs=[pl.BlockSpec((rows, HALF_CH), lambda c: (c, 0)),
                   pl.BlockSpec((1, 2, nv, HALF_ST), lambda c: (c, 0, 0, 0))],
        out_shape=[jax.ShapeDtypeStruct(u2r.shape, F32), jax.ShapeDtypeStruct((n_chunks, 2, nv, HALF_ST), F32)],
        scratch_shapes=[pltpu.VMEM((tlen, nv, HALF_ST), F32), pltpu.VMEM((tlen, nv, HALF_ST), F32),
                        pltpu.VMEM((2, nv, HALF_ST), F32)],
        args=(u2r, bb, cc, lam8, d8), comm=_schedule(comm, n_chunks))


def _pool_forward(ext, pv, pos, wp_ref, bp_ref):
    cur = ext
    zs, zls = [], []
    for gi, w in enumerate(POOL_WINDOWS):
        cur = cur + pltpu.roll(cur, w // 2, 0)
        sw = cur[POOL_HALO:, 0:128]
        z = sw / jnp.minimum(pos, float(w)) - pv[:, gi * 128:(gi + 1) * 128]
        zs.append(z)
        zls.append(_dot(z.astype(BF16), wp_ref[gi]) + bp_ref[:, gi * 128:(gi + 1) * 128])
        if gi + 1 < len(POOL_WINDOWS):
            cur = cur[:, 128:]
    return zs, zls


def _mixer_out_fwd(y2, p, x, gt1, w_glu_b, b_glu, w_pool_b, b_pool, pscale, w_out_b):
    bsz, seq, _ = x.shape
    tt = min(seq, TT_MIX)

    def body(y_ref, p_ref, x_ref, gt_ref, wg_ref, bg_ref, wp_ref, bp_ref, ps_ref, wo_ref, x1_ref, mix_ref, ext):
        ti = pl.program_id(1)

        @pl.when(ti == 0)
        def _():
            ext[0:POOL_HALO, :] = jnp.zeros((POOL_HALO, POOL_W), F32)

        pv = p_ref[0]
        ext[POOL_HALO:, :] = pv
        pos = (ti * tt + lax.broadcasted_iota(jnp.int32, (tt, 1), 0) + 1).astype(F32)
        _, zls = _pool_forward(ext[...], pv, pos, wp_ref, bp_ref)
        ext[0:POOL_HALO, :] = pv[tt - POOL_HALO:, :]
        a = _gelu(y_ref[...])
        gl = _dot(a.astype(BF16), wg_ref[...]) + bg_ref[...]
        y_ssm = gl[:, :SSM_W] * jax.nn.sigmoid(gl[:, SSM_W:])
        y_pool = [zl * ps_ref[:, gi * 128:(gi + 1) * 128] for gi, zl in enumerate(zls)]
        mixcat = jnp.concatenate([y_ssm] + y_pool, axis=1).astype(BF16)
        mix_ref[0] = mixcat
        x1_ref[0] = x_ref[0] + gt_ref[0] * _dot(mixcat, wo_ref[...])

    xt = pl.BlockSpec((1, tt, D), lambda b, t: (b, t, 0))
    return pl.pallas_call(
        body, name="mixer_out_fwd", grid=(bsz, seq // tt),
        in_specs=[pl.BlockSpec((tt, SSM_W), lambda b, t: (t, b)),
                  pl.BlockSpec((1, tt, POOL_W), lambda b, t: (b, t, 0)), xt,
                  pl.BlockSpec((1, 1, D), lambda b, t: (b, 0, 0)), VMEM, VMEM, VMEM, VMEM, VMEM, VMEM],
        out_specs=[xt, xt],
        out_shape=[jax.ShapeDtypeStruct(x.shape, F32), jax.ShapeDtypeStruct(x.shape, BF16)],
        scratch_shapes=[pltpu.VMEM((POOL_HALO + tt, POOL_W), F32)],
        compiler_params=_params(2),
    )(y2, p, x, gt1, w_glu_b, b_glu, w_pool_b, b_pool, pscale, w_out_b)


def _conv_gate(g, ge, wc, bc):
    g1 = pltpu.roll(ge, 1, 0)[CONV_HALO:]
    g2 = pltpu.roll(ge, 2, 0)[CONV_HALO:]
    return wc[2:3] * g + wc[1:2] * g1 + wc[0:1] * g2 + bc, g1, g2


def _ffn_fwd(x1, tgt, sh2, sc2, gt2, g_ffn, w_up_b, w_conv, b_conv, w_down_b, g_fin):
    bsz, seq, _ = x1.shape
    tt = min(seq, TT_FFN)
    n_t = seq // tt
    n_ck = DFF // FF_CH

    def body(x1_ref, tg_ref, sh_ref, sc_ref, gt_ref, gf_ref, wu_ref, wc_ref, bc_ref, wd_ref, gfin_ref,
             h2_ref, v_ref, g_ref, gc_ref, act_ref, ddn_ref, dx2_ref, loss_ref, dgfin_ref, dgt_ref, gext, lacc):
        b = pl.program_id(0)
        ti = pl.program_id(1)

        @pl.when((b == 0) & (ti == 0))
        def _():
            lacc[...] = jnp.zeros_like(lacc)
            dgfin_ref[...] = jnp.zeros_like(dgfin_ref)

        @pl.when(ti == 0)
        def _():
            dgt_ref[...] = jnp.zeros_like(dgt_ref)
            gext[:, 0:CONV_HALO, :] = jnp.zeros((n_ck, CONV_HALO, FF_CH), F32)

        x1v = x1_ref[0]
        xhat, _ = _rms(x1v)
        h2b = (xhat * gf_ref[...] * (1.0 + sc_ref[0]) + sh_ref[0]).astype(BF16)
        h2_ref[0] = h2b
        dn = jnp.zeros((tt, D), F32)
        for ck in range(n_ck):
            c0 = ck * FF_CH
            v = _dot_nt(h2b, wu_ref[c0:c0 + FF_CH, :])
            g = _dot_nt(h2b, wu_ref[DFF + c0:DFF + c0 + FF_CH, :])
            v_ref[0, :, c0:c0 + FF_CH] = v.astype(BF16)
            g_ref[0, :, c0:c0 + FF_CH] = g.astype(BF16)
            gext[ck, CONV_HALO:, :] = g
            gc, _, _ = _conv_gate(g, gext[ck], wc_ref[:, c0:c0 + FF_CH], bc_ref[:, c0:c0 + FF_CH])
            gext[ck, 0:CONV_HALO, :] = g[tt - CONV_HALO:, :]
            gc_ref[0, :, c0:c0 + FF_CH] = gc.astype(BF16)
            actb = (gc * jax.nn.sigmoid(gc) * v).astype(BF16)
            act_ref[0, :, c0:c0 + FF_CH] = actb
            dn = dn + _dot(actb, wd_ref[c0:c0 + FF_CH, :])
        gt = gt_ref[0]
        xh3, r3 = _rms(x1v + gt * dn)
        gfin = gfin_ref[...]
        diff = xh3 * gfin - tg_ref[0]
        lacc[...] += _colsum(diff * diff)
        dy = diff * (1.0 / D)
        dgfin_ref[...] += _colsum(dy * xh3)
        dx2 = _rms_bwd(dy * gfin, xh3, r3)
        dx2_ref[0] = dx2
        dgt_ref[0] += _colsum(dx2 * dn)
        ddn_ref[0] = (gt * dx2).astype(BF16)

        @pl.when((b == bsz - 1) & (ti == n_t - 1))
        def _():
            loss_ref[...] = jnp.full(loss_ref.shape, 0.5 / D * jnp.sum(lacc[...]), F32)

    xt = pl.BlockSpec((1, tt, D), lambda b, t: (b, t, 0))
    ft = pl.BlockSpec((1, tt, DFF), lambda b, t: (b, t, 0))
    row = pl.BlockSpec((1, 1, D), lambda b, t: (b, 0, 0))
    vec = pl.BlockSpec((1, D), lambda b, t: (0, 0))
    ff = jax.ShapeDtypeStruct((bsz, seq, DFF), BF16)
    xs = jax.ShapeDtypeStruct((bsz, seq, D), BF16)
    return pl.pallas_call(
        body, name="ffn_fwd", grid=(bsz, n_t),
        in_specs=[xt, xt, row, row, row, vec, VMEM, VMEM, VMEM, VMEM, vec],
        out_specs=[xt, ft, ft, ft, ft, xt, xt, pl.BlockSpec((1, 128), lambda b, t: (0, 0)), vec, row],
        out_shape=[xs, ff, ff, ff, ff, xs, jax.ShapeDtypeStruct((bsz, seq, D), F32),
                   jax.ShapeDtypeStruct((1, 128), F32), jax.ShapeDtypeStruct((1, D), F32),
                   jax.ShapeDtypeStruct((bsz, 1, D), F32)],
        scratch_shapes=[pltpu.VMEM((n_ck, CONV_HALO + tt, FF_CH), F32), pltpu.VMEM((1, D), F32)],
        compiler_params=_params(2),
    )(x1, tgt, sh2, sc2, gt2, g_ffn, w_up_b, w_conv, b_conv, w_down_b, g_fin)


def _ffn_bwd(ddn, gq, gcq, vq, x1, dx2, sh2, sc2, g_ffn, w_conv, w_down_b, w_up_b):
    bsz, seq, _ = x1.shape
    tt = min(seq, TT_FFN)
    n_t = seq // tt
    n_ck = DFF // FF_CH
    ext_rows = tt + CONV_HALO

    def body(ddn_ref, g_ref, gc_ref, v_ref, x1_ref, dx2_ref, sh_ref, sc_ref, gf_ref, wc_ref, wd_ref,
             wu_ref, dup_ref, dx1_ref, dsh_ref, dsc_ref, dgf_ref, dwc_ref, dbc_ref, dext):
        b = pl.program_id(0)
        i = pl.program_id(1)

        @pl.when((b == 0) & (i == 0))
        def _():
            dgf_ref[...] = jnp.zeros_like(dgf_ref)
            dwc_ref[...] = jnp.zeros_like(dwc_ref)
            dbc_ref[...] = jnp.zeros_like(dbc_ref)

        @pl.when(i == 0)
        def _():
            dsh_ref[...] = jnp.zeros_like(dsh_ref)
            dsc_ref[...] = jnp.zeros_like(dsc_ref)
            dext[:, tt:, :] = jnp.zeros((n_ck, CONV_HALO, FF_CH), F32)

        ddnv = ddn_ref[0]
        dh2 = jnp.zeros((tt, D), F32)
        for ck in range(n_ck):
            c0 = ck * FF_CH
            dact = _dot_nt(ddnv, wd_ref[c0:c0 + FF_CH, :])
            g = g_ref[0, :, c0:c0 + FF_CH].astype(F32)
            gc = gc_ref[0, :, c0:c0 + FF_CH].astype(F32)
            v = v_ref[0, :, c0:c0 + FF_CH].astype(F32)
            wc = wc_ref[:, c0:c0 + FF_CH]
            sg = jax.nn.sigmoid(gc)
            silu = gc * sg
            dv = dact * silu
            dgc = dact * v * (sg + silu * (1.0 - sg))
            dext[ck, 0:tt, :] = dgc
            de = dext[ck]
            d1 = pltpu.roll(de, ext_rows - 1, 0)[0:tt]
            d2 = pltpu.roll(de, ext_rows - 2, 0)[0:tt]
            dext[ck, tt:, :] = dgc[0:CONV_HALO, :]
            dbc_ref[:, c0:c0 + FF_CH] += _colsum(dgc)
            dwc_ref[0:1, c0:c0 + FF_CH] += _colsum(d2 * g)
            dwc_ref[1:2, c0:c0 + FF_CH] += _colsum(d1 * g)
            dwc_ref[2:3, c0:c0 + FF_CH] += _colsum(dgc * g)
            dg = wc[2:3] * dgc + wc[1:2] * d1 + wc[0:1] * d2
            dvb = dv.astype(BF16)
            dgb = dg.astype(BF16)
            dup_ref[0, :, c0:c0 + FF_CH] = dvb
            dup_ref[0, :, DFF + c0:DFF + c0 + FF_CH] = dgb
            dh2 = dh2 + _dot(dvb, wu_ref[c0:c0 + FF_CH, :]) + _dot(dgb, wu_ref[DFF + c0:DFF + c0 + FF_CH, :])
        xhat, rstd = _rms(x1_ref[0])
        gf = gf_ref[...]
        dsh_ref[0] += _colsum(dh2)
        dsc_ref[0] += _colsum(dh2 * xhat * gf)
        t = dh2 * (1.0 + sc_ref[0])
        dgf_ref[...] += _colsum(t * xhat)
        dx1_ref[0] = dx2_ref[0] + _rms_bwd(t * gf, xhat, rstd)

    def rev(b, t):
        return (b, n_t - 1 - t, 0)

    xt = pl.BlockSpec((1, tt, D), rev)
    ft = pl.BlockSpec((1, tt, DFF), rev)
    row = pl.BlockSpec((1, 1, D), lambda b, t: (b, 0, 0))
    vec = pl.BlockSpec((1, D), lambda b, t: (0, 0))
    rows = jax.ShapeDtypeStruct((bsz, 1, D), F32)
    return pl.pallas_call(
        body, name="ffn_bwd", grid=(bsz, n_t),
        in_specs=[xt, ft, ft, ft, xt, xt, row, row, vec, VMEM, VMEM, VMEM],
        out_specs=[pl.BlockSpec((1, tt, 2 * DFF), rev), xt, row, row, vec,
                   pl.BlockSpec((3, DFF), lambda b, t: (0, 0)), pl.BlockSpec((1, DFF), lambda b, t: (0, 0))],
        out_shape=[jax.ShapeDtypeStruct((bsz, seq, 2 * DFF), BF16), jax.ShapeDtypeStruct((bsz, seq, D), F32),
                   rows, rows, jax.ShapeDtypeStruct((1, D), F32), jax.ShapeDtypeStruct((3, DFF), F32),
                   jax.ShapeDtypeStruct((1, DFF), F32)],
        scratch_shapes=[pltpu.VMEM((n_ck, ext_rows, FF_CH), F32)],
        compiler_params=_params(2),
    )(ddn, gq, gcq, vq, x1, dx2, sh2, sc2, g_ffn, w_conv, w_down_b, w_up_b)


def _wgrad(a, b, bk1, bk2, name):
    n, k1 = a.shape
    _, k2 = b.shape
    tt = min(n, TT_WGRAD)

    def body(a_ref, b_ref, o_ref):
        @pl.when(pl.program_id(2) == 0)
        def _():
            o_ref[...] = jnp.zeros_like(o_ref)

        o_ref[...] += _dot_tn(a_ref[...], b_ref[...])

    return pl.pallas_call(
        body, name=name, grid=(k1 // bk1, k2 // bk2, n // tt),
        in_specs=[pl.BlockSpec((tt, bk1), lambda h, j, i: (i, h)), pl.BlockSpec((tt, bk2), lambda h, j, i: (i, j))],
        out_specs=pl.BlockSpec((bk1, bk2), lambda h, j, i: (h, j)),
        out_shape=jax.ShapeDtypeStruct((k1, k2), F32), compiler_params=_params(3),
    )(a, b)


def _mixer_out_bwd(dx1, mixcat, y2, p, gt1, w_glu_b, b_glu, w_pool_b, b_pool, pscale, w_out_b, comm=None):
    bsz, seq, _ = dx1.shape
    tt = min(seq, TT_MIX)
    n_t = seq // tt
    ext_rows = tt + POOL_HALO

    def body(dx1_ref, mc_ref, y_ref, p_ref, ph_ref, gt_ref, wg_ref, bg_ref, wp_ref, bp_ref, ps_ref, wo_ref,
             dy_ref, dp_ref, dwo_ref, dwg_ref, dbg_ref, dwp_ref, dbp_ref, dps_ref, dgt_ref, ext, qext):
        b = pl.program_id(0)
        i = pl.program_id(1)
        tile = n_t - 1 - i

        @pl.when((b == 0) & (i == 0))
        def _():
            for r in (dwo_ref, dwg_ref, dbg_ref, dwp_ref, dbp_ref, dps_ref):
                r[...] = jnp.zeros_like(r)

        @pl.when(i == 0)
        def _():
            dgt_ref[...] = jnp.zeros_like(dgt_ref)
            qext[tt:, :] = jnp.zeros((POOL_HALO, POOL_W), F32)

        dx1v = dx1_ref[0]
        mc = mc_ref[0]
        dgt_ref[0] += _colsum(dx1v * _dot(mc, wo_ref[...]))
        dmixed = (gt_ref[0] * dx1v).astype(BF16)
        dwo_ref[...] += _dot_tn(mc, dmixed)
        dmc = _dot_nt(dmixed, wo_ref[...])
        pv = p_ref[0]
        ext[0:POOL_HALO, :] = ph_ref[0] * (tile > 0).astype(F32)
        ext[POOL_HALO:, :] = pv
        pos = (tile * tt + lax.broadcasted_iota(jnp.int32, (tt, 1), 0) + 1).astype(F32)
        zs, zls = _pool_forward(ext[...], pv, pos, wp_ref, bp_ref)
        dzs = []
        for gi, w in enumerate(POOL_WINDOWS):
            cs = slice(gi * 128, (gi + 1) * 128)
            dyp = dmc[:, SSM_W + gi * 128:SSM_W + (gi + 1) * 128]
            dps_ref[:, cs] += _colsum(dyp * zls[gi])
            dzl = dyp * ps_ref[:, cs]
            dbp_ref[:, cs] += _colsum(dzl)
            dzlb = dzl.astype(BF16)
            dwp_ref[gi] += _dot_tn(zs[gi].astype(BF16), dzlb)
            dz = _dot_nt(dzlb, wp_ref[gi])
            dzs.append(dz)
            qext[0:tt, cs] = dz / jnp.minimum(pos, float(w))
        cur = qext[...]
        dps = []
        for gi, w in enumerate(POOL_WINDOWS):
            cur = cur + pltpu.roll(cur, ext_rows - w // 2, 0)
            dps.append(cur[0:tt, 0:128] - dzs[gi])
            if gi + 1 < len(POOL_WINDOWS):
                cur = cur[:, 128:]
        qhead = qext[0:POOL_HALO, :]
        qext[tt:, :] = qhead
        dp_ref[0] = jnp.concatenate(dps, axis=1)
        yv = y_ref[...]
        ab = _gelu(yv).astype(BF16)
        gl = _dot(ab, wg_ref[...]) + bg_ref[...]
        val = gl[:, :SSM_W]
        sg = jax.nn.sigmoid(gl[:, SSM_W:])
        dys = dmc[:, :SSM_W]
        dgl = jnp.concatenate([dys * sg, dys * val * sg * (1.0 - sg)], axis=1)
        dbg_ref[...] += _colsum(dgl)
        dglb = dgl.astype(BF16)
        dwg_ref[...] += _dot_tn(ab, dglb)
        dy_ref[...] = _dot_nt(dglb, wg_ref[...]) * _gelu_grad(yv)

    def rev(b, t):
        return (b, n_t - 1 - t, 0)

    def halo(b, t):
        return (b, jnp.maximum((n_t - 1 - t) * (tt // POOL_HALO) - 1, 0), 0)

    xt = pl.BlockSpec((1, tt, D), rev)
    pt = pl.BlockSpec((1, tt, POOL_W), rev)
    yt = pl.BlockSpec((tt, SSM_W), lambda b, t: (n_t - 1 - t, b))

    def whole(shape):
        return pl.BlockSpec(shape, lambda b, t: (0,) * len(shape))

    return _fused_call(
        body, name="mixer_out_bwd", grid=(bsz, n_t),
        in_specs=[xt, xt, yt, pt, pl.BlockSpec((1, POOL_HALO, POOL_W), halo),
                  pl.BlockSpec((1, 1, D), lambda b, t: (b, 0, 0)), VMEM, VMEM, VMEM, VMEM, VMEM, VMEM],
        out_specs=[yt, pt, whole((D, D)), whole((SSM_W, 2 * SSM_W)), whole((1, 2 * SSM_W)),
                   whole((4, 128, 128)), whole((1, POOL_W)), whole((1, POOL_W)),
                   pl.BlockSpec((1, 1, D), lambda b, t: (b, 0, 0))],
        out_shape=[jax.ShapeDtypeStruct(y2.shape, F32), jax.ShapeDtypeStruct(p.shape, F32),
                   jax.ShapeDtypeStruct((D, D), F32), jax.ShapeDtypeStruct((SSM_W, 2 * SSM_W), F32),
                   jax.ShapeDtypeStruct((1, 2 * SSM_W), F32), jax.ShapeDtypeStruct((4, 128, 128), F32),
                   jax.ShapeDtypeStruct((1, POOL_W), F32), jax.ShapeDtypeStruct((1, POOL_W), F32),
                   jax.ShapeDtypeStruct((bsz, 1, D), F32)],
        scratch_shapes=[pltpu.VMEM((POOL_HALO + tt, POOL_W), F32), pltpu.VMEM((ext_rows, POOL_W), F32)],
        args=(dx1, mixcat, y2, p, p, gt1, w_glu_b, b_glu, w_pool_b, b_pool, pscale, w_out_b),
        comm=_schedule(comm, bsz * n_t))


def _ssm_bwd(dy2r, u2r, xc, bb, cc, lam8, d8, tlen, comm=None):
    nv = lam8.shape[1]
    rows = nv * tlen
    n_chunks = u2r.shape[0] // rows

    def body(dy_ref, u_ref, xc_ref, bb_ref, cc_ref, lam_ref, d_ref, du_ref, dcc_ref, dbb_ref, dlam_ref, dd_ref,
             s_re, s_im, g_re, g_im, gst):
        i = pl.program_id(0)

        @pl.when(i == 0)
        def _():
            for r in (gst, dcc_ref, dbb_ref, dlam_ref, dd_ref):
                r[...] = jnp.zeros_like(r)

        u = u_ref[...]
        dy = dy_ref[...]
        par0 = (lax.broadcasted_iota(jnp.int32, (rows, 1), 0) % 2) == 0
        s_re[0] = xc_ref[0, 0]
        s_im[0] = xc_ref[0, 1]
        _ssm_project_in(u.astype(BF16), par0, bb_ref, s_re, s_im, 1, tlen, nv)
        for hb in range(HALF_ST // 512):
            ls = slice(hb * 512, (hb + 1) * 512)
            lr = lam_ref[0, :, ls]
            li = lam_ref[1, :, ls]

            def fstep(t, carry, ls=ls, lr=lr, li=li):
                xr, xi = carry
                nr = lr * xr - li * xi + s_re[t + 1, :, ls]
                ni = lr * xi + li * xr + s_im[t + 1, :, ls]
                s_re[t + 1, :, ls] = nr
                s_im[t + 1, :, ls] = ni
                return nr, ni

            lax.fori_loop(0, tlen, fstep, (s_re[0, :, ls], s_im[0, :, ls]), unroll=8)
        zero = jnp.zeros_like(dy)
        dy2 = jnp.concatenate([jnp.where(par0, dy, zero), jnp.where(par0, zero, dy)], axis=1).astype(BF16)
        u2 = jnp.concatenate([jnp.where(par0, u, zero), jnp.where(par0, zero, u)], axis=1).astype(BF16)
        xre = s_re[pl.ds(1, tlen)].reshape(rows, HALF_ST).astype(BF16)
        xim = s_im[pl.ds(1, tlen)].reshape(rows, HALF_ST).astype(BF16)
        dcc_ref[0:HALF_ST, :] += _dot_tn(xre, dy2)
        dcc_ref[HALF_ST:, :] += _dot_tn(xim, dy2)
        for part, gref in ((0, g_re), (1, g_im)):
            for k in range(HALF_ST // 512):
                r0 = part * HALF_ST + k * 512
                gref[:, :, k * 512:(k + 1) * 512] = _dot_nt(dy2, cc_ref[r0:r0 + 512, :]).reshape(tlen, nv, 512)
        for hb in range(HALF_ST // 512):
            ls = slice(hb * 512, (hb + 1) * 512)
            lr = lam_ref[0, :, ls]
            li = lam_ref[1, :, ls]

            def bstep(k, carry, ls=ls, lr=lr, li=li):
                t = tlen - 1 - k
                gr, gi, ar, ai = carry
                ngr = g_re[t, :, ls] + lr * gr + li * gi
                ngi = g_im[t, :, ls] + lr * gi - li * gr
                g_re[t, :, ls] = ngr
                g_im[t, :, ls] = ngi
                xpr = s_re[t, :, ls]
                xpi = s_im[t, :, ls]
                return ngr, ngi, ar + ngr * xpr + ngi * xpi, ai + ngi * xpr - ngr * xpi

            init = (gst[0, :, ls], gst[1, :, ls], dlam_ref[0, :, ls], dlam_ref[1, :, ls])
            gr, gi, ar, ai = lax.fori_loop(0, tlen, bstep, init, unroll=4)
            gst[0, :, ls] = gr
            gst[1, :, ls] = gi
            dlam_ref[0, :, ls] = ar
            dlam_ref[1, :, ls] = ai
        gre = g_re[...].reshape(rows, HALF_ST).astype(BF16)
        gim = g_im[...].reshape(rows, HALF_ST).astype(BF16)
        du0 = _dot_nt(gre, bb_ref[:, 0:HALF_ST]) + _dot_nt(gim, bb_ref[:, HALF_ST:2 * HALF_ST])
        du1 = _dot_nt(gre, bb_ref[:, 2 * HALF_ST:3 * HALF_ST]) + _dot_nt(gim, bb_ref[:, 3 * HALF_ST:])
        skip = (dy.reshape(tlen, nv, HALF_CH) * d_ref[...][None]).reshape(rows, HALF_CH)
        du_ref[...] = jnp.where(par0, du0, du1) + skip
        dbb_ref[:, 0:HALF_ST] += _dot_tn(u2, gre)
        dbb_ref[:, HALF_ST:] += _dot_tn(u2, gim)
        dd_ref[...] += jnp.sum((dy * u).reshape(tlen, nv, HALF_CH), axis=0)

        @pl.when(i == n_chunks - 1)
        def _():
            dcc_ref[HALF_ST:, :] = -dcc_ref[HALF_ST:, :]

    def rev(c):
        return (n_chunks - 1 - c, 0)

    def whole(shape):
        return pl.BlockSpec(shape, lambda c: (0,) * len(shape))

    blk = pl.BlockSpec((rows, HALF_CH), rev)
    return _fused_call(
        body, name="ssm_bwd", grid=(n_chunks,),
        in_specs=[blk, blk, pl.BlockSpec((1, 2, nv, HALF_ST), lambda c: (n_chunks - 1 - c, 0, 0, 0)),
                  VMEM, VMEM, VMEM, VMEM],
        out_specs=[blk, whole((2 * HALF_ST, SSM_W)), whole((SSM_W, 2 * HALF_ST)), whole((2, nv, HALF_ST)),
                   whole((nv, HALF_CH))],
        out_shape=[jax.ShapeDtypeStruct(u2r.shape, F32), jax.ShapeDtypeStruct((2 * HALF_ST, SSM_W), F32),
                   jax.ShapeDtypeStruct((SSM_W, 2 * HALF_ST), F32), jax.ShapeDtypeStruct((2, nv, HALF_ST), F32),
                   jax.ShapeDtypeStruct((nv, HALF_CH), F32)],
        scratch_shapes=[pltpu.VMEM((tlen + 1, nv, HALF_ST), F32), pltpu.VMEM((tlen + 1, nv, HALF_ST), F32),
                        pltpu.VMEM((tlen, nv, HALF_ST), F32), pltpu.VMEM((tlen, nv, HALF_ST), F32),
                        pltpu.VMEM((2, nv, HALF_ST), F32)],
        args=(dy2r, u2r, xc, bb, cc, lam8, d8), comm=_schedule(comm, n_chunks))


def _mixer_in_bwd(du2, dp, x, dx1, sh1, sc1, g_mix, w_in_b, comm=None):
    bsz, seq, _ = x.shape
    tt = min(seq, TT_MIX)

    def body(du_ref, dp_ref, x_ref, dx1_ref, sh_ref, sc_ref, g_ref, w_ref,
             dx_ref, dw_ref, dsh_ref, dsc_ref, dg_ref):
        b = pl.program_id(0)
        ti = pl.program_id(1)

        @pl.when((b == 0) & (ti == 0))
        def _():
            dw_ref[...] = jnp.zeros_like(dw_ref)
            dg_ref[...] = jnp.zeros_like(dg_ref)

        @pl.when(ti == 0)
        def _():
            dsh_ref[...] = jnp.zeros_like(dsh_ref)
            dsc_ref[...] = jnp.zeros_like(dsc_ref)

        dz = jnp.concatenate([du_ref[...], dp_ref[0]], axis=1).astype(BF16)
        xhat, rstd = _rms(x_ref[0])
        g = g_ref[...]
        sc = sc_ref[0]
        a = xhat * g
        h = (a * (1.0 + sc) + sh_ref[0]).astype(BF16)
        dw_ref[...] += _dot_tn(h, dz)
        dh = _dot_nt(dz, w_ref[...])
        dsh_ref[0] += _colsum(dh)
        dsc_ref[0] += _colsum(dh * a)
        t = dh * (1.0 + sc)
        dg_ref[...] += _colsum(t * xhat)
        dx_ref[0] = dx1_ref[0] + _rms_bwd(t * g, xhat, rstd)

    xt = pl.BlockSpec((1, tt, D), lambda b, t: (b, t, 0))
    row = pl.BlockSpec((1, 1, D), lambda b, t: (b, 0, 0))
    vec = pl.BlockSpec((1, D), lambda b, t: (0, 0))
    rows = jax.ShapeDtypeStruct((bsz, 1, D), F32)
    return _fused_call(
        body, name="mixer_in_bwd", grid=(bsz, seq // tt),
        in_specs=[pl.BlockSpec((tt, SSM_W), lambda b, t: (t, b)),
                  pl.BlockSpec((1, tt, POOL_W), lambda b, t: (b, t, 0)), xt, xt, row, row, vec, VMEM],
        out_specs=[xt, pl.BlockSpec((D, D), lambda b, t: (0, 0)), row, row, vec],
        out_shape=[jax.ShapeDtypeStruct(x.shape, F32), jax.ShapeDtypeStruct((D, D), F32), rows, rows,
                   jax.ShapeDtypeStruct((1, D), F32)],
        scratch_shapes=[], args=(du2, dp, x, dx1, sh1, sc1, g_mix, w_in_b),
        comm=_schedule(comm, bsz * (seq // tt)))


def kernel(x, c, w_ada, b_ada, g_norm_mix, w_in, ssm_lam_re, ssm_lam_im, ssm_log_dt, ssm_b_re, ssm_b_im, ssm_c_re, ssm_c_im, ssm_d, w_glu, b_glu, w_pool, b_pool, pool_scale, w_out, g_norm_ffn, w_up, w_conv, b_conv, w_down, g_norm_final, loss_target, m_w_ada, m_b_ada, m_g_norm_mix, m_w_in, m_ssm_lam_re, m_ssm_lam_im, m_ssm_log_dt, m_ssm_b_re, m_ssm_b_im, m_ssm_c_re, m_ssm_c_im, m_ssm_d, m_w_glu, m_b_glu, m_w_pool, m_b_pool, m_pool_scale, m_w_out, m_g_norm_ffn, m_w_up, m_w_conv, m_b_conv, m_w_down, m_g_norm_final, v_w_ada, v_b_ada, v_g_norm_mix, v_w_in, v_ssm_lam_re, v_ssm_lam_im, v_ssm_log_dt, v_ssm_b_re, v_ssm_b_im, v_ssm_c_re, v_ssm_c_im, v_ssm_d, v_w_glu, v_b_glu, v_w_pool, v_b_pool, v_pool_scale, v_w_out, v_g_norm_ffn, v_w_up, v_w_conv, v_b_conv, v_w_down, v_g_norm_final):
    bsz, seq, _ = x.shape
    assert 2 * bsz == 8 and seq % 128 == 0
    px, py, pc = _my_place()
    me = 4 * px + 2 * py + pc
    place = jnp.stack([pc, 2 * px + py]).astype(jnp.int32)
    ncol = ADA_COLS

    cpad = jnp.zeros((16, D), F32).at[0:bsz].set(c).at[8:11, 0:352].set(w_conv[0])
    cg, c_all, mod8, (g_in,) = _ada_fwd(cpad, w_ada[0], b_ada.reshape(N_DEV, 1, ncol), [w_in[0].astype(BF16)])
    w_conv_f = cg[:, 8:11, 0:352].transpose(1, 0, 2).reshape(3, DFF)
    w_in_b = g_in.reshape(D, D)
    sh1, sc1, gt1, sh2, sc2, gt2 = [mod8[0:bsz, k * D:(k + 1) * D].reshape(bsz, 1, D) for k in range(N_MOD)]

    lam_r = ssm_lam_re[0].reshape(1, GRP * NST)
    lam_i = ssm_lam_im[0].reshape(1, GRP * NST)
    ldt = jnp.repeat(ssm_log_dt[0], NST).reshape(1, GRP * NST)
    b_r = ssm_b_re[0].transpose(2, 0, 1).reshape(GCH, GRP * NST)
    b_i = ssm_b_im[0].transpose(2, 0, 1).reshape(GCH, GRP * NST)
    lbr, lbi, bbr, bbi = _ssm_prep(lam_r, lam_i, ldt, b_r, b_i)
    lam8 = jnp.stack([jnp.tile(lbr.reshape(2, HALF_ST), (bsz, 1)), jnp.tile(lbi.reshape(2, HALF_ST), (bsz, 1))])
    bd_r = _blockdiag(bbr.reshape(GCH, 2, GRP // 2, NST).transpose(1, 2, 0, 3))
    bd_i = _blockdiag(bbi.reshape(GCH, 2, GRP // 2, NST).transpose(1, 2, 0, 3))
    bb = jnp.concatenate([bd_r[0], bd_i[0], bd_r[1], bd_i[1]], axis=1).astype(BF16)
    cd_r = _blockdiag(ssm_c_re[0].reshape(2, GRP // 2, GCH, NST).transpose(0, 1, 3, 2))
    cd_i = _blockdiag(ssm_c_im[0].reshape(2, GRP // 2, GCH, NST).transpose(0, 1, 3, 2))
    cc = jnp.concatenate([jnp.concatenate([cd_r[0], cd_r[1]], axis=1),
                          jnp.concatenate([-cd_i[0], -cd_i[1]], axis=1)], axis=0).astype(BF16)
    d8 = jnp.tile(ssm_d[0].reshape(2, HALF_CH), (bsz, 1))

    tlen = min(seq, T_SSM)
    (u2, p), ((g_glu, g_out),) = _mixer_in_fwd(
        x, sh1, sc1, g_norm_mix, w_in_b, comm=[(_gather_plan, [w_glu[0].astype(BF16), w_out[0].astype(BF16)])])
    w_glu_b = g_glu.transpose(1, 0, 2).reshape(SSM_W, 2 * SSM_W)
    w_out_b = g_out.reshape(D, D)
    u2r = u2.reshape(seq * 2 * bsz, HALF_CH)
    (y2r, xc), ((g_up, g_down),) = _ssm_fwd(
        u2r, bb, cc, lam8, d8, tlen, comm=[(_gather_plan, [w_up[0].T.astype(BF16), w_down[0].astype(BF16)])])
    w_up_b = g_up.reshape(2 * DFF, D)
    w_down_b = g_down.reshape(DFF, D)
    y2 = y2r.reshape(seq, bsz * SSM_W)
    w_pool_b = w_pool[0].astype(BF16)
    bp = b_pool[0].reshape(1, POOL_W)
    x1, mixcat = _mixer_out_fwd(y2, p, x, gt1, w_glu_b, b_glu, w_pool_b, bp, pool_scale, w_out_b)
    h2, vq, gq, gcq, act, ddn, dx2, loss_l, dg_fin, dgt2 = _ffn_fwd(
        x1, loss_target, sh2, sc2, gt2, g_norm_ffn, w_up_b, w_conv_f, b_conv, w_down_b, g_norm_final.reshape(1, D))
    loss = lax.psum(loss_l[0, 0], ("x", "y", "c"))

    dup, dx1, dsh2, dsc2, dg_ffn, dw_conv, db_conv = _ffn_bwd(
        ddn, gq, gcq, vq, x1, dx2, sh2, sc2, g_norm_ffn, w_conv_f, w_down_b, w_up_b)
    ntok = bsz * seq
    dw_up_t = _wgrad(dup.reshape(ntok, 2 * DFF), h2.reshape(ntok, D), FF_CH, D, "wgrad_up")
    dw_down = _wgrad(act.reshape(ntok, DFF), ddn.reshape(ntok, D), DFF, 512, "wgrad_down")
    g42_up = dw_up_t.reshape(4, 2, 704, D)
    g42_down = dw_down.reshape(4, 2, 352, D)
    (dy2, dp, dw_out, dw_glu, db_glu, dw_pool, db_pool, dpscale, dgt1), ((ra_up, ra_down),) = _mixer_out_bwd(
        dx1, mixcat, y2, p, gt1, w_glu_b, b_glu, w_pool_b, bp, pool_scale, w_out_b,
        comm=[(_pair_plan, [g42_up, g42_down])])
    own_up, s_up = _pair_sum(g42_up, ra_up, place, "pair_sum_up")
    own_down, s_down = _pair_sum(g42_down, ra_down, place, "pair_sum_down")
    g42_glu = dw_glu.reshape(SSM_W, N_DEV, 128).transpose(1, 0, 2).reshape(4, 2, SSM_W, 128)
    g42_out = dw_out.reshape(4, 2, 128, D)
    small_a = [
        ("b_glu", (1, 2 * SSM_W), db_glu), ("w_pool", (POOL_W, 128), dw_pool.reshape(POOL_W, 128)),
        ("b_pool", (4, 128), db_pool.reshape(4, 128)), ("pool_scale", (1, POOL_W), dpscale),
        ("g_norm_ffn", (1, D), dg_ffn), ("b_conv", (1, DFF), db_conv), ("g_norm_final", (1, D), dg_fin)]
    (du2r, dcc, dbb, dlam8, dd8), ((rc_up, rc_down), (ra_glu, ra_out), parts_a) = _ssm_bwd(
        dy2.reshape(u2r.shape), u2r, xc, bb, cc, lam8, d8, tlen,
        comm=[(_chip_plan, [s_up, s_down]), (_pair_plan, [g42_glu, g42_out]),
              (_gather_plan, [g for _, _, g in small_a] + [dw_conv])])
    big_up = [t.T for t in _final_sum_adamw(own_up, rc_up, w_up[0].T, m_w_up[0].T, v_w_up[0].T, "final_adamw_up")]
    big_down = _final_sum_adamw(own_down, rc_down, w_down[0], m_w_down[0], v_w_down[0], "final_adamw_down")
    own_glu, s_glu = _pair_sum(g42_glu, ra_glu, place, "pair_sum_glu")
    own_out, s_out = _pair_sum(g42_out, ra_out, place, "pair_sum_out")

    def take_c(t):
        return _blockdiag_take(t, NST, GCH).transpose(0, 2, 1)

    dc_re = jnp.concatenate([take_c(dcc[0:HALF_ST, e * HALF_CH:(e + 1) * HALF_CH]) for e in range(2)], axis=0)
    dc_im = jnp.concatenate([take_c(dcc[HALF_ST:, e * HALF_CH:(e + 1) * HALF_CH]) for e in range(2)], axis=0)

    def take_b(t):
        return _blockdiag_take(t, GCH, NST).transpose(1, 0, 2)

    dbbr = jnp.concatenate([take_b(dbb[e * HALF_CH:(e + 1) * HALF_CH, 0:HALF_ST]) for e in range(2)], axis=1)
    dbbi = jnp.concatenate([take_b(dbb[e * HALF_CH:(e + 1) * HALF_CH, HALF_ST:]) for e in range(2)], axis=1)
    glr, gli, gldt, gbr, gbi, gd = _ssm_param_bwd(
        lam_r, lam_i, ldt, b_r, b_i, dlam8, dbbr.reshape(GCH, GRP * NST), dbbi.reshape(GCH, GRP * NST), dd8)
    g_log_dt = jnp.sum(gldt.reshape(GRP, NST), axis=1)

    def view(a, shp):
        return a.reshape(shp)

    small_b = [
        ("ssm_lam_re", (GRP, NST), glr.reshape(GRP, NST)), ("ssm_lam_im", (GRP, NST), gli.reshape(GRP, NST)),
        ("ssm_log_dt", (1, GRP), g_log_dt.reshape(1, GRP)),
        ("ssm_c_re", (GRP * GCH, NST), dc_re.reshape(GRP * GCH, NST)),
        ("ssm_c_im", (GRP * GCH, NST), dc_im.reshape(GRP * GCH, NST)), ("ssm_d", (1, SSM_W), gd)]
    small = small_a + small_b
    given = dict(
        ssm_lam_re=(ssm_lam_re, m_ssm_lam_re, v_ssm_lam_re), ssm_lam_im=(ssm_lam_im, m_ssm_lam_im, v_ssm_lam_im),
        ssm_log_dt=(ssm_log_dt, m_ssm_log_dt, v_ssm_log_dt), ssm_c_re=(ssm_c_re, m_ssm_c_re, v_ssm_c_re),
        ssm_c_im=(ssm_c_im, m_ssm_c_im, v_ssm_c_im), ssm_d=(ssm_d, m_ssm_d, v_ssm_d), b_glu=(b_glu, m_b_glu, v_b_glu),
        w_pool=(w_pool, m_w_pool, v_w_pool), b_pool=(b_pool, m_b_pool, v_b_pool),
        pool_scale=(pool_scale, m_pool_scale, v_pool_scale), g_norm_ffn=(g_norm_ffn, m_g_norm_ffn, v_g_norm_ffn),
        b_conv=(b_conv, m_b_conv, v_b_conv), g_norm_final=(g_norm_final, m_g_norm_final, v_g_norm_final),
        ssm_b_re=(ssm_b_re, m_ssm_b_re, v_ssm_b_re), ssm_b_im=(ssm_b_im, m_ssm_b_im, v_ssm_b_im))
    b_view = (GRP * NST, GCH)
    (grad_x, dw_in, dsh1, dsc1, dg_mix), ((rc_glu, rc_out), parts_b) = _mixer_in_bwd(
        du2r.reshape(u2.shape), dp, x, dx1, sh1, sc1, g_norm_mix, w_in_b,
        comm=[(_chip_plan, [s_glu, s_out]), (_gather_plan, [g for _, _, g in small_b] + [gbr, gbi])])
    big_glu = _final_sum_adamw(own_glu, rc_glu, w_glu[0], m_w_glu[0], v_w_glu[0], "final_adamw_glu")
    big_out = _final_sum_adamw(own_out, rc_out, w_out[0], m_w_out[0], v_w_out[0], "final_adamw_out")
    parts = list(parts_a[:-1]) + list(parts_b[:-2])
    items = [(pt,) + tuple(view(a, shp) for a in given[nm]) for pt, (nm, shp, _) in zip(parts, small)]
    small_out, (g_conv_full, gbr_all, gbi_all) = _small_sum_adamw(items, [parts_a[-1], parts_b[-2], parts_b[-1]])
    result = {nm: [t.reshape(given[nm][0].shape) for t in quad] for quad, (nm, _, _) in zip(small_out, small)}
    for nm, g_all in (("ssm_b_re", gbr_all), ("ssm_b_im", gbi_all)):
        quad = [g_all.T] + list(_adamw_plain(g_all.T, *[view(a, b_view) for a in given[nm]]))
        result[nm] = [t.reshape(given[nm][0].shape) for t in quad]
    g_w_conv = lax.dynamic_slice_in_dim(g_conv_full, 352 * me, 352, axis=1)
    result["w_conv"] = [g_w_conv[None]] + [t[None] for t in _adamw_plain(g_w_conv, w_conv[0], m_w_conv[0], v_w_conv[0])]

    g42_in = dw_in.reshape(4, 2, 128, D)
    (ra_in,) = _comm_call(_pair_plan([g42_in]), [g42_in], "in_grad_pair_exchange")
    own_in, s_in = _pair_sum(g42_in, ra_in, place, "pair_sum_in")
    (rc_in,) = _comm_call(_chip_plan([s_in]), [s_in], "in_grad_chip_exchange")
    big_in = _final_sum_adamw(own_in, rc_in, w_in[0], m_w_in[0], v_w_in[0], "final_adamw_in")
    for nm, quad in (("w_in", big_in), ("w_glu", big_glu), ("w_out", big_out), ("w_up", big_up), ("w_down", big_down)):
        result[nm] = [t[None] for t in quad]

    dmod = jnp.concatenate([t.reshape(bsz, D) for t in (dsh1, dsc1, dgt1, dsh2, dsc2, dgt2)], axis=1)
    dmod_blk = jnp.zeros((N_DEV, 8, ncol), F32).at[:, 0:bsz].set(dmod.reshape(bsz, N_DEV, ncol).transpose(1, 0, 2))
    dmod_blk = dmod_blk.at[0, ADA_RIDER_ROW].set(dg_mix[0, 0:ncol]).at[1, ADA_RIDER_ROW, 0:D - ncol].set(dg_mix[0, ncol:])
    ada = _ada_bwd(dmod_blk.reshape(ADA_ROWS, ncol), c_all, w_ada[0], m_w_ada[0], v_w_ada[0],
                   b_ada.reshape(N_DEV, 1, ncol), m_b_ada.reshape(N_DEV, 1, ncol), v_b_ada.reshape(N_DEV, 1, ncol),
                   g_norm_mix, m_g_norm_mix, v_g_norm_mix)
    result["w_ada"] = [t[None] for t in ada[0:4]]
    result["b_ada"] = [t.reshape(1, N_MOD * D) for t in ada[4:8]]
    result["g_norm_mix"] = list(ada[8:12])

    names = ["w_ada", "b_ada", "g_norm_mix", "w_in", "ssm_lam_re", "ssm_lam_im", "ssm_log_dt", "ssm_b_re", "ssm_b_im",
             "ssm_c_re", "ssm_c_im", "ssm_d", "w_glu", "b_glu", "w_pool", "b_pool", "pool_scale", "w_out", "g_norm_ffn",
             "w_up", "w_conv", "b_conv", "w_down", "g_norm_final"]
    return (loss, grad_x, *[result[nm][k] for k in range(4) for nm in names])
```

```python
import functools
import math

import jax
import jax.numpy as jnp
from jax import lax
from jax.experimental import pallas as pl
from jax.experimental.pallas import tpu as pltpu

F32 = jnp.float32
BF16 = jnp.bfloat16

D = 1024
SSM_W = 512
POOL_W = 512
GRP = 32
GCH = 16
NST = 64
HALF_ST = GRP * NST // 2
HALF_CH = SSM_W // 2
DFF = 2816
FF_CH = 2816
N_MOD = 6
N_DEV = 8
EPS = 1e-6
POOL_WINDOWS = (2, 4, 8, 16)
POOL_HALO = 16
CONV_HALO = 8
GELU_C = math.sqrt(2.0 / math.pi)
GELU_A = 0.044715

ADAM_LR = 0.001
ADAM_B1 = 0.9
ADAM_B2 = 0.999
ADAM_EPS = 1e-08
ADAM_WD = 0.01
ADAM_STEP = 10

VMEM_LIMIT = 56 * 1024 * 1024
TT_MIX = 512
TT_FFN = 256
T_SSM = 128
TT_WGRAD = 2048
MESH = pl.DeviceIdType.MESH
NT = (((1,), (1,)), ((), ()))
TN = (((0,), (0,)), ((), ()))
ANY = pl.BlockSpec(memory_space=pl.ANY)
VMEM = pl.BlockSpec(memory_space=pltpu.VMEM)


def _params(n_grid, vmem=VMEM_LIMIT):
    return pltpu.CompilerParams(dimension_semantics=("arbitrary",) * n_grid, vmem_limit_bytes=vmem)


def _dot(a, b):
    return jnp.dot(a, b, preferred_element_type=F32)


def _dot_nt(a, b):
    return lax.dot_general(a, b, NT, preferred_element_type=F32)


def _dot_tn(a, b):
    return lax.dot_general(a, b, TN, preferred_element_type=F32)


def _colsum(a):
    return jnp.sum(a, axis=0, keepdims=True)


def _rms(x):
    rstd = lax.rsqrt(jnp.mean(x * x, axis=-1, keepdims=True) + EPS)
    return x * rstd, rstd


def _rms_bwd(dxhat, xhat, rstd):
    return rstd * (dxhat - xhat * jnp.mean(dxhat * xhat, axis=-1, keepdims=True))


def _gelu(x):
    return 0.5 * x * (1.0 + jnp.tanh(GELU_C * (x + GELU_A * x * x * x)))


def _gelu_grad(x):
    x2 = x * x
    th = jnp.tanh(GELU_C * (x + GELU_A * x * x2))
    return 0.5 * (1.0 + th) + 0.5 * x * (1.0 - th * th) * GELU_C * (1.0 + 3.0 * GELU_A * x2)


def _adamw(w, g, m, v):
    m = ADAM_B1 * m + (1.0 - ADAM_B1) * g
    v = ADAM_B2 * v + (1.0 - ADAM_B2) * (g * g)
    m_hat = m / (1.0 - ADAM_B1 ** ADAM_STEP)
    v_hat = v / (1.0 - ADAM_B2 ** ADAM_STEP)
    delta = -ADAM_LR * (m_hat / (jnp.sqrt(v_hat) + ADAM_EPS) + ADAM_WD * w)
    return delta, m, v


def _my_place():
    return lax.axis_index("x"), lax.axis_index("y"), lax.axis_index("c")


def _gather_plan(shards):
    n = len(shards)
    out_shape = [jax.ShapeDtypeStruct((N_DEV,) + tuple(s.shape), s.dtype) for s in shards]
    scratch = [pltpu.SemaphoreType.DMA((n, 7)), pltpu.SemaphoreType.DMA((n, 7)), pltpu.SemaphoreType.DMA((n,))]

    def stages(x_refs, out_refs, sems):
        send_sems, recv_sems, local_sems = sems
        x, y, c = _my_place()
        me, sibling = (x, y, c), (x, y, 1 - c)
        chips = [(1 - x, y), (x, 1 - y), (1 - x, 1 - y)]

        def copy(i, k, block, to, own=False):
            px, py, pc = block
            dst = out_refs[i].at[4 * px + 2 * py + pc]
            return pltpu.make_async_remote_copy(
                src_ref=x_refs[i] if own else dst, dst_ref=dst, send_sem=send_sems.at[i, k],
                recv_sem=recv_sems.at[i, k], device_id=to, device_id_type=MESH)

        def mine(i):
            return pltpu.make_async_copy(x_refs[i], out_refs[i].at[4 * x + 2 * y + c], local_sems.at[i])

        def start():
            for i in range(n):
                mine(i).start()
                copy(i, 0, me, sibling, own=True).start()
                for j, chip in enumerate(chips):
                    copy(i, 1 + j, me, (*chip, c), own=True).start()

        def forward():
            for i in range(n):
                for j, chip in enumerate(chips):
                    copy(i, 1 + j, (*chip, c), me).wait_recv()
                    copy(i, 4 + j, (*chip, c), sibling).start()

        def finish():
            for i in range(n):
                copy(i, 0, sibling, me).wait_recv()
                copy(i, 0, me, sibling, own=True).wait_send()
                for j, chip in enumerate(chips):
                    copy(i, 4 + j, (*chip, 1 - c), me).wait_recv()
                    copy(i, 1 + j, me, (*chip, c), own=True).wait_send()
                    copy(i, 4 + j, (*chip, c), sibling).wait_send()
                mine(i).wait()

        return [start, forward, finish]

    return n, out_shape, scratch, stages


def _pair_plan(g42s):
    n = len(g42s)
    out_shape = [jax.ShapeDtypeStruct((4,) + tuple(g.shape[2:]), g.dtype) for g in g42s]
    scratch = [pltpu.SemaphoreType.DMA((n,)), pltpu.SemaphoreType.DMA((n,))]

    def stages(g_refs, out_refs, sems):
        send_sems, recv_sems = sems
        x, y, c = _my_place()

        def copy(i):
            return pltpu.make_async_remote_copy(
                src_ref=g_refs[i].at[:, 1 - c], dst_ref=out_refs[i], send_sem=send_sems.at[i],
                recv_sem=recv_sems.at[i], device_id=(x, y, 1 - c), device_id_type=MESH)

        def start():
            for i in range(n):
                copy(i).start()

        def finish():
            for i in range(n):
                copy(i).wait()

        return [start, finish]

    return n, out_shape, scratch, stages


def _chip_plan(s4s):
    n = len(s4s)
    out_shape = [jax.ShapeDtypeStruct((3,) + tuple(s.shape[1:]), s.dtype) for s in s4s]
    scratch = [pltpu.SemaphoreType.DMA((n, 3)), pltpu.SemaphoreType.DMA((n, 3))]

    def stages(s_refs, out_refs, sems):
        send_sems, recv_sems = sems
        x, y, c = _my_place()

        def copy(i, d):
            px, py = x ^ (d >> 1), y ^ (d & 1)
            return pltpu.make_async_remote_copy(
                src_ref=s_refs[i].at[2 * px + py], dst_ref=out_refs[i].at[d - 1], send_sem=send_sems.at[i, d - 1],
                recv_sem=recv_sems.at[i, d - 1], device_id=(px, py, c), device_id_type=MESH)

        def start():
            for i in range(n):
                for d in (1, 2, 3):
                    copy(i, d).start()

        def finish():
            for i in range(n):
                for d in (1, 2, 3):
                    copy(i, d).wait()

        return [start, finish]

    return n, out_shape, scratch, stages


def _comm_call(plan, arrays, name):
    n, out_shape, scratch, stages = plan

    def body(*refs):
        for stage in stages(refs[:n], refs[n:2 * n], refs[2 * n:]):
            stage()

    return pl.pallas_call(
        body, name=name, out_shape=out_shape, in_specs=[ANY] * n, out_specs=[ANY] * n, scratch_shapes=scratch,
    )(*arrays)


def _fused_call(body, *, name, grid, in_specs, out_specs, out_shape, scratch_shapes, args, comm=None):
    if not comm:
        out = pl.pallas_call(body, name=name, grid=grid, in_specs=in_specs, out_specs=out_specs, out_shape=out_shape,
                             scratch_shapes=scratch_shapes, compiler_params=_params(len(grid)))(*args)
        return out, []
    counts = [plan[0] for plan, _, _ in comm]
    n = sum(counts)
    n_in, n_out, n_scr = len(in_specs), len(out_specs), len(scratch_shapes)

    def fused(*refs):
        ins, refs = refs[:n_in], refs[n_in:]
        c_ins, refs = refs[:n], refs[n:]
        outs, refs = refs[:n_out], refs[n_out:]
        c_outs, refs = refs[:n], refs[n:]
        scr, c_scr = refs[:n_scr], refs[n_scr:]
        step = pl.program_id(0)
        for k in range(1, len(grid)):
            step = step * grid[k] + pl.program_id(k)
        todo, a0, s0 = [], 0, 0
        for (cnt, _, plan_scratch, stages), _, steps in comm:
            sems = c_scr[s0:s0 + len(plan_scratch)]
            todo += list(zip(stages(c_ins[a0:a0 + cnt], c_outs[a0:a0 + cnt], sems), steps))
            a0 += cnt
            s0 += len(plan_scratch)
        for stage, at in todo:
            if at == 0:
                pl.when(step == 0)(stage)
        body(*ins, *outs, *scr)
        for stage, at in todo:
            if at != 0:
                pl.when(step == at)(stage)

    c_shape = [s for plan, _, _ in comm for s in plan[1]]
    c_scratch = [s for plan, _, _ in comm for s in plan[2]]
    arrays = [a for _, arrs, _ in comm for a in arrs]
    out = pl.pallas_call(
        fused, name=name, grid=grid, in_specs=list(in_specs) + [ANY] * n, out_specs=list(out_specs) + [ANY] * n,
        out_shape=list(out_shape) + c_shape, scratch_shapes=list(scratch_shapes) + c_scratch,
        compiler_params=_params(len(grid)))(*args, *arrays)
    outs, c_outs, split, a0 = out[:n_out], out[n_out:], [], 0
    for cnt in counts:
        split.append(c_outs[a0:a0 + cnt])
        a0 += cnt
    return outs, split


def _schedule(comm, n_steps):
    out = []
    for make_plan, arrays in comm or []:
        steps = [0, (3 * n_steps) // 4, n_steps - 1] if make_plan is _gather_plan else [0, n_steps - 1]
        out.append((make_plan(arrays), arrays, steps))
    return out


def _row_tile(r):
    for t in (128, 64, 32, 16, 8):
        if r % t == 0:
            return t
    return r


def _pair_sum(g42, recv, place, name):
    _, _, r, cdim = g42.shape
    tr = _row_tile(r)

    def body(pl_ref, g_ref, r_ref, own_ref, s_ref):
        s_ref[...] = (g_ref[:, 0] + r_ref[...]).astype(BF16)
        q = pl_ref[1]
        own_ref[...] = g_ref[q, 0] + r_ref[q]

    return pl.pallas_call(
        body, name=name,
        grid_spec=pltpu.PrefetchScalarGridSpec(
            num_scalar_prefetch=1, grid=(r // tr,),
            in_specs=[pl.BlockSpec((4, 1, tr, cdim), lambda i, p: (0, p[0], i, 0)),
                      pl.BlockSpec((4, tr, cdim), lambda i, p: (0, i, 0))],
            out_specs=[pl.BlockSpec((tr, cdim), lambda i, p: (i, 0)),
                       pl.BlockSpec((4, tr, cdim), lambda i, p: (0, i, 0))]),
        out_shape=[jax.ShapeDtypeStruct((r, cdim), F32), jax.ShapeDtypeStruct((4, r, cdim), BF16)],
        compiler_params=_params(1),
    )(place, g42, recv)


def _final_sum_adamw(own, recv3, w, m, v, name):
    r, cdim = w.shape
    tr = _row_tile(r)

    def body(s_ref, r_ref, w_ref, m_ref, v_ref, g_out, d_out, m_out, v_out):
        g = s_ref[...] + r_ref[0].astype(F32) + r_ref[1].astype(F32) + r_ref[2].astype(F32)
        d, mn, vn = _adamw(w_ref[...], g, m_ref[...], v_ref[...])
        g_out[...] = g
        d_out[...] = d
        m_out[...] = mn
        v_out[...] = vn

    blk = pl.BlockSpec((tr, cdim), lambda i: (i, 0))
    shp = jax.ShapeDtypeStruct((r, cdim), F32)
    return pl.pallas_call(
        body, name=name, grid=(r // tr,),
        in_specs=[blk, pl.BlockSpec((3, tr, cdim), lambda i: (0, i, 0)), blk, blk, blk],
        out_specs=[blk, blk, blk, blk], out_shape=[shp, shp, shp, shp], compiler_params=_params(1),
    )(own, recv3, w, m, v)


def _small_sum_adamw(items, sums_only):
    n, ne = len(items), len(sums_only)

    def total(p_ref):
        g = p_ref[0]
        for k in range(1, N_DEV):
            g = g + p_ref[k]
        return g

    def body(*refs):
        ins, outs = refs[:4 * n + ne], refs[4 * n + ne:]
        for i in range(n):
            p_ref, w_ref, m_ref, v_ref = ins[4 * i:4 * i + 4]
            g = total(p_ref)
            d, mn, vn = _adamw(w_ref[...], g, m_ref[...], v_ref[...])
            for o_ref, val in zip(outs[4 * i:4 * i + 4], (g, d, mn, vn)):
                o_ref[...] = val
        for j in range(ne):
            outs[4 * n + j][...] = total(ins[4 * n + j])

    args = [a for item in items for a in item] + list(sums_only)
    shapes = [jax.ShapeDtypeStruct(w.shape, F32) for _, w, _, _ in items for _ in range(4)]
    shapes += [jax.ShapeDtypeStruct(p.shape[1:], F32) for p in sums_only]
    out = pl.pallas_call(
        body, name="small_sum_adamw", in_specs=[VMEM] * len(args), out_specs=[VMEM] * len(shapes), out_shape=shapes,
        compiler_params=_params(0),
    )(*args)
    return [out[4 * i:4 * i + 4] for i in range(n)], out[4 * n:]


def _adamw_plain(g, w, m, v):
    r, cdim = w.shape
    tr = r if r * cdim <= 64 * 1024 else _row_tile(r)

    def body(g_ref, w_ref, m_ref, v_ref, d_out, m_out, v_out):
        d, mn, vn = _adamw(w_ref[...], g_ref[...], m_ref[...], v_ref[...])
        d_out[...] = d
        m_out[...] = mn
        v_out[...] = vn

    blk = pl.BlockSpec((tr, cdim), lambda i: (i, 0))
    shp = jax.ShapeDtypeStruct((r, cdim), F32)
    return pl.pallas_call(
        body, name="adamw_plain", grid=(r // tr,), in_specs=[blk, blk, blk, blk],
        out_specs=[blk, blk, blk], out_shape=[shp, shp, shp], compiler_params=_params(1),
    )(g, w, m, v)


ADA_COLS = N_MOD * D // N_DEV
ADA_ROWS = 8 * N_DEV


def _ada_fwd(cpad, w_ada, b_blocks, mixer_shards):
    n_w, w_shape, w_scr, w_stages = _gather_plan(mixer_shards)
    _, _, c_scr, c_stages = _gather_plan([cpad])
    _, _, p_scr, p_stages = _gather_plan([jax.ShapeDtypeStruct((ADA_ROWS, ADA_COLS), F32)])

    def body(c_ref, wa_ref, b_ref, *refs):
        w_refs, refs = refs[:n_w], refs[n_w:]
        cg_ref, call_ref, mod_ref = refs[:3]
        wg_refs, refs = refs[3:3 + n_w], refs[3 + n_w:]
        part_ref, pg_ref = refs[:2]
        c_sems, p_sems, w_sems = refs[2:5], refs[5:8], refs[8:11]
        w_start, w_forward, w_finish = w_stages(w_refs, wg_refs, w_sems)
        w_start()
        for stage in c_stages([c_ref], [cg_ref], c_sems):
            stage()
        cv = cg_ref[:, 0:8, :].reshape(ADA_ROWS, D)
        call_ref[...] = cv
        part_ref[...] = _dot(cv * jax.nn.sigmoid(cv), wa_ref[...])
        for stage in p_stages([part_ref], [pg_ref], p_sems):
            stage()
        x, y, c = _my_place()
        r0 = pl.multiple_of(8 * (4 * x + 2 * y + c), 8)
        for k in range(N_DEV):
            mod_ref[:, k * ADA_COLS:(k + 1) * ADA_COLS] = pg_ref[k, pl.ds(r0, 8), :] + b_ref[k]
        w_forward()
        w_finish()

    out = pl.pallas_call(
        body, name="ada_fwd", in_specs=[VMEM, VMEM, VMEM] + [ANY] * n_w,
        out_specs=[VMEM, VMEM, VMEM] + [ANY] * n_w,
        out_shape=[jax.ShapeDtypeStruct((N_DEV,) + cpad.shape, F32), jax.ShapeDtypeStruct((ADA_ROWS, D), F32),
                   jax.ShapeDtypeStruct((8, N_MOD * D), F32)] + list(w_shape),
        scratch_shapes=[pltpu.VMEM((ADA_ROWS, ADA_COLS), F32), pltpu.VMEM((N_DEV, ADA_ROWS, ADA_COLS), F32)]
        + list(c_scr) + list(p_scr) + list(w_scr),
        compiler_params=_params(0),
    )(cpad, w_ada, b_blocks, *mixer_shards)
    return out[0], out[1], out[2], out[3:]


ADA_RIDER_ROW = 4


def _ada_bwd(dmod_blk, c_all, w_ada, m_w, v_w, b_blocks, m_b, v_b, g_w, g_m, g_v):
    _, _, g_scr, g_stages = _gather_plan([dmod_blk])
    rest = D - ADA_COLS

    def body(dm_ref, c_ref, w_ref, mw_ref, vw_ref, b_ref, mb_ref, vb_ref, gw_ref, gm_ref, gv_ref,
             gw_o, dw_o, mw_o, vw_o, gb_o, dbb_o, mb_o, vb_o, gg_o, dgg_o, mg_o, vg_o, dg_ref, *sems):
        for stage in g_stages([dm_ref], [dg_ref], sems):
            stage()
        x, y, c = _my_place()
        r0 = pl.multiple_of(8 * (4 * x + 2 * y + c), 8)
        cols = dg_ref[:, pl.ds(r0, 8), :].reshape(ADA_ROWS, ADA_COLS)
        cv = c_ref[...]
        gw = _dot_tn(cv * jax.nn.sigmoid(cv), cols)
        d, mn, vn = _adamw(w_ref[...], gw, mw_ref[...], vw_ref[...])
        gw_o[...] = gw
        dw_o[...] = d
        mw_o[...] = mn
        vw_o[...] = vn
        is_example = lax.broadcasted_iota(jnp.int32, (8, 1), 0) < ADA_RIDER_ROW
        blocks = []
        for k in range(N_DEV):
            s = dg_ref[0, 8 * k:8 * k + 8, :]
            for dev in range(1, N_DEV):
                s = s + dg_ref[dev, 8 * k:8 * k + 8, :]
            blocks.append(s)
            gb = _colsum(jnp.where(is_example, s, 0.0))
            cs = slice(k * ADA_COLS, (k + 1) * ADA_COLS)
            d, mn, vn = _adamw(b_ref[:, cs], gb, mb_ref[:, cs], vb_ref[:, cs])
            gb_o[:, cs] = gb
            dbb_o[:, cs] = d
            mb_o[:, cs] = mn
            vb_o[:, cs] = vn
        rider = jnp.concatenate([blocks[0][ADA_RIDER_ROW:ADA_RIDER_ROW + 1, :],
                                 blocks[1][ADA_RIDER_ROW:ADA_RIDER_ROW + 1, 0:rest]], axis=1)
        d, mn, vn = _adamw(gw_ref[...], rider, gm_ref[...], gv_ref[...])
        gg_o[...] = rider
        dgg_o[...] = d
        mg_o[...] = mn
        vg_o[...] = vn

    ws = jax.ShapeDtypeStruct(w_ada.shape, F32)
    bs = jax.ShapeDtypeStruct(b_blocks.shape, F32)
    gs = jax.ShapeDtypeStruct(g_w.shape, F32)
    return pl.pallas_call(
        body, name="ada_bwd", in_specs=[VMEM] * 11, out_specs=[VMEM] * 12,
        out_shape=[ws, ws, ws, ws, bs, bs, bs, bs, gs, gs, gs, gs],
        scratch_shapes=[pltpu.VMEM((N_DEV, ADA_ROWS, ADA_COLS), F32)] + list(g_scr),
        compiler_params=_params(0),
    )(dmod_blk, c_all, w_ada, m_w, v_w, b_blocks, m_b, v_b, g_w, g_m, g_v)


def _ssm_param_fn(lr, li, ldt, br, bi):
    dt = jnp.exp(ldt)
    mag = jnp.exp(lr * dt)
    ang = li * dt
    lbr = mag * jnp.cos(ang)
    lbi = mag * jnp.sin(ang)
    nr = lbr - 1.0
    den = lr * lr + li * li
    cr = (nr * lr + lbi * li) / den
    ci = (lbi * lr - nr * li) / den
    return lbr, lbi, cr * br - ci * bi, cr * bi + ci * br


def _ssm_prep(lr, li, ldt, br, bi):
    def body(lr_ref, li_ref, ldt_ref, br_ref, bi_ref, lbr_o, lbi_o, bbr_o, bbi_o):
        lbr, lbi, bbr, bbi = _ssm_param_fn(lr_ref[...], li_ref[...], ldt_ref[...], br_ref[...], bi_ref[...])
        lbr_o[...] = lbr
        lbi_o[...] = lbi
        bbr_o[...] = bbr
        bbi_o[...] = bbi

    row = jax.ShapeDtypeStruct(lr.shape, F32)
    mat = jax.ShapeDtypeStruct(br.shape, F32)
    return pl.pallas_call(
        body, name="ssm_prep", in_specs=[VMEM] * 5, out_specs=[VMEM] * 4,
        out_shape=[row, row, mat, mat], compiler_params=_params(0),
    )(lr, li, ldt, br, bi)


def _ssm_param_bwd(lr, li, ldt, br, bi, dlam8, dbbr, dbbi, dd8):
    nv = dlam8.shape[1]

    def body(lr_ref, li_ref, ldt_ref, br_ref, bi_ref, dl_ref, dbr_ref, dbi_ref, dd_ref,
             glr_o, gli_o, gldt_o, gbr_o, gbi_o, gd_o):
        halves_r, halves_i, halves_d = [], [], []
        for e in range(2):
            ar = dl_ref[0, e:e + 1, :]
            ai = dl_ref[1, e:e + 1, :]
            ad = dd_ref[e:e + 1, :]
            for b in range(1, nv // 2):
                ar = ar + dl_ref[0, 2 * b + e:2 * b + e + 1, :]
                ai = ai + dl_ref[1, 2 * b + e:2 * b + e + 1, :]
                ad = ad + dd_ref[2 * b + e:2 * b + e + 1, :]
            halves_r.append(ar)
            halves_i.append(ai)
            halves_d.append(ad)
        dlbr = jnp.concatenate(halves_r, axis=1)
        dlbi = jnp.concatenate(halves_i, axis=1)
        gd_o[...] = jnp.concatenate(halves_d, axis=1)
        _, vjp = jax.vjp(_ssm_param_fn, lr_ref[...], li_ref[...], ldt_ref[...], br_ref[...], bi_ref[...])
        glr, gli, gldt, gbr, gbi = vjp((dlbr, dlbi, dbr_ref[...], dbi_ref[...]))
        glr_o[...] = glr
        gli_o[...] = gli
        gldt_o[...] = gldt
        gbr_o[...] = gbr
        gbi_o[...] = gbi

    row = jax.ShapeDtypeStruct(lr.shape, F32)
    mat = jax.ShapeDtypeStruct(br.shape, F32)
    return pl.pallas_call(
        body, name="ssm_param_bwd", in_specs=[VMEM] * 9, out_specs=[VMEM] * 6,
        out_shape=[row, row, row, mat, mat, jax.ShapeDtypeStruct((1, SSM_W), F32)],
        compiler_params=_params(0),
    )(lr, li, ldt, br, bi, dlam8, dbbr, dbbi, dd8)


def _blockdiag(m):
    _, g, a, b = m.shape
    eye = jnp.eye(g, dtype=m.dtype)
    return jnp.einsum("egab,gk->egakb", m, eye).reshape(2, g * a, g * b)


def _blockdiag_take(t, a, b):
    return jnp.einsum("gagb->gab", t.reshape(GRP // 2, a, GRP // 2, b))


def _mixer_in_fwd(x, sh1, sc1, g_mix, w_in_b, comm=None):
    bsz, seq, _ = x.shape
    tt = min(seq, TT_MIX)

    def body(x_ref, sh_ref, sc_ref, g_ref, w_ref, u_ref, p_ref):
        xhat, _ = _rms(x_ref[0])
        h = xhat * g_ref[...] * (1.0 + sc_ref[0]) + sh_ref[0]
        z = _dot(h.astype(BF16), w_ref[...])
        u_ref[...] = z[:, :SSM_W]
        p_ref[0] = z[:, SSM_W:]

    row = pl.BlockSpec((1, 1, D), lambda b, t: (b, 0, 0))
    return _fused_call(
        body, name="mixer_in_fwd", grid=(bsz, seq // tt),
        in_specs=[pl.BlockSpec((1, tt, D), lambda b, t: (b, t, 0)), row, row,
                  pl.BlockSpec((1, D), lambda b, t: (0, 0)), VMEM],
        out_specs=[pl.BlockSpec((tt, SSM_W), lambda b, t: (t, b)),
                   pl.BlockSpec((1, tt, POOL_W), lambda b, t: (b, t, 0))],
        out_shape=[jax.ShapeDtypeStruct((seq, bsz * SSM_W), F32), jax.ShapeDtypeStruct((bsz, seq, POOL_W), F32)],
        scratch_shapes=[], args=(x, sh1, sc1, g_mix, w_in_b), comm=_schedule(comm, bsz * (seq // tt)))


def _ssm_project_in(ub, par0, bb_ref, s_re, s_im, row0, tlen, nv):
    for part, sref in ((0, s_re), (1, s_im)):
        for k in range(HALF_ST // 512):
            c0 = part * HALF_ST + k * 512
            a0 = _dot(ub, bb_ref[:, c0:c0 + 512])
            a1 = _dot(ub, bb_ref[:, 2 * HALF_ST + c0:2 * HALF_ST + c0 + 512])
            sref[pl.ds(row0, tlen), :, k * 512:(k + 1) * 512] = jnp.where(par0, a0, a1).reshape(tlen, nv, 512)


def _ssm_fwd(u2r, bb, cc, lam8, d8, tlen, comm=None):
    nv = lam8.shape[1]
    rows = nv * tlen
    n_chunks = u2r.shape[0] // rows

    def body(u_ref, bb_ref, cc_ref, lam_ref, d_ref, y_ref, xc_ref, xre_ref, xim_ref, s_re, s_im, st):
        @pl.when(pl.program_id(0) == 0)
        def _():
            st[...] = jnp.zeros_like(st)

        xc_ref[0] = st[...]
        u = u_ref[...]
        par0 = (lax.broadcasted_iota(jnp.int32, (rows, 1), 0) % 2) == 0
        _ssm_project_in(u.astype(BF16), par0, bb_ref, s_re, s_im, 0, tlen, nv)
        for hb in range(HALF_ST // 512):
            ls = slice(hb * 512, (hb + 1) * 512)
            lr = lam_ref[0, :, ls]
            li = lam_ref[1, :, ls]

            def step(t, carry, ls=ls, lr=lr, li=li):
                xr, xi = carry
                nr = lr * xr - li * xi + s_re[t, :, ls]
                ni = lr * xi + li * xr + s_im[t, :, ls]
                s_re[t, :, ls] = nr
                s_im[t, :, ls] = ni
                return nr, ni

            xr, xi = lax.fori_loop(0, tlen, step, (st[0, :, ls], st[1, :, ls]), unroll=8)
            st[0, :, ls] = xr
            st[1, :, ls] = xi
        xre = s_re[...].reshape(rows, HALF_ST).astype(BF16)
        xim = s_im[...].reshape(rows, HALF_ST).astype(BF16)
        xre_ref[...] = xre
        xim_ref[...] = xim
        y2 = _dot(xre, cc_ref[0:HALF_ST, :]) + _dot(xim, cc_ref[HALF_ST:, :])
        y = jnp.where(par0, y2[:, :HALF_CH], y2[:, HALF_CH:])
        skip = (u.reshape(tlen, nv, HALF_CH) * d_ref[...][None]).reshape(rows, HALF_CH)
        y_ref[...] = y + skip

    st_blk = pl.BlockSpec((rows, HALF_ST), lambda c: (c, 0))
    st_shape = jax.ShapeDtypeStruct((u2r.shape[0], HALF_ST), BF16)
    return _fused_call(
        body, name="ssm_fwd", grid=(n_chunks,),
        in_specs=[pl.BlockSpec((rows, HALF_CH), lambda c: (c, 0)), VMEM, VMEM, VMEM, VMEM],
        out_specs=[pl.BlockSpec((rows, HALF_CH), lambda c: (c, 0)),
                   pl.BlockSpec((1, 2, nv, HALF_ST), lambda c: (c, 0, 0, 0)), st_blk, st_blk],
        out_shape=[jax.ShapeDtypeStruct(u2r.shape, F32), jax.ShapeDtypeStruct((n_chunks, 2, nv, HALF_ST), F32),
                   st_shape, st_shape],
        scratch_shapes=[pltpu.VMEM((tlen, nv, HALF_ST), F32), pltpu.VMEM((tlen, nv, HALF_ST), F32),
                        pltpu.VMEM((2, nv, HALF_ST), F32)],
        args=(u2r, bb, cc, lam8, d8), comm=_schedule(comm, n_chunks))


def _pool_forward(ext, pv, pos, wp_ref, bp_ref):
    cur = ext
    zs, zls = [], []
    for gi, w in enumerate(POOL_WINDOWS):
        cur = cur + pltpu.roll(cur, w // 2, 0)
        sw = cur[POOL_HALO:, 0:128]
        z = sw / jnp.minimum(pos, float(w)) - pv[:, gi * 128:(gi + 1) * 128]
        zs.append(z)
        zls.append(_dot(z.astype(BF16), wp_ref[gi]) + bp_ref[:, gi * 128:(gi + 1) * 128])
        if gi + 1 < len(POOL_WINDOWS):
            cur = cur[:, 128:]
    return zs, zls


def _mixer_out_fwd(y2, p, x, gt1, w_glu_b, b_glu, w_pool_b, b_pool, pscale, w_out_b):
    bsz, seq, _ = x.shape
    tt = min(seq, TT_MIX)

    def body(y_ref, p_ref, x_ref, gt_ref, wg_ref, bg_ref, wp_ref, bp_ref, ps_ref, wo_ref, x1_ref, mix_ref, ext):
        ti = pl.program_id(1)

        @pl.when(ti == 0)
        def _():
            ext[0:POOL_HALO, :] = jnp.zeros((POOL_HALO, POOL_W), F32)

        pv = p_ref[0]
        ext[POOL_HALO:, :] = pv
        pos = (ti * tt + lax.broadcasted_iota(jnp.int32, (tt, 1), 0) + 1).astype(F32)
        _, zls = _pool_forward(ext[...], pv, pos, wp_ref, bp_ref)
        ext[0:POOL_HALO, :] = pv[tt - POOL_HALO:, :]
        a = _gelu(y_ref[...])
        gl = _dot(a.astype(BF16), wg_ref[...]) + bg_ref[...]
        y_ssm = gl[:, :SSM_W] * jax.nn.sigmoid(gl[:, SSM_W:])
        y_pool = [zl * ps_ref[:, gi * 128:(gi + 1) * 128] for gi, zl in enumerate(zls)]
        mixcat = jnp.concatenate([y_ssm] + y_pool, axis=1).astype(BF16)
        mix_ref[0] = mixcat
        x1_ref[0] = x_ref[0] + gt_ref[0] * _dot(mixcat, wo_ref[...])

    xt = pl.BlockSpec((1, tt, D), lambda b, t: (b, t, 0))
    return pl.pallas_call(
        body, name="mixer_out_fwd", grid=(bsz, seq // tt),
        in_specs=[pl.BlockSpec((tt, SSM_W), lambda b, t: (t, b)),
                  pl.BlockSpec((1, tt, POOL_W), lambda b, t: (b, t, 0)), xt,
                  pl.BlockSpec((1, 1, D), lambda b, t: (b, 0, 0)), VMEM, VMEM, VMEM, VMEM, VMEM, VMEM],
        out_specs=[xt, xt],
        out_shape=[jax.ShapeDtypeStruct(x.shape, F32), jax.ShapeDtypeStruct(x.shape, BF16)],
        scratch_shapes=[pltpu.VMEM((POOL_HALO + tt, POOL_W), F32)],
        compiler_params=_params(2),
    )(y2, p, x, gt1, w_glu_b, b_glu, w_pool_b, b_pool, pscale, w_out_b)


def _conv_gate(g, ge, wc, bc):
    g1 = pltpu.roll(ge, 1, 0)[CONV_HALO:]
    g2 = pltpu.roll(ge, 2, 0)[CONV_HALO:]
    return wc[2:3] * g + wc[1:2] * g1 + wc[0:1] * g2 + bc, g1, g2


def _ffn_fwd(x1, tgt, sh2, sc2, gt2, g_ffn, w_up_b, w_conv, b_conv, w_down_b, g_fin):
    bsz, seq, _ = x1.shape
    tt = min(seq, TT_FFN)
    n_t = seq // tt
    n_ck = DFF // FF_CH

    def body(x1_ref, tg_ref, sh_ref, sc_ref, gt_ref, gf_ref, wu_ref, wc_ref, bc_ref, wd_ref, gfin_ref,
             h2_ref, v_ref, g_ref, gc_ref, act_ref, ddn_ref, dx2_ref, loss_ref, dgfin_ref, dgt_ref, gext, lacc):
        b = pl.program_id(0)
        ti = pl.program_id(1)

        @pl.when((b == 0) & (ti == 0))
        def _():
            lacc[...] = jnp.zeros_like(lacc)
            dgfin_ref[...] = jnp.zeros_like(dgfin_ref)

        @pl.when(ti == 0)
        def _():
            dgt_ref[...] = jnp.zeros_like(dgt_ref)
            gext[:, 0:CONV_HALO, :] = jnp.zeros((n_ck, CONV_HALO, FF_CH), F32)

        x1v = x1_ref[0]
        xhat, _ = _rms(x1v)
        h2b = (xhat * gf_ref[...] * (1.0 + sc_ref[0]) + sh_ref[0]).astype(BF16)
        h2_ref[0] = h2b
        dn = jnp.zeros((tt, D), F32)
        for ck in range(n_ck):
            c0 = ck * FF_CH
            v = _dot_nt(h2b, wu_ref[c0:c0 + FF_CH, :])
            g = _dot_nt(h2b, wu_ref[DFF + c0:DFF + c0 + FF_CH, :])
            v_ref[0, :, c0:c0 + FF_CH] = v.astype(BF16)
            g_ref[0, :, c0:c0 + FF_CH] = g.astype(BF16)
            gext[ck, CONV_HALO:, :] = g
            gc, _, _ = _conv_gate(g, gext[ck], wc_ref[:, c0:c0 + FF_CH], bc_ref[:, c0:c0 + FF_CH])
            gext[ck, 0:CONV_HALO, :] = g[tt - CONV_HALO:, :]
            gc_ref[0, :, c0:c0 + FF_CH] = gc.astype(BF16)
            actb = (gc * jax.nn.sigmoid(gc) * v).astype(BF16)
            act_ref[0, :, c0:c0 + FF_CH] = actb
            dn = dn + _dot(actb, wd_ref[c0:c0 + FF_CH, :])
        gt = gt_ref[0]
        xh3, r3 = _rms(x1v + gt * dn)
        gfin = gfin_ref[...]
        diff = xh3 * gfin - tg_ref[0]
        lacc[...] += _colsum(diff * diff)
        dy = diff * (1.0 / D)
        dgfin_ref[...] += _colsum(dy * xh3)
        dx2 = _rms_bwd(dy * gfin, xh3, r3)
        dx2_ref[0] = dx2
        dgt_ref[0] += _colsum(dx2 * dn)
        ddn_ref[0] = (gt * dx2).astype(BF16)

        @pl.when((b == bsz - 1) & (ti == n_t - 1))
        def _():
            loss_ref[...] = jnp.full(loss_ref.shape, 0.5 / D * jnp.sum(lacc[...]), F32)

    xt = pl.BlockSpec((1, tt, D), lambda b, t: (b, t, 0))
    ft = pl.BlockSpec((1, tt, DFF), lambda b, t: (b, t, 0))
    row = pl.BlockSpec((1, 1, D), lambda b, t: (b, 0, 0))
    vec = pl.BlockSpec((1, D), lambda b, t: (0, 0))
    ff = jax.ShapeDtypeStruct((bsz, seq, DFF), BF16)
    xs = jax.ShapeDtypeStruct((bsz, seq, D), BF16)
    return pl.pallas_call(
        body, name="ffn_fwd", grid=(bsz, n_t),
        in_specs=[xt, xt, row, row, row, vec, VMEM, VMEM, VMEM, VMEM, vec],
        out_specs=[xt, ft, ft, ft, ft, xt, xt, pl.BlockSpec((1, 128), lambda b, t: (0, 0)), vec, row],
        out_shape=[xs, ff, ff, ff, ff, xs, jax.ShapeDtypeStruct((bsz, seq, D), F32),
                   jax.ShapeDtypeStruct((1, 128), F32), jax.ShapeDtypeStruct((1, D), F32),
                   jax.ShapeDtypeStruct((bsz, 1, D), F32)],
        scratch_shapes=[pltpu.VMEM((n_ck, CONV_HALO + tt, FF_CH), F32), pltpu.VMEM((1, D), F32)],
        compiler_params=_params(2),
    )(x1, tgt, sh2, sc2, gt2, g_ffn, w_up_b, w_conv, b_conv, w_down_b, g_fin)


def _ffn_bwd(ddn, gq, gcq, vq, x1, dx2, sh2, sc2, g_ffn, w_conv, w_down_b, w_up_b):
    bsz, seq, _ = x1.shape
    tt = min(seq, TT_FFN)
    n_t = seq // tt
    n_ck = DFF // FF_CH
    ext_rows = tt + CONV_HALO

    def body(ddn_ref, g_ref, gc_ref, v_ref, x1_ref, dx2_ref, sh_ref, sc_ref, gf_ref, wc_ref, wd_ref,
             wu_ref, dup_ref, dx1_ref, dsh_ref, dsc_ref, dgf_ref, dwc_ref, dbc_ref, dext):
        b = pl.program_id(0)
        i = pl.program_id(1)

        @pl.when((b == 0) & (i == 0))
        def _():
            dgf_ref[...] = jnp.zeros_like(dgf_ref)
            dwc_ref[...] = jnp.zeros_like(dwc_ref)
            dbc_ref[...] = jnp.zeros_like(dbc_ref)

        @pl.when(i == 0)
        def _():
            dsh_ref[...] = jnp.zeros_like(dsh_ref)
            dsc_ref[...] = jnp.zeros_like(dsc_ref)
            dext[:, tt:, :] = jnp.zeros((n_ck, CONV_HALO, FF_CH), F32)

        ddnv = ddn_ref[0]
        dh2 = jnp.zeros((tt, D), F32)
        for ck in range(n_ck):
            c0 = ck * FF_CH
            dact = _dot_nt(ddnv, wd_ref[c0:c0 + FF_CH, :])
            g = g_ref[0, :, c0:c0 + FF_CH].astype(F32)
            gc = gc_ref[0, :, c0:c0 + FF_CH].astype(F32)
            v = v_ref[0, :, c0:c0 + FF_CH].astype(F32)
            wc = wc_ref[:, c0:c0 + FF_CH]
            sg = jax.nn.sigmoid(gc)
            silu = gc * sg
            dv = dact * silu
            dgc = dact * v * (sg + silu * (1.0 - sg))
            dext[ck, 0:tt, :] = dgc
            de = dext[ck]
            d1 = pltpu.roll(de, ext_rows - 1, 0)[0:tt]
            d2 = pltpu.roll(de, ext_rows - 2, 0)[0:tt]
            dext[ck, tt:, :] = dgc[0:CONV_HALO, :]
            dbc_ref[:, c0:c0 + FF_CH] += _colsum(dgc)
            dwc_ref[0:1, c0:c0 + FF_CH] += _colsum(d2 * g)
            dwc_ref[1:2, c0:c0 + FF_CH] += _colsum(d1 * g)
            dwc_ref[2:3, c0:c0 + FF_CH] += _colsum(dgc * g)
            dg = wc[2:3] * dgc + wc[1:2] * d1 + wc[0:1] * d2
            dvb = dv.astype(BF16)
            dgb = dg.astype(BF16)
            dup_ref[0, :, c0:c0 + FF_CH] = dvb
            dup_ref[0, :, DFF + c0:DFF + c0 + FF_CH] = dgb
            dh2 = dh2 + _dot(dvb, wu_ref[c0:c0 + FF_CH, :]) + _dot(dgb, wu_ref[DFF + c0:DFF + c0 + FF_CH, :])
        xhat, rstd = _rms(x1_ref[0])
        gf = gf_ref[...]
        dsh_ref[0] += _colsum(dh2)
        dsc_ref[0] += _colsum(dh2 * xhat * gf)
        t = dh2 * (1.0 + sc_ref[0])
        dgf_ref[...] += _colsum(t * xhat)
        dx1_ref[0] = dx2_ref[0] + _rms_bwd(t * gf, xhat, rstd)

    def rev(b, t):
        return (b, n_t - 1 - t, 0)

    xt = pl.BlockSpec((1, tt, D), rev)
    ft = pl.BlockSpec((1, tt, DFF), rev)
    row = pl.BlockSpec((1, 1, D), lambda b, t: (b, 0, 0))
    vec = pl.BlockSpec((1, D), lambda b, t: (0, 0))
    rows = jax.ShapeDtypeStruct((bsz, 1, D), F32)
    return pl.pallas_call(
        body, name="ffn_bwd", grid=(bsz, n_t),
        in_specs=[xt, ft, ft, ft, xt, xt, row, row, vec, VMEM, VMEM, VMEM],
        out_specs=[pl.BlockSpec((1, tt, 2 * DFF), rev), xt, row, row, vec,
                   pl.BlockSpec((3, DFF), lambda b, t: (0, 0)), pl.BlockSpec((1, DFF), lambda b, t: (0, 0))],
        out_shape=[jax.ShapeDtypeStruct((bsz, seq, 2 * DFF), BF16), jax.ShapeDtypeStruct((bsz, seq, D), F32),
                   rows, rows, jax.ShapeDtypeStruct((1, D), F32), jax.ShapeDtypeStruct((3, DFF), F32),
                   jax.ShapeDtypeStruct((1, DFF), F32)],
        scratch_shapes=[pltpu.VMEM((n_ck, ext_rows, FF_CH), F32)],
        compiler_params=_params(2),
    )(ddn, gq, gcq, vq, x1, dx2, sh2, sc2, g_ffn, w_conv, w_down_b, w_up_b)


def _wgrad(a, b, bk1, bk2, name):
    n, k1 = a.shape
    _, k2 = b.shape
    tt = min(n, TT_WGRAD)

    def body(a_ref, b_ref, o_ref):
        @pl.when(pl.program_id(2) == 0)
        def _():
            o_ref[...] = jnp.zeros_like(o_ref)

        o_ref[...] += _dot_tn(a_ref[...], b_ref[...])

    return pl.pallas_call(
        body, name=name, grid=(k1 // bk1, k2 // bk2, n // tt),
        in_specs=[pl.BlockSpec((tt, bk1), lambda h, j, i: (i, h)), pl.BlockSpec((tt, bk2), lambda h, j, i: (i, j))],
        out_specs=pl.BlockSpec((bk1, bk2), lambda h, j, i: (h, j)),
        out_shape=jax.ShapeDtypeStruct((k1, k2), F32), compiler_params=_params(3),
    )(a, b)


def _mixer_out_bwd(dx1, mixcat, y2, p, gt1, w_glu_b, b_glu, w_pool_b, b_pool, pscale, w_out_b, comm=None):
    bsz, seq, _ = dx1.shape
    tt = min(seq, TT_MIX)
    n_t = seq // tt
    ext_rows = tt + POOL_HALO

    def body(dx1_ref, mc_ref, y_ref, p_ref, ph_ref, gt_ref, wg_ref, bg_ref, wp_ref, bp_ref, ps_ref, wo_ref,
             dy_ref, dp_ref, dwo_ref, dwg_ref, dbg_ref, dwp_ref, dbp_ref, dps_ref, dgt_ref, ext, qext):
        b = pl.program_id(0)
        i = pl.program_id(1)
        tile = n_t - 1 - i

        @pl.when((b == 0) & (i == 0))
        def _():
            for r in (dwo_ref, dwg_ref, dbg_ref, dwp_ref, dbp_ref, dps_ref):
                r[...] = jnp.zeros_like(r)

        @pl.when(i == 0)
        def _():
            dgt_ref[...] = jnp.zeros_like(dgt_ref)
            qext[tt:, :] = jnp.zeros((POOL_HALO, POOL_W), F32)

        dx1v = dx1_ref[0]
        mc = mc_ref[0]
        dgt_ref[0] += _colsum(dx1v * _dot(mc, wo_ref[...]))
        dmixed = (gt_ref[0] * dx1v).astype(BF16)
        dwo_ref[...] += _dot_tn(mc, dmixed)
        dmc = _dot_nt(dmixed, wo_ref[...])
        pv = p_ref[0]
        ext[0:POOL_HALO, :] = ph_ref[0] * (tile > 0).astype(F32)
        ext[POOL_HALO:, :] = pv
        pos = (tile * tt + lax.broadcasted_iota(jnp.int32, (tt, 1), 0) + 1).astype(F32)
        zs, zls = _pool_forward(ext[...], pv, pos, wp_ref, bp_ref)
        dzs = []
        for gi, w in enumerate(POOL_WINDOWS):
            cs = slice(gi * 128, (gi + 1) * 128)
            dyp = dmc[:, SSM_W + gi * 128:SSM_W + (gi + 1) * 128]
            dps_ref[:, cs] += _colsum(dyp * zls[gi])
            dzl = dyp * ps_ref[:, cs]
            dbp_ref[:, cs] += _colsum(dzl)
            dzlb = dzl.astype(BF16)
            dwp_ref[gi] += _dot_tn(zs[gi].astype(BF16), dzlb)
            dz = _dot_nt(dzlb, wp_ref[gi])
            dzs.append(dz)
            qext[0:tt, cs] = dz / jnp.minimum(pos, float(w))
        cur = qext[...]
        dps = []
        for gi, w in enumerate(POOL_WINDOWS):
            cur = cur + pltpu.roll(cur, ext_rows - w // 2, 0)
            dps.append(cur[0:tt, 0:128] - dzs[gi])
            if gi + 1 < len(POOL_WINDOWS):
                cur = cur[:, 128:]
        qhead = qext[0:POOL_HALO, :]
        qext[tt:, :] = qhead
        dp_ref[0] = jnp.concatenate(dps, axis=1)
        yv = y_ref[...]
        ab = _gelu(yv).astype(BF16)
        gl = _dot(ab, wg_ref[...]) + bg_ref[...]
        val = gl[:, :SSM_W]
        sg = jax.nn.sigmoid(gl[:, SSM_W:])
        dys = dmc[:, :SSM_W]
        dgl = jnp.concatenate([dys * sg, dys * val * sg * (1.0 - sg)], axis=1)
        dbg_ref[...] += _colsum(dgl)
        dglb = dgl.astype(BF16)
        dwg_ref[...] += _dot_tn(ab, dglb)
        dy_ref[...] = _dot_nt(dglb, wg_ref[...]) * _gelu_grad(yv)

    def rev(b, t):
        return (b, n_t - 1 - t, 0)

    def halo(b, t):
        return (b, jnp.maximum((n_t - 1 - t) * (tt // POOL_HALO) - 1, 0), 0)

    xt = pl.BlockSpec((1, tt, D), rev)
    pt = pl.BlockSpec((1, tt, POOL_W), rev)
    yt = pl.BlockSpec((tt, SSM_W), lambda b, t: (n_t - 1 - t, b))

    def whole(shape):
        return pl.BlockSpec(shape, lambda b, t: (0,) * len(shape))

    return _fused_call(
        body, name="mixer_out_bwd", grid=(bsz, n_t),
        in_specs=[xt, xt, yt, pt, pl.BlockSpec((1, POOL_HALO, POOL_W), halo),
                  pl.BlockSpec((1, 1, D), lambda b, t: (b, 0, 0)), VMEM, VMEM, VMEM, VMEM, VMEM, VMEM],
        out_specs=[yt, pt, whole((D, D)), whole((SSM_W, 2 * SSM_W)), whole((1, 2 * SSM_W)),
                   whole((4, 128, 128)), whole((1, POOL_W)), whole((1, POOL_W)),
                   pl.BlockSpec((1, 1, D), lambda b, t: (b, 0, 0))],
        out_shape=[jax.ShapeDtypeStruct(y2.shape, F32), jax.ShapeDtypeStruct(p.shape, F32),
                   jax.ShapeDtypeStruct((D, D), F32), jax.ShapeDtypeStruct((SSM_W, 2 * SSM_W), F32),
                   jax.ShapeDtypeStruct((1, 2 * SSM_W), F32), jax.ShapeDtypeStruct((4, 128, 128), F32),
                   jax.ShapeDtypeStruct((1, POOL_W), F32), jax.ShapeDtypeStruct((1, POOL_W), F32),
                   jax.ShapeDtypeStruct((bsz, 1, D), F32)],
        scratch_shapes=[pltpu.VMEM((POOL_HALO + tt, POOL_W), F32), pltpu.VMEM((ext_rows, POOL_W), F32)],
        args=(dx1, mixcat, y2, p, p, gt1, w_glu_b, b_glu, w_pool_b, b_pool, pscale, w_out_b),
        comm=_schedule(comm, bsz * n_t))


def _ssm_bwd(dy2r, u2r, xc, xs_re, xs_im, bb, cc, lam8, d8, tlen, comm=None):
    nv = lam8.shape[1]
    rows = nv * tlen
    n_chunks = u2r.shape[0] // rows

    def body(dy_ref, u_ref, xc_ref, xre_ref, xim_ref, bb_ref, cc_ref, lam_ref, d_ref,
             du_ref, dcc_ref, dbb_ref, dlam_ref, dd_ref, s_re, s_im, g_re, g_im, gst):
        i = pl.program_id(0)

        @pl.when(i == 0)
        def _():
            for r in (gst, dcc_ref, dbb_ref, dlam_ref, dd_ref):
                r[...] = jnp.zeros_like(r)

        u = u_ref[...]
        dy = dy_ref[...]
        par0 = (lax.broadcasted_iota(jnp.int32, (rows, 1), 0) % 2) == 0
        xre = xre_ref[...]
        xim = xim_ref[...]
        s_re[0] = xc_ref[0, 0]
        s_im[0] = xc_ref[0, 1]
        s_re[pl.ds(1, tlen)] = xre.astype(F32).reshape(tlen, nv, HALF_ST)
        s_im[pl.ds(1, tlen)] = xim.astype(F32).reshape(tlen, nv, HALF_ST)
        zero = jnp.zeros_like(dy)
        dy2 = jnp.concatenate([jnp.where(par0, dy, zero), jnp.where(par0, zero, dy)], axis=1).astype(BF16)
        u2 = jnp.concatenate([jnp.where(par0, u, zero), jnp.where(par0, zero, u)], axis=1).astype(BF16)
        dcc_ref[0:HALF_ST, :] += _dot_tn(xre, dy2)
        dcc_ref[HALF_ST:, :] += _dot_tn(xim, dy2)
        for part, gref in ((0, g_re), (1, g_im)):
            for k in range(HALF_ST // 512):
                r0 = part * HALF_ST + k * 512
                gref[:, :, k * 512:(k + 1) * 512] = _dot_nt(dy2, cc_ref[r0:r0 + 512, :]).reshape(tlen, nv, 512)
        for hb in range(HALF_ST // 512):
            ls = slice(hb * 512, (hb + 1) * 512)
            lr = lam_ref[0, :, ls]
            li = lam_ref[1, :, ls]

            def bstep(k, carry, ls=ls, lr=lr, li=li):
                t = tlen - 1 - k
                gr, gi, ar, ai = carry
                ngr = g_re[t, :, ls] + lr * gr + li * gi
                ngi = g_im[t, :, ls] + lr * gi - li * gr
                g_re[t, :, ls] = ngr
                g_im[t, :, ls] = ngi
                xpr = s_re[t, :, ls]
                xpi = s_im[t, :, ls]
                return ngr, ngi, ar + ngr * xpr + ngi * xpi, ai + ngi * xpr - ngr * xpi

            init = (gst[0, :, ls], gst[1, :, ls], dlam_ref[0, :, ls], dlam_ref[1, :, ls])
            gr, gi, ar, ai = lax.fori_loop(0, tlen, bstep, init, unroll=4)
            gst[0, :, ls] = gr
            gst[1, :, ls] = gi
            dlam_ref[0, :, ls] = ar
            dlam_ref[1, :, ls] = ai
        gre = g_re[...].reshape(rows, HALF_ST).astype(BF16)
        gim = g_im[...].reshape(rows, HALF_ST).astype(BF16)
        du0 = _dot_nt(gre, bb_ref[:, 0:HALF_ST]) + _dot_nt(gim, bb_ref[:, HALF_ST:2 * HALF_ST])
        du1 = _dot_nt(gre, bb_ref[:, 2 * HALF_ST:3 * HALF_ST]) + _dot_nt(gim, bb_ref[:, 3 * HALF_ST:])
        skip = (dy.reshape(tlen, nv, HALF_CH) * d_ref[...][None]).reshape(rows, HALF_CH)
        du_ref[...] = jnp.where(par0, du0, du1) + skip
        dbb_ref[:, 0:HALF_ST] += _dot_tn(u2, gre)
        dbb_ref[:, HALF_ST:] += _dot_tn(u2, gim)
        dd_ref[...] += jnp.sum((dy * u).reshape(tlen, nv, HALF_CH), axis=0)

        @pl.when(i == n_chunks - 1)
        def _():
            dcc_ref[HALF_ST:, :] = -dcc_ref[HALF_ST:, :]

    def rev(c):
        return (n_chunks - 1 - c, 0)

    def whole(shape):
        return pl.BlockSpec(shape, lambda c: (0,) * len(shape))

    blk = pl.BlockSpec((rows, HALF_CH), rev)
    st_blk = pl.BlockSpec((rows, HALF_ST), rev)
    return _fused_call(
        body, name="ssm_bwd", grid=(n_chunks,),
        in_specs=[blk, blk, pl.BlockSpec((1, 2, nv, HALF_ST), lambda c: (n_chunks - 1 - c, 0, 0, 0)),
                  st_blk, st_blk, VMEM, VMEM, VMEM, VMEM],
        out_specs=[blk, whole((2 * HALF_ST, SSM_W)), whole((SSM_W, 2 * HALF_ST)), whole((2, nv, HALF_ST)),
                   whole((nv, HALF_CH))],
        out_shape=[jax.ShapeDtypeStruct(u2r.shape, F32), jax.ShapeDtypeStruct((2 * HALF_ST, SSM_W), F32),
                   jax.ShapeDtypeStruct((SSM_W, 2 * HALF_ST), F32), jax.ShapeDtypeStruct((2, nv, HALF_ST), F32),
                   jax.ShapeDtypeStruct((nv, HALF_CH), F32)],
        scratch_shapes=[pltpu.VMEM((tlen + 1, nv, HALF_ST), F32), pltpu.VMEM((tlen + 1, nv, HALF_ST), F32),
                        pltpu.VMEM((tlen, nv, HALF_ST), F32), pltpu.VMEM((tlen, nv, HALF_ST), F32),
                        pltpu.VMEM((2, nv, HALF_ST), F32)],
        args=(dy2r, u2r, xc, xs_re, xs_im, bb, cc, lam8, d8), comm=_schedule(comm, n_chunks))


def _mixer_in_bwd(du2, dp, x, dx1, sh1, sc1, g_mix, w_in_b, comm=None):
    bsz, seq, _ = x.shape
    tt = min(seq, TT_MIX)

    def body(du_ref, dp_ref, x_ref, dx1_ref, sh_ref, sc_ref, g_ref, w_ref,
             dx_ref, dw_ref, dsh_ref, dsc_ref, dg_ref):
        b = pl.program_id(0)
        ti = pl.program_id(1)

        @pl.when((b == 0) & (ti == 0))
        def _():
            dw_ref[...] = jnp.zeros_like(dw_ref)
            dg_ref[...] = jnp.zeros_like(dg_ref)

        @pl.when(ti == 0)
        def _():
            dsh_ref[...] = jnp.zeros_like(dsh_ref)
            dsc_ref[...] = jnp.zeros_like(dsc_ref)

        dz = jnp.concatenate([du_ref[...], dp_ref[0]], axis=1).astype(BF16)
        xhat, rstd = _rms(x_ref[0])
        g = g_ref[...]
        sc = sc_ref[0]
        a = xhat * g
        h = (a * (1.0 + sc) + sh_ref[0]).astype(BF16)
        dw_ref[...] += _dot_tn(h, dz)
        dh = _dot_nt(dz, w_ref[...])
        dsh_ref[0] += _colsum(dh)
        dsc_ref[0] += _colsum(dh * a)
        t = dh * (1.0 + sc)
        dg_ref[...] += _colsum(t * xhat)
        dx_ref[0] = dx1_ref[0] + _rms_bwd(t * g, xhat, rstd)

    xt = pl.BlockSpec((1, tt, D), lambda b, t: (b, t, 0))
    row = pl.BlockSpec((1, 1, D), lambda b, t: (b, 0, 0))
    vec = pl.BlockSpec((1, D), lambda b, t: (0, 0))
    rows = jax.ShapeDtypeStruct((bsz, 1, D), F32)
    return _fused_call(
        body, name="mixer_in_bwd", grid=(bsz, seq // tt),
        in_specs=[pl.BlockSpec((tt, SSM_W), lambda b, t: (t, b)),
                  pl.BlockSpec((1, tt, POOL_W), lambda b, t: (b, t, 0)), xt, xt, row, row, vec, VMEM],
        out_specs=[xt, pl.BlockSpec((D, D), lambda b, t: (0, 0)), row, row, vec],
        out_shape=[jax.ShapeDtypeStruct(x.shape, F32), jax.ShapeDtypeStruct((D, D), F32), rows, rows,
                   jax.ShapeDtypeStruct((1, D), F32)],
        scratch_shapes=[], args=(du2, dp, x, dx1, sh1, sc1, g_mix, w_in_b),
        comm=_schedule(comm, bsz * (seq // tt)))


def kernel(x, c, w_ada, b_ada, g_norm_mix, w_in, ssm_lam_re, ssm_lam_im, ssm_log_dt, ssm_b_re, ssm_b_im, ssm_c_re, ssm_c_im, ssm_d, w_glu, b_glu, w_pool, b_pool, pool_scale, w_out, g_norm_ffn, w_up, w_conv, b_conv, w_down, g_norm_final, loss_target, m_w_ada, m_b_ada, m_g_norm_mix, m_w_in, m_ssm_lam_re, m_ssm_lam_im, m_ssm_log_dt, m_ssm_b_re, m_ssm_b_im, m_ssm_c_re, m_ssm_c_im, m_ssm_d, m_w_glu, m_b_glu, m_w_pool, m_b_pool, m_pool_scale, m_w_out, m_g_norm_ffn, m_w_up, m_w_conv, m_b_conv, m_w_down, m_g_norm_final, v_w_ada, v_b_ada, v_g_norm_mix, v_w_in, v_ssm_lam_re, v_ssm_lam_im, v_ssm_log_dt, v_ssm_b_re, v_ssm_b_im, v_ssm_c_re, v_ssm_c_im, v_ssm_d, v_w_glu, v_b_glu, v_w_pool, v_b_pool, v_pool_scale, v_w_out, v_g_norm_ffn, v_w_up, v_w_conv, v_b_conv, v_w_down, v_g_norm_final):
    bsz, seq, _ = x.shape
    assert 2 * bsz == 8 and seq % 128 == 0
    px, py, pc = _my_place()
    me = 4 * px + 2 * py + pc
    place = jnp.stack([pc, 2 * px + py]).astype(jnp.int32)
    ncol = ADA_COLS

    cpad = jnp.zeros((16, D), F32).at[0:bsz].set(c).at[8:11, 0:352].set(w_conv[0])
    cg, c_all, mod8, (g_in,) = _ada_fwd(cpad, w_ada[0], b_ada.reshape(N_DEV, 1, ncol), [w_in[0].astype(BF16)])
    w_conv_f = cg[:, 8:11, 0:352].transpose(1, 0, 2).reshape(3, DFF)
    w_in_b = g_in.reshape(D, D)
    sh1, sc1, gt1, sh2, sc2, gt2 = [mod8[0:bsz, k * D:(k + 1) * D].reshape(bsz, 1, D) for k in range(N_MOD)]

    lam_r = ssm_lam_re[0].reshape(1, GRP * NST)
    lam_i = ssm_lam_im[0].reshape(1, GRP * NST)
    ldt = jnp.repeat(ssm_log_dt[0], NST).reshape(1, GRP * NST)
    b_r = ssm_b_re[0].transpose(2, 0, 1).reshape(GCH, GRP * NST)
    b_i = ssm_b_im[0].transpose(2, 0, 1).reshape(GCH, GRP * NST)
    lbr, lbi, bbr, bbi = _ssm_prep(lam_r, lam_i, ldt, b_r, b_i)
    lam8 = jnp.stack([jnp.tile(lbr.reshape(2, HALF_ST), (bsz, 1)), jnp.tile(lbi.reshape(2, HALF_ST), (bsz, 1))])
    bd_r = _blockdiag(bbr.reshape(GCH, 2, GRP // 2, NST).transpose(1, 2, 0, 3))
    bd_i = _blockdiag(bbi.reshape(GCH, 2, GRP // 2, NST).transpose(1, 2, 0, 3))
    bb = jnp.concatenate([bd_r[0], bd_i[0], bd_r[1], bd_i[1]], axis=1).astype(BF16)
    cd_r = _blockdiag(ssm_c_re[0].reshape(2, GRP // 2, GCH, NST).transpose(0, 1, 3, 2))
    cd_i = _blockdiag(ssm_c_im[0].reshape(2, GRP // 2, GCH, NST).transpose(0, 1, 3, 2))
    cc = jnp.concatenate([jnp.concatenate([cd_r[0], cd_r[1]], axis=1),
                          jnp.concatenate([-cd_i[0], -cd_i[1]], axis=1)], axis=0).astype(BF16)
    d8 = jnp.tile(ssm_d[0].reshape(2, HALF_CH), (bsz, 1))

    tlen = min(seq, T_SSM)
    (u2, p), ((g_glu, g_out),) = _mixer_in_fwd(
        x, sh1, sc1, g_norm_mix, w_in_b, comm=[(_gather_plan, [w_glu[0].astype(BF16), w_out[0].astype(BF16)])])
    w_glu_b = g_glu.transpose(1, 0, 2).reshape(SSM_W, 2 * SSM_W)
    w_out_b = g_out.reshape(D, D)
    u2r = u2.reshape(seq * 2 * bsz, HALF_CH)
    (y2r, xc, xs_re, xs_im), ((g_up, g_down),) = _ssm_fwd(
        u2r, bb, cc, lam8, d8, tlen, comm=[(_gather_plan, [w_up[0].T.astype(BF16), w_down[0].astype(BF16)])])
    w_up_b = g_up.reshape(2 * DFF, D)
    w_down_b = g_down.reshape(DFF, D)
    y2 = y2r.reshape(seq, bsz * SSM_W)
    w_pool_b = w_pool[0].astype(BF16)
    bp = b_pool[0].reshape(1, POOL_W)
    x1, mixcat = _mixer_out_fwd(y2, p, x, gt1, w_glu_b, b_glu, w_pool_b, bp, pool_scale, w_out_b)
    h2, vq, gq, gcq, act, ddn, dx2, loss_l, dg_fin, dgt2 = _ffn_fwd(
        x1, loss_target, sh2, sc2, gt2, g_norm_ffn, w_up_b, w_conv_f, b_conv, w_down_b, g_norm_final.reshape(1, D))
    loss = lax.psum(loss_l[0, 0], ("x", "y", "c"))

    dup, dx1, dsh2, dsc2, dg_ffn, dw_conv, db_conv = _ffn_bwd(
        ddn, gq, gcq, vq, x1, dx2, sh2, sc2, g_norm_ffn, w_conv_f, w_down_b, w_up_b)
    ntok = bsz * seq
    dw_up_t = _wgrad(dup.reshape(ntok, 2 * DFF), h2.reshape(ntok, D), DFF // 2, D, "wgrad_up")
    dw_down = _wgrad(act.reshape(ntok, DFF), ddn.reshape(ntok, D), DFF, 512, "wgrad_down")
    g42_up = dw_up_t.reshape(4, 2, 704, D)
    g42_down = dw_down.reshape(4, 2, 352, D)
    (dy2, dp, dw_out, dw_glu, db_glu, dw_pool, db_pool, dpscale, dgt1), ((ra_up, ra_down),) = _mixer_out_bwd(
        dx1, mixcat, y2, p, gt1, w_glu_b, b_glu, w_pool_b, bp, pool_scale, w_out_b,
        comm=[(_pair_plan, [g42_up, g42_down])])
    own_up, s_up = _pair_sum(g42_up, ra_up, place, "pair_sum_up")
    own_down, s_down = _pair_sum(g42_down, ra_down, place, "pair_sum_down")
    g42_glu = dw_glu.reshape(SSM_W, N_DEV, 128).transpose(1, 0, 2).reshape(4, 2, SSM_W, 128)
    g42_out = dw_out.reshape(4, 2, 128, D)
    small_a = [
        ("b_glu", (1, 2 * SSM_W), db_glu), ("w_pool", (POOL_W, 128), dw_pool.reshape(POOL_W, 128)),
        ("b_pool", (4, 128), db_pool.reshape(4, 128)), ("pool_scale", (1, POOL_W), dpscale),
        ("g_norm_ffn", (1, D), dg_ffn), ("b_conv", (1, DFF), db_conv), ("g_norm_final", (1, D), dg_fin)]
    (du2r, dcc, dbb, dlam8, dd8), ((rc_up, rc_down), (ra_glu, ra_out), parts_a) = _ssm_bwd(
        dy2.reshape(u2r.shape), u2r, xc, xs_re, xs_im, bb, cc, lam8, d8, tlen,
        comm=[(_chip_plan, [s_up, s_down]), (_pair_plan, [g42_glu, g42_out]),
              (_gather_plan, [g for _, _, g in small_a] + [dw_conv])])
    big_up = [t.T for t in _final_sum_adamw(own_up, rc_up, w_up[0].T, m_w_up[0].T, v_w_up[0].T, "final_adamw_up")]
    big_down = _final_sum_adamw(own_down, rc_down, w_down[0], m_w_down[0], v_w_down[0], "final_adamw_down")
    own_glu, s_glu = _pair_sum(g42_glu, ra_glu, place, "pair_sum_glu")
    own_out, s_out = _pair_sum(g42_out, ra_out, place, "pair_sum_out")

    def take_c(t):
        return _blockdiag_take(t, NST, GCH).transpose(0, 2, 1)

    dc_re = jnp.concatenate([take_c(dcc[0:HALF_ST, e * HALF_CH:(e + 1) * HALF_CH]) for e in range(2)], axis=0)
    dc_im = jnp.concatenate([take_c(dcc[HALF_ST:, e * HALF_CH:(e + 1) * HALF_CH]) for e in range(2)], axis=0)

    def take_b(t):
        return _blockdiag_take(t, GCH, NST).transpose(1, 0, 2)

    dbbr = jnp.concatenate([take_b(dbb[e * HALF_CH:(e + 1) * HALF_CH, 0:HALF_ST]) for e in range(2)], axis=1)
    dbbi = jnp.concatenate([take_b(dbb[e * HALF_CH:(e + 1) * HALF_CH, HALF_ST:]) for e in range(2)], axis=1)
    glr, gli, gldt, gbr, gbi, gd = _ssm_param_bwd(
        lam_r, lam_i, ldt, b_r, b_i, dlam8, dbbr.reshape(GCH, GRP * NST), dbbi.reshape(GCH, GRP * NST), dd8)
    g_log_dt = jnp.sum(gldt.reshape(GRP, NST), axis=1)

    def view(a, shp):
        return a.reshape(shp)

    small_b = [
        ("ssm_lam_re", (GRP, NST), glr.reshape(GRP, NST)), ("ssm_lam_im", (GRP, NST), gli.reshape(GRP, NST)),
        ("ssm_log_dt", (1, GRP), g_log_dt.reshape(1, GRP)),
        ("ssm_c_re", (GRP * GCH, NST), dc_re.reshape(GRP * GCH, NST)),
        ("ssm_c_im", (GRP * GCH, NST), dc_im.reshape(GRP * GCH, NST)), ("ssm_d", (1, SSM_W), gd)]
    small = small_a + small_b
    given = dict(
        ssm_lam_re=(ssm_lam_re, m_ssm_lam_re, v_ssm_lam_re), ssm_lam_im=(ssm_lam_im, m_ssm_lam_im, v_ssm_lam_im),
        ssm_log_dt=(ssm_log_dt, m_ssm_log_dt, v_ssm_log_dt), ssm_c_re=(ssm_c_re, m_ssm_c_re, v_ssm_c_re),
        ssm_c_im=(ssm_c_im, m_ssm_c_im, v_ssm_c_im), ssm_d=(ssm_d, m_ssm_d, v_ssm_d), b_glu=(b_glu, m_b_glu, v_b_glu),
        w_pool=(w_pool, m_w_pool, v_w_pool), b_pool=(b_pool, m_b_pool, v_b_pool),
        pool_scale=(pool_scale, m_pool_scale, v_pool_scale), g_norm_ffn=(g_norm_ffn, m_g_norm_ffn, v_g_norm_ffn),
        b_conv=(b_conv, m_b_conv, v_b_conv), g_norm_final=(g_norm_final, m_g_norm_final, v_g_norm_final),
        ssm_b_re=(ssm_b_re, m_ssm_b_re, v_ssm_b_re), ssm_b_im=(ssm_b_im, m_ssm_b_im, v_ssm_b_im))
    b_view = (GRP * NST, GCH)
    (grad_x, dw_in, dsh1, dsc1, dg_mix), ((rc_glu, rc_out), parts_b) = _mixer_in_bwd(
        du2r.reshape(u2.shape), dp, x, dx1, sh1, sc1, g_norm_mix, w_in_b,
        comm=[(_chip_plan, [s_glu, s_out]), (_gather_plan, [g for _, _, g in small_b] + [gbr, gbi])])
    big_glu = _final_sum_adamw(own_glu, rc_glu, w_glu[0], m_w_glu[0], v_w_glu[0], "final_adamw_glu")
    big_out = _final_sum_adamw(own_out, rc_out, w_out[0], m_w_out[0], v_w_out[0], "final_adamw_out")
    parts = list(parts_a[:-1]) + list(parts_b[:-2])
    items = [(pt,) + tuple(view(a, shp) for a in given[nm]) for pt, (nm, shp, _) in zip(parts, small)]
    small_out, (g_conv_full, gbr_all, gbi_all) = _small_sum_adamw(items, [parts_a[-1], parts_b[-2], parts_b[-1]])
    result = {nm: [t.reshape(given[nm][0].shape) for t in quad] for quad, (nm, _, _) in zip(small_out, small)}
    for nm, g_all in (("ssm_b_re", gbr_all), ("ssm_b_im", gbi_all)):
        quad = [g_all.T] + list(_adamw_plain(g_all.T, *[view(a, b_view) for a in given[nm]]))
        result[nm] = [t.reshape(given[nm][0].shape) for t in quad]
    g_w_conv = lax.dynamic_slice_in_dim(g_conv_full, 352 * me, 352, axis=1)
    result["w_conv"] = [g_w_conv[None]] + [t[None] for t in _adamw_plain(g_w_conv, w_conv[0], m_w_conv[0], v_w_conv[0])]

    g42_in = dw_in.reshape(4, 2, 128, D)
    (ra_in,) = _comm_call(_pair_plan([g42_in]), [g42_in], "in_grad_pair_exchange")
    own_in, s_in = _pair_sum(g42_in, ra_in, place, "pair_sum_in")
    (rc_in,) = _comm_call(_chip_plan([s_in]), [s_in], "in_grad_chip_exchange")
    big_in = _final_sum_adamw(own_in, rc_in, w_in[0], m_w_in[0], v_w_in[0], "final_adamw_in")
    for nm, quad in (("w_in", big_in), ("w_glu", big_glu), ("w_out", big_out), ("w_up", big_up), ("w_down", big_down)):
        result[nm] = [t[None] for t in quad]

    dmod = jnp.concatenate([t.reshape(bsz, D) for t in (dsh1, dsc1, dgt1, dsh2, dsc2, dgt2)], axis=1)
    dmod_blk = jnp.zeros((N_DEV, 8, ncol), F32).at[:, 0:bsz].set(dmod.reshape(bsz, N_DEV, ncol).transpose(1, 0, 2))
    dmod_blk = dmod_blk.at[0, ADA_RIDER_ROW].set(dg_mix[0, 0:ncol]).at[1, ADA_RIDER_ROW, 0:D - ncol].set(dg_mix[0, ncol:])
    ada = _ada_bwd(dmod_blk.reshape(ADA_ROWS, ncol), c_all, w_ada[0], m_w_ada[0], v_w_ada[0],
                   b_ada, m_b_ada, v_b_ada, g_norm_mix, m_g_norm_mix, v_g_norm_mix)
    result["w_ada"] = [t[None] for t in ada[0:4]]
    result["b_ada"] = list(ada[4:8])
    result["g_norm_mix"] = list(ada[8:12])

    names = ["w_ada", "b_ada", "g_norm_mix", "w_in", "ssm_lam_re", "ssm_lam_im", "ssm_log_dt", "ssm_b_re", "ssm_b_im",
             "ssm_c_re", "ssm_c_im", "ssm_d", "w_glu", "b_glu", "w_pool", "b_pool", "pool_scale", "w_out", "g_norm_ffn",
             "w_up", "w_conv", "b_conv", "w_down", "g_norm_final"]
    return (loss, grad_x, *[result[nm][k] for k in range(4) for nm in names])
```

```python
import functools
import math

import jax
import jax.numpy as jnp
from jax import lax
from jax.experimental import pallas as pl
from jax.experimental.pallas import tpu as pltpu

F32 = jnp.float32
BF16 = jnp.bfloat16

D = 1024
SSM_W = 512
POOL_W = 512
GRP = 32
GCH = 16
NST = 64
HALF_ST = GRP * NST // 2
HALF_CH = SSM_W // 2
DFF = 2816
FF_CH = 2816
N_MOD = 6
N_DEV = 8
EPS = 1e-6
POOL_WINDOWS = (2, 4, 8, 16)
POOL_HALO = 16
CONV_HALO = 8
GELU_C = math.sqrt(2.0 / math.pi)
GELU_A = 0.044715

ADAM_LR = 0.001
ADAM_B1 = 0.9
ADAM_B2 = 0.999
ADAM_EPS = 1e-08
ADAM_WD = 0.01
ADAM_STEP = 10

VMEM_LIMIT = 56 * 1024 * 1024
TT_MIX = 512
TT_FFN = 256
T_SSM = 128
TT_WGRAD = 2048
MESH = pl.DeviceIdType.MESH
NT = (((1,), (1,)), ((), ()))
TN = (((0,), (0,)), ((), ()))
ANY = pl.BlockSpec(memory_space=pl.ANY)
VMEM = pl.BlockSpec(memory_space=pltpu.VMEM)


def _params(n_grid, vmem=VMEM_LIMIT):
    return pltpu.CompilerParams(dimension_semantics=("arbitrary",) * n_grid, vmem_limit_bytes=vmem)


def _dot(a, b):
    return jnp.dot(a, b, preferred_element_type=F32)


def _dot_nt(a, b):
    return lax.dot_general(a, b, NT, preferred_element_type=F32)


def _dot_tn(a, b):
    return lax.dot_general(a, b, TN, preferred_element_type=F32)


def _colsum(a):
    return jnp.sum(a, axis=0, keepdims=True)


def _rms(x):
    rstd = lax.rsqrt(jnp.mean(x * x, axis=-1, keepdims=True) + EPS)
    return x * rstd, rstd


def _rms_bwd(dxhat, xhat, rstd):
    return rstd * (dxhat - xhat * jnp.mean(dxhat * xhat, axis=-1, keepdims=True))


def _gelu(x):
    return 0.5 * x * (1.0 + jnp.tanh(GELU_C * (x + GELU_A * x * x * x)))


def _gelu_grad(x):
    x2 = x * x
    th = jnp.tanh(GELU_C * (x + GELU_A * x * x2))
    return 0.5 * (1.0 + th) + 0.5 * x * (1.0 - th * th) * GELU_C * (1.0 + 3.0 * GELU_A * x2)


def _adamw(w, g, m, v):
    m = ADAM_B1 * m + (1.0 - ADAM_B1) * g
    v = ADAM_B2 * v + (1.0 - ADAM_B2) * (g * g)
    m_hat = m / (1.0 - ADAM_B1 ** ADAM_STEP)
    v_hat = v / (1.0 - ADAM_B2 ** ADAM_STEP)
    delta = -ADAM_LR * (m_hat / (jnp.sqrt(v_hat) + ADAM_EPS) + ADAM_WD * w)
    return delta, m, v


def _my_place():
    return lax.axis_index("x"), lax.axis_index("y"), lax.axis_index("c")


def _gather_plan(shards):
    n = len(shards)
    out_shape = [jax.ShapeDtypeStruct((N_DEV,) + tuple(s.shape), s.dtype) for s in shards]
    scratch = [pltpu.SemaphoreType.DMA((n, 7)), pltpu.SemaphoreType.DMA((n, 7)), pltpu.SemaphoreType.DMA((n,))]

    def stages(x_refs, out_refs, sems):
        send_sems, recv_sems, local_sems = sems
        x, y, c = _my_place()
        me, sibling = (x, y, c), (x, y, 1 - c)
        chips = [(1 - x, y), (x, 1 - y), (1 - x, 1 - y)]

        def copy(i, k, block, to, own=False):
            px, py, pc = block
            dst = out_refs[i].at[4 * px + 2 * py + pc]
            return pltpu.make_async_remote_copy(
                src_ref=x_refs[i] if own else dst, dst_ref=dst, send_sem=send_sems.at[i, k],
                recv_sem=recv_sems.at[i, k], device_id=to, device_id_type=MESH)

        def mine(i):
            return pltpu.make_async_copy(x_refs[i], out_refs[i].at[4 * x + 2 * y + c], local_sems.at[i])

        def start():
            for i in range(n):
                mine(i).start()
                copy(i, 0, me, sibling, own=True).start()
                for j, chip in enumerate(chips):
                    copy(i, 1 + j, me, (*chip, c), own=True).start()

        def forward():
            for i in range(n):
                for j, chip in enumerate(chips):
                    copy(i, 1 + j, (*chip, c), me).wait_recv()
                    copy(i, 4 + j, (*chip, c), sibling).start()

        def finish():
            for i in range(n):
                copy(i, 0, sibling, me).wait_recv()
                copy(i, 0, me, sibling, own=True).wait_send()
                for j, chip in enumerate(chips):
                    copy(i, 4 + j, (*chip, 1 - c), me).wait_recv()
                    copy(i, 1 + j, me, (*chip, c), own=True).wait_send()
                    copy(i, 4 + j, (*chip, c), sibling).wait_send()
                mine(i).wait()

        return [start, forward, finish]

    return n, out_shape, scratch, stages


def _pair_plan(g42s):
    n = len(g42s)
    out_shape = [jax.ShapeDtypeStruct((4,) + tuple(g.shape[2:]), g.dtype) for g in g42s]
    scratch = [pltpu.SemaphoreType.DMA((n,)), pltpu.SemaphoreType.DMA((n,))]

    def stages(g_refs, out_refs, sems):
        send_sems, recv_sems = sems
        x, y, c = _my_place()

        def copy(i):
            return pltpu.make_async_remote_copy(
                src_ref=g_refs[i].at[:, 1 - c], dst_ref=out_refs[i], send_sem=send_sems.at[i],
                recv_sem=recv_sems.at[i], device_id=(x, y, 1 - c), device_id_type=MESH)

        def start():
            for i in range(n):
                copy(i).start()

        def finish():
            for i in range(n):
                copy(i).wait()

        return [start, finish]

    return n, out_shape, scratch, stages


def _chip_plan(s4s):
    n = len(s4s)
    out_shape = [jax.ShapeDtypeStruct((3,) + tuple(s.shape[1:]), s.dtype) for s in s4s]
    scratch = [pltpu.SemaphoreType.DMA((n, 3)), pltpu.SemaphoreType.DMA((n, 3))]

    def stages(s_refs, out_refs, sems):
        send_sems, recv_sems = sems
        x, y, c = _my_place()

        def copy(i, d):
            px, py = x ^ (d >> 1), y ^ (d & 1)
            return pltpu.make_async_remote_copy(
                src_ref=s_refs[i].at[2 * px + py], dst_ref=out_refs[i].at[d - 1], send_sem=send_sems.at[i, d - 1],
                recv_sem=recv_sems.at[i, d - 1], device_id=(px, py, c), device_id_type=MESH)

        def start():
            for i in range(n):
                for d in (1, 2, 3):
                    copy(i, d).start()

        def finish():
            for i in range(n):
                for d in (1, 2, 3):
                    copy(i, d).wait()

        return [start, finish]

    return n, out_shape, scratch, stages


def _comm_call(plan, arrays, name):
    n, out_shape, scratch, stages = plan

    def body(*refs):
        for stage in stages(refs[:n], refs[n:2 * n], refs[2 * n:]):
            stage()

    return pl.pallas_call(
        body, name=name, out_shape=out_shape, in_specs=[ANY] * n, out_specs=[ANY] * n, scratch_shapes=scratch,
    )(*arrays)


def _fused_call(body, *, name, grid, in_specs, out_specs, out_shape, scratch_shapes, args, comm=None):
    if not comm:
        out = pl.pallas_call(body, name=name, grid=grid, in_specs=in_specs, out_specs=out_specs, out_shape=out_shape,
                             scratch_shapes=scratch_shapes, compiler_params=_params(len(grid)))(*args)
        return out, []
    counts = [plan[0] for plan, _, _ in comm]
    n = sum(counts)
    n_in, n_out, n_scr = len(in_specs), len(out_specs), len(scratch_shapes)

    def fused(*refs):
        ins, refs = refs[:n_in], refs[n_in:]
        c_ins, refs = refs[:n], refs[n:]
        outs, refs = refs[:n_out], refs[n_out:]
        c_outs, refs = refs[:n], refs[n:]
        scr, c_scr = refs[:n_scr], refs[n_scr:]
        step = pl.program_id(0)
        for k in range(1, len(grid)):
            step = step * grid[k] + pl.program_id(k)
        todo, a0, s0 = [], 0, 0
        for (cnt, _, plan_scratch, stages), _, steps in comm:
            sems = c_scr[s0:s0 + len(plan_scratch)]
            todo += list(zip(stages(c_ins[a0:a0 + cnt], c_outs[a0:a0 + cnt], sems), steps))
            a0 += cnt
            s0 += len(plan_scratch)
        for stage, at in todo:
            if at == 0:
                pl.when(step == 0)(stage)
        body(*ins, *outs, *scr)
        for stage, at in todo:
            if at != 0:
                pl.when(step == at)(stage)

    c_shape = [s for plan, _, _ in comm for s in plan[1]]
    c_scratch = [s for plan, _, _ in comm for s in plan[2]]
    arrays = [a for _, arrs, _ in comm for a in arrs]
    out = pl.pallas_call(
        fused, name=name, grid=grid, in_specs=list(in_specs) + [ANY] * n, out_specs=list(out_specs) + [ANY] * n,
        out_shape=list(out_shape) + c_shape, scratch_shapes=list(scratch_shapes) + c_scratch,
        compiler_params=_params(len(grid)))(*args, *arrays)
    outs, c_outs, split, a0 = out[:n_out], out[n_out:], [], 0
    for cnt in counts:
        split.append(c_outs[a0:a0 + cnt])
        a0 += cnt
    return outs, split


def _schedule(comm, n_steps):
    out = []
    for make_plan, arrays in comm or []:
        steps = [0, (3 * n_steps) // 4, n_steps - 1] if make_plan is _gather_plan else [0, n_steps - 1]
        out.append((make_plan(arrays), arrays, steps))
    return out


def _row_tile(r):
    for t in (128, 64, 32, 16, 8):
        if r % t == 0:
            return t
    return r


def _pair_sum(g42, recv, place, name):
    _, _, r, cdim = g42.shape
    tr = _row_tile(r)

    def body(pl_ref, g_ref, r_ref, own_ref, s_ref):
        s_ref[...] = (g_ref[:, 0] + r_ref[...]).astype(BF16)
        q = pl_ref[1]
        own_ref[...] = g_ref[q, 0] + r_ref[q]

    return pl.pallas_call(
        body, name=name,
        grid_spec=pltpu.PrefetchScalarGridSpec(
            num_scalar_prefetch=1, grid=(r // tr,),
            in_specs=[pl.BlockSpec((4, 1, tr, cdim), lambda i, p: (0, p[0], i, 0)),
                      pl.BlockSpec((4, tr, cdim), lambda i, p: (0, i, 0))],
            out_specs=[pl.BlockSpec((tr, cdim), lambda i, p: (i, 0)),
                       pl.BlockSpec((4, tr, cdim), lambda i, p: (0, i, 0))]),
        out_shape=[jax.ShapeDtypeStruct((r, cdim), F32), jax.ShapeDtypeStruct((4, r, cdim), BF16)],
        compiler_params=_params(1),
    )(place, g42, recv)


def _final_sum_adamw(own, recv3, w, m, v, name):
    r, cdim = w.shape
    tr = _row_tile(r)

    def body(s_ref, r_ref, w_ref, m_ref, v_ref, g_out, d_out, m_out, v_out):
        g = s_ref[...] + r_ref[0].astype(F32) + r_ref[1].astype(F32) + r_ref[2].astype(F32)
        d, mn, vn = _adamw(w_ref[...], g, m_ref[...], v_ref[...])
        g_out[...] = g
        d_out[...] = d
        m_out[...] = mn
        v_out[...] = vn

    blk = pl.BlockSpec((tr, cdim), lambda i: (i, 0))
    shp = jax.ShapeDtypeStruct((r, cdim), F32)
    return pl.pallas_call(
        body, name=name, grid=(r // tr,),
        in_specs=[blk, pl.BlockSpec((3, tr, cdim), lambda i: (0, i, 0)), blk, blk, blk],
        out_specs=[blk, blk, blk, blk], out_shape=[shp, shp, shp, shp], compiler_params=_params(1),
    )(own, recv3, w, m, v)


def _small_sum_adamw(items, sums_only):
    n, ne = len(items), len(sums_only)

    def total(p_ref):
        g = p_ref[0]
        for k in range(1, N_DEV):
            g = g + p_ref[k]
        return g

    def body(*refs):
        ins, outs = refs[:4 * n + ne], refs[4 * n + ne:]
        for i in range(n):
            p_ref, w_ref, m_ref, v_ref = ins[4 * i:4 * i + 4]
            g = total(p_ref)
            d, mn, vn = _adamw(w_ref[...], g, m_ref[...], v_ref[...])
            for o_ref, val in zip(outs[4 * i:4 * i + 4], (g, d, mn, vn)):
                o_ref[...] = val
        for j in range(ne):
            outs[4 * n + j][...] = total(ins[4 * n + j])

    args = [a for item in items for a in item] + list(sums_only)
    shapes = [jax.ShapeDtypeStruct(w.shape, F32) for _, w, _, _ in items for _ in range(4)]
    shapes += [jax.ShapeDtypeStruct(p.shape[1:], F32) for p in sums_only]
    out = pl.pallas_call(
        body, name="small_sum_adamw", in_specs=[VMEM] * len(args), out_specs=[VMEM] * len(shapes), out_shape=shapes,
        compiler_params=_params(0),
    )(*args)
    return [out[4 * i:4 * i + 4] for i in range(n)], out[4 * n:]


def _adamw_plain(g, w, m, v):
    r, cdim = w.shape
    tr = r if r * cdim <= 64 * 1024 else _row_tile(r)

    def body(g_ref, w_ref, m_ref, v_ref, d_out, m_out, v_out):
        d, mn, vn = _adamw(w_ref[...], g_ref[...], m_ref[...], v_ref[...])
        d_out[...] = d
        m_out[...] = mn
        v_out[...] = vn

    blk = pl.BlockSpec((tr, cdim), lambda i: (i, 0))
    shp = jax.ShapeDtypeStruct((r, cdim), F32)
    return pl.pallas_call(
        body, name="adamw_plain", grid=(r // tr,), in_specs=[blk, blk, blk, blk],
        out_specs=[blk, blk, blk], out_shape=[shp, shp, shp], compiler_params=_params(1),
    )(g, w, m, v)


ADA_COLS = N_MOD * D // N_DEV
ADA_ROWS = 8 * N_DEV


def _ada_fwd(cpad, w_ada, b_blocks, mixer_shards):
    n_w, w_shape, w_scr, w_stages = _gather_plan(mixer_shards)
    _, _, c_scr, c_stages = _gather_plan([cpad])
    _, _, p_scr, p_stages = _gather_plan([jax.ShapeDtypeStruct((ADA_ROWS, ADA_COLS), F32)])

    def body(c_ref, wa_ref, b_ref, *refs):
        w_refs, refs = refs[:n_w], refs[n_w:]
        cg_ref, call_ref, mod_ref = refs[:3]
        wg_refs, refs = refs[3:3 + n_w], refs[3 + n_w:]
        part_ref, pg_ref = refs[:2]
        c_sems, p_sems, w_sems = refs[2:5], refs[5:8], refs[8:11]
        w_start, w_forward, w_finish = w_stages(w_refs, wg_refs, w_sems)
        w_start()
        for stage in c_stages([c_ref], [cg_ref], c_sems):
            stage()
        cv = cg_ref[:, 0:8, :].reshape(ADA_ROWS, D)
        call_ref[...] = cv
        part_ref[...] = _dot(cv * jax.nn.sigmoid(cv), wa_ref[...])
        for stage in p_stages([part_ref], [pg_ref], p_sems):
            stage()
        x, y, c = _my_place()
        r0 = pl.multiple_of(8 * (4 * x + 2 * y + c), 8)
        for k in range(N_DEV):
            mod_ref[:, k * ADA_COLS:(k + 1) * ADA_COLS] = pg_ref[k, pl.ds(r0, 8), :] + b_ref[k]
        w_forward()
        w_finish()

    out = pl.pallas_call(
        body, name="ada_fwd", in_specs=[VMEM, VMEM, VMEM] + [ANY] * n_w,
        out_specs=[VMEM, VMEM, VMEM] + [ANY] * n_w,
        out_shape=[jax.ShapeDtypeStruct((N_DEV,) + cpad.shape, F32), jax.ShapeDtypeStruct((ADA_ROWS, D), F32),
                   jax.ShapeDtypeStruct((8, N_MOD * D), F32)] + list(w_shape),
        scratch_shapes=[pltpu.VMEM((ADA_ROWS, ADA_COLS), F32), pltpu.VMEM((N_DEV, ADA_ROWS, ADA_COLS), F32)]
        + list(c_scr) + list(p_scr) + list(w_scr),
        compiler_params=_params(0),
    )(cpad, w_ada, b_blocks, *mixer_shards)
    return out[0], out[1], out[2], out[3:]


ADA_RIDER_ROW = 4


def _ada_bwd(dmod_blk, c_all, w_ada, m_w, v_w, b_blocks, m_b, v_b, g_w, g_m, g_v):
    _, _, g_scr, g_stages = _gather_plan([dmod_blk])
    rest = D - ADA_COLS

    def body(dm_ref, c_ref, w_ref, mw_ref, vw_ref, b_ref, mb_ref, vb_ref, gw_ref, gm_ref, gv_ref,
             gw_o, dw_o, mw_o, vw_o, gb_o, dbb_o, mb_o, vb_o, gg_o, dgg_o, mg_o, vg_o, dg_ref, *sems):
        for stage in g_stages([dm_ref], [dg_ref], sems):
            stage()
        x, y, c = _my_place()
        r0 = pl.multiple_of(8 * (4 * x + 2 * y + c), 8)
        cols = dg_ref[:, pl.ds(r0, 8), :].reshape(ADA_ROWS, ADA_COLS)
        cv = c_ref[...]
        gw = _dot_tn(cv * jax.nn.sigmoid(cv), cols)
        d, mn, vn = _adamw(w_ref[...], gw, mw_ref[...], vw_ref[...])
        gw_o[...] = gw
        dw_o[...] = d
        mw_o[...] = mn
        vw_o[...] = vn
        is_example = lax.broadcasted_iota(jnp.int32, (8, 1), 0) < ADA_RIDER_ROW
        blocks = []
        for k in range(N_DEV):
            s = dg_ref[0, 8 * k:8 * k + 8, :]
            for dev in range(1, N_DEV):
                s = s + dg_ref[dev, 8 * k:8 * k + 8, :]
            blocks.append(s)
            gb = _colsum(jnp.where(is_example, s, 0.0))
            cs = slice(k * ADA_COLS, (k + 1) * ADA_COLS)
            d, mn, vn = _adamw(b_ref[:, cs], gb, mb_ref[:, cs], vb_ref[:, cs])
            gb_o[:, cs] = gb
            dbb_o[:, cs] = d
            mb_o[:, cs] = mn
            vb_o[:, cs] = vn
        rider = jnp.concatenate([blocks[0][ADA_RIDER_ROW:ADA_RIDER_ROW + 1, :],
                                 blocks[1][ADA_RIDER_ROW:ADA_RIDER_ROW + 1, 0:rest]], axis=1)
        d, mn, vn = _adamw(gw_ref[...], rider, gm_ref[...], gv_ref[...])
        gg_o[...] = rider
        dgg_o[...] = d
        mg_o[...] = mn
        vg_o[...] = vn

    ws = jax.ShapeDtypeStruct(w_ada.shape, F32)
    bs = jax.ShapeDtypeStruct(b_blocks.shape, F32)
    gs = jax.ShapeDtypeStruct(g_w.shape, F32)
    return pl.pallas_call(
        body, name="ada_bwd", in_specs=[VMEM] * 11, out_specs=[VMEM] * 12,
        out_shape=[ws, ws, ws, ws, bs, bs, bs, bs, gs, gs, gs, gs],
        scratch_shapes=[pltpu.VMEM((N_DEV, ADA_ROWS, ADA_COLS), F32)] + list(g_scr),
        compiler_params=_params(0),
    )(dmod_blk, c_all, w_ada, m_w, v_w, b_blocks, m_b, v_b, g_w, g_m, g_v)


def _ssm_param_fn(lr, li, ldt, br, bi):
    dt = jnp.exp(ldt)
    mag = jnp.exp(lr * dt)
    ang = li * dt
    lbr = mag * jnp.cos(ang)
    lbi = mag * jnp.sin(ang)
    nr = lbr - 1.0
    den = lr * lr + li * li
    cr = (nr * lr + lbi * li) / den
    ci = (lbi * lr - nr * li) / den
    return lbr, lbi, cr * br - ci * bi, cr * bi + ci * br


def _ssm_prep(lr, li, ldt, br, bi):
    def body(lr_ref, li_ref, ldt_ref, br_ref, bi_ref, lbr_o, lbi_o, bbr_o, bbi_o):
        lbr, lbi, bbr, bbi = _ssm_param_fn(lr_ref[...], li_ref[...], ldt_ref[...], br_ref[...], bi_ref[...])
        lbr_o[...] = lbr
        lbi_o[...] = lbi
        bbr_o[...] = bbr
        bbi_o[...] = bbi

    row = jax.ShapeDtypeStruct(lr.shape, F32)
    mat = jax.ShapeDtypeStruct(br.shape, F32)
    return pl.pallas_call(
        body, name="ssm_prep", in_specs=[VMEM] * 5, out_specs=[VMEM] * 4,
        out_shape=[row, row, mat, mat], compiler_params=_params(0),
    )(lr, li, ldt, br, bi)


def _ssm_param_bwd(lr, li, ldt, br, bi, dlam8, dbbr, dbbi, dd8):
    nv = dlam8.shape[1]

    def body(lr_ref, li_ref, ldt_ref, br_ref, bi_ref, dl_ref, dbr_ref, dbi_ref, dd_ref,
             glr_o, gli_o, gldt_o, gbr_o, gbi_o, gd_o):
        halves_r, halves_i, halves_d = [], [], []
        for e in range(2):
            ar = dl_ref[0, e:e + 1, :]
            ai = dl_ref[1, e:e + 1, :]
            ad = dd_ref[e:e + 1, :]
            for b in range(1, nv // 2):
                ar = ar + dl_ref[0, 2 * b + e:2 * b + e + 1, :]
                ai = ai + dl_ref[1, 2 * b + e:2 * b + e + 1, :]
                ad = ad + dd_ref[2 * b + e:2 * b + e + 1, :]
            halves_r.append(ar)
            halves_i.append(ai)
            halves_d.append(ad)
        dlbr = jnp.concatenate(halves_r, axis=1)
        dlbi = jnp.concatenate(halves_i, axis=1)
        gd_o[...] = jnp.concatenate(halves_d, axis=1)
        _, vjp = jax.vjp(_ssm_param_fn, lr_ref[...], li_ref[...], ldt_ref[...], br_ref[...], bi_ref[...])
        glr, gli, gldt, gbr, gbi = vjp((dlbr, dlbi, dbr_ref[...], dbi_ref[...]))
        glr_o[...] = glr
        gli_o[...] = gli
        gldt_o[...] = gldt
        gbr_o[...] = gbr
        gbi_o[...] = gbi

    row = jax.ShapeDtypeStruct(lr.shape, F32)
    mat = jax.ShapeDtypeStruct(br.shape, F32)
    return pl.pallas_call(
        body, name="ssm_param_bwd", in_specs=[VMEM] * 9, out_specs=[VMEM] * 6,
        out_shape=[row, row, row, mat, mat, jax.ShapeDtypeStruct((1, SSM_W), F32)],
        compiler_params=_params(0),
    )(lr, li, ldt, br, bi, dlam8, dbbr, dbbi, dd8)


def _blockdiag(m):
    _, g, a, b = m.shape
    eye = jnp.eye(g, dtype=m.dtype)
    return jnp.einsum("egab,gk->egakb", m, eye).reshape(2, g * a, g * b)


def _blockdiag_take(t, a, b):
    return jnp.einsum("gagb->gab", t.reshape(GRP // 2, a, GRP // 2, b))


def _mixer_in_fwd(x, sh1, sc1, g_mix, w_in_b, comm=None):
    bsz, seq, _ = x.shape
    tt = min(seq, TT_MIX)

    def body(x_ref, sh_ref, sc_ref, g_ref, w_ref, u_ref, p_ref):
        xhat, _ = _rms(x_ref[0])
        h = xhat * g_ref[...] * (1.0 + sc_ref[0]) + sh_ref[0]
        z = _dot(h.astype(BF16), w_ref[...])
        u_ref[...] = z[:, :SSM_W].astype(BF16)
        p_ref[0] = z[:, SSM_W:]

    row = pl.BlockSpec((1, 1, D), lambda b, t: (b, 0, 0))
    return _fused_call(
        body, name="mixer_in_fwd", grid=(bsz, seq // tt),
        in_specs=[pl.BlockSpec((1, tt, D), lambda b, t: (b, t, 0)), row, row,
                  pl.BlockSpec((1, D), lambda b, t: (0, 0)), VMEM],
        out_specs=[pl.BlockSpec((tt, SSM_W), lambda b, t: (t, b)),
                   pl.BlockSpec((1, tt, POOL_W), lambda b, t: (b, t, 0))],
        out_shape=[jax.ShapeDtypeStruct((seq, bsz * SSM_W), BF16), jax.ShapeDtypeStruct((bsz, seq, POOL_W), F32)],
        scratch_shapes=[], args=(x, sh1, sc1, g_mix, w_in_b), comm=_schedule(comm, bsz * (seq // tt)))


def _ssm_project_in(ub, par0, bb_ref, s_re, s_im, row0, tlen, nv):
    for part, sref in ((0, s_re), (1, s_im)):
        for k in range(HALF_ST // 512):
            c0 = part * HALF_ST + k * 512
            a0 = _dot(ub, bb_ref[:, c0:c0 + 512])
            a1 = _dot(ub, bb_ref[:, 2 * HALF_ST + c0:2 * HALF_ST + c0 + 512])
            sref[pl.ds(row0, tlen), :, k * 512:(k + 1) * 512] = jnp.where(par0, a0, a1).reshape(tlen, nv, 512)


def _ssm_fwd(u2r, bb, cc, lam8, d8, tlen, comm=None):
    nv = lam8.shape[1]
    rows = nv * tlen
    n_chunks = u2r.shape[0] // rows

    def body(u_ref, bb_ref, cc_ref, lam_ref, d_ref, y_ref, xc_ref, xre_ref, xim_ref, s_re, s_im, st):
        @pl.when(pl.program_id(0) == 0)
        def _():
            st[...] = jnp.zeros_like(st)

        xc_ref[0] = st[...]
        ub = u_ref[...]
        u = ub.astype(F32)
        par0 = (lax.broadcasted_iota(jnp.int32, (rows, 1), 0) % 2) == 0
        _ssm_project_in(ub, par0, bb_ref, s_re, s_im, 0, tlen, nv)
        for hb in range(HALF_ST // 512):
            ls = slice(hb * 512, (hb + 1) * 512)
            lr = lam_ref[0, :, ls]
            li = lam_ref[1, :, ls]

            def step(t, carry, ls=ls, lr=lr, li=li):
                xr, xi = carry
                nr = lr * xr - li * xi + s_re[t, :, ls]
                ni = lr * xi + li * xr + s_im[t, :, ls]
                s_re[t, :, ls] = nr
                s_im[t, :, ls] = ni
                return nr, ni

            xr, xi = lax.fori_loop(0, tlen, step, (st[0, :, ls], st[1, :, ls]), unroll=8)
            st[0, :, ls] = xr
            st[1, :, ls] = xi
        xre = s_re[...].reshape(rows, HALF_ST).astype(BF16)
        xim = s_im[...].reshape(rows, HALF_ST).astype(BF16)
        xre_ref[...] = xre
        xim_ref[...] = xim
        y2 = _dot(xre, cc_ref[0:HALF_ST, :]) + _dot(xim, cc_ref[HALF_ST:, :])
        y = jnp.where(par0, y2[:, :HALF_CH], y2[:, HALF_CH:])
        skip = (u.reshape(tlen, nv, HALF_CH) * d_ref[...][None]).reshape(rows, HALF_CH)
        y_ref[...] = (y + skip).astype(BF16)

    st_blk = pl.BlockSpec((rows, HALF_ST), lambda c: (c, 0))
    st_shape = jax.ShapeDtypeStruct((u2r.shape[0], HALF_ST), BF16)
    return _fused_call(
        body, name="ssm_fwd", grid=(n_chunks,),
        in_specs=[pl.BlockSpec((rows, HALF_CH), lambda c: (c, 0)), VMEM, VMEM, VMEM, VMEM],
        out_specs=[pl.BlockSpec((rows, HALF_CH), lambda c: (c, 0)),
                   pl.BlockSpec((1, 2, nv, HALF_ST), lambda c: (c, 0, 0, 0)), st_blk, st_blk],
        out_shape=[jax.ShapeDtypeStruct(u2r.shape, BF16), jax.ShapeDtypeStruct((n_chunks, 2, nv, HALF_ST), F32),
                   st_shape, st_shape],
        scratch_shapes=[pltpu.VMEM((tlen, nv, HALF_ST), F32), pltpu.VMEM((tlen, nv, HALF_ST), F32),
                        pltpu.VMEM((2, nv, HALF_ST), F32)],
        args=(u2r, bb, cc, lam8, d8), comm=_schedule(comm, n_chunks))


def _pool_forward(ext, pv, pos, wp_ref, bp_ref):
    cur = ext
    zs, zls = [], []
    for gi, w in enumerate(POOL_WINDOWS):
        cur = cur + pltpu.roll(cur, w // 2, 0)
        sw = cur[POOL_HALO:, 0:128]
        z = sw / jnp.minimum(pos, float(w)) - pv[:, gi * 128:(gi + 1) * 128]
        zs.append(z)
        zls.append(_dot(z.astype(BF16), wp_ref[gi]) + bp_ref[:, gi * 128:(gi + 1) * 128])
        if gi + 1 < len(POOL_WINDOWS):
            cur = cur[:, 128:]
    return zs, zls


def _mixer_out_fwd(y2, p, x, gt1, w_glu_b, b_glu, w_pool_b, b_pool, pscale, w_out_b):
    bsz, seq, _ = x.shape
    tt = min(seq, TT_MIX)

    def body(y_ref, p_ref, x_ref, gt_ref, wg_ref, bg_ref, wp_ref, bp_ref, ps_ref, wo_ref, x1_ref, mix_ref, ext):
        ti = pl.program_id(1)

        @pl.when(ti == 0)
        def _():
            ext[0:POOL_HALO, :] = jnp.zeros((POOL_HALO, POOL_W), F32)

        pv = p_ref[0]
        ext[POOL_HALO:, :] = pv
        pos = (ti * tt + lax.broadcasted_iota(jnp.int32, (tt, 1), 0) + 1).astype(F32)
        _, zls = _pool_forward(ext[...], pv, pos, wp_ref, bp_ref)
        ext[0:POOL_HALO, :] = pv[tt - POOL_HALO:, :]
        a = _gelu(y_ref[...].astype(F32))
        gl = _dot(a.astype(BF16), wg_ref[...]) + bg_ref[...]
        y_ssm = gl[:, :SSM_W] * jax.nn.sigmoid(gl[:, SSM_W:])
        y_pool = [zl * ps_ref[:, gi * 128:(gi + 1) * 128] for gi, zl in enumerate(zls)]
        mixcat = jnp.concatenate([y_ssm] + y_pool, axis=1).astype(BF16)
        mix_ref[0] = mixcat
        x1_ref[0] = x_ref[0] + gt_ref[0] * _dot(mixcat, wo_ref[...])

    xt = pl.BlockSpec((1, tt, D), lambda b, t: (b, t, 0))
    return pl.pallas_call(
        body, name="mixer_out_fwd", grid=(bsz, seq // tt),
        in_specs=[pl.BlockSpec((tt, SSM_W), lambda b, t: (t, b)),
                  pl.BlockSpec((1, tt, POOL_W), lambda b, t: (b, t, 0)), xt,
                  pl.BlockSpec((1, 1, D), lambda b, t: (b, 0, 0)), VMEM, VMEM, VMEM, VMEM, VMEM, VMEM],
        out_specs=[xt, xt],
        out_shape=[jax.ShapeDtypeStruct(x.shape, F32), jax.ShapeDtypeStruct(x.shape, BF16)],
        scratch_shapes=[pltpu.VMEM((POOL_HALO + tt, POOL_W), F32)],
        compiler_params=_params(2),
    )(y2, p, x, gt1, w_glu_b, b_glu, w_pool_b, b_pool, pscale, w_out_b)


def _conv_gate(g, ge, wc, bc):
    g1 = pltpu.roll(ge, 1, 0)[CONV_HALO:]
    g2 = pltpu.roll(ge, 2, 0)[CONV_HALO:]
    return wc[2:3] * g + wc[1:2] * g1 + wc[0:1] * g2 + bc, g1, g2


def _ffn_fwd(x1, tgt, sh2, sc2, gt2, g_ffn, w_up_b, w_conv, b_conv, w_down_b, g_fin):
    bsz, seq, _ = x1.shape
    tt = min(seq, TT_FFN)
    n_t = seq // tt
    n_ck = DFF // FF_CH

    def body(x1_ref, tg_ref, sh_ref, sc_ref, gt_ref, gf_ref, wu_ref, wc_ref, bc_ref, wd_ref, gfin_ref,
             h2_ref, v_ref, g_ref, gc_ref, act_ref, ddn_ref, dx2_ref, loss_ref, dgfin_ref, dgt_ref, gext, lacc):
        b = pl.program_id(0)
        ti = pl.program_id(1)

        @pl.when((b == 0) & (ti == 0))
        def _():
            lacc[...] = jnp.zeros_like(lacc)
            dgfin_ref[...] = jnp.zeros_like(dgfin_ref)

        @pl.when(ti == 0)
        def _():
            dgt_ref[...] = jnp.zeros_like(dgt_ref)
            gext[:, 0:CONV_HALO, :] = jnp.zeros((n_ck, CONV_HALO, FF_CH), F32)

        x1v = x1_ref[0]
        xhat, _ = _rms(x1v)
        h2b = (xhat * gf_ref[...] * (1.0 + sc_ref[0]) + sh_ref[0]).astype(BF16)
        h2_ref[0] = h2b
        dn = jnp.zeros((tt, D), F32)
        for ck in range(n_ck):
            c0 = ck * FF_CH
            v = _dot_nt(h2b, wu_ref[c0:c0 + FF_CH, :])
            g = _dot_nt(h2b, wu_ref[DFF + c0:DFF + c0 + FF_CH, :])
            v_ref[0, :, c0:c0 + FF_CH] = v.astype(BF16)
            g_ref[0, :, c0:c0 + FF_CH] = g.astype(BF16)
            gext[ck, CONV_HALO:, :] = g
            gc, _, _ = _conv_gate(g, gext[ck], wc_ref[:, c0:c0 + FF_CH], bc_ref[:, c0:c0 + FF_CH])
            gext[ck, 0:CONV_HALO, :] = g[tt - CONV_HALO:, :]
            gc_ref[0, :, c0:c0 + FF_CH] = gc.astype(BF16)
            actb = (gc * jax.nn.sigmoid(gc) * v).astype(BF16)
            act_ref[0, :, c0:c0 + FF_CH] = actb
            dn = dn + _dot(actb, wd_ref[c0:c0 + FF_CH, :])
        gt = gt_ref[0]
        xh3, r3 = _rms(x1v + gt * dn)
        gfin = gfin_ref[...]
        diff = xh3 * gfin - tg_ref[0]
        lacc[...] += _colsum(diff * diff)
        dy = diff * (1.0 / D)
        dgfin_ref[...] += _colsum(dy * xh3)
        dx2 = _rms_bwd(dy * gfin, xh3, r3)
        dx2_ref[0] = dx2
        dgt_ref[0] += _colsum(dx2 * dn)
        ddn_ref[0] = (gt * dx2).astype(BF16)

        @pl.when((b == bsz - 1) & (ti == n_t - 1))
        def _():
            loss_ref[...] = jnp.full(loss_ref.shape, 0.5 / D * jnp.sum(lacc[...]), F32)

    xt = pl.BlockSpec((1, tt, D), lambda b, t: (b, t, 0))
    ft = pl.BlockSpec((1, tt, DFF), lambda b, t: (b, t, 0))
    row = pl.BlockSpec((1, 1, D), lambda b, t: (b, 0, 0))
    vec = pl.BlockSpec((1, D), lambda b, t: (0, 0))
    ff = jax.ShapeDtypeStruct((bsz, seq, DFF), BF16)
    xs = jax.ShapeDtypeStruct((bsz, seq, D), BF16)
    return pl.pallas_call(
        body, name="ffn_fwd", grid=(bsz, n_t),
        in_specs=[xt, xt, row, row, row, vec, VMEM, VMEM, VMEM, VMEM, vec],
        out_specs=[xt, ft, ft, ft, ft, xt, xt, pl.BlockSpec((1, 128), lambda b, t: (0, 0)), vec, row],
        out_shape=[xs, ff, ff, ff, ff, xs, jax.ShapeDtypeStruct((bsz, seq, D), F32),
                   jax.ShapeDtypeStruct((1, 128), F32), jax.ShapeDtypeStruct((1, D), F32),
                   jax.ShapeDtypeStruct((bsz, 1, D), F32)],
        scratch_shapes=[pltpu.VMEM((n_ck, CONV_HALO + tt, FF_CH), F32), pltpu.VMEM((1, D), F32)],
        compiler_params=_params(2),
    )(x1, tgt, sh2, sc2, gt2, g_ffn, w_up_b, w_conv, b_conv, w_down_b, g_fin)


def _ffn_bwd(ddn, gq, gcq, vq, x1, dx2, sh2, sc2, g_ffn, w_conv, w_down_b, w_up_b):
    bsz, seq, _ = x1.shape
    tt = min(seq, TT_FFN)
    n_t = seq // tt
    n_ck = DFF // FF_CH
    ext_rows = tt + CONV_HALO

    def body(ddn_ref, g_ref, gc_ref, v_ref, x1_ref, dx2_ref, sh_ref, sc_ref, gf_ref, wc_ref, wd_ref,
             wu_ref, dup_ref, dx1_ref, dsh_ref, dsc_ref, dgf_ref, dwc_ref, dbc_ref, dext):
        b = pl.program_id(0)
        i = pl.program_id(1)

        @pl.when((b == 0) & (i == 0))
        def _():
            dgf_ref[...] = jnp.zeros_like(dgf_ref)
            dwc_ref[...] = jnp.zeros_like(dwc_ref)
            dbc_ref[...] = jnp.zeros_like(dbc_ref)

        @pl.when(i == 0)
        def _():
            dsh_ref[...] = jnp.zeros_like(dsh_ref)
            dsc_ref[...] = jnp.zeros_like(dsc_ref)
            dext[:, tt:, :] = jnp.zeros((n_ck, CONV_HALO, FF_CH), F32)

        ddnv = ddn_ref[0]
        dh2 = jnp.zeros((tt, D), F32)
        for ck in range(n_ck):
            c0 = ck * FF_CH
            dact = _dot_nt(ddnv, wd_ref[c0:c0 + FF_CH, :])
            g = g_ref[0, :, c0:c0 + FF_CH].astype(F32)
            gc = gc_ref[0, :, c0:c0 + FF_CH].astype(F32)
            v = v_ref[0, :, c0:c0 + FF_CH].astype(F32)
            wc = wc_ref[:, c0:c0 + FF_CH]
            sg = jax.nn.sigmoid(gc)
            silu = gc * sg
            dv = dact * silu
            dgc = dact * v * (sg + silu * (1.0 - sg))
            dext[ck, 0:tt, :] = dgc
            de = dext[ck]
            d1 = pltpu.roll(de, ext_rows - 1, 0)[0:tt]
            d2 = pltpu.roll(de, ext_rows - 2, 0)[0:tt]
            dext[ck, tt:, :] = dgc[0:CONV_HALO, :]
            dbc_ref[:, c0:c0 + FF_CH] += _colsum(dgc)
            dwc_ref[0:1, c0:c0 + FF_CH] += _colsum(d2 * g)
            dwc_ref[1:2, c0:c0 + FF_CH] += _colsum(d1 * g)
            dwc_ref[2:3, c0:c0 + FF_CH] += _colsum(dgc * g)
            dg = wc[2:3] * dgc + wc[1:2] * d1 + wc[0:1] * d2
            dvb = dv.astype(BF16)
            dgb = dg.astype(BF16)
            dup_ref[0, :, c0:c0 + FF_CH] = dvb
            dup_ref[0, :, DFF + c0:DFF + c0 + FF_CH] = dgb
            dh2 = dh2 + _dot(dvb, wu_ref[c0:c0 + FF_CH, :]) + _dot(dgb, wu_ref[DFF + c0:DFF + c0 + FF_CH, :])
        xhat, rstd = _rms(x1_ref[0])
        gf = gf_ref[...]
        dsh_ref[0] += _colsum(dh2)
        dsc_ref[0] += _colsum(dh2 * xhat * gf)
        t = dh2 * (1.0 + sc_ref[0])
        dgf_ref[...] += _colsum(t * xhat)
        dx1_ref[0] = dx2_ref[0] + _rms_bwd(t * gf, xhat, rstd)

    def rev(b, t):
        return (b, n_t - 1 - t, 0)

    xt = pl.BlockSpec((1, tt, D), rev)
    ft = pl.BlockSpec((1, tt, DFF), rev)
    row = pl.BlockSpec((1, 1, D), lambda b, t: (b, 0, 0))
    vec = pl.BlockSpec((1, D), lambda b, t: (0, 0))
    rows = jax.ShapeDtypeStruct((bsz, 1, D), F32)
    return pl.pallas_call(
        body, name="ffn_bwd", grid=(bsz, n_t),
        in_specs=[xt, ft, ft, ft, xt, xt, row, row, vec, VMEM, VMEM, VMEM],
        out_specs=[pl.BlockSpec((1, tt, 2 * DFF), rev), xt, row, row, vec,
                   pl.BlockSpec((3, DFF), lambda b, t: (0, 0)), pl.BlockSpec((1, DFF), lambda b, t: (0, 0))],
        out_shape=[jax.ShapeDtypeStruct((bsz, seq, 2 * DFF), BF16), jax.ShapeDtypeStruct((bsz, seq, D), F32),
                   rows, rows, jax.ShapeDtypeStruct((1, D), F32), jax.ShapeDtypeStruct((3, DFF), F32),
                   jax.ShapeDtypeStruct((1, DFF), F32)],
        scratch_shapes=[pltpu.VMEM((n_ck, ext_rows, FF_CH), F32)],
        compiler_params=_params(2),
    )(ddn, gq, gcq, vq, x1, dx2, sh2, sc2, g_ffn, w_conv, w_down_b, w_up_b)


def _wgrad(a, b, bk1, bk2, name):
    n, k1 = a.shape
    _, k2 = b.shape
    tt = min(n, TT_WGRAD)

    def body(a_ref, b_ref, o_ref):
        @pl.when(pl.program_id(2) == 0)
        def _():
            o_ref[...] = jnp.zeros_like(o_ref)

        o_ref[...] += _dot_tn(a_ref[...], b_ref[...])

    return pl.pallas_call(
        body, name=name, grid=(k1 // bk1, k2 // bk2, n // tt),
        in_specs=[pl.BlockSpec((tt, bk1), lambda h, j, i: (i, h)), pl.BlockSpec((tt, bk2), lambda h, j, i: (i, j))],
        out_specs=pl.BlockSpec((bk1, bk2), lambda h, j, i: (h, j)),
        out_shape=jax.ShapeDtypeStruct((k1, k2), F32), compiler_params=_params(3),
    )(a, b)


def _mixer_out_bwd(dx1, mixcat, y2, p, gt1, w_glu_b, b_glu, w_pool_b, b_pool, pscale, w_out_b, comm=None):
    bsz, seq, _ = dx1.shape
    tt = min(seq, TT_MIX)
    n_t = seq // tt
    ext_rows = tt + POOL_HALO

    def body(dx1_ref, mc_ref, y_ref, p_ref, ph_ref, gt_ref, wg_ref, bg_ref, wp_ref, bp_ref, ps_ref, wo_ref,
             dy_ref, dp_ref, dwo_ref, dwg_ref, dbg_ref, dwp_ref, dbp_ref, dps_ref, dgt_ref, ext, qext):
        b = pl.program_id(0)
        i = pl.program_id(1)
        tile = n_t - 1 - i

        @pl.when((b == 0) & (i == 0))
        def _():
            for r in (dwo_ref, dwg_ref, dbg_ref, dwp_ref, dbp_ref, dps_ref):
                r[...] = jnp.zeros_like(r)

        @pl.when(i == 0)
        def _():
            dgt_ref[...] = jnp.zeros_like(dgt_ref)
            qext[tt:, :] = jnp.zeros((POOL_HALO, POOL_W), F32)

        dx1v = dx1_ref[0]
        mc = mc_ref[0]
        dgt_ref[0] += _colsum(dx1v * _dot(mc, wo_ref[...]))
        dmixed = (gt_ref[0] * dx1v).astype(BF16)
        dwo_ref[...] += _dot_tn(mc, dmixed)
        dmc = _dot_nt(dmixed, wo_ref[...])
        pv = p_ref[0]
        ext[0:POOL_HALO, :] = ph_ref[0] * (tile > 0).astype(F32)
        ext[POOL_HALO:, :] = pv
        pos = (tile * tt + lax.broadcasted_iota(jnp.int32, (tt, 1), 0) + 1).astype(F32)
        zs, zls = _pool_forward(ext[...], pv, pos, wp_ref, bp_ref)
        dzs = []
        for gi, w in enumerate(POOL_WINDOWS):
            cs = slice(gi * 128, (gi + 1) * 128)
            dyp = dmc[:, SSM_W + gi * 128:SSM_W + (gi + 1) * 128]
            dps_ref[:, cs] += _colsum(dyp * zls[gi])
            dzl = dyp * ps_ref[:, cs]
            dbp_ref[:, cs] += _colsum(dzl)
            dzlb = dzl.astype(BF16)
            dwp_ref[gi] += _dot_tn(zs[gi].astype(BF16), dzlb)
            dz = _dot_nt(dzlb, wp_ref[gi])
            dzs.append(dz)
            qext[0:tt, cs] = dz / jnp.minimum(pos, float(w))
        cur = qext[...]
        dps = []
        for gi, w in enumerate(POOL_WINDOWS):
            cur = cur + pltpu.roll(cur, ext_rows - w // 2, 0)
            dps.append(cur[0:tt, 0:128] - dzs[gi])
            if gi + 1 < len(POOL_WINDOWS):
                cur = cur[:, 128:]
        qhead = qext[0:POOL_HALO, :]
        qext[tt:, :] = qhead
        dp_ref[0] = jnp.concatenate(dps, axis=1)
        yv = y_ref[...].astype(F32)
        ab = _gelu(yv).astype(BF16)
        gl = _dot(ab, wg_ref[...]) + bg_ref[...]
        val = gl[:, :SSM_W]
        sg = jax.nn.sigmoid(gl[:, SSM_W:])
        dys = dmc[:, :SSM_W]
        dgl = jnp.concatenate([dys * sg, dys * val * sg * (1.0 - sg)], axis=1)
        dbg_ref[...] += _colsum(dgl)
        dglb = dgl.astype(BF16)
        dwg_ref[...] += _dot_tn(ab, dglb)
        dy_ref[...] = (_dot_nt(dglb, wg_ref[...]) * _gelu_grad(yv)).astype(BF16)

    def rev(b, t):
        return (b, n_t - 1 - t, 0)

    def halo(b, t):
        return (b, jnp.maximum((n_t - 1 - t) * (tt // POOL_HALO) - 1, 0), 0)

    xt = pl.BlockSpec((1, tt, D), rev)
    pt = pl.BlockSpec((1, tt, POOL_W), rev)
    yt = pl.BlockSpec((tt, SSM_W), lambda b, t: (n_t - 1 - t, b))

    def whole(shape):
        return pl.BlockSpec(shape, lambda b, t: (0,) * len(shape))

    return _fused_call(
        body, name="mixer_out_bwd", grid=(bsz, n_t),
        in_specs=[xt, xt, yt, pt, pl.BlockSpec((1, POOL_HALO, POOL_W), halo),
                  pl.BlockSpec((1, 1, D), lambda b, t: (b, 0, 0)), VMEM, VMEM, VMEM, VMEM, VMEM, VMEM],
        out_specs=[yt, pt, whole((D, D)), whole((SSM_W, 2 * SSM_W)), whole((1, 2 * SSM_W)),
                   whole((4, 128, 128)), whole((1, POOL_W)), whole((1, POOL_W)),
                   pl.BlockSpec((1, 1, D), lambda b, t: (b, 0, 0))],
        out_shape=[jax.ShapeDtypeStruct(y2.shape, BF16), jax.ShapeDtypeStruct(p.shape, F32),
                   jax.ShapeDtypeStruct((D, D), F32), jax.ShapeDtypeStruct((SSM_W, 2 * SSM_W), F32),
                   jax.ShapeDtypeStruct((1, 2 * SSM_W), F32), jax.ShapeDtypeStruct((4, 128, 128), F32),
                   jax.ShapeDtypeStruct((1, POOL_W), F32), jax.ShapeDtypeStruct((1, POOL_W), F32),
                   jax.ShapeDtypeStruct((bsz, 1, D), F32)],
        scratch_shapes=[pltpu.VMEM((POOL_HALO + tt, POOL_W), F32), pltpu.VMEM((ext_rows, POOL_W), F32)],
        args=(dx1, mixcat, y2, p, p, gt1, w_glu_b, b_glu, w_pool_b, b_pool, pscale, w_out_b),
        comm=_schedule(comm, bsz * n_t))


def _ssm_bwd(dy2r, u2r, xc, xs_re, xs_im, bb, cc, lam8, d8, tlen, comm=None):
    nv = lam8.shape[1]
    rows = nv * tlen
    n_chunks = u2r.shape[0] // rows

    def body(dy_ref, u_ref, xc_ref, xre_ref, xim_ref, bb_ref, cc_ref, lam_ref, d_ref,
             du_ref, dcc_ref, dbb_ref, dlam_ref, dd_ref, s_re, s_im, g_re, g_im, gst):
        i = pl.program_id(0)

        @pl.when(i == 0)
        def _():
            for r in (gst, dcc_ref, dbb_ref, dlam_ref, dd_ref):
                r[...] = jnp.zeros_like(r)

        u = u_ref[...].astype(F32)
        dy = dy_ref[...].astype(F32)
        par0 = (lax.broadcasted_iota(jnp.int32, (rows, 1), 0) % 2) == 0
        xre = xre_ref[...]
        xim = xim_ref[...]
        s_re[0] = xc_ref[0, 0]
        s_im[0] = xc_ref[0, 1]
        s_re[pl.ds(1, tlen)] = xre.astype(F32).reshape(tlen, nv, HALF_ST)
        s_im[pl.ds(1, tlen)] = xim.astype(F32).reshape(tlen, nv, HALF_ST)
        zero = jnp.zeros_like(dy)
        dy2 = jnp.concatenate([jnp.where(par0, dy, zero), jnp.where(par0, zero, dy)], axis=1).astype(BF16)
        u2 = jnp.concatenate([jnp.where(par0, u, zero), jnp.where(par0, zero, u)], axis=1).astype(BF16)
        dcc_ref[0:HALF_ST, :] += _dot_tn(xre, dy2)
        dcc_ref[HALF_ST:, :] += _dot_tn(xim, dy2)
        for part, gref in ((0, g_re), (1, g_im)):
            for k in range(HALF_ST // 512):
                r0 = part * HALF_ST + k * 512
                gref[:, :, k * 512:(k + 1) * 512] = _dot_nt(dy2, cc_ref[r0:r0 + 512, :]).reshape(tlen, nv, 512)
        for hb in range(HALF_ST // 512):
            ls = slice(hb * 512, (hb + 1) * 512)
            lr = lam_ref[0, :, ls]
            li = lam_ref[1, :, ls]

            def bstep(k, carry, ls=ls, lr=lr, li=li):
                t = tlen - 1 - k
                gr, gi, ar, ai = carry
                ngr = g_re[t, :, ls] + lr * gr + li * gi
                ngi = g_im[t, :, ls] + lr * gi - li * gr
                g_re[t, :, ls] = ngr
                g_im[t, :, ls] = ngi
                xpr = s_re[t, :, ls]
                xpi = s_im[t, :, ls]
                return ngr, ngi, ar + ngr * xpr + ngi * xpi, ai + ngi * xpr - ngr * xpi

            init = (gst[0, :, ls], gst[1, :, ls], dlam_ref[0, :, ls], dlam_ref[1, :, ls])
            gr, gi, ar, ai = lax.fori_loop(0, tlen, bstep, init, unroll=4)
            gst[0, :, ls] = gr
            gst[1, :, ls] = gi
            dlam_ref[0, :, ls] = ar
            dlam_ref[1, :, ls] = ai
        gre = g_re[...].reshape(rows, HALF_ST).astype(BF16)
        gim = g_im[...].reshape(rows, HALF_ST).astype(BF16)
        du0 = _dot_nt(gre, bb_ref[:, 0:HALF_ST]) + _dot_nt(gim, bb_ref[:, HALF_ST:2 * HALF_ST])
        du1 = _dot_nt(gre, bb_ref[:, 2 * HALF_ST:3 * HALF_ST]) + _dot_nt(gim, bb_ref[:, 3 * HALF_ST:])
        skip = (dy.reshape(tlen, nv, HALF_CH) * d_ref[...][None]).reshape(rows, HALF_CH)
        du_ref[...] = (jnp.where(par0, du0, du1) + skip).astype(BF16)
        dbb_ref[:, 0:HALF_ST] += _dot_tn(u2, gre)
        dbb_ref[:, HALF_ST:] += _dot_tn(u2, gim)
        dd_ref[...] += jnp.sum((dy * u).reshape(tlen, nv, HALF_CH), axis=0)

        @pl.when(i == n_chunks - 1)
        def _():
            dcc_ref[HALF_ST:, :] = -dcc_ref[HALF_ST:, :]

    def rev(c):
        return (n_chunks - 1 - c, 0)

    def whole(shape):
        return pl.BlockSpec(shape, lambda c: (0,) * len(shape))

    blk = pl.BlockSpec((rows, HALF_CH), rev)
    st_blk = pl.BlockSpec((rows, HALF_ST), rev)
    return _fused_call(
        body, name="ssm_bwd", grid=(n_chunks,),
        in_specs=[blk, blk, pl.BlockSpec((1, 2, nv, HALF_ST), lambda c: (n_chunks - 1 - c, 0, 0, 0)),
                  st_blk, st_blk, VMEM, VMEM, VMEM, VMEM],
        out_specs=[blk, whole((2 * HALF_ST, SSM_W)), whole((SSM_W, 2 * HALF_ST)), whole((2, nv, HALF_ST)),
                   whole((nv, HALF_CH))],
        out_shape=[jax.ShapeDtypeStruct(u2r.shape, BF16), jax.ShapeDtypeStruct((2 * HALF_ST, SSM_W), F32),
                   jax.ShapeDtypeStruct((SSM_W, 2 * HALF_ST), F32), jax.ShapeDtypeStruct((2, nv, HALF_ST), F32),
                   jax.ShapeDtypeStruct((nv, HALF_CH), F32)],
        scratch_shapes=[pltpu.VMEM((tlen + 1, nv, HALF_ST), F32), pltpu.VMEM((tlen + 1, nv, HALF_ST), F32),
                        pltpu.VMEM((tlen, nv, HALF_ST), F32), pltpu.VMEM((tlen, nv, HALF_ST), F32),
                        pltpu.VMEM((2, nv, HALF_ST), F32)],
        args=(dy2r, u2r, xc, xs_re, xs_im, bb, cc, lam8, d8), comm=_schedule(comm, n_chunks))


def _mixer_in_bwd(du2, dp, x, dx1, sh1, sc1, g_mix, w_in_b, comm=None):
    bsz, seq, _ = x.shape
    tt = min(seq, TT_MIX)

    def body(du_ref, dp_ref, x_ref, dx1_ref, sh_ref, sc_ref, g_ref, w_ref,
             dx_ref, dw_ref, dsh_ref, dsc_ref, dg_ref):
        b = pl.program_id(0)
        ti = pl.program_id(1)

        @pl.when((b == 0) & (ti == 0))
        def _():
            dw_ref[...] = jnp.zeros_like(dw_ref)
            dg_ref[...] = jnp.zeros_like(dg_ref)

        @pl.when(ti == 0)
        def _():
            dsh_ref[...] = jnp.zeros_like(dsh_ref)
            dsc_ref[...] = jnp.zeros_like(dsc_ref)

        dz = jnp.concatenate([du_ref[...], dp_ref[0].astype(BF16)], axis=1)
        xhat, rstd = _rms(x_ref[0])
        g = g_ref[...]
        sc = sc_ref[0]
        a = xhat * g
        h = (a * (1.0 + sc) + sh_ref[0]).astype(BF16)
        dw_ref[...] += _dot_tn(h, dz)
        dh = _dot_nt(dz, w_ref[...])
        dsh_ref[0] += _colsum(dh)
        dsc_ref[0] += _colsum(dh * a)
        t = dh * (1.0 + sc)
        dg_ref[...] += _colsum(t * xhat)
        dx_ref[0] = dx1_ref[0] + _rms_bwd(t * g, xhat, rstd)

    xt = pl.BlockSpec((1, tt, D), lambda b, t: (b, t, 0))
    row = pl.BlockSpec((1, 1, D), lambda b, t: (b, 0, 0))
    vec = pl.BlockSpec((1, D), lambda b, t: (0, 0))
    rows = jax.ShapeDtypeStruct((bsz, 1, D), F32)
    return _fused_call(
        body, name="mixer_in_bwd", grid=(bsz, seq // tt),
        in_specs=[pl.BlockSpec((tt, SSM_W), lambda b, t: (t, b)),
                  pl.BlockSpec((1, tt, POOL_W), lambda b, t: (b, t, 0)), xt, xt, row, row, vec, VMEM],
        out_specs=[xt, pl.BlockSpec((D, D), lambda b, t: (0, 0)), row, row, vec],
        out_shape=[jax.ShapeDtypeStruct(x.shape, F32), jax.ShapeDtypeStruct((D, D), F32), rows, rows,
                   jax.ShapeDtypeStruct((1, D), F32)],
        scratch_shapes=[], args=(du2, dp, x, dx1, sh1, sc1, g_mix, w_in_b),
        comm=_schedule(comm, bsz * (seq // tt)))


def kernel(x, c, w_ada, b_ada, g_norm_mix, w_in, ssm_lam_re, ssm_lam_im, ssm_log_dt, ssm_b_re, ssm_b_im, ssm_c_re, ssm_c_im, ssm_d, w_glu, b_glu, w_pool, b_pool, pool_scale, w_out, g_norm_ffn, w_up, w_conv, b_conv, w_down, g_norm_final, loss_target, m_w_ada, m_b_ada, m_g_norm_mix, m_w_in, m_ssm_lam_re, m_ssm_lam_im, m_ssm_log_dt, m_ssm_b_re, m_ssm_b_im, m_ssm_c_re, m_ssm_c_im, m_ssm_d, m_w_glu, m_b_glu, m_w_pool, m_b_pool, m_pool_scale, m_w_out, m_g_norm_ffn, m_w_up, m_w_conv, m_b_conv, m_w_down, m_g_norm_final, v_w_ada, v_b_ada, v_g_norm_mix, v_w_in, v_ssm_lam_re, v_ssm_lam_im, v_ssm_log_dt, v_ssm_b_re, v_ssm_b_im, v_ssm_c_re, v_ssm_c_im, v_ssm_d, v_w_glu, v_b_glu, v_w_pool, v_b_pool, v_pool_scale, v_w_out, v_g_norm_ffn, v_w_up, v_w_conv, v_b_conv, v_w_down, v_g_norm_final):
    bsz, seq, _ = x.shape
    assert 2 * bsz == 8 and seq % 128 == 0
    px, py, pc = _my_place()
    me = 4 * px + 2 * py + pc
    place = jnp.stack([pc, 2 * px + py]).astype(jnp.int32)
    ncol = ADA_COLS

    cpad = jnp.zeros((16, D), F32).at[0:bsz].set(c).at[8:11, 0:352].set(w_conv[0])
    cg, c_all, mod8, (g_in,) = _ada_fwd(cpad, w_ada[0], b_ada.reshape(N_DEV, 1, ncol), [w_in[0].astype(BF16)])
    w_conv_f = cg[:, 8:11, 0:352].transpose(1, 0, 2).reshape(3, DFF)
    w_in_b = g_in.reshape(D, D)
    sh1, sc1, gt1, sh2, sc2, gt2 = [mod8[0:bsz, k * D:(k + 1) * D].reshape(bsz, 1, D) for k in range(N_MOD)]

    lam_r = ssm_lam_re[0].reshape(1, GRP * NST)
    lam_i = ssm_lam_im[0].reshape(1, GRP * NST)
    ldt = jnp.repeat(ssm_log_dt[0], NST).reshape(1, GRP * NST)
    b_r = ssm_b_re[0].transpose(2, 0, 1).reshape(GCH, GRP * NST)
    b_i = ssm_b_im[0].transpose(2, 0, 1).reshape(GCH, GRP * NST)
    lbr, lbi, bbr, bbi = _ssm_prep(lam_r, lam_i, ldt, b_r, b_i)
    lam8 = jnp.stack([jnp.tile(lbr.reshape(2, HALF_ST), (bsz, 1)), jnp.tile(lbi.reshape(2, HALF_ST), (bsz, 1))])
    bd_r = _blockdiag(bbr.reshape(GCH, 2, GRP // 2, NST).transpose(1, 2, 0, 3))
    bd_i = _blockdiag(bbi.reshape(GCH, 2, GRP // 2, NST).transpose(1, 2, 0, 3))
    bb = jnp.concatenate([bd_r[0], bd_i[0], bd_r[1], bd_i[1]], axis=1).astype(BF16)
    cd_r = _blockdiag(ssm_c_re[0].reshape(2, GRP // 2, GCH, NST).transpose(0, 1, 3, 2))
    cd_i = _blockdiag(ssm_c_im[0].reshape(2, GRP // 2, GCH, NST).transpose(0, 1, 3, 2))
    cc = jnp.concatenate([jnp.concatenate([cd_r[0], cd_r[1]], axis=1),
                          jnp.concatenate([-cd_i[0], -cd_i[1]], axis=1)], axis=0).astype(BF16)
    d8 = jnp.tile(ssm_d[0].reshape(2, HALF_CH), (bsz, 1))

    tlen = min(seq, T_SSM)
    (u2, p), ((g_glu, g_out),) = _mixer_in_fwd(
        x, sh1, sc1, g_norm_mix, w_in_b, comm=[(_gather_plan, [w_glu[0].astype(BF16), w_out[0].astype(BF16)])])
    w_glu_b = g_glu.transpose(1, 0, 2).reshape(SSM_W, 2 * SSM_W)
    w_out_b = g_out.reshape(D, D)
    u2r = u2.reshape(seq * 2 * bsz, HALF_CH)
    (y2r, xc, xs_re, xs_im), ((g_up, g_down),) = _ssm_fwd(
        u2r, bb, cc, lam8, d8, tlen, comm=[(_gather_plan, [w_up[0].T.astype(BF16), w_down[0].astype(BF16)])])
    w_up_b = g_up.reshape(2 * DFF, D)
    w_down_b = g_down.reshape(DFF, D)
    y2 = y2r.reshape(seq, bsz * SSM_W)
    w_pool_b = w_pool[0].astype(BF16)
    bp = b_pool[0].reshape(1, POOL_W)
    x1, mixcat = _mixer_out_fwd(y2, p, x, gt1, w_glu_b, b_glu, w_pool_b, bp, pool_scale, w_out_b)
    h2, vq, gq, gcq, act, ddn, dx2, loss_l, dg_fin, dgt2 = _ffn_fwd(
        x1, loss_target, sh2, sc2, gt2, g_norm_ffn, w_up_b, w_conv_f, b_conv, w_down_b, g_norm_final.reshape(1, D))

    dup, dx1, dsh2, dsc2, dg_ffn, dw_conv, db_conv = _ffn_bwd(
        ddn, gq, gcq, vq, x1, dx2, sh2, sc2, g_norm_ffn, w_conv_f, w_down_b, w_up_b)
    ntok = bsz * seq
    dw_up_t = _wgrad(dup.reshape(ntok, 2 * DFF), h2.reshape(ntok, D), DFF // 2, D, "wgrad_up")
    dw_down = _wgrad(act.reshape(ntok, DFF), ddn.reshape(ntok, D), DFF, 512, "wgrad_down")
    g42_up = dw_up_t.reshape(4, 2, 704, D)
    g42_down = dw_down.reshape(4, 2, 352, D)
    (dy2, dp, dw_out, dw_glu, db_glu, dw_pool, db_pool, dpscale, dgt1), ((ra_up, ra_down),) = _mixer_out_bwd(
        dx1, mixcat, y2, p, gt1, w_glu_b, b_glu, w_pool_b, bp, pool_scale, w_out_b,
        comm=[(_pair_plan, [g42_up, g42_down])])
    own_up, s_up = _pair_sum(g42_up, ra_up, place, "pair_sum_up")
    own_down, s_down = _pair_sum(g42_down, ra_down, place, "pair_sum_down")
    g42_glu = dw_glu.reshape(SSM_W, N_DEV, 128).transpose(1, 0, 2).reshape(4, 2, SSM_W, 128)
    g42_out = dw_out.reshape(4, 2, 128, D)
    small_a = [
        ("b_glu", (1, 2 * SSM_W), db_glu), ("w_pool", (POOL_W, 128), dw_pool.reshape(POOL_W, 128)),
        ("b_pool", (4, 128), db_pool.reshape(4, 128)), ("pool_scale", (1, POOL_W), dpscale),
        ("g_norm_ffn", (1, D), dg_ffn), ("b_conv", (1, DFF), db_conv), ("g_norm_final", (1, D), dg_fin)]
    (du2r, dcc, dbb, dlam8, dd8), ((rc_up, rc_down), (ra_glu, ra_out), parts_a) = _ssm_bwd(
        dy2.reshape(u2r.shape), u2r, xc, xs_re, xs_im, bb, cc, lam8, d8, tlen,
        comm=[(_chip_plan, [s_up, s_down]), (_pair_plan, [g42_glu, g42_out]),
              (_gather_plan, [g for _, _, g in small_a] + [dw_conv, loss_l])])
    big_up = [t.T for t in _final_sum_adamw(own_up, rc_up, w_up[0].T, m_w_up[0].T, v_w_up[0].T, "final_adamw_up")]
    big_down = _final_sum_adamw(own_down, rc_down, w_down[0], m_w_down[0], v_w_down[0], "final_adamw_down")
    own_glu, s_glu = _pair_sum(g42_glu, ra_glu, place, "pair_sum_glu")
    own_out, s_out = _pair_sum(g42_out, ra_out, place, "pair_sum_out")

    def take_c(t):
        return _blockdiag_take(t, NST, GCH).transpose(0, 2, 1)

    dc_re = jnp.concatenate([take_c(dcc[0:HALF_ST, e * HALF_CH:(e + 1) * HALF_CH]) for e in range(2)], axis=0)
    dc_im = jnp.concatenate([take_c(dcc[HALF_ST:, e * HALF_CH:(e + 1) * HALF_CH]) for e in range(2)], axis=0)

    def take_b(t):
        return _blockdiag_take(t, GCH, NST).transpose(1, 0, 2)

    dbbr = jnp.concatenate([take_b(dbb[e * HALF_CH:(e + 1) * HALF_CH, 0:HALF_ST]) for e in range(2)], axis=1)
    dbbi = jnp.concatenate([take_b(dbb[e * HALF_CH:(e + 1) * HALF_CH, HALF_ST:]) for e in range(2)], axis=1)
    glr, gli, gldt, gbr, gbi, gd = _ssm_param_bwd(
        lam_r, lam_i, ldt, b_r, b_i, dlam8, dbbr.reshape(GCH, GRP * NST), dbbi.reshape(GCH, GRP * NST), dd8)
    g_log_dt = jnp.sum(gldt.reshape(GRP, NST), axis=1)

    def view(a, shp):
        return a.reshape(shp)

    small_b = [
        ("ssm_lam_re", (GRP, NST), glr.reshape(GRP, NST)), ("ssm_lam_im", (GRP, NST), gli.reshape(GRP, NST)),
        ("ssm_log_dt", (1, GRP), g_log_dt.reshape(1, GRP)),
        ("ssm_c_re", (GRP * GCH, NST), dc_re.reshape(GRP * GCH, NST)),
        ("ssm_c_im", (GRP * GCH, NST), dc_im.reshape(GRP * GCH, NST)), ("ssm_d", (1, SSM_W), gd)]
    small = small_a + small_b
    given = dict(
        ssm_lam_re=(ssm_lam_re, m_ssm_lam_re, v_ssm_lam_re), ssm_lam_im=(ssm_lam_im, m_ssm_lam_im, v_ssm_lam_im),
        ssm_log_dt=(ssm_log_dt, m_ssm_log_dt, v_ssm_log_dt), ssm_c_re=(ssm_c_re, m_ssm_c_re, v_ssm_c_re),
        ssm_c_im=(ssm_c_im, m_ssm_c_im, v_ssm_c_im), ssm_d=(ssm_d, m_ssm_d, v_ssm_d), b_glu=(b_glu, m_b_glu, v_b_glu),
        w_pool=(w_pool, m_w_pool, v_w_pool), b_pool=(b_pool, m_b_pool, v_b_pool),
        pool_scale=(pool_scale, m_pool_scale, v_pool_scale), g_norm_ffn=(g_norm_ffn, m_g_norm_ffn, v_g_norm_ffn),
        b_conv=(b_conv, m_b_conv, v_b_conv), g_norm_final=(g_norm_final, m_g_norm_final, v_g_norm_final),
        ssm_b_re=(ssm_b_re, m_ssm_b_re, v_ssm_b_re), ssm_b_im=(ssm_b_im, m_ssm_b_im, v_ssm_b_im))
    b_view = (GRP * NST, GCH)
    (grad_x, dw_in, dsh1, dsc1, dg_mix), ((rc_glu, rc_out), parts_b) = _mixer_in_bwd(
        du2r.reshape(u2.shape), dp, x, dx1, sh1, sc1, g_norm_mix, w_in_b,
        comm=[(_chip_plan, [s_glu, s_out]), (_gather_plan, [g for _, _, g in small_b] + [gbr, gbi])])
    big_glu = _final_sum_adamw(own_glu, rc_glu, w_glu[0], m_w_glu[0], v_w_glu[0], "final_adamw_glu")
    big_out = _final_sum_adamw(own_out, rc_out, w_out[0], m_w_out[0], v_w_out[0], "final_adamw_out")
    parts = list(parts_a[:-2]) + list(parts_b[:-2])
    items = [(pt,) + tuple(view(a, shp) for a in given[nm]) for pt, (nm, shp, _) in zip(parts, small)]
    small_out, (g_conv_full, loss_all, gbr_all, gbi_all) = _small_sum_adamw(
        items, [parts_a[-2], parts_a[-1], parts_b[-2], parts_b[-1]])
    loss = loss_all[0, 0]
    result = {nm: [t.reshape(given[nm][0].shape) for t in quad] for quad, (nm, _, _) in zip(small_out, small)}
    for nm, g_all in (("ssm_b_re", gbr_all), ("ssm_b_im", gbi_all)):
        quad = [g_all.T] + list(_adamw_plain(g_all.T, *[view(a, b_view) for a in given[nm]]))
        result[nm] = [t.reshape(given[nm][0].shape) for t in quad]
    g_w_conv = lax.dynamic_slice_in_dim(g_conv_full, 352 * me, 352, axis=1)
    result["w_conv"] = [g_w_conv[None]] + [t[None] for t in _adamw_plain(g_w_conv, w_conv[0], m_w_conv[0], v_w_conv[0])]

    g42_in = dw_in.reshape(4, 2, 128, D)
    (ra_in,) = _comm_call(_pair_plan([g42_in]), [g42_in], "in_grad_pair_exchange")
    own_in, s_in = _pair_sum(g42_in, ra_in, place, "pair_sum_in")
    (rc_in,) = _comm_call(_chip_plan([s_in]), [s_in], "in_grad_chip_exchange")
    big_in = _final_sum_adamw(own_in, rc_in, w_in[0], m_w_in[0], v_w_in[0], "final_adamw_in")
    for nm, quad in (("w_in", big_in), ("w_glu", big_glu), ("w_out", big_out), ("w_up", big_up), ("w_down", big_down)):
        result[nm] = [t[None] for t in quad]

    dmod = jnp.concatenate([t.reshape(bsz, D) for t in (dsh1, dsc1, dgt1, dsh2, dsc2, dgt2)], axis=1)
    dmod_blk = jnp.zeros((N_DEV, 8, ncol), F32).at[:, 0:bsz].set(dmod.reshape(bsz, N_DEV, ncol).transpose(1, 0, 2))
    dmod_blk = dmod_blk.at[0, ADA_RIDER_ROW].set(dg_mix[0, 0:ncol]).at[1, ADA_RIDER_ROW, 0:D - ncol].set(dg_mix[0, ncol:])
    ada = _ada_bwd(dmod_blk.reshape(ADA_ROWS, ncol), c_all, w_ada[0], m_w_ada[0], v_w_ada[0],
                   b_ada, m_b_ada, v_b_ada, g_norm_mix, m_g_norm_mix, v_g_norm_mix)
    result["w_ada"] = [t[None] for t in ada[0:4]]
    result["b_ada"] = list(ada[4:8])
    result["g_norm_mix"] = list(ada[8:12])

    names = ["w_ada", "b_ada", "g_norm_mix", "w_in", "ssm_lam_re", "ssm_lam_im", "ssm_log_dt", "ssm_b_re", "ssm_b_im",
             "ssm_c_re", "ssm_c_im", "ssm_d", "w_glu", "b_glu", "w_pool", "b_pool", "pool_scale", "w_out", "g_norm_ffn",
             "w_up", "w_conv", "b_conv", "w_down", "g_norm_final"]
    return (loss, grad_x, *[result[nm][k] for k in range(4) for nm in names])
```

```python
import functools
import math

import jax
import jax.numpy as jnp
from jax import lax
from jax.experimental import pallas as pl
from jax.experimental.pallas import tpu as pltpu

F32 = jnp.float32
BF16 = jnp.bfloat16

D = 1024
SSM_W = 512
POOL_W = 512
GRP = 32
GCH = 16
NST = 64
HALF_ST = GRP * NST // 2
HALF_CH = SSM_W // 2
DFF = 2816
FF_CH = 2816
N_MOD = 6
N_DEV = 8
EPS = 1e-6
POOL_WINDOWS = (2, 4, 8, 16)
POOL_HALO = 16
CONV_HALO = 8
GELU_C = math.sqrt(2.0 / math.pi)
GELU_A = 0.044715

ADAM_LR = 0.001
ADAM_B1 = 0.9
ADAM_B2 = 0.999
ADAM_EPS = 1e-08
ADAM_WD = 0.01
ADAM_STEP = 10

VMEM_LIMIT = 56 * 1024 * 1024
TT_MIX = 512
TT_FFN = 256
T_SSM = 128
TT_WGRAD = 2048
MESH = pl.DeviceIdType.MESH
NT = (((1,), (1,)), ((), ()))
TN = (((0,), (0,)), ((), ()))
ANY = pl.BlockSpec(memory_space=pl.ANY)
VMEM = pl.BlockSpec(memory_space=pltpu.VMEM)


def _params(n_grid, vmem=VMEM_LIMIT):
    return pltpu.CompilerParams(dimension_semantics=("arbitrary",) * n_grid, vmem_limit_bytes=vmem)


def _dot(a, b):
    return jnp.dot(a, b, preferred_element_type=F32)


def _dot_nt(a, b):
    return lax.dot_general(a, b, NT, preferred_element_type=F32)


def _dot_tn(a, b):
    return lax.dot_general(a, b, TN, preferred_element_type=F32)


def _colsum(a):
    return jnp.sum(a, axis=0, keepdims=True)


def _rms(x):
    rstd = lax.rsqrt(jnp.mean(x * x, axis=-1, keepdims=True) + EPS)
    return x * rstd, rstd


def _rms_bwd(dxhat, xhat, rstd):
    return rstd * (dxhat - xhat * jnp.mean(dxhat * xhat, axis=-1, keepdims=True))


def _gelu(x):
    return 0.5 * x * (1.0 + jnp.tanh(GELU_C * (x + GELU_A * x * x * x)))


def _gelu_grad(x):
    x2 = x * x
    th = jnp.tanh(GELU_C * (x + GELU_A * x * x2))
    return 0.5 * (1.0 + th) + 0.5 * x * (1.0 - th * th) * GELU_C * (1.0 + 3.0 * GELU_A * x2)


def _adamw(w, g, m, v):
    m = ADAM_B1 * m + (1.0 - ADAM_B1) * g
    v = ADAM_B2 * v + (1.0 - ADAM_B2) * (g * g)
    m_hat = m / (1.0 - ADAM_B1 ** ADAM_STEP)
    v_hat = v / (1.0 - ADAM_B2 ** ADAM_STEP)
    delta = -ADAM_LR * (m_hat / (jnp.sqrt(v_hat) + ADAM_EPS) + ADAM_WD * w)
    return delta, m, v


def _my_place():
    return lax.axis_index("x"), lax.axis_index("y"), lax.axis_index("c")


def _gather_plan(shards):
    n = len(shards)
    out_shape = [jax.ShapeDtypeStruct((N_DEV,) + tuple(s.shape), s.dtype) for s in shards]
    scratch = [pltpu.SemaphoreType.DMA((n, 7)), pltpu.SemaphoreType.DMA((n, 7)), pltpu.SemaphoreType.DMA((n,))]

    def stages(x_refs, out_refs, sems):
        send_sems, recv_sems, local_sems = sems
        x, y, c = _my_place()
        me, sibling = (x, y, c), (x, y, 1 - c)
        chips = [(1 - x, y), (x, 1 - y), (1 - x, 1 - y)]

        def copy(i, k, block, to, own=False):
            px, py, pc = block
            dst = out_refs[i].at[4 * px + 2 * py + pc]
            return pltpu.make_async_remote_copy(
                src_ref=x_refs[i] if own else dst, dst_ref=dst, send_sem=send_sems.at[i, k],
                recv_sem=recv_sems.at[i, k], device_id=to, device_id_type=MESH)

        def mine(i):
            return pltpu.make_async_copy(x_refs[i], out_refs[i].at[4 * x + 2 * y + c], local_sems.at[i])

        def start():
            for i in range(n):
                mine(i).start()
                copy(i, 0, me, sibling, own=True).start()
                for j, chip in enumerate(chips):
                    copy(i, 1 + j, me, (*chip, c), own=True).start()

        def forward():
            for i in range(n):
                for j, chip in enumerate(chips):
                    copy(i, 1 + j, (*chip, c), me).wait_recv()
                    copy(i, 4 + j, (*chip, c), sibling).start()

        def finish():
            for i in range(n):
                copy(i, 0, sibling, me).wait_recv()
                copy(i, 0, me, sibling, own=True).wait_send()
                for j, chip in enumerate(chips):
                    copy(i, 4 + j, (*chip, 1 - c), me).wait_recv()
                    copy(i, 1 + j, me, (*chip, c), own=True).wait_send()
                    copy(i, 4 + j, (*chip, c), sibling).wait_send()
                mine(i).wait()

        return [start, forward, finish]

    return n, out_shape, scratch, stages


def _pair_plan(g42s):
    n = len(g42s)
    out_shape = [jax.ShapeDtypeStruct((4,) + tuple(g.shape[2:]), g.dtype) for g in g42s]
    scratch = [pltpu.SemaphoreType.DMA((n,)), pltpu.SemaphoreType.DMA((n,))]

    def stages(g_refs, out_refs, sems):
        send_sems, recv_sems = sems
        x, y, c = _my_place()

        def copy(i):
            return pltpu.make_async_remote_copy(
                src_ref=g_refs[i].at[:, 1 - c], dst_ref=out_refs[i], send_sem=send_sems.at[i],
                recv_sem=recv_sems.at[i], device_id=(x, y, 1 - c), device_id_type=MESH)

        def start():
            for i in range(n):
                copy(i).start()

        def finish():
            for i in range(n):
                copy(i).wait()

        return [start, finish]

    return n, out_shape, scratch, stages


def _chip_plan(s4s):
    n = len(s4s)
    out_shape = [jax.ShapeDtypeStruct((3,) + tuple(s.shape[1:]), s.dtype) for s in s4s]
    scratch = [pltpu.SemaphoreType.DMA((n, 3)), pltpu.SemaphoreType.DMA((n, 3))]

    def stages(s_refs, out_refs, sems):
        send_sems, recv_sems = sems
        x, y, c = _my_place()

        def copy(i, d):
            px, py = x ^ (d >> 1), y ^ (d & 1)
            return pltpu.make_async_remote_copy(
                src_ref=s_refs[i].at[2 * px + py], dst_ref=out_refs[i].at[d - 1], send_sem=send_sems.at[i, d - 1],
                recv_sem=recv_sems.at[i, d - 1], device_id=(px, py, c), device_id_type=MESH)

        def start():
            for i in range(n):
                for d in (1, 2, 3):
                    copy(i, d).start()

        def finish():
            for i in range(n):
                for d in (1, 2, 3):
                    copy(i, d).wait()

        return [start, finish]

    return n, out_shape, scratch, stages


def _comm_call(plan, arrays, name):
    n, out_shape, scratch, stages = plan

    def body(*refs):
        for stage in stages(refs[:n], refs[n:2 * n], refs[2 * n:]):
            stage()

    return pl.pallas_call(
        body, name=name, out_shape=out_shape, in_specs=[ANY] * n, out_specs=[ANY] * n, scratch_shapes=scratch,
    )(*arrays)


def _fused_call(body, *, name, grid, in_specs, out_specs, out_shape, scratch_shapes, args, comm=None):
    if not comm:
        out = pl.pallas_call(body, name=name, grid=grid, in_specs=in_specs, out_specs=out_specs, out_shape=out_shape,
                             scratch_shapes=scratch_shapes, compiler_params=_params(len(grid)))(*args)
        return out, []
    counts = [plan[0] for plan, _, _ in comm]
    n = sum(counts)
    n_in, n_out, n_scr = len(in_specs), len(out_specs), len(scratch_shapes)

    def fused(*refs):
        ins, refs = refs[:n_in], refs[n_in:]
        c_ins, refs = refs[:n], refs[n:]
        outs, refs = refs[:n_out], refs[n_out:]
        c_outs, refs = refs[:n], refs[n:]
        scr, c_scr = refs[:n_scr], refs[n_scr:]
        step = pl.program_id(0)
        for k in range(1, len(grid)):
            step = step * grid[k] + pl.program_id(k)
        todo, a0, s0 = [], 0, 0
        for (cnt, _, plan_scratch, stages), _, steps in comm:
            sems = c_scr[s0:s0 + len(plan_scratch)]
            todo += list(zip(stages(c_ins[a0:a0 + cnt], c_outs[a0:a0 + cnt], sems), steps))
            a0 += cnt
            s0 += len(plan_scratch)
        for stage, at in todo:
            if at == 0:
                pl.when(step == 0)(stage)
        body(*ins, *outs, *scr)
        for stage, at in todo:
            if at != 0:
                pl.when(step == at)(stage)

    c_shape = [s for plan, _, _ in comm for s in plan[1]]
    c_scratch = [s for plan, _, _ in comm for s in plan[2]]
    arrays = [a for _, arrs, _ in comm for a in arrs]
    out = pl.pallas_call(
        fused, name=name, grid=grid, in_specs=list(in_specs) + [ANY] * n, out_specs=list(out_specs) + [ANY] * n,
        out_shape=list(out_shape) + c_shape, scratch_shapes=list(scratch_shapes) + c_scratch,
        compiler_params=_params(len(grid)))(*args, *arrays)
    outs, c_outs, split, a0 = out[:n_out], out[n_out:], [], 0
    for cnt in counts:
        split.append(c_outs[a0:a0 + cnt])
        a0 += cnt
    return outs, split


def _schedule(comm, n_steps):
    out = []
    for make_plan, arrays in comm or []:
        steps = [0, (3 * n_steps) // 4, n_steps - 1] if make_plan is _gather_plan else [0, n_steps - 1]
        out.append((make_plan(arrays), arrays, steps))
    return out


def _row_tile(r):
    for t in (128, 64, 32, 16, 8):
        if r % t == 0:
            return t
    return r


def _pair_sum(g42, recv, place, name):
    _, _, r, cdim = g42.shape
    tr = _row_tile(r)

    def body(pl_ref, g_ref, r_ref, own_ref, s_ref):
        s_ref[...] = (g_ref[:, 0] + r_ref[...]).astype(BF16)
        q = pl_ref[1]
        own_ref[...] = g_ref[q, 0] + r_ref[q]

    return pl.pallas_call(
        body, name=name,
        grid_spec=pltpu.PrefetchScalarGridSpec(
            num_scalar_prefetch=1, grid=(r // tr,),
            in_specs=[pl.BlockSpec((4, 1, tr, cdim), lambda i, p: (0, p[0], i, 0)),
                      pl.BlockSpec((4, tr, cdim), lambda i, p: (0, i, 0))],
            out_specs=[pl.BlockSpec((tr, cdim), lambda i, p: (i, 0)),
                       pl.BlockSpec((4, tr, cdim), lambda i, p: (0, i, 0))]),
        out_shape=[jax.ShapeDtypeStruct((r, cdim), F32), jax.ShapeDtypeStruct((4, r, cdim), BF16)],
        compiler_params=_params(1),
    )(place, g42, recv)


def _final_sum_adamw(own, recv3, w, m, v, name):
    r, cdim = w.shape
    tr = _row_tile(r)

    def body(s_ref, r_ref, w_ref, m_ref, v_ref, g_out, d_out, m_out, v_out):
        g = s_ref[...] + r_ref[0].astype(F32) + r_ref[1].astype(F32) + r_ref[2].astype(F32)
        d, mn, vn = _adamw(w_ref[...], g, m_ref[...], v_ref[...])
        g_out[...] = g
        d_out[...] = d
        m_out[...] = mn
        v_out[...] = vn

    blk = pl.BlockSpec((tr, cdim), lambda i: (i, 0))
    shp = jax.ShapeDtypeStruct((r, cdim), F32)
    return pl.pallas_call(
        body, name=name, grid=(r // tr,),
        in_specs=[blk, pl.BlockSpec((3, tr, cdim), lambda i: (0, i, 0)), blk, blk, blk],
        out_specs=[blk, blk, blk, blk], out_shape=[shp, shp, shp, shp], compiler_params=_params(1),
    )(own, recv3, w, m, v)


def _small_sum_adamw(items, sums_only):
    n, ne = len(items), len(sums_only)

    def total(p_ref):
        g = p_ref[0]
        for k in range(1, N_DEV):
            g = g + p_ref[k]
        return g

    def body(*refs):
        ins, outs = refs[:4 * n + ne], refs[4 * n + ne:]
        for i in range(n):
            p_ref, w_ref, m_ref, v_ref = ins[4 * i:4 * i + 4]
            g = total(p_ref)
            d, mn, vn = _adamw(w_ref[...], g, m_ref[...], v_ref[...])
            for o_ref, val in zip(outs[4 * i:4 * i + 4], (g, d, mn, vn)):
                o_ref[...] = val
        for j in range(ne):
            outs[4 * n + j][...] = total(ins[4 * n + j])

    args = [a for item in items for a in item] + list(sums_only)
    shapes = [jax.ShapeDtypeStruct(w.shape, F32) for _, w, _, _ in items for _ in range(4)]
    shapes += [jax.ShapeDtypeStruct(p.shape[1:], F32) for p in sums_only]
    out = pl.pallas_call(
        body, name="small_sum_adamw", in_specs=[VMEM] * len(args), out_specs=[VMEM] * len(shapes), out_shape=shapes,
        compiler_params=_params(0),
    )(*args)
    return [out[4 * i:4 * i + 4] for i in range(n)], out[4 * n:]


def _adamw_plain(g, w, m, v):
    r, cdim = w.shape
    tr = r if r * cdim <= 64 * 1024 else _row_tile(r)

    def body(g_ref, w_ref, m_ref, v_ref, d_out, m_out, v_out):
        d, mn, vn = _adamw(w_ref[...], g_ref[...], m_ref[...], v_ref[...])
        d_out[...] = d
        m_out[...] = mn
        v_out[...] = vn

    blk = pl.BlockSpec((tr, cdim), lambda i: (i, 0))
    shp = jax.ShapeDtypeStruct((r, cdim), F32)
    return pl.pallas_call(
        body, name="adamw_plain", grid=(r // tr,), in_specs=[blk, blk, blk, blk],
        out_specs=[blk, blk, blk], out_shape=[shp, shp, shp], compiler_params=_params(1),
    )(g, w, m, v)


ADA_COLS = N_MOD * D // N_DEV
ADA_ROWS = 8 * N_DEV


def _ada_fwd(cpad, w_ada, b_blocks, mixer_shards):
    n_w, w_shape, w_scr, w_stages = _gather_plan(mixer_shards)
    _, _, c_scr, c_stages = _gather_plan([cpad])
    _, _, p_scr, p_stages = _gather_plan([jax.ShapeDtypeStruct((ADA_ROWS, ADA_COLS), F32)])

    def body(c_ref, wa_ref, b_ref, *refs):
        w_refs, refs = refs[:n_w], refs[n_w:]
        cg_ref, call_ref, mod_ref = refs[:3]
        wg_refs, refs = refs[3:3 + n_w], refs[3 + n_w:]
        part_ref, pg_ref = refs[:2]
        c_sems, p_sems, w_sems = refs[2:5], refs[5:8], refs[8:11]
        w_start, w_forward, w_finish = w_stages(w_refs, wg_refs, w_sems)
        w_start()
        for stage in c_stages([c_ref], [cg_ref], c_sems):
            stage()
        cv = cg_ref[:, 0:8, :].reshape(ADA_ROWS, D)
        call_ref[...] = cv
        part_ref[...] = _dot(cv * jax.nn.sigmoid(cv), wa_ref[...])
        for stage in p_stages([part_ref], [pg_ref], p_sems):
            stage()
        x, y, c = _my_place()
        r0 = pl.multiple_of(8 * (4 * x + 2 * y + c), 8)
        for k in range(N_DEV):
            mod_ref[:, k * ADA_COLS:(k + 1) * ADA_COLS] = pg_ref[k, pl.ds(r0, 8), :] + b_ref[k]
        w_forward()
        w_finish()

    out = pl.pallas_call(
        body, name="ada_fwd", in_specs=[VMEM, VMEM, VMEM] + [ANY] * n_w,
        out_specs=[VMEM, VMEM, VMEM] + [ANY] * n_w,
        out_shape=[jax.ShapeDtypeStruct((N_DEV,) + cpad.shape, F32), jax.ShapeDtypeStruct((ADA_ROWS, D), F32),
                   jax.ShapeDtypeStruct((8, N_MOD * D), F32)] + list(w_shape),
        scratch_shapes=[pltpu.VMEM((ADA_ROWS, ADA_COLS), F32), pltpu.VMEM((N_DEV, ADA_ROWS, ADA_COLS), F32)]
        + list(c_scr) + list(p_scr) + list(w_scr),
        compiler_params=_params(0),
    )(cpad, w_ada, b_blocks, *mixer_shards)
    return out[0], out[1], out[2], out[3:]


ADA_RIDER_ROW = 4


def _ada_bwd(dmod_blk, c_all, w_ada, m_w, v_w, b_blocks, m_b, v_b, g_w, g_m, g_v, g42, w_l, m_l, v_l):
    _, _, g_scr, g_stages = _gather_plan([dmod_blk])
    rest = D - ADA_COLS
    blk = tuple(g42.shape[2:])

    def body(dm_ref, c_ref, w_ref, mw_ref, vw_ref, b_ref, mb_ref, vb_ref, gw_ref, gm_ref, gv_ref,
             g42_ref, wl_ref, ml_ref, vl_ref,
             gw_o, dw_o, mw_o, vw_o, gb_o, dbb_o, mb_o, vb_o, gg_o, dgg_o, mg_o, vg_o, gl_o, dl_o, ml_o, vl_o,
             dg_ref, pr_ref, sbf_ref, rc_ref, pair_send, pair_recv, chip_send, chip_recv, *sems):
        x, y, c = _my_place()
        q = 2 * x + y
        pair = pltpu.make_async_remote_copy(
            src_ref=g42_ref.at[:, 1 - c], dst_ref=pr_ref, send_sem=pair_send, recv_sem=pair_recv,
            device_id=(x, y, 1 - c), device_id_type=MESH)
        pair.start()
        for stage in g_stages([dm_ref], [dg_ref], sems):
            stage()
        r0 = pl.multiple_of(8 * (4 * x + 2 * y + c), 8)
        cols = dg_ref[:, pl.ds(r0, 8), :].reshape(ADA_ROWS, ADA_COLS)
        cv = c_ref[...]
        gw = _dot_tn(cv * jax.nn.sigmoid(cv), cols)
        d, mn, vn = _adamw(w_ref[...], gw, mw_ref[...], vw_ref[...])
        gw_o[...] = gw
        dw_o[...] = d
        mw_o[...] = mn
        vw_o[...] = vn
        is_example = lax.broadcasted_iota(jnp.int32, (8, 1), 0) < ADA_RIDER_ROW
        blocks = []
        for k in range(N_DEV):
            s = dg_ref[0, 8 * k:8 * k + 8, :]
            for dev in range(1, N_DEV):
                s = s + dg_ref[dev, 8 * k:8 * k + 8, :]
            blocks.append(s)
            gb = _colsum(jnp.where(is_example, s, 0.0))
            cs = slice(k * ADA_COLS, (k + 1) * ADA_COLS)
            d, mn, vn = _adamw(b_ref[:, cs], gb, mb_ref[:, cs], vb_ref[:, cs])
            gb_o[:, cs] = gb
            dbb_o[:, cs] = d
            mb_o[:, cs] = mn
            vb_o[:, cs] = vn
        rider = jnp.concatenate([blocks[0][ADA_RIDER_ROW:ADA_RIDER_ROW + 1, :],
                                 blocks[1][ADA_RIDER_ROW:ADA_RIDER_ROW + 1, 0:rest]], axis=1)
        d, mn, vn = _adamw(gw_ref[...], rider, gm_ref[...], gv_ref[...])
        gg_o[...] = rider
        dgg_o[...] = d
        mg_o[...] = mn
        vg_o[...] = vn
        pair.wait()
        for k in range(4):
            sbf_ref[k] = (g42_ref[k, c] + pr_ref[k]).astype(BF16)

        def chip_copy(dist):
            px, py = x ^ (dist >> 1), y ^ (dist & 1)
            return pltpu.make_async_remote_copy(
                src_ref=sbf_ref.at[2 * px + py], dst_ref=rc_ref.at[dist - 1], send_sem=chip_send.at[dist - 1],
                recv_sem=chip_recv.at[dist - 1], device_id=(px, py, c), device_id_type=MESH)

        for dist in (1, 2, 3):
            chip_copy(dist).start()
        for dist in (1, 2, 3):
            chip_copy(dist).wait()
        gl = g42_ref[q, c] + pr_ref[q] + rc_ref[0].astype(F32) + rc_ref[1].astype(F32) + rc_ref[2].astype(F32)
        d, mn, vn = _adamw(wl_ref[...], gl, ml_ref[...], vl_ref[...])
        gl_o[...] = gl
        dl_o[...] = d
        ml_o[...] = mn
        vl_o[...] = vn

    ws = jax.ShapeDtypeStruct(w_ada.shape, F32)
    bs = jax.ShapeDtypeStruct(b_blocks.shape, F32)
    gs = jax.ShapeDtypeStruct(g_w.shape, F32)
    ls = jax.ShapeDtypeStruct(w_l.shape, F32)
    return pl.pallas_call(
        body, name="ada_bwd", in_specs=[VMEM] * 15, out_specs=[VMEM] * 16,
        out_shape=[ws, ws, ws, ws, bs, bs, bs, bs, gs, gs, gs, gs, ls, ls, ls, ls],
        scratch_shapes=[pltpu.VMEM((N_DEV, ADA_ROWS, ADA_COLS), F32), pltpu.VMEM((4,) + blk, F32),
                        pltpu.VMEM((4,) + blk, BF16), pltpu.VMEM((3,) + blk, BF16),
                        pltpu.SemaphoreType.DMA(()), pltpu.SemaphoreType.DMA(()),
                        pltpu.SemaphoreType.DMA((3,)), pltpu.SemaphoreType.DMA((3,))] + list(g_scr),
        compiler_params=_params(0),
    )(dmod_blk, c_all, w_ada, m_w, v_w, b_blocks, m_b, v_b, g_w, g_m, g_v, g42, w_l, m_l, v_l)


def _ssm_param_fn(lr, li, ldt, br, bi):
    dt = jnp.exp(ldt)
    mag = jnp.exp(lr * dt)
    ang = li * dt
    lbr = mag * jnp.cos(ang)
    lbi = mag * jnp.sin(ang)
    nr = lbr - 1.0
    den = lr * lr + li * li
    cr = (nr * lr + lbi * li) / den
    ci = (lbi * lr - nr * li) / den
    return lbr, lbi, cr * br - ci * bi, cr * bi + ci * br


def _ssm_prep(lr, li, ldt, br, bi):
    def body(lr_ref, li_ref, ldt_ref, br_ref, bi_ref, lbr_o, lbi_o, bbr_o, bbi_o):
        lbr, lbi, bbr, bbi = _ssm_param_fn(lr_ref[...], li_ref[...], ldt_ref[...], br_ref[...], bi_ref[...])
        lbr_o[...] = lbr
        lbi_o[...] = lbi
        bbr_o[...] = bbr
        bbi_o[...] = bbi

    row = jax.ShapeDtypeStruct(lr.shape, F32)
    mat = jax.ShapeDtypeStruct(br.shape, F32)
    return pl.pallas_call(
        body, name="ssm_prep", in_specs=[VMEM] * 5, out_specs=[VMEM] * 4,
        out_shape=[row, row, mat, mat], compiler_params=_params(0),
    )(lr, li, ldt, br, bi)


def _ssm_param_bwd(lr, li, ldt, br, bi, dlam8, dbbr, dbbi, dd8):
    nv = dlam8.shape[1]

    def body(lr_ref, li_ref, ldt_ref, br_ref, bi_ref, dl_ref, dbr_ref, dbi_ref, dd_ref,
             glr_o, gli_o, gldt_o, gbr_o, gbi_o, gd_o):
        halves_r, halves_i, halves_d = [], [], []
        for e in range(2):
            ar = dl_ref[0, e:e + 1, :]
            ai = dl_ref[1, e:e + 1, :]
            ad = dd_ref[e:e + 1, :]
            for b in range(1, nv // 2):
                ar = ar + dl_ref[0, 2 * b + e:2 * b + e + 1, :]
                ai = ai + dl_ref[1, 2 * b + e:2 * b + e + 1, :]
                ad = ad + dd_ref[2 * b + e:2 * b + e + 1, :]
            halves_r.append(ar)
            halves_i.append(ai)
            halves_d.append(ad)
        dlbr = jnp.concatenate(halves_r, axis=1)
        dlbi = jnp.concatenate(halves_i, axis=1)
        gd_o[...] = jnp.concatenate(halves_d, axis=1)
        _, vjp = jax.vjp(_ssm_param_fn, lr_ref[...], li_ref[...], ldt_ref[...], br_ref[...], bi_ref[...])
        glr, gli, gldt, gbr, gbi = vjp((dlbr, dlbi, dbr_ref[...], dbi_ref[...]))
        glr_o[...] = glr
        gli_o[...] = gli
        gldt_o[...] = gldt
        gbr_o[...] = gbr
        gbi_o[...] = gbi

    row = jax.ShapeDtypeStruct(lr.shape, F32)
    mat = jax.ShapeDtypeStruct(br.shape, F32)
    return pl.pallas_call(
        body, name="ssm_param_bwd", in_specs=[VMEM] * 9, out_specs=[VMEM] * 6,
        out_shape=[row, row, row, mat, mat, jax.ShapeDtypeStruct((1, SSM_W), F32)],
        compiler_params=_params(0),
    )(lr, li, ldt, br, bi, dlam8, dbbr, dbbi, dd8)


def _blockdiag(m):
    _, g, a, b = m.shape
    eye = jnp.eye(g, dtype=m.dtype)
    return jnp.einsum("egab,gk->egakb", m, eye).reshape(2, g * a, g * b)


def _blockdiag_take(t, a, b):
    return jnp.einsum("gagb->gab", t.reshape(GRP // 2, a, GRP // 2, b))


def _mixer_in_fwd(x, sh1, sc1, g_mix, w_in_b, comm=None):
    bsz, seq, _ = x.shape
    tt = min(seq, TT_MIX)

    def body(x_ref, sh_ref, sc_ref, g_ref, w_ref, u_ref, p_ref):
        xhat, _ = _rms(x_ref[0])
        h = xhat * g_ref[...] * (1.0 + sc_ref[0]) + sh_ref[0]
        z = _dot(h.astype(BF16), w_ref[...])
        u_ref[...] = z[:, :SSM_W].astype(BF16)
        p_ref[0] = z[:, SSM_W:]

    row = pl.BlockSpec((1, 1, D), lambda b, t: (b, 0, 0))
    return _fused_call(
        body, name="mixer_in_fwd", grid=(bsz, seq // tt),
        in_specs=[pl.BlockSpec((1, tt, D), lambda b, t: (b, t, 0)), row, row,
                  pl.BlockSpec((1, D), lambda b, t: (0, 0)), VMEM],
        out_specs=[pl.BlockSpec((tt, SSM_W), lambda b, t: (t, b)),
                   pl.BlockSpec((1, tt, POOL_W), lambda b, t: (b, t, 0))],
        out_shape=[jax.ShapeDtypeStruct((seq, bsz * SSM_W), BF16), jax.ShapeDtypeStruct((bsz, seq, POOL_W), F32)],
        scratch_shapes=[], args=(x, sh1, sc1, g_mix, w_in_b), comm=_schedule(comm, bsz * (seq // tt)))


def _ssm_project_in(ub, par0, bb_ref, s_re, s_im, row0, tlen, nv):
    for part, sref in ((0, s_re), (1, s_im)):
        for k in range(HALF_ST // 512):
            c0 = part * HALF_ST + k * 512
            a0 = _dot(ub, bb_ref[:, c0:c0 + 512])
            a1 = _dot(ub, bb_ref[:, 2 * HALF_ST + c0:2 * HALF_ST + c0 + 512])
            sref[pl.ds(row0, tlen), :, k * 512:(k + 1) * 512] = jnp.where(par0, a0, a1).reshape(tlen, nv, 512)


def _ssm_fwd(u2r, bb, cc, lam8, d8, tlen, comm=None):
    nv = lam8.shape[1]
    rows = nv * tlen
    n_chunks = u2r.shape[0] // rows

    def body(u_ref, bb_ref, cc_ref, lam_ref, d_ref, y_ref, xc_ref, xre_ref, xim_ref, s_re, s_im, st):
        @pl.when(pl.program_id(0) == 0)
        def _():
            st[...] = jnp.zeros_like(st)

        xc_ref[0] = st[...]
        ub = u_ref[...]
        u = ub.astype(F32)
        par0 = (lax.broadcasted_iota(jnp.int32, (rows, 1), 0) % 2) == 0
        _ssm_project_in(ub, par0, bb_ref, s_re, s_im, 0, tlen, nv)
        for hb in range(HALF_ST // 512):
            ls = slice(hb * 512, (hb + 1) * 512)
            lr = lam_ref[0, :, ls]
            li = lam_ref[1, :, ls]

            def step(t, carry, ls=ls, lr=lr, li=li):
                xr, xi = carry
                nr = lr * xr - li * xi + s_re[t, :, ls]
                ni = lr * xi + li * xr + s_im[t, :, ls]
                s_re[t, :, ls] = nr
                s_im[t, :, ls] = ni
                return nr, ni

            xr, xi = lax.fori_loop(0, tlen, step, (st[0, :, ls], st[1, :, ls]), unroll=8)
            st[0, :, ls] = xr
            st[1, :, ls] = xi
        xre = s_re[...].reshape(rows, HALF_ST).astype(BF16)
        xim = s_im[...].reshape(rows, HALF_ST).astype(BF16)
        xre_ref[...] = xre
        xim_ref[...] = xim
        y2 = _dot(xre, cc_ref[0:HALF_ST, :]) + _dot(xim, cc_ref[HALF_ST:, :])
        y = jnp.where(par0, y2[:, :HALF_CH], y2[:, HALF_CH:])
        skip = (u.reshape(tlen, nv, HALF_CH) * d_ref[...][None]).reshape(rows, HALF_CH)
        y_ref[...] = (y + skip).astype(BF16)

    st_blk = pl.BlockSpec((rows, HALF_ST), lambda c: (c, 0))
    st_shape = jax.ShapeDtypeStruct((u2r.shape[0], HALF_ST), BF16)
    return _fused_call(
        body, name="ssm_fwd", grid=(n_chunks,),
        in_specs=[pl.BlockSpec((rows, HALF_CH), lambda c: (c, 0)), VMEM, VMEM, VMEM, VMEM],
        out_specs=[pl.BlockSpec((rows, HALF_CH), lambda c: (c, 0)),
                   pl.BlockSpec((1, 2, nv, HALF_ST), lambda c: (c, 0, 0, 0)), st_blk, st_blk],
        out_shape=[jax.ShapeDtypeStruct(u2r.shape, BF16), jax.ShapeDtypeStruct((n_chunks, 2, nv, HALF_ST), F32),
                   st_shape, st_shape],
        scratch_shapes=[pltpu.VMEM((tlen, nv, HALF_ST), F32), pltpu.VMEM((tlen, nv, HALF_ST), F32),
                        pltpu.VMEM((2, nv, HALF_ST), F32)],
        args=(u2r, bb, cc, lam8, d8), comm=_schedule(comm, n_chunks))


def _pool_forward(ext, pv, pos, wp_ref, bp_ref):
    cur = ext
    zs, zls = [], []
    for gi, w in enumerate(POOL_WINDOWS):
        cur = cur + pltpu.roll(cur, w // 2, 0)
        sw = cur[POOL_HALO:, 0:128]
        z = sw / jnp.minimum(pos, float(w)) - pv[:, gi * 128:(gi + 1) * 128]
        zs.append(z)
        zls.append(_dot(z.astype(BF16), wp_ref[gi]) + bp_ref[:, gi * 128:(gi + 1) * 128])
        if gi + 1 < len(POOL_WINDOWS):
            cur = cur[:, 128:]
    return zs, zls


def _mixer_out_fwd(y2, p, x, gt1, w_glu_b, b_glu, w_pool_b, b_pool, pscale, w_out_b):
    bsz, seq, _ = x.shape
    tt = min(seq, TT_MIX)

    def body(y_ref, p_ref, x_ref, gt_ref, wg_ref, bg_ref, wp_ref, bp_ref, ps_ref, wo_ref, x1_ref, mix_ref, mxd_ref,
             ext):
        ti = pl.program_id(1)

        @pl.when(ti == 0)
        def _():
            ext[0:POOL_HALO, :] = jnp.zeros((POOL_HALO, POOL_W), F32)

        pv = p_ref[0]
        ext[POOL_HALO:, :] = pv
        pos = (ti * tt + lax.broadcasted_iota(jnp.int32, (tt, 1), 0) + 1).astype(F32)
        _, zls = _pool_forward(ext[...], pv, pos, wp_ref, bp_ref)
        ext[0:POOL_HALO, :] = pv[tt - POOL_HALO:, :]
        a = _gelu(y_ref[...].astype(F32))
        gl = _dot(a.astype(BF16), wg_ref[...]) + bg_ref[...]
        y_ssm = gl[:, :SSM_W] * jax.nn.sigmoid(gl[:, SSM_W:])
        y_pool = [zl * ps_ref[:, gi * 128:(gi + 1) * 128] for gi, zl in enumerate(zls)]
        mixcat = jnp.concatenate([y_ssm] + y_pool, axis=1).astype(BF16)
        mix_ref[0] = mixcat
        mixed = _dot(mixcat, wo_ref[...])
        mxd_ref[0] = mixed.astype(BF16)
        x1_ref[0] = x_ref[0] + gt_ref[0] * mixed

    xt = pl.BlockSpec((1, tt, D), lambda b, t: (b, t, 0))
    return pl.pallas_call(
        body, name="mixer_out_fwd", grid=(bsz, seq // tt),
        in_specs=[pl.BlockSpec((tt, SSM_W), lambda b, t: (t, b)),
                  pl.BlockSpec((1, tt, POOL_W), lambda b, t: (b, t, 0)), xt,
                  pl.BlockSpec((1, 1, D), lambda b, t: (b, 0, 0)), VMEM, VMEM, VMEM, VMEM, VMEM, VMEM],
        out_specs=[xt, xt, xt],
        out_shape=[jax.ShapeDtypeStruct(x.shape, F32), jax.ShapeDtypeStruct(x.shape, BF16),
                   jax.ShapeDtypeStruct(x.shape, BF16)],
        scratch_shapes=[pltpu.VMEM((POOL_HALO + tt, POOL_W), F32)],
        compiler_params=_params(2),
    )(y2, p, x, gt1, w_glu_b, b_glu, w_pool_b, b_pool, pscale, w_out_b)


def _conv_gate(g, ge, wc, bc):
    g1 = pltpu.roll(ge, 1, 0)[CONV_HALO:]
    g2 = pltpu.roll(ge, 2, 0)[CONV_HALO:]
    return wc[2:3] * g + wc[1:2] * g1 + wc[0:1] * g2 + bc, g1, g2


def _ffn_fwd(x1, tgt, sh2, sc2, gt2, g_ffn, w_up_b, w_conv, b_conv, w_down_b, g_fin):
    bsz, seq, _ = x1.shape
    tt = min(seq, TT_FFN)
    n_t = seq // tt
    n_ck = DFF // FF_CH

    def body(x1_ref, tg_ref, sh_ref, sc_ref, gt_ref, gf_ref, wu_ref, wc_ref, bc_ref, wd_ref, gfin_ref,
             h2_ref, v_ref, g_ref, gc_ref, act_ref, ddn_ref, dx2_ref, loss_ref, dgfin_ref, dgt_ref, gext, lacc):
        b = pl.program_id(0)
        ti = pl.program_id(1)

        @pl.when((b == 0) & (ti == 0))
        def _():
            lacc[...] = jnp.zeros_like(lacc)
            dgfin_ref[...] = jnp.zeros_like(dgfin_ref)

        @pl.when(ti == 0)
        def _():
            dgt_ref[...] = jnp.zeros_like(dgt_ref)
            gext[:, 0:CONV_HALO, :] = jnp.zeros((n_ck, CONV_HALO, FF_CH), F32)

        x1v = x1_ref[0]
        xhat, _ = _rms(x1v)
        h2b = (xhat * gf_ref[...] * (1.0 + sc_ref[0]) + sh_ref[0]).astype(BF16)
        h2_ref[0] = h2b
        dn = jnp.zeros((tt, D), F32)
        for ck in range(n_ck):
            c0 = ck * FF_CH
            v = _dot_nt(h2b, wu_ref[c0:c0 + FF_CH, :])
            g = _dot_nt(h2b, wu_ref[DFF + c0:DFF + c0 + FF_CH, :])
            v_ref[0, :, c0:c0 + FF_CH] = v.astype(BF16)
            g_ref[0, :, c0:c0 + FF_CH] = g.astype(BF16)
            gext[ck, CONV_HALO:, :] = g
            gc, _, _ = _conv_gate(g, gext[ck], wc_ref[:, c0:c0 + FF_CH], bc_ref[:, c0:c0 + FF_CH])
            gext[ck, 0:CONV_HALO, :] = g[tt - CONV_HALO:, :]
            gc_ref[0, :, c0:c0 + FF_CH] = gc.astype(BF16)
            actb = (gc * jax.nn.sigmoid(gc) * v).astype(BF16)
            act_ref[0, :, c0:c0 + FF_CH] = actb
            dn = dn + _dot(actb, wd_ref[c0:c0 + FF_CH, :])
        gt = gt_ref[0]
        xh3, r3 = _rms(x1v + gt * dn)
        gfin = gfin_ref[...]
        diff = xh3 * gfin - tg_ref[0]
        lacc[...] += _colsum(diff * diff)
        dy = diff * (1.0 / D)
        dgfin_ref[...] += _colsum(dy * xh3)
        dx2 = _rms_bwd(dy * gfin, xh3, r3)
        dx2_ref[0] = dx2
        dgt_ref[0] += _colsum(dx2 * dn)
        ddn_ref[0] = (gt * dx2).astype(BF16)

        @pl.when((b == bsz - 1) & (ti == n_t - 1))
        def _():
            loss_ref[...] = jnp.full(loss_ref.shape, 0.5 / D * jnp.sum(lacc[...]), F32)

    xt = pl.BlockSpec((1, tt, D), lambda b, t: (b, t, 0))
    ft = pl.BlockSpec((1, tt, DFF), lambda b, t: (b, t, 0))
    row = pl.BlockSpec((1, 1, D), lambda b, t: (b, 0, 0))
    vec = pl.BlockSpec((1, D), lambda b, t: (0, 0))
    ff = jax.ShapeDtypeStruct((bsz, seq, DFF), BF16)
    xs = jax.ShapeDtypeStruct((bsz, seq, D), BF16)
    return pl.pallas_call(
        body, name="ffn_fwd", grid=(bsz, n_t),
        in_specs=[xt, xt, row, row, row, vec, VMEM, VMEM, VMEM, VMEM, vec],
        out_specs=[xt, ft, ft, ft, ft, xt, xt, pl.BlockSpec((1, 128), lambda b, t: (0, 0)), vec, row],
        out_shape=[xs, ff, ff, ff, ff, xs, jax.ShapeDtypeStruct((bsz, seq, D), F32),
                   jax.ShapeDtypeStruct((1, 128), F32), jax.ShapeDtypeStruct((1, D), F32),
                   jax.ShapeDtypeStruct((bsz, 1, D), F32)],
        scratch_shapes=[pltpu.VMEM((n_ck, CONV_HALO + tt, FF_CH), F32), pltpu.VMEM((1, D), F32)],
        compiler_params=_params(2),
    )(x1, tgt, sh2, sc2, gt2, g_ffn, w_up_b, w_conv, b_conv, w_down_b, g_fin)


def _ffn_bwd(ddn, gq, gcq, vq, x1, dx2, sh2, sc2, g_ffn, w_conv, w_down_b, w_up_b):
    bsz, seq, _ = x1.shape
    tt = min(seq, TT_FFN)
    n_t = seq // tt
    n_ck = DFF // FF_CH
    ext_rows = tt + CONV_HALO

    def body(ddn_ref, g_ref, gc_ref, v_ref, x1_ref, dx2_ref, sh_ref, sc_ref, gf_ref, wc_ref, wd_ref,
             wu_ref, dup_ref, dx1_ref, dsh_ref, dsc_ref, dgf_ref, dwc_ref, dbc_ref, dext):
        b = pl.program_id(0)
        i = pl.program_id(1)

        @pl.when((b == 0) & (i == 0))
        def _():
            dgf_ref[...] = jnp.zeros_like(dgf_ref)
            dwc_ref[...] = jnp.zeros_like(dwc_ref)
            dbc_ref[...] = jnp.zeros_like(dbc_ref)

        @pl.when(i == 0)
        def _():
            dsh_ref[...] = jnp.zeros_like(dsh_ref)
            dsc_ref[...] = jnp.zeros_like(dsc_ref)
            dext[:, tt:, :] = jnp.zeros((n_ck, CONV_HALO, FF_CH), F32)

        ddnv = ddn_ref[0]
        dh2 = jnp.zeros((tt, D), F32)
        for ck in range(n_ck):
            c0 = ck * FF_CH
            dact = _dot_nt(ddnv, wd_ref[c0:c0 + FF_CH, :])
            g = g_ref[0, :, c0:c0 + FF_CH].astype(F32)
            gc = gc_ref[0, :, c0:c0 + FF_CH].astype(F32)
            v = v_ref[0, :, c0:c0 + FF_CH].astype(F32)
            wc = wc_ref[:, c0:c0 + FF_CH]
            sg = jax.nn.sigmoid(gc)
            silu = gc * sg
            dv = dact * silu
            dgc = dact * v * (sg + silu * (1.0 - sg))
            dext[ck, 0:tt, :] = dgc
            de = dext[ck]
            d1 = pltpu.roll(de, ext_rows - 1, 0)[0:tt]
            d2 = pltpu.roll(de, ext_rows - 2, 0)[0:tt]
            dext[ck, tt:, :] = dgc[0:CONV_HALO, :]
            dbc_ref[:, c0:c0 + FF_CH] += _colsum(dgc)
            dwc_ref[0:1, c0:c0 + FF_CH] += _colsum(d2 * g)
            dwc_ref[1:2, c0:c0 + FF_CH] += _colsum(d1 * g)
            dwc_ref[2:3, c0:c0 + FF_CH] += _colsum(dgc * g)
            dg = wc[2:3] * dgc + wc[1:2] * d1 + wc[0:1] * d2
            dvb = dv.astype(BF16)
            dgb = dg.astype(BF16)
            dup_ref[0, :, c0:c0 + FF_CH] = dvb
            dup_ref[0, :, DFF + c0:DFF + c0 + FF_CH] = dgb
            dh2 = dh2 + _dot(dvb, wu_ref[c0:c0 + FF_CH, :]) + _dot(dgb, wu_ref[DFF + c0:DFF + c0 + FF_CH, :])
        xhat, rstd = _rms(x1_ref[0])
        gf = gf_ref[...]
        dsh_ref[0] += _colsum(dh2)
        dsc_ref[0] += _colsum(dh2 * xhat * gf)
        t = dh2 * (1.0 + sc_ref[0])
        dgf_ref[...] += _colsum(t * xhat)
        dx1_ref[0] = dx2_ref[0] + _rms_bwd(t * gf, xhat, rstd)

    def rev(b, t):
        return (b, n_t - 1 - t, 0)

    xt = pl.BlockSpec((1, tt, D), rev)
    ft = pl.BlockSpec((1, tt, DFF), rev)
    row = pl.BlockSpec((1, 1, D), lambda b, t: (b, 0, 0))
    vec = pl.BlockSpec((1, D), lambda b, t: (0, 0))
    rows = jax.ShapeDtypeStruct((bsz, 1, D), F32)
    return pl.pallas_call(
        body, name="ffn_bwd", grid=(bsz, n_t),
        in_specs=[xt, ft, ft, ft, xt, xt, row, row, vec, VMEM, VMEM, VMEM],
        out_specs=[pl.BlockSpec((1, tt, 2 * DFF), rev), xt, row, row, vec,
                   pl.BlockSpec((3, DFF), lambda b, t: (0, 0)), pl.BlockSpec((1, DFF), lambda b, t: (0, 0))],
        out_shape=[jax.ShapeDtypeStruct((bsz, seq, 2 * DFF), BF16), jax.ShapeDtypeStruct((bsz, seq, D), F32),
                   rows, rows, jax.ShapeDtypeStruct((1, D), F32), jax.ShapeDtypeStruct((3, DFF), F32),
                   jax.ShapeDtypeStruct((1, DFF), F32)],
        scratch_shapes=[pltpu.VMEM((n_ck, ext_rows, FF_CH), F32)],
        compiler_params=_params(2),
    )(ddn, gq, gcq, vq, x1, dx2, sh2, sc2, g_ffn, w_conv, w_down_b, w_up_b)


def _wgrad(a, b, bk1, bk2, name):
    n, k1 = a.shape
    _, k2 = b.shape
    tt = min(n, TT_WGRAD)

    def body(a_ref, b_ref, o_ref):
        @pl.when(pl.program_id(2) == 0)
        def _():
            o_ref[...] = jnp.zeros_like(o_ref)

        o_ref[...] += _dot_tn(a_ref[...], b_ref[...])

    return pl.pallas_call(
        body, name=name, grid=(k1 // bk1, k2 // bk2, n // tt),
        in_specs=[pl.BlockSpec((tt, bk1), lambda h, j, i: (i, h)), pl.BlockSpec((tt, bk2), lambda h, j, i: (i, j))],
        out_specs=pl.BlockSpec((bk1, bk2), lambda h, j, i: (h, j)),
        out_shape=jax.ShapeDtypeStruct((k1, k2), F32), compiler_params=_params(3),
    )(a, b)


def _mixer_out_bwd(dx1, mixcat, mixed, y2, p, gt1, w_glu_b, b_glu, w_pool_b, b_pool, pscale, w_out_b, comm=None):
    bsz, seq, _ = dx1.shape
    tt = min(seq, TT_MIX)
    n_t = seq // tt
    ext_rows = tt + POOL_HALO

    def body(dx1_ref, mc_ref, mxd_ref, y_ref, p_ref, ph_ref, gt_ref, wg_ref, bg_ref, wp_ref, bp_ref, ps_ref, wo_ref,
             dy_ref, dp_ref, dwo_ref, dwg_ref, dbg_ref, dwp_ref, dbp_ref, dps_ref, dgt_ref, ext, qext):
        b = pl.program_id(0)
        i = pl.program_id(1)
        tile = n_t - 1 - i

        @pl.when((b == 0) & (i == 0))
        def _():
            for r in (dwo_ref, dwg_ref, dbg_ref, dwp_ref, dbp_ref, dps_ref):
                r[...] = jnp.zeros_like(r)

        @pl.when(i == 0)
        def _():
            dgt_ref[...] = jnp.zeros_like(dgt_ref)
            qext[tt:, :] = jnp.zeros((POOL_HALO, POOL_W), F32)

        dx1v = dx1_ref[0]
        mc = mc_ref[0]
        dgt_ref[0] += _colsum(dx1v * mxd_ref[0].astype(F32))
        dmixed = (gt_ref[0] * dx1v).astype(BF16)
        dwo_ref[...] += _dot_tn(mc, dmixed)
        dmc = _dot_nt(dmixed, wo_ref[...])
        pv = p_ref[0]
        ext[0:POOL_HALO, :] = ph_ref[0] * (tile > 0).astype(F32)
        ext[POOL_HALO:, :] = pv
        pos = (tile * tt + lax.broadcasted_iota(jnp.int32, (tt, 1), 0) + 1).astype(F32)
        zs, zls = _pool_forward(ext[...], pv, pos, wp_ref, bp_ref)
        dzs = []
        for gi, w in enumerate(POOL_WINDOWS):
            cs = slice(gi * 128, (gi + 1) * 128)
            dyp = dmc[:, SSM_W + gi * 128:SSM_W + (gi + 1) * 128]
            dps_ref[:, cs] += _colsum(dyp * zls[gi])
            dzl = dyp * ps_ref[:, cs]
            dbp_ref[:, cs] += _colsum(dzl)
            dzlb = dzl.astype(BF16)
            dwp_ref[gi] += _dot_tn(zs[gi].astype(BF16), dzlb)
            dz = _dot_nt(dzlb, wp_ref[gi])
            dzs.append(dz)
            qext[0:tt, cs] = dz / jnp.minimum(pos, float(w))
        cur = qext[...]
        dps = []
        for gi, w in enumerate(POOL_WINDOWS):
            cur = cur + pltpu.roll(cur, ext_rows - w // 2, 0)
            dps.append(cur[0:tt, 0:128] - dzs[gi])
            if gi + 1 < len(POOL_WINDOWS):
                cur = cur[:, 128:]
        qhead = qext[0:POOL_HALO, :]
        qext[tt:, :] = qhead
        dp_ref[0] = jnp.concatenate(dps, axis=1)
        yv = y_ref[...].astype(F32)
        ab = _gelu(yv).astype(BF16)
        gl = _dot(ab, wg_ref[...]) + bg_ref[...]
        val = gl[:, :SSM_W]
        sg = jax.nn.sigmoid(gl[:, SSM_W:])
        dys = dmc[:, :SSM_W]
        dgl = jnp.concatenate([dys * sg, dys * val * sg * (1.0 - sg)], axis=1)
        dbg_ref[...] += _colsum(dgl)
        dglb = dgl.astype(BF16)
        dwg_ref[...] += _dot_tn(ab, dglb)
        dy_ref[...] = (_dot_nt(dglb, wg_ref[...]) * _gelu_grad(yv)).astype(BF16)

    def rev(b, t):
        return (b, n_t - 1 - t, 0)

    def halo(b, t):
        return (b, jnp.maximum((n_t - 1 - t) * (tt // POOL_HALO) - 1, 0), 0)

    xt = pl.BlockSpec((1, tt, D), rev)
    pt = pl.BlockSpec((1, tt, POOL_W), rev)
    yt = pl.BlockSpec((tt, SSM_W), lambda b, t: (n_t - 1 - t, b))

    def whole(shape):
        return pl.BlockSpec(shape, lambda b, t: (0,) * len(shape))

    return _fused_call(
        body, name="mixer_out_bwd", grid=(bsz, n_t),
        in_specs=[xt, xt, xt, yt, pt, pl.BlockSpec((1, POOL_HALO, POOL_W), halo),
                  pl.BlockSpec((1, 1, D), lambda b, t: (b, 0, 0)), VMEM, VMEM, VMEM, VMEM, VMEM, VMEM],
        out_specs=[yt, pt, whole((D, D)), whole((SSM_W, 2 * SSM_W)), whole((1, 2 * SSM_W)),
                   whole((4, 128, 128)), whole((1, POOL_W)), whole((1, POOL_W)),
                   pl.BlockSpec((1, 1, D), lambda b, t: (b, 0, 0))],
        out_shape=[jax.ShapeDtypeStruct(y2.shape, BF16), jax.ShapeDtypeStruct(p.shape, F32),
                   jax.ShapeDtypeStruct((D, D), F32), jax.ShapeDtypeStruct((SSM_W, 2 * SSM_W), F32),
                   jax.ShapeDtypeStruct((1, 2 * SSM_W), F32), jax.ShapeDtypeStruct((4, 128, 128), F32),
                   jax.ShapeDtypeStruct((1, POOL_W), F32), jax.ShapeDtypeStruct((1, POOL_W), F32),
                   jax.ShapeDtypeStruct((bsz, 1, D), F32)],
        scratch_shapes=[pltpu.VMEM((POOL_HALO + tt, POOL_W), F32), pltpu.VMEM((ext_rows, POOL_W), F32)],
        args=(dx1, mixcat, mixed, y2, p, p, gt1, w_glu_b, b_glu, w_pool_b, b_pool, pscale, w_out_b),
        comm=_schedule(comm, bsz * n_t))


def _ssm_bwd(dy2r, u2r, xc, xs_re, xs_im, bb, cc, lam8, d8, tlen, comm=None):
    nv = lam8.shape[1]
    rows = nv * tlen
    n_chunks = u2r.shape[0] // rows

    def body(dy_ref, u_ref, xc_ref, xre_ref, xim_ref, bb_ref, cc_ref, lam_ref, d_ref,
             du_ref, dcc_ref, dbb_ref, dlam_ref, dd_ref, s_re, s_im, g_re, g_im, gst):
        i = pl.program_id(0)

        @pl.when(i == 0)
        def _():
            for r in (gst, dcc_ref, dbb_ref, dlam_ref, dd_ref):
                r[...] = jnp.zeros_like(r)

        u = u_ref[...].astype(F32)
        dy = dy_ref[...].astype(F32)
        par0 = (lax.broadcasted_iota(jnp.int32, (rows, 1), 0) % 2) == 0
        xre = xre_ref[...]
        xim = xim_ref[...]
        s_re[0] = xc_ref[0, 0]
        s_im[0] = xc_ref[0, 1]
        s_re[pl.ds(1, tlen)] = xre.astype(F32).reshape(tlen, nv, HALF_ST)
        s_im[pl.ds(1, tlen)] = xim.astype(F32).reshape(tlen, nv, HALF_ST)
        zero = jnp.zeros_like(dy)
        dy2 = jnp.concatenate([jnp.where(par0, dy, zero), jnp.where(par0, zero, dy)], axis=1).astype(BF16)
        u2 = jnp.concatenate([jnp.where(par0, u, zero), jnp.where(par0, zero, u)], axis=1).astype(BF16)
        dcc_ref[0:HALF_ST, :] += _dot_tn(xre, dy2)
        dcc_ref[HALF_ST:, :] += _dot_tn(xim, dy2)
        for part, gref in ((0, g_re), (1, g_im)):
            for k in range(HALF_ST // 512):
                r0 = part * HALF_ST + k * 512
                gref[:, :, k * 512:(k + 1) * 512] = _dot_nt(dy2, cc_ref[r0:r0 + 512, :]).reshape(tlen, nv, 512)
        for hb in range(HALF_ST // 512):
            ls = slice(hb * 512, (hb + 1) * 512)
            lr = lam_ref[0, :, ls]
            li = lam_ref[1, :, ls]

            def bstep(k, carry, ls=ls, lr=lr, li=li):
                t = tlen - 1 - k
                gr, gi, ar, ai = carry
                ngr = g_re[t, :, ls] + lr * gr + li * gi
                ngi = g_im[t, :, ls] + lr * gi - li * gr
                g_re[t, :, ls] = ngr
                g_im[t, :, ls] = ngi
                xpr = s_re[t, :, ls]
                xpi = s_im[t, :, ls]
                return ngr, ngi, ar + ngr * xpr + ngi * xpi, ai + ngi * xpr - ngr * xpi

            init = (gst[0, :, ls], gst[1, :, ls], dlam_ref[0, :, ls], dlam_ref[1, :, ls])
            gr, gi, ar, ai = lax.fori_loop(0, tlen, bstep, init, unroll=4)
            gst[0, :, ls] = gr
            gst[1, :, ls] = gi
            dlam_ref[0, :, ls] = ar
            dlam_ref[1, :, ls] = ai
        gre = g_re[...].reshape(rows, HALF_ST).astype(BF16)
        gim = g_im[...].reshape(rows, HALF_ST).astype(BF16)
        du0 = _dot_nt(gre, bb_ref[:, 0:HALF_ST]) + _dot_nt(gim, bb_ref[:, HALF_ST:2 * HALF_ST])
        du1 = _dot_nt(gre, bb_ref[:, 2 * HALF_ST:3 * HALF_ST]) + _dot_nt(gim, bb_ref[:, 3 * HALF_ST:])
        skip = (dy.reshape(tlen, nv, HALF_CH) * d_ref[...][None]).reshape(rows, HALF_CH)
        du_ref[...] = (jnp.where(par0, du0, du1) + skip).astype(BF16)
        dbb_ref[:, 0:HALF_ST] += _dot_tn(u2, gre)
        dbb_ref[:, HALF_ST:] += _dot_tn(u2, gim)
        dd_ref[...] += jnp.sum((dy * u).reshape(tlen, nv, HALF_CH), axis=0)

        @pl.when(i == n_chunks - 1)
        def _():
            dcc_ref[HALF_ST:, :] = -dcc_ref[HALF_ST:, :]

    def rev(c):
        return (n_chunks - 1 - c, 0)

    def whole(shape):
        return pl.BlockSpec(shape, lambda c: (0,) * len(shape))

    blk = pl.BlockSpec((rows, HALF_CH), rev)
    st_blk = pl.BlockSpec((rows, HALF_ST), rev)
    return _fused_call(
        body, name="ssm_bwd", grid=(n_chunks,),
        in_specs=[blk, blk, pl.BlockSpec((1, 2, nv, HALF_ST), lambda c: (n_chunks - 1 - c, 0, 0, 0)),
                  st_blk, st_blk, VMEM, VMEM, VMEM, VMEM],
        out_specs=[blk, whole((2 * HALF_ST, SSM_W)), whole((SSM_W, 2 * HALF_ST)), whole((2, nv, HALF_ST)),
                   whole((nv, HALF_CH))],
        out_shape=[jax.ShapeDtypeStruct(u2r.shape, BF16), jax.ShapeDtypeStruct((2 * HALF_ST, SSM_W), F32),
                   jax.ShapeDtypeStruct((SSM_W, 2 * HALF_ST), F32), jax.ShapeDtypeStruct((2, nv, HALF_ST), F32),
                   jax.ShapeDtypeStruct((nv, HALF_CH), F32)],
        scratch_shapes=[pltpu.VMEM((tlen + 1, nv, HALF_ST), F32), pltpu.VMEM((tlen + 1, nv, HALF_ST), F32),
                        pltpu.VMEM((tlen, nv, HALF_ST), F32), pltpu.VMEM((tlen, nv, HALF_ST), F32),
                        pltpu.VMEM((2, nv, HALF_ST), F32)],
        args=(dy2r, u2r, xc, xs_re, xs_im, bb, cc, lam8, d8), comm=_schedule(comm, n_chunks))


def _mixer_in_bwd(du2, dp, x, dx1, sh1, sc1, g_mix, w_in_b, comm=None):
    bsz, seq, _ = x.shape
    tt = min(seq, TT_MIX)

    def body(du_ref, dp_ref, x_ref, dx1_ref, sh_ref, sc_ref, g_ref, w_ref,
             dx_ref, dw_ref, dsh_ref, dsc_ref, dg_ref):
        b = pl.program_id(0)
        ti = pl.program_id(1)

        @pl.when((b == 0) & (ti == 0))
        def _():
            dw_ref[...] = jnp.zeros_like(dw_ref)
            dg_ref[...] = jnp.zeros_like(dg_ref)

        @pl.when(ti == 0)
        def _():
            dsh_ref[...] = jnp.zeros_like(dsh_ref)
            dsc_ref[...] = jnp.zeros_like(dsc_ref)

        dz = jnp.concatenate([du_ref[...], dp_ref[0].astype(BF16)], axis=1)
        xhat, rstd = _rms(x_ref[0])
        g = g_ref[...]
        sc = sc_ref[0]
        a = xhat * g
        h = (a * (1.0 + sc) + sh_ref[0]).astype(BF16)
        dw_ref[...] += _dot_tn(h, dz)
        dh = _dot_nt(dz, w_ref[...])
        dsh_ref[0] += _colsum(dh)
        dsc_ref[0] += _colsum(dh * a)
        t = dh * (1.0 + sc)
        dg_ref[...] += _colsum(t * xhat)
        dx_ref[0] = dx1_ref[0] + _rms_bwd(t * g, xhat, rstd)

    xt = pl.BlockSpec((1, tt, D), lambda b, t: (b, t, 0))
    row = pl.BlockSpec((1, 1, D), lambda b, t: (b, 0, 0))
    vec = pl.BlockSpec((1, D), lambda b, t: (0, 0))
    rows = jax.ShapeDtypeStruct((bsz, 1, D), F32)
    return _fused_call(
        body, name="mixer_in_bwd", grid=(bsz, seq // tt),
        in_specs=[pl.BlockSpec((tt, SSM_W), lambda b, t: (t, b)),
                  pl.BlockSpec((1, tt, POOL_W), lambda b, t: (b, t, 0)), xt, xt, row, row, vec, VMEM],
        out_specs=[xt, pl.BlockSpec((D, D), lambda b, t: (0, 0)), row, row, vec],
        out_shape=[jax.ShapeDtypeStruct(x.shape, F32), jax.ShapeDtypeStruct((D, D), F32), rows, rows,
                   jax.ShapeDtypeStruct((1, D), F32)],
        scratch_shapes=[], args=(du2, dp, x, dx1, sh1, sc1, g_mix, w_in_b),
        comm=_schedule(comm, bsz * (seq // tt)))


def kernel(x, c, w_ada, b_ada, g_norm_mix, w_in, ssm_lam_re, ssm_lam_im, ssm_log_dt, ssm_b_re, ssm_b_im, ssm_c_re, ssm_c_im, ssm_d, w_glu, b_glu, w_pool, b_pool, pool_scale, w_out, g_norm_ffn, w_up, w_conv, b_conv, w_down, g_norm_final, loss_target, m_w_ada, m_b_ada, m_g_norm_mix, m_w_in, m_ssm_lam_re, m_ssm_lam_im, m_ssm_log_dt, m_ssm_b_re, m_ssm_b_im, m_ssm_c_re, m_ssm_c_im, m_ssm_d, m_w_glu, m_b_glu, m_w_pool, m_b_pool, m_pool_scale, m_w_out, m_g_norm_ffn, m_w_up, m_w_conv, m_b_conv, m_w_down, m_g_norm_final, v_w_ada, v_b_ada, v_g_norm_mix, v_w_in, v_ssm_lam_re, v_ssm_lam_im, v_ssm_log_dt, v_ssm_b_re, v_ssm_b_im, v_ssm_c_re, v_ssm_c_im, v_ssm_d, v_w_glu, v_b_glu, v_w_pool, v_b_pool, v_pool_scale, v_w_out, v_g_norm_ffn, v_w_up, v_w_conv, v_b_conv, v_w_down, v_g_norm_final):
    bsz, seq, _ = x.shape
    assert 2 * bsz == 8 and seq % 128 == 0
    px, py, pc = _my_place()
    me = 4 * px + 2 * py + pc
    place = jnp.stack([pc, 2 * px + py]).astype(jnp.int32)
    ncol = ADA_COLS

    cpad = jnp.zeros((16, D), F32).at[0:bsz].set(c).at[8:11, 0:352].set(w_conv[0])
    cg, c_all, mod8, (g_in,) = _ada_fwd(cpad, w_ada[0], b_ada.reshape(N_DEV, 1, ncol), [w_in[0].astype(BF16)])
    w_conv_f = cg[:, 8:11, 0:352].transpose(1, 0, 2).reshape(3, DFF)
    w_in_b = g_in.reshape(D, D)
    sh1, sc1, gt1, sh2, sc2, gt2 = [mod8[0:bsz, k * D:(k + 1) * D].reshape(bsz, 1, D) for k in range(N_MOD)]

    lam_r = ssm_lam_re[0].reshape(1, GRP * NST)
    lam_i = ssm_lam_im[0].reshape(1, GRP * NST)
    ldt = jnp.repeat(ssm_log_dt[0], NST).reshape(1, GRP * NST)
    b_r = ssm_b_re[0].transpose(2, 0, 1).reshape(GCH, GRP * NST)
    b_i = ssm_b_im[0].transpose(2, 0, 1).reshape(GCH, GRP * NST)
    lbr, lbi, bbr, bbi = _ssm_prep(lam_r, lam_i, ldt, b_r, b_i)
    lam8 = jnp.stack([jnp.tile(lbr.reshape(2, HALF_ST), (bsz, 1)), jnp.tile(lbi.reshape(2, HALF_ST), (bsz, 1))])
    bd_r = _blockdiag(bbr.reshape(GCH, 2, GRP // 2, NST).transpose(1, 2, 0, 3))
    bd_i = _blockdiag(bbi.reshape(GCH, 2, GRP // 2, NST).transpose(1, 2, 0, 3))
    bb = jnp.concatenate([bd_r[0], bd_i[0], bd_r[1], bd_i[1]], axis=1).astype(BF16)
    cd_r = _blockdiag(ssm_c_re[0].reshape(2, GRP // 2, GCH, NST).transpose(0, 1, 3, 2))
    cd_i = _blockdiag(ssm_c_im[0].reshape(2, GRP // 2, GCH, NST).transpose(0, 1, 3, 2))
    cc = jnp.concatenate([jnp.concatenate([cd_r[0], cd_r[1]], axis=1),
                          jnp.concatenate([-cd_i[0], -cd_i[1]], axis=1)], axis=0).astype(BF16)
    d8 = jnp.tile(ssm_d[0].reshape(2, HALF_CH), (bsz, 1))

    tlen = min(seq, T_SSM)
    (u2, p), ((g_glu, g_out),) = _mixer_in_fwd(
        x, sh1, sc1, g_norm_mix, w_in_b, comm=[(_gather_plan, [w_glu[0].astype(BF16), w_out[0].astype(BF16)])])
    w_glu_b = g_glu.transpose(1, 0, 2).reshape(SSM_W, 2 * SSM_W)
    w_out_b = g_out.reshape(D, D)
    u2r = u2.reshape(seq * 2 * bsz, HALF_CH)
    (y2r, xc, xs_re, xs_im), ((g_up, g_down),) = _ssm_fwd(
        u2r, bb, cc, lam8, d8, tlen, comm=[(_gather_plan, [w_up[0].T.astype(BF16), w_down[0].astype(BF16)])])
    w_up_b = g_up.reshape(2 * DFF, D)
    w_down_b = g_down.reshape(DFF, D)
    y2 = y2r.reshape(seq, bsz * SSM_W)
    w_pool_b = w_pool[0].astype(BF16)
    bp = b_pool[0].reshape(1, POOL_W)
    x1, mixcat, mixed = _mixer_out_fwd(y2, p, x, gt1, w_glu_b, b_glu, w_pool_b, bp, pool_scale, w_out_b)
    h2, vq, gq, gcq, act, ddn, dx2, loss_l, dg_fin, dgt2 = _ffn_fwd(
        x1, loss_target, sh2, sc2, gt2, g_norm_ffn, w_up_b, w_conv_f, b_conv, w_down_b, g_norm_final.reshape(1, D))

    dup, dx1, dsh2, dsc2, dg_ffn, dw_conv, db_conv = _ffn_bwd(
        ddn, gq, gcq, vq, x1, dx2, sh2, sc2, g_norm_ffn, w_conv_f, w_down_b, w_up_b)
    ntok = bsz * seq
    dw_up_t = _wgrad(dup.reshape(ntok, 2 * DFF), h2.reshape(ntok, D), DFF // 2, D, "wgrad_up")
    dw_down = _wgrad(act.reshape(ntok, DFF), ddn.reshape(ntok, D), DFF, 512, "wgrad_down")
    g42_up = dw_up_t.reshape(4, 2, 704, D)
    g42_down = dw_down.reshape(4, 2, 352, D)
    (dy2, dp, dw_out, dw_glu, db_glu, dw_pool, db_pool, dpscale, dgt1), ((ra_up, ra_down),) = _mixer_out_bwd(
        dx1, mixcat, mixed, y2, p, gt1, w_glu_b, b_glu, w_pool_b, bp, pool_scale, w_out_b,
        comm=[(_pair_plan, [g42_up, g42_down])])
    own_up, s_up = _pair_sum(g42_up, ra_up, place, "pair_sum_up")
    own_down, s_down = _pair_sum(g42_down, ra_down, place, "pair_sum_down")
    g42_glu = dw_glu.reshape(SSM_W, N_DEV, 128).transpose(1, 0, 2).reshape(4, 2, SSM_W, 128)
    g42_out = dw_out.reshape(4, 2, 128, D)
    small_a = [
        ("b_glu", (1, 2 * SSM_W), db_glu), ("w_pool", (POOL_W, 128), dw_pool.reshape(POOL_W, 128)),
        ("b_pool", (4, 128), db_pool.reshape(4, 128)), ("pool_scale", (1, POOL_W), dpscale),
        ("g_norm_ffn", (1, D), dg_ffn), ("b_conv", (1, DFF), db_conv), ("g_norm_final", (1, D), dg_fin)]
    (du2r, dcc, dbb, dlam8, dd8), ((rc_up, rc_down), (ra_glu, ra_out), parts_a) = _ssm_bwd(
        dy2.reshape(u2r.shape), u2r, xc, xs_re, xs_im, bb, cc, lam8, d8, tlen,
        comm=[(_chip_plan, [s_up, s_down]), (_pair_plan, [g42_glu, g42_out]),
              (_gather_plan, [g for _, _, g in small_a] + [dw_conv, loss_l])])
    big_up = [t.T for t in _final_sum_adamw(own_up, rc_up, w_up[0].T, m_w_up[0].T, v_w_up[0].T, "final_adamw_up")]
    big_down = _final_sum_adamw(own_down, rc_down, w_down[0], m_w_down[0], v_w_down[0], "final_adamw_down")
    own_glu, s_glu = _pair_sum(g42_glu, ra_glu, place, "pair_sum_glu")
    own_out, s_out = _pair_sum(g42_out, ra_out, place, "pair_sum_out")

    def take_c(t):
        return _blockdiag_take(t, NST, GCH).transpose(0, 2, 1)

    dc_re = jnp.concatenate([take_c(dcc[0:HALF_ST, e * HALF_CH:(e + 1) * HALF_CH]) for e in range(2)], axis=0)
    dc_im = jnp.concatenate([take_c(dcc[HALF_ST:, e * HALF_CH:(e + 1) * HALF_CH]) for e in range(2)], axis=0)

    def take_b(t):
        return _blockdiag_take(t, GCH, NST).transpose(1, 0, 2)

    dbbr = jnp.concatenate([take_b(dbb[e * HALF_CH:(e + 1) * HALF_CH, 0:HALF_ST]) for e in range(2)], axis=1)
    dbbi = jnp.concatenate([take_b(dbb[e * HALF_CH:(e + 1) * HALF_CH, HALF_ST:]) for e in range(2)], axis=1)
    glr, gli, gldt, gbr, gbi, gd = _ssm_param_bwd(
        lam_r, lam_i, ldt, b_r, b_i, dlam8, dbbr.reshape(GCH, GRP * NST), dbbi.reshape(GCH, GRP * NST), dd8)
    g_log_dt = jnp.sum(gldt.reshape(GRP, NST), axis=1)

    def view(a, shp):
        return a.reshape(shp)

    small_b = [
        ("ssm_lam_re", (GRP, NST), glr.reshape(GRP, NST)), ("ssm_lam_im", (GRP, NST), gli.reshape(GRP, NST)),
        ("ssm_log_dt", (1, GRP), g_log_dt.reshape(1, GRP)),
        ("ssm_c_re", (GRP * GCH, NST), dc_re.reshape(GRP * GCH, NST)),
        ("ssm_c_im", (GRP * GCH, NST), dc_im.reshape(GRP * GCH, NST)), ("ssm_d", (1, SSM_W), gd)]
    small = small_a + small_b
    given = dict(
        ssm_lam_re=(ssm_lam_re, m_ssm_lam_re, v_ssm_lam_re), ssm_lam_im=(ssm_lam_im, m_ssm_lam_im, v_ssm_lam_im),
        ssm_log_dt=(ssm_log_dt, m_ssm_log_dt, v_ssm_log_dt), ssm_c_re=(ssm_c_re, m_ssm_c_re, v_ssm_c_re),
        ssm_c_im=(ssm_c_im, m_ssm_c_im, v_ssm_c_im), ssm_d=(ssm_d, m_ssm_d, v_ssm_d), b_glu=(b_glu, m_b_glu, v_b_glu),
        w_pool=(w_pool, m_w_pool, v_w_pool), b_pool=(b_pool, m_b_pool, v_b_pool),
        pool_scale=(pool_scale, m_pool_scale, v_pool_scale), g_norm_ffn=(g_norm_ffn, m_g_norm_ffn, v_g_norm_ffn),
        b_conv=(b_conv, m_b_conv, v_b_conv), g_norm_final=(g_norm_final, m_g_norm_final, v_g_norm_final),
        ssm_b_re=(ssm_b_re, m_ssm_b_re, v_ssm_b_re), ssm_b_im=(ssm_b_im, m_ssm_b_im, v_ssm_b_im))
    b_view = (GRP * NST, GCH)
    (grad_x, dw_in, dsh1, dsc1, dg_mix), ((rc_glu, rc_out), parts_b) = _mixer_in_bwd(
        du2r.reshape(u2.shape), dp, x, dx1, sh1, sc1, g_norm_mix, w_in_b,
        comm=[(_chip_plan, [s_glu, s_out]), (_gather_plan, [g for _, _, g in small_b] + [gbr, gbi])])
    big_glu = _final_sum_adamw(own_glu, rc_glu, w_glu[0], m_w_glu[0], v_w_glu[0], "final_adamw_glu")
    big_out = _final_sum_adamw(own_out, rc_out, w_out[0], m_w_out[0], v_w_out[0], "final_adamw_out")
    parts = list(parts_a[:-2]) + list(parts_b[:-2])
    items = [(pt,) + tuple(view(a, shp) for a in given[nm]) for pt, (nm, shp, _) in zip(parts, small)]
    small_out, (g_conv_full, loss_all, gbr_all, gbi_all) = _small_sum_adamw(
        items, [parts_a[-2], parts_a[-1], parts_b[-2], parts_b[-1]])
    loss = loss_all[0, 0]
    result = {nm: [t.reshape(given[nm][0].shape) for t in quad] for quad, (nm, _, _) in zip(small_out, small)}
    for nm, g_all in (("ssm_b_re", gbr_all), ("ssm_b_im", gbi_all)):
        quad = [g_all.T] + list(_adamw_plain(g_all.T, *[view(a, b_view) for a in given[nm]]))
        result[nm] = [t.reshape(given[nm][0].shape) for t in quad]
    g_w_conv = lax.dynamic_slice_in_dim(g_conv_full, 352 * me, 352, axis=1)
    result["w_conv"] = [g_w_conv[None]] + [t[None] for t in _adamw_plain(g_w_conv, w_conv[0], m_w_conv[0], v_w_conv[0])]

    for nm, quad in (("w_glu", big_glu), ("w_out", big_out), ("w_up", big_up), ("w_down", big_down)):
        result[nm] = [t[None] for t in quad]

    dmod = jnp.concatenate([t.reshape(bsz, D) for t in (dsh1, dsc1, dgt1, dsh2, dsc2, dgt2)], axis=1)
    dmod_blk = jnp.zeros((N_DEV, 8, ncol), F32).at[:, 0:bsz].set(dmod.reshape(bsz, N_DEV, ncol).transpose(1, 0, 2))
    dmod_blk = dmod_blk.at[0, ADA_RIDER_ROW].set(dg_mix[0, 0:ncol]).at[1, ADA_RIDER_ROW, 0:D - ncol].set(dg_mix[0, ncol:])
    ada = _ada_bwd(dmod_blk.reshape(ADA_ROWS, ncol), c_all, w_ada[0], m_w_ada[0], v_w_ada[0],
                   b_ada, m_b_ada, v_b_ada, g_norm_mix, m_g_norm_mix, v_g_norm_mix,
                   dw_in.reshape(4, 2, 128, D), w_in[0], m_w_in[0], v_w_in[0])
    result["w_ada"] = [t[None] for t in ada[0:4]]
    result["b_ada"] = list(ada[4:8])
    result["g_norm_mix"] = list(ada[8:12])
    result["w_in"] = [t[None] for t in ada[12:16]]

    names = ["w_ada", "b_ada", "g_norm_mix", "w_in", "ssm_lam_re", "ssm_lam_im", "ssm_log_dt", "ssm_b_re", "ssm_b_im",
             "ssm_c_re", "ssm_c_im", "ssm_d", "w_glu", "b_glu", "w_pool", "b_pool", "pool_scale", "w_out", "g_norm_ffn",
             "w_up", "w_conv", "b_conv", "w_down", "g_norm_final"]
    return (loss, grad_x, *[result[nm][k] for k in range(4) for nm in names])
```

```python
import functools
import math

import jax
import jax.numpy as jnp
from jax import lax
from jax.experimental import pallas as pl
from jax.experimental.pallas import tpu as pltpu

F32 = jnp.float32
BF16 = jnp.bfloat16

D = 1024
SSM_W = 512
POOL_W = 512
GRP = 32
GCH = 16
NST = 64
HALF_ST = GRP * NST // 2
HALF_CH = SSM_W // 2
DFF = 2816
FF_CH = 2816
N_MOD = 6
N_DEV = 8
EPS = 1e-6
POOL_WINDOWS = (2, 4, 8, 16)
POOL_HALO = 16
CONV_HALO = 8
GELU_C = math.sqrt(2.0 / math.pi)
GELU_A = 0.044715

ADAM_LR = 0.001
ADAM_B1 = 0.9
ADAM_B2 = 0.999
ADAM_EPS = 1e-08
ADAM_WD = 0.01
ADAM_STEP = 10

VMEM_LIMIT = 56 * 1024 * 1024
TT_MIX = 512
TT_FFN = 256
T_SSM = 128
TT_WGRAD = 2048
MESH = pl.DeviceIdType.MESH
NT = (((1,), (1,)), ((), ()))
TN = (((0,), (0,)), ((), ()))
ANY = pl.BlockSpec(memory_space=pl.ANY)
VMEM = pl.BlockSpec(memory_space=pltpu.VMEM)


def _params(n_grid, vmem=VMEM_LIMIT):
    return pltpu.CompilerParams(dimension_semantics=("arbitrary",) * n_grid, vmem_limit_bytes=vmem)


def _dot(a, b):
    return jnp.dot(a, b, preferred_element_type=F32)


def _dot_nt(a, b):
    return lax.dot_general(a, b, NT, preferred_element_type=F32)


def _dot_tn(a, b):
    return lax.dot_general(a, b, TN, preferred_element_type=F32)


def _colsum(a):
    return jnp.sum(a, axis=0, keepdims=True)


def _rms(x):
    rstd = lax.rsqrt(jnp.mean(x * x, axis=-1, keepdims=True) + EPS)
    return x * rstd, rstd


def _rms_bwd(dxhat, xhat, rstd):
    return rstd * (dxhat - xhat * jnp.mean(dxhat * xhat, axis=-1, keepdims=True))


def _gelu(x):
    return 0.5 * x * (1.0 + jnp.tanh(GELU_C * (x + GELU_A * x * x * x)))


def _gelu_grad(x):
    x2 = x * x
    th = jnp.tanh(GELU_C * (x + GELU_A * x * x2))
    return 0.5 * (1.0 + th) + 0.5 * x * (1.0 - th * th) * GELU_C * (1.0 + 3.0 * GELU_A * x2)


def _adamw(w, g, m, v):
    m = ADAM_B1 * m + (1.0 - ADAM_B1) * g
    v = ADAM_B2 * v + (1.0 - ADAM_B2) * (g * g)
    m_hat = m / (1.0 - ADAM_B1 ** ADAM_STEP)
    v_hat = v / (1.0 - ADAM_B2 ** ADAM_STEP)
    delta = -ADAM_LR * (m_hat / (jnp.sqrt(v_hat) + ADAM_EPS) + ADAM_WD * w)
    return delta, m, v


def _my_place():
    return lax.axis_index("x"), lax.axis_index("y"), lax.axis_index("c")


def _gather_plan(shards):
    n = len(shards)
    out_shape = [jax.ShapeDtypeStruct((N_DEV,) + tuple(s.shape), s.dtype) for s in shards]
    scratch = [pltpu.SemaphoreType.DMA((n, 7)), pltpu.SemaphoreType.DMA((n, 7)), pltpu.SemaphoreType.DMA((n,))]

    def stages(x_refs, out_refs, sems):
        send_sems, recv_sems, local_sems = sems
        x, y, c = _my_place()
        me, sibling = (x, y, c), (x, y, 1 - c)
        chips = [(1 - x, y), (x, 1 - y), (1 - x, 1 - y)]

        def copy(i, k, block, to, own=False):
            px, py, pc = block
            dst = out_refs[i].at[4 * px + 2 * py + pc]
            return pltpu.make_async_remote_copy(
                src_ref=x_refs[i] if own else dst, dst_ref=dst, send_sem=send_sems.at[i, k],
                recv_sem=recv_sems.at[i, k], device_id=to, device_id_type=MESH)

        def mine(i):
            return pltpu.make_async_copy(x_refs[i], out_refs[i].at[4 * x + 2 * y + c], local_sems.at[i])

        def start():
            for i in range(n):
                mine(i).start()
                copy(i, 0, me, sibling, own=True).start()
                for j, chip in enumerate(chips):
                    copy(i, 1 + j, me, (*chip, c), own=True).start()

        def forward():
            for i in range(n):
                for j, chip in enumerate(chips):
                    copy(i, 1 + j, (*chip, c), me).wait_recv()
                    copy(i, 4 + j, (*chip, c), sibling).start()

        def finish():
            for i in range(n):
                copy(i, 0, sibling, me).wait_recv()
                copy(i, 0, me, sibling, own=True).wait_send()
                for j, chip in enumerate(chips):
                    copy(i, 4 + j, (*chip, 1 - c), me).wait_recv()
                    copy(i, 1 + j, me, (*chip, c), own=True).wait_send()
                    copy(i, 4 + j, (*chip, c), sibling).wait_send()
                mine(i).wait()

        return [start, forward, finish]

    return n, out_shape, scratch, stages


def _pair_plan(g42s):
    n = len(g42s)
    out_shape = [jax.ShapeDtypeStruct((4,) + tuple(g.shape[2:]), g.dtype) for g in g42s]
    scratch = [pltpu.SemaphoreType.DMA((n,)), pltpu.SemaphoreType.DMA((n,))]

    def stages(g_refs, out_refs, sems):
        send_sems, recv_sems = sems
        x, y, c = _my_place()

        def copy(i):
            return pltpu.make_async_remote_copy(
                src_ref=g_refs[i].at[:, 1 - c], dst_ref=out_refs[i], send_sem=send_sems.at[i],
                recv_sem=recv_sems.at[i], device_id=(x, y, 1 - c), device_id_type=MESH)

        def start():
            for i in range(n):
                copy(i).start()

        def finish():
            for i in range(n):
                copy(i).wait()

        return [start, finish]

    return n, out_shape, scratch, stages


def _chip_plan(s4s):
    n = len(s4s)
    out_shape = [jax.ShapeDtypeStruct((3,) + tuple(s.shape[1:]), s.dtype) for s in s4s]
    scratch = [pltpu.SemaphoreType.DMA((n, 3)), pltpu.SemaphoreType.DMA((n, 3))]

    def stages(s_refs, out_refs, sems):
        send_sems, recv_sems = sems
        x, y, c = _my_place()

        def copy(i, d):
            px, py = x ^ (d >> 1), y ^ (d & 1)
            return pltpu.make_async_remote_copy(
                src_ref=s_refs[i].at[2 * px + py], dst_ref=out_refs[i].at[d - 1], send_sem=send_sems.at[i, d - 1],
                recv_sem=recv_sems.at[i, d - 1], device_id=(px, py, c), device_id_type=MESH)

        def start():
            for i in range(n):
                for d in (1, 2, 3):
                    copy(i, d).start()

        def finish():
            for i in range(n):
                for d in (1, 2, 3):
                    copy(i, d).wait()

        return [start, finish]

    return n, out_shape, scratch, stages


def _comm_call(plan, arrays, name):
    n, out_shape, scratch, stages = plan

    def body(*refs):
        for stage in stages(refs[:n], refs[n:2 * n], refs[2 * n:]):
            stage()

    return pl.pallas_call(
        body, name=name, out_shape=out_shape, in_specs=[ANY] * n, out_specs=[ANY] * n, scratch_shapes=scratch,
    )(*arrays)


def _fused_call(body, *, name, grid, in_specs, out_specs, out_shape, scratch_shapes, args, comm=None, aliases=None):
    if not comm:
        out = pl.pallas_call(body, name=name, grid=grid, in_specs=in_specs, out_specs=out_specs, out_shape=out_shape,
                             scratch_shapes=scratch_shapes, input_output_aliases=aliases or {},
                             compiler_params=_params(len(grid)))(*args)
        return out, []
    assert not aliases
    counts = [plan[0] for plan, _, _ in comm]
    n = sum(counts)
    n_in, n_out, n_scr = len(in_specs), len(out_specs), len(scratch_shapes)

    def fused(*refs):
        ins, refs = refs[:n_in], refs[n_in:]
        c_ins, refs = refs[:n], refs[n:]
        outs, refs = refs[:n_out], refs[n_out:]
        c_outs, refs = refs[:n], refs[n:]
        scr, c_scr = refs[:n_scr], refs[n_scr:]
        step = pl.program_id(0)
        for k in range(1, len(grid)):
            step = step * grid[k] + pl.program_id(k)
        todo, a0, s0 = [], 0, 0
        for (cnt, _, plan_scratch, stages), _, steps in comm:
            sems = c_scr[s0:s0 + len(plan_scratch)]
            todo += list(zip(stages(c_ins[a0:a0 + cnt], c_outs[a0:a0 + cnt], sems), steps))
            a0 += cnt
            s0 += len(plan_scratch)
        for stage, at in todo:
            if at == 0:
                pl.when(step == 0)(stage)
        body(*ins, *outs, *scr)
        for stage, at in todo:
            if at != 0:
                pl.when(step == at)(stage)

    c_shape = [s for plan, _, _ in comm for s in plan[1]]
    c_scratch = [s for plan, _, _ in comm for s in plan[2]]
    arrays = [a for _, arrs, _ in comm for a in arrs]
    out = pl.pallas_call(
        fused, name=name, grid=grid, in_specs=list(in_specs) + [ANY] * n, out_specs=list(out_specs) + [ANY] * n,
        out_shape=list(out_shape) + c_shape, scratch_shapes=list(scratch_shapes) + c_scratch,
        compiler_params=_params(len(grid)))(*args, *arrays)
    outs, c_outs, split, a0 = out[:n_out], out[n_out:], [], 0
    for cnt in counts:
        split.append(c_outs[a0:a0 + cnt])
        a0 += cnt
    return outs, split


def _schedule(comm, n_steps):
    out = []
    for make_plan, arrays in comm or []:
        steps = [0, (3 * n_steps) // 4, n_steps - 1] if make_plan is _gather_plan else [0, n_steps - 1]
        out.append((make_plan(arrays), arrays, steps))
    return out


def _row_tile(r):
    for t in (128, 64, 32, 16, 8):
        if r % t == 0:
            return t
    return r


def _pair_sum(g42, recv, place, name):
    _, _, r, cdim = g42.shape
    tr = _row_tile(r)

    def body(pl_ref, g_ref, r_ref, own_ref, s_ref):
        s_ref[...] = (g_ref[:, 0] + r_ref[...]).astype(BF16)
        q = pl_ref[1]
        own_ref[...] = g_ref[q, 0] + r_ref[q]

    return pl.pallas_call(
        body, name=name,
        grid_spec=pltpu.PrefetchScalarGridSpec(
            num_scalar_prefetch=1, grid=(r // tr,),
            in_specs=[pl.BlockSpec((4, 1, tr, cdim), lambda i, p: (0, p[0], i, 0)),
                      pl.BlockSpec((4, tr, cdim), lambda i, p: (0, i, 0))],
            out_specs=[pl.BlockSpec((tr, cdim), lambda i, p: (i, 0)),
                       pl.BlockSpec((4, tr, cdim), lambda i, p: (0, i, 0))]),
        out_shape=[jax.ShapeDtypeStruct((r, cdim), F32), jax.ShapeDtypeStruct((4, r, cdim), BF16)],
        compiler_params=_params(1),
    )(place, g42, recv)


def _final_sum_adamw(own, recv3, w, m, v, name):
    r, cdim = w.shape
    tr = _row_tile(r)

    def body(s_ref, r_ref, w_ref, m_ref, v_ref, g_out, d_out, m_out, v_out):
        g = s_ref[...] + r_ref[0].astype(F32) + r_ref[1].astype(F32) + r_ref[2].astype(F32)
        d, mn, vn = _adamw(w_ref[...], g, m_ref[...], v_ref[...])
        g_out[...] = g
        d_out[...] = d
        m_out[...] = mn
        v_out[...] = vn

    blk = pl.BlockSpec((tr, cdim), lambda i: (i, 0))
    shp = jax.ShapeDtypeStruct((r, cdim), F32)
    return pl.pallas_call(
        body, name=name, grid=(r // tr,),
        in_specs=[blk, pl.BlockSpec((3, tr, cdim), lambda i: (0, i, 0)), blk, blk, blk],
        out_specs=[blk, blk, blk, blk], out_shape=[shp, shp, shp, shp], compiler_params=_params(1),
    )(own, recv3, w, m, v)


def _small_sum_adamw(items, sums_only):
    n, ne = len(items), len(sums_only)

    def total(p_ref):
        g = p_ref[0]
        for k in range(1, N_DEV):
            g = g + p_ref[k]
        return g

    def body(*refs):
        ins, outs = refs[:4 * n + ne], refs[4 * n + ne:]
        for i in range(n):
            p_ref, w_ref, m_ref, v_ref = ins[4 * i:4 * i + 4]
            g = total(p_ref)
            d, mn, vn = _adamw(w_ref[...], g, m_ref[...], v_ref[...])
            for o_ref, val in zip(outs[4 * i:4 * i + 4], (g, d, mn, vn)):
                o_ref[...] = val
        for j in range(ne):
            outs[4 * n + j][...] = total(ins[4 * n + j])

    args = [a for item in items for a in item] + list(sums_only)
    shapes = [jax.ShapeDtypeStruct(w.shape, F32) for _, w, _, _ in items for _ in range(4)]
    shapes += [jax.ShapeDtypeStruct(p.shape[1:], F32) for p in sums_only]
    out = pl.pallas_call(
        body, name="small_sum_adamw", in_specs=[VMEM] * len(args), out_specs=[VMEM] * len(shapes), out_shape=shapes,
        compiler_params=_params(0),
    )(*args)
    return [out[4 * i:4 * i + 4] for i in range(n)], out[4 * n:]


def _adamw_plain(g, w, m, v):
    r, cdim = w.shape
    tr = r if r * cdim <= 64 * 1024 else _row_tile(r)

    def body(g_ref, w_ref, m_ref, v_ref, d_out, m_out, v_out):
        d, mn, vn = _adamw(w_ref[...], g_ref[...], m_ref[...], v_ref[...])
        d_out[...] = d
        m_out[...] = mn
        v_out[...] = vn

    blk = pl.BlockSpec((tr, cdim), lambda i: (i, 0))
    shp = jax.ShapeDtypeStruct((r, cdim), F32)
    return pl.pallas_call(
        body, name="adamw_plain", grid=(r // tr,), in_specs=[blk, blk, blk, blk],
        out_specs=[blk, blk, blk], out_shape=[shp, shp, shp], compiler_params=_params(1),
    )(g, w, m, v)


ADA_COLS = N_MOD * D // N_DEV
ADA_ROWS = 8 * N_DEV


def _ada_fwd(cpad, w_ada, b_blocks, mixer_shards):
    n_w, w_shape, w_scr, w_stages = _gather_plan(mixer_shards)
    _, _, c_scr, c_stages = _gather_plan([cpad])
    _, _, p_scr, p_stages = _gather_plan([jax.ShapeDtypeStruct((ADA_ROWS, ADA_COLS), F32)])

    def body(c_ref, wa_ref, b_ref, *refs):
        w_refs, refs = refs[:n_w], refs[n_w:]
        cg_ref, call_ref, mod_ref = refs[:3]
        wg_refs, refs = refs[3:3 + n_w], refs[3 + n_w:]
        part_ref, pg_ref = refs[:2]
        c_sems, p_sems, w_sems = refs[2:5], refs[5:8], refs[8:11]
        w_start, w_forward, w_finish = w_stages(w_refs, wg_refs, w_sems)
        w_start()
        for stage in c_stages([c_ref], [cg_ref], c_sems):
            stage()
        cv = cg_ref[:, 0:8, :].reshape(ADA_ROWS, D)
        call_ref[...] = cv
        part_ref[...] = _dot(cv * jax.nn.sigmoid(cv), wa_ref[...])
        for stage in p_stages([part_ref], [pg_ref], p_sems):
            stage()
        x, y, c = _my_place()
        r0 = pl.multiple_of(8 * (4 * x + 2 * y + c), 8)
        for k in range(N_DEV):
            mod_ref[:, k * ADA_COLS:(k + 1) * ADA_COLS] = pg_ref[k, pl.ds(r0, 8), :] + b_ref[k]
        w_forward()
        w_finish()

    out = pl.pallas_call(
        body, name="ada_fwd", in_specs=[VMEM, VMEM, VMEM] + [ANY] * n_w,
        out_specs=[VMEM, VMEM, VMEM] + [ANY] * n_w,
        out_shape=[jax.ShapeDtypeStruct((N_DEV,) + cpad.shape, F32), jax.ShapeDtypeStruct((ADA_ROWS, D), F32),
                   jax.ShapeDtypeStruct((8, N_MOD * D), F32)] + list(w_shape),
        scratch_shapes=[pltpu.VMEM((ADA_ROWS, ADA_COLS), F32), pltpu.VMEM((N_DEV, ADA_ROWS, ADA_COLS), F32)]
        + list(c_scr) + list(p_scr) + list(w_scr),
        compiler_params=_params(0),
    )(cpad, w_ada, b_blocks, *mixer_shards)
    return out[0], out[1], out[2], out[3:]


ADA_RIDER_ROW = 4


def _ada_bwd(dmod_blk, c_all, w_ada, m_w, v_w, b_blocks, m_b, v_b, g_w, g_m, g_v, g42, w_l, m_l, v_l):
    _, _, g_scr, g_stages = _gather_plan([dmod_blk])
    rest = D - ADA_COLS
    blk = tuple(g42.shape[2:])

    def body(dm_ref, c_ref, w_ref, mw_ref, vw_ref, b_ref, mb_ref, vb_ref, gw_ref, gm_ref, gv_ref,
             g42_ref, wl_ref, ml_ref, vl_ref,
             gw_o, dw_o, mw_o, vw_o, gb_o, dbb_o, mb_o, vb_o, gg_o, dgg_o, mg_o, vg_o, gl_o, dl_o, ml_o, vl_o,
             dg_ref, pr_ref, sbf_ref, rc_ref, pair_send, pair_recv, chip_send, chip_recv, *sems):
        x, y, c = _my_place()
        q = 2 * x + y
        pair = pltpu.make_async_remote_copy(
            src_ref=g42_ref.at[:, 1 - c], dst_ref=pr_ref, send_sem=pair_send, recv_sem=pair_recv,
            device_id=(x, y, 1 - c), device_id_type=MESH)
        pair.start()
        for stage in g_stages([dm_ref], [dg_ref], sems):
            stage()
        r0 = pl.multiple_of(8 * (4 * x + 2 * y + c), 8)
        cols = dg_ref[:, pl.ds(r0, 8), :].reshape(ADA_ROWS, ADA_COLS)
        cv = c_ref[...]
        gw = _dot_tn(cv * jax.nn.sigmoid(cv), cols)
        d, mn, vn = _adamw(w_ref[...], gw, mw_ref[...], vw_ref[...])
        gw_o[...] = gw
        dw_o[...] = d
        mw_o[...] = mn
        vw_o[...] = vn
        is_example = lax.broadcasted_iota(jnp.int32, (8, 1), 0) < ADA_RIDER_ROW
        blocks = []
        for k in range(N_DEV):
            s = dg_ref[0, 8 * k:8 * k + 8, :]
            for dev in range(1, N_DEV):
                s = s + dg_ref[dev, 8 * k:8 * k + 8, :]
            blocks.append(s)
            gb = _colsum(jnp.where(is_example, s, 0.0))
            cs = slice(k * ADA_COLS, (k + 1) * ADA_COLS)
            d, mn, vn = _adamw(b_ref[:, cs], gb, mb_ref[:, cs], vb_ref[:, cs])
            gb_o[:, cs] = gb
            dbb_o[:, cs] = d
            mb_o[:, cs] = mn
            vb_o[:, cs] = vn
        rider = jnp.concatenate([blocks[0][ADA_RIDER_ROW:ADA_RIDER_ROW + 1, :],
                                 blocks[1][ADA_RIDER_ROW:ADA_RIDER_ROW + 1, 0:rest]], axis=1)
        d, mn, vn = _adamw(gw_ref[...], rider, gm_ref[...], gv_ref[...])
        gg_o[...] = rider
        dgg_o[...] = d
        mg_o[...] = mn
        vg_o[...] = vn
        pair.wait()
        for k in range(4):
            sbf_ref[k] = (g42_ref[k, c] + pr_ref[k]).astype(BF16)

        def chip_copy(dist):
            px, py = x ^ (dist >> 1), y ^ (dist & 1)
            return pltpu.make_async_remote_copy(
                src_ref=sbf_ref.at[2 * px + py], dst_ref=rc_ref.at[dist - 1], send_sem=chip_send.at[dist - 1],
                recv_sem=chip_recv.at[dist - 1], device_id=(px, py, c), device_id_type=MESH)

        for dist in (1, 2, 3):
            chip_copy(dist).start()
        for dist in (1, 2, 3):
            chip_copy(dist).wait()
        gl = g42_ref[q, c] + pr_ref[q] + rc_ref[0].astype(F32) + rc_ref[1].astype(F32) + rc_ref[2].astype(F32)
        d, mn, vn = _adamw(wl_ref[...], gl, ml_ref[...], vl_ref[...])
        gl_o[...] = gl
        dl_o[...] = d
        ml_o[...] = mn
        vl_o[...] = vn

    ws = jax.ShapeDtypeStruct(w_ada.shape, F32)
    bs = jax.ShapeDtypeStruct(b_blocks.shape, F32)
    gs = jax.ShapeDtypeStruct(g_w.shape, F32)
    ls = jax.ShapeDtypeStruct(w_l.shape, F32)
    return pl.pallas_call(
        body, name="ada_bwd", in_specs=[VMEM] * 15, out_specs=[VMEM] * 16,
        out_shape=[ws, ws, ws, ws, bs, bs, bs, bs, gs, gs, gs, gs, ls, ls, ls, ls],
        scratch_shapes=[pltpu.VMEM((N_DEV, ADA_ROWS, ADA_COLS), F32), pltpu.VMEM((4,) + blk, F32),
                        pltpu.VMEM((4,) + blk, BF16), pltpu.VMEM((3,) + blk, BF16),
                        pltpu.SemaphoreType.DMA(()), pltpu.SemaphoreType.DMA(()),
                        pltpu.SemaphoreType.DMA((3,)), pltpu.SemaphoreType.DMA((3,))] + list(g_scr),
        compiler_params=_params(0),
    )(dmod_blk, c_all, w_ada, m_w, v_w, b_blocks, m_b, v_b, g_w, g_m, g_v, g42, w_l, m_l, v_l)


def _ssm_param_fn(lr, li, ldt, br, bi):
    dt = jnp.exp(ldt)
    mag = jnp.exp(lr * dt)
    ang = li * dt
    lbr = mag * jnp.cos(ang)
    lbi = mag * jnp.sin(ang)
    nr = lbr - 1.0
    den = lr * lr + li * li
    cr = (nr * lr + lbi * li) / den
    ci = (lbi * lr - nr * li) / den
    return lbr, lbi, cr * br - ci * bi, cr * bi + ci * br


def _ssm_prep(lr, li, ldt, br, bi):
    def body(lr_ref, li_ref, ldt_ref, br_ref, bi_ref, lbr_o, lbi_o, bbr_o, bbi_o):
        lbr, lbi, bbr, bbi = _ssm_param_fn(lr_ref[...], li_ref[...], ldt_ref[...], br_ref[...], bi_ref[...])
        lbr_o[...] = lbr
        lbi_o[...] = lbi
        bbr_o[...] = bbr
        bbi_o[...] = bbi

    row = jax.ShapeDtypeStruct(lr.shape, F32)
    mat = jax.ShapeDtypeStruct(br.shape, F32)
    return pl.pallas_call(
        body, name="ssm_prep", in_specs=[VMEM] * 5, out_specs=[VMEM] * 4,
        out_shape=[row, row, mat, mat], compiler_params=_params(0),
    )(lr, li, ldt, br, bi)


def _ssm_param_bwd(lr, li, ldt, br, bi, dlam8, dbbr, dbbi, dd8):
    nv = dlam8.shape[1]

    def body(lr_ref, li_ref, ldt_ref, br_ref, bi_ref, dl_ref, dbr_ref, dbi_ref, dd_ref,
             glr_o, gli_o, gldt_o, gbr_o, gbi_o, gd_o):
        halves_r, halves_i, halves_d = [], [], []
        for e in range(2):
            ar = dl_ref[0, e:e + 1, :]
            ai = dl_ref[1, e:e + 1, :]
            ad = dd_ref[e:e + 1, :]
            for b in range(1, nv // 2):
                ar = ar + dl_ref[0, 2 * b + e:2 * b + e + 1, :]
                ai = ai + dl_ref[1, 2 * b + e:2 * b + e + 1, :]
                ad = ad + dd_ref[2 * b + e:2 * b + e + 1, :]
            halves_r.append(ar)
            halves_i.append(ai)
            halves_d.append(ad)
        dlbr = jnp.concatenate(halves_r, axis=1)
        dlbi = jnp.concatenate(halves_i, axis=1)
        gd_o[...] = jnp.concatenate(halves_d, axis=1)
        _, vjp = jax.vjp(_ssm_param_fn, lr_ref[...], li_ref[...], ldt_ref[...], br_ref[...], bi_ref[...])
        glr, gli, gldt, gbr, gbi = vjp((dlbr, dlbi, dbr_ref[...], dbi_ref[...]))
        glr_o[...] = glr
        gli_o[...] = gli
        gldt_o[...] = gldt
        gbr_o[...] = gbr
        gbi_o[...] = gbi

    row = jax.ShapeDtypeStruct(lr.shape, F32)
    mat = jax.ShapeDtypeStruct(br.shape, F32)
    return pl.pallas_call(
        body, name="ssm_param_bwd", in_specs=[VMEM] * 9, out_specs=[VMEM] * 6,
        out_shape=[row, row, row, mat, mat, jax.ShapeDtypeStruct((1, SSM_W), F32)],
        compiler_params=_params(0),
    )(lr, li, ldt, br, bi, dlam8, dbbr, dbbi, dd8)


def _blockdiag(m):
    _, g, a, b = m.shape
    eye = jnp.eye(g, dtype=m.dtype)
    return jnp.einsum("egab,gk->egakb", m, eye).reshape(2, g * a, g * b)


def _blockdiag_take(t, a, b):
    return jnp.einsum("gagb->gab", t.reshape(GRP // 2, a, GRP // 2, b))


def _mixer_in_fwd(x, sh1, sc1, g_mix, w_in_b, comm=None):
    bsz, seq, _ = x.shape
    tt = min(seq, TT_MIX)

    def body(x_ref, sh_ref, sc_ref, g_ref, w_ref, u_ref, p_ref):
        xhat, _ = _rms(x_ref[0])
        h = xhat * g_ref[...] * (1.0 + sc_ref[0]) + sh_ref[0]
        z = _dot(h.astype(BF16), w_ref[...])
        u_ref[...] = z[:, :SSM_W].astype(BF16)
        p_ref[0] = z[:, SSM_W:]

    row = pl.BlockSpec((1, 1, D), lambda b, t: (b, 0, 0))
    return _fused_call(
        body, name="mixer_in_fwd", grid=(bsz, seq // tt),
        in_specs=[pl.BlockSpec((1, tt, D), lambda b, t: (b, t, 0)), row, row,
                  pl.BlockSpec((1, D), lambda b, t: (0, 0)), VMEM],
        out_specs=[pl.BlockSpec((tt, SSM_W), lambda b, t: (t, b)),
                   pl.BlockSpec((1, tt, POOL_W), lambda b, t: (b, t, 0))],
        out_shape=[jax.ShapeDtypeStruct((seq, bsz * SSM_W), BF16), jax.ShapeDtypeStruct((bsz, seq, POOL_W), F32)],
        scratch_shapes=[], args=(x, sh1, sc1, g_mix, w_in_b), comm=_schedule(comm, bsz * (seq // tt)))


def _ssm_project_in(ub, par0, bb_ref, s_re, s_im, row0, tlen, nv):
    for part, sref in ((0, s_re), (1, s_im)):
        for k in range(HALF_ST // 512):
            c0 = part * HALF_ST + k * 512
            a0 = _dot(ub, bb_ref[:, c0:c0 + 512])
            a1 = _dot(ub, bb_ref[:, 2 * HALF_ST + c0:2 * HALF_ST + c0 + 512])
            sref[pl.ds(row0, tlen), :, k * 512:(k + 1) * 512] = jnp.where(par0, a0, a1).reshape(tlen, nv, 512)


def _ssm_fwd(u2r, bb, cc, lam8, d8, tlen, comm=None):
    nv = lam8.shape[1]
    rows = nv * tlen
    n_chunks = u2r.shape[0] // rows

    def body(u_ref, bb_ref, cc_ref, lam_ref, d_ref, y_ref, xc_ref, xre_ref, xim_ref, s_re, s_im, st):
        @pl.when(pl.program_id(0) == 0)
        def _():
            st[...] = jnp.zeros_like(st)

        xc_ref[0] = st[...]
        ub = u_ref[...]
        u = ub.astype(F32)
        par0 = (lax.broadcasted_iota(jnp.int32, (rows, 1), 0) % 2) == 0
        _ssm_project_in(ub, par0, bb_ref, s_re, s_im, 0, tlen, nv)
        for hb in range(HALF_ST // 512):
            ls = slice(hb * 512, (hb + 1) * 512)
            lr = lam_ref[0, :, ls]
            li = lam_ref[1, :, ls]

            def step(t, carry, ls=ls, lr=lr, li=li):
                xr, xi = carry
                nr = lr * xr - li * xi + s_re[t, :, ls]
                ni = lr * xi + li * xr + s_im[t, :, ls]
                s_re[t, :, ls] = nr
                s_im[t, :, ls] = ni
                return nr, ni

            xr, xi = lax.fori_loop(0, tlen, step, (st[0, :, ls], st[1, :, ls]), unroll=8)
            st[0, :, ls] = xr
            st[1, :, ls] = xi
        xre = s_re[...].reshape(rows, HALF_ST).astype(BF16)
        xim = s_im[...].reshape(rows, HALF_ST).astype(BF16)
        xre_ref[...] = xre
        xim_ref[...] = xim
        y2 = _dot(xre, cc_ref[0:HALF_ST, :]) + _dot(xim, cc_ref[HALF_ST:, :])
        y = jnp.where(par0, y2[:, :HALF_CH], y2[:, HALF_CH:])
        skip = (u.reshape(tlen, nv, HALF_CH) * d_ref[...][None]).reshape(rows, HALF_CH)
        y_ref[...] = (y + skip).astype(BF16)

    st_blk = pl.BlockSpec((rows, HALF_ST), lambda c: (c, 0))
    st_shape = jax.ShapeDtypeStruct((u2r.shape[0], HALF_ST), BF16)
    return _fused_call(
        body, name="ssm_fwd", grid=(n_chunks,),
        in_specs=[pl.BlockSpec((rows, HALF_CH), lambda c: (c, 0)), VMEM, VMEM, VMEM, VMEM],
        out_specs=[pl.BlockSpec((rows, HALF_CH), lambda c: (c, 0)),
                   pl.BlockSpec((1, 2, nv, HALF_ST), lambda c: (c, 0, 0, 0)), st_blk, st_blk],
        out_shape=[jax.ShapeDtypeStruct(u2r.shape, BF16), jax.ShapeDtypeStruct((n_chunks, 2, nv, HALF_ST), F32),
                   st_shape, st_shape],
        scratch_shapes=[pltpu.VMEM((tlen, nv, HALF_ST), F32), pltpu.VMEM((tlen, nv, HALF_ST), F32),
                        pltpu.VMEM((2, nv, HALF_ST), F32)],
        args=(u2r, bb, cc, lam8, d8), comm=_schedule(comm, n_chunks))


def _pool_forward(ext, pv, pos, wp_ref, bp_ref):
    cur = ext
    zs, zls = [], []
    for gi, w in enumerate(POOL_WINDOWS):
        cur = cur + pltpu.roll(cur, w // 2, 0)
        sw = cur[POOL_HALO:, 0:128]
        z = sw / jnp.minimum(pos, float(w)) - pv[:, gi * 128:(gi + 1) * 128]
        zs.append(z)
        zls.append(_dot(z.astype(BF16), wp_ref[gi]) + bp_ref[:, gi * 128:(gi + 1) * 128])
        if gi + 1 < len(POOL_WINDOWS):
            cur = cur[:, 128:]
    return zs, zls


def _mixer_out_fwd(y2, p, x, gt1, w_glu_b, b_glu, w_pool_b, b_pool, pscale, w_out_b):
    bsz, seq, _ = x.shape
    tt = min(seq, TT_MIX)

    def body(y_ref, p_ref, x_ref, gt_ref, wg_ref, bg_ref, wp_ref, bp_ref, ps_ref, wo_ref, x1_ref, mix_ref, mxd_ref,
             ext):
        ti = pl.program_id(1)

        @pl.when(ti == 0)
        def _():
            ext[0:POOL_HALO, :] = jnp.zeros((POOL_HALO, POOL_W), F32)

        pv = p_ref[0]
        ext[POOL_HALO:, :] = pv
        pos = (ti * tt + lax.broadcasted_iota(jnp.int32, (tt, 1), 0) + 1).astype(F32)
        _, zls = _pool_forward(ext[...], pv, pos, wp_ref, bp_ref)
        ext[0:POOL_HALO, :] = pv[tt - POOL_HALO:, :]
        a = _gelu(y_ref[...].astype(F32))
        gl = _dot(a.astype(BF16), wg_ref[...]) + bg_ref[...]
        y_ssm = gl[:, :SSM_W] * jax.nn.sigmoid(gl[:, SSM_W:])
        y_pool = [zl * ps_ref[:, gi * 128:(gi + 1) * 128] for gi, zl in enumerate(zls)]
        mixcat = jnp.concatenate([y_ssm] + y_pool, axis=1).astype(BF16)
        mix_ref[0] = mixcat
        mixed = _dot(mixcat, wo_ref[...])
        mxd_ref[0] = mixed.astype(BF16)
        x1_ref[0] = x_ref[0] + gt_ref[0] * mixed

    xt = pl.BlockSpec((1, tt, D), lambda b, t: (b, t, 0))
    return pl.pallas_call(
        body, name="mixer_out_fwd", grid=(bsz, seq // tt),
        in_specs=[pl.BlockSpec((tt, SSM_W), lambda b, t: (t, b)),
                  pl.BlockSpec((1, tt, POOL_W), lambda b, t: (b, t, 0)), xt,
                  pl.BlockSpec((1, 1, D), lambda b, t: (b, 0, 0)), VMEM, VMEM, VMEM, VMEM, VMEM, VMEM],
        out_specs=[xt, xt, xt],
        out_shape=[jax.ShapeDtypeStruct(x.shape, F32), jax.ShapeDtypeStruct(x.shape, BF16),
                   jax.ShapeDtypeStruct(x.shape, BF16)],
        scratch_shapes=[pltpu.VMEM((POOL_HALO + tt, POOL_W), F32)],
        compiler_params=_params(2),
    )(y2, p, x, gt1, w_glu_b, b_glu, w_pool_b, b_pool, pscale, w_out_b)


def _conv_gate(g, ge, wc, bc):
    g1 = pltpu.roll(ge, 1, 0)[CONV_HALO:]
    g2 = pltpu.roll(ge, 2, 0)[CONV_HALO:]
    return wc[2:3] * g + wc[1:2] * g1 + wc[0:1] * g2 + bc, g1, g2


def _ffn_fwd(x1, tgt, sh2, sc2, gt2, g_ffn, w_up_b, w_conv, b_conv, w_down_b, g_fin):
    bsz, seq, _ = x1.shape
    tt = min(seq, TT_FFN)
    n_t = seq // tt
    n_ck = DFF // FF_CH

    def body(x1_ref, tg_ref, sh_ref, sc_ref, gt_ref, gf_ref, wu_ref, wc_ref, bc_ref, wd_ref, gfin_ref,
             h2_ref, v_ref, g_ref, gc_ref, act_ref, ddn_ref, dx2_ref, loss_ref, dgfin_ref, dgt_ref, gext, lacc):
        b = pl.program_id(0)
        ti = pl.program_id(1)

        @pl.when((b == 0) & (ti == 0))
        def _():
            lacc[...] = jnp.zeros_like(lacc)
            dgfin_ref[...] = jnp.zeros_like(dgfin_ref)

        @pl.when(ti == 0)
        def _():
            dgt_ref[...] = jnp.zeros_like(dgt_ref)
            gext[:, 0:CONV_HALO, :] = jnp.zeros((n_ck, CONV_HALO, FF_CH), F32)

        x1v = x1_ref[0]
        xhat, _ = _rms(x1v)
        h2b = (xhat * gf_ref[...] * (1.0 + sc_ref[0]) + sh_ref[0]).astype(BF16)
        h2_ref[0] = h2b
        dn = jnp.zeros((tt, D), F32)
        for ck in range(n_ck):
            c0 = ck * FF_CH
            v = _dot_nt(h2b, wu_ref[c0:c0 + FF_CH, :])
            g = _dot_nt(h2b, wu_ref[DFF + c0:DFF + c0 + FF_CH, :])
            v_ref[0, :, c0:c0 + FF_CH] = v.astype(BF16)
            g_ref[0, :, c0:c0 + FF_CH] = g.astype(BF16)
            gext[ck, CONV_HALO:, :] = g
            gc, _, _ = _conv_gate(g, gext[ck], wc_ref[:, c0:c0 + FF_CH], bc_ref[:, c0:c0 + FF_CH])
            gext[ck, 0:CONV_HALO, :] = g[tt - CONV_HALO:, :]
            gc_ref[0, :, c0:c0 + FF_CH] = gc.astype(BF16)
            actb = (gc * jax.nn.sigmoid(gc) * v).astype(BF16)
            act_ref[0, :, c0:c0 + FF_CH] = actb
            dn = dn + _dot(actb, wd_ref[c0:c0 + FF_CH, :])
        gt = gt_ref[0]
        xh3, r3 = _rms(x1v + gt * dn)
        gfin = gfin_ref[...]
        diff = xh3 * gfin - tg_ref[0]
        lacc[...] += _colsum(diff * diff)
        dy = diff * (1.0 / D)
        dgfin_ref[...] += _colsum(dy * xh3)
        dx2 = _rms_bwd(dy * gfin, xh3, r3)
        dx2_ref[0] = dx2
        dgt_ref[0] += _colsum(dx2 * dn)
        ddn_ref[0] = (gt * dx2).astype(BF16)

        @pl.when((b == bsz - 1) & (ti == n_t - 1))
        def _():
            loss_ref[...] = jnp.full(loss_ref.shape, 0.5 / D * jnp.sum(lacc[...]), F32)

    xt = pl.BlockSpec((1, tt, D), lambda b, t: (b, t, 0))
    ft = pl.BlockSpec((1, tt, DFF), lambda b, t: (b, t, 0))
    row = pl.BlockSpec((1, 1, D), lambda b, t: (b, 0, 0))
    vec = pl.BlockSpec((1, D), lambda b, t: (0, 0))
    ff = jax.ShapeDtypeStruct((bsz, seq, DFF), BF16)
    xs = jax.ShapeDtypeStruct((bsz, seq, D), BF16)
    return pl.pallas_call(
        body, name="ffn_fwd", grid=(bsz, n_t),
        in_specs=[xt, xt, row, row, row, vec, VMEM, VMEM, VMEM, VMEM, vec],
        out_specs=[xt, ft, ft, ft, ft, xt, xt, pl.BlockSpec((1, 128), lambda b, t: (0, 0)), vec, row],
        out_shape=[xs, ff, ff, ff, ff, xs, jax.ShapeDtypeStruct((bsz, seq, D), F32),
                   jax.ShapeDtypeStruct((1, 128), F32), jax.ShapeDtypeStruct((1, D), F32),
                   jax.ShapeDtypeStruct((bsz, 1, D), F32)],
        scratch_shapes=[pltpu.VMEM((n_ck, CONV_HALO + tt, FF_CH), F32), pltpu.VMEM((1, D), F32)],
        compiler_params=_params(2),
    )(x1, tgt, sh2, sc2, gt2, g_ffn, w_up_b, w_conv, b_conv, w_down_b, g_fin)


def _ffn_bwd(ddn, gq, gcq, vq, x1, dx2, sh2, sc2, g_ffn, w_conv, w_down_b, w_up_b):
    bsz, seq, _ = x1.shape
    tt = min(seq, TT_FFN)
    n_t = seq // tt
    n_ck = DFF // FF_CH
    ext_rows = tt + CONV_HALO

    def body(ddn_ref, g_ref, gc_ref, v_ref, x1_ref, dx2_ref, sh_ref, sc_ref, gf_ref, wc_ref, wd_ref,
             wu_ref, dup_ref, dx1_ref, dsh_ref, dsc_ref, dgf_ref, dwc_ref, dbc_ref, dext):
        b = pl.program_id(0)
        i = pl.program_id(1)

        @pl.when((b == 0) & (i == 0))
        def _():
            dgf_ref[...] = jnp.zeros_like(dgf_ref)
            dwc_ref[...] = jnp.zeros_like(dwc_ref)
            dbc_ref[...] = jnp.zeros_like(dbc_ref)

        @pl.when(i == 0)
        def _():
            dsh_ref[...] = jnp.zeros_like(dsh_ref)
            dsc_ref[...] = jnp.zeros_like(dsc_ref)
            dext[:, tt:, :] = jnp.zeros((n_ck, CONV_HALO, FF_CH), F32)

        ddnv = ddn_ref[0]
        dh2 = jnp.zeros((tt, D), F32)
        for ck in range(n_ck):
            c0 = ck * FF_CH
            dact = _dot_nt(ddnv, wd_ref[c0:c0 + FF_CH, :])
            g = g_ref[0, :, c0:c0 + FF_CH].astype(F32)
            gc = gc_ref[0, :, c0:c0 + FF_CH].astype(F32)
            v = v_ref[0, :, c0:c0 + FF_CH].astype(F32)
            wc = wc_ref[:, c0:c0 + FF_CH]
            sg = jax.nn.sigmoid(gc)
            silu = gc * sg
            dv = dact * silu
            dgc = dact * v * (sg + silu * (1.0 - sg))
            dext[ck, 0:tt, :] = dgc
            de = dext[ck]
            d1 = pltpu.roll(de, ext_rows - 1, 0)[0:tt]
            d2 = pltpu.roll(de, ext_rows - 2, 0)[0:tt]
            dext[ck, tt:, :] = dgc[0:CONV_HALO, :]
            dbc_ref[:, c0:c0 + FF_CH] += _colsum(dgc)
            dwc_ref[0:1, c0:c0 + FF_CH] += _colsum(d2 * g)
            dwc_ref[1:2, c0:c0 + FF_CH] += _colsum(d1 * g)
            dwc_ref[2:3, c0:c0 + FF_CH] += _colsum(dgc * g)
            dg = wc[2:3] * dgc + wc[1:2] * d1 + wc[0:1] * d2
            dvb = dv.astype(BF16)
            dgb = dg.astype(BF16)
            dup_ref[0, :, c0:c0 + FF_CH] = dvb
            dup_ref[0, :, DFF + c0:DFF + c0 + FF_CH] = dgb
            dh2 = dh2 + _dot(dvb, wu_ref[c0:c0 + FF_CH, :]) + _dot(dgb, wu_ref[DFF + c0:DFF + c0 + FF_CH, :])
        xhat, rstd = _rms(x1_ref[0])
        gf = gf_ref[...]
        dsh_ref[0] += _colsum(dh2)
        dsc_ref[0] += _colsum(dh2 * xhat * gf)
        t = dh2 * (1.0 + sc_ref[0])
        dgf_ref[...] += _colsum(t * xhat)
        dx1_ref[0] = dx2_ref[0] + _rms_bwd(t * gf, xhat, rstd)

    def rev(b, t):
        return (b, n_t - 1 - t, 0)

    xt = pl.BlockSpec((1, tt, D), rev)
    ft = pl.BlockSpec((1, tt, DFF), rev)
    row = pl.BlockSpec((1, 1, D), lambda b, t: (b, 0, 0))
    vec = pl.BlockSpec((1, D), lambda b, t: (0, 0))
    rows = jax.ShapeDtypeStruct((bsz, 1, D), F32)
    return pl.pallas_call(
        body, name="ffn_bwd", grid=(bsz, n_t),
        in_specs=[xt, ft, ft, ft, xt, xt, row, row, vec, VMEM, VMEM, VMEM],
        out_specs=[pl.BlockSpec((1, tt, 2 * DFF), rev), xt, row, row, vec,
                   pl.BlockSpec((3, DFF), lambda b, t: (0, 0)), pl.BlockSpec((1, DFF), lambda b, t: (0, 0))],
        out_shape=[jax.ShapeDtypeStruct((bsz, seq, 2 * DFF), BF16), jax.ShapeDtypeStruct((bsz, seq, D), F32),
                   rows, rows, jax.ShapeDtypeStruct((1, D), F32), jax.ShapeDtypeStruct((3, DFF), F32),
                   jax.ShapeDtypeStruct((1, DFF), F32)],
        scratch_shapes=[pltpu.VMEM((n_ck, ext_rows, FF_CH), F32)],
        compiler_params=_params(2),
    )(ddn, gq, gcq, vq, x1, dx2, sh2, sc2, g_ffn, w_conv, w_down_b, w_up_b)


def _wgrad(a, b, bk1, bk2, name):
    n, k1 = a.shape
    _, k2 = b.shape
    tt = min(n, TT_WGRAD)

    def body(a_ref, b_ref, o_ref):
        @pl.when(pl.program_id(2) == 0)
        def _():
            o_ref[...] = jnp.zeros_like(o_ref)

        o_ref[...] += _dot_tn(a_ref[...], b_ref[...])

    return pl.pallas_call(
        body, name=name, grid=(k1 // bk1, k2 // bk2, n // tt),
        in_specs=[pl.BlockSpec((tt, bk1), lambda h, j, i: (i, h)), pl.BlockSpec((tt, bk2), lambda h, j, i: (i, j))],
        out_specs=pl.BlockSpec((bk1, bk2), lambda h, j, i: (h, j)),
        out_shape=jax.ShapeDtypeStruct((k1, k2), F32), compiler_params=_params(3),
    )(a, b)


def _mixer_out_bwd(dx1, mixcat, mixed, y2, p, gt1, w_glu_b, b_glu, w_pool_b, b_pool, pscale, w_out_b, comm=None):
    bsz, seq, _ = dx1.shape
    tt = min(seq, TT_MIX)
    n_t = seq // tt
    ext_rows = tt + POOL_HALO

    def body(dx1_ref, mc_ref, mxd_ref, y_ref, p_ref, ph_ref, gt_ref, wg_ref, bg_ref, wp_ref, bp_ref, ps_ref, wo_ref,
             dy_ref, dp_ref, dwo_ref, dwg_ref, dbg_ref, dwp_ref, dbp_ref, dps_ref, dgt_ref, ext, qext):
        b = pl.program_id(0)
        i = pl.program_id(1)
        tile = n_t - 1 - i

        @pl.when((b == 0) & (i == 0))
        def _():
            for r in (dwo_ref, dwg_ref, dbg_ref, dwp_ref, dbp_ref, dps_ref):
                r[...] = jnp.zeros_like(r)

        @pl.when(i == 0)
        def _():
            dgt_ref[...] = jnp.zeros_like(dgt_ref)
            qext[tt:, :] = jnp.zeros((POOL_HALO, POOL_W), F32)

        dx1v = dx1_ref[0]
        mc = mc_ref[0]
        dgt_ref[0] += _colsum(dx1v * mxd_ref[0].astype(F32))
        dmixed = (gt_ref[0] * dx1v).astype(BF16)
        dwo_ref[...] += _dot_tn(mc, dmixed)
        dmc = _dot_nt(dmixed, wo_ref[...])
        pv = p_ref[0]
        ext[0:POOL_HALO, :] = ph_ref[0] * (tile > 0).astype(F32)
        ext[POOL_HALO:, :] = pv
        pos = (tile * tt + lax.broadcasted_iota(jnp.int32, (tt, 1), 0) + 1).astype(F32)
        zs, zls = _pool_forward(ext[...], pv, pos, wp_ref, bp_ref)
        dzs = []
        for gi, w in enumerate(POOL_WINDOWS):
            cs = slice(gi * 128, (gi + 1) * 128)
            dyp = dmc[:, SSM_W + gi * 128:SSM_W + (gi + 1) * 128]
            dps_ref[:, cs] += _colsum(dyp * zls[gi])
            dzl = dyp * ps_ref[:, cs]
            dbp_ref[:, cs] += _colsum(dzl)
            dzlb = dzl.astype(BF16)
            dwp_ref[gi] += _dot_tn(zs[gi].astype(BF16), dzlb)
            dz = _dot_nt(dzlb, wp_ref[gi])
            dzs.append(dz)
            qext[0:tt, cs] = dz / jnp.minimum(pos, float(w))
        cur = qext[...]
        dps = []
        for gi, w in enumerate(POOL_WINDOWS):
            cur = cur + pltpu.roll(cur, ext_rows - w // 2, 0)
            dps.append(cur[0:tt, 0:128] - dzs[gi])
            if gi + 1 < len(POOL_WINDOWS):
                cur = cur[:, 128:]
        qhead = qext[0:POOL_HALO, :]
        qext[tt:, :] = qhead
        dp_ref[0] = jnp.concatenate(dps, axis=1)
        yv = y_ref[...].astype(F32)
        ab = _gelu(yv).astype(BF16)
        gl = _dot(ab, wg_ref[...]) + bg_ref[...]
        val = gl[:, :SSM_W]
        sg = jax.nn.sigmoid(gl[:, SSM_W:])
        dys = dmc[:, :SSM_W]
        dgl = jnp.concatenate([dys * sg, dys * val * sg * (1.0 - sg)], axis=1)
        dbg_ref[...] += _colsum(dgl)
        dglb = dgl.astype(BF16)
        dwg_ref[...] += _dot_tn(ab, dglb)
        dy_ref[...] = (_dot_nt(dglb, wg_ref[...]) * _gelu_grad(yv)).astype(BF16)

    def rev(b, t):
        return (b, n_t - 1 - t, 0)

    def halo(b, t):
        return (b, jnp.maximum((n_t - 1 - t) * (tt // POOL_HALO) - 1, 0), 0)

    xt = pl.BlockSpec((1, tt, D), rev)
    pt = pl.BlockSpec((1, tt, POOL_W), rev)
    yt = pl.BlockSpec((tt, SSM_W), lambda b, t: (n_t - 1 - t, b))

    def whole(shape):
        return pl.BlockSpec(shape, lambda b, t: (0,) * len(shape))

    return _fused_call(
        body, name="mixer_out_bwd", grid=(bsz, n_t),
        in_specs=[xt, xt, xt, yt, pt, pl.BlockSpec((1, POOL_HALO, POOL_W), halo),
                  pl.BlockSpec((1, 1, D), lambda b, t: (b, 0, 0)), VMEM, VMEM, VMEM, VMEM, VMEM, VMEM],
        out_specs=[yt, pt, whole((D, D)), whole((SSM_W, 2 * SSM_W)), whole((1, 2 * SSM_W)),
                   whole((4, 128, 128)), whole((1, POOL_W)), whole((1, POOL_W)),
                   pl.BlockSpec((1, 1, D), lambda b, t: (b, 0, 0))],
        out_shape=[jax.ShapeDtypeStruct(y2.shape, BF16), jax.ShapeDtypeStruct(p.shape, F32),
                   jax.ShapeDtypeStruct((D, D), F32), jax.ShapeDtypeStruct((SSM_W, 2 * SSM_W), F32),
                   jax.ShapeDtypeStruct((1, 2 * SSM_W), F32), jax.ShapeDtypeStruct((4, 128, 128), F32),
                   jax.ShapeDtypeStruct((1, POOL_W), F32), jax.ShapeDtypeStruct((1, POOL_W), F32),
                   jax.ShapeDtypeStruct((bsz, 1, D), F32)],
        scratch_shapes=[pltpu.VMEM((POOL_HALO + tt, POOL_W), F32), pltpu.VMEM((ext_rows, POOL_W), F32)],
        args=(dx1, mixcat, mixed, y2, p, p, gt1, w_glu_b, b_glu, w_pool_b, b_pool, pscale, w_out_b),
        comm=_schedule(comm, bsz * n_t))


def _ssm_bwd(dy2r, u2r, xc, xs_re, xs_im, bb, cc, lam8, d8, tlen, comm=None):
    nv = lam8.shape[1]
    rows = nv * tlen
    n_chunks = u2r.shape[0] // rows

    def body(dy_ref, u_ref, xc_ref, xre_ref, xim_ref, bb_ref, cc_ref, lam_ref, d_ref,
             du_ref, dcc_ref, dbb_ref, dlam_ref, dd_ref, s_re, s_im, g_re, g_im, gst):
        i = pl.program_id(0)

        @pl.when(i == 0)
        def _():
            for r in (gst, dcc_ref, dbb_ref, dlam_ref, dd_ref):
                r[...] = jnp.zeros_like(r)

        u = u_ref[...].astype(F32)
        dy = dy_ref[...].astype(F32)
        par0 = (lax.broadcasted_iota(jnp.int32, (rows, 1), 0) % 2) == 0
        xre = xre_ref[...]
        xim = xim_ref[...]
        s_re[0] = xc_ref[0, 0]
        s_im[0] = xc_ref[0, 1]
        s_re[pl.ds(1, tlen)] = xre.astype(F32).reshape(tlen, nv, HALF_ST)
        s_im[pl.ds(1, tlen)] = xim.astype(F32).reshape(tlen, nv, HALF_ST)
        zero = jnp.zeros_like(dy)
        dy2 = jnp.concatenate([jnp.where(par0, dy, zero), jnp.where(par0, zero, dy)], axis=1).astype(BF16)
        u2 = jnp.concatenate([jnp.where(par0, u, zero), jnp.where(par0, zero, u)], axis=1).astype(BF16)
        dcc_ref[0:HALF_ST, :] += _dot_tn(xre, dy2)
        dcc_ref[HALF_ST:, :] += _dot_tn(xim, dy2)
        for part, gref in ((0, g_re), (1, g_im)):
            for k in range(HALF_ST // 512):
                r0 = part * HALF_ST + k * 512
                gref[:, :, k * 512:(k + 1) * 512] = _dot_nt(dy2, cc_ref[r0:r0 + 512, :]).reshape(tlen, nv, 512)
        for hb in range(HALF_ST // 512):
            ls = slice(hb * 512, (hb + 1) * 512)
            lr = lam_ref[0, :, ls]
            li = lam_ref[1, :, ls]

            def bstep(k, carry, ls=ls, lr=lr, li=li):
                t = tlen - 1 - k
                gr, gi, ar, ai = carry
                ngr = g_re[t, :, ls] + lr * gr + li * gi
                ngi = g_im[t, :, ls] + lr * gi - li * gr
                g_re[t, :, ls] = ngr
                g_im[t, :, ls] = ngi
                xpr = s_re[t, :, ls]
                xpi = s_im[t, :, ls]
                return ngr, ngi, ar + ngr * xpr + ngi * xpi, ai + ngi * xpr - ngr * xpi

            init = (gst[0, :, ls], gst[1, :, ls], dlam_ref[0, :, ls], dlam_ref[1, :, ls])
            gr, gi, ar, ai = lax.fori_loop(0, tlen, bstep, init, unroll=4)
            gst[0, :, ls] = gr
            gst[1, :, ls] = gi
            dlam_ref[0, :, ls] = ar
            dlam_ref[1, :, ls] = ai
        gre = g_re[...].reshape(rows, HALF_ST).astype(BF16)
        gim = g_im[...].reshape(rows, HALF_ST).astype(BF16)
        du0 = _dot_nt(gre, bb_ref[:, 0:HALF_ST]) + _dot_nt(gim, bb_ref[:, HALF_ST:2 * HALF_ST])
        du1 = _dot_nt(gre, bb_ref[:, 2 * HALF_ST:3 * HALF_ST]) + _dot_nt(gim, bb_ref[:, 3 * HALF_ST:])
        skip = (dy.reshape(tlen, nv, HALF_CH) * d_ref[...][None]).reshape(rows, HALF_CH)
        du_ref[...] = (jnp.where(par0, du0, du1) + skip).astype(BF16)
        dbb_ref[:, 0:HALF_ST] += _dot_tn(u2, gre)
        dbb_ref[:, HALF_ST:] += _dot_tn(u2, gim)
        dd_ref[...] += jnp.sum((dy * u).reshape(tlen, nv, HALF_CH), axis=0)

        @pl.when(i == n_chunks - 1)
        def _():
            dcc_ref[HALF_ST:, :] = -dcc_ref[HALF_ST:, :]

    def rev(c):
        return (n_chunks - 1 - c, 0)

    def whole(shape):
        return pl.BlockSpec(shape, lambda c: (0,) * len(shape))

    blk = pl.BlockSpec((rows, HALF_CH), rev)
    st_blk = pl.BlockSpec((rows, HALF_ST), rev)
    return _fused_call(
        body, name="ssm_bwd", grid=(n_chunks,),
        in_specs=[blk, blk, pl.BlockSpec((1, 2, nv, HALF_ST), lambda c: (n_chunks - 1 - c, 0, 0, 0)),
                  st_blk, st_blk, VMEM, VMEM, VMEM, VMEM],
        out_specs=[blk, whole((2 * HALF_ST, SSM_W)), whole((SSM_W, 2 * HALF_ST)), whole((2, nv, HALF_ST)),
                   whole((nv, HALF_CH))],
        out_shape=[jax.ShapeDtypeStruct(u2r.shape, BF16), jax.ShapeDtypeStruct((2 * HALF_ST, SSM_W), F32),
                   jax.ShapeDtypeStruct((SSM_W, 2 * HALF_ST), F32), jax.ShapeDtypeStruct((2, nv, HALF_ST), F32),
                   jax.ShapeDtypeStruct((nv, HALF_CH), F32)],
        scratch_shapes=[pltpu.VMEM((tlen + 1, nv, HALF_ST), F32), pltpu.VMEM((tlen + 1, nv, HALF_ST), F32),
                        pltpu.VMEM((tlen, nv, HALF_ST), F32), pltpu.VMEM((tlen, nv, HALF_ST), F32),
                        pltpu.VMEM((2, nv, HALF_ST), F32)],
        args=(dy2r, u2r, xc, xs_re, xs_im, bb, cc, lam8, d8), comm=_schedule(comm, n_chunks))


def _mixer_in_bwd(du2, dp, x, dx1, sh1, sc1, g_mix, w_in_b, b0=0, nb=None, prev=None, comm=None):
    bsz, seq, _ = x.shape
    nb = bsz if nb is None else nb
    tt = min(seq, TT_MIX)
    n_prev = 0 if prev is None else 5

    def body(du_ref, dp_ref, x_ref, dx1_ref, sh_ref, sc_ref, g_ref, w_ref, *rest):
        prev_refs, (dx_ref, dw_ref, dsh_ref, dsc_ref, dg_ref) = rest[:n_prev], rest[n_prev:]
        b = pl.program_id(0)
        ti = pl.program_id(1)

        @pl.when((b == 0) & (ti == 0))
        def _():
            if prev is None:
                dw_ref[...] = jnp.zeros_like(dw_ref)
                dg_ref[...] = jnp.zeros_like(dg_ref)
            else:
                dw_ref[...] = prev_refs[1][...]
                dg_ref[...] = prev_refs[4][...]

        @pl.when(ti == 0)
        def _():
            dsh_ref[...] = jnp.zeros_like(dsh_ref)
            dsc_ref[...] = jnp.zeros_like(dsc_ref)

        dz = jnp.concatenate([du_ref[...], dp_ref[0].astype(BF16)], axis=1)
        xhat, rstd = _rms(x_ref[0])
        g = g_ref[...]
        sc = sc_ref[0]
        a = xhat * g
        h = (a * (1.0 + sc) + sh_ref[0]).astype(BF16)
        dw_ref[...] += _dot_tn(h, dz)
        dh = _dot_nt(dz, w_ref[...])
        dsh_ref[0] += _colsum(dh)
        dsc_ref[0] += _colsum(dh * a)
        t = dh * (1.0 + sc)
        dg_ref[...] += _colsum(t * xhat)
        dx_ref[0] = dx1_ref[0] + _rms_bwd(t * g, xhat, rstd)

    xt = pl.BlockSpec((1, tt, D), lambda b, t: (b + b0, t, 0))
    row = pl.BlockSpec((1, 1, D), lambda b, t: (b + b0, 0, 0))
    vec = pl.BlockSpec((1, D), lambda b, t: (0, 0))
    mat = pl.BlockSpec((D, D), lambda b, t: (0, 0))
    rows = jax.ShapeDtypeStruct((bsz, 1, D), F32)
    n_in = 8
    prev_specs = [] if prev is None else [ANY, mat, ANY, ANY, vec]
    aliases = {} if prev is None else {n_in: 0, n_in + 2: 2, n_in + 3: 3}
    return _fused_call(
        body, name="mixer_in_bwd" if prev is None else "mixer_in_bwd_last", grid=(nb, seq // tt),
        in_specs=[pl.BlockSpec((tt, SSM_W), lambda b, t: (t, b + b0)),
                  pl.BlockSpec((1, tt, POOL_W), lambda b, t: (b + b0, t, 0)), xt, xt, row, row, vec, VMEM] + prev_specs,
        out_specs=[xt, mat, row, row, vec],
        out_shape=[jax.ShapeDtypeStruct(x.shape, F32), jax.ShapeDtypeStruct((D, D), F32), rows, rows,
                   jax.ShapeDtypeStruct((1, D), F32)],
        scratch_shapes=[], args=(du2, dp, x, dx1, sh1, sc1, g_mix, w_in_b) + tuple(prev or ()),
        comm=_schedule(comm, nb * (seq // tt)), aliases=aliases)


def kernel(x, c, w_ada, b_ada, g_norm_mix, w_in, ssm_lam_re, ssm_lam_im, ssm_log_dt, ssm_b_re, ssm_b_im, ssm_c_re, ssm_c_im, ssm_d, w_glu, b_glu, w_pool, b_pool, pool_scale, w_out, g_norm_ffn, w_up, w_conv, b_conv, w_down, g_norm_final, loss_target, m_w_ada, m_b_ada, m_g_norm_mix, m_w_in, m_ssm_lam_re, m_ssm_lam_im, m_ssm_log_dt, m_ssm_b_re, m_ssm_b_im, m_ssm_c_re, m_ssm_c_im, m_ssm_d, m_w_glu, m_b_glu, m_w_pool, m_b_pool, m_pool_scale, m_w_out, m_g_norm_ffn, m_w_up, m_w_conv, m_b_conv, m_w_down, m_g_norm_final, v_w_ada, v_b_ada, v_g_norm_mix, v_w_in, v_ssm_lam_re, v_ssm_lam_im, v_ssm_log_dt, v_ssm_b_re, v_ssm_b_im, v_ssm_c_re, v_ssm_c_im, v_ssm_d, v_w_glu, v_b_glu, v_w_pool, v_b_pool, v_pool_scale, v_w_out, v_g_norm_ffn, v_w_up, v_w_conv, v_b_conv, v_w_down, v_g_norm_final):
    bsz, seq, _ = x.shape
    assert 2 * bsz == 8 and seq % 128 == 0
    px, py, pc = _my_place()
    me = 4 * px + 2 * py + pc
    place = jnp.stack([pc, 2 * px + py]).astype(jnp.int32)
    ncol = ADA_COLS

    cpad = jnp.zeros((16, D), F32).at[0:bsz].set(c).at[8:11, 0:352].set(w_conv[0])
    cg, c_all, mod8, (g_in,) = _ada_fwd(cpad, w_ada[0], b_ada.reshape(N_DEV, 1, ncol), [w_in[0].astype(BF16)])
    w_conv_f = cg[:, 8:11, 0:352].transpose(1, 0, 2).reshape(3, DFF)
    w_in_b = g_in.reshape(D, D)
    sh1, sc1, gt1, sh2, sc2, gt2 = [mod8[0:bsz, k * D:(k + 1) * D].reshape(bsz, 1, D) for k in range(N_MOD)]

    lam_r = ssm_lam_re[0].reshape(1, GRP * NST)
    lam_i = ssm_lam_im[0].reshape(1, GRP * NST)
    ldt = jnp.repeat(ssm_log_dt[0], NST).reshape(1, GRP * NST)
    b_r = ssm_b_re[0].transpose(2, 0, 1).reshape(GCH, GRP * NST)
    b_i = ssm_b_im[0].transpose(2, 0, 1).reshape(GCH, GRP * NST)
    lbr, lbi, bbr, bbi = _ssm_prep(lam_r, lam_i, ldt, b_r, b_i)
    lam8 = jnp.stack([jnp.tile(lbr.reshape(2, HALF_ST), (bsz, 1)), jnp.tile(lbi.reshape(2, HALF_ST), (bsz, 1))])
    bd_r = _blockdiag(bbr.reshape(GCH, 2, GRP // 2, NST).transpose(1, 2, 0, 3))
    bd_i = _blockdiag(bbi.reshape(GCH, 2, GRP // 2, NST).transpose(1, 2, 0, 3))
    bb = jnp.concatenate([bd_r[0], bd_i[0], bd_r[1], bd_i[1]], axis=1).astype(BF16)
    cd_r = _blockdiag(ssm_c_re[0].reshape(2, GRP // 2, GCH, NST).transpose(0, 1, 3, 2))
    cd_i = _blockdiag(ssm_c_im[0].reshape(2, GRP // 2, GCH, NST).transpose(0, 1, 3, 2))
    cc = jnp.concatenate([jnp.concatenate([cd_r[0], cd_r[1]], axis=1),
                          jnp.concatenate([-cd_i[0], -cd_i[1]], axis=1)], axis=0).astype(BF16)
    d8 = jnp.tile(ssm_d[0].reshape(2, HALF_CH), (bsz, 1))

    tlen = min(seq, T_SSM)
    (u2, p), ((g_glu, g_out),) = _mixer_in_fwd(
        x, sh1, sc1, g_norm_mix, w_in_b, comm=[(_gather_plan, [w_glu[0].astype(BF16), w_out[0].astype(BF16)])])
    w_glu_b = g_glu.transpose(1, 0, 2).reshape(SSM_W, 2 * SSM_W)
    w_out_b = g_out.reshape(D, D)
    u2r = u2.reshape(seq * 2 * bsz, HALF_CH)
    (y2r, xc, xs_re, xs_im), ((g_up, g_down),) = _ssm_fwd(
        u2r, bb, cc, lam8, d8, tlen, comm=[(_gather_plan, [w_up[0].T.astype(BF16), w_down[0].astype(BF16)])])
    w_up_b = g_up.reshape(2 * DFF, D)
    w_down_b = g_down.reshape(DFF, D)
    y2 = y2r.reshape(seq, bsz * SSM_W)
    w_pool_b = w_pool[0].astype(BF16)
    bp = b_pool[0].reshape(1, POOL_W)
    x1, mixcat, mixed = _mixer_out_fwd(y2, p, x, gt1, w_glu_b, b_glu, w_pool_b, bp, pool_scale, w_out_b)
    h2, vq, gq, gcq, act, ddn, dx2, loss_l, dg_fin, dgt2 = _ffn_fwd(
        x1, loss_target, sh2, sc2, gt2, g_norm_ffn, w_up_b, w_conv_f, b_conv, w_down_b, g_norm_final.reshape(1, D))

    dup, dx1, dsh2, dsc2, dg_ffn, dw_conv, db_conv = _ffn_bwd(
        ddn, gq, gcq, vq, x1, dx2, sh2, sc2, g_norm_ffn, w_conv_f, w_down_b, w_up_b)
    ntok = bsz * seq
    dw_up_t = _wgrad(dup.reshape(ntok, 2 * DFF), h2.reshape(ntok, D), DFF // 2, D, "wgrad_up")
    dw_down = _wgrad(act.reshape(ntok, DFF), ddn.reshape(ntok, D), DFF, 512, "wgrad_down")
    g42_up = dw_up_t.reshape(4, 2, 704, D)
    g42_down = dw_down.reshape(4, 2, 352, D)
    (dy2, dp, dw_out, dw_glu, db_glu, dw_pool, db_pool, dpscale, dgt1), ((ra_up, ra_down),) = _mixer_out_bwd(
        dx1, mixcat, mixed, y2, p, gt1, w_glu_b, b_glu, w_pool_b, bp, pool_scale, w_out_b,
        comm=[(_pair_plan, [g42_up, g42_down])])
    own_up, s_up = _pair_sum(g42_up, ra_up, place, "pair_sum_up")
    own_down, s_down = _pair_sum(g42_down, ra_down, place, "pair_sum_down")
    g42_glu = dw_glu.reshape(SSM_W, N_DEV, 128).transpose(1, 0, 2).reshape(4, 2, SSM_W, 128)
    g42_out = dw_out.reshape(4, 2, 128, D)
    small_a = [
        ("b_glu", (1, 2 * SSM_W), db_glu), ("w_pool", (POOL_W, 128), dw_pool.reshape(POOL_W, 128)),
        ("b_pool", (4, 128), db_pool.reshape(4, 128)), ("pool_scale", (1, POOL_W), dpscale),
        ("g_norm_ffn", (1, D), dg_ffn), ("b_conv", (1, DFF), db_conv), ("g_norm_final", (1, D), dg_fin)]
    (du2r, dcc, dbb, dlam8, dd8), ((rc_up, rc_down), (ra_glu, ra_out), parts_a) = _ssm_bwd(
        dy2.reshape(u2r.shape), u2r, xc, xs_re, xs_im, bb, cc, lam8, d8, tlen,
        comm=[(_chip_plan, [s_up, s_down]), (_pair_plan, [g42_glu, g42_out]),
              (_gather_plan, [g for _, _, g in small_a] + [dw_conv, loss_l])])
    big_up = [t.T for t in _final_sum_adamw(own_up, rc_up, w_up[0].T, m_w_up[0].T, v_w_up[0].T, "final_adamw_up")]
    big_down = _final_sum_adamw(own_down, rc_down, w_down[0], m_w_down[0], v_w_down[0], "final_adamw_down")
    own_glu, s_glu = _pair_sum(g42_glu, ra_glu, place, "pair_sum_glu")
    own_out, s_out = _pair_sum(g42_out, ra_out, place, "pair_sum_out")

    def take_c(t):
        return _blockdiag_take(t, NST, GCH).transpose(0, 2, 1)

    dc_re = jnp.concatenate([take_c(dcc[0:HALF_ST, e * HALF_CH:(e + 1) * HALF_CH]) for e in range(2)], axis=0)
    dc_im = jnp.concatenate([take_c(dcc[HALF_ST:, e * HALF_CH:(e + 1) * HALF_CH]) for e in range(2)], axis=0)

    def take_b(t):
        return _blockdiag_take(t, GCH, NST).transpose(1, 0, 2)

    dbbr = jnp.concatenate([take_b(dbb[e * HALF_CH:(e + 1) * HALF_CH, 0:HALF_ST]) for e in range(2)], axis=1)
    dbbi = jnp.concatenate([take_b(dbb[e * HALF_CH:(e + 1) * HALF_CH, HALF_ST:]) for e in range(2)], axis=1)
    glr, gli, gldt, gbr, gbi, gd = _ssm_param_bwd(
        lam_r, lam_i, ldt, b_r, b_i, dlam8, dbbr.reshape(GCH, GRP * NST), dbbi.reshape(GCH, GRP * NST), dd8)
    g_log_dt = jnp.sum(gldt.reshape(GRP, NST), axis=1)

    def view(a, shp):
        return a.reshape(shp)

    small_b = [
        ("ssm_lam_re", (GRP, NST), glr.reshape(GRP, NST)), ("ssm_lam_im", (GRP, NST), gli.reshape(GRP, NST)),
        ("ssm_log_dt", (1, GRP), g_log_dt.reshape(1, GRP)),
        ("ssm_c_re", (GRP * GCH, NST), dc_re.reshape(GRP * GCH, NST)),
        ("ssm_c_im", (GRP * GCH, NST), dc_im.reshape(GRP * GCH, NST)), ("ssm_d", (1, SSM_W), gd)]
    small = small_a + small_b
    given = dict(
        ssm_lam_re=(ssm_lam_re, m_ssm_lam_re, v_ssm_lam_re), ssm_lam_im=(ssm_lam_im, m_ssm_lam_im, v_ssm_lam_im),
        ssm_log_dt=(ssm_log_dt, m_ssm_log_dt, v_ssm_log_dt), ssm_c_re=(ssm_c_re, m_ssm_c_re, v_ssm_c_re),
        ssm_c_im=(ssm_c_im, m_ssm_c_im, v_ssm_c_im), ssm_d=(ssm_d, m_ssm_d, v_ssm_d), b_glu=(b_glu, m_b_glu, v_b_glu),
        w_pool=(w_pool, m_w_pool, v_w_pool), b_pool=(b_pool, m_b_pool, v_b_pool),
        pool_scale=(pool_scale, m_pool_scale, v_pool_scale), g_norm_ffn=(g_norm_ffn, m_g_norm_ffn, v_g_norm_ffn),
        b_conv=(b_conv, m_b_conv, v_b_conv), g_norm_final=(g_norm_final, m_g_norm_final, v_g_norm_final),
        ssm_b_re=(ssm_b_re, m_ssm_b_re, v_ssm_b_re), ssm_b_im=(ssm_b_im, m_ssm_b_im, v_ssm_b_im))
    b_view = (GRP * NST, GCH)
    du2 = du2r.reshape(u2.shape)
    first, ((rc_glu, rc_out), parts_b) = _mixer_in_bwd(
        du2, dp, x, dx1, sh1, sc1, g_norm_mix, w_in_b, nb=bsz - 1,
        comm=[(_chip_plan, [s_glu, s_out]), (_gather_plan, [g for _, _, g in small_b] + [gbr, gbi])])
    (grad_x, dw_in, dsh1, dsc1, dg_mix), _ = _mixer_in_bwd(
        du2, dp, x, dx1, sh1, sc1, g_norm_mix, w_in_b, b0=bsz - 1, nb=1, prev=first)
    big_glu = _final_sum_adamw(own_glu, rc_glu, w_glu[0], m_w_glu[0], v_w_glu[0], "final_adamw_glu")
    big_out = _final_sum_adamw(own_out, rc_out, w_out[0], m_w_out[0], v_w_out[0], "final_adamw_out")
    parts = list(parts_a[:-2]) + list(parts_b[:-2])
    items = [(pt,) + tuple(view(a, shp) for a in given[nm]) for pt, (nm, shp, _) in zip(parts, small)]
    small_out, (g_conv_full, loss_all, gbr_all, gbi_all) = _small_sum_adamw(
        items, [parts_a[-2], parts_a[-1], parts_b[-2], parts_b[-1]])
    loss = loss_all[0, 0]
    result = {nm: [t.reshape(given[nm][0].shape) for t in quad] for quad, (nm, _, _) in zip(small_out, small)}
    for nm, g_all in (("ssm_b_re", gbr_all), ("ssm_b_im", gbi_all)):
        quad = [g_all.T] + list(_adamw_plain(g_all.T, *[view(a, b_view) for a in given[nm]]))
        result[nm] = [t.reshape(given[nm][0].shape) for t in quad]
    g_w_conv = lax.dynamic_slice_in_dim(g_conv_full, 352 * me, 352, axis=1)
    result["w_conv"] = [g_w_conv[None]] + [t[None] for t in _adamw_plain(g_w_conv, w_conv[0], m_w_conv[0], v_w_conv[0])]

    for nm, quad in (("w_glu", big_glu), ("w_out", big_out), ("w_up", big_up), ("w_down", big_down)):
        result[nm] = [t[None] for t in quad]

    dmod = jnp.concatenate([t.reshape(bsz, D) for t in (dsh1, dsc1, dgt1, dsh2, dsc2, dgt2)], axis=1)
    dmod_blk = jnp.zeros((N_DEV, 8, ncol), F32).at[:, 0:bsz].set(dmod.reshape(bsz, N_DEV, ncol).transpose(1, 0, 2))
    dmod_blk = dmod_blk.at[0, ADA_RIDER_ROW].set(dg_mix[0, 0:ncol]).at[1, ADA_RIDER_ROW, 0:D - ncol].set(dg_mix[0, ncol:])
    ada = _ada_bwd(dmod_blk.reshape(ADA_ROWS, ncol), c_all, w_ada[0], m_w_ada[0], v_w_ada[0],
                   b_ada, m_b_ada, v_b_ada, g_norm_mix, m_g_norm_mix, v_g_norm_mix,
                   dw_in.reshape(4, 2, 128, D), w_in[0], m_w_in[0], v_w_in[0])
    result["w_ada"] = [t[None] for t in ada[0:4]]
    result["b_ada"] = list(ada[4:8])
    result["g_norm_mix"] = list(ada[8:12])
    result["w_in"] = [t[None] for t in ada[12:16]]

    names = ["w_ada", "b_ada", "g_norm_mix", "w_in", "ssm_lam_re", "ssm_lam_im", "ssm_log_dt", "ssm_b_re", "ssm_b_im",
             "ssm_c_re", "ssm_c_im", "ssm_d", "w_glu", "b_glu", "w_pool", "b_pool", "pool_scale", "w_out", "g_norm_ffn",
             "w_up", "w_conv", "b_conv", "w_down", "g_norm_final"]
    return (loss, grad_x, *[result[nm][k] for k in range(4) for nm in names])
```

```python
import functools
import math

import jax
import jax.numpy as jnp
from jax import lax
from jax.experimental import pallas as pl
from jax.experimental.pallas import tpu as pltpu

F32 = jnp.float32
BF16 = jnp.bfloat16

D = 1024
SSM_W = 512
POOL_W = 512
GRP = 32
GCH = 16
NST = 64
HALF_ST = GRP * NST // 2
HALF_CH = SSM_W // 2
DFF = 2816
FF_CH = 2816
N_MOD = 6
N_DEV = 8
EPS = 1e-6
POOL_WINDOWS = (2, 4, 8, 16)
POOL_HALO = 16
CONV_HALO = 8
GELU_C = math.sqrt(2.0 / math.pi)
GELU_A = 0.044715

ADAM_LR = 0.001
ADAM_B1 = 0.9
ADAM_B2 = 0.999
ADAM_EPS = 1e-08
ADAM_WD = 0.01
ADAM_STEP = 10

VMEM_LIMIT = 56 * 1024 * 1024
TT_MIX = 1024
TT_FFN = 256
T_SSM = 128
TT_WGRAD = 2048
MESH = pl.DeviceIdType.MESH
NT = (((1,), (1,)), ((), ()))
TN = (((0,), (0,)), ((), ()))
ANY = pl.BlockSpec(memory_space=pl.ANY)
VMEM = pl.BlockSpec(memory_space=pltpu.VMEM)


def _params(n_grid, vmem=VMEM_LIMIT):
    return pltpu.CompilerParams(dimension_semantics=("arbitrary",) * n_grid, vmem_limit_bytes=vmem)


def _dot(a, b):
    return jnp.dot(a, b, preferred_element_type=F32)


def _dot_nt(a, b):
    return lax.dot_general(a, b, NT, preferred_element_type=F32)


def _dot_tn(a, b):
    return lax.dot_general(a, b, TN, preferred_element_type=F32)


def _colsum(a):
    return jnp.sum(a, axis=0, keepdims=True)


def _rms(x):
    rstd = lax.rsqrt(jnp.mean(x * x, axis=-1, keepdims=True) + EPS)
    return x * rstd, rstd


def _rms_bwd(dxhat, xhat, rstd):
    return rstd * (dxhat - xhat * jnp.mean(dxhat * xhat, axis=-1, keepdims=True))


def _gelu(x):
    return 0.5 * x * (1.0 + jnp.tanh(GELU_C * (x + GELU_A * x * x * x)))


def _gelu_grad(x):
    x2 = x * x
    th = jnp.tanh(GELU_C * (x + GELU_A * x * x2))
    return 0.5 * (1.0 + th) + 0.5 * x * (1.0 - th * th) * GELU_C * (1.0 + 3.0 * GELU_A * x2)


def _adamw(w, g, m, v):
    m = ADAM_B1 * m + (1.0 - ADAM_B1) * g
    v = ADAM_B2 * v + (1.0 - ADAM_B2) * (g * g)
    m_hat = m / (1.0 - ADAM_B1 ** ADAM_STEP)
    v_hat = v / (1.0 - ADAM_B2 ** ADAM_STEP)
    delta = -ADAM_LR * (m_hat / (jnp.sqrt(v_hat) + ADAM_EPS) + ADAM_WD * w)
    return delta, m, v


def _my_place():
    return lax.axis_index("x"), lax.axis_index("y"), lax.axis_index("c")


def _gather_plan(shards):
    n = len(shards)
    out_shape = [jax.ShapeDtypeStruct((N_DEV,) + tuple(s.shape), s.dtype) for s in shards]
    scratch = [pltpu.SemaphoreType.DMA((n, 7)), pltpu.SemaphoreType.DMA((n, 7)), pltpu.SemaphoreType.DMA((n,))]

    def stages(x_refs, out_refs, sems):
        send_sems, recv_sems, local_sems = sems
        x, y, c = _my_place()
        me, sibling = (x, y, c), (x, y, 1 - c)
        chips = [(1 - x, y), (x, 1 - y), (1 - x, 1 - y)]

        def copy(i, k, block, to, own=False):
            px, py, pc = block
            dst = out_refs[i].at[4 * px + 2 * py + pc]
            return pltpu.make_async_remote_copy(
                src_ref=x_refs[i] if own else dst, dst_ref=dst, send_sem=send_sems.at[i, k],
                recv_sem=recv_sems.at[i, k], device_id=to, device_id_type=MESH)

        def mine(i):
            return pltpu.make_async_copy(x_refs[i], out_refs[i].at[4 * x + 2 * y + c], local_sems.at[i])

        def start():
            for i in range(n):
                mine(i).start()
                copy(i, 0, me, sibling, own=True).start()
                for j, chip in enumerate(chips):
                    copy(i, 1 + j, me, (*chip, c), own=True).start()

        def forward():
            for i in range(n):
                for j, chip in enumerate(chips):
                    copy(i, 1 + j, (*chip, c), me).wait_recv()
                    copy(i, 4 + j, (*chip, c), sibling).start()

        def finish():
            for i in range(n):
                copy(i, 0, sibling, me).wait_recv()
                copy(i, 0, me, sibling, own=True).wait_send()
                for j, chip in enumerate(chips):
                    copy(i, 4 + j, (*chip, 1 - c), me).wait_recv()
                    copy(i, 1 + j, me, (*chip, c), own=True).wait_send()
                    copy(i, 4 + j, (*chip, c), sibling).wait_send()
                mine(i).wait()

        return [start, forward, finish]

    return n, out_shape, scratch, stages


def _pair_plan(g42s):
    n = len(g42s)
    out_shape = [jax.ShapeDtypeStruct((4,) + tuple(g.shape[2:]), g.dtype) for g in g42s]
    scratch = [pltpu.SemaphoreType.DMA((n,)), pltpu.SemaphoreType.DMA((n,))]

    def stages(g_refs, out_refs, sems):
        send_sems, recv_sems = sems
        x, y, c = _my_place()

        def copy(i):
            return pltpu.make_async_remote_copy(
                src_ref=g_refs[i].at[:, 1 - c], dst_ref=out_refs[i], send_sem=send_sems.at[i],
                recv_sem=recv_sems.at[i], device_id=(x, y, 1 - c), device_id_type=MESH)

        def start():
            for i in range(n):
                copy(i).start()

        def finish():
            for i in range(n):
                copy(i).wait()

        return [start, finish]

    return n, out_shape, scratch, stages


def _chip_plan(s4s):
    n = len(s4s)
    out_shape = [jax.ShapeDtypeStruct((3,) + tuple(s.shape[1:]), s.dtype) for s in s4s]
    scratch = [pltpu.SemaphoreType.DMA((n, 3)), pltpu.SemaphoreType.DMA((n, 3))]

    def stages(s_refs, out_refs, sems):
        send_sems, recv_sems = sems
        x, y, c = _my_place()

        def copy(i, d):
            px, py = x ^ (d >> 1), y ^ (d & 1)
            return pltpu.make_async_remote_copy(
                src_ref=s_refs[i].at[2 * px + py], dst_ref=out_refs[i].at[d - 1], send_sem=send_sems.at[i, d - 1],
                recv_sem=recv_sems.at[i, d - 1], device_id=(px, py, c), device_id_type=MESH)

        def start():
            for i in range(n):
                for d in (1, 2, 3):
                    copy(i, d).start()

        def finish():
            for i in range(n):
                for d in (1, 2, 3):
                    copy(i, d).wait()

        return [start, finish]

    return n, out_shape, scratch, stages


def _comm_call(plan, arrays, name):
    n, out_shape, scratch, stages = plan

    def body(*refs):
        for stage in stages(refs[:n], refs[n:2 * n], refs[2 * n:]):
            stage()

    return pl.pallas_call(
        body, name=name, out_shape=out_shape, in_specs=[ANY] * n, out_specs=[ANY] * n, scratch_shapes=scratch,
    )(*arrays)


def _fused_call(body, *, name, grid, in_specs, out_specs, out_shape, scratch_shapes, args, comm=None):
    if not comm:
        out = pl.pallas_call(body, name=name, grid=grid, in_specs=in_specs, out_specs=out_specs, out_shape=out_shape,
                             scratch_shapes=scratch_shapes, compiler_params=_params(len(grid)))(*args)
        return out, []
    counts = [plan[0] for plan, _, _ in comm]
    n = sum(counts)
    n_in, n_out, n_scr = len(in_specs), len(out_specs), len(scratch_shapes)

    def fused(*refs):
        ins, refs = refs[:n_in], refs[n_in:]
        c_ins, refs = refs[:n], refs[n:]
        outs, refs = refs[:n_out], refs[n_out:]
        c_outs, refs = refs[:n], refs[n:]
        scr, c_scr = refs[:n_scr], refs[n_scr:]
        step = pl.program_id(0)
        for k in range(1, len(grid)):
            step = step * grid[k] + pl.program_id(k)
        todo, a0, s0 = [], 0, 0
        for (cnt, _, plan_scratch, stages), _, steps in comm:
            sems = c_scr[s0:s0 + len(plan_scratch)]
            todo += list(zip(stages(c_ins[a0:a0 + cnt], c_outs[a0:a0 + cnt], sems), steps))
            a0 += cnt
            s0 += len(plan_scratch)
        for stage, at in todo:
            if at == 0:
                pl.when(step == 0)(stage)
        body(*ins, *outs, *scr)
        for stage, at in todo:
            if at != 0:
                pl.when(step == at)(stage)

    c_shape = [s for plan, _, _ in comm for s in plan[1]]
    c_scratch = [s for plan, _, _ in comm for s in plan[2]]
    arrays = [a for _, arrs, _ in comm for a in arrs]
    out = pl.pallas_call(
        fused, name=name, grid=grid, in_specs=list(in_specs) + [ANY] * n, out_specs=list(out_specs) + [ANY] * n,
        out_shape=list(out_shape) + c_shape, scratch_shapes=list(scratch_shapes) + c_scratch,
        compiler_params=_params(len(grid)))(*args, *arrays)
    outs, c_outs, split, a0 = out[:n_out], out[n_out:], [], 0
    for cnt in counts:
        split.append(c_outs[a0:a0 + cnt])
        a0 += cnt
    return outs, split


def _schedule(comm, n_steps):
    out = []
    for make_plan, arrays in comm or []:
        steps = [0, (3 * n_steps) // 4, n_steps - 1] if make_plan is _gather_plan else [0, n_steps - 1]
        out.append((make_plan(arrays), arrays, steps))
    return out


def _row_tile(r):
    for t in (128, 64, 32, 16, 8):
        if r % t == 0:
            return t
    return r


def _pair_sum(g42, recv, place, name):
    _, _, r, cdim = g42.shape
    tr = _row_tile(r)

    def body(pl_ref, g_ref, r_ref, own_ref, s_ref):
        s_ref[...] = (g_ref[:, 0] + r_ref[...]).astype(BF16)
        q = pl_ref[1]
        own_ref[...] = g_ref[q, 0] + r_ref[q]

    return pl.pallas_call(
        body, name=name,
        grid_spec=pltpu.PrefetchScalarGridSpec(
            num_scalar_prefetch=1, grid=(r // tr,),
            in_specs=[pl.BlockSpec((4, 1, tr, cdim), lambda i, p: (0, p[0], i, 0)),
                      pl.BlockSpec((4, tr, cdim), lambda i, p: (0, i, 0))],
            out_specs=[pl.BlockSpec((tr, cdim), lambda i, p: (i, 0)),
                       pl.BlockSpec((4, tr, cdim), lambda i, p: (0, i, 0))]),
        out_shape=[jax.ShapeDtypeStruct((r, cdim), F32), jax.ShapeDtypeStruct((4, r, cdim), BF16)],
        compiler_params=_params(1),
    )(place, g42, recv)


def _final_sum_adamw(own, recv3, w, m, v, name):
    r, cdim = w.shape
    tr = _row_tile(r)

    def body(s_ref, r_ref, w_ref, m_ref, v_ref, g_out, d_out, m_out, v_out):
        g = s_ref[...] + r_ref[0].astype(F32) + r_ref[1].astype(F32) + r_ref[2].astype(F32)
        d, mn, vn = _adamw(w_ref[...], g, m_ref[...], v_ref[...])
        g_out[...] = g
        d_out[...] = d
        m_out[...] = mn
        v_out[...] = vn

    blk = pl.BlockSpec((tr, cdim), lambda i: (i, 0))
    shp = jax.ShapeDtypeStruct((r, cdim), F32)
    return pl.pallas_call(
        body, name=name, grid=(r // tr,),
        in_specs=[blk, pl.BlockSpec((3, tr, cdim), lambda i: (0, i, 0)), blk, blk, blk],
        out_specs=[blk, blk, blk, blk], out_shape=[shp, shp, shp, shp], compiler_params=_params(1),
    )(own, recv3, w, m, v)


def _small_sum_adamw(items, sums_only):
    n, ne = len(items), len(sums_only)

    def total(p_ref):
        g = p_ref[0]
        for k in range(1, N_DEV):
            g = g + p_ref[k]
        return g

    def body(*refs):
        ins, outs = refs[:4 * n + ne], refs[4 * n + ne:]
        for i in range(n):
            p_ref, w_ref, m_ref, v_ref = ins[4 * i:4 * i + 4]
            g = total(p_ref)
            d, mn, vn = _adamw(w_ref[...], g, m_ref[...], v_ref[...])
            for o_ref, val in zip(outs[4 * i:4 * i + 4], (g, d, mn, vn)):
                o_ref[...] = val
        for j in range(ne):
            outs[4 * n + j][...] = total(ins[4 * n + j])

    args = [a for item in items for a in item] + list(sums_only)
    shapes = [jax.ShapeDtypeStruct(w.shape, F32) for _, w, _, _ in items for _ in range(4)]
    shapes += [jax.ShapeDtypeStruct(p.shape[1:], F32) for p in sums_only]
    out = pl.pallas_call(
        body, name="small_sum_adamw", in_specs=[VMEM] * len(args), out_specs=[VMEM] * len(shapes), out_shape=shapes,
        compiler_params=_params(0),
    )(*args)
    return [out[4 * i:4 * i + 4] for i in range(n)], out[4 * n:]


def _adamw_plain(g, w, m, v):
    r, cdim = w.shape
    tr = r if r * cdim <= 64 * 1024 else _row_tile(r)

    def body(g_ref, w_ref, m_ref, v_ref, d_out, m_out, v_out):
        d, mn, vn = _adamw(w_ref[...], g_ref[...], m_ref[...], v_ref[...])
        d_out[...] = d
        m_out[...] = mn
        v_out[...] = vn

    blk = pl.BlockSpec((tr, cdim), lambda i: (i, 0))
    shp = jax.ShapeDtypeStruct((r, cdim), F32)
    return pl.pallas_call(
        body, name="adamw_plain", grid=(r // tr,), in_specs=[blk, blk, blk, blk],
        out_specs=[blk, blk, blk], out_shape=[shp, shp, shp], compiler_params=_params(1),
    )(g, w, m, v)


ADA_COLS = N_MOD * D // N_DEV
ADA_ROWS = 8 * N_DEV


def _ada_fwd(cpad, w_ada, b_blocks, mixer_shards):
    n_w, w_shape, w_scr, w_stages = _gather_plan(mixer_shards)
    _, _, c_scr, c_stages = _gather_plan([cpad])
    _, _, p_scr, p_stages = _gather_plan([jax.ShapeDtypeStruct((ADA_ROWS, ADA_COLS), F32)])

    def body(c_ref, wa_ref, b_ref, *refs):
        w_refs, refs = refs[:n_w], refs[n_w:]
        cg_ref, call_ref, mod_ref = refs[:3]
        wg_refs, refs = refs[3:3 + n_w], refs[3 + n_w:]
        part_ref, pg_ref = refs[:2]
        c_sems, p_sems, w_sems = refs[2:5], refs[5:8], refs[8:11]
        w_start, w_forward, w_finish = w_stages(w_refs, wg_refs, w_sems)
        w_start()
        for stage in c_stages([c_ref], [cg_ref], c_sems):
            stage()
        cv = cg_ref[:, 0:8, :].reshape(ADA_ROWS, D)
        call_ref[...] = cv
        part_ref[...] = _dot(cv * jax.nn.sigmoid(cv), wa_ref[...])
        for stage in p_stages([part_ref], [pg_ref], p_sems):
            stage()
        x, y, c = _my_place()
        r0 = pl.multiple_of(8 * (4 * x + 2 * y + c), 8)
        for k in range(N_DEV):
            mod_ref[:, k * ADA_COLS:(k + 1) * ADA_COLS] = pg_ref[k, pl.ds(r0, 8), :] + b_ref[k]
        w_forward()
        w_finish()

    out = pl.pallas_call(
        body, name="ada_fwd", in_specs=[VMEM, VMEM, VMEM] + [ANY] * n_w,
        out_specs=[VMEM, VMEM, VMEM] + [ANY] * n_w,
        out_shape=[jax.ShapeDtypeStruct((N_DEV,) + cpad.shape, F32), jax.ShapeDtypeStruct((ADA_ROWS, D), F32),
                   jax.ShapeDtypeStruct((8, N_MOD * D), F32)] + list(w_shape),
        scratch_shapes=[pltpu.VMEM((ADA_ROWS, ADA_COLS), F32), pltpu.VMEM((N_DEV, ADA_ROWS, ADA_COLS), F32)]
        + list(c_scr) + list(p_scr) + list(w_scr),
        compiler_params=_params(0),
    )(cpad, w_ada, b_blocks, *mixer_shards)
    return out[0], out[1], out[2], out[3:]


ADA_RIDER_ROW = 4


def _ada_bwd(dmod_blk, c_all, w_ada, m_w, v_w, b_blocks, m_b, v_b, g_w, g_m, g_v, g42, w_l, m_l, v_l):
    _, _, g_scr, g_stages = _gather_plan([dmod_blk])
    rest = D - ADA_COLS
    blk = tuple(g42.shape[2:])

    def body(dm_ref, c_ref, w_ref, mw_ref, vw_ref, b_ref, mb_ref, vb_ref, gw_ref, gm_ref, gv_ref,
             g42_ref, wl_ref, ml_ref, vl_ref,
             gw_o, dw_o, mw_o, vw_o, gb_o, dbb_o, mb_o, vb_o, gg_o, dgg_o, mg_o, vg_o, gl_o, dl_o, ml_o, vl_o,
             dg_ref, pr_ref, sbf_ref, rc_ref, pair_send, pair_recv, chip_send, chip_recv, *sems):
        x, y, c = _my_place()
        q = 2 * x + y
        pair = pltpu.make_async_remote_copy(
            src_ref=g42_ref.at[:, 1 - c], dst_ref=pr_ref, send_sem=pair_send, recv_sem=pair_recv,
            device_id=(x, y, 1 - c), device_id_type=MESH)
        pair.start()
        for stage in g_stages([dm_ref], [dg_ref], sems):
            stage()
        r0 = pl.multiple_of(8 * (4 * x + 2 * y + c), 8)
        cols = dg_ref[:, pl.ds(r0, 8), :].reshape(ADA_ROWS, ADA_COLS)
        cv = c_ref[...]
        gw = _dot_tn(cv * jax.nn.sigmoid(cv), cols)
        d, mn, vn = _adamw(w_ref[...], gw, mw_ref[...], vw_ref[...])
        gw_o[...] = gw
        dw_o[...] = d
        mw_o[...] = mn
        vw_o[...] = vn
        is_example = lax.broadcasted_iota(jnp.int32, (8, 1), 0) < ADA_RIDER_ROW
        blocks = []
        for k in range(N_DEV):
            s = dg_ref[0, 8 * k:8 * k + 8, :]
            for dev in range(1, N_DEV):
                s = s + dg_ref[dev, 8 * k:8 * k + 8, :]
            blocks.append(s)
            gb = _colsum(jnp.where(is_example, s, 0.0))
            cs = slice(k * ADA_COLS, (k + 1) * ADA_COLS)
            d, mn, vn = _adamw(b_ref[:, cs], gb, mb_ref[:, cs], vb_ref[:, cs])
            gb_o[:, cs] = gb
            dbb_o[:, cs] = d
            mb_o[:, cs] = mn
            vb_o[:, cs] = vn
        rider = jnp.concatenate([blocks[0][ADA_RIDER_ROW:ADA_RIDER_ROW + 1, :],
                                 blocks[1][ADA_RIDER_ROW:ADA_RIDER_ROW + 1, 0:rest]], axis=1)
        d, mn, vn = _adamw(gw_ref[...], rider, gm_ref[...], gv_ref[...])
        gg_o[...] = rider
        dgg_o[...] = d
        mg_o[...] = mn
        vg_o[...] = vn
        pair.wait()
        for k in range(4):
            sbf_ref[k] = (g42_ref[k, c] + pr_ref[k]).astype(BF16)

        def chip_copy(dist):
            px, py = x ^ (dist >> 1), y ^ (dist & 1)
            return pltpu.make_async_remote_copy(
                src_ref=sbf_ref.at[2 * px + py], dst_ref=rc_ref.at[dist - 1], send_sem=chip_send.at[dist - 1],
                recv_sem=chip_recv.at[dist - 1], device_id=(px, py, c), device_id_type=MESH)

        for dist in (1, 2, 3):
            chip_copy(dist).start()
        for dist in (1, 2, 3):
            chip_copy(dist).wait()
        gl = g42_ref[q, c] + pr_ref[q] + rc_ref[0].astype(F32) + rc_ref[1].astype(F32) + rc_ref[2].astype(F32)
        d, mn, vn = _adamw(wl_ref[...], gl, ml_ref[...], vl_ref[...])
        gl_o[...] = gl
        dl_o[...] = d
        ml_o[...] = mn
        vl_o[...] = vn

    ws = jax.ShapeDtypeStruct(w_ada.shape, F32)
    bs = jax.ShapeDtypeStruct(b_blocks.shape, F32)
    gs = jax.ShapeDtypeStruct(g_w.shape, F32)
    ls = jax.ShapeDtypeStruct(w_l.shape, F32)
    return pl.pallas_call(
        body, name="ada_bwd", in_specs=[VMEM] * 15, out_specs=[VMEM] * 16,
        out_shape=[ws, ws, ws, ws, bs, bs, bs, bs, gs, gs, gs, gs, ls, ls, ls, ls],
        scratch_shapes=[pltpu.VMEM((N_DEV, ADA_ROWS, ADA_COLS), F32), pltpu.VMEM((4,) + blk, F32),
                        pltpu.VMEM((4,) + blk, BF16), pltpu.VMEM((3,) + blk, BF16),
                        pltpu.SemaphoreType.DMA(()), pltpu.SemaphoreType.DMA(()),
                        pltpu.SemaphoreType.DMA((3,)), pltpu.SemaphoreType.DMA((3,))] + list(g_scr),
        compiler_params=_params(0),
    )(dmod_blk, c_all, w_ada, m_w, v_w, b_blocks, m_b, v_b, g_w, g_m, g_v, g42, w_l, m_l, v_l)


def _ssm_param_fn(lr, li, ldt, br, bi):
    dt = jnp.exp(ldt)
    mag = jnp.exp(lr * dt)
    ang = li * dt
    lbr = mag * jnp.cos(ang)
    lbi = mag * jnp.sin(ang)
    nr = lbr - 1.0
    den = lr * lr + li * li
    cr = (nr * lr + lbi * li) / den
    ci = (lbi * lr - nr * li) / den
    return lbr, lbi, cr * br - ci * bi, cr * bi + ci * br


def _ssm_prep(lr, li, ldt, br, bi):
    def body(lr_ref, li_ref, ldt_ref, br_ref, bi_ref, lbr_o, lbi_o, bbr_o, bbi_o):
        lbr, lbi, bbr, bbi = _ssm_param_fn(lr_ref[...], li_ref[...], ldt_ref[...], br_ref[...], bi_ref[...])
        lbr_o[...] = lbr
        lbi_o[...] = lbi
        bbr_o[...] = bbr
        bbi_o[...] = bbi

    row = jax.ShapeDtypeStruct(lr.shape, F32)
    mat = jax.ShapeDtypeStruct(br.shape, F32)
    return pl.pallas_call(
        body, name="ssm_prep", in_specs=[VMEM] * 5, out_specs=[VMEM] * 4,
        out_shape=[row, row, mat, mat], compiler_params=_params(0),
    )(lr, li, ldt, br, bi)


def _ssm_param_bwd(lr, li, ldt, br, bi, dlam8, dbbr, dbbi, dd8):
    nv = dlam8.shape[1]

    def body(lr_ref, li_ref, ldt_ref, br_ref, bi_ref, dl_ref, dbr_ref, dbi_ref, dd_ref,
             glr_o, gli_o, gldt_o, gbr_o, gbi_o, gd_o):
        halves_r, halves_i, halves_d = [], [], []
        for e in range(2):
            ar = dl_ref[0, e:e + 1, :]
            ai = dl_ref[1, e:e + 1, :]
            ad = dd_ref[e:e + 1, :]
            for b in range(1, nv // 2):
                ar = ar + dl_ref[0, 2 * b + e:2 * b + e + 1, :]
                ai = ai + dl_ref[1, 2 * b + e:2 * b + e + 1, :]
                ad = ad + dd_ref[2 * b + e:2 * b + e + 1, :]
            halves_r.append(ar)
            halves_i.append(ai)
            halves_d.append(ad)
        dlbr = jnp.concatenate(halves_r, axis=1)
        dlbi = jnp.concatenate(halves_i, axis=1)
        gd_o[...] = jnp.concatenate(halves_d, axis=1)
        _, vjp = jax.vjp(_ssm_param_fn, lr_ref[...], li_ref[...], ldt_ref[...], br_ref[...], bi_ref[...])
        glr, gli, gldt, gbr, gbi = vjp((dlbr, dlbi, dbr_ref[...], dbi_ref[...]))
        glr_o[...] = glr
        gli_o[...] = gli
        gldt_o[...] = gldt
        gbr_o[...] = gbr
        gbi_o[...] = gbi

    row = jax.ShapeDtypeStruct(lr.shape, F32)
    mat = jax.ShapeDtypeStruct(br.shape, F32)
    return pl.pallas_call(
        body, name="ssm_param_bwd", in_specs=[VMEM] * 9, out_specs=[VMEM] * 6,
        out_shape=[row, row, row, mat, mat, jax.ShapeDtypeStruct((1, SSM_W), F32)],
        compiler_params=_params(0),
    )(lr, li, ldt, br, bi, dlam8, dbbr, dbbi, dd8)


def _blockdiag(m):
    _, g, a, b = m.shape
    eye = jnp.eye(g, dtype=m.dtype)
    return jnp.einsum("egab,gk->egakb", m, eye).reshape(2, g * a, g * b)


def _blockdiag_take(t, a, b):
    return jnp.einsum("gagb->gab", t.reshape(GRP // 2, a, GRP // 2, b))


def _mixer_in_fwd(x, sh1, sc1, g_mix, w_in_b, comm=None):
    bsz, seq, _ = x.shape
    tt = min(seq, TT_MIX)

    def body(x_ref, sh_ref, sc_ref, g_ref, w_ref, u_ref, p_ref):
        xhat, _ = _rms(x_ref[0])
        h = xhat * g_ref[...] * (1.0 + sc_ref[0]) + sh_ref[0]
        z = _dot(h.astype(BF16), w_ref[...])
        u_ref[...] = z[:, :SSM_W].astype(BF16)
        p_ref[0] = z[:, SSM_W:]

    row = pl.BlockSpec((1, 1, D), lambda b, t: (b, 0, 0))
    return _fused_call(
        body, name="mixer_in_fwd", grid=(bsz, seq // tt),
        in_specs=[pl.BlockSpec((1, tt, D), lambda b, t: (b, t, 0)), row, row,
                  pl.BlockSpec((1, D), lambda b, t: (0, 0)), VMEM],
        out_specs=[pl.BlockSpec((tt, SSM_W), lambda b, t: (t, b)),
                   pl.BlockSpec((1, tt, POOL_W), lambda b, t: (b, t, 0))],
        out_shape=[jax.ShapeDtypeStruct((seq, bsz * SSM_W), BF16), jax.ShapeDtypeStruct((bsz, seq, POOL_W), F32)],
        scratch_shapes=[], args=(x, sh1, sc1, g_mix, w_in_b), comm=_schedule(comm, bsz * (seq // tt)))


def _ssm_project_in(ub, par0, bb_ref, s_re, s_im, row0, tlen, nv):
    for part, sref in ((0, s_re), (1, s_im)):
        for k in range(HALF_ST // 512):
            c0 = part * HALF_ST + k * 512
            a0 = _dot(ub, bb_ref[:, c0:c0 + 512])
            a1 = _dot(ub, bb_ref[:, 2 * HALF_ST + c0:2 * HALF_ST + c0 + 512])
            sref[pl.ds(row0, tlen), :, k * 512:(k + 1) * 512] = jnp.where(par0, a0, a1).reshape(tlen, nv, 512)


def _ssm_fwd(u2r, bb, cc, lam8, d8, tlen, comm=None):
    nv = lam8.shape[1]
    rows = nv * tlen
    n_chunks = u2r.shape[0] // rows

    def body(u_ref, bb_ref, cc_ref, lam_ref, d_ref, y_ref, xc_ref, xre_ref, xim_ref, s_re, s_im, st):
        @pl.when(pl.program_id(0) == 0)
        def _():
            st[...] = jnp.zeros_like(st)

        xc_ref[0] = st[...]
        ub = u_ref[...]
        u = ub.astype(F32)
        par0 = (lax.broadcasted_iota(jnp.int32, (rows, 1), 0) % 2) == 0
        _ssm_project_in(ub, par0, bb_ref, s_re, s_im, 0, tlen, nv)
        for hb in range(HALF_ST // 512):
            ls = slice(hb * 512, (hb + 1) * 512)
            lr = lam_ref[0, :, ls]
            li = lam_ref[1, :, ls]

            def step(t, carry, ls=ls, lr=lr, li=li):
                xr, xi = carry
                nr = lr * xr - li * xi + s_re[t, :, ls]
                ni = lr * xi + li * xr + s_im[t, :, ls]
                s_re[t, :, ls] = nr
                s_im[t, :, ls] = ni
                return nr, ni

            xr, xi = lax.fori_loop(0, tlen, step, (st[0, :, ls], st[1, :, ls]), unroll=8)
            st[0, :, ls] = xr
            st[1, :, ls] = xi
        xre = s_re[...].reshape(rows, HALF_ST).astype(BF16)
        xim = s_im[...].reshape(rows, HALF_ST).astype(BF16)
        xre_ref[...] = xre
        xim_ref[...] = xim
        y2 = _dot(xre, cc_ref[0:HALF_ST, :]) + _dot(xim, cc_ref[HALF_ST:, :])
        y = jnp.where(par0, y2[:, :HALF_CH], y2[:, HALF_CH:])
        skip = (u.reshape(tlen, nv, HALF_CH) * d_ref[...][None]).reshape(rows, HALF_CH)
        y_ref[...] = (y + skip).astype(BF16)

    st_blk = pl.BlockSpec((rows, HALF_ST), lambda c: (c, 0))
    st_shape = jax.ShapeDtypeStruct((u2r.shape[0], HALF_ST), BF16)
    return _fused_call(
        body, name="ssm_fwd", grid=(n_chunks,),
        in_specs=[pl.BlockSpec((rows, HALF_CH), lambda c: (c, 0)), VMEM, VMEM, VMEM, VMEM],
        out_specs=[pl.BlockSpec((rows, HALF_CH), lambda c: (c, 0)),
                   pl.BlockSpec((1, 2, nv, HALF_ST), lambda c: (c, 0, 0, 0)), st_blk, st_blk],
        out_shape=[jax.ShapeDtypeStruct(u2r.shape, BF16), jax.ShapeDtypeStruct((n_chunks, 2, nv, HALF_ST), F32),
                   st_shape, st_shape],
        scratch_shapes=[pltpu.VMEM((tlen, nv, HALF_ST), F32), pltpu.VMEM((tlen, nv, HALF_ST), F32),
                        pltpu.VMEM((2, nv, HALF_ST), F32)],
        args=(u2r, bb, cc, lam8, d8), comm=_schedule(comm, n_chunks))


def _pool_forward(ext, pv, pos, wp_ref, bp_ref):
    cur = ext
    zs, zls = [], []
    for gi, w in enumerate(POOL_WINDOWS):
        cur = cur + pltpu.roll(cur, w // 2, 0)
        sw = cur[POOL_HALO:, 0:128]
        z = sw / jnp.minimum(pos, float(w)) - pv[:, gi * 128:(gi + 1) * 128]
        zs.append(z)
        zls.append(_dot(z.astype(BF16), wp_ref[gi]) + bp_ref[:, gi * 128:(gi + 1) * 128])
        if gi + 1 < len(POOL_WINDOWS):
            cur = cur[:, 128:]
    return zs, zls


def _mixer_out_fwd(y2, p, x, gt1, w_glu_b, b_glu, w_pool_b, b_pool, pscale, w_out_b):
    bsz, seq, _ = x.shape
    tt = min(seq, TT_MIX)

    def body(y_ref, p_ref, x_ref, gt_ref, wg_ref, bg_ref, wp_ref, bp_ref, ps_ref, wo_ref, x1_ref, mix_ref, mxd_ref,
             ext):
        ti = pl.program_id(1)

        @pl.when(ti == 0)
        def _():
            ext[0:POOL_HALO, :] = jnp.zeros((POOL_HALO, POOL_W), F32)

        pv = p_ref[0]
        ext[POOL_HALO:, :] = pv
        pos = (ti * tt + lax.broadcasted_iota(jnp.int32, (tt, 1), 0) + 1).astype(F32)
        _, zls = _pool_forward(ext[...], pv, pos, wp_ref, bp_ref)
        ext[0:POOL_HALO, :] = pv[tt - POOL_HALO:, :]
        a = _gelu(y_ref[...].astype(F32))
        gl = _dot(a.astype(BF16), wg_ref[...]) + bg_ref[...]
        y_ssm = gl[:, :SSM_W] * jax.nn.sigmoid(gl[:, SSM_W:])
        y_pool = [zl * ps_ref[:, gi * 128:(gi + 1) * 128] for gi, zl in enumerate(zls)]
        mixcat = jnp.concatenate([y_ssm] + y_pool, axis=1).astype(BF16)
        mix_ref[0] = mixcat
        mixed = _dot(mixcat, wo_ref[...])
        mxd_ref[0] = mixed.astype(BF16)
        x1_ref[0] = x_ref[0] + gt_ref[0] * mixed

    xt = pl.BlockSpec((1, tt, D), lambda b, t: (b, t, 0))
    return pl.pallas_call(
        body, name="mixer_out_fwd", grid=(bsz, seq // tt),
        in_specs=[pl.BlockSpec((tt, SSM_W), lambda b, t: (t, b)),
                  pl.BlockSpec((1, tt, POOL_W), lambda b, t: (b, t, 0)), xt,
                  pl.BlockSpec((1, 1, D), lambda b, t: (b, 0, 0)), VMEM, VMEM, VMEM, VMEM, VMEM, VMEM],
        out_specs=[xt, xt, xt],
        out_shape=[jax.ShapeDtypeStruct(x.shape, F32), jax.ShapeDtypeStruct(x.shape, BF16),
                   jax.ShapeDtypeStruct(x.shape, BF16)],
        scratch_shapes=[pltpu.VMEM((POOL_HALO + tt, POOL_W), F32)],
        compiler_params=_params(2),
    )(y2, p, x, gt1, w_glu_b, b_glu, w_pool_b, b_pool, pscale, w_out_b)


def _conv_gate(g, ge, wc, bc):
    g1 = pltpu.roll(ge, 1, 0)[CONV_HALO:]
    g2 = pltpu.roll(ge, 2, 0)[CONV_HALO:]
    return wc[2:3] * g + wc[1:2] * g1 + wc[0:1] * g2 + bc, g1, g2


def _ffn_fwd(x1, tgt, sh2, sc2, gt2, g_ffn, w_up_b, w_conv, b_conv, w_down_b, g_fin):
    bsz, seq, _ = x1.shape
    tt = min(seq, TT_FFN)
    n_t = seq // tt
    n_ck = DFF // FF_CH

    def body(x1_ref, tg_ref, sh_ref, sc_ref, gt_ref, gf_ref, wu_ref, wc_ref, bc_ref, wd_ref, gfin_ref,
             h2_ref, v_ref, g_ref, gc_ref, act_ref, ddn_ref, dx2_ref, loss_ref, dgfin_ref, dgt_ref, gext, lacc):
        b = pl.program_id(0)
        ti = pl.program_id(1)

        @pl.when((b == 0) & (ti == 0))
        def _():
            lacc[...] = jnp.zeros_like(lacc)
            dgfin_ref[...] = jnp.zeros_like(dgfin_ref)

        @pl.when(ti == 0)
        def _():
            dgt_ref[...] = jnp.zeros_like(dgt_ref)
            gext[:, 0:CONV_HALO, :] = jnp.zeros((n_ck, CONV_HALO, FF_CH), F32)

        x1v = x1_ref[0]
        xhat, _ = _rms(x1v)
        h2b = (xhat * gf_ref[...] * (1.0 + sc_ref[0]) + sh_ref[0]).astype(BF16)
        h2_ref[0] = h2b
        dn = jnp.zeros((tt, D), F32)
        for ck in range(n_ck):
            c0 = ck * FF_CH
            v = _dot_nt(h2b, wu_ref[c0:c0 + FF_CH, :])
            g = _dot_nt(h2b, wu_ref[DFF + c0:DFF + c0 + FF_CH, :])
            v_ref[0, :, c0:c0 + FF_CH] = v.astype(BF16)
            g_ref[0, :, c0:c0 + FF_CH] = g.astype(BF16)
            gext[ck, CONV_HALO:, :] = g
            gc, _, _ = _conv_gate(g, gext[ck], wc_ref[:, c0:c0 + FF_CH], bc_ref[:, c0:c0 + FF_CH])
            gext[ck, 0:CONV_HALO, :] = g[tt - CONV_HALO:, :]
            gc_ref[0, :, c0:c0 + FF_CH] = gc.astype(BF16)
            actb = (gc * jax.nn.sigmoid(gc) * v).astype(BF16)
            act_ref[0, :, c0:c0 + FF_CH] = actb
            dn = dn + _dot(actb, wd_ref[c0:c0 + FF_CH, :])
        gt = gt_ref[0]
        xh3, r3 = _rms(x1v + gt * dn)
        gfin = gfin_ref[...]
        diff = xh3 * gfin - tg_ref[0]
        lacc[...] += _colsum(diff * diff)
        dy = diff * (1.0 / D)
        dgfin_ref[...] += _colsum(dy * xh3)
        dx2 = _rms_bwd(dy * gfin, xh3, r3)
        dx2_ref[0] = dx2
        dgt_ref[0] += _colsum(dx2 * dn)
        ddn_ref[0] = (gt * dx2).astype(BF16)

        @pl.when((b == bsz - 1) & (ti == n_t - 1))
        def _():
            loss_ref[...] = jnp.full(loss_ref.shape, 0.5 / D * jnp.sum(lacc[...]), F32)

    xt = pl.BlockSpec((1, tt, D), lambda b, t: (b, t, 0))
    ft = pl.BlockSpec((1, tt, DFF), lambda b, t: (b, t, 0))
    row = pl.BlockSpec((1, 1, D), lambda b, t: (b, 0, 0))
    vec = pl.BlockSpec((1, D), lambda b, t: (0, 0))
    ff = jax.ShapeDtypeStruct((bsz, seq, DFF), BF16)
    xs = jax.ShapeDtypeStruct((bsz, seq, D), BF16)
    return pl.pallas_call(
        body, name="ffn_fwd", grid=(bsz, n_t),
        in_specs=[xt, xt, row, row, row, vec, VMEM, VMEM, VMEM, VMEM, vec],
        out_specs=[xt, ft, ft, ft, ft, xt, xt, pl.BlockSpec((1, 128), lambda b, t: (0, 0)), vec, row],
        out_shape=[xs, ff, ff, ff, ff, xs, jax.ShapeDtypeStruct((bsz, seq, D), F32),
                   jax.ShapeDtypeStruct((1, 128), F32), jax.ShapeDtypeStruct((1, D), F32),
                   jax.ShapeDtypeStruct((bsz, 1, D), F32)],
        scratch_shapes=[pltpu.VMEM((n_ck, CONV_HALO + tt, FF_CH), F32), pltpu.VMEM((1, D), F32)],
        compiler_params=_params(2),
    )(x1, tgt, sh2, sc2, gt2, g_ffn, w_up_b, w_conv, b_conv, w_down_b, g_fin)


def _ffn_bwd(ddn, gq, gcq, vq, x1, dx2, sh2, sc2, g_ffn, w_conv, w_down_b, w_up_b):
    bsz, seq, _ = x1.shape
    tt = min(seq, TT_FFN)
    n_t = seq // tt
    n_ck = DFF // FF_CH
    ext_rows = tt + CONV_HALO

    def body(ddn_ref, g_ref, gc_ref, v_ref, x1_ref, dx2_ref, sh_ref, sc_ref, gf_ref, wc_ref, wd_ref,
             wu_ref, dup_ref, dx1_ref, dsh_ref, dsc_ref, dgf_ref, dwc_ref, dbc_ref, dext):
        b = pl.program_id(0)
        i = pl.program_id(1)

        @pl.when((b == 0) & (i == 0))
        def _():
            dgf_ref[...] = jnp.zeros_like(dgf_ref)
            dwc_ref[...] = jnp.zeros_like(dwc_ref)
            dbc_ref[...] = jnp.zeros_like(dbc_ref)

        @pl.when(i == 0)
        def _():
            dsh_ref[...] = jnp.zeros_like(dsh_ref)
            dsc_ref[...] = jnp.zeros_like(dsc_ref)
            dext[:, tt:, :] = jnp.zeros((n_ck, CONV_HALO, FF_CH), F32)

        ddnv = ddn_ref[0]
        dh2 = jnp.zeros((tt, D), F32)
        for ck in range(n_ck):
            c0 = ck * FF_CH
            dact = _dot_nt(ddnv, wd_ref[c0:c0 + FF_CH, :])
            g = g_ref[0, :, c0:c0 + FF_CH].astype(F32)
            gc = gc_ref[0, :, c0:c0 + FF_CH].astype(F32)
            v = v_ref[0, :, c0:c0 + FF_CH].astype(F32)
            wc = wc_ref[:, c0:c0 + FF_CH]
            sg = jax.nn.sigmoid(gc)
            silu = gc * sg
            dv = dact * silu
            dgc = dact * v * (sg + silu * (1.0 - sg))
            dext[ck, 0:tt, :] = dgc
            de = dext[ck]
            d1 = pltpu.roll(de, ext_rows - 1, 0)[0:tt]
            d2 = pltpu.roll(de, ext_rows - 2, 0)[0:tt]
            dext[ck, tt:, :] = dgc[0:CONV_HALO, :]
            dbc_ref[:, c0:c0 + FF_CH] += _colsum(dgc)
            dwc_ref[0:1, c0:c0 + FF_CH] += _colsum(d2 * g)
            dwc_ref[1:2, c0:c0 + FF_CH] += _colsum(d1 * g)
            dwc_ref[2:3, c0:c0 + FF_CH] += _colsum(dgc * g)
            dg = wc[2:3] * dgc + wc[1:2] * d1 + wc[0:1] * d2
            dvb = dv.astype(BF16)
            dgb = dg.astype(BF16)
            dup_ref[0, :, c0:c0 + FF_CH] = dvb
            dup_ref[0, :, DFF + c0:DFF + c0 + FF_CH] = dgb
            dh2 = dh2 + _dot(dvb, wu_ref[c0:c0 + FF_CH, :]) + _dot(dgb, wu_ref[DFF + c0:DFF + c0 + FF_CH, :])
        xhat, rstd = _rms(x1_ref[0])
        gf = gf_ref[...]
        dsh_ref[0] += _colsum(dh2)
        dsc_ref[0] += _colsum(dh2 * xhat * gf)
        t = dh2 * (1.0 + sc_ref[0])
        dgf_ref[...] += _colsum(t * xhat)
        dx1_ref[0] = dx2_ref[0] + _rms_bwd(t * gf, xhat, rstd)

    def rev(b, t):
        return (b, n_t - 1 - t, 0)

    xt = pl.BlockSpec((1, tt, D), rev)
    ft = pl.BlockSpec((1, tt, DFF), rev)
    row = pl.BlockSpec((1, 1, D), lambda b, t: (b, 0, 0))
    vec = pl.BlockSpec((1, D), lambda b, t: (0, 0))
    rows = jax.ShapeDtypeStruct((bsz, 1, D), F32)
    return pl.pallas_call(
        body, name="ffn_bwd", grid=(bsz, n_t),
        in_specs=[xt, ft, ft, ft, xt, xt, row, row, vec, VMEM, VMEM, VMEM],
        out_specs=[pl.BlockSpec((1, tt, 2 * DFF), rev), xt, row, row, vec,
                   pl.BlockSpec((3, DFF), lambda b, t: (0, 0)), pl.BlockSpec((1, DFF), lambda b, t: (0, 0))],
        out_shape=[jax.ShapeDtypeStruct((bsz, seq, 2 * DFF), BF16), jax.ShapeDtypeStruct((bsz, seq, D), F32),
                   rows, rows, jax.ShapeDtypeStruct((1, D), F32), jax.ShapeDtypeStruct((3, DFF), F32),
                   jax.ShapeDtypeStruct((1, DFF), F32)],
        scratch_shapes=[pltpu.VMEM((n_ck, ext_rows, FF_CH), F32)],
        compiler_params=_params(2),
    )(ddn, gq, gcq, vq, x1, dx2, sh2, sc2, g_ffn, w_conv, w_down_b, w_up_b)


def _wgrad(a, b, bk1, bk2, name):
    n, k1 = a.shape
    _, k2 = b.shape
    tt = min(n, TT_WGRAD)

    def body(a_ref, b_ref, o_ref):
        @pl.when(pl.program_id(2) == 0)
        def _():
            o_ref[...] = jnp.zeros_like(o_ref)

        o_ref[...] += _dot_tn(a_ref[...], b_ref[...])

    return pl.pallas_call(
        body, name=name, grid=(k1 // bk1, k2 // bk2, n // tt),
        in_specs=[pl.BlockSpec((tt, bk1), lambda h, j, i: (i, h)), pl.BlockSpec((tt, bk2), lambda h, j, i: (i, j))],
        out_specs=pl.BlockSpec((bk1, bk2), lambda h, j, i: (h, j)),
        out_shape=jax.ShapeDtypeStruct((k1, k2), F32), compiler_params=_params(3),
    )(a, b)


def _mixer_out_bwd(dx1, mixcat, mixed, y2, p, gt1, w_glu_b, b_glu, w_pool_b, b_pool, pscale, w_out_b, comm=None):
    bsz, seq, _ = dx1.shape
    tt = min(seq, TT_MIX)
    n_t = seq // tt
    ext_rows = tt + POOL_HALO

    def body(dx1_ref, mc_ref, mxd_ref, y_ref, p_ref, ph_ref, gt_ref, wg_ref, bg_ref, wp_ref, bp_ref, ps_ref, wo_ref,
             dy_ref, dp_ref, dwo_ref, dwg_ref, dbg_ref, dwp_ref, dbp_ref, dps_ref, dgt_ref, ext, qext):
        b = pl.program_id(0)
        i = pl.program_id(1)
        tile = n_t - 1 - i

        @pl.when((b == 0) & (i == 0))
        def _():
            for r in (dwo_ref, dwg_ref, dbg_ref, dwp_ref, dbp_ref, dps_ref):
                r[...] = jnp.zeros_like(r)

        @pl.when(i == 0)
        def _():
            dgt_ref[...] = jnp.zeros_like(dgt_ref)
            qext[tt:, :] = jnp.zeros((POOL_HALO, POOL_W), F32)

        dx1v = dx1_ref[0]
        mc = mc_ref[0]
        dgt_ref[0] += _colsum(dx1v * mxd_ref[0].astype(F32))
        dmixed = (gt_ref[0] * dx1v).astype(BF16)
        dwo_ref[...] += _dot_tn(mc, dmixed)
        dmc = _dot_nt(dmixed, wo_ref[...])
        pv = p_ref[0]
        ext[0:POOL_HALO, :] = ph_ref[0] * (tile > 0).astype(F32)
        ext[POOL_HALO:, :] = pv
        pos = (tile * tt + lax.broadcasted_iota(jnp.int32, (tt, 1), 0) + 1).astype(F32)
        zs, zls = _pool_forward(ext[...], pv, pos, wp_ref, bp_ref)
        dzs = []
        for gi, w in enumerate(POOL_WINDOWS):
            cs = slice(gi * 128, (gi + 1) * 128)
            dyp = dmc[:, SSM_W + gi * 128:SSM_W + (gi + 1) * 128]
            dps_ref[:, cs] += _colsum(dyp * zls[gi])
            dzl = dyp * ps_ref[:, cs]
            dbp_ref[:, cs] += _colsum(dzl)
            dzlb = dzl.astype(BF16)
            dwp_ref[gi] += _dot_tn(zs[gi].astype(BF16), dzlb)
            dz = _dot_nt(dzlb, wp_ref[gi])
            dzs.append(dz)
            qext[0:tt, cs] = dz / jnp.minimum(pos, float(w))
        cur = qext[...]
        dps = []
        for gi, w in enumerate(POOL_WINDOWS):
            cur = cur + pltpu.roll(cur, ext_rows - w // 2, 0)
            dps.append(cur[0:tt, 0:128] - dzs[gi])
            if gi + 1 < len(POOL_WINDOWS):
                cur = cur[:, 128:]
        qhead = qext[0:POOL_HALO, :]
        qext[tt:, :] = qhead
        dp_ref[0] = jnp.concatenate(dps, axis=1)
        yv = y_ref[...].astype(F32)
        ab = _gelu(yv).astype(BF16)
        gl = _dot(ab, wg_ref[...]) + bg_ref[...]
        val = gl[:, :SSM_W]
        sg = jax.nn.sigmoid(gl[:, SSM_W:])
        dys = dmc[:, :SSM_W]
        dgl = jnp.concatenate([dys * sg, dys * val * sg * (1.0 - sg)], axis=1)
        dbg_ref[...] += _colsum(dgl)
        dglb = dgl.astype(BF16)
        dwg_ref[...] += _dot_tn(ab, dglb)
        dy_ref[...] = (_dot_nt(dglb, wg_ref[...]) * _gelu_grad(yv)).astype(BF16)

    def rev(b, t):
        return (b, n_t - 1 - t, 0)

    def halo(b, t):
        return (b, jnp.maximum((n_t - 1 - t) * (tt // POOL_HALO) - 1, 0), 0)

    xt = pl.BlockSpec((1, tt, D), rev)
    pt = pl.BlockSpec((1, tt, POOL_W), rev)
    yt = pl.BlockSpec((tt, SSM_W), lambda b, t: (n_t - 1 - t, b))

    def whole(shape):
        return pl.BlockSpec(shape, lambda b, t: (0,) * len(shape))

    return _fused_call(
        body, name="mixer_out_bwd", grid=(bsz, n_t),
        in_specs=[xt, xt, xt, yt, pt, pl.BlockSpec((1, POOL_HALO, POOL_W), halo),
                  pl.BlockSpec((1, 1, D), lambda b, t: (b, 0, 0)), VMEM, VMEM, VMEM, VMEM, VMEM, VMEM],
        out_specs=[yt, pt, whole((D, D)), whole((SSM_W, 2 * SSM_W)), whole((1, 2 * SSM_W)),
                   whole((4, 128, 128)), whole((1, POOL_W)), whole((1, POOL_W)),
                   pl.BlockSpec((1, 1, D), lambda b, t: (b, 0, 0))],
        out_shape=[jax.ShapeDtypeStruct(y2.shape, BF16), jax.ShapeDtypeStruct(p.shape, F32),
                   jax.ShapeDtypeStruct((D, D), F32), jax.ShapeDtypeStruct((SSM_W, 2 * SSM_W), F32),
                   jax.ShapeDtypeStruct((1, 2 * SSM_W), F32), jax.ShapeDtypeStruct((4, 128, 128), F32),
                   jax.ShapeDtypeStruct((1, POOL_W), F32), jax.ShapeDtypeStruct((1, POOL_W), F32),
                   jax.ShapeDtypeStruct((bsz, 1, D), F32)],
        scratch_shapes=[pltpu.VMEM((POOL_HALO + tt, POOL_W), F32), pltpu.VMEM((ext_rows, POOL_W), F32)],
        args=(dx1, mixcat, mixed, y2, p, p, gt1, w_glu_b, b_glu, w_pool_b, b_pool, pscale, w_out_b),
        comm=_schedule(comm, bsz * n_t))


def _ssm_bwd(dy2r, u2r, xc, xs_re, xs_im, bb, cc, lam8, d8, tlen, comm=None):
    nv = lam8.shape[1]
    rows = nv * tlen
    n_chunks = u2r.shape[0] // rows

    def body(dy_ref, u_ref, xc_ref, xre_ref, xim_ref, bb_ref, cc_ref, lam_ref, d_ref,
             du_ref, dcc_ref, dbb_ref, dlam_ref, dd_ref, s_re, s_im, g_re, g_im, gst):
        i = pl.program_id(0)

        @pl.when(i == 0)
        def _():
            for r in (gst, dcc_ref, dbb_ref, dlam_ref, dd_ref):
                r[...] = jnp.zeros_like(r)

        u = u_ref[...].astype(F32)
        dy = dy_ref[...].astype(F32)
        par0 = (lax.broadcasted_iota(jnp.int32, (rows, 1), 0) % 2) == 0
        xre = xre_ref[...]
        xim = xim_ref[...]
        s_re[0] = xc_ref[0, 0]
        s_im[0] = xc_ref[0, 1]
        s_re[pl.ds(1, tlen)] = xre.astype(F32).reshape(tlen, nv, HALF_ST)
        s_im[pl.ds(1, tlen)] = xim.astype(F32).reshape(tlen, nv, HALF_ST)
        zero = jnp.zeros_like(dy)
        dy2 = jnp.concatenate([jnp.where(par0, dy, zero), jnp.where(par0, zero, dy)], axis=1).astype(BF16)
        u2 = jnp.concatenate([jnp.where(par0, u, zero), jnp.where(par0, zero, u)], axis=1).astype(BF16)
        dcc_ref[0:HALF_ST, :] += _dot_tn(xre, dy2)
        dcc_ref[HALF_ST:, :] += _dot_tn(xim, dy2)
        for part, gref in ((0, g_re), (1, g_im)):
            for k in range(HALF_ST // 512):
                r0 = part * HALF_ST + k * 512
                gref[:, :, k * 512:(k + 1) * 512] = _dot_nt(dy2, cc_ref[r0:r0 + 512, :]).reshape(tlen, nv, 512)
        for hb in range(HALF_ST // 512):
            ls = slice(hb * 512, (hb + 1) * 512)
            lr = lam_ref[0, :, ls]
            li = lam_ref[1, :, ls]

            def bstep(k, carry, ls=ls, lr=lr, li=li):
                t = tlen - 1 - k
                gr, gi, ar, ai = carry
                ngr = g_re[t, :, ls] + lr * gr + li * gi
                ngi = g_im[t, :, ls] + lr * gi - li * gr
                g_re[t, :, ls] = ngr
                g_im[t, :, ls] = ngi
                xpr = s_re[t, :, ls]
                xpi = s_im[t, :, ls]
                return ngr, ngi, ar + ngr * xpr + ngi * xpi, ai + ngi * xpr - ngr * xpi

            init = (gst[0, :, ls], gst[1, :, ls], dlam_ref[0, :, ls], dlam_ref[1, :, ls])
            gr, gi, ar, ai = lax.fori_loop(0, tlen, bstep, init, unroll=4)
            gst[0, :, ls] = gr
            gst[1, :, ls] = gi
            dlam_ref[0, :, ls] = ar
            dlam_ref[1, :, ls] = ai
        gre = g_re[...].reshape(rows, HALF_ST).astype(BF16)
        gim = g_im[...].reshape(rows, HALF_ST).astype(BF16)
        du0 = _dot_nt(gre, bb_ref[:, 0:HALF_ST]) + _dot_nt(gim, bb_ref[:, HALF_ST:2 * HALF_ST])
        du1 = _dot_nt(gre, bb_ref[:, 2 * HALF_ST:3 * HALF_ST]) + _dot_nt(gim, bb_ref[:, 3 * HALF_ST:])
        skip = (dy.reshape(tlen, nv, HALF_CH) * d_ref[...][None]).reshape(rows, HALF_CH)
        du_ref[...] = (jnp.where(par0, du0, du1) + skip).astype(BF16)
        dbb_ref[:, 0:HALF_ST] += _dot_tn(u2, gre)
        dbb_ref[:, HALF_ST:] += _dot_tn(u2, gim)
        dd_ref[...] += jnp.sum((dy * u).reshape(tlen, nv, HALF_CH), axis=0)

        @pl.when(i == n_chunks - 1)
        def _():
            dcc_ref[HALF_ST:, :] = -dcc_ref[HALF_ST:, :]

    def rev(c):
        return (n_chunks - 1 - c, 0)

    def whole(shape):
        return pl.BlockSpec(shape, lambda c: (0,) * len(shape))

    blk = pl.BlockSpec((rows, HALF_CH), rev)
    st_blk = pl.BlockSpec((rows, HALF_ST), rev)
    return _fused_call(
        body, name="ssm_bwd", grid=(n_chunks,),
        in_specs=[blk, blk, pl.BlockSpec((1, 2, nv, HALF_ST), lambda c: (n_chunks - 1 - c, 0, 0, 0)),
                  st_blk, st_blk, VMEM, VMEM, VMEM, VMEM],
        out_specs=[blk, whole((2 * HALF_ST, SSM_W)), whole((SSM_W, 2 * HALF_ST)), whole((2, nv, HALF_ST)),
                   whole((nv, HALF_CH))],
        out_shape=[jax.ShapeDtypeStruct(u2r.shape, BF16), jax.ShapeDtypeStruct((2 * HALF_ST, SSM_W), F32),
                   jax.ShapeDtypeStruct((SSM_W, 2 * HALF_ST), F32), jax.ShapeDtypeStruct((2, nv, HALF_ST), F32),
                   jax.ShapeDtypeStruct((nv, HALF_CH), F32)],
        scratch_shapes=[pltpu.VMEM((tlen + 1, nv, HALF_ST), F32), pltpu.VMEM((tlen + 1, nv, HALF_ST), F32),
                        pltpu.VMEM((tlen, nv, HALF_ST), F32), pltpu.VMEM((tlen, nv, HALF_ST), F32),
                        pltpu.VMEM((2, nv, HALF_ST), F32)],
        args=(dy2r, u2r, xc, xs_re, xs_im, bb, cc, lam8, d8), comm=_schedule(comm, n_chunks))


def _mixer_in_bwd(du2, dp, x, dx1, sh1, sc1, g_mix, w_in_b, comm=None):
    bsz, seq, _ = x.shape
    tt = min(seq, TT_MIX)

    def body(du_ref, dp_ref, x_ref, dx1_ref, sh_ref, sc_ref, g_ref, w_ref,
             dx_ref, dw_ref, dsh_ref, dsc_ref, dg_ref):
        b = pl.program_id(0)
        ti = pl.program_id(1)

        @pl.when((b == 0) & (ti == 0))
        def _():
            dw_ref[...] = jnp.zeros_like(dw_ref)
            dg_ref[...] = jnp.zeros_like(dg_ref)

        @pl.when(ti == 0)
        def _():
            dsh_ref[...] = jnp.zeros_like(dsh_ref)
            dsc_ref[...] = jnp.zeros_like(dsc_ref)

        dz = jnp.concatenate([du_ref[...], dp_ref[0].astype(BF16)], axis=1)
        xhat, rstd = _rms(x_ref[0])
        g = g_ref[...]
        sc = sc_ref[0]
        a = xhat * g
        h = (a * (1.0 + sc) + sh_ref[0]).astype(BF16)
        dw_ref[...] += _dot_tn(h, dz)
        dh = _dot_nt(dz, w_ref[...])
        dsh_ref[0] += _colsum(dh)
        dsc_ref[0] += _colsum(dh * a)
        t = dh * (1.0 + sc)
        dg_ref[...] += _colsum(t * xhat)
        dx_ref[0] = dx1_ref[0] + _rms_bwd(t * g, xhat, rstd)

    xt = pl.BlockSpec((1, tt, D), lambda b, t: (b, t, 0))
    row = pl.BlockSpec((1, 1, D), lambda b, t: (b, 0, 0))
    vec = pl.BlockSpec((1, D), lambda b, t: (0, 0))
    rows = jax.ShapeDtypeStruct((bsz, 1, D), F32)
    return _fused_call(
        body, name="mixer_in_bwd", grid=(bsz, seq // tt),
        in_specs=[pl.BlockSpec((tt, SSM_W), lambda b, t: (t, b)),
                  pl.BlockSpec((1, tt, POOL_W), lambda b, t: (b, t, 0)), xt, xt, row, row, vec, VMEM],
        out_specs=[xt, pl.BlockSpec((D, D), lambda b, t: (0, 0)), row, row, vec],
        out_shape=[jax.ShapeDtypeStruct(x.shape, F32), jax.ShapeDtypeStruct((D, D), F32), rows, rows,
                   jax.ShapeDtypeStruct((1, D), F32)],
        scratch_shapes=[], args=(du2, dp, x, dx1, sh1, sc1, g_mix, w_in_b),
        comm=_schedule(comm, bsz * (seq // tt)))


def kernel(x, c, w_ada, b_ada, g_norm_mix, w_in, ssm_lam_re, ssm_lam_im, ssm_log_dt, ssm_b_re, ssm_b_im, ssm_c_re, ssm_c_im, ssm_d, w_glu, b_glu, w_pool, b_pool, pool_scale, w_out, g_norm_ffn, w_up, w_conv, b_conv, w_down, g_norm_final, loss_target, m_w_ada, m_b_ada, m_g_norm_mix, m_w_in, m_ssm_lam_re, m_ssm_lam_im, m_ssm_log_dt, m_ssm_b_re, m_ssm_b_im, m_ssm_c_re, m_ssm_c_im, m_ssm_d, m_w_glu, m_b_glu, m_w_pool, m_b_pool, m_pool_scale, m_w_out, m_g_norm_ffn, m_w_up, m_w_conv, m_b_conv, m_w_down, m_g_norm_final, v_w_ada, v_b_ada, v_g_norm_mix, v_w_in, v_ssm_lam_re, v_ssm_lam_im, v_ssm_log_dt, v_ssm_b_re, v_ssm_b_im, v_ssm_c_re, v_ssm_c_im, v_ssm_d, v_w_glu, v_b_glu, v_w_pool, v_b_pool, v_pool_scale, v_w_out, v_g_norm_ffn, v_w_up, v_w_conv, v_b_conv, v_w_down, v_g_norm_final):
    bsz, seq, _ = x.shape
    assert 2 * bsz == 8 and seq % 128 == 0
    px, py, pc = _my_place()
    me = 4 * px + 2 * py + pc
    place = jnp.stack([pc, 2 * px + py]).astype(jnp.int32)
    ncol = ADA_COLS

    cpad = jnp.zeros((16, D), F32).at[0:bsz].set(c).at[8:11, 0:352].set(w_conv[0])
    cg, c_all, mod8, (g_in,) = _ada_fwd(cpad, w_ada[0], b_ada.reshape(N_DEV, 1, ncol), [w_in[0].astype(BF16)])
    w_conv_f = cg[:, 8:11, 0:352].transpose(1, 0, 2).reshape(3, DFF)
    w_in_b = g_in.reshape(D, D)
    sh1, sc1, gt1, sh2, sc2, gt2 = [mod8[0:bsz, k * D:(k + 1) * D].reshape(bsz, 1, D) for k in range(N_MOD)]

    lam_r = ssm_lam_re[0].reshape(1, GRP * NST)
    lam_i = ssm_lam_im[0].reshape(1, GRP * NST)
    ldt = jnp.repeat(ssm_log_dt[0], NST).reshape(1, GRP * NST)
    b_r = ssm_b_re[0].transpose(2, 0, 1).reshape(GCH, GRP * NST)
    b_i = ssm_b_im[0].transpose(2, 0, 1).reshape(GCH, GRP * NST)
    lbr, lbi, bbr, bbi = _ssm_prep(lam_r, lam_i, ldt, b_r, b_i)
    lam8 = jnp.stack([jnp.tile(lbr.reshape(2, HALF_ST), (bsz, 1)), jnp.tile(lbi.reshape(2, HALF_ST), (bsz, 1))])
    bd_r = _blockdiag(bbr.reshape(GCH, 2, GRP // 2, NST).transpose(1, 2, 0, 3))
    bd_i = _blockdiag(bbi.reshape(GCH, 2, GRP // 2, NST).transpose(1, 2, 0, 3))
    bb = jnp.concatenate([bd_r[0], bd_i[0], bd_r[1], bd_i[1]], axis=1).astype(BF16)
    cd_r = _blockdiag(ssm_c_re[0].reshape(2, GRP // 2, GCH, NST).transpose(0, 1, 3, 2))
    cd_i = _blockdiag(ssm_c_im[0].reshape(2, GRP // 2, GCH, NST).transpose(0, 1, 3, 2))
    cc = jnp.concatenate([jnp.concatenate([cd_r[0], cd_r[1]], axis=1),
                          jnp.concatenate([-cd_i[0], -cd_i[1]], axis=1)], axis=0).astype(BF16)
    d8 = jnp.tile(ssm_d[0].reshape(2, HALF_CH), (bsz, 1))

    tlen = min(seq, T_SSM)
    (u2, p), ((g_glu, g_out),) = _mixer_in_fwd(
        x, sh1, sc1, g_norm_mix, w_in_b, comm=[(_gather_plan, [w_glu[0].astype(BF16), w_out[0].astype(BF16)])])
    w_glu_b = g_glu.transpose(1, 0, 2).reshape(SSM_W, 2 * SSM_W)
    w_out_b = g_out.reshape(D, D)
    u2r = u2.reshape(seq * 2 * bsz, HALF_CH)
    (y2r, xc, xs_re, xs_im), ((g_up, g_down),) = _ssm_fwd(
        u2r, bb, cc, lam8, d8, tlen, comm=[(_gather_plan, [w_up[0].T.astype(BF16), w_down[0].astype(BF16)])])
    w_up_b = g_up.reshape(2 * DFF, D)
    w_down_b = g_down.reshape(DFF, D)
    y2 = y2r.reshape(seq, bsz * SSM_W)
    w_pool_b = w_pool[0].astype(BF16)
    bp = b_pool[0].reshape(1, POOL_W)
    x1, mixcat, mixed = _mixer_out_fwd(y2, p, x, gt1, w_glu_b, b_glu, w_pool_b, bp, pool_scale, w_out_b)
    h2, vq, gq, gcq, act, ddn, dx2, loss_l, dg_fin, dgt2 = _ffn_fwd(
        x1, loss_target, sh2, sc2, gt2, g_norm_ffn, w_up_b, w_conv_f, b_conv, w_down_b, g_norm_final.reshape(1, D))

    dup, dx1, dsh2, dsc2, dg_ffn, dw_conv, db_conv = _ffn_bwd(
        ddn, gq, gcq, vq, x1, dx2, sh2, sc2, g_norm_ffn, w_conv_f, w_down_b, w_up_b)
    ntok = bsz * seq
    dw_up_t = _wgrad(dup.reshape(ntok, 2 * DFF), h2.reshape(ntok, D), DFF // 2, D, "wgrad_up")
    dw_down = _wgrad(act.reshape(ntok, DFF), ddn.reshape(ntok, D), DFF, 512, "wgrad_down")
    g42_up = dw_up_t.reshape(4, 2, 704, D)
    g42_down = dw_down.reshape(4, 2, 352, D)
    (dy2, dp, dw_out, dw_glu, db_glu, dw_pool, db_pool, dpscale, dgt1), ((ra_up, ra_down),) = _mixer_out_bwd(
        dx1, mixcat, mixed, y2, p, gt1, w_glu_b, b_glu, w_pool_b, bp, pool_scale, w_out_b,
        comm=[(_pair_plan, [g42_up, g42_down])])
    own_up, s_up = _pair_sum(g42_up, ra_up, place, "pair_sum_up")
    own_down, s_down = _pair_sum(g42_down, ra_down, place, "pair_sum_down")
    g42_glu = dw_glu.reshape(SSM_W, N_DEV, 128).transpose(1, 0, 2).reshape(4, 2, SSM_W, 128)
    g42_out = dw_out.reshape(4, 2, 128, D)
    small_a = [
        ("b_glu", (1, 2 * SSM_W), db_glu), ("w_pool", (POOL_W, 128), dw_pool.reshape(POOL_W, 128)),
        ("b_pool", (4, 128), db_pool.reshape(4, 128)), ("pool_scale", (1, POOL_W), dpscale),
        ("g_norm_ffn", (1, D), dg_ffn), ("b_conv", (1, DFF), db_conv), ("g_norm_final", (1, D), dg_fin)]
    (du2r, dcc, dbb, dlam8, dd8), ((rc_up, rc_down), (ra_glu, ra_out), parts_a) = _ssm_bwd(
        dy2.reshape(u2r.shape), u2r, xc, xs_re, xs_im, bb, cc, lam8, d8, tlen,
        comm=[(_chip_plan, [s_up, s_down]), (_pair_plan, [g42_glu, g42_out]),
              (_gather_plan, [g for _, _, g in small_a] + [dw_conv, loss_l])])
    big_up = [t.T for t in _final_sum_adamw(own_up, rc_up, w_up[0].T, m_w_up[0].T, v_w_up[0].T, "final_adamw_up")]
    big_down = _final_sum_adamw(own_down, rc_down, w_down[0], m_w_down[0], v_w_down[0], "final_adamw_down")
    own_glu, s_glu = _pair_sum(g42_glu, ra_glu, place, "pair_sum_glu")
    own_out, s_out = _pair_sum(g42_out, ra_out, place, "pair_sum_out")

    def take_c(t):
        return _blockdiag_take(t, NST, GCH).transpose(0, 2, 1)

    dc_re = jnp.concatenate([take_c(dcc[0:HALF_ST, e * HALF_CH:(e + 1) * HALF_CH]) for e in range(2)], axis=0)
    dc_im = jnp.concatenate([take_c(dcc[HALF_ST:, e * HALF_CH:(e + 1) * HALF_CH]) for e in range(2)], axis=0)

    def take_b(t):
        return _blockdiag_take(t, GCH, NST).transpose(1, 0, 2)

    dbbr = jnp.concatenate([take_b(dbb[e * HALF_CH:(e + 1) * HALF_CH, 0:HALF_ST]) for e in range(2)], axis=1)
    dbbi = jnp.concatenate([take_b(dbb[e * HALF_CH:(e + 1) * HALF_CH, HALF_ST:]) for e in range(2)], axis=1)
    glr, gli, gldt, gbr, gbi, gd = _ssm_param_bwd(
        lam_r, lam_i, ldt, b_r, b_i, dlam8, dbbr.reshape(GCH, GRP * NST), dbbi.reshape(GCH, GRP * NST), dd8)
    g_log_dt = jnp.sum(gldt.reshape(GRP, NST), axis=1)

    def view(a, shp):
        return a.reshape(shp)

    small_b = [
        ("ssm_lam_re", (GRP, NST), glr.reshape(GRP, NST)), ("ssm_lam_im", (GRP, NST), gli.reshape(GRP, NST)),
        ("ssm_log_dt", (1, GRP), g_log_dt.reshape(1, GRP)),
        ("ssm_c_re", (GRP * GCH, NST), dc_re.reshape(GRP * GCH, NST)),
        ("ssm_c_im", (GRP * GCH, NST), dc_im.reshape(GRP * GCH, NST)), ("ssm_d", (1, SSM_W), gd)]
    small = small_a + small_b
    given = dict(
        ssm_lam_re=(ssm_lam_re, m_ssm_lam_re, v_ssm_lam_re), ssm_lam_im=(ssm_lam_im, m_ssm_lam_im, v_ssm_lam_im),
        ssm_log_dt=(ssm_log_dt, m_ssm_log_dt, v_ssm_log_dt), ssm_c_re=(ssm_c_re, m_ssm_c_re, v_ssm_c_re),
        ssm_c_im=(ssm_c_im, m_ssm_c_im, v_ssm_c_im), ssm_d=(ssm_d, m_ssm_d, v_ssm_d), b_glu=(b_glu, m_b_glu, v_b_glu),
        w_pool=(w_pool, m_w_pool, v_w_pool), b_pool=(b_pool, m_b_pool, v_b_pool),
        pool_scale=(pool_scale, m_pool_scale, v_pool_scale), g_norm_ffn=(g_norm_ffn, m_g_norm_ffn, v_g_norm_ffn),
        b_conv=(b_conv, m_b_conv, v_b_conv), g_norm_final=(g_norm_final, m_g_norm_final, v_g_norm_final),
        ssm_b_re=(ssm_b_re, m_ssm_b_re, v_ssm_b_re), ssm_b_im=(ssm_b_im, m_ssm_b_im, v_ssm_b_im))
    b_view = (GRP * NST, GCH)
    (grad_x, dw_in, dsh1, dsc1, dg_mix), ((rc_glu, rc_out), parts_b) = _mixer_in_bwd(
        du2r.reshape(u2.shape), dp, x, dx1, sh1, sc1, g_norm_mix, w_in_b,
        comm=[(_chip_plan, [s_glu, s_out]), (_gather_plan, [g for _, _, g in small_b] + [gbr, gbi])])
    big_glu = _final_sum_adamw(own_glu, rc_glu, w_glu[0], m_w_glu[0], v_w_glu[0], "final_adamw_glu")
    big_out = _final_sum_adamw(own_out, rc_out, w_out[0], m_w_out[0], v_w_out[0], "final_adamw_out")
    parts = list(parts_a[:-2]) + list(parts_b[:-2])
    items = [(pt,) + tuple(view(a, shp) for a in given[nm]) for pt, (nm, shp, _) in zip(parts, small)]
    small_out, (g_conv_full, loss_all, gbr_all, gbi_all) = _small_sum_adamw(
        items, [parts_a[-2], parts_a[-1], parts_b[-2], parts_b[-1]])
    loss = loss_all[0, 0]
    result = {nm: [t.reshape(given[nm][0].shape) for t in quad] for quad, (nm, _, _) in zip(small_out, small)}
    for nm, g_all in (("ssm_b_re", gbr_all), ("ssm_b_im", gbi_all)):
        quad = [g_all.T] + list(_adamw_plain(g_all.T, *[view(a, b_view) for a in given[nm]]))
        result[nm] = [t.reshape(given[nm][0].shape) for t in quad]
    g_w_conv = lax.dynamic_slice_in_dim(g_conv_full, 352 * me, 352, axis=1)
    result["w_conv"] = [g_w_conv[None]] + [t[None] for t in _adamw_plain(g_w_conv, w_conv[0], m_w_conv[0], v_w_conv[0])]

    for nm, quad in (("w_glu", big_glu), ("w_out", big_out), ("w_up", big_up), ("w_down", big_down)):
        result[nm] = [t[None] for t in quad]

    dmod = jnp.concatenate([t.reshape(bsz, D) for t in (dsh1, dsc1, dgt1, dsh2, dsc2, dgt2)], axis=1)
    dmod_blk = jnp.zeros((N_DEV, 8, ncol), F32).at[:, 0:bsz].set(dmod.reshape(bsz, N_DEV, ncol).transpose(1, 0, 2))
    dmod_blk = dmod_blk.at[0, ADA_RIDER_ROW].set(dg_mix[0, 0:ncol]).at[1, ADA_RIDER_ROW, 0:D - ncol].set(dg_mix[0, ncol:])
    ada = _ada_bwd(dmod_blk.reshape(ADA_ROWS, ncol), c_all, w_ada[0], m_w_ada[0], v_w_ada[0],
                   b_ada, m_b_ada, v_b_ada, g_norm_mix, m_g_norm_mix, v_g_norm_mix,
                   dw_in.reshape(4, 2, 128, D), w_in[0], m_w_in[0], v_w_in[0])
    result["w_ada"] = [t[None] for t in ada[0:4]]
    result["b_ada"] = list(ada[4:8])
    result["g_norm_mix"] = list(ada[8:12])
    result["w_in"] = [t[None] for t in ada[12:16]]

    names = ["w_ada", "b_ada", "g_norm_mix", "w_in", "ssm_lam_re", "ssm_lam_im", "ssm_log_dt", "ssm_b_re", "ssm_b_im",
             "ssm_c_re", "ssm_c_im", "ssm_d", "w_glu", "b_glu", "w_pool", "b_pool", "pool_scale", "w_out", "g_norm_ffn",
             "w_up", "w_conv", "b_conv", "w_down", "g_norm_final"]
    return (loss, grad_x, *[result[nm][k] for k in range(4) for nm in names])
```

```python
import functools
import math

import jax
import jax.numpy as jnp
from jax import lax
from jax.experimental import pallas as pl
from jax.experimental.pallas import tpu as pltpu

F32 = jnp.float32
BF16 = jnp.bfloat16

D = 1024
SSM_W = 512
POOL_W = 512
GRP = 32
GCH = 16
NST = 64
HALF_ST = GRP * NST // 2
HALF_CH = SSM_W // 2
DFF = 2816
FF_CH = 2816
N_MOD = 6
N_DEV = 8
EPS = 1e-6
POOL_WINDOWS = (2, 4, 8, 16)
POOL_HALO = 16
CONV_HALO = 8
GELU_C = math.sqrt(2.0 / math.pi)
GELU_A = 0.044715

ADAM_LR = 0.001
ADAM_B1 = 0.9
ADAM_B2 = 0.999
ADAM_EPS = 1e-08
ADAM_WD = 0.01
ADAM_STEP = 10

VMEM_LIMIT = 56 * 1024 * 1024
TT_MIX = 1024
TT_FFN = 256
T_SSM = 128
TT_WGRAD = 2048
MESH = pl.DeviceIdType.MESH
NT = (((1,), (1,)), ((), ()))
TN = (((0,), (0,)), ((), ()))
ANY = pl.BlockSpec(memory_space=pl.ANY)
VMEM = pl.BlockSpec(memory_space=pltpu.VMEM)


def _params(n_grid, vmem=VMEM_LIMIT):
    return pltpu.CompilerParams(dimension_semantics=("arbitrary",) * n_grid, vmem_limit_bytes=vmem)


def _dot(a, b):
    return jnp.dot(a, b, preferred_element_type=F32)


def _dot_nt(a, b):
    return lax.dot_general(a, b, NT, preferred_element_type=F32)


def _dot_tn(a, b):
    return lax.dot_general(a, b, TN, preferred_element_type=F32)


def _colsum(a):
    return jnp.sum(a, axis=0, keepdims=True)


def _rms(x):
    rstd = lax.rsqrt(jnp.mean(x * x, axis=-1, keepdims=True) + EPS)
    return x * rstd, rstd


def _rms_bwd(dxhat, xhat, rstd):
    return rstd * (dxhat - xhat * jnp.mean(dxhat * xhat, axis=-1, keepdims=True))


def _gelu(x):
    return 0.5 * x * (1.0 + jnp.tanh(GELU_C * (x + GELU_A * x * x * x)))


def _gelu_grad(x):
    x2 = x * x
    th = jnp.tanh(GELU_C * (x + GELU_A * x * x2))
    return 0.5 * (1.0 + th) + 0.5 * x * (1.0 - th * th) * GELU_C * (1.0 + 3.0 * GELU_A * x2)


def _adamw(w, g, m, v):
    m = ADAM_B1 * m + (1.0 - ADAM_B1) * g
    v = ADAM_B2 * v + (1.0 - ADAM_B2) * (g * g)
    m_hat = m / (1.0 - ADAM_B1 ** ADAM_STEP)
    v_hat = v / (1.0 - ADAM_B2 ** ADAM_STEP)
    delta = -ADAM_LR * (m_hat / (jnp.sqrt(v_hat) + ADAM_EPS) + ADAM_WD * w)
    return delta, m, v


def _my_place():
    return lax.axis_index("x"), lax.axis_index("y"), lax.axis_index("c")


def _gather_plan(shards):
    n = len(shards)
    out_shape = [jax.ShapeDtypeStruct((N_DEV,) + tuple(s.shape), s.dtype) for s in shards]
    scratch = [pltpu.SemaphoreType.DMA((n, 7)), pltpu.SemaphoreType.DMA((n, 7)), pltpu.SemaphoreType.DMA((n,))]

    def stages(x_refs, out_refs, sems):
        send_sems, recv_sems, local_sems = sems
        x, y, c = _my_place()
        me, sibling = (x, y, c), (x, y, 1 - c)
        chips = [(1 - x, y), (x, 1 - y), (1 - x, 1 - y)]

        def copy(i, k, block, to, own=False):
            px, py, pc = block
            dst = out_refs[i].at[4 * px + 2 * py + pc]
            return pltpu.make_async_remote_copy(
                src_ref=x_refs[i] if own else dst, dst_ref=dst, send_sem=send_sems.at[i, k],
                recv_sem=recv_sems.at[i, k], device_id=to, device_id_type=MESH)

        def mine(i):
            return pltpu.make_async_copy(x_refs[i], out_refs[i].at[4 * x + 2 * y + c], local_sems.at[i])

        def start():
            for i in range(n):
                mine(i).start()
                copy(i, 0, me, sibling, own=True).start()
                for j, chip in enumerate(chips):
                    copy(i, 1 + j, me, (*chip, c), own=True).start()

        def forward():
            for i in range(n):
                for j, chip in enumerate(chips):
                    copy(i, 1 + j, (*chip, c), me).wait_recv()
                    copy(i, 4 + j, (*chip, c), sibling).start()

        def finish():
            for i in range(n):
                copy(i, 0, sibling, me).wait_recv()
                copy(i, 0, me, sibling, own=True).wait_send()
                for j, chip in enumerate(chips):
                    copy(i, 4 + j, (*chip, 1 - c), me).wait_recv()
                    copy(i, 1 + j, me, (*chip, c), own=True).wait_send()
                    copy(i, 4 + j, (*chip, c), sibling).wait_send()
                mine(i).wait()

        return [start, forward, finish]

    return n, out_shape, scratch, stages


def _pair_plan(g42s):
    n = len(g42s)
    out_shape = [jax.ShapeDtypeStruct((4,) + tuple(g.shape[2:]), g.dtype) for g in g42s]
    scratch = [pltpu.SemaphoreType.DMA((n,)), pltpu.SemaphoreType.DMA((n,))]

    def stages(g_refs, out_refs, sems):
        send_sems, recv_sems = sems
        x, y, c = _my_place()

        def copy(i):
            return pltpu.make_async_remote_copy(
                src_ref=g_refs[i].at[:, 1 - c], dst_ref=out_refs[i], send_sem=send_sems.at[i],
                recv_sem=recv_sems.at[i], device_id=(x, y, 1 - c), device_id_type=MESH)

        def start():
            for i in range(n):
                copy(i).start()

        def finish():
            for i in range(n):
                copy(i).wait()

        return [start, finish]

    return n, out_shape, scratch, stages


def _chip_plan(s4s):
    n = len(s4s)
    out_shape = [jax.ShapeDtypeStruct((3,) + tuple(s.shape[1:]), s.dtype) for s in s4s]
    scratch = [pltpu.SemaphoreType.DMA((n, 3)), pltpu.SemaphoreType.DMA((n, 3))]

    def stages(s_refs, out_refs, sems):
        send_sems, recv_sems = sems
        x, y, c = _my_place()

        def copy(i, d):
            px, py = x ^ (d >> 1), y ^ (d & 1)
            return pltpu.make_async_remote_copy(
                src_ref=s_refs[i].at[2 * px + py], dst_ref=out_refs[i].at[d - 1], send_sem=send_sems.at[i, d - 1],
                recv_sem=recv_sems.at[i, d - 1], device_id=(px, py, c), device_id_type=MESH)

        def start():
            for i in range(n):
                for d in (1, 2, 3):
                    copy(i, d).start()

        def finish():
            for i in range(n):
                for d in (1, 2, 3):
                    copy(i, d).wait()

        return [start, finish]

    return n, out_shape, scratch, stages


def _comm_call(plan, arrays, name):
    n, out_shape, scratch, stages = plan

    def body(*refs):
        for stage in stages(refs[:n], refs[n:2 * n], refs[2 * n:]):
            stage()

    return pl.pallas_call(
        body, name=name, out_shape=out_shape, in_specs=[ANY] * n, out_specs=[ANY] * n, scratch_shapes=scratch,
    )(*arrays)


def _fused_call(body, *, name, grid, in_specs, out_specs, out_shape, scratch_shapes, args, comm=None):
    if not comm:
        out = pl.pallas_call(body, name=name, grid=grid, in_specs=in_specs, out_specs=out_specs, out_shape=out_shape,
                             scratch_shapes=scratch_shapes, compiler_params=_params(len(grid)))(*args)
        return out, []
    counts = [plan[0] for plan, _, _ in comm]
    n = sum(counts)
    n_in, n_out, n_scr = len(in_specs), len(out_specs), len(scratch_shapes)

    def fused(*refs):
        ins, refs = refs[:n_in], refs[n_in:]
        c_ins, refs = refs[:n], refs[n:]
        outs, refs = refs[:n_out], refs[n_out:]
        c_outs, refs = refs[:n], refs[n:]
        scr, c_scr = refs[:n_scr], refs[n_scr:]
        step = pl.program_id(0)
        for k in range(1, len(grid)):
            step = step * grid[k] + pl.program_id(k)
        todo, a0, s0 = [], 0, 0
        for (cnt, _, plan_scratch, stages), _, steps in comm:
            sems = c_scr[s0:s0 + len(plan_scratch)]
            todo += list(zip(stages(c_ins[a0:a0 + cnt], c_outs[a0:a0 + cnt], sems), steps))
            a0 += cnt
            s0 += len(plan_scratch)
        for stage, at in todo:
            if at == 0:
                pl.when(step == 0)(stage)
        body(*ins, *outs, *scr)
        for stage, at in todo:
            if at != 0:
                pl.when(step == at)(stage)

    c_shape = [s for plan, _, _ in comm for s in plan[1]]
    c_scratch = [s for plan, _, _ in comm for s in plan[2]]
    arrays = [a for _, arrs, _ in comm for a in arrs]
    out = pl.pallas_call(
        fused, name=name, grid=grid, in_specs=list(in_specs) + [ANY] * n, out_specs=list(out_specs) + [ANY] * n,
        out_shape=list(out_shape) + c_shape, scratch_shapes=list(scratch_shapes) + c_scratch,
        compiler_params=_params(len(grid)))(*args, *arrays)
    outs, c_outs, split, a0 = out[:n_out], out[n_out:], [], 0
    for cnt in counts:
        split.append(c_outs[a0:a0 + cnt])
        a0 += cnt
    return outs, split


def _schedule(comm, n_steps):
    out = []
    for make_plan, arrays, *small in comm or []:
        middle = n_steps // 2 if small else (3 * n_steps) // 4
        steps = [0, middle, n_steps - 1] if make_plan is _gather_plan else [0, n_steps - 1]
        out.append((make_plan(arrays), arrays, steps))
    return out


def _row_tile(r):
    for t in (128, 64, 32, 16, 8):
        if r % t == 0:
            return t
    return r


def _pair_sum(g42, recv, place, name):
    _, _, r, cdim = g42.shape
    tr = _row_tile(r)

    def body(pl_ref, g_ref, r_ref, own_ref, s_ref):
        s_ref[...] = (g_ref[:, 0] + r_ref[...]).astype(BF16)
        q = pl_ref[1]
        own_ref[...] = g_ref[q, 0] + r_ref[q]

    return pl.pallas_call(
        body, name=name,
        grid_spec=pltpu.PrefetchScalarGridSpec(
            num_scalar_prefetch=1, grid=(r // tr,),
            in_specs=[pl.BlockSpec((4, 1, tr, cdim), lambda i, p: (0, p[0], i, 0)),
                      pl.BlockSpec((4, tr, cdim), lambda i, p: (0, i, 0))],
            out_specs=[pl.BlockSpec((tr, cdim), lambda i, p: (i, 0)),
                       pl.BlockSpec((4, tr, cdim), lambda i, p: (0, i, 0))]),
        out_shape=[jax.ShapeDtypeStruct((r, cdim), F32), jax.ShapeDtypeStruct((4, r, cdim), BF16)],
        compiler_params=_params(1),
    )(place, g42, recv)


def _final_sum_adamw(own, recv3, w, m, v, name):
    r, cdim = w.shape
    tr = _row_tile(r)

    def body(s_ref, r_ref, w_ref, m_ref, v_ref, g_out, d_out, m_out, v_out):
        g = s_ref[...] + r_ref[0].astype(F32) + r_ref[1].astype(F32) + r_ref[2].astype(F32)
        d, mn, vn = _adamw(w_ref[...], g, m_ref[...], v_ref[...])
        g_out[...] = g
        d_out[...] = d
        m_out[...] = mn
        v_out[...] = vn

    blk = pl.BlockSpec((tr, cdim), lambda i: (i, 0))
    shp = jax.ShapeDtypeStruct((r, cdim), F32)
    return pl.pallas_call(
        body, name=name, grid=(r // tr,),
        in_specs=[blk, pl.BlockSpec((3, tr, cdim), lambda i: (0, i, 0)), blk, blk, blk],
        out_specs=[blk, blk, blk, blk], out_shape=[shp, shp, shp, shp], compiler_params=_params(1),
    )(own, recv3, w, m, v)


def _small_sum_adamw(items, sums_only):
    n, ne = len(items), len(sums_only)

    def total(p_ref):
        g = p_ref[0]
        for k in range(1, N_DEV):
            g = g + p_ref[k]
        return g

    def body(*refs):
        ins, outs = refs[:4 * n + ne], refs[4 * n + ne:]
        for i in range(n):
            p_ref, w_ref, m_ref, v_ref = ins[4 * i:4 * i + 4]
            g = total(p_ref)
            d, mn, vn = _adamw(w_ref[...], g, m_ref[...], v_ref[...])
            for o_ref, val in zip(outs[4 * i:4 * i + 4], (g, d, mn, vn)):
                o_ref[...] = val
        for j in range(ne):
            outs[4 * n + j][...] = total(ins[4 * n + j])

    args = [a for item in items for a in item] + list(sums_only)
    shapes = [jax.ShapeDtypeStruct(w.shape, F32) for _, w, _, _ in items for _ in range(4)]
    shapes += [jax.ShapeDtypeStruct(p.shape[1:], F32) for p in sums_only]
    out = pl.pallas_call(
        body, name="small_sum_adamw", in_specs=[VMEM] * len(args), out_specs=[VMEM] * len(shapes), out_shape=shapes,
        compiler_params=_params(0),
    )(*args)
    return [out[4 * i:4 * i + 4] for i in range(n)], out[4 * n:]


def _adamw_plain(g, w, m, v):
    r, cdim = w.shape
    tr = r if r * cdim <= 64 * 1024 else _row_tile(r)

    def body(g_ref, w_ref, m_ref, v_ref, d_out, m_out, v_out):
        d, mn, vn = _adamw(w_ref[...], g_ref[...], m_ref[...], v_ref[...])
        d_out[...] = d
        m_out[...] = mn
        v_out[...] = vn

    blk = pl.BlockSpec((tr, cdim), lambda i: (i, 0))
    shp = jax.ShapeDtypeStruct((r, cdim), F32)
    return pl.pallas_call(
        body, name="adamw_plain", grid=(r // tr,), in_specs=[blk, blk, blk, blk],
        out_specs=[blk, blk, blk], out_shape=[shp, shp, shp], compiler_params=_params(1),
    )(g, w, m, v)


ADA_COLS = N_MOD * D // N_DEV
ADA_ROWS = 8 * N_DEV


def _ada_fwd(cpad, w_ada, b_blocks, mixer_shards):
    n_w, w_shape, w_scr, w_stages = _gather_plan(mixer_shards)
    _, _, c_scr, c_stages = _gather_plan([cpad])
    _, _, p_scr, p_stages = _gather_plan([jax.ShapeDtypeStruct((ADA_ROWS, ADA_COLS), F32)])

    def body(c_ref, wa_ref, b_ref, *refs):
        w_refs, refs = refs[:n_w], refs[n_w:]
        cg_ref, call_ref, mod_ref = refs[:3]
        wg_refs, refs = refs[3:3 + n_w], refs[3 + n_w:]
        part_ref, pg_ref = refs[:2]
        c_sems, p_sems, w_sems = refs[2:5], refs[5:8], refs[8:11]
        w_start, w_forward, w_finish = w_stages(w_refs, wg_refs, w_sems)
        w_start()
        for stage in c_stages([c_ref], [cg_ref], c_sems):
            stage()
        cv = cg_ref[:, 0:8, :].reshape(ADA_ROWS, D)
        call_ref[...] = cv
        part_ref[...] = _dot(cv * jax.nn.sigmoid(cv), wa_ref[...])
        for stage in p_stages([part_ref], [pg_ref], p_sems):
            stage()
        x, y, c = _my_place()
        r0 = pl.multiple_of(8 * (4 * x + 2 * y + c), 8)
        for k in range(N_DEV):
            mod_ref[:, k * ADA_COLS:(k + 1) * ADA_COLS] = pg_ref[k, pl.ds(r0, 8), :] + b_ref[k]
        w_forward()
        w_finish()

    out = pl.pallas_call(
        body, name="ada_fwd", in_specs=[VMEM, VMEM, VMEM] + [ANY] * n_w,
        out_specs=[VMEM, VMEM, VMEM] + [ANY] * n_w,
        out_shape=[jax.ShapeDtypeStruct((N_DEV,) + cpad.shape, F32), jax.ShapeDtypeStruct((ADA_ROWS, D), F32),
                   jax.ShapeDtypeStruct((8, N_MOD * D), F32)] + list(w_shape),
        scratch_shapes=[pltpu.VMEM((ADA_ROWS, ADA_COLS), F32), pltpu.VMEM((N_DEV, ADA_ROWS, ADA_COLS), F32)]
        + list(c_scr) + list(p_scr) + list(w_scr),
        compiler_params=_params(0),
    )(cpad, w_ada, b_blocks, *mixer_shards)
    return out[0], out[1], out[2], out[3:]


ADA_RIDER_ROW = 4


def _ada_bwd(dmod_blk, c_all, w_ada, m_w, v_w, b_blocks, m_b, v_b, g_w, g_m, g_v, g42, w_l, m_l, v_l):
    _, _, g_scr, g_stages = _gather_plan([dmod_blk])
    rest = D - ADA_COLS
    blk = tuple(g42.shape[2:])

    def body(dm_ref, c_ref, w_ref, mw_ref, vw_ref, b_ref, mb_ref, vb_ref, gw_ref, gm_ref, gv_ref,
             g42_ref, wl_ref, ml_ref, vl_ref,
             gw_o, dw_o, mw_o, vw_o, gb_o, dbb_o, mb_o, vb_o, gg_o, dgg_o, mg_o, vg_o, gl_o, dl_o, ml_o, vl_o,
             dg_ref, pr_ref, sbf_ref, rc_ref, pair_send, pair_recv, chip_send, chip_recv, *sems):
        x, y, c = _my_place()
        q = 2 * x + y
        pair = pltpu.make_async_remote_copy(
            src_ref=g42_ref.at[:, 1 - c], dst_ref=pr_ref, send_sem=pair_send, recv_sem=pair_recv,
            device_id=(x, y, 1 - c), device_id_type=MESH)
        pair.start()
        for stage in g_stages([dm_ref], [dg_ref], sems):
            stage()
        pair.wait()
        for k in range(4):
            sbf_ref[k] = (g42_ref[k, c] + pr_ref[k]).astype(BF16)

        def chip_copy(dist):
            px, py = x ^ (dist >> 1), y ^ (dist & 1)
            return pltpu.make_async_remote_copy(
                src_ref=sbf_ref.at[2 * px + py], dst_ref=rc_ref.at[dist - 1], send_sem=chip_send.at[dist - 1],
                recv_sem=chip_recv.at[dist - 1], device_id=(px, py, c), device_id_type=MESH)

        for dist in (1, 2, 3):
            chip_copy(dist).start()
        r0 = pl.multiple_of(8 * (4 * x + 2 * y + c), 8)
        cols = dg_ref[:, pl.ds(r0, 8), :].reshape(ADA_ROWS, ADA_COLS)
        cv = c_ref[...]
        gw = _dot_tn(cv * jax.nn.sigmoid(cv), cols)
        d, mn, vn = _adamw(w_ref[...], gw, mw_ref[...], vw_ref[...])
        gw_o[...] = gw
        dw_o[...] = d
        mw_o[...] = mn
        vw_o[...] = vn
        is_example = lax.broadcasted_iota(jnp.int32, (8, 1), 0) < ADA_RIDER_ROW
        blocks = []
        for k in range(N_DEV):
            s = dg_ref[0, 8 * k:8 * k + 8, :]
            for dev in range(1, N_DEV):
                s = s + dg_ref[dev, 8 * k:8 * k + 8, :]
            blocks.append(s)
            gb = _colsum(jnp.where(is_example, s, 0.0))
            cs = slice(k * ADA_COLS, (k + 1) * ADA_COLS)
            d, mn, vn = _adamw(b_ref[:, cs], gb, mb_ref[:, cs], vb_ref[:, cs])
            gb_o[:, cs] = gb
            dbb_o[:, cs] = d
            mb_o[:, cs] = mn
            vb_o[:, cs] = vn
        rider = jnp.concatenate([blocks[0][ADA_RIDER_ROW:ADA_RIDER_ROW + 1, :],
                                 blocks[1][ADA_RIDER_ROW:ADA_RIDER_ROW + 1, 0:rest]], axis=1)
        d, mn, vn = _adamw(gw_ref[...], rider, gm_ref[...], gv_ref[...])
        gg_o[...] = rider
        dgg_o[...] = d
        mg_o[...] = mn
        vg_o[...] = vn
        for dist in (1, 2, 3):
            chip_copy(dist).wait()
        gl = g42_ref[q, c] + pr_ref[q] + rc_ref[0].astype(F32) + rc_ref[1].astype(F32) + rc_ref[2].astype(F32)
        d, mn, vn = _adamw(wl_ref[...], gl, ml_ref[...], vl_ref[...])
        gl_o[...] = gl
        dl_o[...] = d
        ml_o[...] = mn
        vl_o[...] = vn

    ws = jax.ShapeDtypeStruct(w_ada.shape, F32)
    bs = jax.ShapeDtypeStruct(b_blocks.shape, F32)
    gs = jax.ShapeDtypeStruct(g_w.shape, F32)
    ls = jax.ShapeDtypeStruct(w_l.shape, F32)
    return pl.pallas_call(
        body, name="ada_bwd", in_specs=[VMEM] * 15, out_specs=[VMEM] * 16,
        out_shape=[ws, ws, ws, ws, bs, bs, bs, bs, gs, gs, gs, gs, ls, ls, ls, ls],
        scratch_shapes=[pltpu.VMEM((N_DEV, ADA_ROWS, ADA_COLS), F32), pltpu.VMEM((4,) + blk, F32),
                        pltpu.VMEM((4,) + blk, BF16), pltpu.VMEM((3,) + blk, BF16),
                        pltpu.SemaphoreType.DMA(()), pltpu.SemaphoreType.DMA(()),
                        pltpu.SemaphoreType.DMA((3,)), pltpu.SemaphoreType.DMA((3,))] + list(g_scr),
        compiler_params=_params(0),
    )(dmod_blk, c_all, w_ada, m_w, v_w, b_blocks, m_b, v_b, g_w, g_m, g_v, g42, w_l, m_l, v_l)


def _ssm_param_fn(lr, li, ldt, br, bi):
    dt = jnp.exp(ldt)
    mag = jnp.exp(lr * dt)
    ang = li * dt
    lbr = mag * jnp.cos(ang)
    lbi = mag * jnp.sin(ang)
    nr = lbr - 1.0
    den = lr * lr + li * li
    cr = (nr * lr + lbi * li) / den
    ci = (lbi * lr - nr * li) / den
    return lbr, lbi, cr * br - ci * bi, cr * bi + ci * br


def _ssm_prep(lr, li, ldt, br, bi):
    def body(lr_ref, li_ref, ldt_ref, br_ref, bi_ref, lbr_o, lbi_o, bbr_o, bbi_o):
        lbr, lbi, bbr, bbi = _ssm_param_fn(lr_ref[...], li_ref[...], ldt_ref[...], br_ref[...], bi_ref[...])
        lbr_o[...] = lbr
        lbi_o[...] = lbi
        bbr_o[...] = bbr
        bbi_o[...] = bbi

    row = jax.ShapeDtypeStruct(lr.shape, F32)
    mat = jax.ShapeDtypeStruct(br.shape, F32)
    return pl.pallas_call(
        body, name="ssm_prep", in_specs=[VMEM] * 5, out_specs=[VMEM] * 4,
        out_shape=[row, row, mat, mat], compiler_params=_params(0),
    )(lr, li, ldt, br, bi)


def _ssm_param_bwd(lr, li, ldt, br, bi, dlam8, dbbr, dbbi, dd8):
    nv = dlam8.shape[1]

    def body(lr_ref, li_ref, ldt_ref, br_ref, bi_ref, dl_ref, dbr_ref, dbi_ref, dd_ref,
             glr_o, gli_o, gldt_o, gbr_o, gbi_o, gd_o):
        halves_r, halves_i, halves_d = [], [], []
        for e in range(2):
            ar = dl_ref[0, e:e + 1, :]
            ai = dl_ref[1, e:e + 1, :]
            ad = dd_ref[e:e + 1, :]
            for b in range(1, nv // 2):
                ar = ar + dl_ref[0, 2 * b + e:2 * b + e + 1, :]
                ai = ai + dl_ref[1, 2 * b + e:2 * b + e + 1, :]
                ad = ad + dd_ref[2 * b + e:2 * b + e + 1, :]
            halves_r.append(ar)
            halves_i.append(ai)
            halves_d.append(ad)
        dlbr = jnp.concatenate(halves_r, axis=1)
        dlbi = jnp.concatenate(halves_i, axis=1)
        gd_o[...] = jnp.concatenate(halves_d, axis=1)
        _, vjp = jax.vjp(_ssm_param_fn, lr_ref[...], li_ref[...], ldt_ref[...], br_ref[...], bi_ref[...])
        glr, gli, gldt, gbr, gbi = vjp((dlbr, dlbi, dbr_ref[...], dbi_ref[...]))
        glr_o[...] = glr
        gli_o[...] = gli
        gldt_o[...] = gldt
        gbr_o[...] = gbr
        gbi_o[...] = gbi

    row = jax.ShapeDtypeStruct(lr.shape, F32)
    mat = jax.ShapeDtypeStruct(br.shape, F32)
    return pl.pallas_call(
        body, name="ssm_param_bwd", in_specs=[VMEM] * 9, out_specs=[VMEM] * 6,
        out_shape=[row, row, row, mat, mat, jax.ShapeDtypeStruct((1, SSM_W), F32)],
        compiler_params=_params(0),
    )(lr, li, ldt, br, bi, dlam8, dbbr, dbbi, dd8)


def _blockdiag(m):
    _, g, a, b = m.shape
    eye = jnp.eye(g, dtype=m.dtype)
    return jnp.einsum("egab,gk->egakb", m, eye).reshape(2, g * a, g * b)


def _blockdiag_take(t, a, b):
    return jnp.einsum("gagb->gab", t.reshape(GRP // 2, a, GRP // 2, b))


def _mixer_in_fwd(x, sh1, sc1, g_mix, w_in_b, comm=None):
    bsz, seq, _ = x.shape
    tt = min(seq, TT_MIX)

    def body(x_ref, sh_ref, sc_ref, g_ref, w_ref, u_ref, p_ref):
        xhat, _ = _rms(x_ref[0])
        h = xhat * g_ref[...] * (1.0 + sc_ref[0]) + sh_ref[0]
        z = _dot(h.astype(BF16), w_ref[...])
        u_ref[...] = z[:, :SSM_W].astype(BF16)
        p_ref[0] = z[:, SSM_W:]

    row = pl.BlockSpec((1, 1, D), lambda b, t: (b, 0, 0))
    return _fused_call(
        body, name="mixer_in_fwd", grid=(bsz, seq // tt),
        in_specs=[pl.BlockSpec((1, tt, D), lambda b, t: (b, t, 0)), row, row,
                  pl.BlockSpec((1, D), lambda b, t: (0, 0)), VMEM],
        out_specs=[pl.BlockSpec((tt, SSM_W), lambda b, t: (t, b)),
                   pl.BlockSpec((1, tt, POOL_W), lambda b, t: (b, t, 0))],
        out_shape=[jax.ShapeDtypeStruct((seq, bsz * SSM_W), BF16), jax.ShapeDtypeStruct((bsz, seq, POOL_W), F32)],
        scratch_shapes=[], args=(x, sh1, sc1, g_mix, w_in_b), comm=_schedule(comm, bsz * (seq // tt)))


def _ssm_project_in(ub, par0, bb_ref, s_re, s_im, row0, tlen, nv):
    for part, sref in ((0, s_re), (1, s_im)):
        for k in range(HALF_ST // 512):
            c0 = part * HALF_ST + k * 512
            a0 = _dot(ub, bb_ref[:, c0:c0 + 512])
            a1 = _dot(ub, bb_ref[:, 2 * HALF_ST + c0:2 * HALF_ST + c0 + 512])
            sref[pl.ds(row0, tlen), :, k * 512:(k + 1) * 512] = jnp.where(par0, a0, a1).reshape(tlen, nv, 512)


def _ssm_fwd(u2r, bb, cc, lam8, d8, tlen, comm=None):
    nv = lam8.shape[1]
    rows = nv * tlen
    n_chunks = u2r.shape[0] // rows

    def body(u_ref, bb_ref, cc_ref, lam_ref, d_ref, y_ref, xc_ref, xre_ref, xim_ref, s_re, s_im, st):
        @pl.when(pl.program_id(0) == 0)
        def _():
            st[...] = jnp.zeros_like(st)

        xc_ref[0] = st[...]
        ub = u_ref[...]
        u = ub.astype(F32)
        par0 = (lax.broadcasted_iota(jnp.int32, (rows, 1), 0) % 2) == 0
        _ssm_project_in(ub, par0, bb_ref, s_re, s_im, 0, tlen, nv)
        for hb in range(HALF_ST // 512):
            ls = slice(hb * 512, (hb + 1) * 512)
            lr = lam_ref[0, :, ls]
            li = lam_ref[1, :, ls]

            def step(t, carry, ls=ls, lr=lr, li=li):
                xr, xi = carry
                nr = lr * xr - li * xi + s_re[t, :, ls]
                ni = lr * xi + li * xr + s_im[t, :, ls]
                s_re[t, :, ls] = nr
                s_im[t, :, ls] = ni
                return nr, ni

            xr, xi = lax.fori_loop(0, tlen, step, (st[0, :, ls], st[1, :, ls]), unroll=8)
            st[0, :, ls] = xr
            st[1, :, ls] = xi
        xre = s_re[...].reshape(rows, HALF_ST).astype(BF16)
        xim = s_im[...].reshape(rows, HALF_ST).astype(BF16)
        xre_ref[...] = xre
        xim_ref[...] = xim
        y2 = _dot(xre, cc_ref[0:HALF_ST, :]) + _dot(xim, cc_ref[HALF_ST:, :])
        y = jnp.where(par0, y2[:, :HALF_CH], y2[:, HALF_CH:])
        skip = (u.reshape(tlen, nv, HALF_CH) * d_ref[...][None]).reshape(rows, HALF_CH)
        y_ref[...] = (y + skip).astype(BF16)

    st_blk = pl.BlockSpec((rows, HALF_ST), lambda c: (c, 0))
    st_shape = jax.ShapeDtypeStruct((u2r.shape[0], HALF_ST), BF16)
    return _fused_call(
        body, name="ssm_fwd", grid=(n_chunks,),
        in_specs=[pl.BlockSpec((rows, HALF_CH), lambda c: (c, 0)), VMEM, VMEM, VMEM, VMEM],
        out_specs=[pl.BlockSpec((rows, HALF_CH), lambda c: (c, 0)),
                   pl.BlockSpec((1, 2, nv, HALF_ST), lambda c: (c, 0, 0, 0)), st_blk, st_blk],
        out_shape=[jax.ShapeDtypeStruct(u2r.shape, BF16), jax.ShapeDtypeStruct((n_chunks, 2, nv, HALF_ST), F32),
                   st_shape, st_shape],
        scratch_shapes=[pltpu.VMEM((tlen, nv, HALF_ST), F32), pltpu.VMEM((tlen, nv, HALF_ST), F32),
                        pltpu.VMEM((2, nv, HALF_ST), F32)],
        args=(u2r, bb, cc, lam8, d8), comm=_schedule(comm, n_chunks))


def _pool_forward(ext, pv, pos, wp_ref, bp_ref):
    cur = ext
    zs, zls = [], []
    for gi, w in enumerate(POOL_WINDOWS):
        cur = cur + pltpu.roll(cur, w // 2, 0)
        sw = cur[POOL_HALO:, 0:128]
        z = sw / jnp.minimum(pos, float(w)) - pv[:, gi * 128:(gi + 1) * 128]
        zs.append(z)
        zls.append(_dot(z.astype(BF16), wp_ref[gi]) + bp_ref[:, gi * 128:(gi + 1) * 128])
        if gi + 1 < len(POOL_WINDOWS):
            cur = cur[:, 128:]
    return zs, zls


def _mixer_out_fwd(y2, p, x, gt1, w_glu_b, b_glu, w_pool_b, b_pool, pscale, w_out_b):
    bsz, seq, _ = x.shape
    tt = min(seq, TT_MIX)

    def body(y_ref, p_ref, x_ref, gt_ref, wg_ref, bg_ref, wp_ref, bp_ref, ps_ref, wo_ref, x1_ref, mix_ref, mxd_ref,
             ext):
        ti = pl.program_id(1)

        @pl.when(ti == 0)
        def _():
            ext[0:POOL_HALO, :] = jnp.zeros((POOL_HALO, POOL_W), F32)

        pv = p_ref[0]
        ext[POOL_HALO:, :] = pv
        pos = (ti * tt + lax.broadcasted_iota(jnp.int32, (tt, 1), 0) + 1).astype(F32)
        _, zls = _pool_forward(ext[...], pv, pos, wp_ref, bp_ref)
        ext[0:POOL_HALO, :] = pv[tt - POOL_HALO:, :]
        a = _gelu(y_ref[...].astype(F32))
        gl = _dot(a.astype(BF16), wg_ref[...]) + bg_ref[...]
        y_ssm = gl[:, :SSM_W] * jax.nn.sigmoid(gl[:, SSM_W:])
        y_pool = [zl * ps_ref[:, gi * 128:(gi + 1) * 128] for gi, zl in enumerate(zls)]
        mixcat = jnp.concatenate([y_ssm] + y_pool, axis=1).astype(BF16)
        mix_ref[0] = mixcat
        mixed = _dot(mixcat, wo_ref[...])
        mxd_ref[0] = mixed.astype(BF16)
        x1_ref[0] = x_ref[0] + gt_ref[0] * mixed

    xt = pl.BlockSpec((1, tt, D), lambda b, t: (b, t, 0))
    return pl.pallas_call(
        body, name="mixer_out_fwd", grid=(bsz, seq // tt),
        in_specs=[pl.BlockSpec((tt, SSM_W), lambda b, t: (t, b)),
                  pl.BlockSpec((1, tt, POOL_W), lambda b, t: (b, t, 0)), xt,
                  pl.BlockSpec((1, 1, D), lambda b, t: (b, 0, 0)), VMEM, VMEM, VMEM, VMEM, VMEM, VMEM],
        out_specs=[xt, xt, xt],
        out_shape=[jax.ShapeDtypeStruct(x.shape, F32), jax.ShapeDtypeStruct(x.shape, BF16),
                   jax.ShapeDtypeStruct(x.shape, BF16)],
        scratch_shapes=[pltpu.VMEM((POOL_HALO + tt, POOL_W), F32)],
        compiler_params=_params(2),
    )(y2, p, x, gt1, w_glu_b, b_glu, w_pool_b, b_pool, pscale, w_out_b)


def _conv_gate(g, ge, wc, bc):
    g1 = pltpu.roll(ge, 1, 0)[CONV_HALO:]
    g2 = pltpu.roll(ge, 2, 0)[CONV_HALO:]
    return wc[2:3] * g + wc[1:2] * g1 + wc[0:1] * g2 + bc, g1, g2


def _ffn_fwd(x1, tgt, sh2, sc2, gt2, g_ffn, w_up_b, w_conv, b_conv, w_down_b, g_fin):
    bsz, seq, _ = x1.shape
    tt = min(seq, TT_FFN)
    n_t = seq // tt
    n_ck = DFF // FF_CH

    def body(x1_ref, tg_ref, sh_ref, sc_ref, gt_ref, gf_ref, wu_ref, wc_ref, bc_ref, wd_ref, gfin_ref,
             h2_ref, v_ref, g_ref, gc_ref, act_ref, ddn_ref, dx2_ref, loss_ref, dgfin_ref, dgt_ref, gext, lacc):
        b = pl.program_id(0)
        ti = pl.program_id(1)

        @pl.when((b == 0) & (ti == 0))
        def _():
            lacc[...] = jnp.zeros_like(lacc)
            dgfin_ref[...] = jnp.zeros_like(dgfin_ref)

        @pl.when(ti == 0)
        def _():
            dgt_ref[...] = jnp.zeros_like(dgt_ref)
            gext[:, 0:CONV_HALO, :] = jnp.zeros((n_ck, CONV_HALO, FF_CH), F32)

        x1v = x1_ref[0]
        xhat, _ = _rms(x1v)
        h2b = (xhat * gf_ref[...] * (1.0 + sc_ref[0]) + sh_ref[0]).astype(BF16)
        h2_ref[0] = h2b
        dn = jnp.zeros((tt, D), F32)
        for ck in range(n_ck):
            c0 = ck * FF_CH
            v = _dot_nt(h2b, wu_ref[c0:c0 + FF_CH, :])
            g = _dot_nt(h2b, wu_ref[DFF + c0:DFF + c0 + FF_CH, :])
            v_ref[0, :, c0:c0 + FF_CH] = v.astype(BF16)
            g_ref[0, :, c0:c0 + FF_CH] = g.astype(BF16)
            gext[ck, CONV_HALO:, :] = g
            gc, _, _ = _conv_gate(g, gext[ck], wc_ref[:, c0:c0 + FF_CH], bc_ref[:, c0:c0 + FF_CH])
            gext[ck, 0:CONV_HALO, :] = g[tt - CONV_HALO:, :]
            gc_ref[0, :, c0:c0 + FF_CH] = gc.astype(BF16)
            actb = (gc * jax.nn.sigmoid(gc) * v).astype(BF16)
            act_ref[0, :, c0:c0 + FF_CH] = actb
            dn = dn + _dot(actb, wd_ref[c0:c0 + FF_CH, :])
        gt = gt_ref[0]
        xh3, r3 = _rms(x1v + gt * dn)
        gfin = gfin_ref[...]
        diff = xh3 * gfin - tg_ref[0]
        lacc[...] += _colsum(diff * diff)
        dy = diff * (1.0 / D)
        dgfin_ref[...] += _colsum(dy * xh3)
        dx2 = _rms_bwd(dy * gfin, xh3, r3)
        dx2_ref[0] = dx2
        dgt_ref[0] += _colsum(dx2 * dn)
        ddn_ref[0] = (gt * dx2).astype(BF16)

        @pl.when((b == bsz - 1) & (ti == n_t - 1))
        def _():
            loss_ref[...] = jnp.full(loss_ref.shape, 0.5 / D * jnp.sum(lacc[...]), F32)

    xt = pl.BlockSpec((1, tt, D), lambda b, t: (b, t, 0))
    ft = pl.BlockSpec((1, tt, DFF), lambda b, t: (b, t, 0))
    row = pl.BlockSpec((1, 1, D), lambda b, t: (b, 0, 0))
    vec = pl.BlockSpec((1, D), lambda b, t: (0, 0))
    ff = jax.ShapeDtypeStruct((bsz, seq, DFF), BF16)
    xs = jax.ShapeDtypeStruct((bsz, seq, D), BF16)
    return pl.pallas_call(
        body, name="ffn_fwd", grid=(bsz, n_t),
        in_specs=[xt, xt, row, row, row, vec, VMEM, VMEM, VMEM, VMEM, vec],
        out_specs=[xt, ft, ft, ft, ft, xt, xt, pl.BlockSpec((1, 128), lambda b, t: (0, 0)), vec, row],
        out_shape=[xs, ff, ff, ff, ff, xs, jax.ShapeDtypeStruct((bsz, seq, D), F32),
                   jax.ShapeDtypeStruct((1, 128), F32), jax.ShapeDtypeStruct((1, D), F32),
                   jax.ShapeDtypeStruct((bsz, 1, D), F32)],
        scratch_shapes=[pltpu.VMEM((n_ck, CONV_HALO + tt, FF_CH), F32), pltpu.VMEM((1, D), F32)],
        compiler_params=_params(2),
    )(x1, tgt, sh2, sc2, gt2, g_ffn, w_up_b, w_conv, b_conv, w_down_b, g_fin)


def _ffn_bwd(ddn, gq, gcq, vq, x1, dx2, sh2, sc2, g_ffn, w_conv, w_down_b, w_up_b):
    bsz, seq, _ = x1.shape
    tt = min(seq, TT_FFN)
    n_t = seq // tt
    n_ck = DFF // FF_CH
    ext_rows = tt + CONV_HALO

    def body(ddn_ref, g_ref, gc_ref, v_ref, x1_ref, dx2_ref, sh_ref, sc_ref, gf_ref, wc_ref, wd_ref,
             wu_ref, dup_ref, dx1_ref, dsh_ref, dsc_ref, dgf_ref, dwc_ref, dbc_ref, dext):
        b = pl.program_id(0)
        i = pl.program_id(1)

        @pl.when((b == 0) & (i == 0))
        def _():
            dgf_ref[...] = jnp.zeros_like(dgf_ref)
            dwc_ref[...] = jnp.zeros_like(dwc_ref)
            dbc_ref[...] = jnp.zeros_like(dbc_ref)

        @pl.when(i == 0)
        def _():
            dsh_ref[...] = jnp.zeros_like(dsh_ref)
            dsc_ref[...] = jnp.zeros_like(dsc_ref)
            dext[:, tt:, :] = jnp.zeros((n_ck, CONV_HALO, FF_CH), F32)

        ddnv = ddn_ref[0]
        dh2 = jnp.zeros((tt, D), F32)
        for ck in range(n_ck):
            c0 = ck * FF_CH
            dact = _dot_nt(ddnv, wd_ref[c0:c0 + FF_CH, :])
            g = g_ref[0, :, c0:c0 + FF_CH].astype(F32)
            gc = gc_ref[0, :, c0:c0 + FF_CH].astype(F32)
            v = v_ref[0, :, c0:c0 + FF_CH].astype(F32)
            wc = wc_ref[:, c0:c0 + FF_CH]
            sg = jax.nn.sigmoid(gc)
            silu = gc * sg
            dv = dact * silu
            dgc = dact * v * (sg + silu * (1.0 - sg))
            dext[ck, 0:tt, :] = dgc
            de = dext[ck]
            d1 = pltpu.roll(de, ext_rows - 1, 0)[0:tt]
            d2 = pltpu.roll(de, ext_rows - 2, 0)[0:tt]
            dext[ck, tt:, :] = dgc[0:CONV_HALO, :]
            dbc_ref[:, c0:c0 + FF_CH] += _colsum(dgc)
            dwc_ref[0:1, c0:c0 + FF_CH] += _colsum(d2 * g)
            dwc_ref[1:2, c0:c0 + FF_CH] += _colsum(d1 * g)
            dwc_ref[2:3, c0:c0 + FF_CH] += _colsum(dgc * g)
            dg = wc[2:3] * dgc + wc[1:2] * d1 + wc[0:1] * d2
            dvb = dv.astype(BF16)
            dgb = dg.astype(BF16)
            dup_ref[0, :, c0:c0 + FF_CH] = dvb
            dup_ref[0, :, DFF + c0:DFF + c0 + FF_CH] = dgb
            dh2 = dh2 + _dot(dvb, wu_ref[c0:c0 + FF_CH, :]) + _dot(dgb, wu_ref[DFF + c0:DFF + c0 + FF_CH, :])
        xhat, rstd = _rms(x1_ref[0])
        gf = gf_ref[...]
        dsh_ref[0] += _colsum(dh2)
        dsc_ref[0] += _colsum(dh2 * xhat * gf)
        t = dh2 * (1.0 + sc_ref[0])
        dgf_ref[...] += _colsum(t * xhat)
        dx1_ref[0] = dx2_ref[0] + _rms_bwd(t * gf, xhat, rstd)

    def rev(b, t):
        return (b, n_t - 1 - t, 0)

    xt = pl.BlockSpec((1, tt, D), rev)
    ft = pl.BlockSpec((1, tt, DFF), rev)
    row = pl.BlockSpec((1, 1, D), lambda b, t: (b, 0, 0))
    vec = pl.BlockSpec((1, D), lambda b, t: (0, 0))
    rows = jax.ShapeDtypeStruct((bsz, 1, D), F32)
    return pl.pallas_call(
        body, name="ffn_bwd", grid=(bsz, n_t),
        in_specs=[xt, ft, ft, ft, xt, xt, row, row, vec, VMEM, VMEM, VMEM],
        out_specs=[pl.BlockSpec((1, tt, 2 * DFF), rev), xt, row, row, vec,
                   pl.BlockSpec((3, DFF), lambda b, t: (0, 0)), pl.BlockSpec((1, DFF), lambda b, t: (0, 0))],
        out_shape=[jax.ShapeDtypeStruct((bsz, seq, 2 * DFF), BF16), jax.ShapeDtypeStruct((bsz, seq, D), F32),
                   rows, rows, jax.ShapeDtypeStruct((1, D), F32), jax.ShapeDtypeStruct((3, DFF), F32),
                   jax.ShapeDtypeStruct((1, DFF), F32)],
        scratch_shapes=[pltpu.VMEM((n_ck, ext_rows, FF_CH), F32)],
        compiler_params=_params(2),
    )(ddn, gq, gcq, vq, x1, dx2, sh2, sc2, g_ffn, w_conv, w_down_b, w_up_b)


def _wgrad(a, b, bk1, bk2, name):
    n, k1 = a.shape
    _, k2 = b.shape
    tt = min(n, TT_WGRAD)

    def body(a_ref, b_ref, o_ref):
        @pl.when(pl.program_id(2) == 0)
        def _():
            o_ref[...] = jnp.zeros_like(o_ref)

        o_ref[...] += _dot_tn(a_ref[...], b_ref[...])

    return pl.pallas_call(
        body, name=name, grid=(k1 // bk1, k2 // bk2, n // tt),
        in_specs=[pl.BlockSpec((tt, bk1), lambda h, j, i: (i, h)), pl.BlockSpec((tt, bk2), lambda h, j, i: (i, j))],
        out_specs=pl.BlockSpec((bk1, bk2), lambda h, j, i: (h, j)),
        out_shape=jax.ShapeDtypeStruct((k1, k2), F32), compiler_params=_params(3),
    )(a, b)


def _mixer_out_bwd(dx1, mixcat, mixed, y2, p, gt1, w_glu_b, b_glu, w_pool_b, b_pool, pscale, w_out_b, comm=None):
    bsz, seq, _ = dx1.shape
    tt = min(seq, TT_MIX)
    n_t = seq // tt
    ext_rows = tt + POOL_HALO

    def body(dx1_ref, mc_ref, mxd_ref, y_ref, p_ref, ph_ref, gt_ref, wg_ref, bg_ref, wp_ref, bp_ref, ps_ref, wo_ref,
             dy_ref, dp_ref, dwo_ref, dwg_ref, dbg_ref, dwp_ref, dbp_ref, dps_ref, dgt_ref, ext, qext):
        b = pl.program_id(0)
        i = pl.program_id(1)
        tile = n_t - 1 - i

        @pl.when((b == 0) & (i == 0))
        def _():
            for r in (dwo_ref, dwg_ref, dbg_ref, dwp_ref, dbp_ref, dps_ref):
                r[...] = jnp.zeros_like(r)

        @pl.when(i == 0)
        def _():
            dgt_ref[...] = jnp.zeros_like(dgt_ref)
            qext[tt:, :] = jnp.zeros((POOL_HALO, POOL_W), F32)

        dx1v = dx1_ref[0]
        mc = mc_ref[0]
        dgt_ref[0] += _colsum(dx1v * mxd_ref[0].astype(F32))
        dmixed = (gt_ref[0] * dx1v).astype(BF16)
        dwo_ref[...] += _dot_tn(mc, dmixed)
        dmc = _dot_nt(dmixed, wo_ref[...])
        pv = p_ref[0]
        ext[0:POOL_HALO, :] = ph_ref[0] * (tile > 0).astype(F32)
        ext[POOL_HALO:, :] = pv
        pos = (tile * tt + lax.broadcasted_iota(jnp.int32, (tt, 1), 0) + 1).astype(F32)
        zs, zls = _pool_forward(ext[...], pv, pos, wp_ref, bp_ref)
        dzs = []
        for gi, w in enumerate(POOL_WINDOWS):
            cs = slice(gi * 128, (gi + 1) * 128)
            dyp = dmc[:, SSM_W + gi * 128:SSM_W + (gi + 1) * 128]
            dps_ref[:, cs] += _colsum(dyp * zls[gi])
            dzl = dyp * ps_ref[:, cs]
            dbp_ref[:, cs] += _colsum(dzl)
            dzlb = dzl.astype(BF16)
            dwp_ref[gi] += _dot_tn(zs[gi].astype(BF16), dzlb)
            dz = _dot_nt(dzlb, wp_ref[gi])
            dzs.append(dz)
            qext[0:tt, cs] = dz / jnp.minimum(pos, float(w))
        cur = qext[...]
        dps = []
        for gi, w in enumerate(POOL_WINDOWS):
            cur = cur + pltpu.roll(cur, ext_rows - w // 2, 0)
            dps.append(cur[0:tt, 0:128] - dzs[gi])
            if gi + 1 < len(POOL_WINDOWS):
                cur = cur[:, 128:]
        qhead = qext[0:POOL_HALO, :]
        qext[tt:, :] = qhead
        dp_ref[0] = jnp.concatenate(dps, axis=1)
        yv = y_ref[...].astype(F32)
        ab = _gelu(yv).astype(BF16)
        gl = _dot(ab, wg_ref[...]) + bg_ref[...]
        val = gl[:, :SSM_W]
        sg = jax.nn.sigmoid(gl[:, SSM_W:])
        dys = dmc[:, :SSM_W]
        dgl = jnp.concatenate([dys * sg, dys * val * sg * (1.0 - sg)], axis=1)
        dbg_ref[...] += _colsum(dgl)
        dglb = dgl.astype(BF16)
        dwg_ref[...] += _dot_tn(ab, dglb)
        dy_ref[...] = (_dot_nt(dglb, wg_ref[...]) * _gelu_grad(yv)).astype(BF16)

    def rev(b, t):
        return (b, n_t - 1 - t, 0)

    def halo(b, t):
        return (b, jnp.maximum((n_t - 1 - t) * (tt // POOL_HALO) - 1, 0), 0)

    xt = pl.BlockSpec((1, tt, D), rev)
    pt = pl.BlockSpec((1, tt, POOL_W), rev)
    yt = pl.BlockSpec((tt, SSM_W), lambda b, t: (n_t - 1 - t, b))

    def whole(shape):
        return pl.BlockSpec(shape, lambda b, t: (0,) * len(shape))

    return _fused_call(
        body, name="mixer_out_bwd", grid=(bsz, n_t),
        in_specs=[xt, xt, xt, yt, pt, pl.BlockSpec((1, POOL_HALO, POOL_W), halo),
                  pl.BlockSpec((1, 1, D), lambda b, t: (b, 0, 0)), VMEM, VMEM, VMEM, VMEM, VMEM, VMEM],
        out_specs=[yt, pt, whole((D, D)), whole((SSM_W, 2 * SSM_W)), whole((1, 2 * SSM_W)),
                   whole((4, 128, 128)), whole((1, POOL_W)), whole((1, POOL_W)),
                   pl.BlockSpec((1, 1, D), lambda b, t: (b, 0, 0))],
        out_shape=[jax.ShapeDtypeStruct(y2.shape, BF16), jax.ShapeDtypeStruct(p.shape, F32),
                   jax.ShapeDtypeStruct((D, D), F32), jax.ShapeDtypeStruct((SSM_W, 2 * SSM_W), F32),
                   jax.ShapeDtypeStruct((1, 2 * SSM_W), F32), jax.ShapeDtypeStruct((4, 128, 128), F32),
                   jax.ShapeDtypeStruct((1, POOL_W), F32), jax.ShapeDtypeStruct((1, POOL_W), F32),
                   jax.ShapeDtypeStruct((bsz, 1, D), F32)],
        scratch_shapes=[pltpu.VMEM((POOL_HALO + tt, POOL_W), F32), pltpu.VMEM((ext_rows, POOL_W), F32)],
        args=(dx1, mixcat, mixed, y2, p, p, gt1, w_glu_b, b_glu, w_pool_b, b_pool, pscale, w_out_b),
        comm=_schedule(comm, bsz * n_t))


def _ssm_bwd(dy2r, u2r, xc, xs_re, xs_im, bb, cc, lam8, d8, tlen, comm=None):
    nv = lam8.shape[1]
    rows = nv * tlen
    n_chunks = u2r.shape[0] // rows

    def body(dy_ref, u_ref, xc_ref, xre_ref, xim_ref, bb_ref, cc_ref, lam_ref, d_ref,
             du_ref, dcc_ref, dbb_ref, dlam_ref, dd_ref, s_re, s_im, g_re, g_im, gst):
        i = pl.program_id(0)

        @pl.when(i == 0)
        def _():
            for r in (gst, dcc_ref, dbb_ref, dlam_ref, dd_ref):
                r[...] = jnp.zeros_like(r)

        u = u_ref[...].astype(F32)
        dy = dy_ref[...].astype(F32)
        par0 = (lax.broadcasted_iota(jnp.int32, (rows, 1), 0) % 2) == 0
        xre = xre_ref[...]
        xim = xim_ref[...]
        s_re[0] = xc_ref[0, 0]
        s_im[0] = xc_ref[0, 1]
        s_re[pl.ds(1, tlen)] = xre.astype(F32).reshape(tlen, nv, HALF_ST)
        s_im[pl.ds(1, tlen)] = xim.astype(F32).reshape(tlen, nv, HALF_ST)
        zero = jnp.zeros_like(dy)
        dy2 = jnp.concatenate([jnp.where(par0, dy, zero), jnp.where(par0, zero, dy)], axis=1).astype(BF16)
        u2 = jnp.concatenate([jnp.where(par0, u, zero), jnp.where(par0, zero, u)], axis=1).astype(BF16)
        dcc_ref[0:HALF_ST, :] += _dot_tn(xre, dy2)
        dcc_ref[HALF_ST:, :] += _dot_tn(xim, dy2)
        for part, gref in ((0, g_re), (1, g_im)):
            for k in range(HALF_ST // 512):
                r0 = part * HALF_ST + k * 512
                gref[:, :, k * 512:(k + 1) * 512] = _dot_nt(dy2, cc_ref[r0:r0 + 512, :]).reshape(tlen, nv, 512)
        for hb in range(HALF_ST // 512):
            ls = slice(hb * 512, (hb + 1) * 512)
            lr = lam_ref[0, :, ls]
            li = lam_ref[1, :, ls]

            def bstep(k, carry, ls=ls, lr=lr, li=li):
                t = tlen - 1 - k
                gr, gi, ar, ai = carry
                ngr = g_re[t, :, ls] + lr * gr + li * gi
                ngi = g_im[t, :, ls] + lr * gi - li * gr
                g_re[t, :, ls] = ngr
                g_im[t, :, ls] = ngi
                xpr = s_re[t, :, ls]
                xpi = s_im[t, :, ls]
                return ngr, ngi, ar + ngr * xpr + ngi * xpi, ai + ngi * xpr - ngr * xpi

            init = (gst[0, :, ls], gst[1, :, ls], dlam_ref[0, :, ls], dlam_ref[1, :, ls])
            gr, gi, ar, ai = lax.fori_loop(0, tlen, bstep, init, unroll=4)
            gst[0, :, ls] = gr
            gst[1, :, ls] = gi
            dlam_ref[0, :, ls] = ar
            dlam_ref[1, :, ls] = ai
        gre = g_re[...].reshape(rows, HALF_ST).astype(BF16)
        gim = g_im[...].reshape(rows, HALF_ST).astype(BF16)
        du0 = _dot_nt(gre, bb_ref[:, 0:HALF_ST]) + _dot_nt(gim, bb_ref[:, HALF_ST:2 * HALF_ST])
        du1 = _dot_nt(gre, bb_ref[:, 2 * HALF_ST:3 * HALF_ST]) + _dot_nt(gim, bb_ref[:, 3 * HALF_ST:])
        skip = (dy.reshape(tlen, nv, HALF_CH) * d_ref[...][None]).reshape(rows, HALF_CH)
        du_ref[...] = (jnp.where(par0, du0, du1) + skip).astype(BF16)
        dbb_ref[:, 0:HALF_ST] += _dot_tn(u2, gre)
        dbb_ref[:, HALF_ST:] += _dot_tn(u2, gim)
        dd_ref[...] += jnp.sum((dy * u).reshape(tlen, nv, HALF_CH), axis=0)

        @pl.when(i == n_chunks - 1)
        def _():
            dcc_ref[HALF_ST:, :] = -dcc_ref[HALF_ST:, :]

    def rev(c):
        return (n_chunks - 1 - c, 0)

    def whole(shape):
        return pl.BlockSpec(shape, lambda c: (0,) * len(shape))

    blk = pl.BlockSpec((rows, HALF_CH), rev)
    st_blk = pl.BlockSpec((rows, HALF_ST), rev)
    return _fused_call(
        body, name="ssm_bwd", grid=(n_chunks,),
        in_specs=[blk, blk, pl.BlockSpec((1, 2, nv, HALF_ST), lambda c: (n_chunks - 1 - c, 0, 0, 0)),
                  st_blk, st_blk, VMEM, VMEM, VMEM, VMEM],
        out_specs=[blk, whole((2 * HALF_ST, SSM_W)), whole((SSM_W, 2 * HALF_ST)), whole((2, nv, HALF_ST)),
                   whole((nv, HALF_CH))],
        out_shape=[jax.ShapeDtypeStruct(u2r.shape, BF16), jax.ShapeDtypeStruct((2 * HALF_ST, SSM_W), F32),
                   jax.ShapeDtypeStruct((SSM_W, 2 * HALF_ST), F32), jax.ShapeDtypeStruct((2, nv, HALF_ST), F32),
                   jax.ShapeDtypeStruct((nv, HALF_CH), F32)],
        scratch_shapes=[pltpu.VMEM((tlen + 1, nv, HALF_ST), F32), pltpu.VMEM((tlen + 1, nv, HALF_ST), F32),
                        pltpu.VMEM((tlen, nv, HALF_ST), F32), pltpu.VMEM((tlen, nv, HALF_ST), F32),
                        pltpu.VMEM((2, nv, HALF_ST), F32)],
        args=(dy2r, u2r, xc, xs_re, xs_im, bb, cc, lam8, d8), comm=_schedule(comm, n_chunks))


def _mixer_in_bwd(du2, dp, x, dx1, sh1, sc1, g_mix, w_in_b, comm=None):
    bsz, seq, _ = x.shape
    tt = min(seq, TT_MIX)

    def body(du_ref, dp_ref, x_ref, dx1_ref, sh_ref, sc_ref, g_ref, w_ref,
             dx_ref, dw_ref, dsh_ref, dsc_ref, dg_ref):
        b = pl.program_id(0)
        ti = pl.program_id(1)

        @pl.when((b == 0) & (ti == 0))
        def _():
            dw_ref[...] = jnp.zeros_like(dw_ref)
            dg_ref[...] = jnp.zeros_like(dg_ref)

        @pl.when(ti == 0)
        def _():
            dsh_ref[...] = jnp.zeros_like(dsh_ref)
            dsc_ref[...] = jnp.zeros_like(dsc_ref)

        dz = jnp.concatenate([du_ref[...], dp_ref[0].astype(BF16)], axis=1)
        xhat, rstd = _rms(x_ref[0])
        g = g_ref[...]
        sc = sc_ref[0]
        a = xhat * g
        h = (a * (1.0 + sc) + sh_ref[0]).astype(BF16)
        dw_ref[...] += _dot_tn(h, dz)
        dh = _dot_nt(dz, w_ref[...])
        dsh_ref[0] += _colsum(dh)
        dsc_ref[0] += _colsum(dh * a)
        t = dh * (1.0 + sc)
        dg_ref[...] += _colsum(t * xhat)
        dx_ref[0] = dx1_ref[0] + _rms_bwd(t * g, xhat, rstd)

    xt = pl.BlockSpec((1, tt, D), lambda b, t: (b, t, 0))
    row = pl.BlockSpec((1, 1, D), lambda b, t: (b, 0, 0))
    vec = pl.BlockSpec((1, D), lambda b, t: (0, 0))
    rows = jax.ShapeDtypeStruct((bsz, 1, D), F32)
    return _fused_call(
        body, name="mixer_in_bwd", grid=(bsz, seq // tt),
        in_specs=[pl.BlockSpec((tt, SSM_W), lambda b, t: (t, b)),
                  pl.BlockSpec((1, tt, POOL_W), lambda b, t: (b, t, 0)), xt, xt, row, row, vec, VMEM],
        out_specs=[xt, pl.BlockSpec((D, D), lambda b, t: (0, 0)), row, row, vec],
        out_shape=[jax.ShapeDtypeStruct(x.shape, F32), jax.ShapeDtypeStruct((D, D), F32), rows, rows,
                   jax.ShapeDtypeStruct((1, D), F32)],
        scratch_shapes=[], args=(du2, dp, x, dx1, sh1, sc1, g_mix, w_in_b),
        comm=_schedule(comm, bsz * (seq // tt)))


def kernel(x, c, w_ada, b_ada, g_norm_mix, w_in, ssm_lam_re, ssm_lam_im, ssm_log_dt, ssm_b_re, ssm_b_im, ssm_c_re, ssm_c_im, ssm_d, w_glu, b_glu, w_pool, b_pool, pool_scale, w_out, g_norm_ffn, w_up, w_conv, b_conv, w_down, g_norm_final, loss_target, m_w_ada, m_b_ada, m_g_norm_mix, m_w_in, m_ssm_lam_re, m_ssm_lam_im, m_ssm_log_dt, m_ssm_b_re, m_ssm_b_im, m_ssm_c_re, m_ssm_c_im, m_ssm_d, m_w_glu, m_b_glu, m_w_pool, m_b_pool, m_pool_scale, m_w_out, m_g_norm_ffn, m_w_up, m_w_conv, m_b_conv, m_w_down, m_g_norm_final, v_w_ada, v_b_ada, v_g_norm_mix, v_w_in, v_ssm_lam_re, v_ssm_lam_im, v_ssm_log_dt, v_ssm_b_re, v_ssm_b_im, v_ssm_c_re, v_ssm_c_im, v_ssm_d, v_w_glu, v_b_glu, v_w_pool, v_b_pool, v_pool_scale, v_w_out, v_g_norm_ffn, v_w_up, v_w_conv, v_b_conv, v_w_down, v_g_norm_final):
    bsz, seq, _ = x.shape
    assert 2 * bsz == 8 and seq % 128 == 0
    px, py, pc = _my_place()
    me = 4 * px + 2 * py + pc
    place = jnp.stack([pc, 2 * px + py]).astype(jnp.int32)
    ncol = ADA_COLS

    cpad = jnp.zeros((16, D), F32).at[0:bsz].set(c).at[8:11, 0:352].set(w_conv[0])
    cg, c_all, mod8, (g_in,) = _ada_fwd(cpad, w_ada[0], b_ada.reshape(N_DEV, 1, ncol), [w_in[0].astype(BF16)])
    w_conv_f = cg[:, 8:11, 0:352].transpose(1, 0, 2).reshape(3, DFF)
    w_in_b = g_in.reshape(D, D)
    sh1, sc1, gt1, sh2, sc2, gt2 = [mod8[0:bsz, k * D:(k + 1) * D].reshape(bsz, 1, D) for k in range(N_MOD)]

    lam_r = ssm_lam_re[0].reshape(1, GRP * NST)
    lam_i = ssm_lam_im[0].reshape(1, GRP * NST)
    ldt = jnp.repeat(ssm_log_dt[0], NST).reshape(1, GRP * NST)
    b_r = ssm_b_re[0].transpose(2, 0, 1).reshape(GCH, GRP * NST)
    b_i = ssm_b_im[0].transpose(2, 0, 1).reshape(GCH, GRP * NST)
    lbr, lbi, bbr, bbi = _ssm_prep(lam_r, lam_i, ldt, b_r, b_i)
    lam8 = jnp.stack([jnp.tile(lbr.reshape(2, HALF_ST), (bsz, 1)), jnp.tile(lbi.reshape(2, HALF_ST), (bsz, 1))])
    bd_r = _blockdiag(bbr.reshape(GCH, 2, GRP // 2, NST).transpose(1, 2, 0, 3))
    bd_i = _blockdiag(bbi.reshape(GCH, 2, GRP // 2, NST).transpose(1, 2, 0, 3))
    bb = jnp.concatenate([bd_r[0], bd_i[0], bd_r[1], bd_i[1]], axis=1).astype(BF16)
    cd_r = _blockdiag(ssm_c_re[0].reshape(2, GRP // 2, GCH, NST).transpose(0, 1, 3, 2))
    cd_i = _blockdiag(ssm_c_im[0].reshape(2, GRP // 2, GCH, NST).transpose(0, 1, 3, 2))
    cc = jnp.concatenate([jnp.concatenate([cd_r[0], cd_r[1]], axis=1),
                          jnp.concatenate([-cd_i[0], -cd_i[1]], axis=1)], axis=0).astype(BF16)
    d8 = jnp.tile(ssm_d[0].reshape(2, HALF_CH), (bsz, 1))

    tlen = min(seq, T_SSM)
    (u2, p), ((g_glu, g_out),) = _mixer_in_fwd(
        x, sh1, sc1, g_norm_mix, w_in_b, comm=[(_gather_plan, [w_glu[0].astype(BF16), w_out[0].astype(BF16)], "small")])
    w_glu_b = g_glu.transpose(1, 0, 2).reshape(SSM_W, 2 * SSM_W)
    w_out_b = g_out.reshape(D, D)
    u2r = u2.reshape(seq * 2 * bsz, HALF_CH)
    (y2r, xc, xs_re, xs_im), ((g_up, g_down),) = _ssm_fwd(
        u2r, bb, cc, lam8, d8, tlen, comm=[(_gather_plan, [w_up[0].T.astype(BF16), w_down[0].astype(BF16)])])
    w_up_b = g_up.reshape(2 * DFF, D)
    w_down_b = g_down.reshape(DFF, D)
    y2 = y2r.reshape(seq, bsz * SSM_W)
    w_pool_b = w_pool[0].astype(BF16)
    bp = b_pool[0].reshape(1, POOL_W)
    x1, mixcat, mixed = _mixer_out_fwd(y2, p, x, gt1, w_glu_b, b_glu, w_pool_b, bp, pool_scale, w_out_b)
    h2, vq, gq, gcq, act, ddn, dx2, loss_l, dg_fin, dgt2 = _ffn_fwd(
        x1, loss_target, sh2, sc2, gt2, g_norm_ffn, w_up_b, w_conv_f, b_conv, w_down_b, g_norm_final.reshape(1, D))

    dup, dx1, dsh2, dsc2, dg_ffn, dw_conv, db_conv = _ffn_bwd(
        ddn, gq, gcq, vq, x1, dx2, sh2, sc2, g_norm_ffn, w_conv_f, w_down_b, w_up_b)
    ntok = bsz * seq
    dw_up_t = _wgrad(dup.reshape(ntok, 2 * DFF), h2.reshape(ntok, D), DFF // 2, D, "wgrad_up")
    dw_down = _wgrad(act.reshape(ntok, DFF), ddn.reshape(ntok, D), DFF, 512, "wgrad_down")
    g42_up = dw_up_t.reshape(4, 2, 704, D)
    g42_down = dw_down.reshape(4, 2, 352, D)
    (dy2, dp, dw_out, dw_glu, db_glu, dw_pool, db_pool, dpscale, dgt1), ((ra_up, ra_down),) = _mixer_out_bwd(
        dx1, mixcat, mixed, y2, p, gt1, w_glu_b, b_glu, w_pool_b, bp, pool_scale, w_out_b,
        comm=[(_pair_plan, [g42_up, g42_down])])
    own_up, s_up = _pair_sum(g42_up, ra_up, place, "pair_sum_up")
    own_down, s_down = _pair_sum(g42_down, ra_down, place, "pair_sum_down")
    g42_glu = dw_glu.reshape(SSM_W, N_DEV, 128).transpose(1, 0, 2).reshape(4, 2, SSM_W, 128)
    g42_out = dw_out.reshape(4, 2, 128, D)
    small_a = [
        ("b_glu", (1, 2 * SSM_W), db_glu), ("w_pool", (POOL_W, 128), dw_pool.reshape(POOL_W, 128)),
        ("b_pool", (4, 128), db_pool.reshape(4, 128)), ("pool_scale", (1, POOL_W), dpscale),
        ("g_norm_ffn", (1, D), dg_ffn), ("b_conv", (1, DFF), db_conv), ("g_norm_final", (1, D), dg_fin)]
    (du2r, dcc, dbb, dlam8, dd8), ((rc_up, rc_down), (ra_glu, ra_out), parts_a) = _ssm_bwd(
        dy2.reshape(u2r.shape), u2r, xc, xs_re, xs_im, bb, cc, lam8, d8, tlen,
        comm=[(_chip_plan, [s_up, s_down]), (_pair_plan, [g42_glu, g42_out]),
              (_gather_plan, [g for _, _, g in small_a] + [dw_conv, loss_l], "small")])
    big_up = [t.T for t in _final_sum_adamw(own_up, rc_up, w_up[0].T, m_w_up[0].T, v_w_up[0].T, "final_adamw_up")]
    big_down = _final_sum_adamw(own_down, rc_down, w_down[0], m_w_down[0], v_w_down[0], "final_adamw_down")
    own_glu, s_glu = _pair_sum(g42_glu, ra_glu, place, "pair_sum_glu")
    own_out, s_out = _pair_sum(g42_out, ra_out, place, "pair_sum_out")

    def take_c(t):
        return _blockdiag_take(t, NST, GCH).transpose(0, 2, 1)

    dc_re = jnp.concatenate([take_c(dcc[0:HALF_ST, e * HALF_CH:(e + 1) * HALF_CH]) for e in range(2)], axis=0)
    dc_im = jnp.concatenate([take_c(dcc[HALF_ST:, e * HALF_CH:(e + 1) * HALF_CH]) for e in range(2)], axis=0)

    def take_b(t):
        return _blockdiag_take(t, GCH, NST).transpose(1, 0, 2)

    dbbr = jnp.concatenate([take_b(dbb[e * HALF_CH:(e + 1) * HALF_CH, 0:HALF_ST]) for e in range(2)], axis=1)
    dbbi = jnp.concatenate([take_b(dbb[e * HALF_CH:(e + 1) * HALF_CH, HALF_ST:]) for e in range(2)], axis=1)
    glr, gli, gldt, gbr, gbi, gd = _ssm_param_bwd(
        lam_r, lam_i, ldt, b_r, b_i, dlam8, dbbr.reshape(GCH, GRP * NST), dbbi.reshape(GCH, GRP * NST), dd8)
    g_log_dt = jnp.sum(gldt.reshape(GRP, NST), axis=1)

    def view(a, shp):
        return a.reshape(shp)

    small_b = [
        ("ssm_lam_re", (GRP, NST), glr.reshape(GRP, NST)), ("ssm_lam_im", (GRP, NST), gli.reshape(GRP, NST)),
        ("ssm_log_dt", (1, GRP), g_log_dt.reshape(1, GRP)),
        ("ssm_c_re", (GRP * GCH, NST), dc_re.reshape(GRP * GCH, NST)),
        ("ssm_c_im", (GRP * GCH, NST), dc_im.reshape(GRP * GCH, NST)), ("ssm_d", (1, SSM_W), gd)]
    small = small_a + small_b
    given = dict(
        ssm_lam_re=(ssm_lam_re, m_ssm_lam_re, v_ssm_lam_re), ssm_lam_im=(ssm_lam_im, m_ssm_lam_im, v_ssm_lam_im),
        ssm_log_dt=(ssm_log_dt, m_ssm_log_dt, v_ssm_log_dt), ssm_c_re=(ssm_c_re, m_ssm_c_re, v_ssm_c_re),
        ssm_c_im=(ssm_c_im, m_ssm_c_im, v_ssm_c_im), ssm_d=(ssm_d, m_ssm_d, v_ssm_d), b_glu=(b_glu, m_b_glu, v_b_glu),
        w_pool=(w_pool, m_w_pool, v_w_pool), b_pool=(b_pool, m_b_pool, v_b_pool),
        pool_scale=(pool_scale, m_pool_scale, v_pool_scale), g_norm_ffn=(g_norm_ffn, m_g_norm_ffn, v_g_norm_ffn),
        b_conv=(b_conv, m_b_conv, v_b_conv), g_norm_final=(g_norm_final, m_g_norm_final, v_g_norm_final),
        ssm_b_re=(ssm_b_re, m_ssm_b_re, v_ssm_b_re), ssm_b_im=(ssm_b_im, m_ssm_b_im, v_ssm_b_im))
    b_view = (GRP * NST, GCH)
    (grad_x, dw_in, dsh1, dsc1, dg_mix), ((rc_glu, rc_out), parts_b) = _mixer_in_bwd(
        du2r.reshape(u2.shape), dp, x, dx1, sh1, sc1, g_norm_mix, w_in_b,
        comm=[(_chip_plan, [s_glu, s_out]), (_gather_plan, [g for _, _, g in small_b] + [gbr, gbi], "small")])
    big_glu = _final_sum_adamw(own_glu, rc_glu, w_glu[0], m_w_glu[0], v_w_glu[0], "final_adamw_glu")
    big_out = _final_sum_adamw(own_out, rc_out, w_out[0], m_w_out[0], v_w_out[0], "final_adamw_out")
    parts = list(parts_a[:-2]) + list(parts_b[:-2])
    items = [(pt,) + tuple(view(a, shp) for a in given[nm]) for pt, (nm, shp, _) in zip(parts, small)]
    small_out, (g_conv_full, loss_all, gbr_all, gbi_all) = _small_sum_adamw(
        items, [parts_a[-2], parts_a[-1], parts_b[-2], parts_b[-1]])
    loss = loss_all[0, 0]
    result = {nm: [t.reshape(given[nm][0].shape) for t in quad] for quad, (nm, _, _) in zip(small_out, small)}
    for nm, g_all in (("ssm_b_re", gbr_all), ("ssm_b_im", gbi_all)):
        quad = [g_all.T] + list(_adamw_plain(g_all.T, *[view(a, b_view) for a in given[nm]]))
        result[nm] = [t.reshape(given[nm][0].shape) for t in quad]
    g_w_conv = lax.dynamic_slice_in_dim(g_conv_full, 352 * me, 352, axis=1)
    result["w_conv"] = [g_w_conv[None]] + [t[None] for t in _adamw_plain(g_w_conv, w_conv[0], m_w_conv[0], v_w_conv[0])]

    for nm, quad in (("w_glu", big_glu), ("w_out", big_out), ("w_up", big_up), ("w_down", big_down)):
        result[nm] = [t[None] for t in quad]

    dmod = jnp.concatenate([t.reshape(bsz, D) for t in (dsh1, dsc1, dgt1, dsh2, dsc2, dgt2)], axis=1)
    dmod_blk = jnp.zeros((N_DEV, 8, ncol), F32).at[:, 0:bsz].set(dmod.reshape(bsz, N_DEV, ncol).transpose(1, 0, 2))
    dmod_blk = dmod_blk.at[0, ADA_RIDER_ROW].set(dg_mix[0, 0:ncol]).at[1, ADA_RIDER_ROW, 0:D - ncol].set(dg_mix[0, ncol:])
    ada = _ada_bwd(dmod_blk.reshape(ADA_ROWS, ncol), c_all, w_ada[0], m_w_ada[0], v_w_ada[0],
                   b_ada, m_b_ada, v_b_ada, g_norm_mix, m_g_norm_mix, v_g_norm_mix,
                   dw_in.reshape(4, 2, 128, D), w_in[0], m_w_in[0], v_w_in[0])
    result["w_ada"] = [t[None] for t in ada[0:4]]
    result["b_ada"] = list(ada[4:8])
    result["g_norm_mix"] = list(ada[8:12])
    result["w_in"] = [t[None] for t in ada[12:16]]

    names = ["w_ada", "b_ada", "g_norm_mix", "w_in", "ssm_lam_re", "ssm_lam_im", "ssm_log_dt", "ssm_b_re", "ssm_b_im",
             "ssm_c_re", "ssm_c_im", "ssm_d", "w_glu", "b_glu", "w_pool", "b_pool", "pool_scale", "w_out", "g_norm_ffn",
             "w_up", "w_conv", "b_conv", "w_down", "g_norm_final"]
    return (loss, grad_x, *[result[nm][k] for k in range(4) for nm in names])
```

```python
import functools
import math

import jax
import jax.numpy as jnp
from jax import lax
from jax.experimental import pallas as pl
from jax.experimental.pallas import tpu as pltpu

F32 = jnp.float32
BF16 = jnp.bfloat16

D = 1024
SSM_W = 512
POOL_W = 512
GRP = 32
GCH = 16
NST = 64
HALF_ST = GRP * NST // 2
HALF_CH = SSM_W // 2
DFF = 2816
FF_CH = 2816
N_MOD = 6
N_DEV = 8
EPS = 1e-6
POOL_WINDOWS = (2, 4, 8, 16)
POOL_HALO = 16
CONV_HALO = 8
GELU_C = math.sqrt(2.0 / math.pi)
GELU_A = 0.044715

ADAM_LR = 0.001
ADAM_B1 = 0.9
ADAM_B2 = 0.999
ADAM_EPS = 1e-08
ADAM_WD = 0.01
ADAM_STEP = 10

VMEM_LIMIT = 56 * 1024 * 1024
TT_MIX = 1024
TT_FFN = 256
T_SSM = 128
TT_WGRAD = 2048
MESH = pl.DeviceIdType.MESH
NT = (((1,), (1,)), ((), ()))
TN = (((0,), (0,)), ((), ()))
ANY = pl.BlockSpec(memory_space=pl.ANY)
VMEM = pl.BlockSpec(memory_space=pltpu.VMEM)


def _params(n_grid, vmem=VMEM_LIMIT):
    return pltpu.CompilerParams(dimension_semantics=("arbitrary",) * n_grid, vmem_limit_bytes=vmem)


def _dot(a, b):
    return jnp.dot(a, b, preferred_element_type=F32)


def _dot_nt(a, b):
    return lax.dot_general(a, b, NT, preferred_element_type=F32)


def _dot_tn(a, b):
    return lax.dot_general(a, b, TN, preferred_element_type=F32)


def _colsum(a):
    return jnp.sum(a, axis=0, keepdims=True)


def _rms(x):
    rstd = lax.rsqrt(jnp.mean(x * x, axis=-1, keepdims=True) + EPS)
    return x * rstd, rstd


def _rms_bwd(dxhat, xhat, rstd):
    return rstd * (dxhat - xhat * jnp.mean(dxhat * xhat, axis=-1, keepdims=True))


def _gelu(x):
    return 0.5 * x * (1.0 + jnp.tanh(GELU_C * (x + GELU_A * x * x * x)))


def _gelu_grad(x):
    x2 = x * x
    th = jnp.tanh(GELU_C * (x + GELU_A * x * x2))
    return 0.5 * (1.0 + th) + 0.5 * x * (1.0 - th * th) * GELU_C * (1.0 + 3.0 * GELU_A * x2)


def _adamw(w, g, m, v):
    m = ADAM_B1 * m + (1.0 - ADAM_B1) * g
    v = ADAM_B2 * v + (1.0 - ADAM_B2) * (g * g)
    m_hat = m / (1.0 - ADAM_B1 ** ADAM_STEP)
    v_hat = v / (1.0 - ADAM_B2 ** ADAM_STEP)
    delta = -ADAM_LR * (m_hat / (jnp.sqrt(v_hat) + ADAM_EPS) + ADAM_WD * w)
    return delta, m, v


def _my_place():
    return lax.axis_index("x"), lax.axis_index("y"), lax.axis_index("c")


def _gather_plan(shards):
    n = len(shards)
    out_shape = [jax.ShapeDtypeStruct((N_DEV,) + tuple(s.shape), s.dtype) for s in shards]
    scratch = [pltpu.SemaphoreType.DMA((n, 7)), pltpu.SemaphoreType.DMA((n, 7)), pltpu.SemaphoreType.DMA((n,))]

    def stages(x_refs, out_refs, sems):
        send_sems, recv_sems, local_sems = sems
        x, y, c = _my_place()
        me, sibling = (x, y, c), (x, y, 1 - c)
        chips = [(1 - x, y), (x, 1 - y), (1 - x, 1 - y)]

        def copy(i, k, block, to, own=False):
            px, py, pc = block
            dst = out_refs[i].at[4 * px + 2 * py + pc]
            return pltpu.make_async_remote_copy(
                src_ref=x_refs[i] if own else dst, dst_ref=dst, send_sem=send_sems.at[i, k],
                recv_sem=recv_sems.at[i, k], device_id=to, device_id_type=MESH)

        def mine(i):
            return pltpu.make_async_copy(x_refs[i], out_refs[i].at[4 * x + 2 * y + c], local_sems.at[i])

        def start():
            for i in range(n):
                mine(i).start()
                copy(i, 0, me, sibling, own=True).start()
                for j, chip in enumerate(chips):
                    copy(i, 1 + j, me, (*chip, c), own=True).start()

        def forward():
            for i in range(n):
                for j, chip in enumerate(chips):
                    copy(i, 1 + j, (*chip, c), me).wait_recv()
                    copy(i, 4 + j, (*chip, c), sibling).start()

        def finish():
            for i in range(n):
                copy(i, 0, sibling, me).wait_recv()
                copy(i, 0, me, sibling, own=True).wait_send()
                for j, chip in enumerate(chips):
                    copy(i, 4 + j, (*chip, 1 - c), me).wait_recv()
                    copy(i, 1 + j, me, (*chip, c), own=True).wait_send()
                    copy(i, 4 + j, (*chip, c), sibling).wait_send()
                mine(i).wait()

        return [start, forward, finish]

    return n, out_shape, scratch, stages


def _pair_plan(g42s):
    n = len(g42s)
    out_shape = [jax.ShapeDtypeStruct((4,) + tuple(g.shape[2:]), g.dtype) for g in g42s]
    scratch = [pltpu.SemaphoreType.DMA((n,)), pltpu.SemaphoreType.DMA((n,))]

    def stages(g_refs, out_refs, sems):
        send_sems, recv_sems = sems
        x, y, c = _my_place()

        def copy(i):
            return pltpu.make_async_remote_copy(
                src_ref=g_refs[i].at[:, 1 - c], dst_ref=out_refs[i], send_sem=send_sems.at[i],
                recv_sem=recv_sems.at[i], device_id=(x, y, 1 - c), device_id_type=MESH)

        def start():
            for i in range(n):
                copy(i).start()

        def finish():
            for i in range(n):
                copy(i).wait()

        return [start, finish]

    return n, out_shape, scratch, stages


def _chip_plan(s4s):
    n = len(s4s)
    out_shape = [jax.ShapeDtypeStruct((3,) + tuple(s.shape[1:]), s.dtype) for s in s4s]
    scratch = [pltpu.SemaphoreType.DMA((n, 3)), pltpu.SemaphoreType.DMA((n, 3))]

    def stages(s_refs, out_refs, sems):
        send_sems, recv_sems = sems
        x, y, c = _my_place()

        def copy(i, d):
            px, py = x ^ (d >> 1), y ^ (d & 1)
            return pltpu.make_async_remote_copy(
                src_ref=s_refs[i].at[2 * px + py], dst_ref=out_refs[i].at[d - 1], send_sem=send_sems.at[i, d - 1],
                recv_sem=recv_sems.at[i, d - 1], device_id=(px, py, c), device_id_type=MESH)

        def start():
            for i in range(n):
                for d in (1, 2, 3):
                    copy(i, d).start()

        def finish():
            for i in range(n):
                for d in (1, 2, 3):
                    copy(i, d).wait()

        return [start, finish]

    return n, out_shape, scratch, stages


def _comm_call(plan, arrays, name):
    n, out_shape, scratch, stages = plan

    def body(*refs):
        for stage in stages(refs[:n], refs[n:2 * n], refs[2 * n:]):
            stage()

    return pl.pallas_call(
        body, name=name, out_shape=out_shape, in_specs=[ANY] * n, out_specs=[ANY] * n, scratch_shapes=scratch,
    )(*arrays)


def _fused_call(body, *, name, grid, in_specs, out_specs, out_shape, scratch_shapes, args, comm=None):
    if not comm:
        out = pl.pallas_call(body, name=name, grid=grid, in_specs=in_specs, out_specs=out_specs, out_shape=out_shape,
                             scratch_shapes=scratch_shapes, compiler_params=_params(len(grid)))(*args)
        return out, []
    counts = [plan[0] for plan, _, _ in comm]
    n = sum(counts)
    n_in, n_out, n_scr = len(in_specs), len(out_specs), len(scratch_shapes)

    def fused(*refs):
        ins, refs = refs[:n_in], refs[n_in:]
        c_ins, refs = refs[:n], refs[n:]
        outs, refs = refs[:n_out], refs[n_out:]
        c_outs, refs = refs[:n], refs[n:]
        scr, c_scr = refs[:n_scr], refs[n_scr:]
        step = pl.program_id(0)
        for k in range(1, len(grid)):
            step = step * grid[k] + pl.program_id(k)
        todo, a0, s0 = [], 0, 0
        for (cnt, _, plan_scratch, stages), _, steps in comm:
            sems = c_scr[s0:s0 + len(plan_scratch)]
            todo += list(zip(stages(c_ins[a0:a0 + cnt], c_outs[a0:a0 + cnt], sems), steps))
            a0 += cnt
            s0 += len(plan_scratch)
        for stage, at in todo:
            if at == 0:
                pl.when(step == 0)(stage)
        body(*ins, *outs, *scr)
        for stage, at in todo:
            if at != 0:
                pl.when(step == at)(stage)

    c_shape = [s for plan, _, _ in comm for s in plan[1]]
    c_scratch = [s for plan, _, _ in comm for s in plan[2]]
    arrays = [a for _, arrs, _ in comm for a in arrs]
    out = pl.pallas_call(
        fused, name=name, grid=grid, in_specs=list(in_specs) + [ANY] * n, out_specs=list(out_specs) + [ANY] * n,
        out_shape=list(out_shape) + c_shape, scratch_shapes=list(scratch_shapes) + c_scratch,
        compiler_params=_params(len(grid)))(*args, *arrays)
    outs, c_outs, split, a0 = out[:n_out], out[n_out:], [], 0
    for cnt in counts:
        split.append(c_outs[a0:a0 + cnt])
        a0 += cnt
    return outs, split


def _schedule(comm, n_steps):
    out = []
    for make_plan, arrays, *small in comm or []:
        middle = n_steps - 1 if small else (3 * n_steps) // 4
        steps = [0, middle, n_steps - 1] if make_plan is _gather_plan else [0, n_steps - 1]
        out.append((make_plan(arrays), arrays, steps))
    return out


def _row_tile(r):
    for t in (128, 64, 32, 16, 8):
        if r % t == 0:
            return t
    return r


def _pair_sum(g42, recv, place, name):
    _, _, r, cdim = g42.shape
    tr = _row_tile(r)

    def body(pl_ref, g_ref, r_ref, own_ref, s_ref):
        s_ref[...] = (g_ref[:, 0] + r_ref[...]).astype(BF16)
        q = pl_ref[1]
        own_ref[...] = g_ref[q, 0] + r_ref[q]

    return pl.pallas_call(
        body, name=name,
        grid_spec=pltpu.PrefetchScalarGridSpec(
            num_scalar_prefetch=1, grid=(r // tr,),
            in_specs=[pl.BlockSpec((4, 1, tr, cdim), lambda i, p: (0, p[0], i, 0)),
                      pl.BlockSpec((4, tr, cdim), lambda i, p: (0, i, 0))],
            out_specs=[pl.BlockSpec((tr, cdim), lambda i, p: (i, 0)),
                       pl.BlockSpec((4, tr, cdim), lambda i, p: (0, i, 0))]),
        out_shape=[jax.ShapeDtypeStruct((r, cdim), F32), jax.ShapeDtypeStruct((4, r, cdim), BF16)],
        compiler_params=_params(1),
    )(place, g42, recv)


def _final_sum_adamw(own, recv3, w, m, v, name):
    r, cdim = w.shape
    tr = _row_tile(r)

    def body(s_ref, r_ref, w_ref, m_ref, v_ref, g_out, d_out, m_out, v_out):
        g = s_ref[...] + r_ref[0].astype(F32) + r_ref[1].astype(F32) + r_ref[2].astype(F32)
        d, mn, vn = _adamw(w_ref[...], g, m_ref[...], v_ref[...])
        g_out[...] = g
        d_out[...] = d
        m_out[...] = mn
        v_out[...] = vn

    blk = pl.BlockSpec((tr, cdim), lambda i: (i, 0))
    shp = jax.ShapeDtypeStruct((r, cdim), F32)
    return pl.pallas_call(
        body, name=name, grid=(r // tr,),
        in_specs=[blk, pl.BlockSpec((3, tr, cdim), lambda i: (0, i, 0)), blk, blk, blk],
        out_specs=[blk, blk, blk, blk], out_shape=[shp, shp, shp, shp], compiler_params=_params(1),
    )(own, recv3, w, m, v)


def _small_sum_adamw(items, sums_only):
    n, ne = len(items), len(sums_only)

    def total(p_ref):
        g = p_ref[0]
        for k in range(1, N_DEV):
            g = g + p_ref[k]
        return g

    def body(*refs):
        ins, outs = refs[:4 * n + ne], refs[4 * n + ne:]
        for i in range(n):
            p_ref, w_ref, m_ref, v_ref = ins[4 * i:4 * i + 4]
            g = total(p_ref)
            d, mn, vn = _adamw(w_ref[...], g, m_ref[...], v_ref[...])
            for o_ref, val in zip(outs[4 * i:4 * i + 4], (g, d, mn, vn)):
                o_ref[...] = val
        for j in range(ne):
            outs[4 * n + j][...] = total(ins[4 * n + j])

    args = [a for item in items for a in item] + list(sums_only)
    shapes = [jax.ShapeDtypeStruct(w.shape, F32) for _, w, _, _ in items for _ in range(4)]
    shapes += [jax.ShapeDtypeStruct(p.shape[1:], F32) for p in sums_only]
    out = pl.pallas_call(
        body, name="small_sum_adamw", in_specs=[VMEM] * len(args), out_specs=[VMEM] * len(shapes), out_shape=shapes,
        compiler_params=_params(0),
    )(*args)
    return [out[4 * i:4 * i + 4] for i in range(n)], out[4 * n:]


def _adamw_plain(g, w, m, v):
    r, cdim = w.shape
    tr = r if r * cdim <= 64 * 1024 else _row_tile(r)

    def body(g_ref, w_ref, m_ref, v_ref, d_out, m_out, v_out):
        d, mn, vn = _adamw(w_ref[...], g_ref[...], m_ref[...], v_ref[...])
        d_out[...] = d
        m_out[...] = mn
        v_out[...] = vn

    blk = pl.BlockSpec((tr, cdim), lambda i: (i, 0))
    shp = jax.ShapeDtypeStruct((r, cdim), F32)
    return pl.pallas_call(
        body, name="adamw_plain", grid=(r // tr,), in_specs=[blk, blk, blk, blk],
        out_specs=[blk, blk, blk], out_shape=[shp, shp, shp], compiler_params=_params(1),
    )(g, w, m, v)


ADA_COLS = N_MOD * D // N_DEV
ADA_ROWS = 8 * N_DEV


def _ada_fwd(cpad, w_ada, b_blocks, mixer_shards):
    n_w, w_shape, w_scr, w_stages = _gather_plan(mixer_shards)
    _, _, c_scr, c_stages = _gather_plan([cpad])
    _, _, p_scr, p_stages = _gather_plan([jax.ShapeDtypeStruct((ADA_ROWS, ADA_COLS), F32)])

    def body(c_ref, wa_ref, b_ref, *refs):
        w_refs, refs = refs[:n_w], refs[n_w:]
        cg_ref, call_ref, mod_ref = refs[:3]
        wg_refs, refs = refs[3:3 + n_w], refs[3 + n_w:]
        part_ref, pg_ref = refs[:2]
        c_sems, p_sems, w_sems = refs[2:5], refs[5:8], refs[8:11]
        w_start, w_forward, w_finish = w_stages(w_refs, wg_refs, w_sems)
        w_start()
        for stage in c_stages([c_ref], [cg_ref], c_sems):
            stage()
        cv = cg_ref[:, 0:8, :].reshape(ADA_ROWS, D)
        call_ref[...] = cv
        part_ref[...] = _dot(cv * jax.nn.sigmoid(cv), wa_ref[...])
        for stage in p_stages([part_ref], [pg_ref], p_sems):
            stage()
        x, y, c = _my_place()
        r0 = pl.multiple_of(8 * (4 * x + 2 * y + c), 8)
        for k in range(N_DEV):
            mod_ref[:, k * ADA_COLS:(k + 1) * ADA_COLS] = pg_ref[k, pl.ds(r0, 8), :] + b_ref[k]
        w_forward()
        w_finish()

    out = pl.pallas_call(
        body, name="ada_fwd", in_specs=[VMEM, VMEM, VMEM] + [ANY] * n_w,
        out_specs=[VMEM, VMEM, VMEM] + [ANY] * n_w,
        out_shape=[jax.ShapeDtypeStruct((N_DEV,) + cpad.shape, F32), jax.ShapeDtypeStruct((ADA_ROWS, D), F32),
                   jax.ShapeDtypeStruct((8, N_MOD * D), F32)] + list(w_shape),
        scratch_shapes=[pltpu.VMEM((ADA_ROWS, ADA_COLS), F32), pltpu.VMEM((N_DEV, ADA_ROWS, ADA_COLS), F32)]
        + list(c_scr) + list(p_scr) + list(w_scr),
        compiler_params=_params(0),
    )(cpad, w_ada, b_blocks, *mixer_shards)
    return out[0], out[1], out[2], out[3:]


ADA_RIDER_ROW = 4


def _ada_bwd(dmod_blk, c_all, w_ada, m_w, v_w, b_blocks, m_b, v_b, g_w, g_m, g_v, g42, w_l, m_l, v_l):
    _, _, g_scr, g_stages = _gather_plan([dmod_blk])
    rest = D - ADA_COLS
    blk = tuple(g42.shape[2:])

    def body(dm_ref, c_ref, w_ref, mw_ref, vw_ref, b_ref, mb_ref, vb_ref, gw_ref, gm_ref, gv_ref,
             g42_ref, wl_ref, ml_ref, vl_ref,
             gw_o, dw_o, mw_o, vw_o, gb_o, dbb_o, mb_o, vb_o, gg_o, dgg_o, mg_o, vg_o, gl_o, dl_o, ml_o, vl_o,
             dg_ref, pr_ref, sbf_ref, rc_ref, pair_send, pair_recv, chip_send, chip_recv, *sems):
        x, y, c = _my_place()
        q = 2 * x + y
        pair = pltpu.make_async_remote_copy(
            src_ref=g42_ref.at[:, 1 - c], dst_ref=pr_ref, send_sem=pair_send, recv_sem=pair_recv,
            device_id=(x, y, 1 - c), device_id_type=MESH)
        pair.start()
        for stage in g_stages([dm_ref], [dg_ref], sems):
            stage()
        pair.wait()
        for k in range(4):
            sbf_ref[k] = (g42_ref[k, c] + pr_ref[k]).astype(BF16)

        def chip_copy(dist):
            px, py = x ^ (dist >> 1), y ^ (dist & 1)
            return pltpu.make_async_remote_copy(
                src_ref=sbf_ref.at[2 * px + py], dst_ref=rc_ref.at[dist - 1], send_sem=chip_send.at[dist - 1],
                recv_sem=chip_recv.at[dist - 1], device_id=(px, py, c), device_id_type=MESH)

        for dist in (1, 2, 3):
            chip_copy(dist).start()
        r0 = pl.multiple_of(8 * (4 * x + 2 * y + c), 8)
        cols = dg_ref[:, pl.ds(r0, 8), :].reshape(ADA_ROWS, ADA_COLS)
        cv = c_ref[...]
        gw = _dot_tn(cv * jax.nn.sigmoid(cv), cols)
        d, mn, vn = _adamw(w_ref[...], gw, mw_ref[...], vw_ref[...])
        gw_o[...] = gw
        dw_o[...] = d
        mw_o[...] = mn
        vw_o[...] = vn
        is_example = lax.broadcasted_iota(jnp.int32, (8, 1), 0) < ADA_RIDER_ROW
        blocks = []
        for k in range(N_DEV):
            s = dg_ref[0, 8 * k:8 * k + 8, :]
            for dev in range(1, N_DEV):
                s = s + dg_ref[dev, 8 * k:8 * k + 8, :]
            blocks.append(s)
            gb = _colsum(jnp.where(is_example, s, 0.0))
            cs = slice(k * ADA_COLS, (k + 1) * ADA_COLS)
            d, mn, vn = _adamw(b_ref[:, cs], gb, mb_ref[:, cs], vb_ref[:, cs])
            gb_o[:, cs] = gb
            dbb_o[:, cs] = d
            mb_o[:, cs] = mn
            vb_o[:, cs] = vn
        rider = jnp.concatenate([blocks[0][ADA_RIDER_ROW:ADA_RIDER_ROW + 1, :],
                                 blocks[1][ADA_RIDER_ROW:ADA_RIDER_ROW + 1, 0:rest]], axis=1)
        d, mn, vn = _adamw(gw_ref[...], rider, gm_ref[...], gv_ref[...])
        gg_o[...] = rider
        dgg_o[...] = d
        mg_o[...] = mn
        vg_o[...] = vn
        for dist in (1, 2, 3):
            chip_copy(dist).wait()
        gl = g42_ref[q, c] + pr_ref[q] + rc_ref[0].astype(F32) + rc_ref[1].astype(F32) + rc_ref[2].astype(F32)
        d, mn, vn = _adamw(wl_ref[...], gl, ml_ref[...], vl_ref[...])
        gl_o[...] = gl
        dl_o[...] = d
        ml_o[...] = mn
        vl_o[...] = vn

    ws = jax.ShapeDtypeStruct(w_ada.shape, F32)
    bs = jax.ShapeDtypeStruct(b_blocks.shape, F32)
    gs = jax.ShapeDtypeStruct(g_w.shape, F32)
    ls = jax.ShapeDtypeStruct(w_l.shape, F32)
    return pl.pallas_call(
        body, name="ada_bwd", in_specs=[VMEM] * 15, out_specs=[VMEM] * 16,
        out_shape=[ws, ws, ws, ws, bs, bs, bs, bs, gs, gs, gs, gs, ls, ls, ls, ls],
        scratch_shapes=[pltpu.VMEM((N_DEV, ADA_ROWS, ADA_COLS), F32), pltpu.VMEM((4,) + blk, F32),
                        pltpu.VMEM((4,) + blk, BF16), pltpu.VMEM((3,) + blk, BF16),
                        pltpu.SemaphoreType.DMA(()), pltpu.SemaphoreType.DMA(()),
                        pltpu.SemaphoreType.DMA((3,)), pltpu.SemaphoreType.DMA((3,))] + list(g_scr),
        compiler_params=_params(0),
    )(dmod_blk, c_all, w_ada, m_w, v_w, b_blocks, m_b, v_b, g_w, g_m, g_v, g42, w_l, m_l, v_l)


def _ssm_param_fn(lr, li, ldt, br, bi):
    dt = jnp.exp(ldt)
    mag = jnp.exp(lr * dt)
    ang = li * dt
    lbr = mag * jnp.cos(ang)
    lbi = mag * jnp.sin(ang)
    nr = lbr - 1.0
    den = lr * lr + li * li
    cr = (nr * lr + lbi * li) / den
    ci = (lbi * lr - nr * li) / den
    return lbr, lbi, cr * br - ci * bi, cr * bi + ci * br


def _ssm_prep(lr, li, ldt, br, bi):
    def body(lr_ref, li_ref, ldt_ref, br_ref, bi_ref, lbr_o, lbi_o, bbr_o, bbi_o):
        lbr, lbi, bbr, bbi = _ssm_param_fn(lr_ref[...], li_ref[...], ldt_ref[...], br_ref[...], bi_ref[...])
        lbr_o[...] = lbr
        lbi_o[...] = lbi
        bbr_o[...] = bbr
        bbi_o[...] = bbi

    row = jax.ShapeDtypeStruct(lr.shape, F32)
    mat = jax.ShapeDtypeStruct(br.shape, F32)
    return pl.pallas_call(
        body, name="ssm_prep", in_specs=[VMEM] * 5, out_specs=[VMEM] * 4,
        out_shape=[row, row, mat, mat], compiler_params=_params(0),
    )(lr, li, ldt, br, bi)


def _ssm_param_bwd(lr, li, ldt, br, bi, dlam8, dbbr, dbbi, dd8):
    nv = dlam8.shape[1]

    def body(lr_ref, li_ref, ldt_ref, br_ref, bi_ref, dl_ref, dbr_ref, dbi_ref, dd_ref,
             glr_o, gli_o, gldt_o, gbr_o, gbi_o, gd_o):
        halves_r, halves_i, halves_d = [], [], []
        for e in range(2):
            ar = dl_ref[0, e:e + 1, :]
            ai = dl_ref[1, e:e + 1, :]
            ad = dd_ref[e:e + 1, :]
            for b in range(1, nv // 2):
                ar = ar + dl_ref[0, 2 * b + e:2 * b + e + 1, :]
                ai = ai + dl_ref[1, 2 * b + e:2 * b + e + 1, :]
                ad = ad + dd_ref[2 * b + e:2 * b + e + 1, :]
            halves_r.append(ar)
            halves_i.append(ai)
            halves_d.append(ad)
        dlbr = jnp.concatenate(halves_r, axis=1)
        dlbi = jnp.concatenate(halves_i, axis=1)
        gd_o[...] = jnp.concatenate(halves_d, axis=1)
        _, vjp = jax.vjp(_ssm_param_fn, lr_ref[...], li_ref[...], ldt_ref[...], br_ref[...], bi_ref[...])
        glr, gli, gldt, gbr, gbi = vjp((dlbr, dlbi, dbr_ref[...], dbi_ref[...]))
        glr_o[...] = glr
        gli_o[...] = gli
        gldt_o[...] = gldt
        gbr_o[...] = gbr
        gbi_o[...] = gbi

    row = jax.ShapeDtypeStruct(lr.shape, F32)
    mat = jax.ShapeDtypeStruct(br.shape, F32)
    return pl.pallas_call(
        body, name="ssm_param_bwd", in_specs=[VMEM] * 9, out_specs=[VMEM] * 6,
        out_shape=[row, row, row, mat, mat, jax.ShapeDtypeStruct((1, SSM_W), F32)],
        compiler_params=_params(0),
    )(lr, li, ldt, br, bi, dlam8, dbbr, dbbi, dd8)


def _blockdiag(m):
    _, g, a, b = m.shape
    eye = jnp.eye(g, dtype=m.dtype)
    return jnp.einsum("egab,gk->egakb", m, eye).reshape(2, g * a, g * b)


def _blockdiag_take(t, a, b):
    return jnp.einsum("gagb->gab", t.reshape(GRP // 2, a, GRP // 2, b))


def _mixer_in_fwd(x, sh1, sc1, g_mix, w_in_b, comm=None):
    bsz, seq, _ = x.shape
    tt = min(seq, TT_MIX)

    def body(x_ref, sh_ref, sc_ref, g_ref, w_ref, u_ref, p_ref):
        xhat, _ = _rms(x_ref[0])
        h = xhat * g_ref[...] * (1.0 + sc_ref[0]) + sh_ref[0]
        z = _dot(h.astype(BF16), w_ref[...])
        u_ref[...] = z[:, :SSM_W].astype(BF16)
        p_ref[0] = z[:, SSM_W:]

    row = pl.BlockSpec((1, 1, D), lambda b, t: (b, 0, 0))
    return _fused_call(
        body, name="mixer_in_fwd", grid=(bsz, seq // tt),
        in_specs=[pl.BlockSpec((1, tt, D), lambda b, t: (b, t, 0)), row, row,
                  pl.BlockSpec((1, D), lambda b, t: (0, 0)), VMEM],
        out_specs=[pl.BlockSpec((tt, SSM_W), lambda b, t: (t, b)),
                   pl.BlockSpec((1, tt, POOL_W), lambda b, t: (b, t, 0))],
        out_shape=[jax.ShapeDtypeStruct((seq, bsz * SSM_W), BF16), jax.ShapeDtypeStruct((bsz, seq, POOL_W), F32)],
        scratch_shapes=[], args=(x, sh1, sc1, g_mix, w_in_b), comm=_schedule(comm, bsz * (seq // tt)))


def _ssm_project_in(ub, par0, bb_ref, s_re, s_im, row0, tlen, nv):
    for part, sref in ((0, s_re), (1, s_im)):
        for k in range(HALF_ST // 512):
            c0 = part * HALF_ST + k * 512
            a0 = _dot(ub, bb_ref[:, c0:c0 + 512])
            a1 = _dot(ub, bb_ref[:, 2 * HALF_ST + c0:2 * HALF_ST + c0 + 512])
            sref[pl.ds(row0, tlen), :, k * 512:(k + 1) * 512] = jnp.where(par0, a0, a1).reshape(tlen, nv, 512)


def _ssm_fwd(u2r, bb, cc, lam8, d8, tlen, comm=None):
    nv = lam8.shape[1]
    rows = nv * tlen
    n_chunks = u2r.shape[0] // rows

    def body(u_ref, bb_ref, cc_ref, lam_ref, d_ref, y_ref, xc_ref, xre_ref, xim_ref, s_re, s_im, st):
        @pl.when(pl.program_id(0) == 0)
        def _():
            st[...] = jnp.zeros_like(st)

        xc_ref[0] = st[...]
        ub = u_ref[...]
        u = ub.astype(F32)
        par0 = (lax.broadcasted_iota(jnp.int32, (rows, 1), 0) % 2) == 0
        _ssm_project_in(ub, par0, bb_ref, s_re, s_im, 0, tlen, nv)
        for hb in range(HALF_ST // 512):
            ls = slice(hb * 512, (hb + 1) * 512)
            lr = lam_ref[0, :, ls]
            li = lam_ref[1, :, ls]

            def step(t, carry, ls=ls, lr=lr, li=li):
                xr, xi = carry
                nr = lr * xr - li * xi + s_re[t, :, ls]
                ni = lr * xi + li * xr + s_im[t, :, ls]
                s_re[t, :, ls] = nr
                s_im[t, :, ls] = ni
                return nr, ni

            xr, xi = lax.fori_loop(0, tlen, step, (st[0, :, ls], st[1, :, ls]), unroll=8)
            st[0, :, ls] = xr
            st[1, :, ls] = xi
        xre = s_re[...].reshape(rows, HALF_ST).astype(BF16)
        xim = s_im[...].reshape(rows, HALF_ST).astype(BF16)
        xre_ref[...] = xre
        xim_ref[...] = xim
        y2 = _dot(xre, cc_ref[0:HALF_ST, :]) + _dot(xim, cc_ref[HALF_ST:, :])
        y = jnp.where(par0, y2[:, :HALF_CH], y2[:, HALF_CH:])
        skip = (u.reshape(tlen, nv, HALF_CH) * d_ref[...][None]).reshape(rows, HALF_CH)
        y_ref[...] = (y + skip).astype(BF16)

    st_blk = pl.BlockSpec((rows, HALF_ST), lambda c: (c, 0))
    st_shape = jax.ShapeDtypeStruct((u2r.shape[0], HALF_ST), BF16)
    return _fused_call(
        body, name="ssm_fwd", grid=(n_chunks,),
        in_specs=[pl.BlockSpec((rows, HALF_CH), lambda c: (c, 0)), VMEM, VMEM, VMEM, VMEM],
        out_specs=[pl.BlockSpec((rows, HALF_CH), lambda c: (c, 0)),
                   pl.BlockSpec((1, 2, nv, HALF_ST), lambda c: (c, 0, 0, 0)), st_blk, st_blk],
        out_shape=[jax.ShapeDtypeStruct(u2r.shape, BF16), jax.ShapeDtypeStruct((n_chunks, 2, nv, HALF_ST), F32),
                   st_shape, st_shape],
        scratch_shapes=[pltpu.VMEM((tlen, nv, HALF_ST), F32), pltpu.VMEM((tlen, nv, HALF_ST), F32),
                        pltpu.VMEM((2, nv, HALF_ST), F32)],
        args=(u2r, bb, cc, lam8, d8), comm=_schedule(comm, n_chunks))


def _pool_forward(ext, pv, pos, wp_ref, bp_ref):
    cur = ext
    zs, zls = [], []
    for gi, w in enumerate(POOL_WINDOWS):
        cur = cur + pltpu.roll(cur, w // 2, 0)
        sw = cur[POOL_HALO:, 0:128]
        z = sw / jnp.minimum(pos, float(w)) - pv[:, gi * 128:(gi + 1) * 128]
        zs.append(z)
        zls.append(_dot(z.astype(BF16), wp_ref[gi]) + bp_ref[:, gi * 128:(gi + 1) * 128])
        if gi + 1 < len(POOL_WINDOWS):
            cur = cur[:, 128:]
    return zs, zls


def _mixer_out_fwd(y2, p, x, gt1, w_glu_b, b_glu, w_pool_b, b_pool, pscale, w_out_b):
    bsz, seq, _ = x.shape
    tt = min(seq, TT_MIX)

    def body(y_ref, p_ref, x_ref, gt_ref, wg_ref, bg_ref, wp_ref, bp_ref, ps_ref, wo_ref, x1_ref, mix_ref, mxd_ref,
             ext):
        ti = pl.program_id(1)

        @pl.when(ti == 0)
        def _():
            ext[0:POOL_HALO, :] = jnp.zeros((POOL_HALO, POOL_W), F32)

        pv = p_ref[0]
        ext[POOL_HALO:, :] = pv
        pos = (ti * tt + lax.broadcasted_iota(jnp.int32, (tt, 1), 0) + 1).astype(F32)
        _, zls = _pool_forward(ext[...], pv, pos, wp_ref, bp_ref)
        ext[0:POOL_HALO, :] = pv[tt - POOL_HALO:, :]
        a = _gelu(y_ref[...].astype(F32))
        gl = _dot(a.astype(BF16), wg_ref[...]) + bg_ref[...]
        y_ssm = gl[:, :SSM_W] * jax.nn.sigmoid(gl[:, SSM_W:])
        y_pool = [zl * ps_ref[:, gi * 128:(gi + 1) * 128] for gi, zl in enumerate(zls)]
        mixcat = jnp.concatenate([y_ssm] + y_pool, axis=1).astype(BF16)
        mix_ref[0] = mixcat
        mixed = _dot(mixcat, wo_ref[...])
        mxd_ref[0] = mixed.astype(BF16)
        x1_ref[0] = x_ref[0] + gt_ref[0] * mixed

    xt = pl.BlockSpec((1, tt, D), lambda b, t: (b, t, 0))
    return pl.pallas_call(
        body, name="mixer_out_fwd", grid=(bsz, seq // tt),
        in_specs=[pl.BlockSpec((tt, SSM_W), lambda b, t: (t, b)),
                  pl.BlockSpec((1, tt, POOL_W), lambda b, t: (b, t, 0)), xt,
                  pl.BlockSpec((1, 1, D), lambda b, t: (b, 0, 0)), VMEM, VMEM, VMEM, VMEM, VMEM, VMEM],
        out_specs=[xt, xt, xt],
        out_shape=[jax.ShapeDtypeStruct(x.shape, F32), jax.ShapeDtypeStruct(x.shape, BF16),
                   jax.ShapeDtypeStruct(x.shape, BF16)],
        scratch_shapes=[pltpu.VMEM((POOL_HALO + tt, POOL_W), F32)],
        compiler_params=_params(2),
    )(y2, p, x, gt1, w_glu_b, b_glu, w_pool_b, b_pool, pscale, w_out_b)


def _conv_gate(g, ge, wc, bc):
    g1 = pltpu.roll(ge, 1, 0)[CONV_HALO:]
    g2 = pltpu.roll(ge, 2, 0)[CONV_HALO:]
    return wc[2:3] * g + wc[1:2] * g1 + wc[0:1] * g2 + bc, g1, g2


def _ffn_fwd(x1, tgt, sh2, sc2, gt2, g_ffn, w_up_b, w_conv, b_conv, w_down_b, g_fin):
    bsz, seq, _ = x1.shape
    tt = min(seq, TT_FFN)
    n_t = seq // tt
    n_ck = DFF // FF_CH

    def body(x1_ref, tg_ref, sh_ref, sc_ref, gt_ref, gf_ref, wu_ref, wc_ref, bc_ref, wd_ref, gfin_ref,
             h2_ref, v_ref, g_ref, gc_ref, act_ref, ddn_ref, dx2_ref, loss_ref, dgfin_ref, dgt_ref, gext, lacc):
        b = pl.program_id(0)
        ti = pl.program_id(1)

        @pl.when((b == 0) & (ti == 0))
        def _():
            lacc[...] = jnp.zeros_like(lacc)
            dgfin_ref[...] = jnp.zeros_like(dgfin_ref)

        @pl.when(ti == 0)
        def _():
            dgt_ref[...] = jnp.zeros_like(dgt_ref)
            gext[:, 0:CONV_HALO, :] = jnp.zeros((n_ck, CONV_HALO, FF_CH), F32)

        x1v = x1_ref[0]
        xhat, _ = _rms(x1v)
        h2b = (xhat * gf_ref[...] * (1.0 + sc_ref[0]) + sh_ref[0]).astype(BF16)
        h2_ref[0] = h2b
        dn = jnp.zeros((tt, D), F32)
        for ck in range(n_ck):
            c0 = ck * FF_CH
            v = _dot_nt(h2b, wu_ref[c0:c0 + FF_CH, :])
            g = _dot_nt(h2b, wu_ref[DFF + c0:DFF + c0 + FF_CH, :])
            v_ref[0, :, c0:c0 + FF_CH] = v.astype(BF16)
            g_ref[0, :, c0:c0 + FF_CH] = g.astype(BF16)
            gext[ck, CONV_HALO:, :] = g
            gc, _, _ = _conv_gate(g, gext[ck], wc_ref[:, c0:c0 + FF_CH], bc_ref[:, c0:c0 + FF_CH])
            gext[ck, 0:CONV_HALO, :] = g[tt - CONV_HALO:, :]
            gc_ref[0, :, c0:c0 + FF_CH] = gc.astype(BF16)
            actb = (gc * jax.nn.sigmoid(gc) * v).astype(BF16)
            act_ref[0, :, c0:c0 + FF_CH] = actb
            dn = dn + _dot(actb, wd_ref[c0:c0 + FF_CH, :])
        gt = gt_ref[0]
        xh3, r3 = _rms(x1v + gt * dn)
        gfin = gfin_ref[...]
        diff = xh3 * gfin - tg_ref[0]
        lacc[...] += _colsum(diff * diff)
        dy = diff * (1.0 / D)
        dgfin_ref[...] += _colsum(dy * xh3)
        dx2 = _rms_bwd(dy * gfin, xh3, r3)
        dx2_ref[0] = dx2
        dgt_ref[0] += _colsum(dx2 * dn)
        ddn_ref[0] = (gt * dx2).astype(BF16)

        @pl.when((b == bsz - 1) & (ti == n_t - 1))
        def _():
            loss_ref[...] = jnp.full(loss_ref.shape, 0.5 / D * jnp.sum(lacc[...]), F32)

    xt = pl.BlockSpec((1, tt, D), lambda b, t: (b, t, 0))
    ft = pl.BlockSpec((1, tt, DFF), lambda b, t: (b, t, 0))
    row = pl.BlockSpec((1, 1, D), lambda b, t: (b, 0, 0))
    vec = pl.BlockSpec((1, D), lambda b, t: (0, 0))
    ff = jax.ShapeDtypeStruct((bsz, seq, DFF), BF16)
    xs = jax.ShapeDtypeStruct((bsz, seq, D), BF16)
    return pl.pallas_call(
        body, name="ffn_fwd", grid=(bsz, n_t),
        in_specs=[xt, xt, row, row, row, vec, VMEM, VMEM, VMEM, VMEM, vec],
        out_specs=[xt, ft, ft, ft, ft, xt, xt, pl.BlockSpec((1, 128), lambda b, t: (0, 0)), vec, row],
        out_shape=[xs, ff, ff, ff, ff, xs, jax.ShapeDtypeStruct((bsz, seq, D), F32),
                   jax.ShapeDtypeStruct((1, 128), F32), jax.ShapeDtypeStruct((1, D), F32),
                   jax.ShapeDtypeStruct((bsz, 1, D), F32)],
        scratch_shapes=[pltpu.VMEM((n_ck, CONV_HALO + tt, FF_CH), F32), pltpu.VMEM((1, D), F32)],
        compiler_params=_params(2),
    )(x1, tgt, sh2, sc2, gt2, g_ffn, w_up_b, w_conv, b_conv, w_down_b, g_fin)


def _ffn_bwd(ddn, gq, gcq, vq, x1, dx2, sh2, sc2, g_ffn, w_conv, w_down_b, w_up_b):
    bsz, seq, _ = x1.shape
    tt = min(seq, TT_FFN)
    n_t = seq // tt
    n_ck = DFF // FF_CH
    ext_rows = tt + CONV_HALO

    def body(ddn_ref, g_ref, gc_ref, v_ref, x1_ref, dx2_ref, sh_ref, sc_ref, gf_ref, wc_ref, wd_ref,
             wu_ref, dup_ref, dx1_ref, dsh_ref, dsc_ref, dgf_ref, dwc_ref, dbc_ref, dext):
        b = pl.program_id(0)
        i = pl.program_id(1)

        @pl.when((b == 0) & (i == 0))
        def _():
            dgf_ref[...] = jnp.zeros_like(dgf_ref)
            dwc_ref[...] = jnp.zeros_like(dwc_ref)
            dbc_ref[...] = jnp.zeros_like(dbc_ref)

        @pl.when(i == 0)
        def _():
            dsh_ref[...] = jnp.zeros_like(dsh_ref)
            dsc_ref[...] = jnp.zeros_like(dsc_ref)
            dext[:, tt:, :] = jnp.zeros((n_ck, CONV_HALO, FF_CH), F32)

        ddnv = ddn_ref[0]
        dh2 = jnp.zeros((tt, D), F32)
        for ck in range(n_ck):
            c0 = ck * FF_CH
            dact = _dot_nt(ddnv, wd_ref[c0:c0 + FF_CH, :])
            g = g_ref[0, :, c0:c0 + FF_CH].astype(F32)
            gc = gc_ref[0, :, c0:c0 + FF_CH].astype(F32)
            v = v_ref[0, :, c0:c0 + FF_CH].astype(F32)
            wc = wc_ref[:, c0:c0 + FF_CH]
            sg = jax.nn.sigmoid(gc)
            silu = gc * sg
            dv = dact * silu
            dgc = dact * v * (sg + silu * (1.0 - sg))
            dext[ck, 0:tt, :] = dgc
            de = dext[ck]
            d1 = pltpu.roll(de, ext_rows - 1, 0)[0:tt]
            d2 = pltpu.roll(de, ext_rows - 2, 0)[0:tt]
            dext[ck, tt:, :] = dgc[0:CONV_HALO, :]
            dbc_ref[:, c0:c0 + FF_CH] += _colsum(dgc)
            dwc_ref[0:1, c0:c0 + FF_CH] += _colsum(d2 * g)
            dwc_ref[1:2, c0:c0 + FF_CH] += _colsum(d1 * g)
            dwc_ref[2:3, c0:c0 + FF_CH] += _colsum(dgc * g)
            dg = wc[2:3] * dgc + wc[1:2] * d1 + wc[0:1] * d2
            dvb = dv.astype(BF16)
            dgb = dg.astype(BF16)
            dup_ref[0, :, c0:c0 + FF_CH] = dvb
            dup_ref[0, :, DFF + c0:DFF + c0 + FF_CH] = dgb
            dh2 = dh2 + _dot(dvb, wu_ref[c0:c0 + FF_CH, :]) + _dot(dgb, wu_ref[DFF + c0:DFF + c0 + FF_CH, :])
        xhat, rstd = _rms(x1_ref[0])
        gf = gf_ref[...]
        dsh_ref[0] += _colsum(dh2)
        dsc_ref[0] += _colsum(dh2 * xhat * gf)
        t = dh2 * (1.0 + sc_ref[0])
        dgf_ref[...] += _colsum(t * xhat)
        dx1_ref[0] = dx2_ref[0] + _rms_bwd(t * gf, xhat, rstd)

    def rev(b, t):
        return (b, n_t - 1 - t, 0)

    xt = pl.BlockSpec((1, tt, D), rev)
    ft = pl.BlockSpec((1, tt, DFF), rev)
    row = pl.BlockSpec((1, 1, D), lambda b, t: (b, 0, 0))
    vec = pl.BlockSpec((1, D), lambda b, t: (0, 0))
    rows = jax.ShapeDtypeStruct((bsz, 1, D), F32)
    return pl.pallas_call(
        body, name="ffn_bwd", grid=(bsz, n_t),
        in_specs=[xt, ft, ft, ft, xt, xt, row, row, vec, VMEM, VMEM, VMEM],
        out_specs=[pl.BlockSpec((1, tt, 2 * DFF), rev), xt, row, row, vec,
                   pl.BlockSpec((3, DFF), lambda b, t: (0, 0)), pl.BlockSpec((1, DFF), lambda b, t: (0, 0))],
        out_shape=[jax.ShapeDtypeStruct((bsz, seq, 2 * DFF), BF16), jax.ShapeDtypeStruct((bsz, seq, D), F32),
                   rows, rows, jax.ShapeDtypeStruct((1, D), F32), jax.ShapeDtypeStruct((3, DFF), F32),
                   jax.ShapeDtypeStruct((1, DFF), F32)],
        scratch_shapes=[pltpu.VMEM((n_ck, ext_rows, FF_CH), F32)],
        compiler_params=_params(2),
    )(ddn, gq, gcq, vq, x1, dx2, sh2, sc2, g_ffn, w_conv, w_down_b, w_up_b)


def _wgrad(a, b, bk1, bk2, name):
    n, k1 = a.shape
    _, k2 = b.shape
    tt = min(n, TT_WGRAD)

    def body(a_ref, b_ref, o_ref):
        @pl.when(pl.program_id(2) == 0)
        def _():
            o_ref[...] = jnp.zeros_like(o_ref)

        o_ref[...] += _dot_tn(a_ref[...], b_ref[...])

    return pl.pallas_call(
        body, name=name, grid=(k1 // bk1, k2 // bk2, n // tt),
        in_specs=[pl.BlockSpec((tt, bk1), lambda h, j, i: (i, h)), pl.BlockSpec((tt, bk2), lambda h, j, i: (i, j))],
        out_specs=pl.BlockSpec((bk1, bk2), lambda h, j, i: (h, j)),
        out_shape=jax.ShapeDtypeStruct((k1, k2), F32), compiler_params=_params(3),
    )(a, b)


def _mixer_out_bwd(dx1, mixcat, mixed, y2, p, gt1, w_glu_b, b_glu, w_pool_b, b_pool, pscale, w_out_b, comm=None):
    bsz, seq, _ = dx1.shape
    tt = min(seq, TT_MIX)
    n_t = seq // tt
    ext_rows = tt + POOL_HALO

    def body(dx1_ref, mc_ref, mxd_ref, y_ref, p_ref, ph_ref, gt_ref, wg_ref, bg_ref, wp_ref, bp_ref, ps_ref, wo_ref,
             dy_ref, dp_ref, dwo_ref, dwg_ref, dbg_ref, dwp_ref, dbp_ref, dps_ref, dgt_ref, ext, qext):
        b = pl.program_id(0)
        i = pl.program_id(1)
        tile = n_t - 1 - i

        @pl.when((b == 0) & (i == 0))
        def _():
            for r in (dwo_ref, dwg_ref, dbg_ref, dwp_ref, dbp_ref, dps_ref):
                r[...] = jnp.zeros_like(r)

        @pl.when(i == 0)
        def _():
            dgt_ref[...] = jnp.zeros_like(dgt_ref)
            qext[tt:, :] = jnp.zeros((POOL_HALO, POOL_W), F32)

        dx1v = dx1_ref[0]
        mc = mc_ref[0]
        dgt_ref[0] += _colsum(dx1v * mxd_ref[0].astype(F32))
        dmixed = (gt_ref[0] * dx1v).astype(BF16)
        dwo_ref[...] += _dot_tn(mc, dmixed)
        dmc = _dot_nt(dmixed, wo_ref[...])
        pv = p_ref[0]
        ext[0:POOL_HALO, :] = ph_ref[0] * (tile > 0).astype(F32)
        ext[POOL_HALO:, :] = pv
        pos = (tile * tt + lax.broadcasted_iota(jnp.int32, (tt, 1), 0) + 1).astype(F32)
        zs, zls = _pool_forward(ext[...], pv, pos, wp_ref, bp_ref)
        dzs = []
        for gi, w in enumerate(POOL_WINDOWS):
            cs = slice(gi * 128, (gi + 1) * 128)
            dyp = dmc[:, SSM_W + gi * 128:SSM_W + (gi + 1) * 128]
            dps_ref[:, cs] += _colsum(dyp * zls[gi])
            dzl = dyp * ps_ref[:, cs]
            dbp_ref[:, cs] += _colsum(dzl)
            dzlb = dzl.astype(BF16)
            dwp_ref[gi] += _dot_tn(zs[gi].astype(BF16), dzlb)
            dz = _dot_nt(dzlb, wp_ref[gi])
            dzs.append(dz)
            qext[0:tt, cs] = dz / jnp.minimum(pos, float(w))
        cur = qext[...]
        dps = []
        for gi, w in enumerate(POOL_WINDOWS):
            cur = cur + pltpu.roll(cur, ext_rows - w // 2, 0)
            dps.append(cur[0:tt, 0:128] - dzs[gi])
            if gi + 1 < len(POOL_WINDOWS):
                cur = cur[:, 128:]
        qhead = qext[0:POOL_HALO, :]
        qext[tt:, :] = qhead
        dp_ref[0] = jnp.concatenate(dps, axis=1)
        yv = y_ref[...].astype(F32)
        ab = _gelu(yv).astype(BF16)
        gl = _dot(ab, wg_ref[...]) + bg_ref[...]
        val = gl[:, :SSM_W]
        sg = jax.nn.sigmoid(gl[:, SSM_W:])
        dys = dmc[:, :SSM_W]
        dgl = jnp.concatenate([dys * sg, dys * val * sg * (1.0 - sg)], axis=1)
        dbg_ref[...] += _colsum(dgl)
        dglb = dgl.astype(BF16)
        dwg_ref[...] += _dot_tn(ab, dglb)
        dy_ref[...] = (_dot_nt(dglb, wg_ref[...]) * _gelu_grad(yv)).astype(BF16)

    def rev(b, t):
        return (b, n_t - 1 - t, 0)

    def halo(b, t):
        return (b, jnp.maximum((n_t - 1 - t) * (tt // POOL_HALO) - 1, 0), 0)

    xt = pl.BlockSpec((1, tt, D), rev)
    pt = pl.BlockSpec((1, tt, POOL_W), rev)
    yt = pl.BlockSpec((tt, SSM_W), lambda b, t: (n_t - 1 - t, b))

    def whole(shape):
        return pl.BlockSpec(shape, lambda b, t: (0,) * len(shape))

    return _fused_call(
        body, name="mixer_out_bwd", grid=(bsz, n_t),
        in_specs=[xt, xt, xt, yt, pt, pl.BlockSpec((1, POOL_HALO, POOL_W), halo),
                  pl.BlockSpec((1, 1, D), lambda b, t: (b, 0, 0)), VMEM, VMEM, VMEM, VMEM, VMEM, VMEM],
        out_specs=[yt, pt, whole((D, D)), whole((SSM_W, 2 * SSM_W)), whole((1, 2 * SSM_W)),
                   whole((4, 128, 128)), whole((1, POOL_W)), whole((1, POOL_W)),
                   pl.BlockSpec((1, 1, D), lambda b, t: (b, 0, 0))],
        out_shape=[jax.ShapeDtypeStruct(y2.shape, BF16), jax.ShapeDtypeStruct(p.shape, F32),
                   jax.ShapeDtypeStruct((D, D), F32), jax.ShapeDtypeStruct((SSM_W, 2 * SSM_W), F32),
                   jax.ShapeDtypeStruct((1, 2 * SSM_W), F32), jax.ShapeDtypeStruct((4, 128, 128), F32),
                   jax.ShapeDtypeStruct((1, POOL_W), F32), jax.ShapeDtypeStruct((1, POOL_W), F32),
                   jax.ShapeDtypeStruct((bsz, 1, D), F32)],
        scratch_shapes=[pltpu.VMEM((POOL_HALO + tt, POOL_W), F32), pltpu.VMEM((ext_rows, POOL_W), F32)],
        args=(dx1, mixcat, mixed, y2, p, p, gt1, w_glu_b, b_glu, w_pool_b, b_pool, pscale, w_out_b),
        comm=_schedule(comm, bsz * n_t))


def _ssm_bwd(dy2r, u2r, xc, xs_re, xs_im, bb, cc, lam8, d8, tlen, comm=None):
    nv = lam8.shape[1]
    rows = nv * tlen
    n_chunks = u2r.shape[0] // rows

    def body(dy_ref, u_ref, xc_ref, xre_ref, xim_ref, bb_ref, cc_ref, lam_ref, d_ref,
             du_ref, dcc_ref, dbb_ref, dlam_ref, dd_ref, s_re, s_im, g_re, g_im, gst):
        i = pl.program_id(0)

        @pl.when(i == 0)
        def _():
            for r in (gst, dcc_ref, dbb_ref, dlam_ref, dd_ref):
                r[...] = jnp.zeros_like(r)

        u = u_ref[...].astype(F32)
        dy = dy_ref[...].astype(F32)
        par0 = (lax.broadcasted_iota(jnp.int32, (rows, 1), 0) % 2) == 0
        xre = xre_ref[...]
        xim = xim_ref[...]
        s_re[0] = xc_ref[0, 0]
        s_im[0] = xc_ref[0, 1]
        s_re[pl.ds(1, tlen)] = xre.astype(F32).reshape(tlen, nv, HALF_ST)
        s_im[pl.ds(1, tlen)] = xim.astype(F32).reshape(tlen, nv, HALF_ST)
        zero = jnp.zeros_like(dy)
        dy2 = jnp.concatenate([jnp.where(par0, dy, zero), jnp.where(par0, zero, dy)], axis=1).astype(BF16)
        u2 = jnp.concatenate([jnp.where(par0, u, zero), jnp.where(par0, zero, u)], axis=1).astype(BF16)
        dcc_ref[0:HALF_ST, :] += _dot_tn(xre, dy2)
        dcc_ref[HALF_ST:, :] += _dot_tn(xim, dy2)
        for part, gref in ((0, g_re), (1, g_im)):
            for k in range(HALF_ST // 512):
                r0 = part * HALF_ST + k * 512
                gref[:, :, k * 512:(k + 1) * 512] = _dot_nt(dy2, cc_ref[r0:r0 + 512, :]).reshape(tlen, nv, 512)
        for hb in range(HALF_ST // 512):
            ls = slice(hb * 512, (hb + 1) * 512)
            lr = lam_ref[0, :, ls]
            li = lam_ref[1, :, ls]

            def bstep(k, carry, ls=ls, lr=lr, li=li):
                t = tlen - 1 - k
                gr, gi, ar, ai = carry
                ngr = g_re[t, :, ls] + lr * gr + li * gi
                ngi = g_im[t, :, ls] + lr * gi - li * gr
                g_re[t, :, ls] = ngr
                g_im[t, :, ls] = ngi
                xpr = s_re[t, :, ls]
                xpi = s_im[t, :, ls]
                return ngr, ngi, ar + ngr * xpr + ngi * xpi, ai + ngi * xpr - ngr * xpi

            init = (gst[0, :, ls], gst[1, :, ls], dlam_ref[0, :, ls], dlam_ref[1, :, ls])
            gr, gi, ar, ai = lax.fori_loop(0, tlen, bstep, init, unroll=4)
            gst[0, :, ls] = gr
            gst[1, :, ls] = gi
            dlam_ref[0, :, ls] = ar
            dlam_ref[1, :, ls] = ai
        gre = g_re[...].reshape(rows, HALF_ST).astype(BF16)
        gim = g_im[...].reshape(rows, HALF_ST).astype(BF16)
        du0 = _dot_nt(gre, bb_ref[:, 0:HALF_ST]) + _dot_nt(gim, bb_ref[:, HALF_ST:2 * HALF_ST])
        du1 = _dot_nt(gre, bb_ref[:, 2 * HALF_ST:3 * HALF_ST]) + _dot_nt(gim, bb_ref[:, 3 * HALF_ST:])
        skip = (dy.reshape(tlen, nv, HALF_CH) * d_ref[...][None]).reshape(rows, HALF_CH)
        du_ref[...] = (jnp.where(par0, du0, du1) + skip).astype(BF16)
        dbb_ref[:, 0:HALF_ST] += _dot_tn(u2, gre)
        dbb_ref[:, HALF_ST:] += _dot_tn(u2, gim)
        dd_ref[...] += jnp.sum((dy * u).reshape(tlen, nv, HALF_CH), axis=0)

        @pl.when(i == n_chunks - 1)
        def _():
            dcc_ref[HALF_ST:, :] = -dcc_ref[HALF_ST:, :]

    def rev(c):
        return (n_chunks - 1 - c, 0)

    def whole(shape):
        return pl.BlockSpec(shape, lambda c: (0,) * len(shape))

    blk = pl.BlockSpec((rows, HALF_CH), rev)
    st_blk = pl.BlockSpec((rows, HALF_ST), rev)
    return _fused_call(
        body, name="ssm_bwd", grid=(n_chunks,),
        in_specs=[blk, blk, pl.BlockSpec((1, 2, nv, HALF_ST), lambda c: (n_chunks - 1 - c, 0, 0, 0)),
                  st_blk, st_blk, VMEM, VMEM, VMEM, VMEM],
        out_specs=[blk, whole((2 * HALF_ST, SSM_W)), whole((SSM_W, 2 * HALF_ST)), whole((2, nv, HALF_ST)),
                   whole((nv, HALF_CH))],
        out_shape=[jax.ShapeDtypeStruct(u2r.shape, BF16), jax.ShapeDtypeStruct((2 * HALF_ST, SSM_W), F32),
                   jax.ShapeDtypeStruct((SSM_W, 2 * HALF_ST), F32), jax.ShapeDtypeStruct((2, nv, HALF_ST), F32),
                   jax.ShapeDtypeStruct((nv, HALF_CH), F32)],
        scratch_shapes=[pltpu.VMEM((tlen + 1, nv, HALF_ST), F32), pltpu.VMEM((tlen + 1, nv, HALF_ST), F32),
                        pltpu.VMEM((tlen, nv, HALF_ST), F32), pltpu.VMEM((tlen, nv, HALF_ST), F32),
                        pltpu.VMEM((2, nv, HALF_ST), F32)],
        args=(dy2r, u2r, xc, xs_re, xs_im, bb, cc, lam8, d8), comm=_schedule(comm, n_chunks))


def _mixer_in_bwd(du2, dp, x, dx1, sh1, sc1, g_mix, w_in_b, comm=None):
    bsz, seq, _ = x.shape
    tt = min(seq, TT_MIX)

    def body(du_ref, dp_ref, x_ref, dx1_ref, sh_ref, sc_ref, g_ref, w_ref,
             dx_ref, dw_ref, dsh_ref, dsc_ref, dg_ref):
        b = pl.program_id(0)
        ti = pl.program_id(1)

        @pl.when((b == 0) & (ti == 0))
        def _():
            dw_ref[...] = jnp.zeros_like(dw_ref)
            dg_ref[...] = jnp.zeros_like(dg_ref)

        @pl.when(ti == 0)
        def _():
            dsh_ref[...] = jnp.zeros_like(dsh_ref)
            dsc_ref[...] = jnp.zeros_like(dsc_ref)

        dz = jnp.concatenate([du_ref[...], dp_ref[0].astype(BF16)], axis=1)
        xhat, rstd = _rms(x_ref[0])
        g = g_ref[...]
        sc = sc_ref[0]
        a = xhat * g
        h = (a * (1.0 + sc) + sh_ref[0]).astype(BF16)
        dw_ref[...] += _dot_tn(h, dz)
        dh = _dot_nt(dz, w_ref[...])
        dsh_ref[0] += _colsum(dh)
        dsc_ref[0] += _colsum(dh * a)
        t = dh * (1.0 + sc)
        dg_ref[...] += _colsum(t * xhat)
        dx_ref[0] = dx1_ref[0] + _rms_bwd(t * g, xhat, rstd)

    xt = pl.BlockSpec((1, tt, D), lambda b, t: (b, t, 0))
    row = pl.BlockSpec((1, 1, D), lambda b, t: (b, 0, 0))
    vec = pl.BlockSpec((1, D), lambda b, t: (0, 0))
    rows = jax.ShapeDtypeStruct((bsz, 1, D), F32)
    return _fused_call(
        body, name="mixer_in_bwd", grid=(bsz, seq // tt),
        in_specs=[pl.BlockSpec((tt, SSM_W), lambda b, t: (t, b)),
                  pl.BlockSpec((1, tt, POOL_W), lambda b, t: (b, t, 0)), xt, xt, row, row, vec, VMEM],
        out_specs=[xt, pl.BlockSpec((D, D), lambda b, t: (0, 0)), row, row, vec],
        out_shape=[jax.ShapeDtypeStruct(x.shape, F32), jax.ShapeDtypeStruct((D, D), F32), rows, rows,
                   jax.ShapeDtypeStruct((1, D), F32)],
        scratch_shapes=[], args=(du2, dp, x, dx1, sh1, sc1, g_mix, w_in_b),
        comm=_schedule(comm, bsz * (seq // tt)))


def kernel(x, c, w_ada, b_ada, g_norm_mix, w_in, ssm_lam_re, ssm_lam_im, ssm_log_dt, ssm_b_re, ssm_b_im, ssm_c_re, ssm_c_im, ssm_d, w_glu, b_glu, w_pool, b_pool, pool_scale, w_out, g_norm_ffn, w_up, w_conv, b_conv, w_down, g_norm_final, loss_target, m_w_ada, m_b_ada, m_g_norm_mix, m_w_in, m_ssm_lam_re, m_ssm_lam_im, m_ssm_log_dt, m_ssm_b_re, m_ssm_b_im, m_ssm_c_re, m_ssm_c_im, m_ssm_d, m_w_glu, m_b_glu, m_w_pool, m_b_pool, m_pool_scale, m_w_out, m_g_norm_ffn, m_w_up, m_w_conv, m_b_conv, m_w_down, m_g_norm_final, v_w_ada, v_b_ada, v_g_norm_mix, v_w_in, v_ssm_lam_re, v_ssm_lam_im, v_ssm_log_dt, v_ssm_b_re, v_ssm_b_im, v_ssm_c_re, v_ssm_c_im, v_ssm_d, v_w_glu, v_b_glu, v_w_pool, v_b_pool, v_pool_scale, v_w_out, v_g_norm_ffn, v_w_up, v_w_conv, v_b_conv, v_w_down, v_g_norm_final):
    bsz, seq, _ = x.shape
    assert 2 * bsz == 8 and seq % 128 == 0
    px, py, pc = _my_place()
    me = 4 * px + 2 * py + pc
    place = jnp.stack([pc, 2 * px + py]).astype(jnp.int32)
    ncol = ADA_COLS

    cpad = jnp.zeros((16, D), F32).at[0:bsz].set(c).at[8:11, 0:352].set(w_conv[0])
    cg, c_all, mod8, (g_in,) = _ada_fwd(cpad, w_ada[0], b_ada.reshape(N_DEV, 1, ncol), [w_in[0].astype(BF16)])
    w_conv_f = cg[:, 8:11, 0:352].transpose(1, 0, 2).reshape(3, DFF)
    w_in_b = g_in.reshape(D, D)
    sh1, sc1, gt1, sh2, sc2, gt2 = [mod8[0:bsz, k * D:(k + 1) * D].reshape(bsz, 1, D) for k in range(N_MOD)]

    lam_r = ssm_lam_re[0].reshape(1, GRP * NST)
    lam_i = ssm_lam_im[0].reshape(1, GRP * NST)
    ldt = jnp.repeat(ssm_log_dt[0], NST).reshape(1, GRP * NST)
    b_r = ssm_b_re[0].transpose(2, 0, 1).reshape(GCH, GRP * NST)
    b_i = ssm_b_im[0].transpose(2, 0, 1).reshape(GCH, GRP * NST)
    lbr, lbi, bbr, bbi = _ssm_prep(lam_r, lam_i, ldt, b_r, b_i)
    lam8 = jnp.stack([jnp.tile(lbr.reshape(2, HALF_ST), (bsz, 1)), jnp.tile(lbi.reshape(2, HALF_ST), (bsz, 1))])
    bd_r = _blockdiag(bbr.reshape(GCH, 2, GRP // 2, NST).transpose(1, 2, 0, 3))
    bd_i = _blockdiag(bbi.reshape(GCH, 2, GRP // 2, NST).transpose(1, 2, 0, 3))
    bb = jnp.concatenate([bd_r[0], bd_i[0], bd_r[1], bd_i[1]], axis=1).astype(BF16)
    cd_r = _blockdiag(ssm_c_re[0].reshape(2, GRP // 2, GCH, NST).transpose(0, 1, 3, 2))
    cd_i = _blockdiag(ssm_c_im[0].reshape(2, GRP // 2, GCH, NST).transpose(0, 1, 3, 2))
    cc = jnp.concatenate([jnp.concatenate([cd_r[0], cd_r[1]], axis=1),
                          jnp.concatenate([-cd_i[0], -cd_i[1]], axis=1)], axis=0).astype(BF16)
    d8 = jnp.tile(ssm_d[0].reshape(2, HALF_CH), (bsz, 1))

    tlen = min(seq, T_SSM)
    (u2, p), ((g_glu, g_out),) = _mixer_in_fwd(
        x, sh1, sc1, g_norm_mix, w_in_b, comm=[(_gather_plan, [w_glu[0].astype(BF16), w_out[0].astype(BF16)], "small")])
    w_glu_b = g_glu.transpose(1, 0, 2).reshape(SSM_W, 2 * SSM_W)
    w_out_b = g_out.reshape(D, D)
    u2r = u2.reshape(seq * 2 * bsz, HALF_CH)
    (y2r, xc, xs_re, xs_im), ((g_up, g_down),) = _ssm_fwd(
        u2r, bb, cc, lam8, d8, tlen, comm=[(_gather_plan, [w_up[0].T.astype(BF16), w_down[0].astype(BF16)])])
    w_up_b = g_up.reshape(2 * DFF, D)
    w_down_b = g_down.reshape(DFF, D)
    y2 = y2r.reshape(seq, bsz * SSM_W)
    w_pool_b = w_pool[0].astype(BF16)
    bp = b_pool[0].reshape(1, POOL_W)
    x1, mixcat, mixed = _mixer_out_fwd(y2, p, x, gt1, w_glu_b, b_glu, w_pool_b, bp, pool_scale, w_out_b)
    h2, vq, gq, gcq, act, ddn, dx2, loss_l, dg_fin, dgt2 = _ffn_fwd(
        x1, loss_target, sh2, sc2, gt2, g_norm_ffn, w_up_b, w_conv_f, b_conv, w_down_b, g_norm_final.reshape(1, D))

    dup, dx1, dsh2, dsc2, dg_ffn, dw_conv, db_conv = _ffn_bwd(
        ddn, gq, gcq, vq, x1, dx2, sh2, sc2, g_norm_ffn, w_conv_f, w_down_b, w_up_b)
    ntok = bsz * seq
    dw_up_t = _wgrad(dup.reshape(ntok, 2 * DFF), h2.reshape(ntok, D), DFF // 2, D, "wgrad_up")
    dw_down = _wgrad(act.reshape(ntok, DFF), ddn.reshape(ntok, D), DFF, 512, "wgrad_down")
    g42_up = dw_up_t.reshape(4, 2, 704, D)
    g42_down = dw_down.reshape(4, 2, 352, D)
    (dy2, dp, dw_out, dw_glu, db_glu, dw_pool, db_pool, dpscale, dgt1), ((ra_up, ra_down),) = _mixer_out_bwd(
        dx1, mixcat, mixed, y2, p, gt1, w_glu_b, b_glu, w_pool_b, bp, pool_scale, w_out_b,
        comm=[(_pair_plan, [g42_up, g42_down])])
    own_up, s_up = _pair_sum(g42_up, ra_up, place, "pair_sum_up")
    own_down, s_down = _pair_sum(g42_down, ra_down, place, "pair_sum_down")
    g42_glu = dw_glu.reshape(SSM_W, N_DEV, 128).transpose(1, 0, 2).reshape(4, 2, SSM_W, 128)
    g42_out = dw_out.reshape(4, 2, 128, D)
    small_a = [
        ("b_glu", (1, 2 * SSM_W), db_glu), ("w_pool", (POOL_W, 128), dw_pool.reshape(POOL_W, 128)),
        ("b_pool", (4, 128), db_pool.reshape(4, 128)), ("pool_scale", (1, POOL_W), dpscale),
        ("g_norm_ffn", (1, D), dg_ffn), ("b_conv", (1, DFF), db_conv), ("g_norm_final", (1, D), dg_fin)]
    (du2r, dcc, dbb, dlam8, dd8), ((rc_up, rc_down), (ra_glu, ra_out), parts_a) = _ssm_bwd(
        dy2.reshape(u2r.shape), u2r, xc, xs_re, xs_im, bb, cc, lam8, d8, tlen,
        comm=[(_chip_plan, [s_up, s_down]), (_pair_plan, [g42_glu, g42_out]),
              (_gather_plan, [g for _, _, g in small_a] + [dw_conv, loss_l], "small")])
    big_up = [t.T for t in _final_sum_adamw(own_up, rc_up, w_up[0].T, m_w_up[0].T, v_w_up[0].T, "final_adamw_up")]
    big_down = _final_sum_adamw(own_down, rc_down, w_down[0], m_w_down[0], v_w_down[0], "final_adamw_down")
    own_glu, s_glu = _pair_sum(g42_glu, ra_glu, place, "pair_sum_glu")
    own_out, s_out = _pair_sum(g42_out, ra_out, place, "pair_sum_out")

    def take_c(t):
        return _blockdiag_take(t, NST, GCH).transpose(0, 2, 1)

    dc_re = jnp.concatenate([take_c(dcc[0:HALF_ST, e * HALF_CH:(e + 1) * HALF_CH]) for e in range(2)], axis=0)
    dc_im = jnp.concatenate([take_c(dcc[HALF_ST:, e * HALF_CH:(e + 1) * HALF_CH]) for e in range(2)], axis=0)

    def take_b(t):
        return _blockdiag_take(t, GCH, NST).transpose(1, 0, 2)

    dbbr = jnp.concatenate([take_b(dbb[e * HALF_CH:(e + 1) * HALF_CH, 0:HALF_ST]) for e in range(2)], axis=1)
    dbbi = jnp.concatenate([take_b(dbb[e * HALF_CH:(e + 1) * HALF_CH, HALF_ST:]) for e in range(2)], axis=1)
    glr, gli, gldt, gbr, gbi, gd = _ssm_param_bwd(
        lam_r, lam_i, ldt, b_r, b_i, dlam8, dbbr.reshape(GCH, GRP * NST), dbbi.reshape(GCH, GRP * NST), dd8)
    g_log_dt = jnp.sum(gldt.reshape(GRP, NST), axis=1)

    def view(a, shp):
        return a.reshape(shp)

    small_b = [
        ("ssm_lam_re", (GRP, NST), glr.reshape(GRP, NST)), ("ssm_lam_im", (GRP, NST), gli.reshape(GRP, NST)),
        ("ssm_log_dt", (1, GRP), g_log_dt.reshape(1, GRP)),
        ("ssm_c_re", (GRP * GCH, NST), dc_re.reshape(GRP * GCH, NST)),
        ("ssm_c_im", (GRP * GCH, NST), dc_im.reshape(GRP * GCH, NST)), ("ssm_d", (1, SSM_W), gd)]
    small = small_a + small_b
    given = dict(
        ssm_lam_re=(ssm_lam_re, m_ssm_lam_re, v_ssm_lam_re), ssm_lam_im=(ssm_lam_im, m_ssm_lam_im, v_ssm_lam_im),
        ssm_log_dt=(ssm_log_dt, m_ssm_log_dt, v_ssm_log_dt), ssm_c_re=(ssm_c_re, m_ssm_c_re, v_ssm_c_re),
        ssm_c_im=(ssm_c_im, m_ssm_c_im, v_ssm_c_im), ssm_d=(ssm_d, m_ssm_d, v_ssm_d), b_glu=(b_glu, m_b_glu, v_b_glu),
        w_pool=(w_pool, m_w_pool, v_w_pool), b_pool=(b_pool, m_b_pool, v_b_pool),
        pool_scale=(pool_scale, m_pool_scale, v_pool_scale), g_norm_ffn=(g_norm_ffn, m_g_norm_ffn, v_g_norm_ffn),
        b_conv=(b_conv, m_b_conv, v_b_conv), g_norm_final=(g_norm_final, m_g_norm_final, v_g_norm_final),
        ssm_b_re=(ssm_b_re, m_ssm_b_re, v_ssm_b_re), ssm_b_im=(ssm_b_im, m_ssm_b_im, v_ssm_b_im))
    b_view = (GRP * NST, GCH)
    (grad_x, dw_in, dsh1, dsc1, dg_mix), ((rc_glu, rc_out), parts_b) = _mixer_in_bwd(
        du2r.reshape(u2.shape), dp, x, dx1, sh1, sc1, g_norm_mix, w_in_b,
        comm=[(_chip_plan, [s_glu, s_out]), (_gather_plan, [g for _, _, g in small_b] + [gbr, gbi], "small")])
    big_glu = _final_sum_adamw(own_glu, rc_glu, w_glu[0], m_w_glu[0], v_w_glu[0], "final_adamw_glu")
    big_out = _final_sum_adamw(own_out, rc_out, w_out[0], m_w_out[0], v_w_out[0], "final_adamw_out")
    parts = list(parts_a[:-2]) + list(parts_b[:-2])
    items = [(pt,) + tuple(view(a, shp) for a in given[nm]) for pt, (nm, shp, _) in zip(parts, small)]
    small_out, (g_conv_full, loss_all, gbr_all, gbi_all) = _small_sum_adamw(
        items, [parts_a[-2], parts_a[-1], parts_b[-2], parts_b[-1]])
    loss = loss_all[0, 0]
    result = {nm: [t.reshape(given[nm][0].shape) for t in quad] for quad, (nm, _, _) in zip(small_out, small)}
    for nm, g_all in (("ssm_b_re", gbr_all), ("ssm_b_im", gbi_all)):
        quad = [g_all.T] + list(_adamw_plain(g_all.T, *[view(a, b_view) for a in given[nm]]))
        result[nm] = [t.reshape(given[nm][0].shape) for t in quad]
    g_w_conv = lax.dynamic_slice_in_dim(g_conv_full, 352 * me, 352, axis=1)
    result["w_conv"] = [g_w_conv[None]] + [t[None] for t in _adamw_plain(g_w_conv, w_conv[0], m_w_conv[0], v_w_conv[0])]

    for nm, quad in (("w_glu", big_glu), ("w_out", big_out), ("w_up", big_up), ("w_down", big_down)):
        result[nm] = [t[None] for t in quad]

    dmod = jnp.concatenate([t.reshape(bsz, D) for t in (dsh1, dsc1, dgt1, dsh2, dsc2, dgt2)], axis=1)
    dmod_blk = jnp.zeros((N_DEV, 8, ncol), F32).at[:, 0:bsz].set(dmod.reshape(bsz, N_DEV, ncol).transpose(1, 0, 2))
    dmod_blk = dmod_blk.at[0, ADA_RIDER_ROW].set(dg_mix[0, 0:ncol]).at[1, ADA_RIDER_ROW, 0:D - ncol].set(dg_mix[0, ncol:])
    ada = _ada_bwd(dmod_blk.reshape(ADA_ROWS, ncol), c_all, w_ada[0], m_w_ada[0], v_w_ada[0],
                   b_ada, m_b_ada, v_b_ada, g_norm_mix, m_g_norm_mix, v_g_norm_mix,
                   dw_in.reshape(4, 2, 128, D), w_in[0], m_w_in[0], v_w_in[0])
    result["w_ada"] = [t[None] for t in ada[0:4]]
    result["b_ada"] = list(ada[4:8])
    result["g_norm_mix"] = list(ada[8:12])
    result["w_in"] = [t[None] for t in ada[12:16]]

    names = ["w_ada", "b_ada", "g_norm_mix", "w_in", "ssm_lam_re", "ssm_lam_im", "ssm_log_dt", "ssm_b_re", "ssm_b_im",
             "ssm_c_re", "ssm_c_im", "ssm_d", "w_glu", "b_glu", "w_pool", "b_pool", "pool_scale", "w_out", "g_norm_ffn",
             "w_up", "w_conv", "b_conv", "w_down", "g_norm_final"]
    return (loss, grad_x, *[result[nm][k] for k in range(4) for nm in names])
```

```python
import functools
import math

import jax
import jax.numpy as jnp
from jax import lax
from jax.experimental import pallas as pl
from jax.experimental.pallas import tpu as pltpu

F32 = jnp.float32
BF16 = jnp.bfloat16

D = 1024
SSM_W = 512
POOL_W = 512
GRP = 32
GCH = 16
NST = 64
HALF_ST = GRP * NST // 2
HALF_CH = SSM_W // 2
DFF = 2816
FF_CH = 2816
N_MOD = 6
N_DEV = 8
EPS = 1e-6
POOL_WINDOWS = (2, 4, 8, 16)
POOL_HALO = 16
CONV_HALO = 8
GELU_C = math.sqrt(2.0 / math.pi)
GELU_A = 0.044715

ADAM_LR = 0.001
ADAM_B1 = 0.9
ADAM_B2 = 0.999
ADAM_EPS = 1e-08
ADAM_WD = 0.01
ADAM_STEP = 10

VMEM_LIMIT = 56 * 1024 * 1024
TT_MIX = 1024
TT_FFN = 256
T_SSM = 128
TT_WGRAD = 2048
MESH = pl.DeviceIdType.MESH
NT = (((1,), (1,)), ((), ()))
TN = (((0,), (0,)), ((), ()))
ANY = pl.BlockSpec(memory_space=pl.ANY)
VMEM = pl.BlockSpec(memory_space=pltpu.VMEM)


def _params(n_grid, vmem=VMEM_LIMIT):
    return pltpu.CompilerParams(dimension_semantics=("arbitrary",) * n_grid, vmem_limit_bytes=vmem)


def _dot(a, b):
    return jnp.dot(a, b, preferred_element_type=F32)


def _dot_nt(a, b):
    return lax.dot_general(a, b, NT, preferred_element_type=F32)


def _dot_tn(a, b):
    return lax.dot_general(a, b, TN, preferred_element_type=F32)


def _colsum(a):
    return jnp.sum(a, axis=0, keepdims=True)


def _rms(x):
    rstd = lax.rsqrt(jnp.mean(x * x, axis=-1, keepdims=True) + EPS)
    return x * rstd, rstd


def _rms_bwd(dxhat, xhat, rstd):
    return rstd * (dxhat - xhat * jnp.mean(dxhat * xhat, axis=-1, keepdims=True))


def _gelu(x):
    return 0.5 * x * (1.0 + jnp.tanh(GELU_C * (x + GELU_A * x * x * x)))


def _gelu_grad(x):
    x2 = x * x
    th = jnp.tanh(GELU_C * (x + GELU_A * x * x2))
    return 0.5 * (1.0 + th) + 0.5 * x * (1.0 - th * th) * GELU_C * (1.0 + 3.0 * GELU_A * x2)


def _adamw(w, g, m, v):
    m = ADAM_B1 * m + (1.0 - ADAM_B1) * g
    v = ADAM_B2 * v + (1.0 - ADAM_B2) * (g * g)
    m_hat = m / (1.0 - ADAM_B1 ** ADAM_STEP)
    v_hat = v / (1.0 - ADAM_B2 ** ADAM_STEP)
    delta = -ADAM_LR * (m_hat / (jnp.sqrt(v_hat) + ADAM_EPS) + ADAM_WD * w)
    return delta, m, v


def _my_place():
    return lax.axis_index("x"), lax.axis_index("y"), lax.axis_index("c")


def _gather_plan(shards):
    n = len(shards)
    out_shape = [jax.ShapeDtypeStruct((N_DEV,) + tuple(s.shape), s.dtype) for s in shards]
    scratch = [pltpu.SemaphoreType.DMA((n, 7)), pltpu.SemaphoreType.DMA((n, 7)), pltpu.SemaphoreType.DMA((n,))]

    def stages(x_refs, out_refs, sems):
        send_sems, recv_sems, local_sems = sems
        x, y, c = _my_place()
        me, sibling = (x, y, c), (x, y, 1 - c)
        chips = [(1 - x, y), (x, 1 - y), (1 - x, 1 - y)]

        def copy(i, k, block, to, own=False):
            px, py, pc = block
            dst = out_refs[i].at[4 * px + 2 * py + pc]
            return pltpu.make_async_remote_copy(
                src_ref=x_refs[i] if own else dst, dst_ref=dst, send_sem=send_sems.at[i, k],
                recv_sem=recv_sems.at[i, k], device_id=to, device_id_type=MESH)

        def mine(i):
            return pltpu.make_async_copy(x_refs[i], out_refs[i].at[4 * x + 2 * y + c], local_sems.at[i])

        def start():
            for i in range(n):
                mine(i).start()
                copy(i, 0, me, sibling, own=True).start()
                for j, chip in enumerate(chips):
                    copy(i, 1 + j, me, (*chip, c), own=True).start()

        def forward():
            for i in range(n):
                for j, chip in enumerate(chips):
                    copy(i, 1 + j, (*chip, c), me).wait_recv()
                    copy(i, 4 + j, (*chip, c), sibling).start()

        def finish():
            for i in range(n):
                copy(i, 0, sibling, me).wait_recv()
                copy(i, 0, me, sibling, own=True).wait_send()
                for j, chip in enumerate(chips):
                    copy(i, 4 + j, (*chip, 1 - c), me).wait_recv()
                    copy(i, 1 + j, me, (*chip, c), own=True).wait_send()
                    copy(i, 4 + j, (*chip, c), sibling).wait_send()
                mine(i).wait()

        return [start, forward, finish]

    return n, out_shape, scratch, stages


def _pair_plan(g42s):
    n = len(g42s)
    out_shape = [jax.ShapeDtypeStruct((4,) + tuple(g.shape[2:]), g.dtype) for g in g42s]
    scratch = [pltpu.SemaphoreType.DMA((n,)), pltpu.SemaphoreType.DMA((n,))]

    def stages(g_refs, out_refs, sems):
        send_sems, recv_sems = sems
        x, y, c = _my_place()

        def copy(i):
            return pltpu.make_async_remote_copy(
                src_ref=g_refs[i].at[:, 1 - c], dst_ref=out_refs[i], send_sem=send_sems.at[i],
                recv_sem=recv_sems.at[i], device_id=(x, y, 1 - c), device_id_type=MESH)

        def start():
            for i in range(n):
                copy(i).start()

        def finish():
            for i in range(n):
                copy(i).wait()

        return [start, finish]

    return n, out_shape, scratch, stages


def _chip_plan(s4s):
    n = len(s4s)
    out_shape = [jax.ShapeDtypeStruct((3,) + tuple(s.shape[1:]), s.dtype) for s in s4s]
    scratch = [pltpu.SemaphoreType.DMA((n, 3)), pltpu.SemaphoreType.DMA((n, 3))]

    def stages(s_refs, out_refs, sems):
        send_sems, recv_sems = sems
        x, y, c = _my_place()

        def copy(i, d):
            px, py = x ^ (d >> 1), y ^ (d & 1)
            return pltpu.make_async_remote_copy(
                src_ref=s_refs[i].at[2 * px + py], dst_ref=out_refs[i].at[d - 1], send_sem=send_sems.at[i, d - 1],
                recv_sem=recv_sems.at[i, d - 1], device_id=(px, py, c), device_id_type=MESH)

        def start():
            for i in range(n):
                for d in (1, 2, 3):
                    copy(i, d).start()

        def finish():
            for i in range(n):
                for d in (1, 2, 3):
                    copy(i, d).wait()

        return [start, finish]

    return n, out_shape, scratch, stages


def _comm_call(plan, arrays, name):
    n, out_shape, scratch, stages = plan

    def body(*refs):
        for stage in stages(refs[:n], refs[n:2 * n], refs[2 * n:]):
            stage()

    return pl.pallas_call(
        body, name=name, out_shape=out_shape, in_specs=[ANY] * n, out_specs=[ANY] * n, scratch_shapes=scratch,
    )(*arrays)


def _fused_call(body, *, name, grid, in_specs, out_specs, out_shape, scratch_shapes, args, comm=None):
    if not comm:
        out = pl.pallas_call(body, name=name, grid=grid, in_specs=in_specs, out_specs=out_specs, out_shape=out_shape,
                             scratch_shapes=scratch_shapes, compiler_params=_params(len(grid)))(*args)
        return out, []
    counts = [plan[0] for plan, _, _ in comm]
    n = sum(counts)
    n_in, n_out, n_scr = len(in_specs), len(out_specs), len(scratch_shapes)

    def fused(*refs):
        ins, refs = refs[:n_in], refs[n_in:]
        c_ins, refs = refs[:n], refs[n:]
        outs, refs = refs[:n_out], refs[n_out:]
        c_outs, refs = refs[:n], refs[n:]
        scr, c_scr = refs[:n_scr], refs[n_scr:]
        step = pl.program_id(0)
        for k in range(1, len(grid)):
            step = step * grid[k] + pl.program_id(k)
        todo, a0, s0 = [], 0, 0
        for (cnt, _, plan_scratch, stages), _, steps in comm:
            sems = c_scr[s0:s0 + len(plan_scratch)]
            todo += list(zip(stages(c_ins[a0:a0 + cnt], c_outs[a0:a0 + cnt], sems), steps))
            a0 += cnt
            s0 += len(plan_scratch)
        for stage, at in todo:
            if at == 0:
                pl.when(step == 0)(stage)
        body(*ins, *outs, *scr)
        for stage, at in todo:
            if at != 0:
                pl.when(step == at)(stage)

    c_shape = [s for plan, _, _ in comm for s in plan[1]]
    c_scratch = [s for plan, _, _ in comm for s in plan[2]]
    arrays = [a for _, arrs, _ in comm for a in arrs]
    out = pl.pallas_call(
        fused, name=name, grid=grid, in_specs=list(in_specs) + [ANY] * n, out_specs=list(out_specs) + [ANY] * n,
        out_shape=list(out_shape) + c_shape, scratch_shapes=list(scratch_shapes) + c_scratch,
        compiler_params=_params(len(grid)))(*args, *arrays)
    outs, c_outs, split, a0 = out[:n_out], out[n_out:], [], 0
    for cnt in counts:
        split.append(c_outs[a0:a0 + cnt])
        a0 += cnt
    return outs, split


def _schedule(comm, n_steps):
    out = []
    for make_plan, arrays, *small in comm or []:
        middle = n_steps - 1 if small else (3 * n_steps) // 4
        steps = [0, middle, n_steps - 1] if make_plan is _gather_plan else [0, n_steps - 1]
        out.append((make_plan(arrays), arrays, steps))
    return out


def _row_tile(r):
    for t in (128, 64, 32, 16, 8):
        if r % t == 0:
            return t
    return r


def _pair_sum(g42, recv, place, name):
    _, _, r, cdim = g42.shape
    tr = _row_tile(r)

    def body(pl_ref, g_ref, r_ref, own_ref, s_ref):
        s_ref[...] = (g_ref[:, 0] + r_ref[...]).astype(BF16)
        q = pl_ref[1]
        own_ref[...] = g_ref[q, 0] + r_ref[q]

    return pl.pallas_call(
        body, name=name,
        grid_spec=pltpu.PrefetchScalarGridSpec(
            num_scalar_prefetch=1, grid=(r // tr,),
            in_specs=[pl.BlockSpec((4, 1, tr, cdim), lambda i, p: (0, p[0], i, 0)),
                      pl.BlockSpec((4, tr, cdim), lambda i, p: (0, i, 0))],
            out_specs=[pl.BlockSpec((tr, cdim), lambda i, p: (i, 0)),
                       pl.BlockSpec((4, tr, cdim), lambda i, p: (0, i, 0))]),
        out_shape=[jax.ShapeDtypeStruct((r, cdim), F32), jax.ShapeDtypeStruct((4, r, cdim), BF16)],
        compiler_params=_params(1),
    )(place, g42, recv)


def _final_sum_adamw(own, recv3, w, m, v, name):
    r, cdim = w.shape
    tr = _row_tile(r)

    def body(s_ref, r_ref, w_ref, m_ref, v_ref, g_out, d_out, m_out, v_out):
        g = s_ref[...] + r_ref[0].astype(F32) + r_ref[1].astype(F32) + r_ref[2].astype(F32)
        d, mn, vn = _adamw(w_ref[...], g, m_ref[...], v_ref[...])
        g_out[...] = g
        d_out[...] = d
        m_out[...] = mn
        v_out[...] = vn

    blk = pl.BlockSpec((tr, cdim), lambda i: (i, 0))
    shp = jax.ShapeDtypeStruct((r, cdim), F32)
    return pl.pallas_call(
        body, name=name, grid=(r // tr,),
        in_specs=[blk, pl.BlockSpec((3, tr, cdim), lambda i: (0, i, 0)), blk, blk, blk],
        out_specs=[blk, blk, blk, blk], out_shape=[shp, shp, shp, shp], compiler_params=_params(1),
    )(own, recv3, w, m, v)


def _small_sum_adamw(items, sums_only):
    n, ne = len(items), len(sums_only)

    def total(p_ref):
        g = p_ref[0]
        for k in range(1, N_DEV):
            g = g + p_ref[k]
        return g

    def body(*refs):
        ins, outs = refs[:4 * n + ne], refs[4 * n + ne:]
        for i in range(n):
            p_ref, w_ref, m_ref, v_ref = ins[4 * i:4 * i + 4]
            g = total(p_ref)
            d, mn, vn = _adamw(w_ref[...], g, m_ref[...], v_ref[...])
            for o_ref, val in zip(outs[4 * i:4 * i + 4], (g, d, mn, vn)):
                o_ref[...] = val
        for j in range(ne):
            outs[4 * n + j][...] = total(ins[4 * n + j])

    args = [a for item in items for a in item] + list(sums_only)
    shapes = [jax.ShapeDtypeStruct(w.shape, F32) for _, w, _, _ in items for _ in range(4)]
    shapes += [jax.ShapeDtypeStruct(p.shape[1:], F32) for p in sums_only]
    out = pl.pallas_call(
        body, name="small_sum_adamw", in_specs=[VMEM] * len(args), out_specs=[VMEM] * len(shapes), out_shape=shapes,
        compiler_params=_params(0),
    )(*args)
    return [out[4 * i:4 * i + 4] for i in range(n)], out[4 * n:]


def _adamw_plain(g, w, m, v):
    r, cdim = w.shape
    tr = r if r * cdim <= 64 * 1024 else _row_tile(r)

    def body(g_ref, w_ref, m_ref, v_ref, d_out, m_out, v_out):
        d, mn, vn = _adamw(w_ref[...], g_ref[...], m_ref[...], v_ref[...])
        d_out[...] = d
        m_out[...] = mn
        v_out[...] = vn

    blk = pl.BlockSpec((tr, cdim), lambda i: (i, 0))
    shp = jax.ShapeDtypeStruct((r, cdim), F32)
    return pl.pallas_call(
        body, name="adamw_plain", grid=(r // tr,), in_specs=[blk, blk, blk, blk],
        out_specs=[blk, blk, blk], out_shape=[shp, shp, shp], compiler_params=_params(1),
    )(g, w, m, v)


ADA_COLS = N_MOD * D // N_DEV
ADA_ROWS = 8 * N_DEV


def _ada_fwd(cpad, w_ada, b_blocks, mixer_shards):
    n_w, w_shape, w_scr, w_stages = _gather_plan(mixer_shards)
    _, _, c_scr, c_stages = _gather_plan([cpad])
    _, _, p_scr, p_stages = _gather_plan([jax.ShapeDtypeStruct((ADA_ROWS, ADA_COLS), F32)])

    def body(c_ref, wa_ref, b_ref, *refs):
        w_refs, refs = refs[:n_w], refs[n_w:]
        cg_ref, call_ref, mod_ref = refs[:3]
        wg_refs, refs = refs[3:3 + n_w], refs[3 + n_w:]
        part_ref, pg_ref = refs[:2]
        c_sems, p_sems, w_sems = refs[2:5], refs[5:8], refs[8:11]
        w_start, w_forward, w_finish = w_stages(w_refs, wg_refs, w_sems)
        w_start()
        for stage in c_stages([c_ref], [cg_ref], c_sems):
            stage()
        cv = cg_ref[:, 0:8, :].reshape(ADA_ROWS, D)
        call_ref[...] = cv
        part_ref[...] = _dot(cv * jax.nn.sigmoid(cv), wa_ref[...])
        for stage in p_stages([part_ref], [pg_ref], p_sems):
            stage()
        x, y, c = _my_place()
        r0 = pl.multiple_of(8 * (4 * x + 2 * y + c), 8)
        for k in range(N_DEV):
            mod_ref[:, k * ADA_COLS:(k + 1) * ADA_COLS] = pg_ref[k, pl.ds(r0, 8), :] + b_ref[k]
        w_forward()
        w_finish()

    out = pl.pallas_call(
        body, name="ada_fwd", in_specs=[VMEM, VMEM, VMEM] + [ANY] * n_w,
        out_specs=[VMEM, VMEM, VMEM] + [ANY] * n_w,
        out_shape=[jax.ShapeDtypeStruct((N_DEV,) + cpad.shape, F32), jax.ShapeDtypeStruct((ADA_ROWS, D), F32),
                   jax.ShapeDtypeStruct((8, N_MOD * D), F32)] + list(w_shape),
        scratch_shapes=[pltpu.VMEM((ADA_ROWS, ADA_COLS), F32), pltpu.VMEM((N_DEV, ADA_ROWS, ADA_COLS), F32)]
        + list(c_scr) + list(p_scr) + list(w_scr),
        compiler_params=_params(0),
    )(cpad, w_ada, b_blocks, *mixer_shards)
    return out[0], out[1], out[2], out[3:]


ADA_RIDER_ROW = 4


def _ada_bwd(dmod_blk, c_all, w_ada, m_w, v_w, b_blocks, m_b, v_b, g_w, g_m, g_v, g42, w_l, m_l, v_l):
    _, _, g_scr, g_stages = _gather_plan([dmod_blk])
    rest = D - ADA_COLS
    blk = tuple(g42.shape[2:])

    def body(dm_ref, c_ref, w_ref, mw_ref, vw_ref, b_ref, mb_ref, vb_ref, gw_ref, gm_ref, gv_ref,
             g42_ref, wl_ref, ml_ref, vl_ref,
             gw_o, dw_o, mw_o, vw_o, gb_o, dbb_o, mb_o, vb_o, gg_o, dgg_o, mg_o, vg_o, gl_o, dl_o, ml_o, vl_o,
             dg_ref, pr_ref, sbf_ref, rc_ref, pair_send, pair_recv, chip_send, chip_recv, *sems):
        x, y, c = _my_place()
        q = 2 * x + y
        pair = pltpu.make_async_remote_copy(
            src_ref=g42_ref.at[:, 1 - c], dst_ref=pr_ref, send_sem=pair_send, recv_sem=pair_recv,
            device_id=(x, y, 1 - c), device_id_type=MESH)
        pair.start()
        for stage in g_stages([dm_ref], [dg_ref], sems):
            stage()
        pair.wait()
        for k in range(4):
            sbf_ref[k] = (g42_ref[k, c] + pr_ref[k]).astype(BF16)

        def chip_copy(dist):
            px, py = x ^ (dist >> 1), y ^ (dist & 1)
            return pltpu.make_async_remote_copy(
                src_ref=sbf_ref.at[2 * px + py], dst_ref=rc_ref.at[dist - 1], send_sem=chip_send.at[dist - 1],
                recv_sem=chip_recv.at[dist - 1], device_id=(px, py, c), device_id_type=MESH)

        for dist in (1, 2, 3):
            chip_copy(dist).start()
        r0 = pl.multiple_of(8 * (4 * x + 2 * y + c), 8)
        cols = dg_ref[:, pl.ds(r0, 8), :].reshape(ADA_ROWS, ADA_COLS)
        cv = c_ref[...]
        gw = _dot_tn(cv * jax.nn.sigmoid(cv), cols)
        d, mn, vn = _adamw(w_ref[...], gw, mw_ref[...], vw_ref[...])
        gw_o[...] = gw
        dw_o[...] = d
        mw_o[...] = mn
        vw_o[...] = vn
        is_example = lax.broadcasted_iota(jnp.int32, (8, 1), 0) < ADA_RIDER_ROW
        blocks = []
        for k in range(N_DEV):
            s = dg_ref[0, 8 * k:8 * k + 8, :]
            for dev in range(1, N_DEV):
                s = s + dg_ref[dev, 8 * k:8 * k + 8, :]
            blocks.append(s)
            gb = _colsum(jnp.where(is_example, s, 0.0))
            cs = slice(k * ADA_COLS, (k + 1) * ADA_COLS)
            d, mn, vn = _adamw(b_ref[:, cs], gb, mb_ref[:, cs], vb_ref[:, cs])
            gb_o[:, cs] = gb
            dbb_o[:, cs] = d
            mb_o[:, cs] = mn
            vb_o[:, cs] = vn
        rider = jnp.concatenate([blocks[0][ADA_RIDER_ROW:ADA_RIDER_ROW + 1, :],
                                 blocks[1][ADA_RIDER_ROW:ADA_RIDER_ROW + 1, 0:rest]], axis=1)
        d, mn, vn = _adamw(gw_ref[...], rider, gm_ref[...], gv_ref[...])
        gg_o[...] = rider
        dgg_o[...] = d
        mg_o[...] = mn
        vg_o[...] = vn
        for dist in (1, 2, 3):
            chip_copy(dist).wait()
        gl = g42_ref[q, c] + pr_ref[q] + rc_ref[0].astype(F32) + rc_ref[1].astype(F32) + rc_ref[2].astype(F32)
        d, mn, vn = _adamw(wl_ref[...], gl, ml_ref[...], vl_ref[...])
        gl_o[...] = gl
        dl_o[...] = d
        ml_o[...] = mn
        vl_o[...] = vn

    ws = jax.ShapeDtypeStruct(w_ada.shape, F32)
    bs = jax.ShapeDtypeStruct(b_blocks.shape, F32)
    gs = jax.ShapeDtypeStruct(g_w.shape, F32)
    ls = jax.ShapeDtypeStruct(w_l.shape, F32)
    return pl.pallas_call(
        body, name="ada_bwd", in_specs=[VMEM] * 15, out_specs=[VMEM] * 16,
        out_shape=[ws, ws, ws, ws, bs, bs, bs, bs, gs, gs, gs, gs, ls, ls, ls, ls],
        scratch_shapes=[pltpu.VMEM((N_DEV, ADA_ROWS, ADA_COLS), F32), pltpu.VMEM((4,) + blk, F32),
                        pltpu.VMEM((4,) + blk, BF16), pltpu.VMEM((3,) + blk, BF16),
                        pltpu.SemaphoreType.DMA(()), pltpu.SemaphoreType.DMA(()),
                        pltpu.SemaphoreType.DMA((3,)), pltpu.SemaphoreType.DMA((3,))] + list(g_scr),
        compiler_params=_params(0),
    )(dmod_blk, c_all, w_ada, m_w, v_w, b_blocks, m_b, v_b, g_w, g_m, g_v, g42, w_l, m_l, v_l)


def _ssm_param_fn(lr, li, ldt, br, bi):
    dt = jnp.exp(ldt)
    mag = jnp.exp(lr * dt)
    ang = li * dt
    lbr = mag * jnp.cos(ang)
    lbi = mag * jnp.sin(ang)
    nr = lbr - 1.0
    den = lr * lr + li * li
    cr = (nr * lr + lbi * li) / den
    ci = (lbi * lr - nr * li) / den
    return lbr, lbi, cr * br - ci * bi, cr * bi + ci * br


def _ssm_prep(lr, li, ldt, br, bi):
    def body(lr_ref, li_ref, ldt_ref, br_ref, bi_ref, lbr_o, lbi_o, bbr_o, bbi_o):
        lbr, lbi, bbr, bbi = _ssm_param_fn(lr_ref[...], li_ref[...], ldt_ref[...], br_ref[...], bi_ref[...])
        lbr_o[...] = lbr
        lbi_o[...] = lbi
        bbr_o[...] = bbr
        bbi_o[...] = bbi

    row = jax.ShapeDtypeStruct(lr.shape, F32)
    mat = jax.ShapeDtypeStruct(br.shape, F32)
    return pl.pallas_call(
        body, name="ssm_prep", in_specs=[VMEM] * 5, out_specs=[VMEM] * 4,
        out_shape=[row, row, mat, mat], compiler_params=_params(0),
    )(lr, li, ldt, br, bi)


def _ssm_param_bwd(lr, li, ldt, br, bi, dlam8, dbbr, dbbi, dd8):
    nv = dlam8.shape[1]

    def body(lr_ref, li_ref, ldt_ref, br_ref, bi_ref, dl_ref, dbr_ref, dbi_ref, dd_ref,
             glr_o, gli_o, gldt_o, gbr_o, gbi_o, gd_o):
        halves_r, halves_i, halves_d = [], [], []
        for e in range(2):
            ar = dl_ref[0, e:e + 1, :]
            ai = dl_ref[1, e:e + 1, :]
            ad = dd_ref[e:e + 1, :]
            for b in range(1, nv // 2):
                ar = ar + dl_ref[0, 2 * b + e:2 * b + e + 1, :]
                ai = ai + dl_ref[1, 2 * b + e:2 * b + e + 1, :]
                ad = ad + dd_ref[2 * b + e:2 * b + e + 1, :]
            halves_r.append(ar)
            halves_i.append(ai)
            halves_d.append(ad)
        dlbr = jnp.concatenate(halves_r, axis=1)
        dlbi = jnp.concatenate(halves_i, axis=1)
        gd_o[...] = jnp.concatenate(halves_d, axis=1)
        _, vjp = jax.vjp(_ssm_param_fn, lr_ref[...], li_ref[...], ldt_ref[...], br_ref[...], bi_ref[...])
        glr, gli, gldt, gbr, gbi = vjp((dlbr, dlbi, dbr_ref[...], dbi_ref[...]))
        glr_o[...] = glr
        gli_o[...] = gli
        gldt_o[...] = gldt
        gbr_o[...] = gbr
        gbi_o[...] = gbi

    row = jax.ShapeDtypeStruct(lr.shape, F32)
    mat = jax.ShapeDtypeStruct(br.shape, F32)
    return pl.pallas_call(
        body, name="ssm_param_bwd", in_specs=[VMEM] * 9, out_specs=[VMEM] * 6,
        out_shape=[row, row, row, mat, mat, jax.ShapeDtypeStruct((1, SSM_W), F32)],
        compiler_params=_params(0),
    )(lr, li, ldt, br, bi, dlam8, dbbr, dbbi, dd8)


def _blockdiag(m):
    _, g, a, b = m.shape
    eye = jnp.eye(g, dtype=m.dtype)
    return jnp.einsum("egab,gk->egakb", m, eye).reshape(2, g * a, g * b)


def _blockdiag_take(t, a, b):
    return jnp.einsum("gagb->gab", t.reshape(GRP // 2, a, GRP // 2, b))


def _mixer_in_fwd(x, sh1, sc1, g_mix, w_in_b, comm=None):
    bsz, seq, _ = x.shape
    tt = min(seq, TT_MIX)

    def body(x_ref, sh_ref, sc_ref, g_ref, w_ref, u_ref, p_ref):
        xhat, _ = _rms(x_ref[0])
        h = xhat * g_ref[...] * (1.0 + sc_ref[0]) + sh_ref[0]
        z = _dot(h.astype(BF16), w_ref[...])
        u_ref[...] = z[:, :SSM_W].astype(BF16)
        p_ref[0] = z[:, SSM_W:]

    row = pl.BlockSpec((1, 1, D), lambda b, t: (b, 0, 0))
    return _fused_call(
        body, name="mixer_in_fwd", grid=(bsz, seq // tt),
        in_specs=[pl.BlockSpec((1, tt, D), lambda b, t: (b, t, 0)), row, row,
                  pl.BlockSpec((1, D), lambda b, t: (0, 0)), VMEM],
        out_specs=[pl.BlockSpec((tt, SSM_W), lambda b, t: (t, b)),
                   pl.BlockSpec((1, tt, POOL_W), lambda b, t: (b, t, 0))],
        out_shape=[jax.ShapeDtypeStruct((seq, bsz * SSM_W), BF16), jax.ShapeDtypeStruct((bsz, seq, POOL_W), F32)],
        scratch_shapes=[], args=(x, sh1, sc1, g_mix, w_in_b), comm=_schedule(comm, bsz * (seq // tt)))


def _ssm_project_in(ub, par0, bb_ref, s_re, s_im, row0, tlen, nv):
    for part, sref in ((0, s_re), (1, s_im)):
        for k in range(HALF_ST // 512):
            c0 = part * HALF_ST + k * 512
            a0 = _dot(ub, bb_ref[:, c0:c0 + 512])
            a1 = _dot(ub, bb_ref[:, 2 * HALF_ST + c0:2 * HALF_ST + c0 + 512])
            sref[pl.ds(row0, tlen), :, k * 512:(k + 1) * 512] = jnp.where(par0, a0, a1).reshape(tlen, nv, 512)


def _ssm_fwd(u2r, bb, cc, lam8, d8, tlen, comm=None):
    nv = lam8.shape[1]
    rows = nv * tlen
    n_chunks = u2r.shape[0] // rows

    def body(u_ref, bb_ref, cc_ref, lam_ref, d_ref, y_ref, xc_ref, xre_ref, xim_ref, s_re, s_im, st):
        @pl.when(pl.program_id(0) == 0)
        def _():
            st[...] = jnp.zeros_like(st)

        xc_ref[0] = st[...]
        ub = u_ref[...]
        u = ub.astype(F32)
        par0 = (lax.broadcasted_iota(jnp.int32, (rows, 1), 0) % 2) == 0
        _ssm_project_in(ub, par0, bb_ref, s_re, s_im, 0, tlen, nv)
        for hb in range(HALF_ST // 512):
            ls = slice(hb * 512, (hb + 1) * 512)
            lr = lam_ref[0, :, ls]
            li = lam_ref[1, :, ls]

            def step(t, carry, ls=ls, lr=lr, li=li):
                xr, xi = carry
                nr = lr * xr - li * xi + s_re[t, :, ls]
                ni = lr * xi + li * xr + s_im[t, :, ls]
                s_re[t, :, ls] = nr
                s_im[t, :, ls] = ni
                return nr, ni

            xr, xi = lax.fori_loop(0, tlen, step, (st[0, :, ls], st[1, :, ls]), unroll=8)
            st[0, :, ls] = xr
            st[1, :, ls] = xi
        xre = s_re[...].reshape(rows, HALF_ST).astype(BF16)
        xim = s_im[...].reshape(rows, HALF_ST).astype(BF16)
        xre_ref[...] = xre
        xim_ref[...] = xim
        y2 = _dot(xre, cc_ref[0:HALF_ST, :]) + _dot(xim, cc_ref[HALF_ST:, :])
        y = jnp.where(par0, y2[:, :HALF_CH], y2[:, HALF_CH:])
        skip = (u.reshape(tlen, nv, HALF_CH) * d_ref[...][None]).reshape(rows, HALF_CH)
        y_ref[...] = (y + skip).astype(BF16)

    st_blk = pl.BlockSpec((rows, HALF_ST), lambda c: (c, 0))
    st_shape = jax.ShapeDtypeStruct((u2r.shape[0], HALF_ST), BF16)
    return _fused_call(
        body, name="ssm_fwd", grid=(n_chunks,),
        in_specs=[pl.BlockSpec((rows, HALF_CH), lambda c: (c, 0)), VMEM, VMEM, VMEM, VMEM],
        out_specs=[pl.BlockSpec((rows, HALF_CH), lambda c: (c, 0)),
                   pl.BlockSpec((1, 2, nv, HALF_ST), lambda c: (c, 0, 0, 0)), st_blk, st_blk],
        out_shape=[jax.ShapeDtypeStruct(u2r.shape, BF16), jax.ShapeDtypeStruct((n_chunks, 2, nv, HALF_ST), F32),
                   st_shape, st_shape],
        scratch_shapes=[pltpu.VMEM((tlen, nv, HALF_ST), F32), pltpu.VMEM((tlen, nv, HALF_ST), F32),
                        pltpu.VMEM((2, nv, HALF_ST), F32)],
        args=(u2r, bb, cc, lam8, d8), comm=_schedule(comm, n_chunks))


def _pool_forward(ext, pv, pos, wp_ref, bp_ref):
    cur = ext
    zs, zls = [], []
    for gi, w in enumerate(POOL_WINDOWS):
        cur = cur + pltpu.roll(cur, w // 2, 0)
        sw = cur[POOL_HALO:, 0:128]
        z = sw / jnp.minimum(pos, float(w)) - pv[:, gi * 128:(gi + 1) * 128]
        zs.append(z)
        zls.append(_dot(z.astype(BF16), wp_ref[gi]) + bp_ref[:, gi * 128:(gi + 1) * 128])
        if gi + 1 < len(POOL_WINDOWS):
            cur = cur[:, 128:]
    return zs, zls


def _mixer_out_fwd(y2, p, x, gt1, w_glu_b, b_glu, w_pool_b, b_pool, pscale, w_out_b):
    bsz, seq, _ = x.shape
    tt = min(seq, TT_MIX)

    def body(y_ref, p_ref, x_ref, gt_ref, wg_ref, bg_ref, wp_ref, bp_ref, ps_ref, wo_ref, x1_ref, mix_ref, mxd_ref,
             ext):
        ti = pl.program_id(1)

        @pl.when(ti == 0)
        def _():
            ext[0:POOL_HALO, :] = jnp.zeros((POOL_HALO, POOL_W), F32)

        pv = p_ref[0]
        ext[POOL_HALO:, :] = pv
        pos = (ti * tt + lax.broadcasted_iota(jnp.int32, (tt, 1), 0) + 1).astype(F32)
        _, zls = _pool_forward(ext[...], pv, pos, wp_ref, bp_ref)
        ext[0:POOL_HALO, :] = pv[tt - POOL_HALO:, :]
        a = _gelu(y_ref[...].astype(F32))
        gl = _dot(a.astype(BF16), wg_ref[...]) + bg_ref[...]
        y_ssm = gl[:, :SSM_W] * jax.nn.sigmoid(gl[:, SSM_W:])
        y_pool = [zl * ps_ref[:, gi * 128:(gi + 1) * 128] for gi, zl in enumerate(zls)]
        mixcat = jnp.concatenate([y_ssm] + y_pool, axis=1).astype(BF16)
        mix_ref[0] = mixcat
        mixed = _dot(mixcat, wo_ref[...])
        mxd_ref[0] = mixed.astype(BF16)
        x1_ref[0] = x_ref[0] + gt_ref[0] * mixed

    xt = pl.BlockSpec((1, tt, D), lambda b, t: (b, t, 0))
    return pl.pallas_call(
        body, name="mixer_out_fwd", grid=(bsz, seq // tt),
        in_specs=[pl.BlockSpec((tt, SSM_W), lambda b, t: (t, b)),
                  pl.BlockSpec((1, tt, POOL_W), lambda b, t: (b, t, 0)), xt,
                  pl.BlockSpec((1, 1, D), lambda b, t: (b, 0, 0)), VMEM, VMEM, VMEM, VMEM, VMEM, VMEM],
        out_specs=[xt, xt, xt],
        out_shape=[jax.ShapeDtypeStruct(x.shape, F32), jax.ShapeDtypeStruct(x.shape, BF16),
                   jax.ShapeDtypeStruct(x.shape, BF16)],
        scratch_shapes=[pltpu.VMEM((POOL_HALO + tt, POOL_W), F32)],
        compiler_params=_params(2),
    )(y2, p, x, gt1, w_glu_b, b_glu, w_pool_b, b_pool, pscale, w_out_b)


def _conv_gate(g, ge, wc, bc):
    g1 = pltpu.roll(ge, 1, 0)[CONV_HALO:]
    g2 = pltpu.roll(ge, 2, 0)[CONV_HALO:]
    return wc[2:3] * g + wc[1:2] * g1 + wc[0:1] * g2 + bc, g1, g2


def _ffn_fwd(x1, tgt, sh2, sc2, gt2, g_ffn, w_up_b, w_conv, b_conv, w_down_b, g_fin):
    bsz, seq, _ = x1.shape
    tt = min(seq, TT_FFN)
    n_t = seq // tt
    n_ck = DFF // FF_CH

    def body(x1_ref, tg_ref, sh_ref, sc_ref, gt_ref, gf_ref, wu_ref, wc_ref, bc_ref, wd_ref, gfin_ref,
             h2_ref, v_ref, g_ref, gc_ref, act_ref, ddn_ref, dx2_ref, loss_ref, dgfin_ref, dgt_ref, gext, lacc):
        b = pl.program_id(0)
        ti = pl.program_id(1)

        @pl.when((b == 0) & (ti == 0))
        def _():
            lacc[...] = jnp.zeros_like(lacc)
            dgfin_ref[...] = jnp.zeros_like(dgfin_ref)

        @pl.when(ti == 0)
        def _():
            dgt_ref[...] = jnp.zeros_like(dgt_ref)
            gext[:, 0:CONV_HALO, :] = jnp.zeros((n_ck, CONV_HALO, FF_CH), F32)

        x1v = x1_ref[0]
        xhat, _ = _rms(x1v)
        h2b = (xhat * gf_ref[...] * (1.0 + sc_ref[0]) + sh_ref[0]).astype(BF16)
        h2_ref[0] = h2b
        dn = jnp.zeros((tt, D), F32)
        for ck in range(n_ck):
            c0 = ck * FF_CH
            v = _dot_nt(h2b, wu_ref[c0:c0 + FF_CH, :])
            g = _dot_nt(h2b, wu_ref[DFF + c0:DFF + c0 + FF_CH, :])
            v_ref[0, :, c0:c0 + FF_CH] = v.astype(BF16)
            g_ref[0, :, c0:c0 + FF_CH] = g.astype(BF16)
            gext[ck, CONV_HALO:, :] = g
            gc, _, _ = _conv_gate(g, gext[ck], wc_ref[:, c0:c0 + FF_CH], bc_ref[:, c0:c0 + FF_CH])
            gext[ck, 0:CONV_HALO, :] = g[tt - CONV_HALO:, :]
            gc_ref[0, :, c0:c0 + FF_CH] = gc.astype(BF16)
            actb = (gc * jax.nn.sigmoid(gc) * v).astype(BF16)
            act_ref[0, :, c0:c0 + FF_CH] = actb
            dn = dn + _dot(actb, wd_ref[c0:c0 + FF_CH, :])
        gt = gt_ref[0]
        xh3, r3 = _rms(x1v + gt * dn)
        gfin = gfin_ref[...]
        diff = xh3 * gfin - tg_ref[0]
        lacc[...] += _colsum(diff * diff)
        dy = diff * (1.0 / D)
        dgfin_ref[...] += _colsum(dy * xh3)
        dx2 = _rms_bwd(dy * gfin, xh3, r3)
        dx2_ref[0] = dx2
        dgt_ref[0] += _colsum(dx2 * dn)
        ddn_ref[0] = (gt * dx2).astype(BF16)

        @pl.when((b == bsz - 1) & (ti == n_t - 1))
        def _():
            loss_ref[...] = jnp.full(loss_ref.shape, 0.5 / D * jnp.sum(lacc[...]), F32)

    xt = pl.BlockSpec((1, tt, D), lambda b, t: (b, t, 0))
    ft = pl.BlockSpec((1, tt, DFF), lambda b, t: (b, t, 0))
    row = pl.BlockSpec((1, 1, D), lambda b, t: (b, 0, 0))
    vec = pl.BlockSpec((1, D), lambda b, t: (0, 0))
    ff = jax.ShapeDtypeStruct((bsz, seq, DFF), BF16)
    xs = jax.ShapeDtypeStruct((bsz, seq, D), BF16)
    return pl.pallas_call(
        body, name="ffn_fwd", grid=(bsz, n_t),
        in_specs=[xt, xt, row, row, row, vec, VMEM, VMEM, VMEM, VMEM, vec],
        out_specs=[xt, ft, ft, ft, ft, xt, xt, pl.BlockSpec((1, 128), lambda b, t: (0, 0)), vec, row],
        out_shape=[xs, ff, ff, ff, ff, xs, jax.ShapeDtypeStruct((bsz, seq, D), F32),
                   jax.ShapeDtypeStruct((1, 128), F32), jax.ShapeDtypeStruct((1, D), F32),
                   jax.ShapeDtypeStruct((bsz, 1, D), F32)],
        scratch_shapes=[pltpu.VMEM((n_ck, CONV_HALO + tt, FF_CH), F32), pltpu.VMEM((1, D), F32)],
        compiler_params=_params(2),
    )(x1, tgt, sh2, sc2, gt2, g_ffn, w_up_b, w_conv, b_conv, w_down_b, g_fin)


def _ffn_bwd(ddn, gq, gcq, vq, x1, dx2, sh2, sc2, g_ffn, w_conv, w_down_b, w_up_b):
    bsz, seq, _ = x1.shape
    tt = min(seq, TT_FFN)
    n_t = seq // tt
    n_ck = DFF // FF_CH
    ext_rows = tt + CONV_HALO

    def body(ddn_ref, g_ref, gc_ref, v_ref, x1_ref, dx2_ref, sh_ref, sc_ref, gf_ref, wc_ref, wd_ref,
             wu_ref, dup_ref, dx1_ref, dsh_ref, dsc_ref, dgf_ref, dwc_ref, dbc_ref, dext):
        b = pl.program_id(0)
        i = pl.program_id(1)

        @pl.when((b == 0) & (i == 0))
        def _():
            dgf_ref[...] = jnp.zeros_like(dgf_ref)
            dwc_ref[...] = jnp.zeros_like(dwc_ref)
            dbc_ref[...] = jnp.zeros_like(dbc_ref)

        @pl.when(i == 0)
        def _():
            dsh_ref[...] = jnp.zeros_like(dsh_ref)
            dsc_ref[...] = jnp.zeros_like(dsc_ref)
            dext[:, tt:, :] = jnp.zeros((n_ck, CONV_HALO, FF_CH), F32)

        ddnv = ddn_ref[0]
        dh2 = jnp.zeros((tt, D), F32)
        for ck in range(n_ck):
            c0 = ck * FF_CH
            dact = _dot_nt(ddnv, wd_ref[c0:c0 + FF_CH, :])
            g = g_ref[0, :, c0:c0 + FF_CH].astype(F32)
            gc = gc_ref[0, :, c0:c0 + FF_CH].astype(F32)
            v = v_ref[0, :, c0:c0 + FF_CH].astype(F32)
            wc = wc_ref[:, c0:c0 + FF_CH]
            sg = jax.nn.sigmoid(gc)
            silu = gc * sg
            dv = dact * silu
            dgc = dact * v * (sg + silu * (1.0 - sg))
            dext[ck, 0:tt, :] = dgc
            de = dext[ck]
            d1 = pltpu.roll(de, ext_rows - 1, 0)[0:tt]
            d2 = pltpu.roll(de, ext_rows - 2, 0)[0:tt]
            dext[ck, tt:, :] = dgc[0:CONV_HALO, :]
            dbc_ref[:, c0:c0 + FF_CH] += _colsum(dgc)
            dwc_ref[0:1, c0:c0 + FF_CH] += _colsum(d2 * g)
            dwc_ref[1:2, c0:c0 + FF_CH] += _colsum(d1 * g)
            dwc_ref[2:3, c0:c0 + FF_CH] += _colsum(dgc * g)
            dg = wc[2:3] * dgc + wc[1:2] * d1 + wc[0:1] * d2
            dvb = dv.astype(BF16)
            dgb = dg.astype(BF16)
            dup_ref[0, :, c0:c0 + FF_CH] = dvb
            dup_ref[0, :, DFF + c0:DFF + c0 + FF_CH] = dgb
            dh2 = dh2 + _dot(dvb, wu_ref[c0:c0 + FF_CH, :]) + _dot(dgb, wu_ref[DFF + c0:DFF + c0 + FF_CH, :])
        xhat, rstd = _rms(x1_ref[0])
        gf = gf_ref[...]
        dsh_ref[0] += _colsum(dh2)
        dsc_ref[0] += _colsum(dh2 * xhat * gf)
        t = dh2 * (1.0 + sc_ref[0])
        dgf_ref[...] += _colsum(t * xhat)
        dx1_ref[0] = dx2_ref[0] + _rms_bwd(t * gf, xhat, rstd)

    def rev(b, t):
        return (b, n_t - 1 - t, 0)

    xt = pl.BlockSpec((1, tt, D), rev)
    ft = pl.BlockSpec((1, tt, DFF), rev)
    row = pl.BlockSpec((1, 1, D), lambda b, t: (b, 0, 0))
    vec = pl.BlockSpec((1, D), lambda b, t: (0, 0))
    rows = jax.ShapeDtypeStruct((bsz, 1, D), F32)
    return pl.pallas_call(
        body, name="ffn_bwd", grid=(bsz, n_t),
        in_specs=[xt, ft, ft, ft, xt, xt, row, row, vec, VMEM, VMEM, VMEM],
        out_specs=[pl.BlockSpec((1, tt, 2 * DFF), rev), xt, row, row, vec,
                   pl.BlockSpec((3, DFF), lambda b, t: (0, 0)), pl.BlockSpec((1, DFF), lambda b, t: (0, 0))],
        out_shape=[jax.ShapeDtypeStruct((bsz, seq, 2 * DFF), BF16), jax.ShapeDtypeStruct((bsz, seq, D), F32),
                   rows, rows, jax.ShapeDtypeStruct((1, D), F32), jax.ShapeDtypeStruct((3, DFF), F32),
                   jax.ShapeDtypeStruct((1, DFF), F32)],
        scratch_shapes=[pltpu.VMEM((n_ck, ext_rows, FF_CH), F32)],
        compiler_params=_params(2),
    )(ddn, gq, gcq, vq, x1, dx2, sh2, sc2, g_ffn, w_conv, w_down_b, w_up_b)


def _wgrad(a, b, bk1, bk2, name):
    n, k1 = a.shape
    _, k2 = b.shape
    tt = min(n, TT_WGRAD)

    def body(a_ref, b_ref, o_ref):
        @pl.when(pl.program_id(2) == 0)
        def _():
            o_ref[...] = jnp.zeros_like(o_ref)

        o_ref[...] += _dot_tn(a_ref[...], b_ref[...])

    return pl.pallas_call(
        body, name=name, grid=(k1 // bk1, k2 // bk2, n // tt),
        in_specs=[pl.BlockSpec((tt, bk1), lambda h, j, i: (i, h)), pl.BlockSpec((tt, bk2), lambda h, j, i: (i, j))],
        out_specs=pl.BlockSpec((bk1, bk2), lambda h, j, i: (h, j)),
        out_shape=jax.ShapeDtypeStruct((k1, k2), F32), compiler_params=_params(3),
    )(a, b)


def _mixer_out_bwd(dx1, mixcat, mixed, y2, p, gt1, w_glu_b, b_glu, w_pool_b, b_pool, pscale, w_out_b, comm=None):
    bsz, seq, _ = dx1.shape
    tt = min(seq, TT_MIX)
    n_t = seq // tt
    ext_rows = tt + POOL_HALO

    def body(dx1_ref, mc_ref, mxd_ref, y_ref, p_ref, ph_ref, gt_ref, wg_ref, bg_ref, wp_ref, bp_ref, ps_ref, wo_ref,
             dy_ref, dp_ref, dwo_ref, dwg_ref, dbg_ref, dwp_ref, dbp_ref, dps_ref, dgt_ref, ext, qext):
        b = pl.program_id(0)
        i = pl.program_id(1)
        tile = n_t - 1 - i

        @pl.when((b == 0) & (i == 0))
        def _():
            for r in (dwo_ref, dwg_ref, dbg_ref, dwp_ref, dbp_ref, dps_ref):
                r[...] = jnp.zeros_like(r)

        @pl.when(i == 0)
        def _():
            dgt_ref[...] = jnp.zeros_like(dgt_ref)
            qext[tt:, :] = jnp.zeros((POOL_HALO, POOL_W), F32)

        dx1v = dx1_ref[0]
        mc = mc_ref[0]
        dgt_ref[0] += _colsum(dx1v * mxd_ref[0].astype(F32))
        dmixed = (gt_ref[0] * dx1v).astype(BF16)
        dwo_ref[...] += _dot_tn(mc, dmixed)
        dmc = _dot_nt(dmixed, wo_ref[...])
        pv = p_ref[0]
        ext[0:POOL_HALO, :] = ph_ref[0] * (tile > 0).astype(F32)
        ext[POOL_HALO:, :] = pv
        pos = (tile * tt + lax.broadcasted_iota(jnp.int32, (tt, 1), 0) + 1).astype(F32)
        zs, zls = _pool_forward(ext[...], pv, pos, wp_ref, bp_ref)
        dzs = []
        for gi, w in enumerate(POOL_WINDOWS):
            cs = slice(gi * 128, (gi + 1) * 128)
            dyp = dmc[:, SSM_W + gi * 128:SSM_W + (gi + 1) * 128]
            dps_ref[:, cs] += _colsum(dyp * zls[gi])
            dzl = dyp * ps_ref[:, cs]
            dbp_ref[:, cs] += _colsum(dzl)
            dzlb = dzl.astype(BF16)
            dwp_ref[gi] += _dot_tn(zs[gi].astype(BF16), dzlb)
            dz = _dot_nt(dzlb, wp_ref[gi])
            dzs.append(dz)
            qext[0:tt, cs] = dz / jnp.minimum(pos, float(w))
        cur = qext[...]
        dps = []
        for gi, w in enumerate(POOL_WINDOWS):
            cur = cur + pltpu.roll(cur, ext_rows - w // 2, 0)
            dps.append(cur[0:tt, 0:128] - dzs[gi])
            if gi + 1 < len(POOL_WINDOWS):
                cur = cur[:, 128:]
        qhead = qext[0:POOL_HALO, :]
        qext[tt:, :] = qhead
        dp_ref[0] = jnp.concatenate(dps, axis=1)
        yv = y_ref[...].astype(F32)
        ab = _gelu(yv).astype(BF16)
        gl = _dot(ab, wg_ref[...]) + bg_ref[...]
        val = gl[:, :SSM_W]
        sg = jax.nn.sigmoid(gl[:, SSM_W:])
        dys = dmc[:, :SSM_W]
        dgl = jnp.concatenate([dys * sg, dys * val * sg * (1.0 - sg)], axis=1)
        dbg_ref[...] += _colsum(dgl)
        dglb = dgl.astype(BF16)
        dwg_ref[...] += _dot_tn(ab, dglb)
        dy_ref[...] = (_dot_nt(dglb, wg_ref[...]) * _gelu_grad(yv)).astype(BF16)

    def rev(b, t):
        return (b, n_t - 1 - t, 0)

    def halo(b, t):
        return (b, jnp.maximum((n_t - 1 - t) * (tt // POOL_HALO) - 1, 0), 0)

    xt = pl.BlockSpec((1, tt, D), rev)
    pt = pl.BlockSpec((1, tt, POOL_W), rev)
    yt = pl.BlockSpec((tt, SSM_W), lambda b, t: (n_t - 1 - t, b))

    def whole(shape):
        return pl.BlockSpec(shape, lambda b, t: (0,) * len(shape))

    return _fused_call(
        body, name="mixer_out_bwd", grid=(bsz, n_t),
        in_specs=[xt, xt, xt, yt, pt, pl.BlockSpec((1, POOL_HALO, POOL_W), halo),
                  pl.BlockSpec((1, 1, D), lambda b, t: (b, 0, 0)), VMEM, VMEM, VMEM, VMEM, VMEM, VMEM],
        out_specs=[yt, pt, whole((D, D)), whole((SSM_W, 2 * SSM_W)), whole((1, 2 * SSM_W)),
                   whole((4, 128, 128)), whole((1, POOL_W)), whole((1, POOL_W)),
                   pl.BlockSpec((1, 1, D), lambda b, t: (b, 0, 0))],
        out_shape=[jax.ShapeDtypeStruct(y2.shape, BF16), jax.ShapeDtypeStruct(p.shape, F32),
                   jax.ShapeDtypeStruct((D, D), F32), jax.ShapeDtypeStruct((SSM_W, 2 * SSM_W), F32),
                   jax.ShapeDtypeStruct((1, 2 * SSM_W), F32), jax.ShapeDtypeStruct((4, 128, 128), F32),
                   jax.ShapeDtypeStruct((1, POOL_W), F32), jax.ShapeDtypeStruct((1, POOL_W), F32),
                   jax.ShapeDtypeStruct((bsz, 1, D), F32)],
        scratch_shapes=[pltpu.VMEM((POOL_HALO + tt, POOL_W), F32), pltpu.VMEM((ext_rows, POOL_W), F32)],
        args=(dx1, mixcat, mixed, y2, p, p, gt1, w_glu_b, b_glu, w_pool_b, b_pool, pscale, w_out_b),
        comm=_schedule(comm, bsz * n_t))


def _ssm_bwd(dy2r, u2r, xc, xs_re, xs_im, bb, cc, lam8, d8, tlen, comm=None):
    nv = lam8.shape[1]
    rows = nv * tlen
    n_chunks = u2r.shape[0] // rows

    def body(dy_ref, u_ref, xc_ref, xre_ref, xim_ref, bb_ref, cc_ref, lam_ref, d_ref,
             du_ref, dcc_ref, dbb_ref, dlam_ref, dd_ref, s_re, s_im, g_re, g_im, gst):
        i = pl.program_id(0)

        @pl.when(i == 0)
        def _():
            for r in (gst, dcc_ref, dbb_ref, dlam_ref, dd_ref):
                r[...] = jnp.zeros_like(r)

        u = u_ref[...].astype(F32)
        dy = dy_ref[...].astype(F32)
        par0 = (lax.broadcasted_iota(jnp.int32, (rows, 1), 0) % 2) == 0
        xre = xre_ref[...]
        xim = xim_ref[...]
        s_re[0] = xc_ref[0, 0]
        s_im[0] = xc_ref[0, 1]
        s_re[pl.ds(1, tlen)] = xre.astype(F32).reshape(tlen, nv, HALF_ST)
        s_im[pl.ds(1, tlen)] = xim.astype(F32).reshape(tlen, nv, HALF_ST)
        zero = jnp.zeros_like(dy)
        dy2 = jnp.concatenate([jnp.where(par0, dy, zero), jnp.where(par0, zero, dy)], axis=1).astype(BF16)
        u2 = jnp.concatenate([jnp.where(par0, u, zero), jnp.where(par0, zero, u)], axis=1).astype(BF16)
        dcc_ref[0:HALF_ST, :] += _dot_tn(xre, dy2)
        dcc_ref[HALF_ST:, :] += _dot_tn(xim, dy2)
        for part, gref in ((0, g_re), (1, g_im)):
            for k in range(HALF_ST // 512):
                r0 = part * HALF_ST + k * 512
                gref[:, :, k * 512:(k + 1) * 512] = _dot_nt(dy2, cc_ref[r0:r0 + 512, :]).reshape(tlen, nv, 512)
        for hb in range(HALF_ST // 512):
            ls = slice(hb * 512, (hb + 1) * 512)
            lr = lam_ref[0, :, ls]
            li = lam_ref[1, :, ls]

            def bstep(k, carry, ls=ls, lr=lr, li=li):
                t = tlen - 1 - k
                gr, gi, ar, ai = carry
                ngr = g_re[t, :, ls] + lr * gr + li * gi
                ngi = g_im[t, :, ls] + lr * gi - li * gr
                g_re[t, :, ls] = ngr
                g_im[t, :, ls] = ngi
                xpr = s_re[t, :, ls]
                xpi = s_im[t, :, ls]
                return ngr, ngi, ar + ngr * xpr + ngi * xpi, ai + ngi * xpr - ngr * xpi

            init = (gst[0, :, ls], gst[1, :, ls], dlam_ref[0, :, ls], dlam_ref[1, :, ls])
            gr, gi, ar, ai = lax.fori_loop(0, tlen, bstep, init, unroll=4)
            gst[0, :, ls] = gr
            gst[1, :, ls] = gi
            dlam_ref[0, :, ls] = ar
            dlam_ref[1, :, ls] = ai
        gre = g_re[...].reshape(rows, HALF_ST).astype(BF16)
        gim = g_im[...].reshape(rows, HALF_ST).astype(BF16)
        du0 = _dot_nt(gre, bb_ref[:, 0:HALF_ST]) + _dot_nt(gim, bb_ref[:, HALF_ST:2 * HALF_ST])
        du1 = _dot_nt(gre, bb_ref[:, 2 * HALF_ST:3 * HALF_ST]) + _dot_nt(gim, bb_ref[:, 3 * HALF_ST:])
        skip = (dy.reshape(tlen, nv, HALF_CH) * d_ref[...][None]).reshape(rows, HALF_CH)
        du_ref[...] = (jnp.where(par0, du0, du1) + skip).astype(BF16)
        dbb_ref[:, 0:HALF_ST] += _dot_tn(u2, gre)
        dbb_ref[:, HALF_ST:] += _dot_tn(u2, gim)
        dd_ref[...] += jnp.sum((dy * u).reshape(tlen, nv, HALF_CH), axis=0)

        @pl.when(i == n_chunks - 1)
        def _():
            dcc_ref[HALF_ST:, :] = -dcc_ref[HALF_ST:, :]

    def rev(c):
        return (n_chunks - 1 - c, 0)

    def whole(shape):
        return pl.BlockSpec(shape, lambda c: (0,) * len(shape))

    blk = pl.BlockSpec((rows, HALF_CH), rev)
    st_blk = pl.BlockSpec((rows, HALF_ST), rev)
    return _fused_call(
        body, name="ssm_bwd", grid=(n_chunks,),
        in_specs=[blk, blk, pl.BlockSpec((1, 2, nv, HALF_ST), lambda c: (n_chunks - 1 - c, 0, 0, 0)),
                  st_blk, st_blk, VMEM, VMEM, VMEM, VMEM],
        out_specs=[blk, whole((2 * HALF_ST, SSM_W)), whole((SSM_W, 2 * HALF_ST)), whole((2, nv, HALF_ST)),
                   whole((nv, HALF_CH))],
        out_shape=[jax.ShapeDtypeStruct(u2r.shape, BF16), jax.ShapeDtypeStruct((2 * HALF_ST, SSM_W), F32),
                   jax.ShapeDtypeStruct((SSM_W, 2 * HALF_ST), F32), jax.ShapeDtypeStruct((2, nv, HALF_ST), F32),
                   jax.ShapeDtypeStruct((nv, HALF_CH), F32)],
        scratch_shapes=[pltpu.VMEM((tlen + 1, nv, HALF_ST), F32), pltpu.VMEM((tlen + 1, nv, HALF_ST), F32),
                        pltpu.VMEM((tlen, nv, HALF_ST), F32), pltpu.VMEM((tlen, nv, HALF_ST), F32),
                        pltpu.VMEM((2, nv, HALF_ST), F32)],
        args=(dy2r, u2r, xc, xs_re, xs_im, bb, cc, lam8, d8), comm=_schedule(comm, n_chunks))


def _mixer_in_bwd(du2, dp, x, dx1, sh1, sc1, g_mix, w_in_b, comm=None):
    bsz, seq, _ = x.shape
    tt = min(seq, TT_MIX)

    def body(du_ref, dp_ref, x_ref, dx1_ref, sh_ref, sc_ref, g_ref, w_ref,
             dx_ref, dw_ref, dsh_ref, dsc_ref, dg_ref):
        b = pl.program_id(0)
        ti = pl.program_id(1)

        @pl.when((b == 0) & (ti == 0))
        def _():
            dw_ref[...] = jnp.zeros_like(dw_ref)
            dg_ref[...] = jnp.zeros_like(dg_ref)

        @pl.when(ti == 0)
        def _():
            dsh_ref[...] = jnp.zeros_like(dsh_ref)
            dsc_ref[...] = jnp.zeros_like(dsc_ref)

        dz = jnp.concatenate([du_ref[...], dp_ref[0].astype(BF16)], axis=1)
        xhat, rstd = _rms(x_ref[0])
        g = g_ref[...]
        sc = sc_ref[0]
        a = xhat * g
        h = (a * (1.0 + sc) + sh_ref[0]).astype(BF16)
        dw_ref[...] += _dot_tn(h, dz)
        dh = _dot_nt(dz, w_ref[...])
        dsh_ref[0] += _colsum(dh)
        dsc_ref[0] += _colsum(dh * a)
        t = dh * (1.0 + sc)
        dg_ref[...] += _colsum(t * xhat)
        dx_ref[0] = dx1_ref[0] + _rms_bwd(t * g, xhat, rstd)

    xt = pl.BlockSpec((1, tt, D), lambda b, t: (b, t, 0))
    row = pl.BlockSpec((1, 1, D), lambda b, t: (b, 0, 0))
    vec = pl.BlockSpec((1, D), lambda b, t: (0, 0))
    rows = jax.ShapeDtypeStruct((bsz, 1, D), F32)
    return _fused_call(
        body, name="mixer_in_bwd", grid=(bsz, seq // tt),
        in_specs=[pl.BlockSpec((tt, SSM_W), lambda b, t: (t, b)),
                  pl.BlockSpec((1, tt, POOL_W), lambda b, t: (b, t, 0)), xt, xt, row, row, vec, VMEM],
        out_specs=[xt, pl.BlockSpec((D, D), lambda b, t: (0, 0)), row, row, vec],
        out_shape=[jax.ShapeDtypeStruct(x.shape, F32), jax.ShapeDtypeStruct((D, D), F32), rows, rows,
                   jax.ShapeDtypeStruct((1, D), F32)],
        scratch_shapes=[], args=(du2, dp, x, dx1, sh1, sc1, g_mix, w_in_b),
        comm=_schedule(comm, bsz * (seq // tt)))


def kernel(x, c, w_ada, b_ada, g_norm_mix, w_in, ssm_lam_re, ssm_lam_im, ssm_log_dt, ssm_b_re, ssm_b_im, ssm_c_re, ssm_c_im, ssm_d, w_glu, b_glu, w_pool, b_pool, pool_scale, w_out, g_norm_ffn, w_up, w_conv, b_conv, w_down, g_norm_final, loss_target, m_w_ada, m_b_ada, m_g_norm_mix, m_w_in, m_ssm_lam_re, m_ssm_lam_im, m_ssm_log_dt, m_ssm_b_re, m_ssm_b_im, m_ssm_c_re, m_ssm_c_im, m_ssm_d, m_w_glu, m_b_glu, m_w_pool, m_b_pool, m_pool_scale, m_w_out, m_g_norm_ffn, m_w_up, m_w_conv, m_b_conv, m_w_down, m_g_norm_final, v_w_ada, v_b_ada, v_g_norm_mix, v_w_in, v_ssm_lam_re, v_ssm_lam_im, v_ssm_log_dt, v_ssm_b_re, v_ssm_b_im, v_ssm_c_re, v_ssm_c_im, v_ssm_d, v_w_glu, v_b_glu, v_w_pool, v_b_pool, v_pool_scale, v_w_out, v_g_norm_ffn, v_w_up, v_w_conv, v_b_conv, v_w_down, v_g_norm_final):
    bsz, seq, _ = x.shape
    assert 2 * bsz == 8 and seq % 128 == 0
    px, py, pc = _my_place()
    me = 4 * px + 2 * py + pc
    place = jnp.stack([pc, 2 * px + py]).astype(jnp.int32)
    ncol = ADA_COLS

    cpad = jnp.zeros((16, D), F32).at[0:bsz].set(c).at[8:11, 0:352].set(w_conv[0])
    cg, c_all, mod8, (g_in,) = _ada_fwd(cpad, w_ada[0], b_ada.reshape(N_DEV, 1, ncol), [w_in[0].astype(BF16)])
    w_conv_f = cg[:, 8:11, 0:352].transpose(1, 0, 2).reshape(3, DFF)
    w_in_b = g_in.reshape(D, D)
    sh1, sc1, gt1, sh2, sc2, gt2 = [mod8[0:bsz, k * D:(k + 1) * D].reshape(bsz, 1, D) for k in range(N_MOD)]

    lam_r = ssm_lam_re[0].reshape(1, GRP * NST)
    lam_i = ssm_lam_im[0].reshape(1, GRP * NST)
    ldt = jnp.repeat(ssm_log_dt[0], NST).reshape(1, GRP * NST)
    b_r = ssm_b_re[0].transpose(2, 0, 1).reshape(GCH, GRP * NST)
    b_i = ssm_b_im[0].transpose(2, 0, 1).reshape(GCH, GRP * NST)
    lbr, lbi, bbr, bbi = _ssm_prep(lam_r, lam_i, ldt, b_r, b_i)
    lam8 = jnp.stack([jnp.tile(lbr.reshape(2, HALF_ST), (bsz, 1)), jnp.tile(lbi.reshape(2, HALF_ST), (bsz, 1))])
    bd_r = _blockdiag(bbr.reshape(GCH, 2, GRP // 2, NST).transpose(1, 2, 0, 3))
    bd_i = _blockdiag(bbi.reshape(GCH, 2, GRP // 2, NST).transpose(1, 2, 0, 3))
    bb = jnp.concatenate([bd_r[0], bd_i[0], bd_r[1], bd_i[1]], axis=1).astype(BF16)
    cd_r = _blockdiag(ssm_c_re[0].reshape(2, GRP // 2, GCH, NST).transpose(0, 1, 3, 2))
    cd_i = _blockdiag(ssm_c_im[0].reshape(2, GRP // 2, GCH, NST).transpose(0, 1, 3, 2))
    cc = jnp.concatenate([jnp.concatenate([cd_r[0], cd_r[1]], axis=1),
                          jnp.concatenate([-cd_i[0], -cd_i[1]], axis=1)], axis=0).astype(BF16)
    d8 = jnp.tile(ssm_d[0].reshape(2, HALF_CH), (bsz, 1))

    tlen = min(seq, T_SSM)
    (u2, p), ((g_glu, g_out),) = _mixer_in_fwd(
        x, sh1, sc1, g_norm_mix, w_in_b, comm=[(_gather_plan, [w_glu[0].astype(BF16), w_out[0].astype(BF16)])])
    w_glu_b = g_glu.transpose(1, 0, 2).reshape(SSM_W, 2 * SSM_W)
    w_out_b = g_out.reshape(D, D)
    u2r = u2.reshape(seq * 2 * bsz, HALF_CH)
    (y2r, xc, xs_re, xs_im), ((g_up, g_down),) = _ssm_fwd(
        u2r, bb, cc, lam8, d8, tlen, comm=[(_gather_plan, [w_up[0].T.astype(BF16), w_down[0].astype(BF16)])])
    w_up_b = g_up.reshape(2 * DFF, D)
    w_down_b = g_down.reshape(DFF, D)
    y2 = y2r.reshape(seq, bsz * SSM_W)
    w_pool_b = w_pool[0].astype(BF16)
    bp = b_pool[0].reshape(1, POOL_W)
    x1, mixcat, mixed = _mixer_out_fwd(y2, p, x, gt1, w_glu_b, b_glu, w_pool_b, bp, pool_scale, w_out_b)
    h2, vq, gq, gcq, act, ddn, dx2, loss_l, dg_fin, dgt2 = _ffn_fwd(
        x1, loss_target, sh2, sc2, gt2, g_norm_ffn, w_up_b, w_conv_f, b_conv, w_down_b, g_norm_final.reshape(1, D))

    dup, dx1, dsh2, dsc2, dg_ffn, dw_conv, db_conv = _ffn_bwd(
        ddn, gq, gcq, vq, x1, dx2, sh2, sc2, g_norm_ffn, w_conv_f, w_down_b, w_up_b)
    ntok = bsz * seq
    dw_up_t = _wgrad(dup.reshape(ntok, 2 * DFF), h2.reshape(ntok, D), DFF // 2, D, "wgrad_up")
    dw_down = _wgrad(act.reshape(ntok, DFF), ddn.reshape(ntok, D), DFF, 512, "wgrad_down")
    g42_up = dw_up_t.reshape(4, 2, 704, D)
    g42_down = dw_down.reshape(4, 2, 352, D)
    (dy2, dp, dw_out, dw_glu, db_glu, dw_pool, db_pool, dpscale, dgt1), ((ra_up, ra_down),) = _mixer_out_bwd(
        dx1, mixcat, mixed, y2, p, gt1, w_glu_b, b_glu, w_pool_b, bp, pool_scale, w_out_b,
        comm=[(_pair_plan, [g42_up, g42_down])])
    own_up, s_up = _pair_sum(g42_up, ra_up, place, "pair_sum_up")
    own_down, s_down = _pair_sum(g42_down, ra_down, place, "pair_sum_down")
    g42_glu = dw_glu.reshape(SSM_W, N_DEV, 128).transpose(1, 0, 2).reshape(4, 2, SSM_W, 128)
    g42_out = dw_out.reshape(4, 2, 128, D)
    small_a = [
        ("b_glu", (1, 2 * SSM_W), db_glu), ("w_pool", (POOL_W, 128), dw_pool.reshape(POOL_W, 128)),
        ("b_pool", (4, 128), db_pool.reshape(4, 128)), ("pool_scale", (1, POOL_W), dpscale),
        ("g_norm_ffn", (1, D), dg_ffn), ("b_conv", (1, DFF), db_conv), ("g_norm_final", (1, D), dg_fin)]
    (du2r, dcc, dbb, dlam8, dd8), ((rc_up, rc_down), (ra_glu, ra_out), parts_a) = _ssm_bwd(
        dy2.reshape(u2r.shape), u2r, xc, xs_re, xs_im, bb, cc, lam8, d8, tlen,
        comm=[(_chip_plan, [s_up, s_down]), (_pair_plan, [g42_glu, g42_out]),
              (_gather_plan, [g for _, _, g in small_a] + [dw_conv, loss_l], "small")])
    big_up = [t.T for t in _final_sum_adamw(own_up, rc_up, w_up[0].T, m_w_up[0].T, v_w_up[0].T, "final_adamw_up")]
    big_down = _final_sum_adamw(own_down, rc_down, w_down[0], m_w_down[0], v_w_down[0], "final_adamw_down")
    own_glu, s_glu = _pair_sum(g42_glu, ra_glu, place, "pair_sum_glu")
    own_out, s_out = _pair_sum(g42_out, ra_out, place, "pair_sum_out")

    def take_c(t):
        return _blockdiag_take(t, NST, GCH).transpose(0, 2, 1)

    dc_re = jnp.concatenate([take_c(dcc[0:HALF_ST, e * HALF_CH:(e + 1) * HALF_CH]) for e in range(2)], axis=0)
    dc_im = jnp.concatenate([take_c(dcc[HALF_ST:, e * HALF_CH:(e + 1) * HALF_CH]) for e in range(2)], axis=0)

    def take_b(t):
        return _blockdiag_take(t, GCH, NST).transpose(1, 0, 2)

    dbbr = jnp.concatenate([take_b(dbb[e * HALF_CH:(e + 1) * HALF_CH, 0:HALF_ST]) for e in range(2)], axis=1)
    dbbi = jnp.concatenate([take_b(dbb[e * HALF_CH:(e + 1) * HALF_CH, HALF_ST:]) for e in range(2)], axis=1)
    glr, gli, gldt, gbr, gbi, gd = _ssm_param_bwd(
        lam_r, lam_i, ldt, b_r, b_i, dlam8, dbbr.reshape(GCH, GRP * NST), dbbi.reshape(GCH, GRP * NST), dd8)
    g_log_dt = jnp.sum(gldt.reshape(GRP, NST), axis=1)

    def view(a, shp):
        return a.reshape(shp)

    small_b = [
        ("ssm_lam_re", (GRP, NST), glr.reshape(GRP, NST)), ("ssm_lam_im", (GRP, NST), gli.reshape(GRP, NST)),
        ("ssm_log_dt", (1, GRP), g_log_dt.reshape(1, GRP)),
        ("ssm_c_re", (GRP * GCH, NST), dc_re.reshape(GRP * GCH, NST)),
        ("ssm_c_im", (GRP * GCH, NST), dc_im.reshape(GRP * GCH, NST)), ("ssm_d", (1, SSM_W), gd)]
    small = small_a + small_b
    given = dict(
        ssm_lam_re=(ssm_lam_re, m_ssm_lam_re, v_ssm_lam_re), ssm_lam_im=(ssm_lam_im, m_ssm_lam_im, v_ssm_lam_im),
        ssm_log_dt=(ssm_log_dt, m_ssm_log_dt, v_ssm_log_dt), ssm_c_re=(ssm_c_re, m_ssm_c_re, v_ssm_c_re),
        ssm_c_im=(ssm_c_im, m_ssm_c_im, v_ssm_c_im), ssm_d=(ssm_d, m_ssm_d, v_ssm_d), b_glu=(b_glu, m_b_glu, v_b_glu),
        w_pool=(w_pool, m_w_pool, v_w_pool), b_pool=(b_pool, m_b_pool, v_b_pool),
        pool_scale=(pool_scale, m_pool_scale, v_pool_scale), g_norm_ffn=(g_norm_ffn, m_g_norm_ffn, v_g_norm_ffn),
        b_conv=(b_conv, m_b_conv, v_b_conv), g_norm_final=(g_norm_final, m_g_norm_final, v_g_norm_final),
        ssm_b_re=(ssm_b_re, m_ssm_b_re, v_ssm_b_re), ssm_b_im=(ssm_b_im, m_ssm_b_im, v_ssm_b_im))
    b_view = (GRP * NST, GCH)
    (grad_x, dw_in, dsh1, dsc1, dg_mix), ((rc_glu, rc_out), parts_b) = _mixer_in_bwd(
        du2r.reshape(u2.shape), dp, x, dx1, sh1, sc1, g_norm_mix, w_in_b,
        comm=[(_chip_plan, [s_glu, s_out]), (_gather_plan, [g for _, _, g in small_b] + [gbr, gbi], "small")])
    big_glu = _final_sum_adamw(own_glu, rc_glu, w_glu[0], m_w_glu[0], v_w_glu[0], "final_adamw_glu")
    big_out = _final_sum_adamw(own_out, rc_out, w_out[0], m_w_out[0], v_w_out[0], "final_adamw_out")
    parts = list(parts_a[:-2]) + list(parts_b[:-2])
    items = [(pt,) + tuple(view(a, shp) for a in given[nm]) for pt, (nm, shp, _) in zip(parts, small)]
    small_out, (g_conv_full, loss_all, gbr_all, gbi_all) = _small_sum_adamw(
        items, [parts_a[-2], parts_a[-1], parts_b[-2], parts_b[-1]])
    loss = loss_all[0, 0]
    result = {nm: [t.reshape(given[nm][0].shape) for t in quad] for quad, (nm, _, _) in zip(small_out, small)}
    for nm, g_all in (("ssm_b_re", gbr_all), ("ssm_b_im", gbi_all)):
        quad = [g_all.T] + list(_adamw_plain(g_all.T, *[view(a, b_view) for a in given[nm]]))
        result[nm] = [t.reshape(given[nm][0].shape) for t in quad]
    g_w_conv = lax.dynamic_slice_in_dim(g_conv_full, 352 * me, 352, axis=1)
    result["w_conv"] = [g_w_conv[None]] + [t[None] for t in _adamw_plain(g_w_conv, w_conv[0], m_w_conv[0], v_w_conv[0])]

    for nm, quad in (("w_glu", big_glu), ("w_out", big_out), ("w_up", big_up), ("w_down", big_down)):
        result[nm] = [t[None] for t in quad]

    dmod = jnp.concatenate([t.reshape(bsz, D) for t in (dsh1, dsc1, dgt1, dsh2, dsc2, dgt2)], axis=1)
    dmod_blk = jnp.zeros((N_DEV, 8, ncol), F32).at[:, 0:bsz].set(dmod.reshape(bsz, N_DEV, ncol).transpose(1, 0, 2))
    dmod_blk = dmod_blk.at[0, ADA_RIDER_ROW].set(dg_mix[0, 0:ncol]).at[1, ADA_RIDER_ROW, 0:D - ncol].set(dg_mix[0, ncol:])
    ada = _ada_bwd(dmod_blk.reshape(ADA_ROWS, ncol), c_all, w_ada[0], m_w_ada[0], v_w_ada[0],
                   b_ada, m_b_ada, v_b_ada, g_norm_mix, m_g_norm_mix, v_g_norm_mix,
                   dw_in.reshape(4, 2, 128, D), w_in[0], m_w_in[0], v_w_in[0])
    result["w_ada"] = [t[None] for t in ada[0:4]]
    result["b_ada"] = list(ada[4:8])
    result["g_norm_mix"] = list(ada[8:12])
    result["w_in"] = [t[None] for t in ada[12:16]]

    names = ["w_ada", "b_ada", "g_norm_mix", "w_in", "ssm_lam_re", "ssm_lam_im", "ssm_log_dt", "ssm_b_re", "ssm_b_im",
             "ssm_c_re", "ssm_c_im", "ssm_d", "w_glu", "b_glu", "w_pool", "b_pool", "pool_scale", "w_out", "g_norm_ffn",
             "w_up", "w_conv", "b_conv", "w_down", "g_norm_final"]
    return (loss, grad_x, *[result[nm][k] for k in range(4) for nm in names])
```
